```python
import math
import jax, jax.numpy as jnp
from jax import lax
import numpy as np

D_MODEL = 1024
BATCH = 8
SEQ = 16384
DEPTH = 1

GRID_W = 64
CTX_LEN = 256
SSD_HEAD_DIM = 64
D_SSD = D_MODEL
SSD_HEADS = D_SSD // SSD_HEAD_DIM
D_STATE = 128
D_CONV = 5
CHUNK = 128
D_POOL = D_MODEL
POOL_WINDOWS = (2, 4, 8, 16)
N_POOL_GROUPS = len(POOL_WINDOWS)
POOL_GROUP_DIM = D_POOL // N_POOL_GROUPS
D_MIX = D_SSD + D_POOL
D_XBC = D_SSD + 2 * D_STATE
D_IN_PROJ = D_SSD + D_XBC + 2 * SSD_HEADS + D_POOL
D_FF = ((8 * D_MODEL // 3 + 255) // 256) * 256
DEEPNORM_ALPHA = (2 * DEPTH) ** 0.25
DEEPNORM_BETA = (8 * DEPTH) ** -0.25
LN_EPS = 1e-5

kernel_name = 'hybrid_ssd_pool_deepnorm_dit_block'


def layer_norm(x, g, b):
    xf = x.astype(jnp.float32)
    mu = jnp.mean(xf, axis=-1, keepdims=True)
    var = jnp.mean(jnp.square(xf - mu), axis=-1, keepdims=True)
    return ((xf - mu) * lax.rsqrt(var + LN_EPS) * g.astype(jnp.float32) + b.astype(jnp.float32)).astype(x.dtype)


def rms_norm(x, g):
    xf = x.astype(jnp.float32)
    return (xf * lax.rsqrt(jnp.mean(xf * xf, axis=-1, keepdims=True) + LN_EPS) * g.astype(jnp.float32)).astype(x.dtype)


def modulate(x, shift, scale):
    return x * (1 + scale) + shift


def split_projection(h, w_in):
    proj = h @ w_in
    o1 = D_SSD
    o2 = o1 + D_XBC
    o3 = o2 + 2 * SSD_HEADS
    return proj[..., :o1], proj[..., o1:o2], proj[..., o2:o3], proj[..., o3:]


def depthwise_conv_centred(u, w, b):
    out = lax.conv_general_dilated(
        u, w[:, None, :].astype(u.dtype), window_strides=(1,),
        padding=[(D_CONV // 2, D_CONV // 2)],
        dimension_numbers=('NWC', 'WIO', 'NWC'),
        feature_group_count=u.shape[-1])
    return out + b


def ssd_chunked_scan(x, dt, a_neg, B, C, h0):
    f32 = jnp.float32
    b, L, H, P = x.shape
    N = B.shape[-1]
    nc = L // CHUNK
    xc = x.astype(f32).reshape(b, nc, CHUNK, H, P)
    dtc = dt.astype(f32).reshape(b, nc, CHUNK, H)
    Bc = B.astype(f32).reshape(b, nc, CHUNK, N)
    Cc = C.astype(f32).reshape(b, nc, CHUNK, N)
    a_cum = jnp.cumsum(dtc * a_neg.astype(f32), axis=2)
    a_cum_h = jnp.moveaxis(a_cum, -1, 2)
    seg = a_cum_h[..., :, None] - a_cum_h[..., None, :]
    lower = jnp.tril(jnp.ones((CHUNK, CHUNK), dtype=bool))
    decay = jnp.exp(jnp.where(lower, seg, -jnp.inf))
    cb = jnp.einsum('bcin,bcjn->bcij', Cc, Bc)
    scores = cb[:, :, None] * decay * jnp.moveaxis(dtc, -1, 2)[..., None, :]
    y_diag = jnp.einsum('bchij,bcjhp->bcihp', scores, xc)
    w_end = jnp.exp(a_cum[:, :, -1:, :] - a_cum) * dtc
    states = jnp.einsum('bcjn,bcjhp->bchpn', Bc, xc * w_end[..., None])
    chunk_decay = jnp.exp(a_cum[:, :, -1, :])

    def step(h, inp):
        dec, st = inp
        return h * dec[:, :, None, None] + st, h

    h_final, h_prev = lax.scan(step, h0.astype(f32),
                               (jnp.moveaxis(chunk_decay, 1, 0), jnp.moveaxis(states, 1, 0)))
    h_prev = jnp.moveaxis(h_prev, 0, 1)
    y_off = jnp.einsum('bcin,bchpn->bcihp', Cc, h_prev) * jnp.exp(a_cum)[..., None]
    y = (y_diag + y_off).reshape(b, L, H, P)
    return y.astype(x.dtype), h_final


def ssd_bidirectional(xbc, dt_raw, dt_bias, a_log, h0_fwd, h0_bwd):
    b, L, _ = xbc.shape
    xs = xbc[..., :D_SSD].reshape(b, L, SSD_HEADS, SSD_HEAD_DIM)
    Bm = xbc[..., D_SSD:D_SSD + D_STATE]
    Cm = xbc[..., D_SSD + D_STATE:]
    dt = jax.nn.softplus(dt_raw.reshape(b, L, 2, SSD_HEADS) + dt_bias)
    a_neg = -jnp.exp(a_log.astype(jnp.float32))
    y_f, h_f = ssd_chunked_scan(xs, dt[:, :, 0], a_neg[0], Bm, Cm, h0_fwd)
    flip = lambda t: jnp.flip(t, axis=1)
    y_b, h_b = ssd_chunked_scan(flip(xs), flip(dt[:, :, 1]), a_neg[1], flip(Bm), flip(Cm), h0_bwd)
    return y_f + flip(y_b), xs, h_f, h_b


def box_mean(u, w, axis):
    n = u.shape[axis]
    pad = [(0, 0)] * u.ndim
    pad[axis] = (1, 0)
    cs = jnp.pad(jnp.cumsum(u, axis=axis), pad)
    pos = jnp.arange(n)
    lo = jnp.clip(pos - w // 2, 0, n)
    hi = jnp.clip(pos + (w - w // 2), 0, n)
    total = jnp.take(cs, hi, axis=axis) - jnp.take(cs, lo, axis=axis)
    shape = [1] * u.ndim
    shape[axis] = n
    return total / (hi - lo).astype(u.dtype).reshape(shape)


def pool_mixer(u, rows, pool_w, pool_scale):
    b, L, _ = u.shape
    uf = u.astype(jnp.float32).reshape(b, L, N_POOL_GROUPS, POOL_GROUP_DIM)
    outs = []
    for g, w in enumerate(POOL_WINDOWS):
        ug = uf[:, :, g]
        if rows is None:
            m = box_mean(ug, w, 1)
        else:
            grid = ug.reshape(b, rows, GRID_W, POOL_GROUP_DIM)
            m = box_mean(box_mean(grid, w, 1), w, 2).reshape(b, L, POOL_GROUP_DIM)
        outs.append(m - ug)
    d = jnp.stack(outs, axis=2)
    y = jnp.einsum('blgi,gio->blgo', d, pool_w.astype(jnp.float32)).reshape(b, L, D_POOL)
    return (y * pool_scale.astype(jnp.float32)).astype(u.dtype)


def merge_head_groups(y_ssd, xs, z, u_pool, rows, d_skip, ssd_norm_g, pool_w, pool_scale, w_out):
    b, L = z.shape[:2]
    y = (y_ssd + d_skip[:, None] * xs).reshape(b, L, D_SSD)
    y = rms_norm(y * jax.nn.silu(z), ssd_norm_g)
    p = pool_mixer(u_pool, rows, pool_w, pool_scale)
    return jnp.concatenate([y, p], axis=-1) @ w_out


def swiglu(h, w_gate, w_up, w_down):
    return (jax.nn.silu(h @ w_gate) * (h @ w_up)) @ w_down


def _fwd_setup_inputs(seed: int = 0) -> dict:
    key = jax.random.key(seed)
    ks = jax.random.split(key, 26)
    f32 = jnp.float32
    nrm = lambda k, shape, s: jax.random.normal(k, shape, f32) * s
    dt0 = jnp.exp(jax.random.uniform(ks[10], (DEPTH, 2, SSD_HEADS), f32,
                                     minval=math.log(1e-3), maxval=math.log(1e-1)))
    return {
        'x': nrm(ks[0], (BATCH, SEQ, D_MODEL), 1.0),
        'c': nrm(ks[1], (BATCH, D_MODEL), 1.0),
        'ctx': nrm(ks[2], (BATCH, CTX_LEN, D_MODEL), 1.0),
        'c_ctx': nrm(ks[3], (D_MODEL,), 1.0),
        'emb_ln_g': 1.0 + nrm(ks[4], (D_MODEL,), 0.02),
        'emb_ln_b': nrm(ks[5], (D_MODEL,), 0.02),
        'w_ada': nrm(ks[6], (DEPTH, D_MODEL, 6 * D_MODEL), 0.5 * D_MODEL ** -0.5),
        'b_ada': nrm(ks[7], (DEPTH, 6 * D_MODEL), 0.01),
        'in_proj': nrm(ks[8], (DEPTH, D_MODEL, D_IN_PROJ), D_MODEL ** -0.5),
        'conv_w': nrm(ks[9], (DEPTH, D_CONV, D_XBC), D_CONV ** -0.5),
        'conv_b': nrm(ks[11], (DEPTH, D_XBC), 0.01),
        'dt_bias': dt0 + jnp.log(-jnp.expm1(-dt0)),
        'a_log': jnp.log(jax.random.uniform(ks[12], (DEPTH, 2, SSD_HEADS), f32, minval=1.0, maxval=16.0)),
        'd_skip': 1.0 + nrm(ks[13], (DEPTH, SSD_HEADS), 0.1),
        'ssd_norm_g': 1.0 + nrm(ks[14], (DEPTH, D_SSD), 0.02),
        'pool_w': nrm(ks[15], (DEPTH, N_POOL_GROUPS, POOL_GROUP_DIM, POOL_GROUP_DIM), POOL_GROUP_DIM ** -0.5),
        'pool_scale': 1.0 + nrm(ks[16], (DEPTH, D_POOL), 0.02),
        'w_out': nrm(ks[17], (DEPTH, D_MIX, D_MODEL), DEEPNORM_BETA * D_MIX ** -0.5),
        'ln1_g': 1.0 + nrm(ks[18], (DEPTH, D_MODEL), 0.02),
        'ln1_b': nrm(ks[19], (DEPTH, D_MODEL), 0.02),
        'w_gate': nrm(ks[20], (DEPTH, D_MODEL, D_FF), D_MODEL ** -0.5),
        'w_up': nrm(ks[21], (DEPTH, D_MODEL, D_FF), D_MODEL ** -0.5),
        'w_down': nrm(ks[22], (DEPTH, D_FF, D_MODEL), DEEPNORM_BETA * D_FF ** -0.5),
        'ln2_g': 1.0 + nrm(ks[23], (DEPTH, D_MODEL), 0.02),
        'ln2_b': nrm(ks[24], (DEPTH, D_MODEL), 0.02),
    }


def _fwd_reference(x, c, ctx, c_ctx, emb_ln_g, emb_ln_b, w_ada, b_ada, in_proj, conv_w, conv_b,
              dt_bias, a_log, d_skip, ssd_norm_g, pool_w, pool_scale, w_out, ln1_g, ln1_b,
              w_gate, w_up, w_down, ln2_g, ln2_b):
    b = x.shape[0]
    rows = x.shape[1] // GRID_W
    x = layer_norm(x, emb_ln_g, emb_ln_b)
    xc = layer_norm(ctx, emb_ln_g, emb_ln_b)
    silu_c = jax.nn.silu(c)
    silu_cc = jax.nn.silu(c_ctx)
    h_zero = jnp.zeros((b, SSD_HEADS, SSD_HEAD_DIM, D_STATE), jnp.float32)
    for l in range(DEPTH):
        mod = (silu_c @ w_ada[l] + b_ada[l])[:, None, :]
        sh1, sc1, g1, sh2, sc2, g2 = jnp.split(mod, 6, axis=-1)
        modc = silu_cc @ w_ada[l] + b_ada[l]
        sh1c, sc1c, g1c, sh2c, sc2c, g2c = jnp.split(modc, 6, axis=-1)

        zc, xbcc, dtc, upc = split_projection(modulate(xc, sh1c, sc1c), in_proj[l])
        xbcc = jax.nn.silu(depthwise_conv_centred(xbcc, conv_w[l], conv_b[l]))
        yc, xsc, hf_ctx, hb_ctx = ssd_bidirectional(xbcc, dtc, dt_bias[l], a_log[l], h_zero, h_zero)

        z, xbc, dt_raw, up = split_projection(modulate(x, sh1, sc1), in_proj[l])
        xbc = jax.nn.silu(depthwise_conv_centred(xbc, conv_w[l], conv_b[l]))
        y, xs, _, _ = ssd_bidirectional(xbc, dt_raw, dt_bias[l], a_log[l], hf_ctx, hb_ctx)
        mix = merge_head_groups(y, xs, z, up, rows, d_skip[l], ssd_norm_g[l], pool_w[l],
                                pool_scale[l], w_out[l])
        x = layer_norm(DEEPNORM_ALPHA * x + g1 * mix, ln1_g[l], ln1_b[l])
        ffn = swiglu(modulate(x, sh2, sc2), w_gate[l], w_up[l], w_down[l])
        x = layer_norm(DEEPNORM_ALPHA * x + g2 * ffn, ln2_g[l], ln2_b[l])

        if l + 1 < DEPTH:
            mixc = merge_head_groups(yc, xsc, zc, upc, None, d_skip[l], ssd_norm_g[l], pool_w[l],
                                     pool_scale[l], w_out[l])
            xc = layer_norm(DEEPNORM_ALPHA * xc + g1c * mixc, ln1_g[l], ln1_b[l])
            ffnc = swiglu(modulate(xc, sh2c, sc2c), w_gate[l], w_up[l], w_down[l])
            xc = layer_norm(DEEPNORM_ALPHA * xc + g2c * ffnc, ln2_g[l], ln2_b[l])
    return x


import jax as _jax
import jax.numpy as _jnp

TWIN_FORMAT = 'train_step'
FWD_PARAMS = ['x', 'c', 'ctx', 'c_ctx', 'emb_ln_g', 'emb_ln_b', 'w_ada', 'b_ada', 'in_proj', 'conv_w', 'conv_b', 'dt_bias', 'a_log', 'd_skip', 'ssd_norm_g', 'pool_w', 'pool_scale', 'w_out', 'ln1_g', 'ln1_b', 'w_gate', 'w_up', 'w_down', 'ln2_g', 'ln2_b']
TWIN_WEIGHTS = ['c_ctx', 'emb_ln_g', 'emb_ln_b', 'w_ada', 'b_ada', 'in_proj', 'conv_w', 'conv_b', 'dt_bias', 'a_log', 'd_skip', 'ssd_norm_g', 'pool_w', 'pool_scale', 'w_out', 'ln1_g', 'ln1_b', 'w_gate', 'w_up', 'w_down', 'ln2_g', 'ln2_b']
TWIN_DIFF_INPUT = 'x'
TWIN_INPUTS = ['x', 'c', 'ctx', 'c_ctx', 'emb_ln_g', 'emb_ln_b', 'w_ada', 'b_ada', 'in_proj', 'conv_w', 'conv_b', 'dt_bias', 'a_log', 'd_skip', 'ssd_norm_g', 'pool_w', 'pool_scale', 'w_out', 'ln1_g', 'ln1_b', 'w_gate', 'w_up', 'w_down', 'ln2_g', 'ln2_b', 'loss_target', 'm_c_ctx', 'm_emb_ln_g', 'm_emb_ln_b', 'm_w_ada', 'm_b_ada', 'm_in_proj', 'm_conv_w', 'm_conv_b', 'm_dt_bias', 'm_a_log', 'm_d_skip', 'm_ssd_norm_g', 'm_pool_w', 'm_pool_scale', 'm_w_out', 'm_ln1_g', 'm_ln1_b', 'm_w_gate', 'm_w_up', 'm_w_down', 'm_ln2_g', 'm_ln2_b', 'v_c_ctx', 'v_emb_ln_g', 'v_emb_ln_b', 'v_w_ada', 'v_b_ada', 'v_in_proj', 'v_conv_w', 'v_conv_b', 'v_dt_bias', 'v_a_log', 'v_d_skip', 'v_ssd_norm_g', 'v_pool_w', 'v_pool_scale', 'v_w_out', 'v_ln1_g', 'v_ln1_b', 'v_w_gate', 'v_w_up', 'v_w_down', 'v_ln2_g', 'v_ln2_b']
TWIN_OUTPUTS = ['loss', 'grad_x', 'grad_c_ctx', 'grad_emb_ln_g', 'grad_emb_ln_b', 'grad_w_ada', 'grad_b_ada', 'grad_in_proj', 'grad_conv_w', 'grad_conv_b', 'grad_dt_bias', 'grad_a_log', 'grad_d_skip', 'grad_ssd_norm_g', 'grad_pool_w', 'grad_pool_scale', 'grad_w_out', 'grad_ln1_g', 'grad_ln1_b', 'grad_w_gate', 'grad_w_up', 'grad_w_down', 'grad_ln2_g', 'grad_ln2_b', 'delta_c_ctx', 'delta_emb_ln_g', 'delta_emb_ln_b', 'delta_w_ada', 'delta_b_ada', 'delta_in_proj', 'delta_conv_w', 'delta_conv_b', 'delta_dt_bias', 'delta_a_log', 'delta_d_skip', 'delta_ssd_norm_g', 'delta_pool_w', 'delta_pool_scale', 'delta_w_out', 'delta_ln1_g', 'delta_ln1_b', 'delta_w_gate', 'delta_w_up', 'delta_w_down', 'delta_ln2_g', 'delta_ln2_b', 'new_m_c_ctx', 'new_m_emb_ln_g', 'new_m_emb_ln_b', 'new_m_w_ada', 'new_m_b_ada', 'new_m_in_proj', 'new_m_conv_w', 'new_m_conv_b', 'new_m_dt_bias', 'new_m_a_log', 'new_m_d_skip', 'new_m_ssd_norm_g', 'new_m_pool_w', 'new_m_pool_scale', 'new_m_w_out', 'new_m_ln1_g', 'new_m_ln1_b', 'new_m_w_gate', 'new_m_w_up', 'new_m_w_down', 'new_m_ln2_g', 'new_m_ln2_b', 'new_v_c_ctx', 'new_v_emb_ln_g', 'new_v_emb_ln_b', 'new_v_w_ada', 'new_v_b_ada', 'new_v_in_proj', 'new_v_conv_w', 'new_v_conv_b', 'new_v_dt_bias', 'new_v_a_log', 'new_v_d_skip', 'new_v_ssd_norm_g', 'new_v_pool_w', 'new_v_pool_scale', 'new_v_w_out', 'new_v_ln1_g', 'new_v_ln1_b', 'new_v_w_gate', 'new_v_w_up', 'new_v_w_down', 'new_v_ln2_g', 'new_v_ln2_b']
TWIN_LEAF_KINDS = {'loss': 'loss', 'grad_x': 'grad_x', 'grad_c_ctx': 'grad_w', 'grad_emb_ln_g': 'grad_w', 'grad_emb_ln_b': 'grad_w', 'grad_w_ada': 'grad_w', 'grad_b_ada': 'grad_w', 'grad_in_proj': 'grad_w', 'grad_conv_w': 'grad_w', 'grad_conv_b': 'grad_w', 'grad_dt_bias': 'grad_w', 'grad_a_log': 'grad_w', 'grad_d_skip': 'grad_w', 'grad_ssd_norm_g': 'grad_w', 'grad_pool_w': 'grad_w', 'grad_pool_scale': 'grad_w', 'grad_w_out': 'grad_w', 'grad_ln1_g': 'grad_w', 'grad_ln1_b': 'grad_w', 'grad_w_gate': 'grad_w', 'grad_w_up': 'grad_w', 'grad_w_down': 'grad_w', 'grad_ln2_g': 'grad_w', 'grad_ln2_b': 'grad_w', 'delta_c_ctx': 'delta_w', 'delta_emb_ln_g': 'delta_w', 'delta_emb_ln_b': 'delta_w', 'delta_w_ada': 'delta_w', 'delta_b_ada': 'delta_w', 'delta_in_proj': 'delta_w', 'delta_conv_w': 'delta_w', 'delta_conv_b': 'delta_w', 'delta_dt_bias': 'delta_w', 'delta_a_log': 'delta_w', 'delta_d_skip': 'delta_w', 'delta_ssd_norm_g': 'delta_w', 'delta_pool_w': 'delta_w', 'delta_pool_scale': 'delta_w', 'delta_w_out': 'delta_w', 'delta_ln1_g': 'delta_w', 'delta_ln1_b': 'delta_w', 'delta_w_gate': 'delta_w', 'delta_w_up': 'delta_w', 'delta_w_down': 'delta_w', 'delta_ln2_g': 'delta_w', 'delta_ln2_b': 'delta_w', 'new_m_c_ctx': 'new_m', 'new_m_emb_ln_g': 'new_m', 'new_m_emb_ln_b': 'new_m', 'new_m_w_ada': 'new_m', 'new_m_b_ada': 'new_m', 'new_m_in_proj': 'new_m', 'new_m_conv_w': 'new_m', 'new_m_conv_b': 'new_m', 'new_m_dt_bias': 'new_m', 'new_m_a_log': 'new_m', 'new_m_d_skip': 'new_m', 'new_m_ssd_norm_g': 'new_m', 'new_m_pool_w': 'new_m', 'new_m_pool_scale': 'new_m', 'new_m_w_out': 'new_m', 'new_m_ln1_g': 'new_m', 'new_m_ln1_b': 'new_m', 'new_m_w_gate': 'new_m', 'new_m_w_up': 'new_m', 'new_m_w_down': 'new_m', 'new_m_ln2_g': 'new_m', 'new_m_ln2_b': 'new_m', 'new_v_c_ctx': 'new_v', 'new_v_emb_ln_g': 'new_v', 'new_v_emb_ln_b': 'new_v', 'new_v_w_ada': 'new_v', 'new_v_b_ada': 'new_v', 'new_v_in_proj': 'new_v', 'new_v_conv_w': 'new_v', 'new_v_conv_b': 'new_v', 'new_v_dt_bias': 'new_v', 'new_v_a_log': 'new_v', 'new_v_d_skip': 'new_v', 'new_v_ssd_norm_g': 'new_v', 'new_v_pool_w': 'new_v', 'new_v_pool_scale': 'new_v', 'new_v_w_out': 'new_v', 'new_v_ln1_g': 'new_v', 'new_v_ln1_b': 'new_v', 'new_v_w_gate': 'new_v', 'new_v_w_up': 'new_v', 'new_v_w_down': 'new_v', 'new_v_ln2_g': 'new_v', 'new_v_ln2_b': 'new_v'}


def _forward(args):
    return _fwd_reference(*[args[k] for k in FWD_PARAMS])


def _output_shape():
    def fwd():
        inp = _fwd_setup_inputs(0)
        return _fwd_reference(*[inp[k] for k in FWD_PARAMS])
    out = _jax.eval_shape(fwd)
    return out.shape, out.dtype

N_MICROBATCH = 1
ADAM_LR = 0.001
ADAM_B1 = 0.9
ADAM_B2 = 0.999
ADAM_EPS = 1e-08
ADAM_WD = 0.01
ADAM_STEP = 10
PER_EXAMPLE_BATCH_AXIS = {'x': 0, 'c': 0, 'ctx': 0, 'loss_target': 0}
SHARED_INPUTS = []
_WEIGHT_DTYPES = {'c_ctx': _jnp.float32, 'emb_ln_g': _jnp.float32, 'emb_ln_b': _jnp.float32, 'w_ada': _jnp.float32, 'b_ada': _jnp.float32, 'in_proj': _jnp.float32, 'conv_w': _jnp.float32, 'conv_b': _jnp.float32, 'dt_bias': _jnp.float32, 'a_log': _jnp.float32, 'd_skip': _jnp.float32, 'ssd_norm_g': _jnp.float32, 'pool_w': _jnp.float32, 'pool_scale': _jnp.float32, 'w_out': _jnp.float32, 'ln1_g': _jnp.float32, 'ln1_b': _jnp.float32, 'w_gate': _jnp.float32, 'w_up': _jnp.float32, 'w_down': _jnp.float32, 'ln2_g': _jnp.float32, 'ln2_b': _jnp.float32}
MOMENT_SCALE = {'c_ctx': 2.241190e-03, 'emb_ln_g': 4.211422e+00, 'emb_ln_b': 1.759947e+00, 'w_ada': 6.380817e-02, 'b_ada': 1.125508e-01, 'in_proj': 3.753584e-02, 'conv_w': 3.744447e-02, 'conv_b': 5.279138e-02, 'dt_bias': 7.041554e-02, 'a_log': 1.516791e-01, 'd_skip': 1.962036e-01, 'ssd_norm_g': 4.060518e-02, 'pool_w': 3.838641e-02, 'pool_scale': 3.782824e-02, 'w_out': 9.408754e-02, 'ln1_g': 4.322344e+00, 'ln1_b': 1.820647e+00, 'w_gate': 2.419022e-02, 'w_up': 2.358296e-02, 'w_down': 6.551951e-02, 'ln2_g': 1.280317e+02, 'ln2_b': 3.716005e+00}


def _to_microbatches(a, axis):
    t = _jnp.moveaxis(a, axis, 0)
    t = t.reshape((N_MICROBATCH, t.shape[0] // N_MICROBATCH) + t.shape[1:])
    return _jnp.moveaxis(t, 1, axis + 1)


def setup_inputs(seed: int = 0) -> dict:
    inp = _fwd_setup_inputs(seed)
    key = _jax.random.fold_in(_jax.random.key(seed), 7919)
    shape, _ = _output_shape()
    out = dict(inp)
    out["loss_target"] = _jax.random.normal(_jax.random.fold_in(key, 0), shape, _jnp.float32)
    for i, name in enumerate(TWIN_WEIGHTS):
        w = inp[name].astype(_jnp.float32)
        if MOMENT_SCALE is None:
            s = _jnp.sqrt(_jnp.mean(_jnp.square(w)) + 1e-30)
        else:
            s = MOMENT_SCALE[name]
        km, kv = _jax.random.split(_jax.random.fold_in(key, i + 1))
        out[name] = w
        out["m_" + name] = s * _jax.random.normal(km, w.shape, _jnp.float32)
        out["v_" + name] = (s * s) * _jax.random.uniform(kv, w.shape, _jnp.float32, 0.5, 1.5)
    if N_MICROBATCH > 1:
        for name, axis in PER_EXAMPLE_BATCH_AXIS.items():
            out[name] = _to_microbatches(out[name], axis)
    return {'x': out['x'], 'c': out['c'], 'ctx': out['ctx'], 'c_ctx': out['c_ctx'], 'emb_ln_g': out['emb_ln_g'], 'emb_ln_b': out['emb_ln_b'], 'w_ada': out['w_ada'], 'b_ada': out['b_ada'], 'in_proj': out['in_proj'], 'conv_w': out['conv_w'], 'conv_b': out['conv_b'], 'dt_bias': out['dt_bias'], 'a_log': out['a_log'], 'd_skip': out['d_skip'], 'ssd_norm_g': out['ssd_norm_g'], 'pool_w': out['pool_w'], 'pool_scale': out['pool_scale'], 'w_out': out['w_out'], 'ln1_g': out['ln1_g'], 'ln1_b': out['ln1_b'], 'w_gate': out['w_gate'], 'w_up': out['w_up'], 'w_down': out['w_down'], 'ln2_g': out['ln2_g'], 'ln2_b': out['ln2_b'], 'loss_target': out['loss_target'], 'm_c_ctx': out['m_c_ctx'], 'm_emb_ln_g': out['m_emb_ln_g'], 'm_emb_ln_b': out['m_emb_ln_b'], 'm_w_ada': out['m_w_ada'], 'm_b_ada': out['m_b_ada'], 'm_in_proj': out['m_in_proj'], 'm_conv_w': out['m_conv_w'], 'm_conv_b': out['m_conv_b'], 'm_dt_bias': out['m_dt_bias'], 'm_a_log': out['m_a_log'], 'm_d_skip': out['m_d_skip'], 'm_ssd_norm_g': out['m_ssd_norm_g'], 'm_pool_w': out['m_pool_w'], 'm_pool_scale': out['m_pool_scale'], 'm_w_out': out['m_w_out'], 'm_ln1_g': out['m_ln1_g'], 'm_ln1_b': out['m_ln1_b'], 'm_w_gate': out['m_w_gate'], 'm_w_up': out['m_w_up'], 'm_w_down': out['m_w_down'], 'm_ln2_g': out['m_ln2_g'], 'm_ln2_b': out['m_ln2_b'], 'v_c_ctx': out['v_c_ctx'], 'v_emb_ln_g': out['v_emb_ln_g'], 'v_emb_ln_b': out['v_emb_ln_b'], 'v_w_ada': out['v_w_ada'], 'v_b_ada': out['v_b_ada'], 'v_in_proj': out['v_in_proj'], 'v_conv_w': out['v_conv_w'], 'v_conv_b': out['v_conv_b'], 'v_dt_bias': out['v_dt_bias'], 'v_a_log': out['v_a_log'], 'v_d_skip': out['v_d_skip'], 'v_ssd_norm_g': out['v_ssd_norm_g'], 'v_pool_w': out['v_pool_w'], 'v_pool_scale': out['v_pool_scale'], 'v_w_out': out['v_w_out'], 'v_ln1_g': out['v_ln1_g'], 'v_ln1_b': out['v_ln1_b'], 'v_w_gate': out['v_w_gate'], 'v_w_up': out['v_w_up'], 'v_w_down': out['v_w_down'], 'v_ln2_g': out['v_ln2_g'], 'v_ln2_b': out['v_ln2_b']}


def _loss(weights, diff, rest, loss_target):
    with _jax.named_scope("forward"):
        args = {**rest, TWIN_DIFF_INPUT: diff, **{k: w.astype(_WEIGHT_DTYPES[k]) for k, w in weights.items()}}
        y = _forward(args)
    with _jax.named_scope("loss_head"):
        err = _jnp.square(y.astype(_jnp.float32) - loss_target)
        return 0.5 * _jnp.sum(_jnp.mean(err, axis=-1)) if err.ndim else 0.5 * err


def _adamw(w, g, m, v):
    m = ADAM_B1 * m + (1.0 - ADAM_B1) * g
    v = ADAM_B2 * v + (1.0 - ADAM_B2) * _jnp.square(g)
    m_hat = m / (1.0 - ADAM_B1 ** ADAM_STEP)
    v_hat = v / (1.0 - ADAM_B2 ** ADAM_STEP)
    delta = -ADAM_LR * (m_hat / (_jnp.sqrt(v_hat) + ADAM_EPS) + ADAM_WD * w)
    return delta, m, v


def reference(x, c, ctx, c_ctx, emb_ln_g, emb_ln_b, w_ada, b_ada, in_proj, conv_w, conv_b, dt_bias, a_log, d_skip, ssd_norm_g, pool_w, pool_scale, w_out, ln1_g, ln1_b, w_gate, w_up, w_down, ln2_g, ln2_b, loss_target, m_c_ctx, m_emb_ln_g, m_emb_ln_b, m_w_ada, m_b_ada, m_in_proj, m_conv_w, m_conv_b, m_dt_bias, m_a_log, m_d_skip, m_ssd_norm_g, m_pool_w, m_pool_scale, m_w_out, m_ln1_g, m_ln1_b, m_w_gate, m_w_up, m_w_down, m_ln2_g, m_ln2_b, v_c_ctx, v_emb_ln_g, v_emb_ln_b, v_w_ada, v_b_ada, v_in_proj, v_conv_w, v_conv_b, v_dt_bias, v_a_log, v_d_skip, v_ssd_norm_g, v_pool_w, v_pool_scale, v_w_out, v_ln1_g, v_ln1_b, v_w_gate, v_w_up, v_w_down, v_ln2_g, v_ln2_b):
    given = dict(x=x, c=c, ctx=ctx, c_ctx=c_ctx, emb_ln_g=emb_ln_g, emb_ln_b=emb_ln_b, w_ada=w_ada, b_ada=b_ada, in_proj=in_proj, conv_w=conv_w, conv_b=conv_b, dt_bias=dt_bias, a_log=a_log, d_skip=d_skip, ssd_norm_g=ssd_norm_g, pool_w=pool_w, pool_scale=pool_scale, w_out=w_out, ln1_g=ln1_g, ln1_b=ln1_b, w_gate=w_gate, w_up=w_up, w_down=w_down, ln2_g=ln2_g, ln2_b=ln2_b, loss_target=loss_target, m_c_ctx=m_c_ctx, m_emb_ln_g=m_emb_ln_g, m_emb_ln_b=m_emb_ln_b, m_w_ada=m_w_ada, m_b_ada=m_b_ada, m_in_proj=m_in_proj, m_conv_w=m_conv_w, m_conv_b=m_conv_b, m_dt_bias=m_dt_bias, m_a_log=m_a_log, m_d_skip=m_d_skip, m_ssd_norm_g=m_ssd_norm_g, m_pool_w=m_pool_w, m_pool_scale=m_pool_scale, m_w_out=m_w_out, m_ln1_g=m_ln1_g, m_ln1_b=m_ln1_b, m_w_gate=m_w_gate, m_w_up=m_w_up, m_w_down=m_w_down, m_ln2_g=m_ln2_g, m_ln2_b=m_ln2_b, v_c_ctx=v_c_ctx, v_emb_ln_g=v_emb_ln_g, v_emb_ln_b=v_emb_ln_b, v_w_ada=v_w_ada, v_b_ada=v_b_ada, v_in_proj=v_in_proj, v_conv_w=v_conv_w, v_conv_b=v_conv_b, v_dt_bias=v_dt_bias, v_a_log=v_a_log, v_d_skip=v_d_skip, v_ssd_norm_g=v_ssd_norm_g, v_pool_w=v_pool_w, v_pool_scale=v_pool_scale, v_w_out=v_w_out, v_ln1_g=v_ln1_g, v_ln1_b=v_ln1_b, v_w_gate=v_w_gate, v_w_up=v_w_up, v_w_down=v_w_down, v_ln2_g=v_ln2_g, v_ln2_b=v_ln2_b)
    weights = {n: given[n] for n in TWIN_WEIGHTS}
    shared = {n: given[n] for n in SHARED_INPUTS}
    per_example = {n: given[n] for n in ['x', 'c', 'ctx']}
    grad_fn = _jax.value_and_grad(_loss, argnums=(0, 1))

    def one_microbatch(ex, loss_target):
        ex = dict(ex)
        diff = ex.pop(TWIN_DIFF_INPUT)
        return grad_fn(weights, diff, {**shared, **ex}, loss_target)

    if N_MICROBATCH == 1:
        loss, (grad_w, grad_x) = one_microbatch(per_example, given["loss_target"])
    else:
        def body(carry, xs):
            loss_sum, grad_sum = carry
            l_k, (gw_k, gx_k) = one_microbatch(xs[0], xs[1])
            with _jax.named_scope("update"):
                return (loss_sum + l_k, _jax.tree.map(_jnp.add, grad_sum, gw_k)), gx_k

        init = (_jnp.zeros((), _jnp.float32), _jax.tree.map(_jnp.zeros_like, weights))
        (loss, grad_w), grad_x = _jax.lax.scan(body, init, (per_example, given["loss_target"]))
    with _jax.named_scope("update"):
        delta_w, new_m, new_v = {}, {}, {}
        for n in TWIN_WEIGHTS:
            delta_w[n], new_m[n], new_v[n] = _adamw(weights[n], grad_w[n], given["m_" + n], given["v_" + n])
    return (loss, grad_x, *[grad_w[n] for n in TWIN_WEIGHTS], *[delta_w[n] for n in TWIN_WEIGHTS],
            *[new_m[n] for n in TWIN_WEIGHTS], *[new_v[n] for n in TWIN_WEIGHTS])
```

```python
import functools
import math

import jax
import jax.numpy as jnp
from jax import lax
from jax.experimental import pallas as pl
from jax.experimental.pallas import tpu as pltpu

F32 = jnp.float32
BF16 = jnp.bfloat16
MESH = pl.DeviceIdType.MESH

D_MODEL = 1024
SSD_HEADS = 16
HEAD_DIM = 64
D_STATE = 128
CHUNK = 128
D_CONV = 5
D_XBC = D_MODEL + 2 * D_STATE
D_XD = 1408
N_POOL = 4
POOL_DIM = 256
POOL_WINDOWS = (2, 4, 8, 16)
GRID_W = 64
D_FF = 2816
D_IN_PROJ = 3360
LN_EPS = 1e-5
ALPHA = 2.0 ** 0.25
POOL_TB = 512

ADAM_LR = 0.001
ADAM_B1 = 0.9
ADAM_B2 = 0.999
ADAM_EPS = 1e-08
ADAM_WD = 0.01
ADAM_STEP = 10

VMEM_LIMIT = 56 * 1024 * 1024


def _cp(sem=None):
    return pltpu.CompilerParams(dimension_semantics=sem, vmem_limit_bytes=VMEM_LIMIT)


def _sigmoid(x):
    return 1.0 / (1.0 + jnp.exp(-x))


def _silu(x):
    return x * _sigmoid(x)


def _dsilu(x):
    s = _sigmoid(x)
    return s * (1.0 + x * (1.0 - s))


def _softplus(x):
    t = jnp.exp(-jnp.abs(x))
    u = 1.0 + t
    log1p = jnp.where(u == 1.0, t, jnp.log(u) * t / (u - 1.0 + (u == 1.0)))
    return jnp.maximum(x, 0.0) + log1p


def _split(x, n):
    parts, r = [], x
    for _ in range(n):
        p = r.astype(BF16)
        parts.append(p)
        r = r - p.astype(F32)
    return parts


def _dot(a, b):
    return jnp.dot(a, b, preferred_element_type=F32)


def _dot_nt(a, b):
    return lax.dot_general(a, b, (((1,), (1,)), ((), ())), preferred_element_type=F32)


def _dot_tn(a, b):
    return lax.dot_general(a, b, (((0,), (0,)), ((), ())), preferred_element_type=F32)


def _dot_sel_l(sel_bf, x, n=3):
    out = None
    for p in _split(x, n):
        t = _dot(sel_bf, p)
        out = t if out is None else out + t
    return out


def _dot_sel_r(x, sel_bf, n=3):
    out = None
    for p in _split(x, n):
        t = _dot(p, sel_bf)
        out = t if out is None else out + t
    return out


def _row_block(n, cap=256, mult=8):
    best = None
    for t in range(mult, min(n, cap) + 1, mult):
        if n % t == 0:
            best = t
    return best if best is not None else n


def _vec(v):
    return v.reshape(1, -1).astype(F32)


def _mods_fwd(c8, wada_bf, b_ada):
    d = c8.shape[1]
    n = wada_bf.shape[1]

    def body(c_ref, w_ref, b_ref, o_ref):
        s = _silu(c_ref[...]).astype(BF16)
        o_ref[...] = _dot(s, w_ref[...]) + b_ref[...]

    return pl.pallas_call(
        body, name="mods_fwd", grid=(n // d,),
        in_specs=[pl.BlockSpec((8, d), lambda j: (0, 0)),
                  pl.BlockSpec((d, d), lambda j: (0, j)),
                  pl.BlockSpec((1, d), lambda j: (0, j))],
        out_specs=pl.BlockSpec((8, d), lambda j: (0, j)),
        out_shape=jax.ShapeDtypeStruct((8, n), F32),
        compiler_params=_cp(("arbitrary",)),
    )(c8, wada_bf, b_ada)


def _mods_bwd_w(ct8, dm8):
    d = ct8.shape[0]
    n = dm8.shape[1]
    tn = 512

    def body(ct_ref, dm_ref, dw_ref, db_ref):
        s = _silu(ct_ref[...])
        dm = dm_ref[...]
        dw_ref[...] = s[:, 0:1] * dm[0:1, :] + s[:, 1:2] * dm[1:2, :]
        db_ref[...] = jnp.broadcast_to(dm[0:1, :] + dm[1:2, :], (8, tn))

    return pl.pallas_call(
        body, name="mods_bwd_w", grid=(n // tn,),
        in_specs=[pl.BlockSpec((d, 8), lambda j: (0, 0)),
                  pl.BlockSpec((8, tn), lambda j: (0, j))],
        out_specs=[pl.BlockSpec((d, tn), lambda j: (0, j)),
                   pl.BlockSpec((8, tn), lambda j: (0, j))],
        out_shape=[jax.ShapeDtypeStruct((d, n), F32), jax.ShapeDtypeStruct((8, n), F32)],
        compiler_params=_cp(("arbitrary",)),
    )(ct8, dm8)


def _mods_bwd_c(dm8, wada_bf, c8):
    d = c8.shape[1]
    n = dm8.shape[1]
    nk = n // d

    def body(dm_ref, w_ref, c_ref, o_ref):
        k = pl.program_id(0)

        @pl.when(k == 0)
        def _():
            o_ref[...] = jnp.zeros_like(o_ref)

        o_ref[...] += _dot_nt(dm_ref[...].astype(BF16), w_ref[...])

        @pl.when(k == nk - 1)
        def _():
            o_ref[...] = o_ref[...] * _dsilu(c_ref[...])

    return pl.pallas_call(
        body, name="mods_bwd_c", grid=(nk,),
        in_specs=[pl.BlockSpec((8, d), lambda k: (0, k)),
                  pl.BlockSpec((d, d), lambda k: (0, k)),
                  pl.BlockSpec((8, d), lambda k: (0, 0))],
        out_specs=pl.BlockSpec((8, d), lambda k: (0, 0)),
        out_shape=jax.ShapeDtypeStruct((8, d), F32),
        compiler_params=_cp(("arbitrary",)),
    )(dm8, wada_bf, c8)


def _ln_stats(x):
    mu = jnp.mean(x, axis=-1, keepdims=True)
    xc = x - mu
    var = jnp.mean(xc * xc, axis=-1, keepdims=True)
    rstd = lax.rsqrt(var + LN_EPS)
    return xc * rstd, rstd


def _ln_bwd(dxhat, xhat, rstd):
    m1 = jnp.mean(dxhat, axis=-1, keepdims=True)
    m2 = jnp.mean(dxhat * xhat, axis=-1, keepdims=True)
    return rstd * (dxhat - m1 - xhat * m2)


def _row_spec(tb, d):
    return pl.BlockSpec((tb, d), lambda i: (i, 0))


def _par_spec(d):
    return pl.BlockSpec((1, d), lambda i: (0, 0))


def _acc_spec(d):
    return pl.BlockSpec((8, d), lambda i: (0, 0))


def _ln_mod(x, g, b, sh, sc, tb, name):
    n, d = x.shape

    def body(x_ref, g_ref, b_ref, sh_ref, sc_ref, x0_ref, h_ref):
        xhat, _ = _ln_stats(x_ref[...])
        x0 = xhat * g_ref[...] + b_ref[...]
        x0_ref[...] = x0
        h_ref[...] = (x0 * (1.0 + sc_ref[...]) + sh_ref[...]).astype(BF16)

    return pl.pallas_call(
        body, name=name, grid=(n // tb,),
        in_specs=[_row_spec(tb, d)] + [_par_spec(d)] * 4,
        out_specs=[_row_spec(tb, d), _row_spec(tb, d)],
        out_shape=[jax.ShapeDtypeStruct((n, d), F32), jax.ShapeDtypeStruct((n, d), BF16)],
        compiler_params=_cp(("parallel",)),
    )(x, g, b, sh, sc)


def _res_ln(xres, mix, gate, g, b, sh, sc, tb):
    n, d = xres.shape

    def body(xr_ref, mix_ref, gate_ref, g_ref, b_ref, sh_ref, sc_ref, x1_ref, h_ref):
        r = ALPHA * xr_ref[...] + gate_ref[...] * mix_ref[...]
        xhat, _ = _ln_stats(r)
        x1 = xhat * g_ref[...] + b_ref[...]
        x1_ref[...] = x1
        h_ref[...] = (x1 * (1.0 + sc_ref[...]) + sh_ref[...]).astype(BF16)

    return pl.pallas_call(
        body, name="res_ln1", grid=(n // tb,),
        in_specs=[_row_spec(tb, d)] * 2 + [_par_spec(d)] * 5,
        out_specs=[_row_spec(tb, d), _row_spec(tb, d)],
        out_shape=[jax.ShapeDtypeStruct((n, d), F32), jax.ShapeDtypeStruct((n, d), BF16)],
        compiler_params=_cp(("parallel",)),
    )(xres, mix, gate, g, b, sh, sc)


def _final_ln_loss(x1, ffn, gate, g, b, target, tb):
    n, d = x1.shape

    def body(x1_ref, ffn_ref, gate_ref, g_ref, b_ref, t_ref, dffn_ref, dr_ref, acc_ref):
        i = pl.program_id(0)

        @pl.when(i == 0)
        def _():
            acc_ref[...] = jnp.zeros_like(acc_ref)

        ffn = ffn_ref[...]
        r = ALPHA * x1_ref[...] + gate_ref[...] * ffn
        xhat, rstd = _ln_stats(r)
        err = xhat * g_ref[...] + b_ref[...] - t_ref[...]
        dx2 = err * (1.0 / d)
        dr = _ln_bwd(dx2 * g_ref[...], xhat, rstd)
        dr_ref[...] = dr
        dffn_ref[...] = (gate_ref[...] * dr).astype(BF16)
        acc_ref[0:1, :] += jnp.sum(dx2 * xhat, axis=0, keepdims=True)
        acc_ref[1:2, :] += jnp.sum(dx2, axis=0, keepdims=True)
        acc_ref[2:3, :] += jnp.sum(dr * ffn, axis=0, keepdims=True)
        acc_ref[3:4, :] += jnp.sum(err * err, axis=0, keepdims=True)

    return pl.pallas_call(
        body, name="final_ln_loss", grid=(n // tb,),
        in_specs=[_row_spec(tb, d)] * 2 + [_par_spec(d)] * 3 + [_row_spec(tb, d)],
        out_specs=[_row_spec(tb, d), _row_spec(tb, d), _acc_spec(d)],
        out_shape=[jax.ShapeDtypeStruct((n, d), BF16), jax.ShapeDtypeStruct((n, d), F32),
                   jax.ShapeDtypeStruct((8, d), F32)],
        compiler_params=_cp(("arbitrary",)),
    )(x1, ffn, gate, g, b, target)


def _bwd_ln1(dr2, dh2, x1, x0, mix, gate, sc2, g, tb):
    n, d = x1.shape

    def body(dr2_ref, dh2_ref, x1_ref, x0_ref, mix_ref, gate_ref, sc_ref, g_ref, dmix_ref, dr1_ref, acc_ref):
        i = pl.program_id(0)

        @pl.when(i == 0)
        def _():
            acc_ref[...] = jnp.zeros_like(acc_ref)

        dh2 = dh2_ref[...]
        mix = mix_ref[...]
        dx1 = ALPHA * dr2_ref[...] + dh2 * (1.0 + sc_ref[...])
        r = ALPHA * x0_ref[...] + gate_ref[...] * mix
        xhat, rstd = _ln_stats(r)
        dr1 = _ln_bwd(dx1 * g_ref[...], xhat, rstd)
        dr1_ref[...] = dr1
        dmix_ref[...] = (gate_ref[...] * dr1).astype(BF16)
        acc_ref[0:1, :] += jnp.sum(dh2 * x1_ref[...], axis=0, keepdims=True)
        acc_ref[1:2, :] += jnp.sum(dh2, axis=0, keepdims=True)
        acc_ref[2:3, :] += jnp.sum(dx1 * xhat, axis=0, keepdims=True)
        acc_ref[3:4, :] += jnp.sum(dx1, axis=0, keepdims=True)
        acc_ref[4:5, :] += jnp.sum(dr1 * mix, axis=0, keepdims=True)

    return pl.pallas_call(
        body, name="bwd_ln1", grid=(n // tb,),
        in_specs=[_row_spec(tb, d)] * 5 + [_par_spec(d)] * 3,
        out_specs=[_row_spec(tb, d), _row_spec(tb, d), _acc_spec(d)],
        out_shape=[jax.ShapeDtypeStruct((n, d), BF16), jax.ShapeDtypeStruct((n, d), F32),
                   jax.ShapeDtypeStruct((8, d), F32)],
        compiler_params=_cp(("arbitrary",)),
    )(dr2, dh2, x1, x0, mix, gate, sc2, g)


def _bwd_ln0(dres, dh, x, g, b, sc, tb, name):
    n, d = x.shape
    has_res = dres is not None

    def body(*refs):
        if has_res:
            dres_ref, dh_ref, x_ref, g_ref, b_ref, sc_ref, dx_ref, acc_ref = refs
        else:
            dh_ref, x_ref, g_ref, b_ref, sc_ref, dx_ref, acc_ref = refs
        i = pl.program_id(0)

        @pl.when(i == 0)
        def _():
            acc_ref[...] = jnp.zeros_like(acc_ref)

        dh = dh_ref[...]
        xhat, rstd = _ln_stats(x_ref[...])
        x0 = xhat * g_ref[...] + b_ref[...]
        dx0 = dh * (1.0 + sc_ref[...])
        if has_res:
            dx0 = dx0 + ALPHA * dres_ref[...]
        dx_ref[...] = _ln_bwd(dx0 * g_ref[...], xhat, rstd)
        acc_ref[0:1, :] += jnp.sum(dh * x0, axis=0, keepdims=True)
        acc_ref[1:2, :] += jnp.sum(dh, axis=0, keepdims=True)
        acc_ref[2:3, :] += jnp.sum(dx0 * xhat, axis=0, keepdims=True)
        acc_ref[3:4, :] += jnp.sum(dx0, axis=0, keepdims=True)

    ins = ([dres] if has_res else []) + [dh, x, g, b, sc]
    return pl.pallas_call(
        body, name=name, grid=(n // tb,),
        in_specs=[_row_spec(tb, d)] * (3 if has_res else 2) + [_par_spec(d)] * 3,
        out_specs=[_row_spec(tb, d), _acc_spec(d)],
        out_shape=[jax.ShapeDtypeStruct((n, d), F32), jax.ShapeDtypeStruct((8, d), F32)],
        compiler_params=_cp(("arbitrary",)),
    )(*ins)


def _matmul_nn(pairs, out_dtype, tm, tn, name):
    m = pairs[0][0].shape[0]
    n = pairs[0][1].shape[1]
    tm = min(tm, m)
    tn = min(tn, n)
    npair = len(pairs)

    def body(*refs):
        o_ref = refs[-1]
        acc = None
        for p in range(npair):
            t = _dot(refs[2 * p][...].astype(BF16), refs[2 * p + 1][...])
            acc = t if acc is None else acc + t
        o_ref[...] = acc.astype(out_dtype)

    in_specs, args = [], []
    for a, b in pairs:
        k = a.shape[1]
        in_specs += [pl.BlockSpec((tm, k), lambda i, j: (i, 0)), pl.BlockSpec((k, tn), lambda i, j: (0, j))]
        args += [a, b]
    return pl.pallas_call(
        body, name=name, grid=(m // tm, n // tn),
        in_specs=in_specs,
        out_specs=pl.BlockSpec((tm, tn), lambda i, j: (i, j)),
        out_shape=jax.ShapeDtypeStruct((m, n), out_dtype),
        compiler_params=_cp(("parallel", "arbitrary")),
    )(*args)


def _matmul_tn(a, g, tm, tn, name):
    m, k = a.shape
    n = g.shape[1]
    tm = min(tm, m)
    tn = min(tn, n)

    def body(a_ref, g_ref, o_ref):
        i = pl.program_id(1)

        @pl.when(i == 0)
        def _():
            o_ref[...] = jnp.zeros_like(o_ref)

        o_ref[...] += _dot_tn(a_ref[...].astype(BF16), g_ref[...].astype(BF16))

    return pl.pallas_call(
        body, name=name, grid=(n // tn, m // tm),
        in_specs=[pl.BlockSpec((tm, k), lambda j, i: (i, 0)), pl.BlockSpec((tm, tn), lambda j, i: (i, j))],
        out_specs=pl.BlockSpec((k, tn), lambda j, i: (0, j)),
        out_shape=jax.ShapeDtypeStruct((k, n), F32),
        compiler_params=_cp(("parallel", "arbitrary")),
    )(a, g)


def _swiglu_fwd(h, wg, wu, tm, tn):
    m, k = h.shape
    n = wg.shape[1]
    tm = min(tm, m)

    def body(h_ref, wg_ref, wu_ref, gate_ref, up_ref, hmid_ref):
        hv = h_ref[...]
        gate = _dot(hv, wg_ref[...])
        up = _dot(hv, wu_ref[...])
        gate_ref[...] = gate
        up_ref[...] = up
        hmid_ref[...] = (_silu(gate) * up).astype(BF16)

    blk = pl.BlockSpec((tm, tn), lambda i, j: (i, j))
    wspec = pl.BlockSpec((k, tn), lambda i, j: (0, j))
    return pl.pallas_call(
        body, name="swiglu_fwd", grid=(m // tm, n // tn),
        in_specs=[pl.BlockSpec((tm, k), lambda i, j: (i, 0)), wspec, wspec],
        out_specs=[blk, blk, blk],
        out_shape=[jax.ShapeDtypeStruct((m, n), F32), jax.ShapeDtypeStruct((m, n), F32),
                   jax.ShapeDtypeStruct((m, n), BF16)],
        compiler_params=_cp(("parallel", "arbitrary")),
    )(h, wg, wu)


def _swiglu_bwd(dffn, wdt, gate, up, tm, tn):
    m, k = dffn.shape
    n = wdt.shape[1]
    tm = min(tm, m)

    def body(d_ref, w_ref, gate_ref, up_ref, dg_ref, du_ref):
        dh = _dot(d_ref[...], w_ref[...])
        gate = gate_ref[...]
        dg_ref[...] = (dh * up_ref[...] * _dsilu(gate)).astype(BF16)
        du_ref[...] = (dh * _silu(gate)).astype(BF16)

    blk = pl.BlockSpec((tm, tn), lambda i, j: (i, j))
    return pl.pallas_call(
        body, name="swiglu_bwd", grid=(m // tm, n // tn),
        in_specs=[pl.BlockSpec((tm, k), lambda i, j: (i, 0)), pl.BlockSpec((k, tn), lambda i, j: (0, j)), blk, blk],
        out_specs=[blk, blk],
        out_shape=[jax.ShapeDtypeStruct((m, n), BF16), jax.ShapeDtypeStruct((m, n), BF16)],
        compiler_params=_cp(("parallel", "arbitrary")),
    )(dffn, wdt, gate, up)


def _halo_specs(tb, width, nrows):
    r8 = tb // 8
    last = nrows // 8 - 1
    prev = pl.BlockSpec((8, width), lambda i: (jnp.maximum(i * r8 - 1, 0), 0))
    nxt = pl.BlockSpec((8, width), lambda i: (jnp.minimum((i + 1) * r8, last), 0))
    return prev, nxt


def _fill_halo(buf, prev_ref, cur_ref, next_ref, tb, i, nb):
    buf[0:8, :] = prev_ref[...] * (i > 0).astype(F32)
    buf[8:8 + tb, :] = cur_ref[...]
    buf[8 + tb:16 + tb, :] = next_ref[...] * (i < nb - 1).astype(F32)


def _conv_fwd(xd, w8, b, tb, name):
    n = xd.shape[0]
    tb = min(tb, n)
    nb = n // tb
    prev, nxt = _halo_specs(tb, D_XBC, n)

    def body(p_ref, c_ref, n_ref, w_ref, b_ref, u_ref, buf):
        i = pl.program_id(0)
        _fill_halo(buf, p_ref, c_ref, n_ref, tb, i, nb)
        acc = jnp.broadcast_to(b_ref[...], (tb, D_XBC))
        for k in range(D_CONV):
            acc = acc + w_ref[k:k + 1, :] * buf[pl.ds(6 + k, tb), :]
        u_ref[...] = acc

    return pl.pallas_call(
        body, name=name, grid=(nb,),
        in_specs=[prev, pl.BlockSpec((tb, D_XBC), lambda i: (i, 0)), nxt,
                  pl.BlockSpec((8, D_XBC), lambda i: (0, 0)), _par_spec(D_XBC)],
        out_specs=_row_spec(tb, D_XBC),
        out_shape=jax.ShapeDtypeStruct((n, D_XBC), F32),
        scratch_shapes=[pltpu.VMEM((tb + 16, D_XBC), F32)],
        compiler_params=_cp(("parallel",)),
    )(xd, xd, xd, w8, b)


def _conv_bwd_a(dxs, dy, dskip_e, dbc, u, tb, name):
    n = u.shape[0]
    tb = min(tb, n)

    def body(dxs_ref, dy_ref, sk_ref, dbc_ref, u_ref, du_ref, acc_ref):
        i = pl.program_id(0)

        @pl.when(i == 0)
        def _():
            acc_ref[...] = jnp.zeros_like(acc_ref)

        uv = u_ref[...]
        ds = _dsilu(uv)
        gx = dxs_ref[0] + dxs_ref[1] + dy_ref[...] * sk_ref[...]
        gbc = dbc_ref[0] + dbc_ref[1]
        du = jnp.concatenate([gx, gbc], axis=1) * ds
        du_ref[...] = du
        acc_ref[0:1, :] += jnp.sum(du, axis=0, keepdims=True)

    return pl.pallas_call(
        body, name=name, grid=(n // tb,),
        in_specs=[pl.BlockSpec((2, tb, D_MODEL), lambda i: (0, i, 0)), _row_spec(tb, D_MODEL), _par_spec(D_MODEL),
                  pl.BlockSpec((2, tb, 2 * D_STATE), lambda i: (0, i, 0)), _row_spec(tb, D_XBC)],
        out_specs=[_row_spec(tb, D_XBC), _acc_spec(D_XBC)],
        out_shape=[jax.ShapeDtypeStruct((n, D_XBC), F32), jax.ShapeDtypeStruct((8, D_XBC), F32)],
        compiler_params=_cp(("arbitrary",)),
    )(dxs, dy, dskip_e, dbc, u)


def _conv_bwd_b(du, xd, ddt, w8, tb, name):
    n = du.shape[0]
    tb = min(tb, n)
    nb = n // tb
    prev, nxt = _halo_specs(tb, D_XBC, n)

    def body(dp_ref, dc_ref, dn_ref, xp_ref, xc_ref, xn_ref, ddt_ref, w_ref, dxd_ref, acc_ref, dbuf, xbuf):
        i = pl.program_id(0)

        @pl.when(i == 0)
        def _():
            acc_ref[...] = jnp.zeros_like(acc_ref)

        _fill_halo(dbuf, dp_ref, dc_ref, dn_ref, tb, i, nb)
        _fill_halo(xbuf, xp_ref, xc_ref, xn_ref, tb, i, nb)
        duc = dc_ref[...]
        acc = jnp.zeros((tb, D_XBC), F32)
        for k in range(D_CONV):
            acc = acc + w_ref[k:k + 1, :] * dbuf[pl.ds(10 - k, tb), :]
            acc_ref[k:k + 1, :] += jnp.sum(duc * xbuf[pl.ds(6 + k, tb), :], axis=0, keepdims=True)
        dxd_ref[:, 0:D_XBC] = acc.astype(BF16)
        ddt = ddt_ref[0] + pltpu.roll(ddt_ref[1], SSD_HEADS, 1)
        dxd_ref[:, D_XBC:D_XD] = ddt.astype(BF16)

    cur = pl.BlockSpec((tb, D_XBC), lambda i: (i, 0))
    return pl.pallas_call(
        body, name=name, grid=(nb,),
        in_specs=[prev, cur, nxt, prev, cur, nxt,
                  pl.BlockSpec((2, tb, 128), lambda i: (0, i, 0)), pl.BlockSpec((8, D_XBC), lambda i: (0, 0))],
        out_specs=[_row_spec(tb, D_XD), _acc_spec(D_XBC)],
        out_shape=[jax.ShapeDtypeStruct((n, D_XD), BF16), jax.ShapeDtypeStruct((8, D_XBC), F32)],
        scratch_shapes=[pltpu.VMEM((tb + 16, D_XBC), F32), pltpu.VMEM((tb + 16, D_XBC), F32)],
        compiler_params=_cp(("arbitrary",)),
    )(du, du, du, xd, xd, xd, ddt, w8)


def _ssd_chunk_index(nc, reverse):
    def idx(d, k):
        kk = (nc - 1 - k) if reverse else k
        return kk + d * (nc - 1 - 2 * kk)
    return idx


def _ssd_prologue(d, u_ref, xd_ref, bias_ref, a_ref, r_ref):
    q = CHUNK
    xbc = _silu(u_ref[...])
    xs = xbc[:, 0:D_MODEL]
    bm = xbc[:, D_MODEL:D_MODEL + D_STATE]
    cm = xbc[:, D_MODEL + D_STATE:D_XBC]
    row = lax.broadcasted_iota(jnp.int32, (q, q), 0)
    col = lax.broadcasted_iota(jnp.int32, (q, q), 1)
    sgn = 1 - 2 * d
    mask = ((row - col) * sgn) >= 0
    mask_t = ((row - col) * sgn) <= 0
    xdv = xd_ref[...]
    dtraw = jnp.where(d == 0, xdv, pltpu.roll(xdv, 128 - SSD_HEADS, 1)) + bias_ref[...]
    head_lane = col < SSD_HEADS
    dt = jnp.where(head_lane, _softplus(dtraw), 0.0)
    a = a_ref[...]
    tri = jnp.where(mask, 1.0, 0.0).astype(BF16)
    acum = _dot_sel_l(tri, dt * a)
    rexp = r_ref[...]
    acum_e = _dot_sel_r(acum, rexp)
    dt_e = _dot_sel_r(dt, rexp)
    alast_e = jnp.where(d == 0, acum_e[q - 1:q, :], acum_e[0:1, :])
    e = jnp.exp(acum_e)
    decay_end = jnp.exp(alast_e - acum_e)
    g = _dot_nt(cm.astype(BF16), bm.astype(BF16))
    return dict(xs=xs, bm=bm, cm=cm, mask=mask, mask_t=mask_t, dtraw=dtraw, head_lane=head_lane, dt=dt, a=a,
                acum=acum, acum_t=acum.T, dt_t=dt.T, dt_e=dt_e, alast_e=alast_e, e=e, decay_end=decay_end, g=g,
                col=col, row=row)


def _ssd_head_mats(p, h):
    seg = p["acum"][:, h:h + 1] - p["acum_t"][h:h + 1, :]
    lm = jnp.exp(jnp.where(p["mask"], seg, -jnp.inf))
    gl = p["g"] * lm
    s = gl * p["dt_t"][h:h + 1, :]
    return lm, gl, s


def _ssd_fwd(u, xd, bias2, a2, rexp, h0, name):
    n = u.shape[0]
    nc = n // CHUNK
    q = CHUNK
    cidx = _ssd_chunk_index(nc, reverse=False)

    def body(u_ref, xd_ref, bias_ref, a_ref, r_ref, h0_ref, y_ref, hp_ref, hf_ref, st):
        d = pl.program_id(0)
        k = pl.program_id(1)

        @pl.when(k == 0)
        def _():
            st[...] = h0_ref[...]

        p = _ssd_prologue(d, u_ref, xd_ref, bias_ref, a_ref, r_ref)
        stv = st[...]
        st_bf = stv.astype(BF16)
        hp_ref[...] = st_bf
        xs = p["xs"]
        lane128 = p["col"]
        y_off = _dot(p["cm"].astype(BF16), st_bf) * p["e"]
        for pb in range(SSD_HEADS // 2):
            _, _, s0 = _ssd_head_mats(p, 2 * pb)
            _, _, s1 = _ssd_head_mats(p, 2 * pb + 1)
            xp = xs[:, pb * 128:(pb + 1) * 128]
            rhs = jnp.concatenate([jnp.where(lane128 < HEAD_DIM, xp, 0.0), jnp.where(lane128 >= HEAD_DIM, xp, 0.0)],
                                  axis=0).astype(BF16)
            lhs = jnp.concatenate([s0, s1], axis=1).astype(BF16)
            y_ref[:, pb * 128:(pb + 1) * 128] = _dot(lhs, rhs) + y_off[:, pb * 128:(pb + 1) * 128]
        xw = (xs * (p["decay_end"] * p["dt_e"])).astype(BF16)
        new = stv * jnp.exp(p["alast_e"]) + _dot(p["bm"].T.astype(BF16), xw)
        st[...] = new
        hf_ref[...] = new

    return pl.pallas_call(
        body, name=name, grid=(2, nc),
        in_specs=[pl.BlockSpec((q, D_XBC), lambda d, k: (cidx(d, k), 0)),
                  pl.BlockSpec((q, 128), lambda d, k: (cidx(d, k), D_XBC // 128)),
                  pl.BlockSpec((None, 1, 128), lambda d, k: (d, 0, 0)),
                  pl.BlockSpec((None, 1, 128), lambda d, k: (d, 0, 0)),
                  pl.BlockSpec((128, D_MODEL), lambda d, k: (0, 0)),
                  pl.BlockSpec((None, D_STATE, D_MODEL), lambda d, k: (d, 0, 0))],
        out_specs=[pl.BlockSpec((None, q, D_MODEL), lambda d, k: (d, cidx(d, k), 0)),
                   pl.BlockSpec((None, None, D_STATE, D_MODEL), lambda d, k: (d, cidx(d, k), 0, 0)),
                   pl.BlockSpec((None, D_STATE, D_MODEL), lambda d, k: (d, 0, 0))],
        out_shape=[jax.ShapeDtypeStruct((2, n, D_MODEL), F32),
                   jax.ShapeDtypeStruct((2, nc, D_STATE, D_MODEL), BF16),
                   jax.ShapeDtypeStruct((2, D_STATE, D_MODEL), F32)],
        scratch_shapes=[pltpu.VMEM((D_STATE, D_MODEL), F32)],
        compiler_params=_cp(("arbitrary", "arbitrary")),
    )(u, xd, bias2, a2, rexp, h0)


def _ssd_bwd(u, xd, bias2, a2, rexp, rexp_t, dy, hprev, lam0, name):
    n = u.shape[0]
    nc = n // CHUNK
    q = CHUNK
    cidx = _ssd_chunk_index(nc, reverse=True)

    def body(u_ref, xd_ref, bias_ref, a_ref, r_ref, rt_ref, dy_ref, hp_ref, lam0_ref,
             dxs_ref, dbc_ref, ddt_ref, acc_ref, lamo_ref, lam):
        d = pl.program_id(0)
        k = pl.program_id(1)

        @pl.when(k == 0)
        def _():
            lam[...] = lam0_ref[...]
            acc_ref[...] = jnp.zeros_like(acc_ref)

        rexp_t = rt_ref[...]

        def hsum(t):
            return _dot_sel_r(t, rexp_t, n=2)

        p = _ssd_prologue(d, u_ref, xd_ref, bias_ref, a_ref, r_ref)
        xs, bm, cm = p["xs"], p["bm"], p["cm"]
        bm_bf, cm_bf = bm.astype(BF16), cm.astype(BF16)
        lamn = lam[...]
        lamn_bf = lamn.astype(BF16)
        stp = hp_ref[...]
        dyv = dy_ref[...]
        lane128 = p["col"]

        wend_e = p["decay_end"] * p["dt_e"]
        cs = _dot(cm_bf, stp)
        dye = dyv * p["e"]
        dye_bf = dye.astype(BF16)
        dc_off = _dot_nt(dye_bf, stp)
        v = _dot(bm_bf, lamn_bf)
        vx = v * xs
        om_e = vx * wend_e
        xw_bf = (xs * wend_e).astype(BF16)
        db_off = _dot_nt(xw_bf, lamn_bf)
        alast_big = jnp.exp(p["alast_e"])
        dlast_e = jnp.sum(stp.astype(F32) * lamn, axis=0, keepdims=True) * alast_big
        lam_new = lamn * alast_big + _dot(cm.T.astype(BF16), dye_bf)
        lam[...] = lam_new
        lamo_ref[...] = lam_new

        x1 = hsum(dye * cs - om_e)
        x2 = hsum(vx * p["decay_end"])
        x3 = hsum(jnp.broadcast_to(jnp.sum(om_e, axis=0, keepdims=True) + dlast_e, (8, D_MODEL)))

        sub16 = lax.broadcasted_iota(jnp.int32, (SSD_HEADS, q), 0)
        rs = jnp.zeros((q, 128), F32)
        cs_m = jnp.zeros((SSD_HEADS, q), F32)
        dt_m = jnp.zeros((SSD_HEADS, q), F32)
        dg = jnp.zeros((q, q), F32)
        for pb in range(SSD_HEADS // 2):
            xp_bf = xs[:, pb * 128:(pb + 1) * 128].astype(BF16)
            dyp = dyv[:, pb * 128:(pb + 1) * 128]
            dxs_pair = None
            for half in range(2):
                h = 2 * pb + half
                sel = (lane128 < HEAD_DIM) if half == 0 else (lane128 >= HEAD_DIM)
                dyh_bf = jnp.where(sel, dyp, 0.0).astype(BF16)
                lm, gl, s = _ssd_head_mats(p, h)
                ds = _dot_nt(dyh_bf, xp_bf)
                t = _dot_tn(s.astype(BF16), dyh_bf)
                dxs_pair = t if dxs_pair is None else dxs_pair + t
                w = ds * s
                rs = rs + jnp.sum(w, axis=1, keepdims=True) * (lane128 == h).astype(F32)
                cs_m = jnp.where(sub16 == h, jnp.sum(w, axis=0, keepdims=True), cs_m)
                dt_m = jnp.where(sub16 == h, jnp.sum(ds * gl, axis=0, keepdims=True), dt_m)
                dg = dg + ds * lm * p["dt_t"][h:h + 1, :]
            sl = slice(pb * 128, (pb + 1) * 128)
            dxs_ref[:, sl] = dxs_pair + v[:, sl] * wend_e[:, sl]

        def to_lanes(m16):
            return jnp.concatenate([m16, jnp.zeros((128 - SSD_HEADS, q), F32)], axis=0).T

        last = jnp.where(d == 0, q - 1, 0)
        dacum = rs - to_lanes(cs_m) + x1 + jnp.where(p["row"] == last, x3[0:1, :], 0.0)
        tri_t = jnp.where(p["mask_t"], 1.0, 0.0).astype(BF16)
        ddta = _dot_sel_l(tri_t, dacum)
        dt = p["dt"]
        a = p["a"]
        ddt = to_lanes(dt_m) + x2 + a * ddta
        ddtraw = jnp.where(p["head_lane"], ddt * _sigmoid(p["dtraw"]), 0.0)
        ddt_ref[...] = ddtraw
        acc_ref[0:1, :] += jnp.sum(ddtraw, axis=0, keepdims=True)
        acc_ref[1:2, :] += jnp.sum(dt * ddta, axis=0, keepdims=True) * a

        dg_bf = dg.astype(BF16)
        dbc_ref[:, 0:D_STATE] = _dot_tn(dg_bf, cm_bf) + db_off
        dbc_ref[:, D_STATE:2 * D_STATE] = _dot(dg_bf, bm_bf) + dc_off

    cblk = lambda d, k: (cidx(d, k), 0)
    return pl.pallas_call(
        body, name=name, grid=(2, nc),
        in_specs=[pl.BlockSpec((q, D_XBC), cblk),
                  pl.BlockSpec((q, 128), lambda d, k: (cidx(d, k), D_XBC // 128)),
                  pl.BlockSpec((None, 1, 128), lambda d, k: (d, 0, 0)),
                  pl.BlockSpec((None, 1, 128), lambda d, k: (d, 0, 0)),
                  pl.BlockSpec((128, D_MODEL), lambda d, k: (0, 0)),
                  pl.BlockSpec((D_MODEL, 128), lambda d, k: (0, 0)),
                  pl.BlockSpec((q, D_MODEL), cblk),
                  pl.BlockSpec((None, None, D_STATE, D_MODEL), lambda d, k: (d, cidx(d, k), 0, 0)),
                  pl.BlockSpec((None, D_STATE, D_MODEL), lambda d, k: (d, 0, 0))],
        out_specs=[pl.BlockSpec((None, q, D_MODEL), lambda d, k: (d, cidx(d, k), 0)),
                   pl.BlockSpec((None, q, 2 * D_STATE), lambda d, k: (d, cidx(d, k), 0)),
                   pl.BlockSpec((None, q, 128), lambda d, k: (d, cidx(d, k), 0)),
                   pl.BlockSpec((None, 8, 128), lambda d, k: (d, 0, 0)),
                   pl.BlockSpec((None, D_STATE, D_MODEL), lambda d, k: (d, 0, 0))],
        out_shape=[jax.ShapeDtypeStruct((2, n, D_MODEL), F32),
                   jax.ShapeDtypeStruct((2, n, 2 * D_STATE), F32),
                   jax.ShapeDtypeStruct((2, n, 128), F32),
                   jax.ShapeDtypeStruct((2, 8, 128), F32),
                   jax.ShapeDtypeStruct((2, D_STATE, D_MODEL), F32)],
        scratch_shapes=[pltpu.VMEM((D_STATE, D_MODEL), F32)],
        compiler_params=_cp(("arbitrary", "arbitrary")),
    )(u, xd, bias2, a2, rexp, rexp_t, dy, hprev, lam0)


def _merge_fwd(y, u, z, dskip_e, gn, tb):
    n = z.shape[0]

    def body(y_ref, u_ref, z_ref, sk_ref, gn_ref, o_ref):
        xs = _silu(u_ref[...])
        ys = y_ref[0] + y_ref[1] + sk_ref[...] * xs
        gated = ys * _silu(z_ref[...])
        rstd = lax.rsqrt(jnp.mean(gated * gated, axis=-1, keepdims=True) + LN_EPS)
        o_ref[...] = (gated * rstd * gn_ref[...]).astype(BF16)

    return pl.pallas_call(
        body, name="merge_fwd", grid=(n // tb,),
        in_specs=[pl.BlockSpec((2, tb, D_MODEL), lambda i: (0, i, 0)), pl.BlockSpec((tb, D_MODEL), lambda i: (i, 0)),
                  _row_spec(tb, D_MODEL), _par_spec(D_MODEL), _par_spec(D_MODEL)],
        out_specs=_row_spec(tb, D_MODEL),
        out_shape=jax.ShapeDtypeStruct((n, D_MODEL), BF16),
        compiler_params=_cp(("parallel",)),
    )(y, u, z, dskip_e, gn)


def _merge_bwd(dyn, y, u, z, dskip_e, gn, tb):
    n = z.shape[0]

    def body(dyn_ref, y_ref, u_ref, z_ref, sk_ref, gn_ref, dy_ref, dz_ref, acc_ref):
        i = pl.program_id(0)

        @pl.when(i == 0)
        def _():
            acc_ref[...] = jnp.zeros_like(acc_ref)

        xs = _silu(u_ref[...])
        zv = z_ref[...]
        ys = y_ref[0] + y_ref[1] + sk_ref[...] * xs
        gated = ys * _silu(zv)
        rstd = lax.rsqrt(jnp.mean(gated * gated, axis=-1, keepdims=True) + LN_EPS)
        ghat = gated * rstd
        dyn_v = dyn_ref[...]
        t = dyn_v * gn_ref[...]
        dgated = rstd * (t - ghat * jnp.mean(t * ghat, axis=-1, keepdims=True))
        dys = dgated * _silu(zv)
        dy_ref[...] = dys
        dz_ref[...] = (dgated * ys * _dsilu(zv)).astype(BF16)
        acc_ref[0:1, :] += jnp.sum(dyn_v * ghat, axis=0, keepdims=True)
        acc_ref[1:2, :] += jnp.sum(dys * xs, axis=0, keepdims=True)

    return pl.pallas_call(
        body, name="merge_bwd", grid=(n // tb,),
        in_specs=[_row_spec(tb, D_MODEL), pl.BlockSpec((2, tb, D_MODEL), lambda i: (0, i, 0)),
                  pl.BlockSpec((tb, D_MODEL), lambda i: (i, 0)), _row_spec(tb, D_MODEL),
                  _par_spec(D_MODEL), _par_spec(D_MODEL)],
        out_specs=[_row_spec(tb, D_MODEL), _row_spec(tb, D_MODEL), _acc_spec(D_MODEL)],
        out_shape=[jax.ShapeDtypeStruct((n, D_MODEL), F32), jax.ShapeDtypeStruct((n, D_MODEL), BF16),
                   jax.ShapeDtypeStruct((8, D_MODEL), F32)],
        compiler_params=_cp(("arbitrary",)),
    )(dyn, y, u, z, dskip_e, gn)


def _pool_consts(n, transpose):
    tb = POOL_TB
    rows = n // GRID_W
    t = jnp.arange(tb)
    s = jnp.arange(3 * tb)
    rl, cl = t // GRID_W, t % GRID_W
    rs_, cs_ = s // GRID_W - tb // GRID_W, s % GRID_W
    s2 = jnp.arange(tb)
    rl2, cl2 = s2 // GRID_W, s2 % GRID_W
    tok = jnp.arange(n)
    r_tok, c_tok = tok // GRID_W, tok % GRID_W
    brow, bcol, inv = [], [], []
    for w in POOL_WINDOWS:
        lo, hi = -(w // 2), w - w // 2
        cnt_r = jnp.minimum(r_tok + hi, rows) - jnp.maximum(r_tok + lo, 0)
        cnt_c = jnp.minimum(c_tok + hi, GRID_W) - jnp.maximum(c_tok + lo, 0)
        inv.append(1.0 / (cnt_r * cnt_c).astype(F32))
        if transpose:
            lo, hi = -hi + 1, -lo + 1
        dr = rs_[None, :] - rl[:, None]
        brow.append(((cs_[None, :] == cl[:, None]) & (dr >= lo) & (dr < hi)).astype(BF16))
        dc = cl2[None, :] - cl[:, None]
        bcol.append(((rl2[None, :] == rl[:, None]) & (dc >= lo) & (dc < hi)).astype(BF16))
    inv = jnp.stack(inv, axis=1)
    inv = jnp.concatenate([inv, jnp.zeros((n, 128 - N_POOL), F32)], axis=1)
    return jnp.stack(brow), jnp.stack(bcol), inv


def _pool_box(prev_ref, cur_ref, next_ref, brow_ref, bcol_ref, g, i, nb):
    sl = slice(g * POOL_DIM, (g + 1) * POOL_DIM)
    pv = prev_ref[:, sl] * (i > 0).astype(prev_ref.dtype)
    nx = next_ref[:, sl] * (i < nb - 1).astype(next_ref.dtype)
    stack = jnp.concatenate([pv.astype(BF16), cur_ref[:, sl].astype(BF16), nx.astype(BF16)], axis=0)
    r = _dot(brow_ref[g], stack)
    return _dot(bcol_ref[g], r.astype(BF16))


def _pool_halo_specs(n, d):
    tb = POOL_TB
    nb = n // tb
    prev = pl.BlockSpec((tb, d), lambda i: (jnp.maximum(i - 1, 0), 0))
    cur = pl.BlockSpec((tb, d), lambda i: (i, 0))
    nxt = pl.BlockSpec((tb, d), lambda i: (jnp.minimum(i + 1, nb - 1), 0))
    return prev, cur, nxt


def _pool_const_specs():
    tb = POOL_TB
    return [pl.BlockSpec((N_POOL, tb, 3 * tb), lambda i: (0, 0, 0)),
            pl.BlockSpec((N_POOL, tb, tb), lambda i: (0, 0, 0)),
            pl.BlockSpec((tb, 128), lambda i: (i, 0))]


def _pool_fwd(up, consts, pw_bf, pscale):
    n = up.shape[0]
    tb = POOL_TB
    nb = n // tb
    brow, bcol, inv = consts
    prev, cur, nxt = _pool_halo_specs(n, D_MODEL)

    def body(p_ref, c_ref, n_ref, brow_ref, bcol_ref, inv_ref, pw_ref, sc_ref, o_ref, d_ref):
        i = pl.program_id(0)
        for g in range(N_POOL):
            sl = slice(g * POOL_DIM, (g + 1) * POOL_DIM)
            box = _pool_box(p_ref, c_ref, n_ref, brow_ref, bcol_ref, g, i, nb)
            dd = (box * inv_ref[:, g:g + 1] - c_ref[:, sl]).astype(BF16)
            d_ref[:, sl] = dd
            o_ref[:, sl] = (_dot(dd, pw_ref[g]) * sc_ref[:, sl]).astype(BF16)

    return pl.pallas_call(
        body, name="pool_fwd", grid=(nb,),
        in_specs=[prev, cur, nxt] + _pool_const_specs() +
                 [pl.BlockSpec((N_POOL, POOL_DIM, POOL_DIM), lambda i: (0, 0, 0)), _par_spec(D_MODEL)],
        out_specs=[_row_spec(tb, D_MODEL), _row_spec(tb, D_MODEL)],
        out_shape=[jax.ShapeDtypeStruct((n, D_MODEL), BF16), jax.ShapeDtypeStruct((n, D_MODEL), BF16)],
        compiler_params=_cp(("parallel",)),
    )(up, up, up, brow, bcol, inv, pw_bf, pscale)


def _pool_bwd_a(dp, dsave, inv, pw_bf, pwt_bf, pscale):
    n = dp.shape[0]
    tb = POOL_TB

    def body(dp_ref, d_ref, inv_ref, pw_ref, pwt_ref, sc_ref, dd_ref, dds_ref, gw_ref, gs_ref):
        i = pl.program_id(0)

        @pl.when(i == 0)
        def _():
            gw_ref[...] = jnp.zeros_like(gw_ref)
            gs_ref[...] = jnp.zeros_like(gs_ref)

        for g in range(N_POOL):
            sl = slice(g * POOL_DIM, (g + 1) * POOL_DIM)
            dpv = dp_ref[:, sl]
            dv = d_ref[:, sl]
            dpw_bf = (dpv * sc_ref[:, sl]).astype(BF16)
            dd = _dot(dpw_bf, pwt_ref[g])
            dd_ref[:, sl] = dd
            dds_ref[:, sl] = (dd * inv_ref[:, g:g + 1]).astype(BF16)
            gw_ref[g] += _dot_tn(dv, dpw_bf)
            gs_ref[0:1, sl] += jnp.sum(dpv * _dot(dv, pw_ref[g]), axis=0, keepdims=True)

    wspec = pl.BlockSpec((N_POOL, POOL_DIM, POOL_DIM), lambda i: (0, 0, 0))
    return pl.pallas_call(
        body, name="pool_bwd_a", grid=(n // tb,),
        in_specs=[_row_spec(tb, D_MODEL), _row_spec(tb, D_MODEL), pl.BlockSpec((tb, 128), lambda i: (i, 0)),
                  wspec, wspec, _par_spec(D_MODEL)],
        out_specs=[_row_spec(tb, D_MODEL), _row_spec(tb, D_MODEL), wspec, _acc_spec(D_MODEL)],
        out_shape=[jax.ShapeDtypeStruct((n, D_MODEL), F32), jax.ShapeDtypeStruct((n, D_MODEL), BF16),
                   jax.ShapeDtypeStruct((N_POOL, POOL_DIM, POOL_DIM), F32), jax.ShapeDtypeStruct((8, D_MODEL), F32)],
        compiler_params=_cp(("arbitrary",)),
    )(dp, dsave, inv, pw_bf, pwt_bf, pscale)


def _pool_bwd_b(dds, dd, consts_t):
    n = dd.shape[0]
    tb = POOL_TB
    nb = n // tb
    brow, bcol, inv = consts_t
    prev, cur, nxt = _pool_halo_specs(n, D_MODEL)

    def body(p_ref, c_ref, n_ref, brow_ref, bcol_ref, inv_ref, dd_ref, o_ref):
        i = pl.program_id(0)
        for g in range(N_POOL):
            sl = slice(g * POOL_DIM, (g + 1) * POOL_DIM)
            box = _pool_box(p_ref, c_ref, n_ref, brow_ref, bcol_ref, g, i, nb)
            o_ref[:, sl] = (box - dd_ref[:, sl]).astype(BF16)

    return pl.pallas_call(
        body, name="pool_bwd_b", grid=(nb,),
        in_specs=[prev, cur, nxt] + _pool_const_specs() + [_row_spec(tb, D_MODEL)],
        out_specs=_row_spec(tb, D_MODEL),
        out_shape=jax.ShapeDtypeStruct((n, D_MODEL), BF16),
        compiler_params=_cp(("parallel",)),
    )(dds, dds, dds, brow, bcol, inv, dd)


def _add2(a, b, name):
    r, cdim = a.shape
    tb = _row_block(r, 512)

    def body(a_ref, b_ref, o_ref):
        o_ref[...] = a_ref[...] + b_ref[...]

    return pl.pallas_call(
        body, name=name, grid=(r // tb,),
        in_specs=[_row_spec(tb, cdim)] * 2, out_specs=_row_spec(tb, cdim),
        out_shape=jax.ShapeDtypeStruct((r, cdim), F32), compiler_params=_cp(("parallel",)),
    )(a, b)


def _sum4(parts, name):
    _, r, cdim = parts.shape
    tb = _row_block(r, 512)

    def body(p_ref, o_ref):
        o_ref[...] = ((p_ref[0] + p_ref[1]) + p_ref[2]) + p_ref[3]

    return pl.pallas_call(
        body, name=name, grid=(r // tb,),
        in_specs=[pl.BlockSpec((4, tb, cdim), lambda i: (0, i, 0))], out_specs=_row_spec(tb, cdim),
        out_shape=jax.ShapeDtypeStruct((r, cdim), F32), compiler_params=_cp(("parallel",)),
    )(parts)


def _adamw(w, g, m, v, name):
    r, cdim = w.shape
    tb = _row_block(r, 256)
    c1 = 1.0 - ADAM_B1 ** ADAM_STEP
    c2 = 1.0 - ADAM_B2 ** ADAM_STEP

    def body(w_ref, g_ref, m_ref, v_ref, d_ref, nm_ref, nv_ref):
        gv = g_ref[...]
        nm = ADAM_B1 * m_ref[...] + (1.0 - ADAM_B1) * gv
        nv = ADAM_B2 * v_ref[...] + (1.0 - ADAM_B2) * (gv * gv)
        m_hat = nm / c1
        v_hat = nv / c2
        d_ref[...] = -ADAM_LR * (m_hat / (jnp.sqrt(v_hat) + ADAM_EPS) + ADAM_WD * w_ref[...])
        nm_ref[...] = nm
        nv_ref[...] = nv

    spec = _row_spec(tb, cdim)
    shp = jax.ShapeDtypeStruct((r, cdim), F32)
    return pl.pallas_call(
        body, name=name, grid=(r // tb,),
        in_specs=[spec] * 4, out_specs=[spec] * 3, out_shape=[shp] * 3,
        compiler_params=_cp(("parallel",)),
    )(w, g, m, v)


def _mesh_pos():
    return lax.axis_index("x"), lax.axis_index("y"), lax.axis_index("c")


_ANY = pl.BlockSpec(memory_space=pl.ANY)


def _chip_exchange(src, bcast, name):
    shp = src.shape if bcast else src.shape[1:]

    def body(src_ref, out_ref, send_sems, recv_sems, local_sem):
        x, y, c = _mesh_pos()
        me = 2 * x + y
        chips = [(1 - x, y), (x, 1 - y), (1 - x, 1 - y)]

        def slab(j):
            return src_ref if bcast else src_ref.at[j]

        def copy(k, chip, slot):
            px, py = chip
            return pltpu.make_async_remote_copy(
                src_ref=slab(2 * px + py), dst_ref=out_ref.at[slot],
                send_sem=send_sems.at[k], recv_sem=recv_sems.at[k],
                device_id=(px, py, c), device_id_type=MESH)

        mine = pltpu.make_async_copy(slab(me), out_ref.at[me], local_sem)
        mine.start()
        sends = [copy(k, chip, me) for k, chip in enumerate(chips)]
        for cp in sends:
            cp.start()
        for k, (px, py) in enumerate(chips):
            copy(k, (px, py), 2 * px + py).wait_recv()
        for cp in sends:
            cp.wait_send()
        mine.wait()

    return pl.pallas_call(
        body, name=name, in_specs=[_ANY], out_specs=_ANY,
        out_shape=jax.ShapeDtypeStruct((4,) + tuple(shp), src.dtype),
        scratch_shapes=[pltpu.SemaphoreType.DMA((3,)), pltpu.SemaphoreType.DMA((3,)), pltpu.SemaphoreType.DMA],
    )(src)


def _sibling_swap(src, name):
    def body(src_ref, out_ref, send_sem, recv_sem):
        x, y, c = _mesh_pos()
        cp = pltpu.make_async_remote_copy(src_ref=src_ref, dst_ref=out_ref, send_sem=send_sem, recv_sem=recv_sem,
                                          device_id=(x, y, 1 - c), device_id_type=MESH)
        cp.start()
        cp.wait()

    return pl.pallas_call(
        body, name=name, in_specs=[_ANY], out_specs=_ANY,
        out_shape=jax.ShapeDtypeStruct(src.shape, src.dtype),
        scratch_shapes=[pltpu.SemaphoreType.DMA, pltpu.SemaphoreType.DMA],
    )(src)


def _allreduce_small(v, name):
    r, cdim = v.shape

    def body(v_ref, out_ref, buf, send_sems, recv_sems):
        x, y, c = _mesh_pos()
        me = 4 * x + 2 * y + c
        buf[me] = v_ref[...]
        rel = [(bx, by, bc) for bx in (0, 1) for by in (0, 1) for bc in (0, 1)][1:]

        def peer(b):
            bx, by, bc = b
            return ((1 - x) if bx else x, (1 - y) if by else y, (1 - c) if bc else c)

        def copy(k, slot):
            return pltpu.make_async_remote_copy(
                src_ref=v_ref, dst_ref=buf.at[slot], send_sem=send_sems.at[k], recv_sem=recv_sems.at[k],
                device_id=peer(rel[k]), device_id_type=MESH)

        sends = [copy(k, me) for k in range(7)]
        for cp in sends:
            cp.start()
        for k in range(7):
            px, py, pc = peer(rel[k])
            copy(k, 4 * px + 2 * py + pc).wait_recv()
        for cp in sends:
            cp.wait_send()
        acc = buf[0]
        for j in range(1, 8):
            acc = acc + buf[j]
        out_ref[...] = acc

    vm = pl.BlockSpec(memory_space=pltpu.VMEM)
    return pl.pallas_call(
        body, name=name, in_specs=[vm], out_specs=vm,
        out_shape=jax.ShapeDtypeStruct((r, cdim), F32),
        scratch_shapes=[pltpu.VMEM((8, r, cdim), F32), pltpu.SemaphoreType.DMA((7,)), pltpu.SemaphoreType.DMA((7,))],
    )(v)


_BIG = (("in_proj", (D_MODEL, D_IN_PROJ // 4), 1), ("w_out", (2 * D_MODEL // 4, D_MODEL), 0),
        ("w_gate", (D_MODEL, D_FF // 4), 1), ("w_up", (D_MODEL, D_FF // 4), 1), ("w_down", (D_FF // 4, D_MODEL), 0),
        ("pool_w", (N_POOL, POOL_DIM // 4, POOL_DIM), 1), ("w_ada", (D_MODEL, 6 * D_MODEL // 4), 1))
_PACK_COLS = 1024
_TILE_ROWS = 16
_CONVW_ROWS = 16


def _part_rows(shp):
    rows = -(-math.prod(shp) // _PACK_COLS)
    return -(-rows // _TILE_ROWS) * _TILE_ROWS


_PACK_ROWS = sum(_part_rows(shp) for _, shp, _ in _BIG) + _CONVW_ROWS
_HALF_ROWS = _PACK_ROWS // 2
assert _HALF_ROWS % _TILE_ROWS == 0


def _pad_rows(a, rows):
    if a.shape[0] == rows:
        return a
    return jnp.concatenate([a, jnp.zeros((rows - a.shape[0], a.shape[1]), a.dtype)], axis=0)


def _pack(shards, dtype, tail=None):
    parts = [_pad_rows(shards[name].astype(dtype).reshape(-1, _PACK_COLS), _part_rows(shp)) for name, shp, _ in _BIG]
    parts.append(jnp.zeros((_CONVW_ROWS, _PACK_COLS), dtype) if tail is None else tail)
    return jnp.concatenate(parts, axis=0)


def _unpack_full(packed4):
    out, off = {}, 0
    for name, shp, axis in _BIG:
        rows = math.prod(shp) // _PACK_COLS
        seg = packed4[:, off:off + rows, :].reshape((4,) + shp)
        out[name] = jnp.concatenate([seg[j] for j in range(4)], axis=axis)
        off += _part_rows(shp)
    return out, off


def _unpack_shard(packed):
    out, off = {}, 0
    for name, shp, _ in _BIG:
        rows = math.prod(shp) // _PACK_COLS
        out[name] = packed[off:off + rows, :].reshape(shp)
        off += _part_rows(shp)
    return out


def _split_shards(full):
    slabs = []
    for j in range(4):
        sh = {}
        for name, shp, axis in _BIG:
            sh[name] = lax.slice_in_dim(full[name], j * shp[axis], (j + 1) * shp[axis], axis=axis)
        slabs.append(_pack(sh, F32))
    return jnp.stack(slabs)


def _gather_weights(shards, conv_w_shard, ci):
    bits = lax.bitcast_convert_type(conv_w_shard.reshape(-1), BF16).reshape(-1)
    tail = jnp.concatenate([bits, jnp.zeros((_CONVW_ROWS * _PACK_COLS - bits.shape[0],), BF16)]).reshape(_CONVW_ROWS, _PACK_COLS)
    packed = _pack(shards, BF16, tail)
    half = lax.dynamic_slice_in_dim(packed, ci * _HALF_ROWS, _HALF_ROWS, axis=0)
    mine = _chip_exchange(half, True, "gather_w_ici")
    theirs = _sibling_swap(mine, "gather_w_d2d")
    full = jnp.zeros((4, _PACK_ROWS, _PACK_COLS), BF16)
    full = lax.dynamic_update_slice_in_dim(full, mine, ci * _HALF_ROWS, axis=1)
    full = lax.dynamic_update_slice_in_dim(full, theirs, (1 - ci) * _HALF_ROWS, axis=1)
    weights, off = _unpack_full(full)
    nconv = conv_w_shard.size
    cw = full[:, off:off + _CONVW_ROWS, :].reshape(4, -1)[:, :2 * nconv].reshape(4, nconv, 2)
    cw = lax.bitcast_convert_type(cw, F32).reshape((4,) + conv_w_shard.shape)
    conv_w = jnp.concatenate([cw[j] for j in range(4)], axis=-1)
    return weights, conv_w


def _reduce_grads(full_grads, ci):
    slabs = _split_shards(full_grads)
    theirs = lax.dynamic_slice_in_dim(slabs, (1 - ci) * _HALF_ROWS, _HALF_ROWS, axis=1)
    ours = lax.dynamic_slice_in_dim(slabs, ci * _HALF_ROWS, _HALF_ROWS, axis=1)
    got = _sibling_swap(theirs, "reduce_g_d2d")
    pair = _add2(ours.reshape(-1, _PACK_COLS), got.reshape(-1, _PACK_COLS), "reduce_g_pair")
    pair = pair.reshape(4, _HALF_ROWS, _PACK_COLS)
    parts = _chip_exchange(pair, False, "reduce_g_ici")
    total = _sum4(parts, "reduce_g_sum")
    other = _sibling_swap(total, "reduce_g_share")
    packed = jnp.zeros((_PACK_ROWS, _PACK_COLS), F32)
    packed = lax.dynamic_update_slice_in_dim(packed, total, ci * _HALF_ROWS, axis=0)
    packed = lax.dynamic_update_slice_in_dim(packed, other, (1 - ci) * _HALF_ROWS, axis=0)
    return _unpack_shard(packed)


def _pad_cols(w, n):
    return jnp.concatenate([w, jnp.zeros((w.shape[0], n - w.shape[1]), w.dtype)], axis=1)


def _device_step(x, c, ctx, target, wts, conv_w, small, tb):
    n = x.shape[0]
    d = D_MODEL
    c_ctx = small["c_ctx"]

    win = wts["in_proj"]
    wz, wxd, wup = win[:, 0:d], _pad_cols(win[:, d:d + D_XBC + 2 * SSD_HEADS], D_XD), win[:, d + D_XBC + 2 * SSD_HEADS:]
    wout = wts["w_out"]
    wg, wu, wd = wts["w_gate"], wts["w_up"], wts["w_down"]
    pw = wts["pool_w"]
    wada = wts["w_ada"]

    emb_g, emb_b = _vec(small["emb_ln_g"]), _vec(small["emb_ln_b"])
    ln1_g, ln1_b = _vec(small["ln1_g"]), _vec(small["ln1_b"])
    ln2_g, ln2_b = _vec(small["ln2_g"]), _vec(small["ln2_b"])
    gn = _vec(small["ssd_norm_g"])
    pscale = _vec(small["pool_scale"])
    conv_b = _vec(small["conv_b"])
    dskip_e = jnp.repeat(small["d_skip"].reshape(-1), HEAD_DIM).reshape(1, d)
    w8 = jnp.concatenate([conv_w, jnp.zeros((8 - D_CONV, D_XBC), F32)], axis=0)
    zpad = jnp.zeros((2, 1, 128 - SSD_HEADS), F32)
    bias2 = jnp.concatenate([small["dt_bias"].reshape(2, 1, SSD_HEADS), zpad], axis=2)
    a2 = jnp.concatenate([-jnp.exp(small["a_log"].reshape(2, 1, SSD_HEADS)), zpad], axis=2)
    rexp = (jnp.arange(128)[:, None] == (jnp.arange(d)[None, :] // HEAD_DIM)).astype(BF16)
    rexp_t = rexp.T

    c8 = jnp.concatenate([c.reshape(1, d), c_ctx.reshape(1, d), jnp.zeros((6, d), F32)], axis=0)
    mods = _mods_fwd(c8, wada, _vec(small["b_ada"]))
    sh1, sc1, g1, sh2, sc2, g2 = [mods[0:1, i * d:(i + 1) * d] for i in range(6)]
    sh1c, sc1c = mods[1:2, 0:d], mods[1:2, d:2 * d]

    tbc = min(tb, ctx.shape[0])
    xc0, hc = _ln_mod(ctx, emb_g, emb_b, sh1c, sc1c, tbc, "ln_mod_ctx")
    xdc = _matmul_nn([(hc, wxd)], F32, 512, D_XD, "in_proj_ctx")
    uc = _conv_fwd(xdc, w8, conv_b, tbc, "conv_fwd_ctx")
    hzero = jnp.zeros((2, D_STATE, d), F32)
    _, hprev_c, hfin_c = _ssd_fwd(uc, xdc, bias2, a2, rexp, hzero, "ssd_fwd_ctx")

    x0, h1 = _ln_mod(x, emb_g, emb_b, sh1, sc1, tb, "ln_mod")
    z = _matmul_nn([(h1, wz)], F32, 512, 1024, "in_proj_z")
    xd = _matmul_nn([(h1, wxd)], F32, 512, D_XD, "in_proj_xd")
    up = _matmul_nn([(h1, wup)], F32, 512, 1024, "in_proj_up")
    u = _conv_fwd(xd, w8, conv_b, tb, "conv_fwd")
    y, hprev, _ = _ssd_fwd(u, xd, bias2, a2, rexp, hfin_c, "ssd_fwd")
    yn = _merge_fwd(y, u, z, dskip_e, gn, tb)
    pconst = _pool_consts(n, False)
    pool, dsave = _pool_fwd(up, pconst, pw, pscale)
    mix = _matmul_nn([(yn, wout[0:d]), (pool, wout[d:2 * d])], F32, 512, 1024, "out_proj")
    x1, h2 = _res_ln(x0, mix, g1, ln1_g, ln1_b, sh2, sc2, tb)

    gate, upp, hmid = _swiglu_fwd(h2, wg, wu, 512, D_FF // 2)
    ffn = _matmul_nn([(hmid, wd)], F32, 512, 1024, "ffn_down")
    dffn, dr2, acc2 = _final_ln_loss(x1, ffn, g2, ln2_g, ln2_b, target, tb)
    loss = (0.5 / d) * jnp.sum(acc2[3])

    dgate, dupp = _swiglu_bwd(dffn, wd.T, gate, upp, 512, D_FF // 2)
    g_wdown = _matmul_tn(hmid, dffn, 512, 1024, "g_w_down")
    g_wgate = _matmul_tn(h2, dgate, 512, 1408, "g_w_gate")
    g_wup = _matmul_tn(h2, dupp, 512, 1408, "g_w_up")
    dh2 = _matmul_nn([(dgate, wg.T), (dupp, wu.T)], F32, 512, 1024, "d_h2")
    dmix, dr1, acc1 = _bwd_ln1(dr2, dh2, x1, x0, mix, g1, sc2, ln1_g, tb)

    dyn = _matmul_nn([(dmix, wout[0:d].T)], F32, 512, 1024, "d_yn")
    dpool = _matmul_nn([(dmix, wout[d:2 * d].T)], F32, 512, 1024, "d_pool")
    g_wout = jnp.concatenate([_matmul_tn(yn, dmix, 512, 1024, "g_w_out_a"),
                              _matmul_tn(pool, dmix, 512, 1024, "g_w_out_b")], axis=0)
    dd, dds, g_pw, accp = _pool_bwd_a(dpool, dsave, pconst[2], pw, jnp.swapaxes(pw, 1, 2), pscale)
    dup = _pool_bwd_b(dds, dd, _pool_consts(n, True))
    dy, dz, accm = _merge_bwd(dyn, y, u, z, dskip_e, gn, tb)
    lam0 = jnp.zeros((2, D_STATE, d), F32)
    dxs, dbc, ddt, accs, lam_c = _ssd_bwd(u, xd, bias2, a2, rexp, rexp_t, dy, hprev, lam0, "ssd_bwd")
    du, accb = _conv_bwd_a(dxs, dy, dskip_e, dbc, u, tb, "conv_bwd_a")
    dxd, accw = _conv_bwd_b(du, xd, ddt, w8, tb, "conv_bwd_b")

    lc = ctx.shape[0]
    zeros_c = jnp.zeros((lc, d), F32)
    dxs_c, dbc_c, ddt_c, accs_c, _ = _ssd_bwd(uc, xdc, bias2, a2, rexp, rexp_t, zeros_c, hprev_c, lam_c, "ssd_bwd_ctx")
    du_c, accb_c = _conv_bwd_a(dxs_c, zeros_c, dskip_e, dbc_c, uc, tbc, "conv_bwd_a_ctx")
    dxd_c, accw_c = _conv_bwd_b(du_c, xdc, ddt_c, w8, tbc, "conv_bwd_b_ctx")
    dhc = _matmul_nn([(dxd_c, wxd.T)], F32, 512, 1024, "d_hc")
    _, acc0c = _bwd_ln0(None, dhc, ctx, emb_g, emb_b, sc1c, tbc, "bwd_ln0_ctx")

    dh1 = _matmul_nn([(dz, wz.T), (dxd, wxd.T), (dup, wup.T)], F32, 512, 1024, "d_h1")
    g_wz = _matmul_tn(h1, dz, 512, 1024, "g_in_proj_z")
    g_wxd = _matmul_tn(h1, dxd, 512, D_XD, "g_in_proj_xd") + _matmul_tn(hc, dxd_c, 512, D_XD, "g_in_proj_xd_ctx")
    g_wpo = _matmul_tn(h1, dup, 512, 1024, "g_in_proj_up")
    g_win = jnp.concatenate([g_wz, g_wxd[:, 0:D_XBC + 2 * SSD_HEADS], g_wpo], axis=1)
    grad_x, acc0 = _bwd_ln0(dr1, dh1, x, emb_g, emb_b, sc1, tb, "bwd_ln0")

    zero_d = jnp.zeros((1, d), F32)
    dmod = jnp.concatenate([acc0[1:2], acc0[0:1], acc1[4:5], acc1[1:2], acc1[0:1], acc2[2:3]], axis=1)
    dmodc = jnp.concatenate([acc0c[1:2], acc0c[0:1]] + [zero_d] * 4, axis=1)
    dm8 = jnp.concatenate([dmod, dmodc, jnp.zeros((6, 6 * d), F32)], axis=0)
    g_wada, g_bada8 = _mods_bwd_w(c8.T, dm8)
    g_cctx8 = _mods_bwd_c(dm8, wada, c8)

    big = dict(in_proj=g_win, w_out=g_wout, w_gate=g_wgate, w_up=g_wup, w_down=g_wdown, pool_w=g_pw, w_ada=g_wada)
    sml = dict(
        c_ctx=g_cctx8[1], emb_ln_g=acc0[2] + acc0c[2], emb_ln_b=acc0[3] + acc0c[3], b_ada=g_bada8[0],
        conv_w=accw[0:D_CONV] + accw_c[0:D_CONV], conv_b=accb[0] + accb_c[0],
        dt_bias=accs[:, 0, 0:SSD_HEADS] + accs_c[:, 0, 0:SSD_HEADS],
        a_log=accs[:, 1, 0:SSD_HEADS] + accs_c[:, 1, 0:SSD_HEADS],
        d_skip=jnp.sum(accm[1].reshape(SSD_HEADS, HEAD_DIM), axis=1),
        ssd_norm_g=accm[0], pool_scale=accp[0], ln1_g=acc1[2], ln1_b=acc1[3], ln2_g=acc2[0], ln2_b=acc2[1])
    return loss, grad_x, big, sml


_SMALL = ("c_ctx", "emb_ln_g", "emb_ln_b", "b_ada", "conv_w", "conv_b", "dt_bias", "a_log", "d_skip",
          "ssd_norm_g", "pool_scale", "ln1_g", "ln1_b", "ln2_g", "ln2_b")


_SMALL_ROWS = 8


def _pack_small(vals, names):
    rows = []
    for nme in names:
        flat = vals[nme].reshape(-1).astype(F32)
        assert flat.shape[0] <= _SMALL_ROWS * 1024
        rows.append(jnp.concatenate([flat, jnp.zeros((_SMALL_ROWS * 1024 - flat.shape[0],), F32)]).reshape(_SMALL_ROWS, 1024))
    return jnp.concatenate(rows, axis=0)


def _unpack_small(packed, shapes, names):
    out = {}
    for i, nme in enumerate(names):
        size = math.prod(shapes[nme])
        out[nme] = packed[i * _SMALL_ROWS:(i + 1) * _SMALL_ROWS].reshape(-1)[:size].reshape(shapes[nme])
    return out


_WEIGHT_ORDER = ("c_ctx", "emb_ln_g", "emb_ln_b", "w_ada", "b_ada", "in_proj", "conv_w", "conv_b", "dt_bias", "a_log",
                 "d_skip", "ssd_norm_g", "pool_w", "pool_scale", "w_out", "ln1_g", "ln1_b", "w_gate", "w_up", "w_down",
                 "ln2_g", "ln2_b")


def _as2d(a):
    return a.reshape(-1, a.shape[-1])


def kernel(x, c, ctx, c_ctx, emb_ln_g, emb_ln_b, w_ada, b_ada, in_proj, conv_w, conv_b, dt_bias, a_log, d_skip, ssd_norm_g, pool_w, pool_scale, w_out, ln1_g, ln1_b, w_gate, w_up, w_down, ln2_g, ln2_b, loss_target, m_c_ctx, m_emb_ln_g, m_emb_ln_b, m_w_ada, m_b_ada, m_in_proj, m_conv_w, m_conv_b, m_dt_bias, m_a_log, m_d_skip, m_ssd_norm_g, m_pool_w, m_pool_scale, m_w_out, m_ln1_g, m_ln1_b, m_w_gate, m_w_up, m_w_down, m_ln2_g, m_ln2_b, v_c_ctx, v_emb_ln_g, v_emb_ln_b, v_w_ada, v_b_ada, v_in_proj, v_conv_w, v_conv_b, v_dt_bias, v_a_log, v_d_skip, v_ssd_norm_g, v_pool_w, v_pool_scale, v_w_out, v_ln1_g, v_ln1_b, v_w_gate, v_w_up, v_w_down, v_ln2_g, v_ln2_b):
    w = dict(c_ctx=c_ctx, emb_ln_g=emb_ln_g, emb_ln_b=emb_ln_b, w_ada=w_ada, b_ada=b_ada, in_proj=in_proj, conv_w=conv_w,
             conv_b=conv_b, dt_bias=dt_bias, a_log=a_log, d_skip=d_skip, ssd_norm_g=ssd_norm_g, pool_w=pool_w,
             pool_scale=pool_scale, w_out=w_out, ln1_g=ln1_g, ln1_b=ln1_b, w_gate=w_gate, w_up=w_up, w_down=w_down,
             ln2_g=ln2_g, ln2_b=ln2_b)
    m = dict(c_ctx=m_c_ctx, emb_ln_g=m_emb_ln_g, emb_ln_b=m_emb_ln_b, w_ada=m_w_ada, b_ada=m_b_ada, in_proj=m_in_proj,
             conv_w=m_conv_w, conv_b=m_conv_b, dt_bias=m_dt_bias, a_log=m_a_log, d_skip=m_d_skip,
             ssd_norm_g=m_ssd_norm_g, pool_w=m_pool_w, pool_scale=m_pool_scale, w_out=m_w_out, ln1_g=m_ln1_g,
             ln1_b=m_ln1_b, w_gate=m_w_gate, w_up=m_w_up, w_down=m_w_down, ln2_g=m_ln2_g, ln2_b=m_ln2_b)
    v = dict(c_ctx=v_c_ctx, emb_ln_g=v_emb_ln_g, emb_ln_b=v_emb_ln_b, w_ada=v_w_ada, b_ada=v_b_ada, in_proj=v_in_proj,
             conv_w=v_conv_w, conv_b=v_conv_b, dt_bias=v_dt_bias, a_log=v_a_log, d_skip=v_d_skip,
             ssd_norm_g=v_ssd_norm_g, pool_w=v_pool_w, pool_scale=v_pool_scale, w_out=v_w_out, ln1_g=v_ln1_g,
             ln1_b=v_ln1_b, w_gate=v_w_gate, w_up=v_w_up, w_down=v_w_down, ln2_g=v_ln2_g, ln2_b=v_ln2_b)

    xi, yi, ci = _mesh_pos()
    chip = 2 * xi + yi

    shards = {name: w[name][0] for name, _, _ in _BIG}
    wts, conv_w_full = _gather_weights(shards, conv_w[0], ci)
    small = {nme: (w[nme] if nme in ("c_ctx", "emb_ln_g", "emb_ln_b") else w[nme][0]) for nme in _SMALL if nme != "conv_w"}

    loss, grad_x, big, sml = _device_step(x[0], c, ctx[0], loss_target[0], wts, conv_w_full, small, 512)
    loss = lax.psum(loss, ("x", "y", "c"))

    g_big = _reduce_grads(big, ci)
    small_shapes = {nme: sml[nme].shape for nme in _SMALL}
    g_small = _unpack_small(_allreduce_small(_pack_small(sml, _SMALL), "reduce_small"), small_shapes, _SMALL)
    cw_cols = conv_w.shape[-1]
    g_small["conv_w"] = lax.dynamic_slice_in_dim(g_small["conv_w"], chip * cw_cols, cw_cols, axis=1)

    grads, delta, new_m, new_v = {}, {}, {}, {}
    for name, _, _ in _BIG:
        g2 = _as2d(g_big[name])
        d2, m2, v2 = _adamw(_as2d(w[name][0]), g2, _as2d(m[name][0]), _as2d(v[name][0]), "adamw_" + name)
        grads[name] = g2.reshape(w[name].shape)
        delta[name], new_m[name], new_v[name] = (t.reshape(w[name].shape) for t in (d2, m2, v2))
    shp = {nme: w[nme].shape for nme in _SMALL}
    gp = _pack_small(g_small, _SMALL)
    dp, mp, vp = _adamw(_pack_small(w, _SMALL), gp, _pack_small(m, _SMALL), _pack_small(v, _SMALL), "adamw_small")
    for dst, src in ((grads, gp), (delta, dp), (new_m, mp), (new_v, vp)):
        dst.update(_unpack_small(src, shp, _SMALL))

    return (loss, grad_x[None], *[grads[nme] for nme in _WEIGHT_ORDER], *[delta[nme] for nme in _WEIGHT_ORDER],
            *[new_m[nme] for nme in _WEIGHT_ORDER], *[new_v[nme] for nme in _WEIGHT_ORDER])
```

```python
import functools
import math

import jax
import jax.numpy as jnp
from jax import lax
from jax.experimental import pallas as pl
from jax.experimental.pallas import tpu as pltpu

F32 = jnp.float32
BF16 = jnp.bfloat16
MESH = pl.DeviceIdType.MESH

D_MODEL = 1024
SSD_HEADS = 16
HEAD_DIM = 64
D_STATE = 128
CHUNK = 128
D_CONV = 5
D_XBC = D_MODEL + 2 * D_STATE
D_XD = 1408
N_POOL = 4
POOL_DIM = 256
POOL_WINDOWS = (2, 4, 8, 16)
GRID_W = 64
D_FF = 2816
D_IN_PROJ = 3360
LN_EPS = 1e-5
ALPHA = 2.0 ** 0.25
POOL_TB = 512

ADAM_LR = 0.001
ADAM_B1 = 0.9
ADAM_B2 = 0.999
ADAM_EPS = 1e-08
ADAM_WD = 0.01
ADAM_STEP = 10

VMEM_LIMIT = 56 * 1024 * 1024


def _cp(sem=None):
    return pltpu.CompilerParams(dimension_semantics=sem, vmem_limit_bytes=VMEM_LIMIT)


def _sigmoid(x):
    return 1.0 / (1.0 + jnp.exp(-x))


def _silu(x):
    return x * _sigmoid(x)


def _dsilu(x):
    s = _sigmoid(x)
    return s * (1.0 + x * (1.0 - s))


def _softplus(x):
    t = jnp.exp(-jnp.abs(x))
    u = 1.0 + t
    log1p = jnp.where(u == 1.0, t, jnp.log(u) * t / (u - 1.0 + (u == 1.0)))
    return jnp.maximum(x, 0.0) + log1p


def _split(x, n):
    parts, r = [], x
    for _ in range(n):
        p = r.astype(BF16)
        parts.append(p)
        r = r - p.astype(F32)
    return parts


def _dot(a, b):
    return jnp.dot(a, b, preferred_element_type=F32)


def _dot_nt(a, b):
    return lax.dot_general(a, b, (((1,), (1,)), ((), ())), preferred_element_type=F32)


def _dot_tn(a, b):
    return lax.dot_general(a, b, (((0,), (0,)), ((), ())), preferred_element_type=F32)


def _dot_sel_l(sel_bf, x, n=3):
    out = None
    for p in _split(x, n):
        t = _dot(sel_bf, p)
        out = t if out is None else out + t
    return out


def _dot_sel_r(x, sel_bf, n=3):
    out = None
    for p in _split(x, n):
        t = _dot(p, sel_bf)
        out = t if out is None else out + t
    return out


def _row_block(n, cap=256, mult=8):
    best = None
    for t in range(mult, min(n, cap) + 1, mult):
        if n % t == 0:
            best = t
    return best if best is not None else n


def _vec(v):
    return v.reshape(1, -1).astype(F32)


def _mods_fwd(c8, wada_bf, b_ada):
    d = c8.shape[1]
    n = wada_bf.shape[1]

    def body(c_ref, w_ref, b_ref, o_ref):
        s = _silu(c_ref[...]).astype(BF16)
        o_ref[...] = _dot(s, w_ref[...]) + b_ref[...]

    return pl.pallas_call(
        body, name="mods_fwd", grid=(n // d,),
        in_specs=[pl.BlockSpec((8, d), lambda j: (0, 0)),
                  pl.BlockSpec((d, d), lambda j: (0, j)),
                  pl.BlockSpec((1, d), lambda j: (0, j))],
        out_specs=pl.BlockSpec((8, d), lambda j: (0, j)),
        out_shape=jax.ShapeDtypeStruct((8, n), F32),
        compiler_params=_cp(("arbitrary",)),
    )(c8, wada_bf, b_ada)


def _mods_bwd_w(ct8, dm8):
    d = ct8.shape[0]
    n = dm8.shape[1]
    tn = 512

    def body(ct_ref, dm_ref, dw_ref, db_ref):
        s = _silu(ct_ref[...])
        dm = dm_ref[...]
        dw_ref[...] = s[:, 0:1] * dm[0:1, :] + s[:, 1:2] * dm[1:2, :]
        db_ref[...] = jnp.broadcast_to(dm[0:1, :] + dm[1:2, :], (8, tn))

    return pl.pallas_call(
        body, name="mods_bwd_w", grid=(n // tn,),
        in_specs=[pl.BlockSpec((d, 8), lambda j: (0, 0)),
                  pl.BlockSpec((8, tn), lambda j: (0, j))],
        out_specs=[pl.BlockSpec((d, tn), lambda j: (0, j)),
                   pl.BlockSpec((8, tn), lambda j: (0, j))],
        out_shape=[jax.ShapeDtypeStruct((d, n), F32), jax.ShapeDtypeStruct((8, n), F32)],
        compiler_params=_cp(("arbitrary",)),
    )(ct8, dm8)


def _mods_bwd_c(dm8, wada_bf, c8):
    d = c8.shape[1]
    n = dm8.shape[1]
    nk = n // d

    def body(dm_ref, w_ref, c_ref, o_ref):
        k = pl.program_id(0)

        @pl.when(k == 0)
        def _():
            o_ref[...] = jnp.zeros_like(o_ref)

        o_ref[...] += _dot_nt(dm_ref[...].astype(BF16), w_ref[...])

        @pl.when(k == nk - 1)
        def _():
            o_ref[...] = o_ref[...] * _dsilu(c_ref[...])

    return pl.pallas_call(
        body, name="mods_bwd_c", grid=(nk,),
        in_specs=[pl.BlockSpec((8, d), lambda k: (0, k)),
                  pl.BlockSpec((d, d), lambda k: (0, k)),
                  pl.BlockSpec((8, d), lambda k: (0, 0))],
        out_specs=pl.BlockSpec((8, d), lambda k: (0, 0)),
        out_shape=jax.ShapeDtypeStruct((8, d), F32),
        compiler_params=_cp(("arbitrary",)),
    )(dm8, wada_bf, c8)


def _ln_stats(x):
    mu = jnp.mean(x, axis=-1, keepdims=True)
    xc = x - mu
    var = jnp.mean(xc * xc, axis=-1, keepdims=True)
    rstd = lax.rsqrt(var + LN_EPS)
    return xc * rstd, rstd


def _ln_bwd(dxhat, xhat, rstd):
    m1 = jnp.mean(dxhat, axis=-1, keepdims=True)
    m2 = jnp.mean(dxhat * xhat, axis=-1, keepdims=True)
    return rstd * (dxhat - m1 - xhat * m2)


def _row_spec(tb, d):
    return pl.BlockSpec((tb, d), lambda i: (i, 0))


def _par_spec(d):
    return pl.BlockSpec((1, d), lambda i: (0, 0))


def _acc_spec(d):
    return pl.BlockSpec((8, d), lambda i: (0, 0))


def _ln_mod(x, g, b, sh, sc, tb, name):
    n, d = x.shape

    def body(x_ref, g_ref, b_ref, sh_ref, sc_ref, x0_ref, h_ref):
        xhat, _ = _ln_stats(x_ref[...])
        x0 = xhat * g_ref[...] + b_ref[...]
        x0_ref[...] = x0
        h_ref[...] = (x0 * (1.0 + sc_ref[...]) + sh_ref[...]).astype(BF16)

    return pl.pallas_call(
        body, name=name, grid=(n // tb,),
        in_specs=[_row_spec(tb, d)] + [_par_spec(d)] * 4,
        out_specs=[_row_spec(tb, d), _row_spec(tb, d)],
        out_shape=[jax.ShapeDtypeStruct((n, d), F32), jax.ShapeDtypeStruct((n, d), BF16)],
        compiler_params=_cp(("parallel",)),
    )(x, g, b, sh, sc)


def _res_ln(xres, mix, gate, g, b, sh, sc, tb):
    n, d = xres.shape

    def body(xr_ref, mix_ref, gate_ref, g_ref, b_ref, sh_ref, sc_ref, x1_ref, h_ref):
        r = ALPHA * xr_ref[...] + gate_ref[...] * mix_ref[...]
        xhat, _ = _ln_stats(r)
        x1 = xhat * g_ref[...] + b_ref[...]
        x1_ref[...] = x1
        h_ref[...] = (x1 * (1.0 + sc_ref[...]) + sh_ref[...]).astype(BF16)

    return pl.pallas_call(
        body, name="res_ln1", grid=(n // tb,),
        in_specs=[_row_spec(tb, d)] * 2 + [_par_spec(d)] * 5,
        out_specs=[_row_spec(tb, d), _row_spec(tb, d)],
        out_shape=[jax.ShapeDtypeStruct((n, d), F32), jax.ShapeDtypeStruct((n, d), BF16)],
        compiler_params=_cp(("parallel",)),
    )(xres, mix, gate, g, b, sh, sc)


def _final_ln_loss(x1, ffn, gate, g, b, target, tb):
    n, d = x1.shape

    def body(x1_ref, ffn_ref, gate_ref, g_ref, b_ref, t_ref, dffn_ref, dr_ref, acc_ref):
        i = pl.program_id(0)

        @pl.when(i == 0)
        def _():
            acc_ref[...] = jnp.zeros_like(acc_ref)

        ffn = ffn_ref[...]
        r = ALPHA * x1_ref[...] + gate_ref[...] * ffn
        xhat, rstd = _ln_stats(r)
        err = xhat * g_ref[...] + b_ref[...] - t_ref[...]
        dx2 = err * (1.0 / d)
        dr = _ln_bwd(dx2 * g_ref[...], xhat, rstd)
        dr_ref[...] = dr
        dffn_ref[...] = (gate_ref[...] * dr).astype(BF16)
        acc_ref[0:1, :] += jnp.sum(dx2 * xhat, axis=0, keepdims=True)
        acc_ref[1:2, :] += jnp.sum(dx2, axis=0, keepdims=True)
        acc_ref[2:3, :] += jnp.sum(dr * ffn, axis=0, keepdims=True)
        acc_ref[3:4, :] += jnp.sum(err * err, axis=0, keepdims=True)

    return pl.pallas_call(
        body, name="final_ln_loss", grid=(n // tb,),
        in_specs=[_row_spec(tb, d)] * 2 + [_par_spec(d)] * 3 + [_row_spec(tb, d)],
        out_specs=[_row_spec(tb, d), _row_spec(tb, d), _acc_spec(d)],
        out_shape=[jax.ShapeDtypeStruct((n, d), BF16), jax.ShapeDtypeStruct((n, d), F32),
                   jax.ShapeDtypeStruct((8, d), F32)],
        compiler_params=_cp(("arbitrary",)),
    )(x1, ffn, gate, g, b, target)


def _bwd_ln1(dr2, dh2, x1, x0, mix, gate, sc2, g, tb):
    n, d = x1.shape

    def body(dr2_ref, dh2_ref, x1_ref, x0_ref, mix_ref, gate_ref, sc_ref, g_ref, dmix_ref, dr1_ref, acc_ref):
        i = pl.program_id(0)

        @pl.when(i == 0)
        def _():
            acc_ref[...] = jnp.zeros_like(acc_ref)

        dh2 = dh2_ref[...]
        mix = mix_ref[...]
        dx1 = ALPHA * dr2_ref[...] + dh2 * (1.0 + sc_ref[...])
        r = ALPHA * x0_ref[...] + gate_ref[...] * mix
        xhat, rstd = _ln_stats(r)
        dr1 = _ln_bwd(dx1 * g_ref[...], xhat, rstd)
        dr1_ref[...] = dr1
        dmix_ref[...] = (gate_ref[...] * dr1).astype(BF16)
        acc_ref[0:1, :] += jnp.sum(dh2 * x1_ref[...], axis=0, keepdims=True)
        acc_ref[1:2, :] += jnp.sum(dh2, axis=0, keepdims=True)
        acc_ref[2:3, :] += jnp.sum(dx1 * xhat, axis=0, keepdims=True)
        acc_ref[3:4, :] += jnp.sum(dx1, axis=0, keepdims=True)
        acc_ref[4:5, :] += jnp.sum(dr1 * mix, axis=0, keepdims=True)

    return pl.pallas_call(
        body, name="bwd_ln1", grid=(n // tb,),
        in_specs=[_row_spec(tb, d)] * 5 + [_par_spec(d)] * 3,
        out_specs=[_row_spec(tb, d), _row_spec(tb, d), _acc_spec(d)],
        out_shape=[jax.ShapeDtypeStruct((n, d), BF16), jax.ShapeDtypeStruct((n, d), F32),
                   jax.ShapeDtypeStruct((8, d), F32)],
        compiler_params=_cp(("arbitrary",)),
    )(dr2, dh2, x1, x0, mix, gate, sc2, g)


def _bwd_ln0(dres, dh, x, g, b, sc, tb, name):
    n, d = x.shape
    has_res = dres is not None

    def body(*refs):
        if has_res:
            dres_ref, dh_ref, x_ref, g_ref, b_ref, sc_ref, dx_ref, acc_ref = refs
        else:
            dh_ref, x_ref, g_ref, b_ref, sc_ref, dx_ref, acc_ref = refs
        i = pl.program_id(0)

        @pl.when(i == 0)
        def _():
            acc_ref[...] = jnp.zeros_like(acc_ref)

        dh = dh_ref[...]
        xhat, rstd = _ln_stats(x_ref[...])
        x0 = xhat * g_ref[...] + b_ref[...]
        dx0 = dh * (1.0 + sc_ref[...])
        if has_res:
            dx0 = dx0 + ALPHA * dres_ref[...]
        dx_ref[...] = _ln_bwd(dx0 * g_ref[...], xhat, rstd)
        acc_ref[0:1, :] += jnp.sum(dh * x0, axis=0, keepdims=True)
        acc_ref[1:2, :] += jnp.sum(dh, axis=0, keepdims=True)
        acc_ref[2:3, :] += jnp.sum(dx0 * xhat, axis=0, keepdims=True)
        acc_ref[3:4, :] += jnp.sum(dx0, axis=0, keepdims=True)

    ins = ([dres] if has_res else []) + [dh, x, g, b, sc]
    return pl.pallas_call(
        body, name=name, grid=(n // tb,),
        in_specs=[_row_spec(tb, d)] * (3 if has_res else 2) + [_par_spec(d)] * 3,
        out_specs=[_row_spec(tb, d), _acc_spec(d)],
        out_shape=[jax.ShapeDtypeStruct((n, d), F32), jax.ShapeDtypeStruct((8, d), F32)],
        compiler_params=_cp(("arbitrary",)),
    )(*ins)


def _matmul_nn(pairs, out_dtype, tm, tn, name):
    m = pairs[0][0].shape[0]
    n = pairs[0][1].shape[1]
    tm = min(tm, m)
    tn = min(tn, n)
    npair = len(pairs)

    def body(*refs):
        o_ref = refs[-1]
        acc = None
        for p in range(npair):
            t = _dot(refs[2 * p][...].astype(BF16), refs[2 * p + 1][...])
            acc = t if acc is None else acc + t
        o_ref[...] = acc.astype(out_dtype)

    in_specs, args = [], []
    for a, b in pairs:
        k = a.shape[1]
        in_specs += [pl.BlockSpec((tm, k), lambda i, j: (i, 0)), pl.BlockSpec((k, tn), lambda i, j: (0, j))]
        args += [a, b]
    return pl.pallas_call(
        body, name=name, grid=(m // tm, n // tn),
        in_specs=in_specs,
        out_specs=pl.BlockSpec((tm, tn), lambda i, j: (i, j)),
        out_shape=jax.ShapeDtypeStruct((m, n), out_dtype),
        compiler_params=_cp(("parallel", "arbitrary")),
    )(*args)


def _matmul_tn(a, g, tm, tn, name):
    m, k = a.shape
    n = g.shape[1]
    tm = min(tm, m)
    tn = min(tn, n)

    def body(a_ref, g_ref, o_ref):
        i = pl.program_id(1)

        @pl.when(i == 0)
        def _():
            o_ref[...] = jnp.zeros_like(o_ref)

        o_ref[...] += _dot_tn(a_ref[...].astype(BF16), g_ref[...].astype(BF16))

    return pl.pallas_call(
        body, name=name, grid=(n // tn, m // tm),
        in_specs=[pl.BlockSpec((tm, k), lambda j, i: (i, 0)), pl.BlockSpec((tm, tn), lambda j, i: (i, j))],
        out_specs=pl.BlockSpec((k, tn), lambda j, i: (0, j)),
        out_shape=jax.ShapeDtypeStruct((k, n), F32),
        compiler_params=_cp(("parallel", "arbitrary")),
    )(a, g)


def _swiglu_fwd(h, wg, wu, tm, tn):
    m, k = h.shape
    n = wg.shape[1]
    tm = min(tm, m)

    def body(h_ref, wg_ref, wu_ref, gate_ref, up_ref, hmid_ref):
        hv = h_ref[...]
        gate = _dot(hv, wg_ref[...])
        up = _dot(hv, wu_ref[...])
        gate_ref[...] = gate
        up_ref[...] = up
        hmid_ref[...] = (_silu(gate) * up).astype(BF16)

    blk = pl.BlockSpec((tm, tn), lambda i, j: (i, j))
    wspec = pl.BlockSpec((k, tn), lambda i, j: (0, j))
    return pl.pallas_call(
        body, name="swiglu_fwd", grid=(m // tm, n // tn),
        in_specs=[pl.BlockSpec((tm, k), lambda i, j: (i, 0)), wspec, wspec],
        out_specs=[blk, blk, blk],
        out_shape=[jax.ShapeDtypeStruct((m, n), F32), jax.ShapeDtypeStruct((m, n), F32),
                   jax.ShapeDtypeStruct((m, n), BF16)],
        compiler_params=_cp(("parallel", "arbitrary")),
    )(h, wg, wu)


def _swiglu_bwd(dffn, wdt, gate, up, tm, tn):
    m, k = dffn.shape
    n = wdt.shape[1]
    tm = min(tm, m)

    def body(d_ref, w_ref, gate_ref, up_ref, dg_ref, du_ref):
        dh = _dot(d_ref[...], w_ref[...])
        gate = gate_ref[...]
        dg_ref[...] = (dh * up_ref[...] * _dsilu(gate)).astype(BF16)
        du_ref[...] = (dh * _silu(gate)).astype(BF16)

    blk = pl.BlockSpec((tm, tn), lambda i, j: (i, j))
    return pl.pallas_call(
        body, name="swiglu_bwd", grid=(m // tm, n // tn),
        in_specs=[pl.BlockSpec((tm, k), lambda i, j: (i, 0)), pl.BlockSpec((k, tn), lambda i, j: (0, j)), blk, blk],
        out_specs=[blk, blk],
        out_shape=[jax.ShapeDtypeStruct((m, n), BF16), jax.ShapeDtypeStruct((m, n), BF16)],
        compiler_params=_cp(("parallel", "arbitrary")),
    )(dffn, wdt, gate, up)


def _halo_specs(tb, width, nrows):
    r8 = tb // 8
    last = nrows // 8 - 1
    prev = pl.BlockSpec((8, width), lambda i: (jnp.maximum(i * r8 - 1, 0), 0))
    nxt = pl.BlockSpec((8, width), lambda i: (jnp.minimum((i + 1) * r8, last), 0))
    return prev, nxt


def _fill_halo(buf, prev_ref, cur_ref, next_ref, tb, i, nb):
    buf[0:8, :] = prev_ref[...] * (i > 0).astype(F32)
    buf[8:8 + tb, :] = cur_ref[...]
    buf[8 + tb:16 + tb, :] = next_ref[...] * (i < nb - 1).astype(F32)


def _conv_fwd(xd, w8, b, tb, name):
    n = xd.shape[0]
    tb = min(tb, n)
    nb = n // tb
    prev, nxt = _halo_specs(tb, D_XBC, n)

    def body(p_ref, c_ref, n_ref, w_ref, b_ref, u_ref, buf):
        i = pl.program_id(0)
        _fill_halo(buf, p_ref, c_ref, n_ref, tb, i, nb)
        acc = jnp.broadcast_to(b_ref[...], (tb, D_XBC))
        for k in range(D_CONV):
            acc = acc + w_ref[k:k + 1, :] * buf[pl.ds(6 + k, tb), :]
        u_ref[...] = acc

    return pl.pallas_call(
        body, name=name, grid=(nb,),
        in_specs=[prev, pl.BlockSpec((tb, D_XBC), lambda i: (i, 0)), nxt,
                  pl.BlockSpec((8, D_XBC), lambda i: (0, 0)), _par_spec(D_XBC)],
        out_specs=_row_spec(tb, D_XBC),
        out_shape=jax.ShapeDtypeStruct((n, D_XBC), F32),
        scratch_shapes=[pltpu.VMEM((tb + 16, D_XBC), F32)],
        compiler_params=_cp(("parallel",)),
    )(xd, xd, xd, w8, b)


def _conv_bwd_a(dxs, dy, dskip_e, dbc, u, tb, name):
    n = u.shape[0]
    tb = min(tb, n)

    def body(dxs_ref, dy_ref, sk_ref, dbc_ref, u_ref, du_ref, acc_ref):
        i = pl.program_id(0)

        @pl.when(i == 0)
        def _():
            acc_ref[...] = jnp.zeros_like(acc_ref)

        uv = u_ref[...]
        ds = _dsilu(uv)
        gx = dxs_ref[0] + dxs_ref[1] + dy_ref[...] * sk_ref[...]
        gbc = dbc_ref[0] + dbc_ref[1]
        du = jnp.concatenate([gx, gbc], axis=1) * ds
        du_ref[...] = du
        acc_ref[0:1, :] += jnp.sum(du, axis=0, keepdims=True)

    return pl.pallas_call(
        body, name=name, grid=(n // tb,),
        in_specs=[pl.BlockSpec((2, tb, D_MODEL), lambda i: (0, i, 0)), _row_spec(tb, D_MODEL), _par_spec(D_MODEL),
                  pl.BlockSpec((2, tb, 2 * D_STATE), lambda i: (0, i, 0)), _row_spec(tb, D_XBC)],
        out_specs=[_row_spec(tb, D_XBC), _acc_spec(D_XBC)],
        out_shape=[jax.ShapeDtypeStruct((n, D_XBC), F32), jax.ShapeDtypeStruct((8, D_XBC), F32)],
        compiler_params=_cp(("arbitrary",)),
    )(dxs, dy, dskip_e, dbc, u)


def _conv_bwd_b(du, xd, ddt, w8, tb, name):
    n = du.shape[0]
    tb = min(tb, n)
    nb = n // tb
    prev, nxt = _halo_specs(tb, D_XBC, n)

    def body(dp_ref, dc_ref, dn_ref, xp_ref, xc_ref, xn_ref, ddt_ref, w_ref, dxd_ref, acc_ref, dbuf, xbuf):
        i = pl.program_id(0)

        @pl.when(i == 0)
        def _():
            acc_ref[...] = jnp.zeros_like(acc_ref)

        _fill_halo(dbuf, dp_ref, dc_ref, dn_ref, tb, i, nb)
        _fill_halo(xbuf, xp_ref, xc_ref, xn_ref, tb, i, nb)
        duc = dc_ref[...]
        acc = jnp.zeros((tb, D_XBC), F32)
        for k in range(D_CONV):
            acc = acc + w_ref[k:k + 1, :] * dbuf[pl.ds(10 - k, tb), :]
            acc_ref[k:k + 1, :] += jnp.sum(duc * xbuf[pl.ds(6 + k, tb), :], axis=0, keepdims=True)
        dxd_ref[:, 0:D_XBC] = acc.astype(BF16)
        ddt = ddt_ref[0] + pltpu.roll(ddt_ref[1], SSD_HEADS, 1)
        dxd_ref[:, D_XBC:D_XD] = ddt.astype(BF16)

    cur = pl.BlockSpec((tb, D_XBC), lambda i: (i, 0))
    return pl.pallas_call(
        body, name=name, grid=(nb,),
        in_specs=[prev, cur, nxt, prev, cur, nxt,
                  pl.BlockSpec((2, tb, 128), lambda i: (0, i, 0)), pl.BlockSpec((8, D_XBC), lambda i: (0, 0))],
        out_specs=[_row_spec(tb, D_XD), _acc_spec(D_XBC)],
        out_shape=[jax.ShapeDtypeStruct((n, D_XD), BF16), jax.ShapeDtypeStruct((8, D_XBC), F32)],
        scratch_shapes=[pltpu.VMEM((tb + 16, D_XBC), F32), pltpu.VMEM((tb + 16, D_XBC), F32)],
        compiler_params=_cp(("arbitrary",)),
    )(du, du, du, xd, xd, xd, ddt, w8)


def _ssd_chunk_index(nc, reverse):
    def idx(d, k):
        kk = (nc - 1 - k) if reverse else k
        return kk + d * (nc - 1 - 2 * kk)
    return idx


def _ssd_prologue(d, u_ref, xd_ref, bias_ref, a_ref, r_ref):
    q = CHUNK
    xbc = _silu(u_ref[...])
    xs = xbc[:, 0:D_MODEL]
    bm = xbc[:, D_MODEL:D_MODEL + D_STATE]
    cm = xbc[:, D_MODEL + D_STATE:D_XBC]
    row = lax.broadcasted_iota(jnp.int32, (q, q), 0)
    col = lax.broadcasted_iota(jnp.int32, (q, q), 1)
    sgn = 1 - 2 * d
    mask = ((row - col) * sgn) >= 0
    mask_t = ((row - col) * sgn) <= 0
    xdv = xd_ref[...]
    dtraw = jnp.where(d == 0, xdv, pltpu.roll(xdv, 128 - SSD_HEADS, 1)) + bias_ref[...]
    head_lane = col < SSD_HEADS
    dt = jnp.where(head_lane, _softplus(dtraw), 0.0)
    a = a_ref[...]
    tri = jnp.where(mask, 1.0, 0.0).astype(BF16)
    acum = _dot_sel_l(tri, dt * a)
    rexp = r_ref[...]
    acum_e = _dot_sel_r(acum, rexp)
    dt_e = _dot_sel_r(dt, rexp)
    alast_e = jnp.where(d == 0, acum_e[q - 1:q, :], acum_e[0:1, :])
    e = jnp.exp(acum_e)
    decay_end = jnp.exp(alast_e - acum_e)
    g = _dot_nt(cm.astype(BF16), bm.astype(BF16))
    return dict(xs=xs, bm=bm, cm=cm, mask=mask, mask_t=mask_t, dtraw=dtraw, head_lane=head_lane, dt=dt, a=a,
                acum=acum, acum_t=acum.T, dt_t=dt.T, dt_e=dt_e, alast_e=alast_e, e=e, decay_end=decay_end, g=g,
                col=col, row=row)


def _ssd_head_mats(p, h):
    seg = p["acum"][:, h:h + 1] - p["acum_t"][h:h + 1, :]
    lm = jnp.exp(jnp.where(p["mask"], seg, -jnp.inf))
    gl = p["g"] * lm
    s = gl * p["dt_t"][h:h + 1, :]
    return lm, gl, s


def _ssd_fwd(u, xd, bias2, a2, rexp, h0, name):
    n = u.shape[0]
    nc = n // CHUNK
    q = CHUNK
    cidx = _ssd_chunk_index(nc, reverse=False)

    def body(u_ref, xd_ref, bias_ref, a_ref, r_ref, h0_ref, y_ref, hp_ref, hf_ref, st):
        d = pl.program_id(0)
        k = pl.program_id(1)

        @pl.when(k == 0)
        def _():
            st[...] = h0_ref[...]

        p = _ssd_prologue(d, u_ref, xd_ref, bias_ref, a_ref, r_ref)
        stv = st[...]
        st_bf = stv.astype(BF16)
        hp_ref[...] = st_bf
        xs = p["xs"]
        lane128 = p["col"]
        y_off = _dot(p["cm"].astype(BF16), st_bf) * p["e"]
        for pb in range(SSD_HEADS // 2):
            _, _, s0 = _ssd_head_mats(p, 2 * pb)
            _, _, s1 = _ssd_head_mats(p, 2 * pb + 1)
            xp = xs[:, pb * 128:(pb + 1) * 128]
            rhs = jnp.concatenate([jnp.where(lane128 < HEAD_DIM, xp, 0.0), jnp.where(lane128 >= HEAD_DIM, xp, 0.0)],
                                  axis=0).astype(BF16)
            lhs = jnp.concatenate([s0, s1], axis=1).astype(BF16)
            y_ref[:, pb * 128:(pb + 1) * 128] = _dot(lhs, rhs) + y_off[:, pb * 128:(pb + 1) * 128]
        xw = (xs * (p["decay_end"] * p["dt_e"])).astype(BF16)
        new = stv * jnp.exp(p["alast_e"]) + _dot(p["bm"].T.astype(BF16), xw)
        st[...] = new
        hf_ref[...] = new

    return pl.pallas_call(
        body, name=name, grid=(2, nc),
        in_specs=[pl.BlockSpec((q, D_XBC), lambda d, k: (cidx(d, k), 0)),
                  pl.BlockSpec((q, 128), lambda d, k: (cidx(d, k), D_XBC // 128)),
                  pl.BlockSpec((None, 1, 128), lambda d, k: (d, 0, 0)),
                  pl.BlockSpec((None, 1, 128), lambda d, k: (d, 0, 0)),
                  pl.BlockSpec((128, D_MODEL), lambda d, k: (0, 0)),
                  pl.BlockSpec((None, D_STATE, D_MODEL), lambda d, k: (d, 0, 0))],
        out_specs=[pl.BlockSpec((None, q, D_MODEL), lambda d, k: (d, cidx(d, k), 0)),
                   pl.BlockSpec((None, None, D_STATE, D_MODEL), lambda d, k: (d, cidx(d, k), 0, 0)),
                   pl.BlockSpec((None, D_STATE, D_MODEL), lambda d, k: (d, 0, 0))],
        out_shape=[jax.ShapeDtypeStruct((2, n, D_MODEL), F32),
                   jax.ShapeDtypeStruct((2, nc, D_STATE, D_MODEL), BF16),
                   jax.ShapeDtypeStruct((2, D_STATE, D_MODEL), F32)],
        scratch_shapes=[pltpu.VMEM((D_STATE, D_MODEL), F32)],
        compiler_params=_cp(("arbitrary", "arbitrary")),
    )(u, xd, bias2, a2, rexp, h0)


def _ssd_bwd(u, xd, bias2, a2, rexp, rexp_t, dy, hprev, lam0, name):
    n = u.shape[0]
    nc = n // CHUNK
    q = CHUNK
    cidx = _ssd_chunk_index(nc, reverse=True)

    def body(u_ref, xd_ref, bias_ref, a_ref, r_ref, rt_ref, dy_ref, hp_ref, lam0_ref,
             dxs_ref, dbc_ref, ddt_ref, acc_ref, lamo_ref, lam):
        d = pl.program_id(0)
        k = pl.program_id(1)

        @pl.when(k == 0)
        def _():
            lam[...] = lam0_ref[...]
            acc_ref[...] = jnp.zeros_like(acc_ref)

        rexp_t = rt_ref[...]

        def hsum(t):
            return _dot_sel_r(t, rexp_t, n=2)

        p = _ssd_prologue(d, u_ref, xd_ref, bias_ref, a_ref, r_ref)
        xs, bm, cm = p["xs"], p["bm"], p["cm"]
        bm_bf, cm_bf = bm.astype(BF16), cm.astype(BF16)
        lamn = lam[...]
        lamn_bf = lamn.astype(BF16)
        stp = hp_ref[...]
        dyv = dy_ref[...]
        lane128 = p["col"]

        wend_e = p["decay_end"] * p["dt_e"]
        cs = _dot(cm_bf, stp)
        dye = dyv * p["e"]
        dye_bf = dye.astype(BF16)
        dc_off = _dot_nt(dye_bf, stp)
        v = _dot(bm_bf, lamn_bf)
        vx = v * xs
        om_e = vx * wend_e
        xw_bf = (xs * wend_e).astype(BF16)
        db_off = _dot_nt(xw_bf, lamn_bf)
        alast_big = jnp.exp(p["alast_e"])
        dlast_e = jnp.sum(stp.astype(F32) * lamn, axis=0, keepdims=True) * alast_big
        lam_new = lamn * alast_big + _dot(cm.T.astype(BF16), dye_bf)
        lam[...] = lam_new
        lamo_ref[...] = lam_new

        x1 = hsum(dye * cs - om_e)
        x2 = hsum(vx * p["decay_end"])
        x3 = hsum(jnp.broadcast_to(jnp.sum(om_e, axis=0, keepdims=True) + dlast_e, (8, D_MODEL)))

        sub16 = lax.broadcasted_iota(jnp.int32, (SSD_HEADS, q), 0)
        rs = jnp.zeros((q, 128), F32)
        cs_m = jnp.zeros((SSD_HEADS, q), F32)
        dt_m = jnp.zeros((SSD_HEADS, q), F32)
        dg = jnp.zeros((q, q), F32)
        for pb in range(SSD_HEADS // 2):
            xp_bf = xs[:, pb * 128:(pb + 1) * 128].astype(BF16)
            dyp = dyv[:, pb * 128:(pb + 1) * 128]
            dxs_pair = None
            for half in range(2):
                h = 2 * pb + half
                sel = (lane128 < HEAD_DIM) if half == 0 else (lane128 >= HEAD_DIM)
                dyh_bf = jnp.where(sel, dyp, 0.0).astype(BF16)
                lm, gl, s = _ssd_head_mats(p, h)
                ds = _dot_nt(dyh_bf, xp_bf)
                t = _dot_tn(s.astype(BF16), dyh_bf)
                dxs_pair = t if dxs_pair is None else dxs_pair + t
                w = ds * s
                rs = rs + jnp.sum(w, axis=1, keepdims=True) * (lane128 == h).astype(F32)
                cs_m = jnp.where(sub16 == h, jnp.sum(w, axis=0, keepdims=True), cs_m)
                dt_m = jnp.where(sub16 == h, jnp.sum(ds * gl, axis=0, keepdims=True), dt_m)
                dg = dg + ds * lm * p["dt_t"][h:h + 1, :]
            sl = slice(pb * 128, (pb + 1) * 128)
            dxs_ref[:, sl] = dxs_pair + v[:, sl] * wend_e[:, sl]

        def to_lanes(m16):
            return jnp.concatenate([m16, jnp.zeros((128 - SSD_HEADS, q), F32)], axis=0).T

        last = jnp.where(d == 0, q - 1, 0)
        dacum = rs - to_lanes(cs_m) + x1 + jnp.where(p["row"] == last, x3[0:1, :], 0.0)
        tri_t = jnp.where(p["mask_t"], 1.0, 0.0).astype(BF16)
        ddta = _dot_sel_l(tri_t, dacum)
        dt = p["dt"]
        a = p["a"]
        ddt = to_lanes(dt_m) + x2 + a * ddta
        ddtraw = jnp.where(p["head_lane"], ddt * _sigmoid(p["dtraw"]), 0.0)
        ddt_ref[...] = ddtraw
        acc_ref[0:1, :] += jnp.sum(ddtraw, axis=0, keepdims=True)
        acc_ref[1:2, :] += jnp.sum(dt * ddta, axis=0, keepdims=True) * a

        dg_bf = dg.astype(BF16)
        dbc_ref[:, 0:D_STATE] = _dot_tn(dg_bf, cm_bf) + db_off
        dbc_ref[:, D_STATE:2 * D_STATE] = _dot(dg_bf, bm_bf) + dc_off

    cblk = lambda d, k: (cidx(d, k), 0)
    return pl.pallas_call(
        body, name=name, grid=(2, nc),
        in_specs=[pl.BlockSpec((q, D_XBC), cblk),
                  pl.BlockSpec((q, 128), lambda d, k: (cidx(d, k), D_XBC // 128)),
                  pl.BlockSpec((None, 1, 128), lambda d, k: (d, 0, 0)),
                  pl.BlockSpec((None, 1, 128), lambda d, k: (d, 0, 0)),
                  pl.BlockSpec((128, D_MODEL), lambda d, k: (0, 0)),
                  pl.BlockSpec((D_MODEL, 128), lambda d, k: (0, 0)),
                  pl.BlockSpec((q, D_MODEL), cblk),
                  pl.BlockSpec((None, None, D_STATE, D_MODEL), lambda d, k: (d, cidx(d, k), 0, 0)),
                  pl.BlockSpec((None, D_STATE, D_MODEL), lambda d, k: (d, 0, 0))],
        out_specs=[pl.BlockSpec((None, q, D_MODEL), lambda d, k: (d, cidx(d, k), 0)),
                   pl.BlockSpec((None, q, 2 * D_STATE), lambda d, k: (d, cidx(d, k), 0)),
                   pl.BlockSpec((None, q, 128), lambda d, k: (d, cidx(d, k), 0)),
                   pl.BlockSpec((None, 8, 128), lambda d, k: (d, 0, 0)),
                   pl.BlockSpec((None, D_STATE, D_MODEL), lambda d, k: (d, 0, 0))],
        out_shape=[jax.ShapeDtypeStruct((2, n, D_MODEL), F32),
                   jax.ShapeDtypeStruct((2, n, 2 * D_STATE), F32),
                   jax.ShapeDtypeStruct((2, n, 128), F32),
                   jax.ShapeDtypeStruct((2, 8, 128), F32),
                   jax.ShapeDtypeStruct((2, D_STATE, D_MODEL), F32)],
        scratch_shapes=[pltpu.VMEM((D_STATE, D_MODEL), F32)],
        compiler_params=_cp(("arbitrary", "arbitrary")),
    )(u, xd, bias2, a2, rexp, rexp_t, dy, hprev, lam0)


def _merge_fwd(y, u, z, dskip_e, gn, tb):
    n = z.shape[0]

    def body(y_ref, u_ref, z_ref, sk_ref, gn_ref, o_ref):
        xs = _silu(u_ref[...])
        ys = y_ref[0] + y_ref[1] + sk_ref[...] * xs
        gated = ys * _silu(z_ref[...])
        rstd = lax.rsqrt(jnp.mean(gated * gated, axis=-1, keepdims=True) + LN_EPS)
        o_ref[...] = (gated * rstd * gn_ref[...]).astype(BF16)

    return pl.pallas_call(
        body, name="merge_fwd", grid=(n // tb,),
        in_specs=[pl.BlockSpec((2, tb, D_MODEL), lambda i: (0, i, 0)), pl.BlockSpec((tb, D_MODEL), lambda i: (i, 0)),
                  _row_spec(tb, D_MODEL), _par_spec(D_MODEL), _par_spec(D_MODEL)],
        out_specs=_row_spec(tb, D_MODEL),
        out_shape=jax.ShapeDtypeStruct((n, D_MODEL), BF16),
        compiler_params=_cp(("parallel",)),
    )(y, u, z, dskip_e, gn)


def _merge_bwd(dyn, y, u, z, dskip_e, gn, tb):
    n = z.shape[0]

    def body(dyn_ref, y_ref, u_ref, z_ref, sk_ref, gn_ref, dy_ref, dz_ref, acc_ref):
        i = pl.program_id(0)

        @pl.when(i == 0)
        def _():
            acc_ref[...] = jnp.zeros_like(acc_ref)

        xs = _silu(u_ref[...])
        zv = z_ref[...]
        ys = y_ref[0] + y_ref[1] + sk_ref[...] * xs
        gated = ys * _silu(zv)
        rstd = lax.rsqrt(jnp.mean(gated * gated, axis=-1, keepdims=True) + LN_EPS)
        ghat = gated * rstd
        dyn_v = dyn_ref[...]
        t = dyn_v * gn_ref[...]
        dgated = rstd * (t - ghat * jnp.mean(t * ghat, axis=-1, keepdims=True))
        dys = dgated * _silu(zv)
        dy_ref[...] = dys
        dz_ref[...] = (dgated * ys * _dsilu(zv)).astype(BF16)
        acc_ref[0:1, :] += jnp.sum(dyn_v * ghat, axis=0, keepdims=True)
        acc_ref[1:2, :] += jnp.sum(dys * xs, axis=0, keepdims=True)

    return pl.pallas_call(
        body, name="merge_bwd", grid=(n // tb,),
        in_specs=[_row_spec(tb, D_MODEL), pl.BlockSpec((2, tb, D_MODEL), lambda i: (0, i, 0)),
                  pl.BlockSpec((tb, D_MODEL), lambda i: (i, 0)), _row_spec(tb, D_MODEL),
                  _par_spec(D_MODEL), _par_spec(D_MODEL)],
        out_specs=[_row_spec(tb, D_MODEL), _row_spec(tb, D_MODEL), _acc_spec(D_MODEL)],
        out_shape=[jax.ShapeDtypeStruct((n, D_MODEL), F32), jax.ShapeDtypeStruct((n, D_MODEL), BF16),
                   jax.ShapeDtypeStruct((8, D_MODEL), F32)],
        compiler_params=_cp(("arbitrary",)),
    )(dyn, y, u, z, dskip_e, gn)


def _pool_consts(transpose):
    tb = POOL_TB
    t = jnp.arange(tb)
    s = jnp.arange(3 * tb)
    rl, cl = t // GRID_W, t % GRID_W
    rs_, cs_ = s // GRID_W - tb // GRID_W, s % GRID_W
    s2 = jnp.arange(tb)
    rl2, cl2 = s2 // GRID_W, s2 % GRID_W
    brow, bcol = [], []
    for w in POOL_WINDOWS:
        lo, hi = -(w // 2), w - w // 2
        if transpose:
            lo, hi = -hi + 1, -lo + 1
        dr = rs_[None, :] - rl[:, None]
        brow.append(((cs_[None, :] == cl[:, None]) & (dr >= lo) & (dr < hi)).astype(BF16))
        dc = cl2[None, :] - cl[:, None]
        bcol.append(((rl2[None, :] == rl[:, None]) & (dc >= lo) & (dc < hi)).astype(BF16))
    return jnp.stack(brow), jnp.stack(bcol)


def _pool_inv(i, g, n):
    assert GRID_W == 64
    t = i * POOL_TB + lax.broadcasted_iota(jnp.int32, (POOL_TB, 1), 0)
    r = lax.shift_right_logical(t, 6)
    col = t & (GRID_W - 1)
    w = POOL_WINDOWS[g]
    lo, hi = -(w // 2), w - w // 2
    cnt_r = jnp.minimum(r + hi, n // GRID_W) - jnp.maximum(r + lo, 0)
    cnt_c = jnp.minimum(col + hi, GRID_W) - jnp.maximum(col + lo, 0)
    return 1.0 / (cnt_r * cnt_c).astype(F32)


def _pool_box(prev_ref, cur_ref, next_ref, brow_ref, bcol_ref, g, i, nb):
    sl = slice(g * POOL_DIM, (g + 1) * POOL_DIM)
    pv = prev_ref[:, sl] * (i > 0).astype(prev_ref.dtype)
    nx = next_ref[:, sl] * (i < nb - 1).astype(next_ref.dtype)
    stack = jnp.concatenate([pv.astype(BF16), cur_ref[:, sl].astype(BF16), nx.astype(BF16)], axis=0)
    r = _dot(brow_ref[g], stack)
    return _dot(bcol_ref[g], r.astype(BF16))


def _pool_halo_specs(n, d):
    tb = POOL_TB
    nb = n // tb
    prev = pl.BlockSpec((tb, d), lambda i: (jnp.maximum(i - 1, 0), 0))
    cur = pl.BlockSpec((tb, d), lambda i: (i, 0))
    nxt = pl.BlockSpec((tb, d), lambda i: (jnp.minimum(i + 1, nb - 1), 0))
    return prev, cur, nxt


def _pool_const_specs():
    tb = POOL_TB
    return [pl.BlockSpec((N_POOL, tb, 3 * tb), lambda i: (0, 0, 0)),
            pl.BlockSpec((N_POOL, tb, tb), lambda i: (0, 0, 0))]


def _pool_fwd(up, consts, pw_bf, pscale):
    n = up.shape[0]
    tb = POOL_TB
    nb = n // tb
    brow, bcol = consts
    prev, cur, nxt = _pool_halo_specs(n, D_MODEL)

    def body(p_ref, c_ref, n_ref, brow_ref, bcol_ref, pw_ref, sc_ref, o_ref, d_ref):
        i = pl.program_id(0)
        for g in range(N_POOL):
            sl = slice(g * POOL_DIM, (g + 1) * POOL_DIM)
            box = _pool_box(p_ref, c_ref, n_ref, brow_ref, bcol_ref, g, i, nb)
            dd = (box * _pool_inv(i, g, n) - c_ref[:, sl]).astype(BF16)
            d_ref[:, sl] = dd
            o_ref[:, sl] = (_dot(dd, pw_ref[g]) * sc_ref[:, sl]).astype(BF16)

    return pl.pallas_call(
        body, name="pool_fwd", grid=(nb,),
        in_specs=[prev, cur, nxt] + _pool_const_specs() +
                 [pl.BlockSpec((N_POOL, POOL_DIM, POOL_DIM), lambda i: (0, 0, 0)), _par_spec(D_MODEL)],
        out_specs=[_row_spec(tb, D_MODEL), _row_spec(tb, D_MODEL)],
        out_shape=[jax.ShapeDtypeStruct((n, D_MODEL), BF16), jax.ShapeDtypeStruct((n, D_MODEL), BF16)],
        compiler_params=_cp(("parallel",)),
    )(up, up, up, brow, bcol, pw_bf, pscale)


def _pool_bwd_a(dp, dsave, pw_bf, pwt_bf, pscale):
    n = dp.shape[0]
    tb = POOL_TB

    def body(dp_ref, d_ref, pw_ref, pwt_ref, sc_ref, dd_ref, dds_ref, gw_ref, gs_ref):
        i = pl.program_id(0)

        @pl.when(i == 0)
        def _():
            gw_ref[...] = jnp.zeros_like(gw_ref)
            gs_ref[...] = jnp.zeros_like(gs_ref)

        for g in range(N_POOL):
            sl = slice(g * POOL_DIM, (g + 1) * POOL_DIM)
            dpv = dp_ref[:, sl]
            dv = d_ref[:, sl]
            dpw_bf = (dpv * sc_ref[:, sl]).astype(BF16)
            dd = _dot(dpw_bf, pwt_ref[g])
            dd_ref[:, sl] = dd
            dds_ref[:, sl] = (dd * _pool_inv(i, g, n)).astype(BF16)
            gw_ref[g] += _dot_tn(dv, dpw_bf)
            gs_ref[0:1, sl] += jnp.sum(dpv * _dot(dv, pw_ref[g]), axis=0, keepdims=True)

    wspec = pl.BlockSpec((N_POOL, POOL_DIM, POOL_DIM), lambda i: (0, 0, 0))
    return pl.pallas_call(
        body, name="pool_bwd_a", grid=(n // tb,),
        in_specs=[_row_spec(tb, D_MODEL), _row_spec(tb, D_MODEL), wspec, wspec, _par_spec(D_MODEL)],
        out_specs=[_row_spec(tb, D_MODEL), _row_spec(tb, D_MODEL), wspec, _acc_spec(D_MODEL)],
        out_shape=[jax.ShapeDtypeStruct((n, D_MODEL), F32), jax.ShapeDtypeStruct((n, D_MODEL), BF16),
                   jax.ShapeDtypeStruct((N_POOL, POOL_DIM, POOL_DIM), F32), jax.ShapeDtypeStruct((8, D_MODEL), F32)],
        compiler_params=_cp(("arbitrary",)),
    )(dp, dsave, pw_bf, pwt_bf, pscale)


def _pool_bwd_b(dds, dd, consts_t):
    n = dd.shape[0]
    tb = POOL_TB
    nb = n // tb
    brow, bcol = consts_t
    prev, cur, nxt = _pool_halo_specs(n, D_MODEL)

    def body(p_ref, c_ref, n_ref, brow_ref, bcol_ref, dd_ref, o_ref):
        i = pl.program_id(0)
        for g in range(N_POOL):
            sl = slice(g * POOL_DIM, (g + 1) * POOL_DIM)
            box = _pool_box(p_ref, c_ref, n_ref, brow_ref, bcol_ref, g, i, nb)
            o_ref[:, sl] = (box - dd_ref[:, sl]).astype(BF16)

    return pl.pallas_call(
        body, name="pool_bwd_b", grid=(nb,),
        in_specs=[prev, cur, nxt] + _pool_const_specs() + [_row_spec(tb, D_MODEL)],
        out_specs=_row_spec(tb, D_MODEL),
        out_shape=jax.ShapeDtypeStruct((n, D_MODEL), BF16),
        compiler_params=_cp(("parallel",)),
    )(dds, dds, dds, brow, bcol, dd)


def _pair_add(slabs, recvs, core):
    na = len(slabs)
    hr = [t.shape[1] // 4 for t in slabs]

    def body(core_ref, *refs):
        for a in range(na):
            refs[2 * na + a][...] = (refs[a][...] + refs[na + a][...]).astype(BF16)

    own = [pl.BlockSpec((None, hr[a], slabs[a].shape[2]), lambda j, i, c_ref: (j, 2 * c_ref[0] + i, 0)) for a in range(na)]
    got = [pl.BlockSpec((None, hr[a], slabs[a].shape[2]), lambda j, i, c_ref: (j, i, 0)) for a in range(na)]
    return pl.pallas_call(
        body, name="reduce_g_pair",
        grid_spec=pltpu.PrefetchScalarGridSpec(num_scalar_prefetch=1, grid=(4, 2), in_specs=own + got, out_specs=got),
        out_shape=[jax.ShapeDtypeStruct(r.shape, BF16) for r in recvs],
        compiler_params=_cp(("arbitrary", "arbitrary")),
    )(core, *slabs, *recvs)


def _sum4(parts):
    na = len(parts)
    hr = [t.shape[1] // 2 for t in parts]

    def body(*refs):
        for a in range(na):
            p = refs[a]
            refs[na + a][...] = ((p[0].astype(F32) + p[1].astype(F32)) + p[2].astype(F32)) + p[3].astype(F32)

    return pl.pallas_call(
        body, name="reduce_g_sum", grid=(2,),
        in_specs=[pl.BlockSpec((4, hr[a], parts[a].shape[2]), lambda i: (0, i, 0)) for a in range(na)],
        out_specs=[pl.BlockSpec((hr[a], parts[a].shape[2]), lambda i: (i, 0)) for a in range(na)],
        out_shape=[jax.ShapeDtypeStruct(t.shape[1:], F32) for t in parts],
        compiler_params=_cp(("arbitrary",)),
    )(*parts)


def _adamw(w, g, m, v, name):
    r, cdim = w.shape
    tb = _row_block(r, 256)
    c1 = 1.0 - ADAM_B1 ** ADAM_STEP
    c2 = 1.0 - ADAM_B2 ** ADAM_STEP

    def body(w_ref, g_ref, m_ref, v_ref, d_ref, nm_ref, nv_ref):
        gv = g_ref[...]
        nm = ADAM_B1 * m_ref[...] + (1.0 - ADAM_B1) * gv
        nv = ADAM_B2 * v_ref[...] + (1.0 - ADAM_B2) * (gv * gv)
        m_hat = nm / c1
        v_hat = nv / c2
        d_ref[...] = -ADAM_LR * (m_hat / (jnp.sqrt(v_hat) + ADAM_EPS) + ADAM_WD * w_ref[...])
        nm_ref[...] = nm
        nv_ref[...] = nv

    spec = _row_spec(tb, cdim)
    shp = jax.ShapeDtypeStruct((r, cdim), F32)
    return pl.pallas_call(
        body, name=name, grid=(r // tb,),
        in_specs=[spec] * 4, out_specs=[spec] * 3, out_shape=[shp] * 3,
        compiler_params=_cp(("parallel",)),
    )(w, g, m, v)


def _mesh_pos():
    return lax.axis_index("x"), lax.axis_index("y"), lax.axis_index("c")


_ANY = pl.BlockSpec(memory_space=pl.ANY)


def _remote(src, dst, send_sem, recv_sem, device):
    return pltpu.make_async_remote_copy(src_ref=src, dst_ref=dst, send_sem=send_sem, recv_sem=recv_sem,
                                        device_id=device, device_id_type=MESH)


def _other_chips(x, y):
    return [(1 - x, y), (x, 1 - y), (1 - x, 1 - y)]


def _half(nrows, h):
    return pl.ds(h * (nrows // 2), nrows // 2)


def _gather_weights(shards, conv8):
    na = len(shards)
    nrow = [t.shape[0] for t in shards]

    def body(*refs):
        ins, conv_in = refs[:na], refs[na]
        outs, conv_out = refs[na + 1:2 * na + 1], refs[2 * na + 1]
        send_sems, recv_sems, local_sems = refs[2 * na + 2:]
        x, y, c = _mesh_pos()
        me = 2 * x + y
        sib = (x, y, 1 - c)
        chips = _other_chips(x, y)

        def ici(k, a, slot):
            px, py = chips[k]
            rows = _half(nrow[a], c)
            return _remote(ins[a].at[rows, :], outs[a].at[slot, rows, :], send_sems.at[k * na + a],
                           recv_sems.at[k * na + a], (px, py, c))

        def fwd(k, a, h):
            px, py = chips[k]
            blk = outs[a].at[2 * px + py, _half(nrow[a], h), :]
            return _remote(blk, blk, send_sems.at[(3 + k) * na + a], recv_sems.at[(3 + k) * na + a], sib)

        def conv(k, slot):
            px, py = chips[k]
            return _remote(conv_in, conv_out.at[slot], send_sems.at[6 * na + k], recv_sems.at[6 * na + k], (px, py, c))

        local = [pltpu.make_async_copy(ins[a], outs[a].at[me], local_sems.at[a]) for a in range(na)]
        local.append(pltpu.make_async_copy(conv_in, conv_out.at[me], local_sems.at[na]))
        for cp in local:
            cp.start()
        sends = []
        for k in range(3):
            for a in range(na):
                sends.append(ici(k, a, me))
            sends.append(conv(k, me))
        for cp in sends:
            cp.start()
        for k in range(3):
            px, py = chips[k]
            for a in range(na):
                ici(k, a, 2 * px + py).wait_recv()
                cp = fwd(k, a, c)
                cp.start()
                sends.append(cp)
        for k in range(3):
            px, py = chips[k]
            for a in range(na):
                fwd(k, a, 1 - c).wait_recv()
            conv(k, 2 * px + py).wait_recv()
        for cp in sends:
            cp.wait_send()
        for cp in local:
            cp.wait()

    nsem = 6 * na + 3
    return pl.pallas_call(
        body, name="gather_w", in_specs=[_ANY] * (na + 1), out_specs=[_ANY] * (na + 1),
        out_shape=[jax.ShapeDtypeStruct((4,) + t.shape, t.dtype) for t in shards] +
                  [jax.ShapeDtypeStruct((4,) + conv8.shape, conv8.dtype)],
        scratch_shapes=[pltpu.SemaphoreType.DMA((nsem,)), pltpu.SemaphoreType.DMA((nsem,)),
                        pltpu.SemaphoreType.DMA((na + 1,))],
    )(*shards, conv8)


def _pair_swap(slabs):
    na = len(slabs)

    def body(*refs):
        ins, outs = refs[:na], refs[na:2 * na]
        send_sems, recv_sems = refs[2 * na:]
        x, y, c = _mesh_pos()
        cps = [_remote(ins[a].at[:, _half(slabs[a].shape[1], 1 - c), :], outs[a], send_sems.at[a], recv_sems.at[a],
                       (x, y, 1 - c)) for a in range(na)]
        for cp in cps:
            cp.start()
        for cp in cps:
            cp.wait()

    return pl.pallas_call(
        body, name="reduce_g_d2d", in_specs=[_ANY] * na, out_specs=[_ANY] * na,
        out_shape=[jax.ShapeDtypeStruct((4, t.shape[1] // 2, t.shape[2]), t.dtype) for t in slabs],
        scratch_shapes=[pltpu.SemaphoreType.DMA((na,)), pltpu.SemaphoreType.DMA((na,))],
    )(*slabs)


def _chip_exchange(pairs):
    na = len(pairs)

    def body(*refs):
        ins, outs = refs[:na], refs[na:2 * na]
        send_sems, recv_sems, local_sems = refs[2 * na:]
        x, y, c = _mesh_pos()
        me = 2 * x + y
        chips = _other_chips(x, y)

        def copy(k, a, slot):
            px, py = chips[k]
            return _remote(ins[a].at[2 * px + py], outs[a].at[slot], send_sems.at[k * na + a], recv_sems.at[k * na + a],
                           (px, py, c))

        local = [pltpu.make_async_copy(ins[a].at[me], outs[a].at[me], local_sems.at[a]) for a in range(na)]
        for cp in local:
            cp.start()
        sends = [copy(k, a, me) for k in range(3) for a in range(na)]
        for cp in sends:
            cp.start()
        for k in range(3):
            px, py = chips[k]
            for a in range(na):
                copy(k, a, 2 * px + py).wait_recv()
        for cp in sends:
            cp.wait_send()
        for cp in local:
            cp.wait()

    return pl.pallas_call(
        body, name="reduce_g_ici", in_specs=[_ANY] * na, out_specs=[_ANY] * na,
        out_shape=[jax.ShapeDtypeStruct(t.shape, t.dtype) for t in pairs],
        scratch_shapes=[pltpu.SemaphoreType.DMA((3 * na,)), pltpu.SemaphoreType.DMA((3 * na,)),
                        pltpu.SemaphoreType.DMA((na,))],
    )(*pairs)


def _share_halves(totals):
    na = len(totals)
    nrow = [2 * t.shape[0] for t in totals]

    def body(*refs):
        ins, outs = refs[:na], refs[na:2 * na]
        send_sems, recv_sems, local_sems = refs[2 * na:]
        x, y, c = _mesh_pos()
        local = [pltpu.make_async_copy(ins[a], outs[a].at[_half(nrow[a], c), :], local_sems.at[a]) for a in range(na)]
        for cp in local:
            cp.start()

        def copy(a, h):
            return _remote(ins[a], outs[a].at[_half(nrow[a], h), :], send_sems.at[a], recv_sems.at[a], (x, y, 1 - c))

        sends = [copy(a, c) for a in range(na)]
        for cp in sends:
            cp.start()
        for a in range(na):
            copy(a, 1 - c).wait_recv()
        for cp in sends:
            cp.wait_send()
        for cp in local:
            cp.wait()

    return pl.pallas_call(
        body, name="reduce_g_share", in_specs=[_ANY] * na, out_specs=[_ANY] * na,
        out_shape=[jax.ShapeDtypeStruct((nrow[a], totals[a].shape[1]), totals[a].dtype) for a in range(na)],
        scratch_shapes=[pltpu.SemaphoreType.DMA((na,)), pltpu.SemaphoreType.DMA((na,)), pltpu.SemaphoreType.DMA((na,))],
    )(*totals)


def _allreduce_small(v, name):
    r, cdim = v.shape

    def body(v_ref, out_ref, buf, send_sems, recv_sems):
        x, y, c = _mesh_pos()
        me = 4 * x + 2 * y + c
        buf[me] = v_ref[...]
        rel = [(bx, by, bc) for bx in (0, 1) for by in (0, 1) for bc in (0, 1)][1:]

        def peer(b):
            bx, by, bc = b
            return ((1 - x) if bx else x, (1 - y) if by else y, (1 - c) if bc else c)

        def copy(k, slot):
            return pltpu.make_async_remote_copy(
                src_ref=v_ref, dst_ref=buf.at[slot], send_sem=send_sems.at[k], recv_sem=recv_sems.at[k],
                device_id=peer(rel[k]), device_id_type=MESH)

        sends = [copy(k, me) for k in range(7)]
        for cp in sends:
            cp.start()
        for k in range(7):
            px, py, pc = peer(rel[k])
            copy(k, 4 * px + 2 * py + pc).wait_recv()
        for cp in sends:
            cp.wait_send()
        acc = buf[0]
        for j in range(1, 8):
            acc = acc + buf[j]
        out_ref[...] = acc

    vm = pl.BlockSpec(memory_space=pltpu.VMEM)
    return pl.pallas_call(
        body, name=name, in_specs=[vm], out_specs=vm,
        out_shape=jax.ShapeDtypeStruct((r, cdim), F32),
        scratch_shapes=[pltpu.VMEM((8, r, cdim), F32), pltpu.SemaphoreType.DMA((7,)), pltpu.SemaphoreType.DMA((7,))],
    )(v)


_BIG = (("in_proj", (D_MODEL, D_IN_PROJ // 4), 1), ("w_out", (2 * D_MODEL // 4, D_MODEL), 0),
        ("w_gate", (D_MODEL, D_FF // 4), 1), ("w_up", (D_MODEL, D_FF // 4), 1), ("w_down", (D_FF // 4, D_MODEL), 0),
        ("pool_w", (N_POOL * POOL_DIM // 4, POOL_DIM), None), ("w_ada", (D_MODEL, 6 * D_MODEL // 4), 1))


def _assemble(name, t):
    _, r, c = t.shape
    axis = {n: ax for n, _, ax in _BIG}[name]
    if axis == 0:
        return t.reshape(4 * r, c)
    if axis == 1:
        return t.transpose(1, 0, 2).reshape(r, 4 * c)
    return t.reshape(4, N_POOL, POOL_DIM // 4, POOL_DIM).transpose(1, 0, 2, 3).reshape(N_POOL, POOL_DIM, POOL_DIM)


def _to_slabs(name, g):
    (r, c), axis = {n: (sh, ax) for n, sh, ax in _BIG}[name]
    if axis == 0:
        return g.reshape(4, r, c)
    if axis == 1:
        return g.reshape(r, 4, c).transpose(1, 0, 2)
    return g.reshape(N_POOL, 4, POOL_DIM // 4, POOL_DIM).transpose(1, 0, 2, 3).reshape(4, r, c)


def _reduce_grads(full_grads, ci):
    names = [n for n, _, _ in _BIG]
    slabs = [_to_slabs(n, full_grads[n]) for n in names]
    recvs = _pair_swap(slabs)
    pairs = _pair_add(slabs, recvs, ci.reshape(1).astype(jnp.int32))
    parts = _chip_exchange(pairs)
    totals = _sum4(parts)
    return dict(zip(names, _share_halves(totals)))


def _pad_cols(w, n):
    return jnp.concatenate([w, jnp.zeros((w.shape[0], n - w.shape[1]), w.dtype)], axis=1)


def _device_step(x, c, ctx, target, wts, w8, small, tb):
    n = x.shape[0]
    d = D_MODEL
    c_ctx = small["c_ctx"]

    win = wts["in_proj"]
    wz, wxd, wup = win[:, 0:d], _pad_cols(win[:, d:d + D_XBC + 2 * SSD_HEADS], D_XD), win[:, d + D_XBC + 2 * SSD_HEADS:]
    wout = wts["w_out"]
    wg, wu, wd = wts["w_gate"], wts["w_up"], wts["w_down"]
    pw = wts["pool_w"]
    wada = wts["w_ada"]

    emb_g, emb_b = _vec(small["emb_ln_g"]), _vec(small["emb_ln_b"])
    ln1_g, ln1_b = _vec(small["ln1_g"]), _vec(small["ln1_b"])
    ln2_g, ln2_b = _vec(small["ln2_g"]), _vec(small["ln2_b"])
    gn = _vec(small["ssd_norm_g"])
    pscale = _vec(small["pool_scale"])
    conv_b = _vec(small["conv_b"])
    dskip_e = jnp.repeat(small["d_skip"].reshape(-1), HEAD_DIM).reshape(1, d)
    zpad = jnp.zeros((2, 1, 128 - SSD_HEADS), F32)
    bias2 = jnp.concatenate([small["dt_bias"].reshape(2, 1, SSD_HEADS), zpad], axis=2)
    a2 = jnp.concatenate([-jnp.exp(small["a_log"].reshape(2, 1, SSD_HEADS)), zpad], axis=2)
    rexp = (jnp.arange(128)[:, None] == (jnp.arange(d)[None, :] // HEAD_DIM)).astype(BF16)
    rexp_t = rexp.T

    c8 = jnp.concatenate([c.reshape(1, d), c_ctx.reshape(1, d), jnp.zeros((6, d), F32)], axis=0)
    mods = _mods_fwd(c8, wada, _vec(small["b_ada"]))
    sh1, sc1, g1, sh2, sc2, g2 = [mods[0:1, i * d:(i + 1) * d] for i in range(6)]
    sh1c, sc1c = mods[1:2, 0:d], mods[1:2, d:2 * d]

    tbc = min(tb, ctx.shape[0])
    xc0, hc = _ln_mod(ctx, emb_g, emb_b, sh1c, sc1c, tbc, "ln_mod_ctx")
    xdc = _matmul_nn([(hc, wxd)], F32, 512, D_XD, "in_proj_ctx")
    uc = _conv_fwd(xdc, w8, conv_b, tbc, "conv_fwd_ctx")
    hzero = jnp.zeros((2, D_STATE, d), F32)
    _, hprev_c, hfin_c = _ssd_fwd(uc, xdc, bias2, a2, rexp, hzero, "ssd_fwd_ctx")

    x0, h1 = _ln_mod(x, emb_g, emb_b, sh1, sc1, tb, "ln_mod")
    z = _matmul_nn([(h1, wz)], F32, 512, 1024, "in_proj_z")
    xd = _matmul_nn([(h1, wxd)], F32, 512, D_XD, "in_proj_xd")
    up = _matmul_nn([(h1, wup)], F32, 512, 1024, "in_proj_up")
    u = _conv_fwd(xd, w8, conv_b, tb, "conv_fwd")
    y, hprev, _ = _ssd_fwd(u, xd, bias2, a2, rexp, hfin_c, "ssd_fwd")
    yn = _merge_fwd(y, u, z, dskip_e, gn, tb)
    pconst = _pool_consts(False)
    pool, dsave = _pool_fwd(up, pconst, pw, pscale)
    mix = _matmul_nn([(yn, wout[0:d]), (pool, wout[d:2 * d])], F32, 512, 1024, "out_proj")
    x1, h2 = _res_ln(x0, mix, g1, ln1_g, ln1_b, sh2, sc2, tb)

    gate, upp, hmid = _swiglu_fwd(h2, wg, wu, 512, D_FF // 2)
    ffn = _matmul_nn([(hmid, wd)], F32, 512, 1024, "ffn_down")
    dffn, dr2, acc2 = _final_ln_loss(x1, ffn, g2, ln2_g, ln2_b, target, tb)
    loss = (0.5 / d) * jnp.sum(acc2[3])

    dgate, dupp = _swiglu_bwd(dffn, wd.T, gate, upp, 512, D_FF // 2)
    g_wdown = _matmul_tn(hmid, dffn, 512, 1024, "g_w_down")
    g_wgate = _matmul_tn(h2, dgate, 512, 1408, "g_w_gate")
    g_wup = _matmul_tn(h2, dupp, 512, 1408, "g_w_up")
    dh2 = _matmul_nn([(dgate, wg.T), (dupp, wu.T)], F32, 512, 1024, "d_h2")
    dmix, dr1, acc1 = _bwd_ln1(dr2, dh2, x1, x0, mix, g1, sc2, ln1_g, tb)

    dyn = _matmul_nn([(dmix, wout[0:d].T)], F32, 512, 1024, "d_yn")
    dpool = _matmul_nn([(dmix, wout[d:2 * d].T)], F32, 512, 1024, "d_pool")
    g_wout = jnp.concatenate([_matmul_tn(yn, dmix, 512, 1024, "g_w_out_a"),
                              _matmul_tn(pool, dmix, 512, 1024, "g_w_out_b")], axis=0)
    dd, dds, g_pw, accp = _pool_bwd_a(dpool, dsave, pw, jnp.swapaxes(pw, 1, 2), pscale)
    dup = _pool_bwd_b(dds, dd, _pool_consts(True))
    dy, dz, accm = _merge_bwd(dyn, y, u, z, dskip_e, gn, tb)
    lam0 = jnp.zeros((2, D_STATE, d), F32)
    dxs, dbc, ddt, accs, lam_c = _ssd_bwd(u, xd, bias2, a2, rexp, rexp_t, dy, hprev, lam0, "ssd_bwd")
    du, accb = _conv_bwd_a(dxs, dy, dskip_e, dbc, u, tb, "conv_bwd_a")
    dxd, accw = _conv_bwd_b(du, xd, ddt, w8, tb, "conv_bwd_b")

    lc = ctx.shape[0]
    zeros_c = jnp.zeros((lc, d), F32)
    dxs_c, dbc_c, ddt_c, accs_c, _ = _ssd_bwd(uc, xdc, bias2, a2, rexp, rexp_t, zeros_c, hprev_c, lam_c, "ssd_bwd_ctx")
    du_c, accb_c = _conv_bwd_a(dxs_c, zeros_c, dskip_e, dbc_c, uc, tbc, "conv_bwd_a_ctx")
    dxd_c, accw_c = _conv_bwd_b(du_c, xdc, ddt_c, w8, tbc, "conv_bwd_b_ctx")
    dhc = _matmul_nn([(dxd_c, wxd.T)], F32, 512, 1024, "d_hc")
    _, acc0c = _bwd_ln0(None, dhc, ctx, emb_g, emb_b, sc1c, tbc, "bwd_ln0_ctx")

    dh1 = _matmul_nn([(dz, wz.T), (dxd, wxd.T), (dup, wup.T)], F32, 512, 1024, "d_h1")
    g_wz = _matmul_tn(h1, dz, 512, 1024, "g_in_proj_z")
    g_wxd = _matmul_tn(h1, dxd, 512, D_XD, "g_in_proj_xd") + _matmul_tn(hc, dxd_c, 512, D_XD, "g_in_proj_xd_ctx")
    g_wpo = _matmul_tn(h1, dup, 512, 1024, "g_in_proj_up")
    g_win = jnp.concatenate([g_wz, g_wxd[:, 0:D_XBC + 2 * SSD_HEADS], g_wpo], axis=1)
    grad_x, acc0 = _bwd_ln0(dr1, dh1, x, emb_g, emb_b, sc1, tb, "bwd_ln0")

    zero_d = jnp.zeros((1, d), F32)
    dmod = jnp.concatenate([acc0[1:2], acc0[0:1], acc1[4:5], acc1[1:2], acc1[0:1], acc2[2:3]], axis=1)
    dmodc = jnp.concatenate([acc0c[1:2], acc0c[0:1]] + [zero_d] * 4, axis=1)
    dm8 = jnp.concatenate([dmod, dmodc, jnp.zeros((6, 6 * d), F32)], axis=0)
    g_wada, g_bada8 = _mods_bwd_w(c8.T, dm8)
    g_cctx8 = _mods_bwd_c(dm8, wada, c8)

    big = dict(in_proj=g_win, w_out=g_wout, w_gate=g_wgate, w_up=g_wup, w_down=g_wdown, pool_w=g_pw, w_ada=g_wada)
    sml = dict(
        c_ctx=g_cctx8[1], emb_ln_g=acc0[2] + acc0c[2], emb_ln_b=acc0[3] + acc0c[3], b_ada=g_bada8[0],
        conv_w=accw[0:D_CONV] + accw_c[0:D_CONV], conv_b=accb[0] + accb_c[0],
        dt_bias=accs[:, 0, 0:SSD_HEADS] + accs_c[:, 0, 0:SSD_HEADS],
        a_log=accs[:, 1, 0:SSD_HEADS] + accs_c[:, 1, 0:SSD_HEADS],
        d_skip=jnp.sum(accm[1].reshape(SSD_HEADS, HEAD_DIM), axis=1),
        ssd_norm_g=accm[0], pool_scale=accp[0], ln1_g=acc1[2], ln1_b=acc1[3], ln2_g=acc2[0], ln2_b=acc2[1])
    return loss, grad_x, big, sml


_SMALL = ("c_ctx", "emb_ln_g", "emb_ln_b", "b_ada", "conv_w", "conv_b", "dt_bias", "a_log", "d_skip",
          "ssd_norm_g", "pool_scale", "ln1_g", "ln1_b", "ln2_g", "ln2_b")


_SMALL_ROWS = 8


def _pack_small(vals, names):
    rows = []
    for nme in names:
        flat = vals[nme].reshape(-1).astype(F32)
        assert flat.shape[0] <= _SMALL_ROWS * 1024
        rows.append(jnp.concatenate([flat, jnp.zeros((_SMALL_ROWS * 1024 - flat.shape[0],), F32)]).reshape(_SMALL_ROWS, 1024))
    return jnp.concatenate(rows, axis=0)


def _unpack_small(packed, shapes, names):
    out = {}
    for i, nme in enumerate(names):
        size = math.prod(shapes[nme])
        out[nme] = packed[i * _SMALL_ROWS:(i + 1) * _SMALL_ROWS].reshape(-1)[:size].reshape(shapes[nme])
    return out


_WEIGHT_ORDER = ("c_ctx", "emb_ln_g", "emb_ln_b", "w_ada", "b_ada", "in_proj", "conv_w", "conv_b", "dt_bias", "a_log",
                 "d_skip", "ssd_norm_g", "pool_w", "pool_scale", "w_out", "ln1_g", "ln1_b", "w_gate", "w_up", "w_down",
                 "ln2_g", "ln2_b")


def _as2d(a):
    return a.reshape(-1, a.shape[-1])


def kernel(x, c, ctx, c_ctx, emb_ln_g, emb_ln_b, w_ada, b_ada, in_proj, conv_w, conv_b, dt_bias, a_log, d_skip, ssd_norm_g, pool_w, pool_scale, w_out, ln1_g, ln1_b, w_gate, w_up, w_down, ln2_g, ln2_b, loss_target, m_c_ctx, m_emb_ln_g, m_emb_ln_b, m_w_ada, m_b_ada, m_in_proj, m_conv_w, m_conv_b, m_dt_bias, m_a_log, m_d_skip, m_ssd_norm_g, m_pool_w, m_pool_scale, m_w_out, m_ln1_g, m_ln1_b, m_w_gate, m_w_up, m_w_down, m_ln2_g, m_ln2_b, v_c_ctx, v_emb_ln_g, v_emb_ln_b, v_w_ada, v_b_ada, v_in_proj, v_conv_w, v_conv_b, v_dt_bias, v_a_log, v_d_skip, v_ssd_norm_g, v_pool_w, v_pool_scale, v_w_out, v_ln1_g, v_ln1_b, v_w_gate, v_w_up, v_w_down, v_ln2_g, v_ln2_b):
    w = dict(c_ctx=c_ctx, emb_ln_g=emb_ln_g, emb_ln_b=emb_ln_b, w_ada=w_ada, b_ada=b_ada, in_proj=in_proj, conv_w=conv_w,
             conv_b=conv_b, dt_bias=dt_bias, a_log=a_log, d_skip=d_skip, ssd_norm_g=ssd_norm_g, pool_w=pool_w,
             pool_scale=pool_scale, w_out=w_out, ln1_g=ln1_g, ln1_b=ln1_b, w_gate=w_gate, w_up=w_up, w_down=w_down,
             ln2_g=ln2_g, ln2_b=ln2_b)
    m = dict(c_ctx=m_c_ctx, emb_ln_g=m_emb_ln_g, emb_ln_b=m_emb_ln_b, w_ada=m_w_ada, b_ada=m_b_ada, in_proj=m_in_proj,
             conv_w=m_conv_w, conv_b=m_conv_b, dt_bias=m_dt_bias, a_log=m_a_log, d_skip=m_d_skip,
             ssd_norm_g=m_ssd_norm_g, pool_w=m_pool_w, pool_scale=m_pool_scale, w_out=m_w_out, ln1_g=m_ln1_g,
             ln1_b=m_ln1_b, w_gate=m_w_gate, w_up=m_w_up, w_down=m_w_down, ln2_g=m_ln2_g, ln2_b=m_ln2_b)
    v = dict(c_ctx=v_c_ctx, emb_ln_g=v_emb_ln_g, emb_ln_b=v_emb_ln_b, w_ada=v_w_ada, b_ada=v_b_ada, in_proj=v_in_proj,
             conv_w=v_conv_w, conv_b=v_conv_b, dt_bias=v_dt_bias, a_log=v_a_log, d_skip=v_d_skip,
             ssd_norm_g=v_ssd_norm_g, pool_w=v_pool_w, pool_scale=v_pool_scale, w_out=v_w_out, ln1_g=v_ln1_g,
             ln1_b=v_ln1_b, w_gate=v_w_gate, w_up=v_w_up, w_down=v_w_down, ln2_g=v_ln2_g, ln2_b=v_ln2_b)

    xi, yi, ci = _mesh_pos()
    chip = 2 * xi + yi

    shards = [w[name][0].astype(BF16).reshape(shp) for name, shp, _ in _BIG]
    conv8 = jnp.concatenate([conv_w[0], jnp.zeros((8 - D_CONV, conv_w.shape[-1]), F32)], axis=0)
    *gathered, conv4 = _gather_weights(shards, conv8)
    wts = {name: _assemble(name, t) for (name, _, _), t in zip(_BIG, gathered)}
    w8 = conv4.transpose(1, 0, 2).reshape(8, D_XBC)
    small = {nme: (w[nme] if nme in ("c_ctx", "emb_ln_g", "emb_ln_b") else w[nme][0]) for nme in _SMALL if nme != "conv_w"}

    loss, grad_x, big, sml = _device_step(x[0], c, ctx[0], loss_target[0], wts, w8, small, 512)
    loss = lax.psum(loss, ("x", "y", "c"))

    g_big = _reduce_grads(big, ci)
    small_shapes = {nme: sml[nme].shape for nme in _SMALL}
    g_small = _unpack_small(_allreduce_small(_pack_small(sml, _SMALL), "reduce_small"), small_shapes, _SMALL)
    cw_cols = conv_w.shape[-1]
    g_small["conv_w"] = lax.dynamic_slice_in_dim(g_small["conv_w"], chip * cw_cols, cw_cols, axis=1)

    grads, delta, new_m, new_v = {}, {}, {}, {}
    for name, _, _ in _BIG:
        g2 = _as2d(g_big[name])
        d2, m2, v2 = _adamw(_as2d(w[name][0]), g2, _as2d(m[name][0]), _as2d(v[name][0]), "adamw_" + name)
        grads[name] = g2.reshape(w[name].shape)
        delta[name], new_m[name], new_v[name] = (t.reshape(w[name].shape) for t in (d2, m2, v2))
    shp = {nme: w[nme].shape for nme in _SMALL}
    gp = _pack_small(g_small, _SMALL)
    dp, mp, vp = _adamw(_pack_small(w, _SMALL), gp, _pack_small(m, _SMALL), _pack_small(v, _SMALL), "adamw_small")
    for dst, src in ((grads, gp), (delta, dp), (new_m, mp), (new_v, vp)):
        dst.update(_unpack_small(src, shp, _SMALL))

    return (loss, grad_x[None], *[grads[nme] for nme in _WEIGHT_ORDER], *[delta[nme] for nme in _WEIGHT_ORDER],
            *[new_m[nme] for nme in _WEIGHT_ORDER], *[new_v[nme] for nme in _WEIGHT_ORDER])
```

```python
import functools
import math

import jax
import jax.numpy as jnp
from jax import lax
from jax.experimental import pallas as pl
from jax.experimental.pallas import tpu as pltpu

F32 = jnp.float32
BF16 = jnp.bfloat16
MESH = pl.DeviceIdType.MESH

D_MODEL = 1024
SSD_HEADS = 16
HEAD_DIM = 64
D_STATE = 128
CHUNK = 128
D_CONV = 5
D_XBC = D_MODEL + 2 * D_STATE
D_XD = 1408
N_POOL = 4
POOL_DIM = 256
POOL_WINDOWS = (2, 4, 8, 16)
GRID_W = 64
D_FF = 2816
D_IN_PROJ = 3360
LN_EPS = 1e-5
ALPHA = 2.0 ** 0.25
POOL_TB = 512

ADAM_LR = 0.001
ADAM_B1 = 0.9
ADAM_B2 = 0.999
ADAM_EPS = 1e-08
ADAM_WD = 0.01
ADAM_STEP = 10

VMEM_LIMIT = 56 * 1024 * 1024


def _cp(sem=None):
    return pltpu.CompilerParams(dimension_semantics=sem, vmem_limit_bytes=VMEM_LIMIT)


def _sigmoid(x):
    return 1.0 / (1.0 + jnp.exp(-x))


def _silu(x):
    return x * _sigmoid(x)


def _dsilu(x):
    s = _sigmoid(x)
    return s * (1.0 + x * (1.0 - s))


def _softplus(x):
    t = jnp.exp(-jnp.abs(x))
    u = 1.0 + t
    log1p = jnp.where(u == 1.0, t, jnp.log(u) * t / (u - 1.0 + (u == 1.0)))
    return jnp.maximum(x, 0.0) + log1p


def _split(x, n):
    parts, r = [], x
    for _ in range(n):
        p = r.astype(BF16)
        parts.append(p)
        r = r - p.astype(F32)
    return parts


def _dot(a, b):
    return jnp.dot(a, b, preferred_element_type=F32)


def _dot_nt(a, b):
    return lax.dot_general(a, b, (((1,), (1,)), ((), ())), preferred_element_type=F32)


def _dot_tn(a, b):
    return lax.dot_general(a, b, (((0,), (0,)), ((), ())), preferred_element_type=F32)


def _dot_sel_l(sel_bf, x, n=3):
    out = None
    for p in _split(x, n):
        t = _dot(sel_bf, p)
        out = t if out is None else out + t
    return out


def _dot_sel_r(x, sel_bf, n=3):
    out = None
    for p in _split(x, n):
        t = _dot(p, sel_bf)
        out = t if out is None else out + t
    return out


def _row_block(n, cap=256, mult=8):
    best = None
    for t in range(mult, min(n, cap) + 1, mult):
        if n % t == 0:
            best = t
    return best if best is not None else n


def _vec(v):
    return v.reshape(1, -1).astype(F32)


def _mods_fwd(c8, wada_bf, b_ada):
    d = c8.shape[1]
    n = wada_bf.shape[1]

    def body(c_ref, w_ref, b_ref, o_ref):
        s = _silu(c_ref[...]).astype(BF16)
        o_ref[...] = _dot(s, w_ref[...]) + b_ref[...]

    return pl.pallas_call(
        body, name="mods_fwd", grid=(n // d,),
        in_specs=[pl.BlockSpec((8, d), lambda j: (0, 0)),
                  pl.BlockSpec((d, d), lambda j: (0, j)),
                  pl.BlockSpec((1, d), lambda j: (0, j))],
        out_specs=pl.BlockSpec((8, d), lambda j: (0, j)),
        out_shape=jax.ShapeDtypeStruct((8, n), F32),
        compiler_params=_cp(("arbitrary",)),
    )(c8, wada_bf, b_ada)


def _mods_bwd_w(ct8, dm8):
    d = ct8.shape[0]
    n = dm8.shape[1]
    tn = 512

    def body(ct_ref, dm_ref, dw_ref, db_ref):
        s = _silu(ct_ref[...])
        dm = dm_ref[...]
        dw_ref[...] = s[:, 0:1] * dm[0:1, :] + s[:, 1:2] * dm[1:2, :]
        db_ref[...] = jnp.broadcast_to(dm[0:1, :] + dm[1:2, :], (8, tn))

    return pl.pallas_call(
        body, name="mods_bwd_w", grid=(n // tn,),
        in_specs=[pl.BlockSpec((d, 8), lambda j: (0, 0)),
                  pl.BlockSpec((8, tn), lambda j: (0, j))],
        out_specs=[pl.BlockSpec((d, tn), lambda j: (0, j)),
                   pl.BlockSpec((8, tn), lambda j: (0, j))],
        out_shape=[jax.ShapeDtypeStruct((d, n), F32), jax.ShapeDtypeStruct((8, n), F32)],
        compiler_params=_cp(("arbitrary",)),
    )(ct8, dm8)


def _mods_bwd_c(dm8, wada_bf, c8):
    d = c8.shape[1]
    n = dm8.shape[1]
    nk = n // d

    def body(dm_ref, w_ref, c_ref, o_ref):
        k = pl.program_id(0)

        @pl.when(k == 0)
        def _():
            o_ref[...] = jnp.zeros_like(o_ref)

        o_ref[...] += _dot_nt(dm_ref[...].astype(BF16), w_ref[...])

        @pl.when(k == nk - 1)
        def _():
            o_ref[...] = o_ref[...] * _dsilu(c_ref[...])

    return pl.pallas_call(
        body, name="mods_bwd_c", grid=(nk,),
        in_specs=[pl.BlockSpec((8, d), lambda k: (0, k)),
                  pl.BlockSpec((d, d), lambda k: (0, k)),
                  pl.BlockSpec((8, d), lambda k: (0, 0))],
        out_specs=pl.BlockSpec((8, d), lambda k: (0, 0)),
        out_shape=jax.ShapeDtypeStruct((8, d), F32),
        compiler_params=_cp(("arbitrary",)),
    )(dm8, wada_bf, c8)


def _ln_stats(x):
    mu = jnp.mean(x, axis=-1, keepdims=True)
    xc = x - mu
    var = jnp.mean(xc * xc, axis=-1, keepdims=True)
    rstd = lax.rsqrt(var + LN_EPS)
    return xc * rstd, rstd


def _ln_bwd(dxhat, xhat, rstd):
    m1 = jnp.mean(dxhat, axis=-1, keepdims=True)
    m2 = jnp.mean(dxhat * xhat, axis=-1, keepdims=True)
    return rstd * (dxhat - m1 - xhat * m2)


def _row_spec(tb, d):
    return pl.BlockSpec((tb, d), lambda i: (i, 0))


def _par_spec(d):
    return pl.BlockSpec((1, d), lambda i: (0, 0))


def _acc_spec(d):
    return pl.BlockSpec((8, d), lambda i: (0, 0))


def _ln_mod(x, g, b, sh, sc, tb, name):
    n, d = x.shape

    def body(x_ref, g_ref, b_ref, sh_ref, sc_ref, x0_ref, h_ref):
        xhat, _ = _ln_stats(x_ref[...])
        x0 = xhat * g_ref[...] + b_ref[...]
        x0_ref[...] = x0
        h_ref[...] = (x0 * (1.0 + sc_ref[...]) + sh_ref[...]).astype(BF16)

    return pl.pallas_call(
        body, name=name, grid=(n // tb,),
        in_specs=[_row_spec(tb, d)] + [_par_spec(d)] * 4,
        out_specs=[_row_spec(tb, d), _row_spec(tb, d)],
        out_shape=[jax.ShapeDtypeStruct((n, d), F32), jax.ShapeDtypeStruct((n, d), BF16)],
        compiler_params=_cp(("parallel",)),
    )(x, g, b, sh, sc)


def _res_ln(xres, mix, gate, g, b, sh, sc, tb):
    n, d = xres.shape

    def body(xr_ref, mix_ref, gate_ref, g_ref, b_ref, sh_ref, sc_ref, x1_ref, h_ref):
        r = ALPHA * xr_ref[...] + gate_ref[...] * mix_ref[...]
        xhat, _ = _ln_stats(r)
        x1 = xhat * g_ref[...] + b_ref[...]
        x1_ref[...] = x1
        h_ref[...] = (x1 * (1.0 + sc_ref[...]) + sh_ref[...]).astype(BF16)

    return pl.pallas_call(
        body, name="res_ln1", grid=(n // tb,),
        in_specs=[_row_spec(tb, d)] * 2 + [_par_spec(d)] * 5,
        out_specs=[_row_spec(tb, d), _row_spec(tb, d)],
        out_shape=[jax.ShapeDtypeStruct((n, d), F32), jax.ShapeDtypeStruct((n, d), BF16)],
        compiler_params=_cp(("parallel",)),
    )(xres, mix, gate, g, b, sh, sc)


def _final_ln_loss(x1, ffn, gate, g, b, target, tb):
    n, d = x1.shape

    def body(x1_ref, ffn_ref, gate_ref, g_ref, b_ref, t_ref, dffn_ref, dr_ref, acc_ref):
        i = pl.program_id(0)

        @pl.when(i == 0)
        def _():
            acc_ref[...] = jnp.zeros_like(acc_ref)

        ffn = ffn_ref[...]
        r = ALPHA * x1_ref[...] + gate_ref[...] * ffn
        xhat, rstd = _ln_stats(r)
        err = xhat * g_ref[...] + b_ref[...] - t_ref[...]
        dx2 = err * (1.0 / d)
        dr = _ln_bwd(dx2 * g_ref[...], xhat, rstd)
        dr_ref[...] = dr
        dffn_ref[...] = (gate_ref[...] * dr).astype(BF16)
        acc_ref[0:1, :] += jnp.sum(dx2 * xhat, axis=0, keepdims=True)
        acc_ref[1:2, :] += jnp.sum(dx2, axis=0, keepdims=True)
        acc_ref[2:3, :] += jnp.sum(dr * ffn, axis=0, keepdims=True)
        acc_ref[3:4, :] += jnp.sum(err * err, axis=0, keepdims=True)

    return pl.pallas_call(
        body, name="final_ln_loss", grid=(n // tb,),
        in_specs=[_row_spec(tb, d)] * 2 + [_par_spec(d)] * 3 + [_row_spec(tb, d)],
        out_specs=[_row_spec(tb, d), _row_spec(tb, d), _acc_spec(d)],
        out_shape=[jax.ShapeDtypeStruct((n, d), BF16), jax.ShapeDtypeStruct((n, d), F32),
                   jax.ShapeDtypeStruct((8, d), F32)],
        compiler_params=_cp(("arbitrary",)),
    )(x1, ffn, gate, g, b, target)


def _bwd_ln1(dr2, dh2, x1, x0, mix, gate, sc2, g, tb):
    n, d = x1.shape

    def body(dr2_ref, dh2_ref, x1_ref, x0_ref, mix_ref, gate_ref, sc_ref, g_ref, dmix_ref, dr1_ref, acc_ref):
        i = pl.program_id(0)

        @pl.when(i == 0)
        def _():
            acc_ref[...] = jnp.zeros_like(acc_ref)

        dh2 = dh2_ref[...]
        mix = mix_ref[...]
        dx1 = ALPHA * dr2_ref[...] + dh2 * (1.0 + sc_ref[...])
        r = ALPHA * x0_ref[...] + gate_ref[...] * mix
        xhat, rstd = _ln_stats(r)
        dr1 = _ln_bwd(dx1 * g_ref[...], xhat, rstd)
        dr1_ref[...] = dr1
        dmix_ref[...] = (gate_ref[...] * dr1).astype(BF16)
        acc_ref[0:1, :] += jnp.sum(dh2 * x1_ref[...], axis=0, keepdims=True)
        acc_ref[1:2, :] += jnp.sum(dh2, axis=0, keepdims=True)
        acc_ref[2:3, :] += jnp.sum(dx1 * xhat, axis=0, keepdims=True)
        acc_ref[3:4, :] += jnp.sum(dx1, axis=0, keepdims=True)
        acc_ref[4:5, :] += jnp.sum(dr1 * mix, axis=0, keepdims=True)

    return pl.pallas_call(
        body, name="bwd_ln1", grid=(n // tb,),
        in_specs=[_row_spec(tb, d)] * 5 + [_par_spec(d)] * 3,
        out_specs=[_row_spec(tb, d), _row_spec(tb, d), _acc_spec(d)],
        out_shape=[jax.ShapeDtypeStruct((n, d), BF16), jax.ShapeDtypeStruct((n, d), F32),
                   jax.ShapeDtypeStruct((8, d), F32)],
        compiler_params=_cp(("arbitrary",)),
    )(dr2, dh2, x1, x0, mix, gate, sc2, g)


def _bwd_ln0(dres, dh, x, g, b, sc, tb, name):
    n, d = x.shape
    has_res = dres is not None

    def body(*refs):
        if has_res:
            dres_ref, dh_ref, x_ref, g_ref, b_ref, sc_ref, dx_ref, acc_ref = refs
        else:
            dh_ref, x_ref, g_ref, b_ref, sc_ref, dx_ref, acc_ref = refs
        i = pl.program_id(0)

        @pl.when(i == 0)
        def _():
            acc_ref[...] = jnp.zeros_like(acc_ref)

        dh = dh_ref[...]
        xhat, rstd = _ln_stats(x_ref[...])
        x0 = xhat * g_ref[...] + b_ref[...]
        dx0 = dh * (1.0 + sc_ref[...])
        if has_res:
            dx0 = dx0 + ALPHA * dres_ref[...]
        dx_ref[...] = _ln_bwd(dx0 * g_ref[...], xhat, rstd)
        acc_ref[0:1, :] += jnp.sum(dh * x0, axis=0, keepdims=True)
        acc_ref[1:2, :] += jnp.sum(dh, axis=0, keepdims=True)
        acc_ref[2:3, :] += jnp.sum(dx0 * xhat, axis=0, keepdims=True)
        acc_ref[3:4, :] += jnp.sum(dx0, axis=0, keepdims=True)

    ins = ([dres] if has_res else []) + [dh, x, g, b, sc]
    return pl.pallas_call(
        body, name=name, grid=(n // tb,),
        in_specs=[_row_spec(tb, d)] * (3 if has_res else 2) + [_par_spec(d)] * 3,
        out_specs=[_row_spec(tb, d), _acc_spec(d)],
        out_shape=[jax.ShapeDtypeStruct((n, d), F32), jax.ShapeDtypeStruct((8, d), F32)],
        compiler_params=_cp(("arbitrary",)),
    )(*ins)


def _matmul_nn(pairs, out_dtype, tm, tn, name):
    m = pairs[0][0].shape[0]
    n = pairs[0][1].shape[1]
    tm = min(tm, m)
    tn = min(tn, n)
    npair = len(pairs)

    def body(*refs):
        o_ref = refs[-1]
        acc = None
        for p in range(npair):
            t = _dot(refs[2 * p][...].astype(BF16), refs[2 * p + 1][...])
            acc = t if acc is None else acc + t
        o_ref[...] = acc.astype(out_dtype)

    in_specs, args = [], []
    for a, b in pairs:
        k = a.shape[1]
        in_specs += [pl.BlockSpec((tm, k), lambda i, j: (i, 0)), pl.BlockSpec((k, tn), lambda i, j: (0, j))]
        args += [a, b]
    return pl.pallas_call(
        body, name=name, grid=(m // tm, n // tn),
        in_specs=in_specs,
        out_specs=pl.BlockSpec((tm, tn), lambda i, j: (i, j)),
        out_shape=jax.ShapeDtypeStruct((m, n), out_dtype),
        compiler_params=_cp(("parallel", "arbitrary")),
    )(*args)


def _matmul_tn(a, g, tm, tn, name):
    m, k = a.shape
    n = g.shape[1]
    tm = min(tm, m)
    tn = min(tn, n)

    def body(a_ref, g_ref, o_ref):
        i = pl.program_id(1)

        @pl.when(i == 0)
        def _():
            o_ref[...] = jnp.zeros_like(o_ref)

        o_ref[...] += _dot_tn(a_ref[...].astype(BF16), g_ref[...].astype(BF16))

    return pl.pallas_call(
        body, name=name, grid=(n // tn, m // tm),
        in_specs=[pl.BlockSpec((tm, k), lambda j, i: (i, 0)), pl.BlockSpec((tm, tn), lambda j, i: (i, j))],
        out_specs=pl.BlockSpec((k, tn), lambda j, i: (0, j)),
        out_shape=jax.ShapeDtypeStruct((k, n), F32),
        compiler_params=_cp(("parallel", "arbitrary")),
    )(a, g)


def _swiglu_fwd(h, wg, wu, tm, tn):
    m, k = h.shape
    n = wg.shape[1]
    tm = min(tm, m)

    def body(h_ref, wg_ref, wu_ref, gate_ref, up_ref, hmid_ref):
        hv = h_ref[...]
        gate = _dot(hv, wg_ref[...])
        up = _dot(hv, wu_ref[...])
        gate_ref[...] = gate.astype(BF16)
        up_ref[...] = up.astype(BF16)
        hmid_ref[...] = (_silu(gate) * up).astype(BF16)

    blk = pl.BlockSpec((tm, tn), lambda i, j: (i, j))
    wspec = pl.BlockSpec((k, tn), lambda i, j: (0, j))
    return pl.pallas_call(
        body, name="swiglu_fwd", grid=(m // tm, n // tn),
        in_specs=[pl.BlockSpec((tm, k), lambda i, j: (i, 0)), wspec, wspec],
        out_specs=[blk, blk, blk],
        out_shape=[jax.ShapeDtypeStruct((m, n), BF16), jax.ShapeDtypeStruct((m, n), BF16),
                   jax.ShapeDtypeStruct((m, n), BF16)],
        compiler_params=_cp(("parallel", "arbitrary")),
    )(h, wg, wu)


def _swiglu_bwd(dffn, wdt, gate, up, tm, tn):
    m, k = dffn.shape
    n = wdt.shape[1]
    tm = min(tm, m)

    def body(d_ref, w_ref, gate_ref, up_ref, dg_ref, du_ref):
        dh = _dot(d_ref[...], w_ref[...])
        gate = gate_ref[...].astype(F32)
        dg_ref[...] = (dh * up_ref[...].astype(F32) * _dsilu(gate)).astype(BF16)
        du_ref[...] = (dh * _silu(gate)).astype(BF16)

    blk = pl.BlockSpec((tm, tn), lambda i, j: (i, j))
    return pl.pallas_call(
        body, name="swiglu_bwd", grid=(m // tm, n // tn),
        in_specs=[pl.BlockSpec((tm, k), lambda i, j: (i, 0)), pl.BlockSpec((k, tn), lambda i, j: (0, j)), blk, blk],
        out_specs=[blk, blk],
        out_shape=[jax.ShapeDtypeStruct((m, n), BF16), jax.ShapeDtypeStruct((m, n), BF16)],
        compiler_params=_cp(("parallel", "arbitrary")),
    )(dffn, wdt, gate, up)


def _halo_specs(tb, width, nrows):
    r8 = tb // 8
    last = nrows // 8 - 1
    prev = pl.BlockSpec((8, width), lambda i: (jnp.maximum(i * r8 - 1, 0), 0))
    nxt = pl.BlockSpec((8, width), lambda i: (jnp.minimum((i + 1) * r8, last), 0))
    return prev, nxt


def _fill_halo(buf, prev_ref, cur_ref, next_ref, tb, i, nb):
    buf[0:8, :] = prev_ref[...] * (i > 0).astype(F32)
    buf[8:8 + tb, :] = cur_ref[...]
    buf[8 + tb:16 + tb, :] = next_ref[...] * (i < nb - 1).astype(F32)


def _conv_fwd(xd, w8, b, tb, name):
    n = xd.shape[0]
    tb = min(tb, n)
    nb = n // tb
    prev, nxt = _halo_specs(tb, D_XBC, n)

    def body(p_ref, c_ref, n_ref, w_ref, b_ref, u_ref, buf):
        i = pl.program_id(0)
        _fill_halo(buf, p_ref, c_ref, n_ref, tb, i, nb)
        acc = jnp.broadcast_to(b_ref[...], (tb, D_XBC))
        for k in range(D_CONV):
            acc = acc + w_ref[k:k + 1, :] * buf[pl.ds(6 + k, tb), :]
        u_ref[...] = acc

    return pl.pallas_call(
        body, name=name, grid=(nb,),
        in_specs=[prev, pl.BlockSpec((tb, D_XBC), lambda i: (i, 0)), nxt,
                  pl.BlockSpec((8, D_XBC), lambda i: (0, 0)), _par_spec(D_XBC)],
        out_specs=_row_spec(tb, D_XBC),
        out_shape=jax.ShapeDtypeStruct((n, D_XBC), F32),
        scratch_shapes=[pltpu.VMEM((tb + 16, D_XBC), F32)],
        compiler_params=_cp(("parallel",)),
    )(xd, xd, xd, w8, b)


def _conv_bwd_a(dxs, dy, dskip_e, dbc, u, tb, name):
    n = u.shape[0]
    tb = min(tb, n)

    def body(dxs_ref, dy_ref, sk_ref, dbc_ref, u_ref, du_ref, acc_ref):
        i = pl.program_id(0)

        @pl.when(i == 0)
        def _():
            acc_ref[...] = jnp.zeros_like(acc_ref)

        uv = u_ref[...]
        ds = _dsilu(uv)
        gx = dxs_ref[0] + dxs_ref[1] + dy_ref[...] * sk_ref[...]
        gbc = dbc_ref[0] + dbc_ref[1]
        du = jnp.concatenate([gx, gbc], axis=1) * ds
        du_ref[...] = du
        acc_ref[0:1, :] += jnp.sum(du, axis=0, keepdims=True)

    return pl.pallas_call(
        body, name=name, grid=(n // tb,),
        in_specs=[pl.BlockSpec((2, tb, D_MODEL), lambda i: (0, i, 0)), _row_spec(tb, D_MODEL), _par_spec(D_MODEL),
                  pl.BlockSpec((2, tb, 2 * D_STATE), lambda i: (0, i, 0)), _row_spec(tb, D_XBC)],
        out_specs=[_row_spec(tb, D_XBC), _acc_spec(D_XBC)],
        out_shape=[jax.ShapeDtypeStruct((n, D_XBC), F32), jax.ShapeDtypeStruct((8, D_XBC), F32)],
        compiler_params=_cp(("arbitrary",)),
    )(dxs, dy, dskip_e, dbc, u)


def _conv_bwd_b(du, xd, ddt, w8, tb, name):
    n = du.shape[0]
    tb = min(tb, n)
    nb = n // tb
    prev, nxt = _halo_specs(tb, D_XBC, n)

    def body(dp_ref, dc_ref, dn_ref, xp_ref, xc_ref, xn_ref, ddt_ref, w_ref, dxd_ref, acc_ref, dbuf, xbuf):
        i = pl.program_id(0)

        @pl.when(i == 0)
        def _():
            acc_ref[...] = jnp.zeros_like(acc_ref)

        _fill_halo(dbuf, dp_ref, dc_ref, dn_ref, tb, i, nb)
        _fill_halo(xbuf, xp_ref, xc_ref, xn_ref, tb, i, nb)
        duc = dc_ref[...]
        acc = jnp.zeros((tb, D_XBC), F32)
        for k in range(D_CONV):
            acc = acc + w_ref[k:k + 1, :] * dbuf[pl.ds(10 - k, tb), :]
            acc_ref[k:k + 1, :] += jnp.sum(duc * xbuf[pl.ds(6 + k, tb), :], axis=0, keepdims=True)
        dxd_ref[:, 0:D_XBC] = acc.astype(BF16)
        ddt = ddt_ref[0] + pltpu.roll(ddt_ref[1], SSD_HEADS, 1)
        dxd_ref[:, D_XBC:D_XD] = ddt.astype(BF16)

    cur = pl.BlockSpec((tb, D_XBC), lambda i: (i, 0))
    return pl.pallas_call(
        body, name=name, grid=(nb,),
        in_specs=[prev, cur, nxt, prev, cur, nxt,
                  pl.BlockSpec((2, tb, 128), lambda i: (0, i, 0)), pl.BlockSpec((8, D_XBC), lambda i: (0, 0))],
        out_specs=[_row_spec(tb, D_XD), _acc_spec(D_XBC)],
        out_shape=[jax.ShapeDtypeStruct((n, D_XD), BF16), jax.ShapeDtypeStruct((8, D_XBC), F32)],
        scratch_shapes=[pltpu.VMEM((tb + 16, D_XBC), F32), pltpu.VMEM((tb + 16, D_XBC), F32)],
        compiler_params=_cp(("arbitrary",)),
    )(du, du, du, xd, xd, xd, ddt, w8)


def _ssd_chunk_index(nc, reverse):
    def idx(d, k):
        kk = (nc - 1 - k) if reverse else k
        return kk + d * (nc - 1 - 2 * kk)
    return idx


def _ssd_prologue(d, u_ref, xd_ref, bias_ref, a_ref, r_ref):
    q = CHUNK
    xbc = _silu(u_ref[...])
    xs = xbc[:, 0:D_MODEL]
    bm = xbc[:, D_MODEL:D_MODEL + D_STATE]
    cm = xbc[:, D_MODEL + D_STATE:D_XBC]
    row = lax.broadcasted_iota(jnp.int32, (q, q), 0)
    col = lax.broadcasted_iota(jnp.int32, (q, q), 1)
    sgn = 1 - 2 * d
    mask = ((row - col) * sgn) >= 0
    mask_t = ((row - col) * sgn) <= 0
    xdv = xd_ref[...]
    dtraw = jnp.where(d == 0, xdv, pltpu.roll(xdv, 128 - SSD_HEADS, 1)) + bias_ref[...]
    head_lane = col < SSD_HEADS
    dt = jnp.where(head_lane, _softplus(dtraw), 0.0)
    a = a_ref[...]
    tri = jnp.where(mask, 1.0, 0.0).astype(BF16)
    acum = _dot_sel_l(tri, dt * a)
    rexp = r_ref[...]
    acum_e = _dot_sel_r(acum, rexp)
    dt_e = _dot_sel_r(dt, rexp)
    alast_e = jnp.where(d == 0, acum_e[q - 1:q, :], acum_e[0:1, :])
    e = jnp.exp(acum_e)
    decay_end = jnp.exp(alast_e - acum_e)
    g = _dot_nt(cm.astype(BF16), bm.astype(BF16))
    return dict(xs=xs, bm=bm, cm=cm, mask=mask, mask_t=mask_t, dtraw=dtraw, head_lane=head_lane, dt=dt, a=a,
                acum=acum, acum_t=acum.T, dt_t=dt.T, dt_e=dt_e, alast_e=alast_e, e=e, decay_end=decay_end, g=g,
                col=col, row=row)


def _ssd_head_mats(p, h):
    seg = p["acum"][:, h:h + 1] - p["acum_t"][h:h + 1, :]
    lm = jnp.exp(jnp.where(p["mask"], seg, -jnp.inf))
    gl = p["g"] * lm
    s = gl * p["dt_t"][h:h + 1, :]
    return lm, gl, s


def _ssd_fwd(u, xd, bias2, a2, rexp, h0, name):
    n = u.shape[0]
    nc = n // CHUNK
    q = CHUNK
    cidx = _ssd_chunk_index(nc, reverse=False)

    def body(u_ref, xd_ref, bias_ref, a_ref, r_ref, h0_ref, y_ref, hp_ref, hf_ref, st):
        d = pl.program_id(0)
        k = pl.program_id(1)

        @pl.when(k == 0)
        def _():
            st[...] = h0_ref[...]

        p = _ssd_prologue(d, u_ref, xd_ref, bias_ref, a_ref, r_ref)
        stv = st[...]
        st_bf = stv.astype(BF16)
        hp_ref[...] = st_bf
        xs = p["xs"]
        lane128 = p["col"]
        y_off = _dot(p["cm"].astype(BF16), st_bf) * p["e"]
        for pb in range(SSD_HEADS // 2):
            _, _, s0 = _ssd_head_mats(p, 2 * pb)
            _, _, s1 = _ssd_head_mats(p, 2 * pb + 1)
            xp = xs[:, pb * 128:(pb + 1) * 128]
            rhs = jnp.concatenate([jnp.where(lane128 < HEAD_DIM, xp, 0.0), jnp.where(lane128 >= HEAD_DIM, xp, 0.0)],
                                  axis=0).astype(BF16)
            lhs = jnp.concatenate([s0, s1], axis=1).astype(BF16)
            y_ref[:, pb * 128:(pb + 1) * 128] = _dot(lhs, rhs) + y_off[:, pb * 128:(pb + 1) * 128]
        xw = (xs * (p["decay_end"] * p["dt_e"])).astype(BF16)
        new = stv * jnp.exp(p["alast_e"]) + _dot(p["bm"].T.astype(BF16), xw)
        st[...] = new
        hf_ref[...] = new

    return pl.pallas_call(
        body, name=name, grid=(2, nc),
        in_specs=[pl.BlockSpec((q, D_XBC), lambda d, k: (cidx(d, k), 0)),
                  pl.BlockSpec((q, 128), lambda d, k: (cidx(d, k), D_XBC // 128)),
                  pl.BlockSpec((None, 1, 128), lambda d, k: (d, 0, 0)),
                  pl.BlockSpec((None, 1, 128), lambda d, k: (d, 0, 0)),
                  pl.BlockSpec((128, D_MODEL), lambda d, k: (0, 0)),
                  pl.BlockSpec((None, D_STATE, D_MODEL), lambda d, k: (d, 0, 0))],
        out_specs=[pl.BlockSpec((None, q, D_MODEL), lambda d, k: (d, cidx(d, k), 0)),
                   pl.BlockSpec((None, None, D_STATE, D_MODEL), lambda d, k: (d, cidx(d, k), 0, 0)),
                   pl.BlockSpec((None, D_STATE, D_MODEL), lambda d, k: (d, 0, 0))],
        out_shape=[jax.ShapeDtypeStruct((2, n, D_MODEL), F32),
                   jax.ShapeDtypeStruct((2, nc, D_STATE, D_MODEL), BF16),
                   jax.ShapeDtypeStruct((2, D_STATE, D_MODEL), F32)],
        scratch_shapes=[pltpu.VMEM((D_STATE, D_MODEL), F32)],
        compiler_params=_cp(("arbitrary", "arbitrary")),
    )(u, xd, bias2, a2, rexp, h0)


def _ssd_bwd(u, xd, bias2, a2, rexp, rexp_t, dy, hprev, lam0, name):
    n = u.shape[0]
    nc = n // CHUNK
    q = CHUNK
    cidx = _ssd_chunk_index(nc, reverse=True)

    def body(u_ref, xd_ref, bias_ref, a_ref, r_ref, rt_ref, dy_ref, hp_ref, lam0_ref,
             dxs_ref, dbc_ref, ddt_ref, acc_ref, lamo_ref, lam):
        d = pl.program_id(0)
        k = pl.program_id(1)

        @pl.when(k == 0)
        def _():
            lam[...] = lam0_ref[...]
            acc_ref[...] = jnp.zeros_like(acc_ref)

        rexp_t = rt_ref[...]

        def hsum(t):
            return _dot_sel_r(t, rexp_t, n=2)

        p = _ssd_prologue(d, u_ref, xd_ref, bias_ref, a_ref, r_ref)
        xs, bm, cm = p["xs"], p["bm"], p["cm"]
        bm_bf, cm_bf = bm.astype(BF16), cm.astype(BF16)
        lamn = lam[...]
        lamn_bf = lamn.astype(BF16)
        stp = hp_ref[...]
        dyv = dy_ref[...]
        lane128 = p["col"]

        wend_e = p["decay_end"] * p["dt_e"]
        cs = _dot(cm_bf, stp)
        dye = dyv * p["e"]
        dye_bf = dye.astype(BF16)
        dc_off = _dot_nt(dye_bf, stp)
        v = _dot(bm_bf, lamn_bf)
        vx = v * xs
        om_e = vx * wend_e
        xw_bf = (xs * wend_e).astype(BF16)
        db_off = _dot_nt(xw_bf, lamn_bf)
        alast_big = jnp.exp(p["alast_e"])
        dlast_e = jnp.sum(stp.astype(F32) * lamn, axis=0, keepdims=True) * alast_big
        lam_new = lamn * alast_big + _dot(cm.T.astype(BF16), dye_bf)
        lam[...] = lam_new
        lamo_ref[...] = lam_new

        x1 = hsum(dye * cs - om_e)
        x2 = hsum(vx * p["decay_end"])
        x3 = hsum(jnp.broadcast_to(jnp.sum(om_e, axis=0, keepdims=True) + dlast_e, (8, D_MODEL)))

        sub16 = lax.broadcasted_iota(jnp.int32, (SSD_HEADS, q), 0)
        rs = jnp.zeros((q, 128), F32)
        cs_m = jnp.zeros((SSD_HEADS, q), F32)
        dt_m = jnp.zeros((SSD_HEADS, q), F32)
        dg = jnp.zeros((q, q), F32)
        for pb in range(SSD_HEADS // 2):
            xp_bf = xs[:, pb * 128:(pb + 1) * 128].astype(BF16)
            dyp = dyv[:, pb * 128:(pb + 1) * 128]
            dxs_pair = None
            for half in range(2):
                h = 2 * pb + half
                sel = (lane128 < HEAD_DIM) if half == 0 else (lane128 >= HEAD_DIM)
                dyh_bf = jnp.where(sel, dyp, 0.0).astype(BF16)
                lm, gl, s = _ssd_head_mats(p, h)
                ds = _dot_nt(dyh_bf, xp_bf)
                t = _dot_tn(s.astype(BF16), dyh_bf)
                dxs_pair = t if dxs_pair is None else dxs_pair + t
                w = ds * s
                rs = rs + jnp.sum(w, axis=1, keepdims=True) * (lane128 == h).astype(F32)
                cs_m = jnp.where(sub16 == h, jnp.sum(w, axis=0, keepdims=True), cs_m)
                dt_m = jnp.where(sub16 == h, jnp.sum(ds * gl, axis=0, keepdims=True), dt_m)
                dg = dg + ds * lm * p["dt_t"][h:h + 1, :]
            sl = slice(pb * 128, (pb + 1) * 128)
            dxs_ref[:, sl] = dxs_pair + v[:, sl] * wend_e[:, sl]

        def to_lanes(m16):
            return jnp.concatenate([m16, jnp.zeros((128 - SSD_HEADS, q), F32)], axis=0).T

        last = jnp.where(d == 0, q - 1, 0)
        dacum = rs - to_lanes(cs_m) + x1 + jnp.where(p["row"] == last, x3[0:1, :], 0.0)
        tri_t = jnp.where(p["mask_t"], 1.0, 0.0).astype(BF16)
        ddta = _dot_sel_l(tri_t, dacum)
        dt = p["dt"]
        a = p["a"]
        ddt = to_lanes(dt_m) + x2 + a * ddta
        ddtraw = jnp.where(p["head_lane"], ddt * _sigmoid(p["dtraw"]), 0.0)
        ddt_ref[...] = ddtraw
        acc_ref[0:1, :] += jnp.sum(ddtraw, axis=0, keepdims=True)
        acc_ref[1:2, :] += jnp.sum(dt * ddta, axis=0, keepdims=True) * a

        dg_bf = dg.astype(BF16)
        dbc_ref[:, 0:D_STATE] = _dot_tn(dg_bf, cm_bf) + db_off
        dbc_ref[:, D_STATE:2 * D_STATE] = _dot(dg_bf, bm_bf) + dc_off

    cblk = lambda d, k: (cidx(d, k), 0)
    return pl.pallas_call(
        body, name=name, grid=(2, nc),
        in_specs=[pl.BlockSpec((q, D_XBC), cblk),
                  pl.BlockSpec((q, 128), lambda d, k: (cidx(d, k), D_XBC // 128)),
                  pl.BlockSpec((None, 1, 128), lambda d, k: (d, 0, 0)),
                  pl.BlockSpec((None, 1, 128), lambda d, k: (d, 0, 0)),
                  pl.BlockSpec((128, D_MODEL), lambda d, k: (0, 0)),
                  pl.BlockSpec((D_MODEL, 128), lambda d, k: (0, 0)),
                  pl.BlockSpec((q, D_MODEL), cblk),
                  pl.BlockSpec((None, None, D_STATE, D_MODEL), lambda d, k: (d, cidx(d, k), 0, 0)),
                  pl.BlockSpec((None, D_STATE, D_MODEL), lambda d, k: (d, 0, 0))],
        out_specs=[pl.BlockSpec((None, q, D_MODEL), lambda d, k: (d, cidx(d, k), 0)),
                   pl.BlockSpec((None, q, 2 * D_STATE), lambda d, k: (d, cidx(d, k), 0)),
                   pl.BlockSpec((None, q, 128), lambda d, k: (d, cidx(d, k), 0)),
                   pl.BlockSpec((None, 8, 128), lambda d, k: (d, 0, 0)),
                   pl.BlockSpec((None, D_STATE, D_MODEL), lambda d, k: (d, 0, 0))],
        out_shape=[jax.ShapeDtypeStruct((2, n, D_MODEL), F32),
                   jax.ShapeDtypeStruct((2, n, 2 * D_STATE), F32),
                   jax.ShapeDtypeStruct((2, n, 128), F32),
                   jax.ShapeDtypeStruct((2, 8, 128), F32),
                   jax.ShapeDtypeStruct((2, D_STATE, D_MODEL), F32)],
        scratch_shapes=[pltpu.VMEM((D_STATE, D_MODEL), F32)],
        compiler_params=_cp(("arbitrary", "arbitrary")),
    )(u, xd, bias2, a2, rexp, rexp_t, dy, hprev, lam0)


def _merge_fwd(y, u, z, dskip_e, gn, tb):
    n = z.shape[0]

    def body(y_ref, u_ref, z_ref, sk_ref, gn_ref, o_ref):
        xs = _silu(u_ref[...])
        ys = y_ref[0] + y_ref[1] + sk_ref[...] * xs
        gated = ys * _silu(z_ref[...])
        rstd = lax.rsqrt(jnp.mean(gated * gated, axis=-1, keepdims=True) + LN_EPS)
        o_ref[...] = (gated * rstd * gn_ref[...]).astype(BF16)

    return pl.pallas_call(
        body, name="merge_fwd", grid=(n // tb,),
        in_specs=[pl.BlockSpec((2, tb, D_MODEL), lambda i: (0, i, 0)), pl.BlockSpec((tb, D_MODEL), lambda i: (i, 0)),
                  _row_spec(tb, D_MODEL), _par_spec(D_MODEL), _par_spec(D_MODEL)],
        out_specs=_row_spec(tb, D_MODEL),
        out_shape=jax.ShapeDtypeStruct((n, D_MODEL), BF16),
        compiler_params=_cp(("parallel",)),
    )(y, u, z, dskip_e, gn)


def _merge_bwd(dyn, y, u, z, dskip_e, gn, tb):
    n = z.shape[0]

    def body(dyn_ref, y_ref, u_ref, z_ref, sk_ref, gn_ref, dy_ref, dz_ref, acc_ref):
        i = pl.program_id(0)

        @pl.when(i == 0)
        def _():
            acc_ref[...] = jnp.zeros_like(acc_ref)

        xs = _silu(u_ref[...])
        zv = z_ref[...]
        ys = y_ref[0] + y_ref[1] + sk_ref[...] * xs
        gated = ys * _silu(zv)
        rstd = lax.rsqrt(jnp.mean(gated * gated, axis=-1, keepdims=True) + LN_EPS)
        ghat = gated * rstd
        dyn_v = dyn_ref[...]
        t = dyn_v * gn_ref[...]
        dgated = rstd * (t - ghat * jnp.mean(t * ghat, axis=-1, keepdims=True))
        dys = dgated * _silu(zv)
        dy_ref[...] = dys
        dz_ref[...] = (dgated * ys * _dsilu(zv)).astype(BF16)
        acc_ref[0:1, :] += jnp.sum(dyn_v * ghat, axis=0, keepdims=True)
        acc_ref[1:2, :] += jnp.sum(dys * xs, axis=0, keepdims=True)

    return pl.pallas_call(
        body, name="merge_bwd", grid=(n // tb,),
        in_specs=[_row_spec(tb, D_MODEL), pl.BlockSpec((2, tb, D_MODEL), lambda i: (0, i, 0)),
                  pl.BlockSpec((tb, D_MODEL), lambda i: (i, 0)), _row_spec(tb, D_MODEL),
                  _par_spec(D_MODEL), _par_spec(D_MODEL)],
        out_specs=[_row_spec(tb, D_MODEL), _row_spec(tb, D_MODEL), _acc_spec(D_MODEL)],
        out_shape=[jax.ShapeDtypeStruct((n, D_MODEL), F32), jax.ShapeDtypeStruct((n, D_MODEL), BF16),
                   jax.ShapeDtypeStruct((8, D_MODEL), F32)],
        compiler_params=_cp(("arbitrary",)),
    )(dyn, y, u, z, dskip_e, gn)


def _pool_consts(transpose):
    tb = POOL_TB
    t = jnp.arange(tb)
    s = jnp.arange(3 * tb)
    rl, cl = t // GRID_W, t % GRID_W
    rs_, cs_ = s // GRID_W - tb // GRID_W, s % GRID_W
    s2 = jnp.arange(tb)
    rl2, cl2 = s2 // GRID_W, s2 % GRID_W
    brow, bcol = [], []
    for w in POOL_WINDOWS:
        lo, hi = -(w // 2), w - w // 2
        if transpose:
            lo, hi = -hi + 1, -lo + 1
        dr = rs_[None, :] - rl[:, None]
        brow.append(((cs_[None, :] == cl[:, None]) & (dr >= lo) & (dr < hi)).astype(BF16))
        dc = cl2[None, :] - cl[:, None]
        bcol.append(((rl2[None, :] == rl[:, None]) & (dc >= lo) & (dc < hi)).astype(BF16))
    return jnp.stack(brow), jnp.stack(bcol)


def _pool_inv(i, g, n):
    assert GRID_W == 64
    t = i * POOL_TB + lax.broadcasted_iota(jnp.int32, (POOL_TB, 1), 0)
    r = lax.shift_right_logical(t, 6)
    col = t & (GRID_W - 1)
    w = POOL_WINDOWS[g]
    lo, hi = -(w // 2), w - w // 2
    cnt_r = jnp.minimum(r + hi, n // GRID_W) - jnp.maximum(r + lo, 0)
    cnt_c = jnp.minimum(col + hi, GRID_W) - jnp.maximum(col + lo, 0)
    return 1.0 / (cnt_r * cnt_c).astype(F32)


def _pool_box(prev_ref, cur_ref, next_ref, brow_ref, bcol_ref, g, i, nb):
    sl = slice(g * POOL_DIM, (g + 1) * POOL_DIM)
    pv = prev_ref[:, sl] * (i > 0).astype(prev_ref.dtype)
    nx = next_ref[:, sl] * (i < nb - 1).astype(next_ref.dtype)
    stack = jnp.concatenate([pv.astype(BF16), cur_ref[:, sl].astype(BF16), nx.astype(BF16)], axis=0)
    r = _dot(brow_ref[g], stack)
    return _dot(bcol_ref[g], r.astype(BF16))


def _pool_halo_specs(n, d):
    tb = POOL_TB
    nb = n // tb
    prev = pl.BlockSpec((tb, d), lambda i: (jnp.maximum(i - 1, 0), 0))
    cur = pl.BlockSpec((tb, d), lambda i: (i, 0))
    nxt = pl.BlockSpec((tb, d), lambda i: (jnp.minimum(i + 1, nb - 1), 0))
    return prev, cur, nxt


def _pool_const_specs():
    tb = POOL_TB
    return [pl.BlockSpec((N_POOL, tb, 3 * tb), lambda i: (0, 0, 0)),
            pl.BlockSpec((N_POOL, tb, tb), lambda i: (0, 0, 0))]


def _pool_fwd(up, consts, pw_bf, pscale):
    n = up.shape[0]
    tb = POOL_TB
    nb = n // tb
    brow, bcol = consts
    prev, cur, nxt = _pool_halo_specs(n, D_MODEL)

    def body(p_ref, c_ref, n_ref, brow_ref, bcol_ref, pw_ref, sc_ref, o_ref, d_ref):
        i = pl.program_id(0)
        for g in range(N_POOL):
            sl = slice(g * POOL_DIM, (g + 1) * POOL_DIM)
            box = _pool_box(p_ref, c_ref, n_ref, brow_ref, bcol_ref, g, i, nb)
            dd = (box * _pool_inv(i, g, n) - c_ref[:, sl]).astype(BF16)
            d_ref[:, sl] = dd
            o_ref[:, sl] = (_dot(dd, pw_ref[g]) * sc_ref[:, sl]).astype(BF16)

    return pl.pallas_call(
        body, name="pool_fwd", grid=(nb,),
        in_specs=[prev, cur, nxt] + _pool_const_specs() +
                 [pl.BlockSpec((N_POOL, POOL_DIM, POOL_DIM), lambda i: (0, 0, 0)), _par_spec(D_MODEL)],
        out_specs=[_row_spec(tb, D_MODEL), _row_spec(tb, D_MODEL)],
        out_shape=[jax.ShapeDtypeStruct((n, D_MODEL), BF16), jax.ShapeDtypeStruct((n, D_MODEL), BF16)],
        compiler_params=_cp(("parallel",)),
    )(up, up, up, brow, bcol, pw_bf, pscale)


def _pool_bwd_a(dp, dsave, pw_bf, pwt_bf, pscale):
    n = dp.shape[0]
    tb = POOL_TB

    def body(dp_ref, d_ref, pw_ref, pwt_ref, sc_ref, dd_ref, dds_ref, gw_ref, gs_ref):
        i = pl.program_id(0)

        @pl.when(i == 0)
        def _():
            gw_ref[...] = jnp.zeros_like(gw_ref)
            gs_ref[...] = jnp.zeros_like(gs_ref)

        for g in range(N_POOL):
            sl = slice(g * POOL_DIM, (g + 1) * POOL_DIM)
            dpv = dp_ref[:, sl]
            dv = d_ref[:, sl]
            dpw_bf = (dpv * sc_ref[:, sl]).astype(BF16)
            dd = _dot(dpw_bf, pwt_ref[g])
            dd_ref[:, sl] = dd
            dds_ref[:, sl] = (dd * _pool_inv(i, g, n)).astype(BF16)
            gw_ref[g] += _dot_tn(dv, dpw_bf)
            gs_ref[0:1, sl] += jnp.sum(dpv * _dot(dv, pw_ref[g]), axis=0, keepdims=True)

    wspec = pl.BlockSpec((N_POOL, POOL_DIM, POOL_DIM), lambda i: (0, 0, 0))
    return pl.pallas_call(
        body, name="pool_bwd_a", grid=(n // tb,),
        in_specs=[_row_spec(tb, D_MODEL), _row_spec(tb, D_MODEL), wspec, wspec, _par_spec(D_MODEL)],
        out_specs=[_row_spec(tb, D_MODEL), _row_spec(tb, D_MODEL), wspec, _acc_spec(D_MODEL)],
        out_shape=[jax.ShapeDtypeStruct((n, D_MODEL), F32), jax.ShapeDtypeStruct((n, D_MODEL), BF16),
                   jax.ShapeDtypeStruct((N_POOL, POOL_DIM, POOL_DIM), F32), jax.ShapeDtypeStruct((8, D_MODEL), F32)],
        compiler_params=_cp(("arbitrary",)),
    )(dp, dsave, pw_bf, pwt_bf, pscale)


def _pool_bwd_b(dds, dd, consts_t):
    n = dd.shape[0]
    tb = POOL_TB
    nb = n // tb
    brow, bcol = consts_t
    prev, cur, nxt = _pool_halo_specs(n, D_MODEL)

    def body(p_ref, c_ref, n_ref, brow_ref, bcol_ref, dd_ref, o_ref):
        i = pl.program_id(0)
        for g in range(N_POOL):
            sl = slice(g * POOL_DIM, (g + 1) * POOL_DIM)
            box = _pool_box(p_ref, c_ref, n_ref, brow_ref, bcol_ref, g, i, nb)
            o_ref[:, sl] = (box - dd_ref[:, sl]).astype(BF16)

    return pl.pallas_call(
        body, name="pool_bwd_b", grid=(nb,),
        in_specs=[prev, cur, nxt] + _pool_const_specs() + [_row_spec(tb, D_MODEL)],
        out_specs=_row_spec(tb, D_MODEL),
        out_shape=jax.ShapeDtypeStruct((n, D_MODEL), BF16),
        compiler_params=_cp(("parallel",)),
    )(dds, dds, dds, brow, bcol, dd)


def _pair_add(slabs, recvs, core):
    na = len(slabs)
    hr = [t.shape[1] // 4 for t in slabs]

    def body(core_ref, *refs):
        for a in range(na):
            refs[2 * na + a][...] = (refs[a][...] + refs[na + a][...]).astype(BF16)

    own = [pl.BlockSpec((None, hr[a], slabs[a].shape[2]), lambda j, i, c_ref: (j, 2 * c_ref[0] + i, 0)) for a in range(na)]
    got = [pl.BlockSpec((None, hr[a], slabs[a].shape[2]), lambda j, i, c_ref: (j, i, 0)) for a in range(na)]
    return pl.pallas_call(
        body, name="reduce_g_pair",
        grid_spec=pltpu.PrefetchScalarGridSpec(num_scalar_prefetch=1, grid=(4, 2), in_specs=own + got, out_specs=got),
        out_shape=[jax.ShapeDtypeStruct(r.shape, BF16) for r in recvs],
        compiler_params=_cp(("arbitrary", "arbitrary")),
    )(core, *slabs, *recvs)


def _sum4(parts, core):
    na = len(parts)
    hr = [t.shape[1] // 2 for t in parts]

    def body(core_ref, *refs):
        for a in range(na):
            p = refs[a]
            refs[na + a][...] = ((p[0].astype(F32) + p[1].astype(F32)) + p[2].astype(F32)) + p[3].astype(F32)

    return pl.pallas_call(
        body, name="reduce_g_sum",
        grid_spec=pltpu.PrefetchScalarGridSpec(
            num_scalar_prefetch=1, grid=(2,),
            in_specs=[pl.BlockSpec((4, hr[a], parts[a].shape[2]), lambda i, c_ref: (0, i, 0)) for a in range(na)],
            out_specs=[pl.BlockSpec((hr[a], parts[a].shape[2]), lambda i, c_ref: (2 * c_ref[0] + i, 0))
                       for a in range(na)]),
        out_shape=[jax.ShapeDtypeStruct((2 * t.shape[1], t.shape[2]), F32) for t in parts],
        compiler_params=_cp(("arbitrary",)),
    )(core, *parts)


def _adamw(w, g, m, v, name):
    r, cdim = w.shape
    tb = _row_block(r, 256)
    c1 = 1.0 - ADAM_B1 ** ADAM_STEP
    c2 = 1.0 - ADAM_B2 ** ADAM_STEP

    def body(w_ref, g_ref, m_ref, v_ref, d_ref, nm_ref, nv_ref):
        gv = g_ref[...]
        nm = ADAM_B1 * m_ref[...] + (1.0 - ADAM_B1) * gv
        nv = ADAM_B2 * v_ref[...] + (1.0 - ADAM_B2) * (gv * gv)
        m_hat = nm / c1
        v_hat = nv / c2
        d_ref[...] = -ADAM_LR * (m_hat / (jnp.sqrt(v_hat) + ADAM_EPS) + ADAM_WD * w_ref[...])
        nm_ref[...] = nm
        nv_ref[...] = nv

    spec = _row_spec(tb, cdim)
    shp = jax.ShapeDtypeStruct((r, cdim), F32)
    return pl.pallas_call(
        body, name=name, grid=(r // tb,),
        in_specs=[spec] * 4, out_specs=[spec] * 3, out_shape=[shp] * 3,
        compiler_params=_cp(("parallel",)),
    )(w, g, m, v)


def _mesh_pos():
    return lax.axis_index("x"), lax.axis_index("y"), lax.axis_index("c")


_ANY = pl.BlockSpec(memory_space=pl.ANY)


def _remote(src, dst, send_sem, recv_sem, device):
    return pltpu.make_async_remote_copy(src_ref=src, dst_ref=dst, send_sem=send_sem, recv_sem=recv_sem,
                                        device_id=device, device_id_type=MESH)


def _other_chips(x, y):
    return [(1 - x, y), (x, 1 - y), (1 - x, 1 - y)]


def _half(nrows, h):
    return pl.ds(h * (nrows // 2), nrows // 2)


def _gather_weights(shards, conv8):
    na = len(shards)
    nrow = [t.shape[0] for t in shards]

    def body(*refs):
        ins, conv_in = refs[:na], refs[na]
        outs, conv_out = refs[na + 1:2 * na + 1], refs[2 * na + 1]
        send_sems, recv_sems, local_sems = refs[2 * na + 2:]
        x, y, c = _mesh_pos()
        me = 2 * x + y
        sib = (x, y, 1 - c)
        chips = _other_chips(x, y)

        def ici(k, a, slot):
            px, py = chips[k]
            rows = _half(nrow[a], c)
            return _remote(ins[a].at[rows, :], outs[a].at[slot, rows, :], send_sems.at[k * na + a],
                           recv_sems.at[k * na + a], (px, py, c))

        def fwd(k, a, h):
            px, py = chips[k]
            blk = outs[a].at[2 * px + py, _half(nrow[a], h), :]
            return _remote(blk, blk, send_sems.at[(3 + k) * na + a], recv_sems.at[(3 + k) * na + a], sib)

        def conv(k, slot):
            px, py = chips[k]
            return _remote(conv_in, conv_out.at[slot], send_sems.at[6 * na + k], recv_sems.at[6 * na + k], (px, py, c))

        def own(a):
            return _remote(ins[a], outs[a].at[me], send_sems.at[6 * na + 3 + a], recv_sems.at[6 * na + 3 + a], sib)

        local = [pltpu.make_async_copy(conv_in, conv_out.at[me], local_sems.at[0])]
        for cp in local:
            cp.start()
        sends = [own(a) for a in range(na)]
        for k in range(3):
            for a in range(na):
                sends.append(ici(k, a, me))
            sends.append(conv(k, me))
        for cp in sends:
            cp.start()
        for k in range(3):
            px, py = chips[k]
            for a in range(na):
                ici(k, a, 2 * px + py).wait_recv()
                cp = fwd(k, a, c)
                cp.start()
                sends.append(cp)
        for k in range(3):
            px, py = chips[k]
            for a in range(na):
                fwd(k, a, 1 - c).wait_recv()
            conv(k, 2 * px + py).wait_recv()
        for a in range(na):
            own(a).wait_recv()
        for cp in sends:
            cp.wait_send()
        for cp in local:
            cp.wait()

    nsem = 7 * na + 3
    return pl.pallas_call(
        body, name="gather_w", in_specs=[_ANY] * (na + 1), out_specs=[_ANY] * (na + 1),
        out_shape=[jax.ShapeDtypeStruct((4,) + t.shape, t.dtype) for t in shards] +
                  [jax.ShapeDtypeStruct((4,) + conv8.shape, conv8.dtype)],
        scratch_shapes=[pltpu.SemaphoreType.DMA((nsem,)), pltpu.SemaphoreType.DMA((nsem,)),
                        pltpu.SemaphoreType.DMA((1,))],
    )(*shards, conv8)


def _pair_swap(slabs):
    na = len(slabs)

    def body(*refs):
        ins, outs = refs[:na], refs[na:2 * na]
        send_sems, recv_sems = refs[2 * na:]
        x, y, c = _mesh_pos()
        cps = [_remote(ins[a].at[:, _half(slabs[a].shape[1], 1 - c), :], outs[a], send_sems.at[a], recv_sems.at[a],
                       (x, y, 1 - c)) for a in range(na)]
        for cp in cps:
            cp.start()
        for cp in cps:
            cp.wait()

    return pl.pallas_call(
        body, name="reduce_g_d2d", in_specs=[_ANY] * na, out_specs=[_ANY] * na,
        out_shape=[jax.ShapeDtypeStruct((4, t.shape[1] // 2, t.shape[2]), t.dtype) for t in slabs],
        scratch_shapes=[pltpu.SemaphoreType.DMA((na,)), pltpu.SemaphoreType.DMA((na,))],
    )(*slabs)


def _chip_exchange(pairs):
    na = len(pairs)

    def body(*refs):
        ins, outs = refs[:na], refs[na:2 * na]
        send_sems, recv_sems, local_sems = refs[2 * na:]
        x, y, c = _mesh_pos()
        me = 2 * x + y
        chips = _other_chips(x, y)

        def copy(k, a, slot):
            px, py = chips[k]
            return _remote(ins[a].at[2 * px + py], outs[a].at[slot], send_sems.at[k * na + a], recv_sems.at[k * na + a],
                           (px, py, c))

        local = [pltpu.make_async_copy(ins[a].at[me], outs[a].at[me], local_sems.at[a]) for a in range(na)]
        for cp in local:
            cp.start()
        sends = [copy(k, a, me) for k in range(3) for a in range(na)]
        for cp in sends:
            cp.start()
        for k in range(3):
            px, py = chips[k]
            for a in range(na):
                copy(k, a, 2 * px + py).wait_recv()
        for cp in sends:
            cp.wait_send()
        for cp in local:
            cp.wait()

    return pl.pallas_call(
        body, name="reduce_g_ici", in_specs=[_ANY] * na, out_specs=[_ANY] * na,
        out_shape=[jax.ShapeDtypeStruct(t.shape, t.dtype) for t in pairs],
        scratch_shapes=[pltpu.SemaphoreType.DMA((3 * na,)), pltpu.SemaphoreType.DMA((3 * na,)),
                        pltpu.SemaphoreType.DMA((na,))],
    )(*pairs)


def _share_halves(totals):
    na = len(totals)

    def body(*refs):
        bufs = refs[na:2 * na]
        send_sems, recv_sems = refs[2 * na:]
        x, y, c = _mesh_pos()

        def copy(a, h):
            blk = bufs[a].at[_half(totals[a].shape[0], h), :]
            return _remote(blk, blk, send_sems.at[a], recv_sems.at[a], (x, y, 1 - c))

        sends = [copy(a, c) for a in range(na)]
        for cp in sends:
            cp.start()
        for a in range(na):
            copy(a, 1 - c).wait_recv()
        for cp in sends:
            cp.wait_send()

    return pl.pallas_call(
        body, name="reduce_g_share", in_specs=[_ANY] * na, out_specs=[_ANY] * na,
        out_shape=[jax.ShapeDtypeStruct(t.shape, t.dtype) for t in totals],
        input_output_aliases={a: a for a in range(na)},
        scratch_shapes=[pltpu.SemaphoreType.DMA((na,)), pltpu.SemaphoreType.DMA((na,))],
    )(*totals)


def _allreduce_small(v, name):
    r, cdim = v.shape

    def body(v_ref, out_ref, buf, send_sems, recv_sems):
        x, y, c = _mesh_pos()
        me = 4 * x + 2 * y + c
        buf[me] = v_ref[...]
        rel = [(bx, by, bc) for bx in (0, 1) for by in (0, 1) for bc in (0, 1)][1:]

        def peer(b):
            bx, by, bc = b
            return ((1 - x) if bx else x, (1 - y) if by else y, (1 - c) if bc else c)

        def copy(k, slot):
            return pltpu.make_async_remote_copy(
                src_ref=v_ref, dst_ref=buf.at[slot], send_sem=send_sems.at[k], recv_sem=recv_sems.at[k],
                device_id=peer(rel[k]), device_id_type=MESH)

        sends = [copy(k, me) for k in range(7)]
        for cp in sends:
            cp.start()
        for k in range(7):
            px, py, pc = peer(rel[k])
            copy(k, 4 * px + 2 * py + pc).wait_recv()
        for cp in sends:
            cp.wait_send()
        acc = buf[0]
        for j in range(1, 8):
            acc = acc + buf[j]
        out_ref[...] = acc

    vm = pl.BlockSpec(memory_space=pltpu.VMEM)
    return pl.pallas_call(
        body, name=name, in_specs=[vm], out_specs=vm,
        out_shape=jax.ShapeDtypeStruct((r, cdim), F32),
        scratch_shapes=[pltpu.VMEM((8, r, cdim), F32), pltpu.SemaphoreType.DMA((7,)), pltpu.SemaphoreType.DMA((7,))],
    )(v)


_BIG = (("in_proj", (D_MODEL, D_IN_PROJ // 4), 1), ("w_out", (2 * D_MODEL // 4, D_MODEL), 0),
        ("w_gate", (D_MODEL, D_FF // 4), 1), ("w_up", (D_MODEL, D_FF // 4), 1), ("w_down", (D_FF // 4, D_MODEL), 0),
        ("pool_w", (N_POOL * POOL_DIM // 4, POOL_DIM), None), ("w_ada", (D_MODEL, 6 * D_MODEL // 4), 1))


def _assemble(name, t):
    _, r, c = t.shape
    axis = {n: ax for n, _, ax in _BIG}[name]
    if axis == 0:
        return t.reshape(4 * r, c)
    if axis == 1:
        return t.transpose(1, 0, 2).reshape(r, 4 * c)
    return t.reshape(4, N_POOL, POOL_DIM // 4, POOL_DIM).transpose(1, 0, 2, 3).reshape(N_POOL, POOL_DIM, POOL_DIM)


def _to_slabs(name, g):
    (r, c), axis = {n: (sh, ax) for n, sh, ax in _BIG}[name]
    if axis == 0:
        return g.reshape(4, r, c)
    if axis == 1:
        return g.reshape(r, 4, c).transpose(1, 0, 2)
    return g.reshape(N_POOL, 4, POOL_DIM // 4, POOL_DIM).transpose(1, 0, 2, 3).reshape(4, r, c)


def _reduce_grads(full_grads, ci):
    names = [n for n, _, _ in _BIG]
    slabs = [_to_slabs(n, full_grads[n]) for n in names]
    core = ci.reshape(1).astype(jnp.int32)
    recvs = _pair_swap(slabs)
    pairs = _pair_add(slabs, recvs, core)
    parts = _chip_exchange(pairs)
    totals = _sum4(parts, core)
    return dict(zip(names, _share_halves(totals)))


def _pad_cols(w, n):
    return jnp.concatenate([w, jnp.zeros((w.shape[0], n - w.shape[1]), w.dtype)], axis=1)


def _device_step(x, c, ctx, target, wts, w8, small, tb):
    n = x.shape[0]
    d = D_MODEL
    c_ctx = small["c_ctx"]

    win = wts["in_proj"]
    wz, wxd, wup = win[:, 0:d], _pad_cols(win[:, d:d + D_XBC + 2 * SSD_HEADS], D_XD), win[:, d + D_XBC + 2 * SSD_HEADS:]
    wout = wts["w_out"]
    wg, wu, wd = wts["w_gate"], wts["w_up"], wts["w_down"]
    pw = wts["pool_w"]
    wada = wts["w_ada"]

    emb_g, emb_b = _vec(small["emb_ln_g"]), _vec(small["emb_ln_b"])
    ln1_g, ln1_b = _vec(small["ln1_g"]), _vec(small["ln1_b"])
    ln2_g, ln2_b = _vec(small["ln2_g"]), _vec(small["ln2_b"])
    gn = _vec(small["ssd_norm_g"])
    pscale = _vec(small["pool_scale"])
    conv_b = _vec(small["conv_b"])
    dskip_e = jnp.repeat(small["d_skip"].reshape(-1), HEAD_DIM).reshape(1, d)
    zpad = jnp.zeros((2, 1, 128 - SSD_HEADS), F32)
    bias2 = jnp.concatenate([small["dt_bias"].reshape(2, 1, SSD_HEADS), zpad], axis=2)
    a2 = jnp.concatenate([-jnp.exp(small["a_log"].reshape(2, 1, SSD_HEADS)), zpad], axis=2)
    rexp = (jnp.arange(128)[:, None] == (jnp.arange(d)[None, :] // HEAD_DIM)).astype(BF16)
    rexp_t = rexp.T

    c8 = jnp.concatenate([c.reshape(1, d), c_ctx.reshape(1, d), jnp.zeros((6, d), F32)], axis=0)
    mods = _mods_fwd(c8, wada, _vec(small["b_ada"]))
    sh1, sc1, g1, sh2, sc2, g2 = [mods[0:1, i * d:(i + 1) * d] for i in range(6)]
    sh1c, sc1c = mods[1:2, 0:d], mods[1:2, d:2 * d]

    tbc = min(tb, ctx.shape[0])
    xc0, hc = _ln_mod(ctx, emb_g, emb_b, sh1c, sc1c, tbc, "ln_mod_ctx")
    xdc = _matmul_nn([(hc, wxd)], F32, 512, D_XD, "in_proj_ctx")
    uc = _conv_fwd(xdc, w8, conv_b, tbc, "conv_fwd_ctx")
    hzero = jnp.zeros((2, D_STATE, d), F32)
    _, hprev_c, hfin_c = _ssd_fwd(uc, xdc, bias2, a2, rexp, hzero, "ssd_fwd_ctx")

    x0, h1 = _ln_mod(x, emb_g, emb_b, sh1, sc1, tb, "ln_mod")
    z = _matmul_nn([(h1, wz)], F32, 512, 1024, "in_proj_z")
    xd = _matmul_nn([(h1, wxd)], F32, 512, D_XD, "in_proj_xd")
    up = _matmul_nn([(h1, wup)], F32, 512, 1024, "in_proj_up")
    u = _conv_fwd(xd, w8, conv_b, tb, "conv_fwd")
    y, hprev, _ = _ssd_fwd(u, xd, bias2, a2, rexp, hfin_c, "ssd_fwd")
    yn = _merge_fwd(y, u, z, dskip_e, gn, tb)
    pconst = _pool_consts(False)
    pool, dsave = _pool_fwd(up, pconst, pw, pscale)
    mix = _matmul_nn([(yn, wout[0:d]), (pool, wout[d:2 * d])], F32, 512, 1024, "out_proj")
    x1, h2 = _res_ln(x0, mix, g1, ln1_g, ln1_b, sh2, sc2, tb)

    gate, upp, hmid = _swiglu_fwd(h2, wg, wu, 512, D_FF // 2)
    ffn = _matmul_nn([(hmid, wd)], F32, 512, 1024, "ffn_down")
    dffn, dr2, acc2 = _final_ln_loss(x1, ffn, g2, ln2_g, ln2_b, target, tb)
    loss = (0.5 / d) * jnp.sum(acc2[3])

    dgate, dupp = _swiglu_bwd(dffn, wd.T, gate, upp, 512, D_FF // 2)
    g_wdown = _matmul_tn(hmid, dffn, 512, 1024, "g_w_down")
    g_wgate = _matmul_tn(h2, dgate, 512, 1408, "g_w_gate")
    g_wup = _matmul_tn(h2, dupp, 512, 1408, "g_w_up")
    dh2 = _matmul_nn([(dgate, wg.T), (dupp, wu.T)], F32, 512, 1024, "d_h2")
    dmix, dr1, acc1 = _bwd_ln1(dr2, dh2, x1, x0, mix, g1, sc2, ln1_g, tb)

    dyn = _matmul_nn([(dmix, wout[0:d].T)], F32, 512, 1024, "d_yn")
    dpool = _matmul_nn([(dmix, wout[d:2 * d].T)], F32, 512, 1024, "d_pool")
    g_wout = jnp.concatenate([_matmul_tn(yn, dmix, 512, 1024, "g_w_out_a"),
                              _matmul_tn(pool, dmix, 512, 1024, "g_w_out_b")], axis=0)
    dd, dds, g_pw, accp = _pool_bwd_a(dpool, dsave, pw, jnp.swapaxes(pw, 1, 2), pscale)
    dup = _pool_bwd_b(dds, dd, _pool_consts(True))
    dy, dz, accm = _merge_bwd(dyn, y, u, z, dskip_e, gn, tb)
    lam0 = jnp.zeros((2, D_STATE, d), F32)
    dxs, dbc, ddt, accs, lam_c = _ssd_bwd(u, xd, bias2, a2, rexp, rexp_t, dy, hprev, lam0, "ssd_bwd")
    du, accb = _conv_bwd_a(dxs, dy, dskip_e, dbc, u, tb, "conv_bwd_a")
    dxd, accw = _conv_bwd_b(du, xd, ddt, w8, tb, "conv_bwd_b")

    lc = ctx.shape[0]
    zeros_c = jnp.zeros((lc, d), F32)
    dxs_c, dbc_c, ddt_c, accs_c, _ = _ssd_bwd(uc, xdc, bias2, a2, rexp, rexp_t, zeros_c, hprev_c, lam_c, "ssd_bwd_ctx")
    du_c, accb_c = _conv_bwd_a(dxs_c, zeros_c, dskip_e, dbc_c, uc, tbc, "conv_bwd_a_ctx")
    dxd_c, accw_c = _conv_bwd_b(du_c, xdc, ddt_c, w8, tbc, "conv_bwd_b_ctx")
    dhc = _matmul_nn([(dxd_c, wxd.T)], F32, 512, 1024, "d_hc")
    _, acc0c = _bwd_ln0(None, dhc, ctx, emb_g, emb_b, sc1c, tbc, "bwd_ln0_ctx")

    dh1 = _matmul_nn([(dz, wz.T), (dxd, wxd.T), (dup, wup.T)], F32, 512, 1024, "d_h1")
    g_wz = _matmul_tn(h1, dz, 512, 1024, "g_in_proj_z")
    g_wxd = _matmul_tn(h1, dxd, 512, D_XD, "g_in_proj_xd") + _matmul_tn(hc, dxd_c, 512, D_XD, "g_in_proj_xd_ctx")
    g_wpo = _matmul_tn(h1, dup, 512, 1024, "g_in_proj_up")
    g_win = jnp.concatenate([g_wz, g_wxd[:, 0:D_XBC + 2 * SSD_HEADS], g_wpo], axis=1)
    grad_x, acc0 = _bwd_ln0(dr1, dh1, x, emb_g, emb_b, sc1, tb, "bwd_ln0")

    zero_d = jnp.zeros((1, d), F32)
    dmod = jnp.concatenate([acc0[1:2], acc0[0:1], acc1[4:5], acc1[1:2], acc1[0:1], acc2[2:3]], axis=1)
    dmodc = jnp.concatenate([acc0c[1:2], acc0c[0:1]] + [zero_d] * 4, axis=1)
    dm8 = jnp.concatenate([dmod, dmodc, jnp.zeros((6, 6 * d), F32)], axis=0)
    g_wada, g_bada8 = _mods_bwd_w(c8.T, dm8)
    g_cctx8 = _mods_bwd_c(dm8, wada, c8)

    big = dict(in_proj=g_win, w_out=g_wout, w_gate=g_wgate, w_up=g_wup, w_down=g_wdown, pool_w=g_pw, w_ada=g_wada)
    sml = dict(
        c_ctx=g_cctx8[1], emb_ln_g=acc0[2] + acc0c[2], emb_ln_b=acc0[3] + acc0c[3], b_ada=g_bada8[0],
        conv_w=accw[0:D_CONV] + accw_c[0:D_CONV], conv_b=accb[0] + accb_c[0],
        dt_bias=accs[:, 0, 0:SSD_HEADS] + accs_c[:, 0, 0:SSD_HEADS],
        a_log=accs[:, 1, 0:SSD_HEADS] + accs_c[:, 1, 0:SSD_HEADS],
        d_skip=jnp.sum(accm[1].reshape(SSD_HEADS, HEAD_DIM), axis=1),
        ssd_norm_g=accm[0], pool_scale=accp[0], ln1_g=acc1[2], ln1_b=acc1[3], ln2_g=acc2[0], ln2_b=acc2[1])
    return loss, grad_x, big, sml


_SMALL = ("c_ctx", "emb_ln_g", "emb_ln_b", "b_ada", "conv_w", "conv_b", "dt_bias", "a_log", "d_skip",
          "ssd_norm_g", "pool_scale", "ln1_g", "ln1_b", "ln2_g", "ln2_b")


_SMALL_ROWS = 8


def _pack_small(vals, names):
    rows = []
    for nme in names:
        flat = vals[nme].reshape(-1).astype(F32)
        assert flat.shape[0] <= _SMALL_ROWS * 1024
        rows.append(jnp.concatenate([flat, jnp.zeros((_SMALL_ROWS * 1024 - flat.shape[0],), F32)]).reshape(_SMALL_ROWS, 1024))
    return jnp.concatenate(rows, axis=0)


def _unpack_small(packed, shapes, names):
    out = {}
    for i, nme in enumerate(names):
        size = math.prod(shapes[nme])
        out[nme] = packed[i * _SMALL_ROWS:(i + 1) * _SMALL_ROWS].reshape(-1)[:size].reshape(shapes[nme])
    return out


_WEIGHT_ORDER = ("c_ctx", "emb_ln_g", "emb_ln_b", "w_ada", "b_ada", "in_proj", "conv_w", "conv_b", "dt_bias", "a_log",
                 "d_skip", "ssd_norm_g", "pool_w", "pool_scale", "w_out", "ln1_g", "ln1_b", "w_gate", "w_up", "w_down",
                 "ln2_g", "ln2_b")


def _as2d(a):
    return a.reshape(-1, a.shape[-1])


def kernel(x, c, ctx, c_ctx, emb_ln_g, emb_ln_b, w_ada, b_ada, in_proj, conv_w, conv_b, dt_bias, a_log, d_skip, ssd_norm_g, pool_w, pool_scale, w_out, ln1_g, ln1_b, w_gate, w_up, w_down, ln2_g, ln2_b, loss_target, m_c_ctx, m_emb_ln_g, m_emb_ln_b, m_w_ada, m_b_ada, m_in_proj, m_conv_w, m_conv_b, m_dt_bias, m_a_log, m_d_skip, m_ssd_norm_g, m_pool_w, m_pool_scale, m_w_out, m_ln1_g, m_ln1_b, m_w_gate, m_w_up, m_w_down, m_ln2_g, m_ln2_b, v_c_ctx, v_emb_ln_g, v_emb_ln_b, v_w_ada, v_b_ada, v_in_proj, v_conv_w, v_conv_b, v_dt_bias, v_a_log, v_d_skip, v_ssd_norm_g, v_pool_w, v_pool_scale, v_w_out, v_ln1_g, v_ln1_b, v_w_gate, v_w_up, v_w_down, v_ln2_g, v_ln2_b):
    w = dict(c_ctx=c_ctx, emb_ln_g=emb_ln_g, emb_ln_b=emb_ln_b, w_ada=w_ada, b_ada=b_ada, in_proj=in_proj, conv_w=conv_w,
             conv_b=conv_b, dt_bias=dt_bias, a_log=a_log, d_skip=d_skip, ssd_norm_g=ssd_norm_g, pool_w=pool_w,
             pool_scale=pool_scale, w_out=w_out, ln1_g=ln1_g, ln1_b=ln1_b, w_gate=w_gate, w_up=w_up, w_down=w_down,
             ln2_g=ln2_g, ln2_b=ln2_b)
    m = dict(c_ctx=m_c_ctx, emb_ln_g=m_emb_ln_g, emb_ln_b=m_emb_ln_b, w_ada=m_w_ada, b_ada=m_b_ada, in_proj=m_in_proj,
             conv_w=m_conv_w, conv_b=m_conv_b, dt_bias=m_dt_bias, a_log=m_a_log, d_skip=m_d_skip,
             ssd_norm_g=m_ssd_norm_g, pool_w=m_pool_w, pool_scale=m_pool_scale, w_out=m_w_out, ln1_g=m_ln1_g,
             ln1_b=m_ln1_b, w_gate=m_w_gate, w_up=m_w_up, w_down=m_w_down, ln2_g=m_ln2_g, ln2_b=m_ln2_b)
    v = dict(c_ctx=v_c_ctx, emb_ln_g=v_emb_ln_g, emb_ln_b=v_emb_ln_b, w_ada=v_w_ada, b_ada=v_b_ada, in_proj=v_in_proj,
             conv_w=v_conv_w, conv_b=v_conv_b, dt_bias=v_dt_bias, a_log=v_a_log, d_skip=v_d_skip,
             ssd_norm_g=v_ssd_norm_g, pool_w=v_pool_w, pool_scale=v_pool_scale, w_out=v_w_out, ln1_g=v_ln1_g,
             ln1_b=v_ln1_b, w_gate=v_w_gate, w_up=v_w_up, w_down=v_w_down, ln2_g=v_ln2_g, ln2_b=v_ln2_b)

    xi, yi, ci = _mesh_pos()
    chip = 2 * xi + yi

    shards = [w[name][0].astype(BF16).reshape(shp) for name, shp, _ in _BIG]
    conv8 = jnp.concatenate([conv_w[0], jnp.zeros((8 - D_CONV, conv_w.shape[-1]), F32)], axis=0)
    *gathered, conv4 = _gather_weights(shards, conv8)
    wts = {name: _assemble(name, t) for (name, _, _), t in zip(_BIG, gathered)}
    w8 = conv4.transpose(1, 0, 2).reshape(8, D_XBC)
    small = {nme: (w[nme] if nme in ("c_ctx", "emb_ln_g", "emb_ln_b") else w[nme][0]) for nme in _SMALL if nme != "conv_w"}

    loss, grad_x, big, sml = _device_step(x[0], c, ctx[0], loss_target[0], wts, w8, small, 512)
    loss = lax.psum(loss, ("x", "y", "c"))

    g_big = _reduce_grads(big, ci)
    small_shapes = {nme: sml[nme].shape for nme in _SMALL}
    g_small = _unpack_small(_allreduce_small(_pack_small(sml, _SMALL), "reduce_small"), small_shapes, _SMALL)
    cw_cols = conv_w.shape[-1]
    g_small["conv_w"] = lax.dynamic_slice_in_dim(g_small["conv_w"], chip * cw_cols, cw_cols, axis=1)

    grads, delta, new_m, new_v = {}, {}, {}, {}
    for name, _, _ in _BIG:
        g2 = _as2d(g_big[name])
        d2, m2, v2 = _adamw(_as2d(w[name][0]), g2, _as2d(m[name][0]), _as2d(v[name][0]), "adamw_" + name)
        grads[name] = g2.reshape(w[name].shape)
        delta[name], new_m[name], new_v[name] = (t.reshape(w[name].shape) for t in (d2, m2, v2))
    shp = {nme: w[nme].shape for nme in _SMALL}
    gp = _pack_small(g_small, _SMALL)
    dp, mp, vp = _adamw(_pack_small(w, _SMALL), gp, _pack_small(m, _SMALL), _pack_small(v, _SMALL), "adamw_small")
    for dst, src in ((grads, gp), (delta, dp), (new_m, mp), (new_v, vp)):
        dst.update(_unpack_small(src, shp, _SMALL))

    return (loss, grad_x[None], *[grads[nme] for nme in _WEIGHT_ORDER], *[delta[nme] for nme in _WEIGHT_ORDER],
            *[new_m[nme] for nme in _WEIGHT_ORDER], *[new_v[nme] for nme in _WEIGHT_ORDER])
```

```python
import functools
import math

import jax
import jax.numpy as jnp
from jax import lax
from jax.experimental import pallas as pl
from jax.experimental.pallas import tpu as pltpu

F32 = jnp.float32
BF16 = jnp.bfloat16
MESH = pl.DeviceIdType.MESH

D_MODEL = 1024
SSD_HEADS = 16
HEAD_DIM = 64
D_STATE = 128
CHUNK = 128
D_CONV = 5
D_XBC = D_MODEL + 2 * D_STATE
D_XD = 1408
N_POOL = 4
POOL_DIM = 256
POOL_WINDOWS = (2, 4, 8, 16)
GRID_W = 64
D_FF = 2816
D_IN_PROJ = 3360
LN_EPS = 1e-5
ALPHA = 2.0 ** 0.25
POOL_TB = 512
MM_ROWS = 1024

ADAM_LR = 0.001
ADAM_B1 = 0.9
ADAM_B2 = 0.999
ADAM_EPS = 1e-08
ADAM_WD = 0.01
ADAM_STEP = 10

VMEM_LIMIT = 56 * 1024 * 1024


def _cp(sem=None):
    return pltpu.CompilerParams(dimension_semantics=sem, vmem_limit_bytes=VMEM_LIMIT)


def _sigmoid(x):
    return 1.0 / (1.0 + jnp.exp(-x))


def _silu(x):
    return x * _sigmoid(x)


def _dsilu(x):
    s = _sigmoid(x)
    return s * (1.0 + x * (1.0 - s))


def _softplus(x):
    t = jnp.exp(-jnp.abs(x))
    u = 1.0 + t
    log1p = jnp.where(u == 1.0, t, jnp.log(u) * t / (u - 1.0 + (u == 1.0)))
    return jnp.maximum(x, 0.0) + log1p


def _split(x, n):
    parts, r = [], x
    for _ in range(n):
        p = r.astype(BF16)
        parts.append(p)
        r = r - p.astype(F32)
    return parts


def _dot(a, b):
    return jnp.dot(a, b, preferred_element_type=F32)


def _dot_nt(a, b):
    return lax.dot_general(a, b, (((1,), (1,)), ((), ())), preferred_element_type=F32)


def _dot_tn(a, b):
    return lax.dot_general(a, b, (((0,), (0,)), ((), ())), preferred_element_type=F32)


def _dot_sel_l(sel_bf, x, n=3):
    out = None
    for p in _split(x, n):
        t = _dot(sel_bf, p)
        out = t if out is None else out + t
    return out


def _dot_sel_r(x, sel_bf, n=3):
    out = None
    for p in _split(x, n):
        t = _dot(p, sel_bf)
        out = t if out is None else out + t
    return out


def _row_block(n, cap=256, mult=8):
    best = None
    for t in range(mult, min(n, cap) + 1, mult):
        if n % t == 0:
            best = t
    return best if best is not None else n


def _vec(v):
    return v.reshape(1, -1).astype(F32)


def _mods_fwd(c8, wada_bf, b_ada):
    d = c8.shape[1]
    n = wada_bf.shape[1]

    def body(c_ref, w_ref, b_ref, o_ref):
        s = _silu(c_ref[...]).astype(BF16)
        o_ref[...] = _dot(s, w_ref[...]) + b_ref[...]

    return pl.pallas_call(
        body, name="mods_fwd", grid=(n // d,),
        in_specs=[pl.BlockSpec((8, d), lambda j: (0, 0)),
                  pl.BlockSpec((d, d), lambda j: (0, j)),
                  pl.BlockSpec((1, d), lambda j: (0, j))],
        out_specs=pl.BlockSpec((8, d), lambda j: (0, j)),
        out_shape=jax.ShapeDtypeStruct((8, n), F32),
        compiler_params=_cp(("arbitrary",)),
    )(c8, wada_bf, b_ada)


def _mods_bwd_w(ct8, dm8):
    d = ct8.shape[0]
    n = dm8.shape[1]
    tn = 512

    def body(ct_ref, dm_ref, dw_ref, db_ref):
        s = _silu(ct_ref[...])
        dm = dm_ref[...]
        dw_ref[...] = s[:, 0:1] * dm[0:1, :] + s[:, 1:2] * dm[1:2, :]
        db_ref[...] = jnp.broadcast_to(dm[0:1, :] + dm[1:2, :], (8, tn))

    return pl.pallas_call(
        body, name="mods_bwd_w", grid=(n // tn,),
        in_specs=[pl.BlockSpec((d, 8), lambda j: (0, 0)),
                  pl.BlockSpec((8, tn), lambda j: (0, j))],
        out_specs=[pl.BlockSpec((d, tn), lambda j: (0, j)),
                   pl.BlockSpec((8, tn), lambda j: (0, j))],
        out_shape=[jax.ShapeDtypeStruct((d, n), F32), jax.ShapeDtypeStruct((8, n), F32)],
        compiler_params=_cp(("arbitrary",)),
    )(ct8, dm8)


def _mods_bwd_c(dm8, wada_bf, c8):
    d = c8.shape[1]
    n = dm8.shape[1]
    nk = n // d

    def body(dm_ref, w_ref, c_ref, o_ref):
        k = pl.program_id(0)

        @pl.when(k == 0)
        def _():
            o_ref[...] = jnp.zeros_like(o_ref)

        o_ref[...] += _dot_nt(dm_ref[...].astype(BF16), w_ref[...])

        @pl.when(k == nk - 1)
        def _():
            o_ref[...] = o_ref[...] * _dsilu(c_ref[...])

    return pl.pallas_call(
        body, name="mods_bwd_c", grid=(nk,),
        in_specs=[pl.BlockSpec((8, d), lambda k: (0, k)),
                  pl.BlockSpec((d, d), lambda k: (0, k)),
                  pl.BlockSpec((8, d), lambda k: (0, 0))],
        out_specs=pl.BlockSpec((8, d), lambda k: (0, 0)),
        out_shape=jax.ShapeDtypeStruct((8, d), F32),
        compiler_params=_cp(("arbitrary",)),
    )(dm8, wada_bf, c8)


def _ln_stats(x):
    mu = jnp.mean(x, axis=-1, keepdims=True)
    xc = x - mu
    var = jnp.mean(xc * xc, axis=-1, keepdims=True)
    rstd = lax.rsqrt(var + LN_EPS)
    return xc * rstd, rstd


def _ln_bwd(dxhat, xhat, rstd):
    m1 = jnp.mean(dxhat, axis=-1, keepdims=True)
    m2 = jnp.mean(dxhat * xhat, axis=-1, keepdims=True)
    return rstd * (dxhat - m1 - xhat * m2)


def _row_spec(tb, d):
    return pl.BlockSpec((tb, d), lambda i: (i, 0))


def _par_spec(d):
    return pl.BlockSpec((1, d), lambda i: (0, 0))


def _acc_spec(d):
    return pl.BlockSpec((8, d), lambda i: (0, 0))


def _ln_mod(x, g, b, sh, sc, tb, name):
    n, d = x.shape

    def body(x_ref, g_ref, b_ref, sh_ref, sc_ref, x0_ref, h_ref):
        xhat, _ = _ln_stats(x_ref[...])
        x0 = xhat * g_ref[...] + b_ref[...]
        x0_ref[...] = x0
        h_ref[...] = (x0 * (1.0 + sc_ref[...]) + sh_ref[...]).astype(BF16)

    return pl.pallas_call(
        body, name=name, grid=(n // tb,),
        in_specs=[_row_spec(tb, d)] + [_par_spec(d)] * 4,
        out_specs=[_row_spec(tb, d), _row_spec(tb, d)],
        out_shape=[jax.ShapeDtypeStruct((n, d), F32), jax.ShapeDtypeStruct((n, d), BF16)],
        compiler_params=_cp(("parallel",)),
    )(x, g, b, sh, sc)


def _res_ln(xres, mix, gate, g, b, sh, sc, tb):
    n, d = xres.shape

    def body(xr_ref, mix_ref, gate_ref, g_ref, b_ref, sh_ref, sc_ref, x1_ref, h_ref):
        r = ALPHA * xr_ref[...] + gate_ref[...] * mix_ref[...]
        xhat, _ = _ln_stats(r)
        x1 = xhat * g_ref[...] + b_ref[...]
        x1_ref[...] = x1
        h_ref[...] = (x1 * (1.0 + sc_ref[...]) + sh_ref[...]).astype(BF16)

    return pl.pallas_call(
        body, name="res_ln1", grid=(n // tb,),
        in_specs=[_row_spec(tb, d)] * 2 + [_par_spec(d)] * 5,
        out_specs=[_row_spec(tb, d), _row_spec(tb, d)],
        out_shape=[jax.ShapeDtypeStruct((n, d), F32), jax.ShapeDtypeStruct((n, d), BF16)],
        compiler_params=_cp(("parallel",)),
    )(xres, mix, gate, g, b, sh, sc)


def _final_ln_loss(x1, ffn, gate, g, b, target, tb):
    n, d = x1.shape

    def body(x1_ref, ffn_ref, gate_ref, g_ref, b_ref, t_ref, dffn_ref, dr_ref, acc_ref):
        i = pl.program_id(0)

        @pl.when(i == 0)
        def _():
            acc_ref[...] = jnp.zeros_like(acc_ref)

        ffn = ffn_ref[...]
        r = ALPHA * x1_ref[...] + gate_ref[...] * ffn
        xhat, rstd = _ln_stats(r)
        err = xhat * g_ref[...] + b_ref[...] - t_ref[...]
        dx2 = err * (1.0 / d)
        dr = _ln_bwd(dx2 * g_ref[...], xhat, rstd)
        dr_ref[...] = dr
        dffn_ref[...] = (gate_ref[...] * dr).astype(BF16)
        acc_ref[0:1, :] += jnp.sum(dx2 * xhat, axis=0, keepdims=True)
        acc_ref[1:2, :] += jnp.sum(dx2, axis=0, keepdims=True)
        acc_ref[2:3, :] += jnp.sum(dr * ffn, axis=0, keepdims=True)
        acc_ref[3:4, :] += jnp.sum(err * err, axis=0, keepdims=True)

    return pl.pallas_call(
        body, name="final_ln_loss", grid=(n // tb,),
        in_specs=[_row_spec(tb, d)] * 2 + [_par_spec(d)] * 3 + [_row_spec(tb, d)],
        out_specs=[_row_spec(tb, d), _row_spec(tb, d), _acc_spec(d)],
        out_shape=[jax.ShapeDtypeStruct((n, d), BF16), jax.ShapeDtypeStruct((n, d), F32),
                   jax.ShapeDtypeStruct((8, d), F32)],
        compiler_params=_cp(("arbitrary",)),
    )(x1, ffn, gate, g, b, target)


def _bwd_ln1(dr2, dh2, x1, x0, mix, gate, sc2, g, tb):
    n, d = x1.shape

    def body(dr2_ref, dh2_ref, x1_ref, x0_ref, mix_ref, gate_ref, sc_ref, g_ref, dmix_ref, dr1_ref, acc_ref):
        i = pl.program_id(0)

        @pl.when(i == 0)
        def _():
            acc_ref[...] = jnp.zeros_like(acc_ref)

        dh2 = dh2_ref[...]
        mix = mix_ref[...]
        dx1 = ALPHA * dr2_ref[...] + dh2 * (1.0 + sc_ref[...])
        r = ALPHA * x0_ref[...] + gate_ref[...] * mix
        xhat, rstd = _ln_stats(r)
        dr1 = _ln_bwd(dx1 * g_ref[...], xhat, rstd)
        dr1_ref[...] = dr1
        dmix_ref[...] = (gate_ref[...] * dr1).astype(BF16)
        acc_ref[0:1, :] += jnp.sum(dh2 * x1_ref[...], axis=0, keepdims=True)
        acc_ref[1:2, :] += jnp.sum(dh2, axis=0, keepdims=True)
        acc_ref[2:3, :] += jnp.sum(dx1 * xhat, axis=0, keepdims=True)
        acc_ref[3:4, :] += jnp.sum(dx1, axis=0, keepdims=True)
        acc_ref[4:5, :] += jnp.sum(dr1 * mix, axis=0, keepdims=True)

    return pl.pallas_call(
        body, name="bwd_ln1", grid=(n // tb,),
        in_specs=[_row_spec(tb, d)] * 5 + [_par_spec(d)] * 3,
        out_specs=[_row_spec(tb, d), _row_spec(tb, d), _acc_spec(d)],
        out_shape=[jax.ShapeDtypeStruct((n, d), BF16), jax.ShapeDtypeStruct((n, d), F32),
                   jax.ShapeDtypeStruct((8, d), F32)],
        compiler_params=_cp(("arbitrary",)),
    )(dr2, dh2, x1, x0, mix, gate, sc2, g)


def _bwd_ln0(dres, dh, x, g, b, sc, tb, name):
    n, d = x.shape
    has_res = dres is not None

    def body(*refs):
        if has_res:
            dres_ref, dh_ref, x_ref, g_ref, b_ref, sc_ref, dx_ref, acc_ref = refs
        else:
            dh_ref, x_ref, g_ref, b_ref, sc_ref, dx_ref, acc_ref = refs
        i = pl.program_id(0)

        @pl.when(i == 0)
        def _():
            acc_ref[...] = jnp.zeros_like(acc_ref)

        dh = dh_ref[...]
        xhat, rstd = _ln_stats(x_ref[...])
        x0 = xhat * g_ref[...] + b_ref[...]
        dx0 = dh * (1.0 + sc_ref[...])
        if has_res:
            dx0 = dx0 + ALPHA * dres_ref[...]
        dx_ref[...] = _ln_bwd(dx0 * g_ref[...], xhat, rstd)
        acc_ref[0:1, :] += jnp.sum(dh * x0, axis=0, keepdims=True)
        acc_ref[1:2, :] += jnp.sum(dh, axis=0, keepdims=True)
        acc_ref[2:3, :] += jnp.sum(dx0 * xhat, axis=0, keepdims=True)
        acc_ref[3:4, :] += jnp.sum(dx0, axis=0, keepdims=True)

    ins = ([dres] if has_res else []) + [dh, x, g, b, sc]
    return pl.pallas_call(
        body, name=name, grid=(n // tb,),
        in_specs=[_row_spec(tb, d)] * (3 if has_res else 2) + [_par_spec(d)] * 3,
        out_specs=[_row_spec(tb, d), _acc_spec(d)],
        out_shape=[jax.ShapeDtypeStruct((n, d), F32), jax.ShapeDtypeStruct((8, d), F32)],
        compiler_params=_cp(("arbitrary",)),
    )(*ins)


def _matmul_nn(pairs, out_dtype, tm, tn, name):
    m = pairs[0][0].shape[0]
    n = pairs[0][1].shape[1]
    tm = min(tm, m)
    tn = min(tn, n)
    npair = len(pairs)

    def body(*refs):
        o_ref = refs[-1]
        acc = None
        for p in range(npair):
            t = _dot(refs[2 * p][...].astype(BF16), refs[2 * p + 1][...])
            acc = t if acc is None else acc + t
        o_ref[...] = acc.astype(out_dtype)

    in_specs, args = [], []
    for a, b in pairs:
        k = a.shape[1]
        in_specs += [pl.BlockSpec((tm, k), lambda i, j: (i, 0)), pl.BlockSpec((k, tn), lambda i, j: (0, j))]
        args += [a, b]
    return pl.pallas_call(
        body, name=name, grid=(m // tm, n // tn),
        in_specs=in_specs,
        out_specs=pl.BlockSpec((tm, tn), lambda i, j: (i, j)),
        out_shape=jax.ShapeDtypeStruct((m, n), out_dtype),
        compiler_params=_cp(("parallel", "arbitrary")),
    )(*args)


def _matmul_tn(a, g, tm, tn, name):
    m, k = a.shape
    n = g.shape[1]
    tm = min(tm, m)
    tn = min(tn, n)

    def body(a_ref, g_ref, o_ref):
        i = pl.program_id(1)

        @pl.when(i == 0)
        def _():
            o_ref[...] = jnp.zeros_like(o_ref)

        o_ref[...] += _dot_tn(a_ref[...].astype(BF16), g_ref[...].astype(BF16))

    return pl.pallas_call(
        body, name=name, grid=(n // tn, m // tm),
        in_specs=[pl.BlockSpec((tm, k), lambda j, i: (i, 0)), pl.BlockSpec((tm, tn), lambda j, i: (i, j))],
        out_specs=pl.BlockSpec((k, tn), lambda j, i: (0, j)),
        out_shape=jax.ShapeDtypeStruct((k, n), F32),
        compiler_params=_cp(("parallel", "arbitrary")),
    )(a, g)


def _swiglu_fwd(h, wg, wu, tm, tn):
    m, k = h.shape
    n = wg.shape[1]
    tm = min(tm, m)

    def body(h_ref, wg_ref, wu_ref, gate_ref, up_ref, hmid_ref):
        hv = h_ref[...]
        gate = _dot(hv, wg_ref[...])
        up = _dot(hv, wu_ref[...])
        gate_ref[...] = gate.astype(BF16)
        up_ref[...] = up.astype(BF16)
        hmid_ref[...] = (_silu(gate) * up).astype(BF16)

    blk = pl.BlockSpec((tm, tn), lambda i, j: (i, j))
    wspec = pl.BlockSpec((k, tn), lambda i, j: (0, j))
    return pl.pallas_call(
        body, name="swiglu_fwd", grid=(m // tm, n // tn),
        in_specs=[pl.BlockSpec((tm, k), lambda i, j: (i, 0)), wspec, wspec],
        out_specs=[blk, blk, blk],
        out_shape=[jax.ShapeDtypeStruct((m, n), BF16), jax.ShapeDtypeStruct((m, n), BF16),
                   jax.ShapeDtypeStruct((m, n), BF16)],
        compiler_params=_cp(("parallel", "arbitrary")),
    )(h, wg, wu)


def _swiglu_bwd(dffn, wdt, gate, up, tm, tn):
    m, k = dffn.shape
    n = wdt.shape[1]
    tm = min(tm, m)

    def body(d_ref, w_ref, gate_ref, up_ref, dg_ref, du_ref):
        dh = _dot(d_ref[...], w_ref[...])
        gate = gate_ref[...].astype(F32)
        dg_ref[...] = (dh * up_ref[...].astype(F32) * _dsilu(gate)).astype(BF16)
        du_ref[...] = (dh * _silu(gate)).astype(BF16)

    blk = pl.BlockSpec((tm, tn), lambda i, j: (i, j))
    return pl.pallas_call(
        body, name="swiglu_bwd", grid=(m // tm, n // tn),
        in_specs=[pl.BlockSpec((tm, k), lambda i, j: (i, 0)), pl.BlockSpec((k, tn), lambda i, j: (0, j)), blk, blk],
        out_specs=[blk, blk],
        out_shape=[jax.ShapeDtypeStruct((m, n), BF16), jax.ShapeDtypeStruct((m, n), BF16)],
        compiler_params=_cp(("parallel", "arbitrary")),
    )(dffn, wdt, gate, up)


def _halo_specs(tb, width, nrows):
    r8 = tb // 8
    last = nrows // 8 - 1
    prev = pl.BlockSpec((8, width), lambda i: (jnp.maximum(i * r8 - 1, 0), 0))
    nxt = pl.BlockSpec((8, width), lambda i: (jnp.minimum((i + 1) * r8, last), 0))
    return prev, nxt


def _fill_halo(buf, prev_ref, cur_ref, next_ref, tb, i, nb):
    buf[0:8, :] = prev_ref[...] * (i > 0).astype(F32)
    buf[8:8 + tb, :] = cur_ref[...]
    buf[8 + tb:16 + tb, :] = next_ref[...] * (i < nb - 1).astype(F32)


def _conv_fwd(xd, w8, b, tb, name):
    n = xd.shape[0]
    tb = min(tb, n)
    nb = n // tb
    prev, nxt = _halo_specs(tb, D_XBC, n)

    def body(p_ref, c_ref, n_ref, w_ref, b_ref, u_ref, buf):
        i = pl.program_id(0)
        _fill_halo(buf, p_ref, c_ref, n_ref, tb, i, nb)
        acc = jnp.broadcast_to(b_ref[...], (tb, D_XBC))
        for k in range(D_CONV):
            acc = acc + w_ref[k:k + 1, :] * buf[pl.ds(6 + k, tb), :]
        u_ref[...] = acc

    return pl.pallas_call(
        body, name=name, grid=(nb,),
        in_specs=[prev, pl.BlockSpec((tb, D_XBC), lambda i: (i, 0)), nxt,
                  pl.BlockSpec((8, D_XBC), lambda i: (0, 0)), _par_spec(D_XBC)],
        out_specs=_row_spec(tb, D_XBC),
        out_shape=jax.ShapeDtypeStruct((n, D_XBC), F32),
        scratch_shapes=[pltpu.VMEM((tb + 16, D_XBC), F32)],
        compiler_params=_cp(("parallel",)),
    )(xd, xd, xd, w8, b)


def _conv_bwd_a(dxs, dy, dskip_e, dbc, u, tb, name):
    n = u.shape[0]
    tb = min(tb, n)

    def body(dxs_ref, dy_ref, sk_ref, dbc_ref, u_ref, du_ref, acc_ref):
        i = pl.program_id(0)

        @pl.when(i == 0)
        def _():
            acc_ref[...] = jnp.zeros_like(acc_ref)

        uv = u_ref[...]
        ds = _dsilu(uv)
        gx = dxs_ref[0] + dxs_ref[1] + dy_ref[...] * sk_ref[...]
        gbc = dbc_ref[0] + dbc_ref[1]
        du = jnp.concatenate([gx, gbc], axis=1) * ds
        du_ref[...] = du
        acc_ref[0:1, :] += jnp.sum(du, axis=0, keepdims=True)

    return pl.pallas_call(
        body, name=name, grid=(n // tb,),
        in_specs=[pl.BlockSpec((2, tb, D_MODEL), lambda i: (0, i, 0)), _row_spec(tb, D_MODEL), _par_spec(D_MODEL),
                  pl.BlockSpec((2, tb, 2 * D_STATE), lambda i: (0, i, 0)), _row_spec(tb, D_XBC)],
        out_specs=[_row_spec(tb, D_XBC), _acc_spec(D_XBC)],
        out_shape=[jax.ShapeDtypeStruct((n, D_XBC), F32), jax.ShapeDtypeStruct((8, D_XBC), F32)],
        compiler_params=_cp(("arbitrary",)),
    )(dxs, dy, dskip_e, dbc, u)


def _conv_bwd_b(du, xd, ddt, w8, tb, name):
    n = du.shape[0]
    tb = min(tb, n)
    nb = n // tb
    prev, nxt = _halo_specs(tb, D_XBC, n)

    def body(dp_ref, dc_ref, dn_ref, xp_ref, xc_ref, xn_ref, ddt_ref, w_ref, dxd_ref, acc_ref, dbuf, xbuf):
        i = pl.program_id(0)

        @pl.when(i == 0)
        def _():
            acc_ref[...] = jnp.zeros_like(acc_ref)

        _fill_halo(dbuf, dp_ref, dc_ref, dn_ref, tb, i, nb)
        _fill_halo(xbuf, xp_ref, xc_ref, xn_ref, tb, i, nb)
        duc = dc_ref[...]
        acc = jnp.zeros((tb, D_XBC), F32)
        for k in range(D_CONV):
            acc = acc + w_ref[k:k + 1, :] * dbuf[pl.ds(10 - k, tb), :]
            acc_ref[k:k + 1, :] += jnp.sum(duc * xbuf[pl.ds(6 + k, tb), :], axis=0, keepdims=True)
        dxd_ref[:, 0:D_XBC] = acc.astype(BF16)
        ddt = ddt_ref[0] + pltpu.roll(ddt_ref[1], SSD_HEADS, 1)
        dxd_ref[:, D_XBC:D_XD] = ddt.astype(BF16)

    cur = pl.BlockSpec((tb, D_XBC), lambda i: (i, 0))
    return pl.pallas_call(
        body, name=name, grid=(nb,),
        in_specs=[prev, cur, nxt, prev, cur, nxt,
                  pl.BlockSpec((2, tb, 128), lambda i: (0, i, 0)), pl.BlockSpec((8, D_XBC), lambda i: (0, 0))],
        out_specs=[_row_spec(tb, D_XD), _acc_spec(D_XBC)],
        out_shape=[jax.ShapeDtypeStruct((n, D_XD), BF16), jax.ShapeDtypeStruct((8, D_XBC), F32)],
        scratch_shapes=[pltpu.VMEM((tb + 16, D_XBC), F32), pltpu.VMEM((tb + 16, D_XBC), F32)],
        compiler_params=_cp(("arbitrary",)),
    )(du, du, du, xd, xd, xd, ddt, w8)


def _ssd_chunk_index(nc, reverse):
    def idx(d, k):
        kk = (nc - 1 - k) if reverse else k
        return kk + d * (nc - 1 - 2 * kk)
    return idx


def _ssd_prologue(d, u_ref, xd_ref, bias_ref, a_ref, r_ref):
    q = CHUNK
    xbc = _silu(u_ref[...])
    xs = xbc[:, 0:D_MODEL]
    bm = xbc[:, D_MODEL:D_MODEL + D_STATE]
    cm = xbc[:, D_MODEL + D_STATE:D_XBC]
    row = lax.broadcasted_iota(jnp.int32, (q, q), 0)
    col = lax.broadcasted_iota(jnp.int32, (q, q), 1)
    sgn = 1 - 2 * d
    mask = ((row - col) * sgn) >= 0
    mask_t = ((row - col) * sgn) <= 0
    xdv = xd_ref[...]
    dtraw = jnp.where(d == 0, xdv, pltpu.roll(xdv, 128 - SSD_HEADS, 1)) + bias_ref[...]
    head_lane = col < SSD_HEADS
    dt = jnp.where(head_lane, _softplus(dtraw), 0.0)
    a = a_ref[...]
    tri = jnp.where(mask, 1.0, 0.0).astype(BF16)
    acum = _dot_sel_l(tri, dt * a)
    rexp = r_ref[...]
    alast = jnp.where(d == 0, acum[q - 1:q, :], acum[0:1, :])
    e16 = jnp.exp(acum)
    dend16 = jnp.exp(alast - acum)
    wend16 = dend16 * dt
    e = _dot_sel_r(e16, rexp, n=2)
    wend_e = _dot_sel_r(wend16, rexp, n=2)
    elast_e = jnp.where(d == 0, e[q - 1:q, :], e[0:1, :])
    g = _dot_nt(cm.astype(BF16), bm.astype(BF16))
    return dict(xs=xs, bm=bm, cm=cm, mask=mask, mask_t=mask_t, dtraw=dtraw, head_lane=head_lane, dt=dt, a=a,
                acum=acum, acum_t=acum.T, dt_t=dt.T, e16=e16, dend16=dend16, wend16=wend16, e=e, wend_e=wend_e,
                elast_e=elast_e, g=g, col=col, row=row)


def _ssd_head_mats(p, h):
    seg = p["acum"][:, h:h + 1] - p["acum_t"][h:h + 1, :]
    lm = jnp.exp(jnp.where(p["mask"], seg, -jnp.inf))
    gl = p["g"] * lm
    s = gl * p["dt_t"][h:h + 1, :]
    return lm, gl, s


def _ssd_fwd(u, xd, bias2, a2, rexp, h0, name):
    n = u.shape[0]
    nc = n // CHUNK
    q = CHUNK
    cidx = _ssd_chunk_index(nc, reverse=False)

    def body(u_ref, xd_ref, bias_ref, a_ref, r_ref, h0_ref, y_ref, hp_ref, hf_ref, st):
        d = pl.program_id(0)
        k = pl.program_id(1)

        @pl.when(k == 0)
        def _():
            st[...] = h0_ref[...]

        p = _ssd_prologue(d, u_ref, xd_ref, bias_ref, a_ref, r_ref)
        stv = st[...]
        st_bf = stv.astype(BF16)
        hp_ref[...] = st_bf
        xs = p["xs"]
        lane128 = p["col"]
        y_off = _dot(p["cm"].astype(BF16), st_bf) * p["e"]
        for pb in range(SSD_HEADS // 2):
            _, _, s0 = _ssd_head_mats(p, 2 * pb)
            _, _, s1 = _ssd_head_mats(p, 2 * pb + 1)
            xp = xs[:, pb * 128:(pb + 1) * 128]
            rhs = jnp.concatenate([jnp.where(lane128 < HEAD_DIM, xp, 0.0), jnp.where(lane128 >= HEAD_DIM, xp, 0.0)],
                                  axis=0).astype(BF16)
            lhs = jnp.concatenate([s0, s1], axis=1).astype(BF16)
            y_ref[:, pb * 128:(pb + 1) * 128] = _dot(lhs, rhs) + y_off[:, pb * 128:(pb + 1) * 128]
        xw = (xs * p["wend_e"]).astype(BF16)
        new = stv * p["elast_e"] + _dot(p["bm"].T.astype(BF16), xw)
        st[...] = new
        hf_ref[...] = new

    return pl.pallas_call(
        body, name=name, grid=(2, nc),
        in_specs=[pl.BlockSpec((q, D_XBC), lambda d, k: (cidx(d, k), 0)),
                  pl.BlockSpec((q, 128), lambda d, k: (cidx(d, k), D_XBC // 128)),
                  pl.BlockSpec((None, 1, 128), lambda d, k: (d, 0, 0)),
                  pl.BlockSpec((None, 1, 128), lambda d, k: (d, 0, 0)),
                  pl.BlockSpec((128, D_MODEL), lambda d, k: (0, 0)),
                  pl.BlockSpec((None, D_STATE, D_MODEL), lambda d, k: (d, 0, 0))],
        out_specs=[pl.BlockSpec((None, q, D_MODEL), lambda d, k: (d, cidx(d, k), 0)),
                   pl.BlockSpec((None, None, D_STATE, D_MODEL), lambda d, k: (d, cidx(d, k), 0, 0)),
                   pl.BlockSpec((None, D_STATE, D_MODEL), lambda d, k: (d, 0, 0))],
        out_shape=[jax.ShapeDtypeStruct((2, n, D_MODEL), F32),
                   jax.ShapeDtypeStruct((2, nc, D_STATE, D_MODEL), BF16),
                   jax.ShapeDtypeStruct((2, D_STATE, D_MODEL), F32)],
        scratch_shapes=[pltpu.VMEM((D_STATE, D_MODEL), F32)],
        compiler_params=_cp(("arbitrary", "arbitrary")),
    )(u, xd, bias2, a2, rexp, h0)


def _ssd_bwd(u, xd, bias2, a2, rexp, rexp_t, dy, hprev, lam0, name):
    n = u.shape[0]
    nc = n // CHUNK
    q = CHUNK
    cidx = _ssd_chunk_index(nc, reverse=True)

    def body(u_ref, xd_ref, bias_ref, a_ref, r_ref, rt_ref, dy_ref, hp_ref, lam0_ref,
             dxs_ref, dbc_ref, ddt_ref, acc_ref, lamo_ref, lam):
        d = pl.program_id(0)
        k = pl.program_id(1)

        @pl.when(k == 0)
        def _():
            lam[...] = lam0_ref[...]
            acc_ref[...] = jnp.zeros_like(acc_ref)

        rexp_t = rt_ref[...]

        def hsum(t):
            return _dot_sel_r(t, rexp_t, n=2)

        p = _ssd_prologue(d, u_ref, xd_ref, bias_ref, a_ref, r_ref)
        xs, bm, cm = p["xs"], p["bm"], p["cm"]
        bm_bf, cm_bf = bm.astype(BF16), cm.astype(BF16)
        lamn = lam[...]
        lamn_bf = lamn.astype(BF16)
        stp = hp_ref[...]
        dyv = dy_ref[...]
        lane128 = p["col"]

        wend_e = p["wend_e"]
        cs = _dot(cm_bf, stp)
        dye_bf = (dyv * p["e"]).astype(BF16)
        dc_off = _dot_nt(dye_bf, stp)
        v = _dot(bm_bf, lamn_bf)
        xw_bf = (xs * wend_e).astype(BF16)
        db_off = _dot_nt(xw_bf, lamn_bf)
        elast_e = p["elast_e"]
        dlast_e = jnp.sum(stp.astype(F32) * lamn, axis=0, keepdims=True) * elast_e
        lam_new = lamn * elast_e + _dot(cm.T.astype(BF16), dye_bf)
        lam[...] = lam_new
        lamo_ref[...] = lam_new

        hs_vx = hsum(v * xs)
        om = p["wend16"] * hs_vx
        x1 = p["e16"] * hsum(dyv * cs) - om
        x2 = p["dend16"] * hs_vx
        x3 = jnp.sum(om, axis=0, keepdims=True) + hsum(jnp.broadcast_to(dlast_e, (8, D_MODEL)))[0:1, :]

        sub16 = lax.broadcasted_iota(jnp.int32, (SSD_HEADS, q), 0)
        rs = jnp.zeros((q, 128), F32)
        cs_m = jnp.zeros((SSD_HEADS, q), F32)
        dt_m = jnp.zeros((SSD_HEADS, q), F32)
        dg = jnp.zeros((q, q), F32)
        for pb in range(SSD_HEADS // 2):
            xp_bf = xs[:, pb * 128:(pb + 1) * 128].astype(BF16)
            dyp = dyv[:, pb * 128:(pb + 1) * 128]
            dxs_pair = None
            for half in range(2):
                h = 2 * pb + half
                sel = (lane128 < HEAD_DIM) if half == 0 else (lane128 >= HEAD_DIM)
                dyh_bf = jnp.where(sel, dyp, 0.0).astype(BF16)
                lm, gl, s = _ssd_head_mats(p, h)
                ds = _dot_nt(dyh_bf, xp_bf)
                t = _dot_tn(s.astype(BF16), dyh_bf)
                dxs_pair = t if dxs_pair is None else dxs_pair + t
                w = ds * s
                rs = rs + jnp.sum(w, axis=1, keepdims=True) * (lane128 == h).astype(F32)
                cs_m = jnp.where(sub16 == h, jnp.sum(w, axis=0, keepdims=True), cs_m)
                dt_m = jnp.where(sub16 == h, jnp.sum(ds * gl, axis=0, keepdims=True), dt_m)
                dg = dg + ds * lm * p["dt_t"][h:h + 1, :]
            sl = slice(pb * 128, (pb + 1) * 128)
            dxs_ref[:, sl] = dxs_pair + v[:, sl] * wend_e[:, sl]

        def to_lanes(m16):
            return jnp.concatenate([m16, jnp.zeros((128 - SSD_HEADS, q), F32)], axis=0).T

        last = jnp.where(d == 0, q - 1, 0)
        dacum = rs - to_lanes(cs_m) + x1 + jnp.where(p["row"] == last, x3[0:1, :], 0.0)
        tri_t = jnp.where(p["mask_t"], 1.0, 0.0).astype(BF16)
        ddta = _dot_sel_l(tri_t, dacum)
        dt = p["dt"]
        a = p["a"]
        ddt = to_lanes(dt_m) + x2 + a * ddta
        ddtraw = jnp.where(p["head_lane"], ddt * _sigmoid(p["dtraw"]), 0.0)
        ddt_ref[...] = ddtraw
        acc_ref[0:1, :] += jnp.sum(ddtraw, axis=0, keepdims=True)
        acc_ref[1:2, :] += jnp.sum(dt * ddta, axis=0, keepdims=True) * a

        dg_bf = dg.astype(BF16)
        dbc_ref[:, 0:D_STATE] = _dot_tn(dg_bf, cm_bf) + db_off
        dbc_ref[:, D_STATE:2 * D_STATE] = _dot(dg_bf, bm_bf) + dc_off

    cblk = lambda d, k: (cidx(d, k), 0)
    return pl.pallas_call(
        body, name=name, grid=(2, nc),
        in_specs=[pl.BlockSpec((q, D_XBC), cblk),
                  pl.BlockSpec((q, 128), lambda d, k: (cidx(d, k), D_XBC // 128)),
                  pl.BlockSpec((None, 1, 128), lambda d, k: (d, 0, 0)),
                  pl.BlockSpec((None, 1, 128), lambda d, k: (d, 0, 0)),
                  pl.BlockSpec((128, D_MODEL), lambda d, k: (0, 0)),
                  pl.BlockSpec((D_MODEL, 128), lambda d, k: (0, 0)),
                  pl.BlockSpec((q, D_MODEL), cblk),
                  pl.BlockSpec((None, None, D_STATE, D_MODEL), lambda d, k: (d, cidx(d, k), 0, 0)),
                  pl.BlockSpec((None, D_STATE, D_MODEL), lambda d, k: (d, 0, 0))],
        out_specs=[pl.BlockSpec((None, q, D_MODEL), lambda d, k: (d, cidx(d, k), 0)),
                   pl.BlockSpec((None, q, 2 * D_STATE), lambda d, k: (d, cidx(d, k), 0)),
                   pl.BlockSpec((None, q, 128), lambda d, k: (d, cidx(d, k), 0)),
                   pl.BlockSpec((None, 8, 128), lambda d, k: (d, 0, 0)),
                   pl.BlockSpec((None, D_STATE, D_MODEL), lambda d, k: (d, 0, 0))],
        out_shape=[jax.ShapeDtypeStruct((2, n, D_MODEL), F32),
                   jax.ShapeDtypeStruct((2, n, 2 * D_STATE), F32),
                   jax.ShapeDtypeStruct((2, n, 128), F32),
                   jax.ShapeDtypeStruct((2, 8, 128), F32),
                   jax.ShapeDtypeStruct((2, D_STATE, D_MODEL), F32)],
        scratch_shapes=[pltpu.VMEM((D_STATE, D_MODEL), F32)],
        compiler_params=_cp(("arbitrary", "arbitrary")),
    )(u, xd, bias2, a2, rexp, rexp_t, dy, hprev, lam0)


def _merge_fwd(y, u, z, dskip_e, gn, tb):
    n = z.shape[0]

    def body(y_ref, u_ref, z_ref, sk_ref, gn_ref, o_ref):
        xs = _silu(u_ref[...])
        ys = y_ref[0] + y_ref[1] + sk_ref[...] * xs
        gated = ys * _silu(z_ref[...])
        rstd = lax.rsqrt(jnp.mean(gated * gated, axis=-1, keepdims=True) + LN_EPS)
        o_ref[...] = (gated * rstd * gn_ref[...]).astype(BF16)

    return pl.pallas_call(
        body, name="merge_fwd", grid=(n // tb,),
        in_specs=[pl.BlockSpec((2, tb, D_MODEL), lambda i: (0, i, 0)), pl.BlockSpec((tb, D_MODEL), lambda i: (i, 0)),
                  _row_spec(tb, D_MODEL), _par_spec(D_MODEL), _par_spec(D_MODEL)],
        out_specs=_row_spec(tb, D_MODEL),
        out_shape=jax.ShapeDtypeStruct((n, D_MODEL), BF16),
        compiler_params=_cp(("parallel",)),
    )(y, u, z, dskip_e, gn)


def _merge_bwd(dyn, y, u, z, dskip_e, gn, tb):
    n = z.shape[0]

    def body(dyn_ref, y_ref, u_ref, z_ref, sk_ref, gn_ref, dy_ref, dz_ref, acc_ref):
        i = pl.program_id(0)

        @pl.when(i == 0)
        def _():
            acc_ref[...] = jnp.zeros_like(acc_ref)

        xs = _silu(u_ref[...])
        zv = z_ref[...]
        ys = y_ref[0] + y_ref[1] + sk_ref[...] * xs
        gated = ys * _silu(zv)
        rstd = lax.rsqrt(jnp.mean(gated * gated, axis=-1, keepdims=True) + LN_EPS)
        ghat = gated * rstd
        dyn_v = dyn_ref[...]
        t = dyn_v * gn_ref[...]
        dgated = rstd * (t - ghat * jnp.mean(t * ghat, axis=-1, keepdims=True))
        dys = dgated * _silu(zv)
        dy_ref[...] = dys
        dz_ref[...] = (dgated * ys * _dsilu(zv)).astype(BF16)
        acc_ref[0:1, :] += jnp.sum(dyn_v * ghat, axis=0, keepdims=True)
        acc_ref[1:2, :] += jnp.sum(dys * xs, axis=0, keepdims=True)

    return pl.pallas_call(
        body, name="merge_bwd", grid=(n // tb,),
        in_specs=[_row_spec(tb, D_MODEL), pl.BlockSpec((2, tb, D_MODEL), lambda i: (0, i, 0)),
                  pl.BlockSpec((tb, D_MODEL), lambda i: (i, 0)), _row_spec(tb, D_MODEL),
                  _par_spec(D_MODEL), _par_spec(D_MODEL)],
        out_specs=[_row_spec(tb, D_MODEL), _row_spec(tb, D_MODEL), _acc_spec(D_MODEL)],
        out_shape=[jax.ShapeDtypeStruct((n, D_MODEL), F32), jax.ShapeDtypeStruct((n, D_MODEL), BF16),
                   jax.ShapeDtypeStruct((8, D_MODEL), F32)],
        compiler_params=_cp(("arbitrary",)),
    )(dyn, y, u, z, dskip_e, gn)


def _pool_consts(transpose):
    tb = POOL_TB
    t = jnp.arange(tb)
    s = jnp.arange(3 * tb)
    rl, cl = t // GRID_W, t % GRID_W
    rs_, cs_ = s // GRID_W - tb // GRID_W, s % GRID_W
    s2 = jnp.arange(tb)
    rl2, cl2 = s2 // GRID_W, s2 % GRID_W
    brow, bcol = [], []
    for w in POOL_WINDOWS:
        lo, hi = -(w // 2), w - w // 2
        if transpose:
            lo, hi = -hi + 1, -lo + 1
        dr = rs_[None, :] - rl[:, None]
        brow.append(((cs_[None, :] == cl[:, None]) & (dr >= lo) & (dr < hi)).astype(BF16))
        dc = cl2[None, :] - cl[:, None]
        bcol.append(((rl2[None, :] == rl[:, None]) & (dc >= lo) & (dc < hi)).astype(BF16))
    return jnp.stack(brow), jnp.stack(bcol)


def _pool_inv(i, g, n):
    assert GRID_W == 64
    t = i * POOL_TB + lax.broadcasted_iota(jnp.int32, (POOL_TB, 1), 0)
    r = lax.shift_right_logical(t, 6)
    col = t & (GRID_W - 1)
    w = POOL_WINDOWS[g]
    lo, hi = -(w // 2), w - w // 2
    cnt_r = jnp.minimum(r + hi, n // GRID_W) - jnp.maximum(r + lo, 0)
    cnt_c = jnp.minimum(col + hi, GRID_W) - jnp.maximum(col + lo, 0)
    return 1.0 / (cnt_r * cnt_c).astype(F32)


def _pool_box(prev_ref, cur_ref, next_ref, brow_ref, bcol_ref, g, i, nb):
    sl = slice(g * POOL_DIM, (g + 1) * POOL_DIM)
    pv = prev_ref[:, sl] * (i > 0).astype(prev_ref.dtype)
    nx = next_ref[:, sl] * (i < nb - 1).astype(next_ref.dtype)
    stack = jnp.concatenate([pv.astype(BF16), cur_ref[:, sl].astype(BF16), nx.astype(BF16)], axis=0)
    r = _dot(brow_ref[g], stack)
    return _dot(bcol_ref[g], r.astype(BF16))


def _pool_halo_specs(n, d):
    tb = POOL_TB
    nb = n // tb
    prev = pl.BlockSpec((tb, d), lambda i: (jnp.maximum(i - 1, 0), 0))
    cur = pl.BlockSpec((tb, d), lambda i: (i, 0))
    nxt = pl.BlockSpec((tb, d), lambda i: (jnp.minimum(i + 1, nb - 1), 0))
    return prev, cur, nxt


def _pool_const_specs():
    tb = POOL_TB
    return [pl.BlockSpec((N_POOL, tb, 3 * tb), lambda i: (0, 0, 0)),
            pl.BlockSpec((N_POOL, tb, tb), lambda i: (0, 0, 0))]


def _pool_fwd(up, consts, pw_bf, pscale):
    n = up.shape[0]
    tb = POOL_TB
    nb = n // tb
    brow, bcol = consts
    prev, cur, nxt = _pool_halo_specs(n, D_MODEL)

    def body(p_ref, c_ref, n_ref, brow_ref, bcol_ref, pw_ref, sc_ref, o_ref, d_ref):
        i = pl.program_id(0)
        for g in range(N_POOL):
            sl = slice(g * POOL_DIM, (g + 1) * POOL_DIM)
            box = _pool_box(p_ref, c_ref, n_ref, brow_ref, bcol_ref, g, i, nb)
            dd = (box * _pool_inv(i, g, n) - c_ref[:, sl]).astype(BF16)
            d_ref[:, sl] = dd
            o_ref[:, sl] = (_dot(dd, pw_ref[g]) * sc_ref[:, sl]).astype(BF16)

    return pl.pallas_call(
        body, name="pool_fwd", grid=(nb,),
        in_specs=[prev, cur, nxt] + _pool_const_specs() +
                 [pl.BlockSpec((N_POOL, POOL_DIM, POOL_DIM), lambda i: (0, 0, 0)), _par_spec(D_MODEL)],
        out_specs=[_row_spec(tb, D_MODEL), _row_spec(tb, D_MODEL)],
        out_shape=[jax.ShapeDtypeStruct((n, D_MODEL), BF16), jax.ShapeDtypeStruct((n, D_MODEL), BF16)],
        compiler_params=_cp(("parallel",)),
    )(up, up, up, brow, bcol, pw_bf, pscale)


def _pool_bwd_a(dp, dsave, pw_bf, pwt_bf, pscale):
    n = dp.shape[0]
    tb = POOL_TB

    def body(dp_ref, d_ref, pw_ref, pwt_ref, sc_ref, dd_ref, dds_ref, gw_ref, gs_ref):
        i = pl.program_id(0)

        @pl.when(i == 0)
        def _():
            gw_ref[...] = jnp.zeros_like(gw_ref)
            gs_ref[...] = jnp.zeros_like(gs_ref)

        for g in range(N_POOL):
            sl = slice(g * POOL_DIM, (g + 1) * POOL_DIM)
            dpv = dp_ref[:, sl]
            dv = d_ref[:, sl]
            dpw_bf = (dpv * sc_ref[:, sl]).astype(BF16)
            dd = _dot(dpw_bf, pwt_ref[g])
            dd_ref[:, sl] = dd
            dds_ref[:, sl] = (dd * _pool_inv(i, g, n)).astype(BF16)
            gw_ref[g] += _dot_tn(dv, dpw_bf)
            gs_ref[0:1, sl] += jnp.sum(dpv * _dot(dv, pw_ref[g]), axis=0, keepdims=True)

    wspec = pl.BlockSpec((N_POOL, POOL_DIM, POOL_DIM), lambda i: (0, 0, 0))
    return pl.pallas_call(
        body, name="pool_bwd_a", grid=(n // tb,),
        in_specs=[_row_spec(tb, D_MODEL), _row_spec(tb, D_MODEL), wspec, wspec, _par_spec(D_MODEL)],
        out_specs=[_row_spec(tb, D_MODEL), _row_spec(tb, D_MODEL), wspec, _acc_spec(D_MODEL)],
        out_shape=[jax.ShapeDtypeStruct((n, D_MODEL), F32), jax.ShapeDtypeStruct((n, D_MODEL), BF16),
                   jax.ShapeDtypeStruct((N_POOL, POOL_DIM, POOL_DIM), F32), jax.ShapeDtypeStruct((8, D_MODEL), F32)],
        compiler_params=_cp(("arbitrary",)),
    )(dp, dsave, pw_bf, pwt_bf, pscale)


def _pool_bwd_b(dds, dd, consts_t):
    n = dd.shape[0]
    tb = POOL_TB
    nb = n // tb
    brow, bcol = consts_t
    prev, cur, nxt = _pool_halo_specs(n, D_MODEL)

    def body(p_ref, c_ref, n_ref, brow_ref, bcol_ref, dd_ref, o_ref):
        i = pl.program_id(0)
        for g in range(N_POOL):
            sl = slice(g * POOL_DIM, (g + 1) * POOL_DIM)
            box = _pool_box(p_ref, c_ref, n_ref, brow_ref, bcol_ref, g, i, nb)
            o_ref[:, sl] = (box - dd_ref[:, sl]).astype(BF16)

    return pl.pallas_call(
        body, name="pool_bwd_b", grid=(nb,),
        in_specs=[prev, cur, nxt] + _pool_const_specs() + [_row_spec(tb, D_MODEL)],
        out_specs=_row_spec(tb, D_MODEL),
        out_shape=jax.ShapeDtypeStruct((n, D_MODEL), BF16),
        compiler_params=_cp(("parallel",)),
    )(dds, dds, dds, brow, bcol, dd)


def _pair_add(slabs, recvs, core):
    na = len(slabs)
    hr = [t.shape[1] // 4 for t in slabs]

    def body(core_ref, *refs):
        for a in range(na):
            refs[2 * na + a][...] = (refs[a][...] + refs[na + a][...]).astype(BF16)

    own = [pl.BlockSpec((None, hr[a], slabs[a].shape[2]), lambda j, i, c_ref: (j, 2 * c_ref[0] + i, 0)) for a in range(na)]
    got = [pl.BlockSpec((None, hr[a], slabs[a].shape[2]), lambda j, i, c_ref: (j, i, 0)) for a in range(na)]
    return pl.pallas_call(
        body, name="reduce_g_pair",
        grid_spec=pltpu.PrefetchScalarGridSpec(num_scalar_prefetch=1, grid=(4, 2), in_specs=own + got, out_specs=got),
        out_shape=[jax.ShapeDtypeStruct(r.shape, BF16) for r in recvs],
        compiler_params=_cp(("arbitrary", "arbitrary")),
    )(core, *slabs, *recvs)


def _sum4(parts, core):
    na = len(parts)
    hr = [t.shape[1] // 2 for t in parts]

    def body(core_ref, *refs):
        for a in range(na):
            p = refs[a]
            refs[na + a][...] = ((p[0].astype(F32) + p[1].astype(F32)) + p[2].astype(F32)) + p[3].astype(F32)

    return pl.pallas_call(
        body, name="reduce_g_sum",
        grid_spec=pltpu.PrefetchScalarGridSpec(
            num_scalar_prefetch=1, grid=(2,),
            in_specs=[pl.BlockSpec((4, hr[a], parts[a].shape[2]), lambda i, c_ref: (0, i, 0)) for a in range(na)],
            out_specs=[pl.BlockSpec((hr[a], parts[a].shape[2]), lambda i, c_ref: (2 * c_ref[0] + i, 0))
                       for a in range(na)]),
        out_shape=[jax.ShapeDtypeStruct((2 * t.shape[1], t.shape[2]), F32) for t in parts],
        compiler_params=_cp(("arbitrary",)),
    )(core, *parts)


def _adamw(w, g, m, v, name):
    r, cdim = w.shape
    tb = _row_block(r, 256)
    c1 = 1.0 - ADAM_B1 ** ADAM_STEP
    c2 = 1.0 - ADAM_B2 ** ADAM_STEP

    def body(w_ref, g_ref, m_ref, v_ref, d_ref, nm_ref, nv_ref):
        gv = g_ref[...]
        nm = ADAM_B1 * m_ref[...] + (1.0 - ADAM_B1) * gv
        nv = ADAM_B2 * v_ref[...] + (1.0 - ADAM_B2) * (gv * gv)
        m_hat = nm / c1
        v_hat = nv / c2
        d_ref[...] = -ADAM_LR * (m_hat / (jnp.sqrt(v_hat) + ADAM_EPS) + ADAM_WD * w_ref[...])
        nm_ref[...] = nm
        nv_ref[...] = nv

    spec = _row_spec(tb, cdim)
    shp = jax.ShapeDtypeStruct((r, cdim), F32)
    return pl.pallas_call(
        body, name=name, grid=(r // tb,),
        in_specs=[spec] * 4, out_specs=[spec] * 3, out_shape=[shp] * 3,
        compiler_params=_cp(("parallel",)),
    )(w, g, m, v)


def _mesh_pos():
    return lax.axis_index("x"), lax.axis_index("y"), lax.axis_index("c")


_ANY = pl.BlockSpec(memory_space=pl.ANY)


def _remote(src, dst, send_sem, recv_sem, device):
    return pltpu.make_async_remote_copy(src_ref=src, dst_ref=dst, send_sem=send_sem, recv_sem=recv_sem,
                                        device_id=device, device_id_type=MESH)


def _other_chips(x, y):
    return [(1 - x, y), (x, 1 - y), (1 - x, 1 - y)]


def _half(nrows, h):
    return pl.ds(h * (nrows // 2), nrows // 2)


def _gather_weights(shards, conv8):
    na = len(shards)
    nrow = [t.shape[0] for t in shards]

    def body(*refs):
        ins, conv_in = refs[:na], refs[na]
        outs, conv_out = refs[na + 1:2 * na + 1], refs[2 * na + 1]
        send_sems, recv_sems, local_sems = refs[2 * na + 2:]
        x, y, c = _mesh_pos()
        me = 2 * x + y
        sib = (x, y, 1 - c)
        chips = _other_chips(x, y)

        def ici(k, a, slot):
            px, py = chips[k]
            rows = _half(nrow[a], c)
            return _remote(ins[a].at[rows, :], outs[a].at[slot, rows, :], send_sems.at[k * na + a],
                           recv_sems.at[k * na + a], (px, py, c))

        def fwd(k, a, h):
            px, py = chips[k]
            blk = outs[a].at[2 * px + py, _half(nrow[a], h), :]
            return _remote(blk, blk, send_sems.at[(3 + k) * na + a], recv_sems.at[(3 + k) * na + a], sib)

        def conv(k, slot):
            px, py = chips[k]
            return _remote(conv_in, conv_out.at[slot], send_sems.at[6 * na + k], recv_sems.at[6 * na + k], (px, py, c))

        def own(a):
            return _remote(ins[a], outs[a].at[me], send_sems.at[6 * na + 3 + a], recv_sems.at[6 * na + 3 + a], sib)

        local = [pltpu.make_async_copy(conv_in, conv_out.at[me], local_sems.at[0])]
        for cp in local:
            cp.start()
        sends = [own(a) for a in range(na)]
        for k in range(3):
            for a in range(na):
                sends.append(ici(k, a, me))
            sends.append(conv(k, me))
        for cp in sends:
            cp.start()
        for k in range(3):
            px, py = chips[k]
            for a in range(na):
                ici(k, a, 2 * px + py).wait_recv()
                cp = fwd(k, a, c)
                cp.start()
                sends.append(cp)
        for k in range(3):
            px, py = chips[k]
            for a in range(na):
                fwd(k, a, 1 - c).wait_recv()
            conv(k, 2 * px + py).wait_recv()
        for a in range(na):
            own(a).wait_recv()
        for cp in sends:
            cp.wait_send()
        for cp in local:
            cp.wait()

    nsem = 7 * na + 3
    return pl.pallas_call(
        body, name="gather_w", in_specs=[_ANY] * (na + 1), out_specs=[_ANY] * (na + 1),
        out_shape=[jax.ShapeDtypeStruct((4,) + t.shape, t.dtype) for t in shards] +
                  [jax.ShapeDtypeStruct((4,) + conv8.shape, conv8.dtype)],
        scratch_shapes=[pltpu.SemaphoreType.DMA((nsem,)), pltpu.SemaphoreType.DMA((nsem,)),
                        pltpu.SemaphoreType.DMA((1,))],
    )(*shards, conv8)


def _pair_swap(slabs):
    na = len(slabs)

    def body(*refs):
        ins, outs = refs[:na], refs[na:2 * na]
        send_sems, recv_sems = refs[2 * na:]
        x, y, c = _mesh_pos()
        cps = [_remote(ins[a].at[:, _half(slabs[a].shape[1], 1 - c), :], outs[a], send_sems.at[a], recv_sems.at[a],
                       (x, y, 1 - c)) for a in range(na)]
        for cp in cps:
            cp.start()
        for cp in cps:
            cp.wait()

    return pl.pallas_call(
        body, name="reduce_g_d2d", in_specs=[_ANY] * na, out_specs=[_ANY] * na,
        out_shape=[jax.ShapeDtypeStruct((4, t.shape[1] // 2, t.shape[2]), t.dtype) for t in slabs],
        scratch_shapes=[pltpu.SemaphoreType.DMA((na,)), pltpu.SemaphoreType.DMA((na,))],
    )(*slabs)


def _chip_exchange(pairs):
    na = len(pairs)

    def body(*refs):
        ins, outs = refs[:na], refs[na:2 * na]
        send_sems, recv_sems, local_sems = refs[2 * na:]
        x, y, c = _mesh_pos()
        me = 2 * x + y
        chips = _other_chips(x, y)

        def copy(k, a, slot):
            px, py = chips[k]
            return _remote(ins[a].at[2 * px + py], outs[a].at[slot], send_sems.at[k * na + a], recv_sems.at[k * na + a],
                           (px, py, c))

        local = [pltpu.make_async_copy(ins[a].at[me], outs[a].at[me], local_sems.at[a]) for a in range(na)]
        for cp in local:
            cp.start()
        sends = [copy(k, a, me) for k in range(3) for a in range(na)]
        for cp in sends:
            cp.start()
        for k in range(3):
            px, py = chips[k]
            for a in range(na):
                copy(k, a, 2 * px + py).wait_recv()
        for cp in sends:
            cp.wait_send()
        for cp in local:
            cp.wait()

    return pl.pallas_call(
        body, name="reduce_g_ici", in_specs=[_ANY] * na, out_specs=[_ANY] * na,
        out_shape=[jax.ShapeDtypeStruct(t.shape, t.dtype) for t in pairs],
        scratch_shapes=[pltpu.SemaphoreType.DMA((3 * na,)), pltpu.SemaphoreType.DMA((3 * na,)),
                        pltpu.SemaphoreType.DMA((na,))],
    )(*pairs)


def _share_halves(totals):
    na = len(totals)

    def body(*refs):
        bufs = refs[na:2 * na]
        send_sems, recv_sems = refs[2 * na:]
        x, y, c = _mesh_pos()

        def copy(a, h):
            blk = bufs[a].at[_half(totals[a].shape[0], h), :]
            return _remote(blk, blk, send_sems.at[a], recv_sems.at[a], (x, y, 1 - c))

        sends = [copy(a, c) for a in range(na)]
        for cp in sends:
            cp.start()
        for a in range(na):
            copy(a, 1 - c).wait_recv()
        for cp in sends:
            cp.wait_send()

    return pl.pallas_call(
        body, name="reduce_g_share", in_specs=[_ANY] * na, out_specs=[_ANY] * na,
        out_shape=[jax.ShapeDtypeStruct(t.shape, t.dtype) for t in totals],
        input_output_aliases={a: a for a in range(na)},
        scratch_shapes=[pltpu.SemaphoreType.DMA((na,)), pltpu.SemaphoreType.DMA((na,))],
    )(*totals)


def _allreduce_small(v, name):
    r, cdim = v.shape

    def body(v_ref, out_ref, buf, send_sems, recv_sems):
        x, y, c = _mesh_pos()
        me = 4 * x + 2 * y + c
        buf[me] = v_ref[...]
        rel = [(bx, by, bc) for bx in (0, 1) for by in (0, 1) for bc in (0, 1)][1:]

        def peer(b):
            bx, by, bc = b
            return ((1 - x) if bx else x, (1 - y) if by else y, (1 - c) if bc else c)

        def copy(k, slot):
            return pltpu.make_async_remote_copy(
                src_ref=v_ref, dst_ref=buf.at[slot], send_sem=send_sems.at[k], recv_sem=recv_sems.at[k],
                device_id=peer(rel[k]), device_id_type=MESH)

        sends = [copy(k, me) for k in range(7)]
        for cp in sends:
            cp.start()
        for k in range(7):
            px, py, pc = peer(rel[k])
            copy(k, 4 * px + 2 * py + pc).wait_recv()
        for cp in sends:
            cp.wait_send()
        acc = buf[0]
        for j in range(1, 8):
            acc = acc + buf[j]
        out_ref[...] = acc

    vm = pl.BlockSpec(memory_space=pltpu.VMEM)
    return pl.pallas_call(
        body, name=name, in_specs=[vm], out_specs=vm,
        out_shape=jax.ShapeDtypeStruct((r, cdim), F32),
        scratch_shapes=[pltpu.VMEM((8, r, cdim), F32), pltpu.SemaphoreType.DMA((7,)), pltpu.SemaphoreType.DMA((7,))],
    )(v)


_BIG = (("in_proj", (D_MODEL, D_IN_PROJ // 4), 1), ("w_out", (2 * D_MODEL // 4, D_MODEL), 0),
        ("w_gate", (D_MODEL, D_FF // 4), 1), ("w_up", (D_MODEL, D_FF // 4), 1), ("w_down", (D_FF // 4, D_MODEL), 0),
        ("pool_w", (N_POOL * POOL_DIM // 4, POOL_DIM), None), ("w_ada", (D_MODEL, 6 * D_MODEL // 4), 1))


def _assemble(name, t):
    _, r, c = t.shape
    axis = {n: ax for n, _, ax in _BIG}[name]
    if axis == 0:
        return t.reshape(4 * r, c)
    if axis == 1:
        return t.transpose(1, 0, 2).reshape(r, 4 * c)
    return t.reshape(4, N_POOL, POOL_DIM // 4, POOL_DIM).transpose(1, 0, 2, 3).reshape(N_POOL, POOL_DIM, POOL_DIM)


def _to_slabs(name, g):
    (r, c), axis = {n: (sh, ax) for n, sh, ax in _BIG}[name]
    if axis == 0:
        return g.reshape(4, r, c)
    if axis == 1:
        return g.reshape(r, 4, c).transpose(1, 0, 2)
    return g.reshape(N_POOL, 4, POOL_DIM // 4, POOL_DIM).transpose(1, 0, 2, 3).reshape(4, r, c)


def _reduce_grads(full_grads, ci):
    names = [n for n, _, _ in _BIG]
    slabs = [_to_slabs(n, full_grads[n]) for n in names]
    core = ci.reshape(1).astype(jnp.int32)
    recvs = _pair_swap(slabs)
    pairs = _pair_add(slabs, recvs, core)
    parts = _chip_exchange(pairs)
    totals = _sum4(parts, core)
    return dict(zip(names, _share_halves(totals)))


def _pad_cols(w, n):
    return jnp.concatenate([w, jnp.zeros((w.shape[0], n - w.shape[1]), w.dtype)], axis=1)


def _device_step(x, c, ctx, target, wts, w8, small, tb):
    n = x.shape[0]
    d = D_MODEL
    c_ctx = small["c_ctx"]

    win = wts["in_proj"]
    wz, wxd, wup = win[:, 0:d], _pad_cols(win[:, d:d + D_XBC + 2 * SSD_HEADS], D_XD), win[:, d + D_XBC + 2 * SSD_HEADS:]
    wout = wts["w_out"]
    wg, wu, wd = wts["w_gate"], wts["w_up"], wts["w_down"]
    pw = wts["pool_w"]
    wada = wts["w_ada"]

    emb_g, emb_b = _vec(small["emb_ln_g"]), _vec(small["emb_ln_b"])
    ln1_g, ln1_b = _vec(small["ln1_g"]), _vec(small["ln1_b"])
    ln2_g, ln2_b = _vec(small["ln2_g"]), _vec(small["ln2_b"])
    gn = _vec(small["ssd_norm_g"])
    pscale = _vec(small["pool_scale"])
    conv_b = _vec(small["conv_b"])
    dskip_e = jnp.repeat(small["d_skip"].reshape(-1), HEAD_DIM).reshape(1, d)
    zpad = jnp.zeros((2, 1, 128 - SSD_HEADS), F32)
    bias2 = jnp.concatenate([small["dt_bias"].reshape(2, 1, SSD_HEADS), zpad], axis=2)
    a2 = jnp.concatenate([-jnp.exp(small["a_log"].reshape(2, 1, SSD_HEADS)), zpad], axis=2)
    rexp = (jnp.arange(128)[:, None] == (jnp.arange(d)[None, :] // HEAD_DIM)).astype(BF16)
    rexp_t = rexp.T

    c8 = jnp.concatenate([c.reshape(1, d), c_ctx.reshape(1, d), jnp.zeros((6, d), F32)], axis=0)
    mods = _mods_fwd(c8, wada, _vec(small["b_ada"]))
    sh1, sc1, g1, sh2, sc2, g2 = [mods[0:1, i * d:(i + 1) * d] for i in range(6)]
    sh1c, sc1c = mods[1:2, 0:d], mods[1:2, d:2 * d]

    tbc = min(tb, ctx.shape[0])
    xc0, hc = _ln_mod(ctx, emb_g, emb_b, sh1c, sc1c, tbc, "ln_mod_ctx")
    xdc = _matmul_nn([(hc, wxd)], F32, 512, D_XD, "in_proj_ctx")
    uc = _conv_fwd(xdc, w8, conv_b, tbc, "conv_fwd_ctx")
    hzero = jnp.zeros((2, D_STATE, d), F32)
    _, hprev_c, hfin_c = _ssd_fwd(uc, xdc, bias2, a2, rexp, hzero, "ssd_fwd_ctx")

    x0, h1 = _ln_mod(x, emb_g, emb_b, sh1, sc1, tb, "ln_mod")
    z = _matmul_nn([(h1, wz)], F32, MM_ROWS, 1024, "in_proj_z")
    xd = _matmul_nn([(h1, wxd)], F32, MM_ROWS, D_XD, "in_proj_xd")
    up = _matmul_nn([(h1, wup)], F32, MM_ROWS, 1024, "in_proj_up")
    u = _conv_fwd(xd, w8, conv_b, tb, "conv_fwd")
    y, hprev, _ = _ssd_fwd(u, xd, bias2, a2, rexp, hfin_c, "ssd_fwd")
    yn = _merge_fwd(y, u, z, dskip_e, gn, tb)
    pconst = _pool_consts(False)
    pool, dsave = _pool_fwd(up, pconst, pw, pscale)
    mix = _matmul_nn([(yn, wout[0:d]), (pool, wout[d:2 * d])], F32, MM_ROWS, 1024, "out_proj")
    x1, h2 = _res_ln(x0, mix, g1, ln1_g, ln1_b, sh2, sc2, tb)

    gate, upp, hmid = _swiglu_fwd(h2, wg, wu, 512, D_FF // 2)
    ffn = _matmul_nn([(hmid, wd)], F32, MM_ROWS, 1024, "ffn_down")
    dffn, dr2, acc2 = _final_ln_loss(x1, ffn, g2, ln2_g, ln2_b, target, tb)
    loss = (0.5 / d) * jnp.sum(acc2[3])

    dgate, dupp = _swiglu_bwd(dffn, wd.T, gate, upp, 512, D_FF // 2)
    g_wdown = _matmul_tn(hmid, dffn, MM_ROWS, 1024, "g_w_down")
    g_wgate = _matmul_tn(h2, dgate, MM_ROWS, 1408, "g_w_gate")
    g_wup = _matmul_tn(h2, dupp, MM_ROWS, 1408, "g_w_up")
    dh2 = _matmul_nn([(dgate, wg.T), (dupp, wu.T)], F32, 512, 1024, "d_h2")
    dmix, dr1, acc1 = _bwd_ln1(dr2, dh2, x1, x0, mix, g1, sc2, ln1_g, tb)

    dyn = _matmul_nn([(dmix, wout[0:d].T)], F32, MM_ROWS, 1024, "d_yn")
    dpool = _matmul_nn([(dmix, wout[d:2 * d].T)], F32, MM_ROWS, 1024, "d_pool")
    g_wout = jnp.concatenate([_matmul_tn(yn, dmix, MM_ROWS, 1024, "g_w_out_a"),
                              _matmul_tn(pool, dmix, MM_ROWS, 1024, "g_w_out_b")], axis=0)
    dd, dds, g_pw, accp = _pool_bwd_a(dpool, dsave, pw, jnp.swapaxes(pw, 1, 2), pscale)
    dup = _pool_bwd_b(dds, dd, _pool_consts(True))
    dy, dz, accm = _merge_bwd(dyn, y, u, z, dskip_e, gn, tb)
    lam0 = jnp.zeros((2, D_STATE, d), F32)
    dxs, dbc, ddt, accs, lam_c = _ssd_bwd(u, xd, bias2, a2, rexp, rexp_t, dy, hprev, lam0, "ssd_bwd")
    du, accb = _conv_bwd_a(dxs, dy, dskip_e, dbc, u, tb, "conv_bwd_a")
    dxd, accw = _conv_bwd_b(du, xd, ddt, w8, tb, "conv_bwd_b")

    lc = ctx.shape[0]
    zeros_c = jnp.zeros((lc, d), F32)
    dxs_c, dbc_c, ddt_c, accs_c, _ = _ssd_bwd(uc, xdc, bias2, a2, rexp, rexp_t, zeros_c, hprev_c, lam_c, "ssd_bwd_ctx")
    du_c, accb_c = _conv_bwd_a(dxs_c, zeros_c, dskip_e, dbc_c, uc, tbc, "conv_bwd_a_ctx")
    dxd_c, accw_c = _conv_bwd_b(du_c, xdc, ddt_c, w8, tbc, "conv_bwd_b_ctx")
    dhc = _matmul_nn([(dxd_c, wxd.T)], F32, 512, 1024, "d_hc")
    _, acc0c = _bwd_ln0(None, dhc, ctx, emb_g, emb_b, sc1c, tbc, "bwd_ln0_ctx")

    dh1 = _matmul_nn([(dz, wz.T), (dxd, wxd.T), (dup, wup.T)], F32, MM_ROWS, 1024, "d_h1")
    g_wz = _matmul_tn(h1, dz, MM_ROWS, 1024, "g_in_proj_z")
    g_wxd = _matmul_tn(h1, dxd, MM_ROWS, D_XD, "g_in_proj_xd") + _matmul_tn(hc, dxd_c, 512, D_XD, "g_in_proj_xd_ctx")
    g_wpo = _matmul_tn(h1, dup, MM_ROWS, 1024, "g_in_proj_up")
    g_win = jnp.concatenate([g_wz, g_wxd[:, 0:D_XBC + 2 * SSD_HEADS], g_wpo], axis=1)
    grad_x, acc0 = _bwd_ln0(dr1, dh1, x, emb_g, emb_b, sc1, tb, "bwd_ln0")

    zero_d = jnp.zeros((1, d), F32)
    dmod = jnp.concatenate([acc0[1:2], acc0[0:1], acc1[4:5], acc1[1:2], acc1[0:1], acc2[2:3]], axis=1)
    dmodc = jnp.concatenate([acc0c[1:2], acc0c[0:1]] + [zero_d] * 4, axis=1)
    dm8 = jnp.concatenate([dmod, dmodc, jnp.zeros((6, 6 * d), F32)], axis=0)
    g_wada, g_bada8 = _mods_bwd_w(c8.T, dm8)
    g_cctx8 = _mods_bwd_c(dm8, wada, c8)

    big = dict(in_proj=g_win, w_out=g_wout, w_gate=g_wgate, w_up=g_wup, w_down=g_wdown, pool_w=g_pw, w_ada=g_wada)
    sml = dict(
        c_ctx=g_cctx8[1], emb_ln_g=acc0[2] + acc0c[2], emb_ln_b=acc0[3] + acc0c[3], b_ada=g_bada8[0],
        conv_w=accw[0:D_CONV] + accw_c[0:D_CONV], conv_b=accb[0] + accb_c[0],
        dt_bias=accs[:, 0, 0:SSD_HEADS] + accs_c[:, 0, 0:SSD_HEADS],
        a_log=accs[:, 1, 0:SSD_HEADS] + accs_c[:, 1, 0:SSD_HEADS],
        d_skip=jnp.sum(accm[1].reshape(SSD_HEADS, HEAD_DIM), axis=1),
        ssd_norm_g=accm[0], pool_scale=accp[0], ln1_g=acc1[2], ln1_b=acc1[3], ln2_g=acc2[0], ln2_b=acc2[1])
    return loss, grad_x, big, sml


_SMALL = ("c_ctx", "emb_ln_g", "emb_ln_b", "b_ada", "conv_w", "conv_b", "dt_bias", "a_log", "d_skip",
          "ssd_norm_g", "pool_scale", "ln1_g", "ln1_b", "ln2_g", "ln2_b")


_SMALL_ROWS = 8


def _pack_small(vals, names):
    rows = []
    for nme in names:
        flat = vals[nme].reshape(-1).astype(F32)
        assert flat.shape[0] <= _SMALL_ROWS * 1024
        rows.append(jnp.concatenate([flat, jnp.zeros((_SMALL_ROWS * 1024 - flat.shape[0],), F32)]).reshape(_SMALL_ROWS, 1024))
    return jnp.concatenate(rows, axis=0)


def _unpack_small(packed, shapes, names):
    out = {}
    for i, nme in enumerate(names):
        size = math.prod(shapes[nme])
        out[nme] = packed[i * _SMALL_ROWS:(i + 1) * _SMALL_ROWS].reshape(-1)[:size].reshape(shapes[nme])
    return out


_WEIGHT_ORDER = ("c_ctx", "emb_ln_g", "emb_ln_b", "w_ada", "b_ada", "in_proj", "conv_w", "conv_b", "dt_bias", "a_log",
                 "d_skip", "ssd_norm_g", "pool_w", "pool_scale", "w_out", "ln1_g", "ln1_b", "w_gate", "w_up", "w_down",
                 "ln2_g", "ln2_b")


def _as2d(a):
    return a.reshape(-1, a.shape[-1])


def kernel(x, c, ctx, c_ctx, emb_ln_g, emb_ln_b, w_ada, b_ada, in_proj, conv_w, conv_b, dt_bias, a_log, d_skip, ssd_norm_g, pool_w, pool_scale, w_out, ln1_g, ln1_b, w_gate, w_up, w_down, ln2_g, ln2_b, loss_target, m_c_ctx, m_emb_ln_g, m_emb_ln_b, m_w_ada, m_b_ada, m_in_proj, m_conv_w, m_conv_b, m_dt_bias, m_a_log, m_d_skip, m_ssd_norm_g, m_pool_w, m_pool_scale, m_w_out, m_ln1_g, m_ln1_b, m_w_gate, m_w_up, m_w_down, m_ln2_g, m_ln2_b, v_c_ctx, v_emb_ln_g, v_emb_ln_b, v_w_ada, v_b_ada, v_in_proj, v_conv_w, v_conv_b, v_dt_bias, v_a_log, v_d_skip, v_ssd_norm_g, v_pool_w, v_pool_scale, v_w_out, v_ln1_g, v_ln1_b, v_w_gate, v_w_up, v_w_down, v_ln2_g, v_ln2_b):
    w = dict(c_ctx=c_ctx, emb_ln_g=emb_ln_g, emb_ln_b=emb_ln_b, w_ada=w_ada, b_ada=b_ada, in_proj=in_proj, conv_w=conv_w,
             conv_b=conv_b, dt_bias=dt_bias, a_log=a_log, d_skip=d_skip, ssd_norm_g=ssd_norm_g, pool_w=pool_w,
             pool_scale=pool_scale, w_out=w_out, ln1_g=ln1_g, ln1_b=ln1_b, w_gate=w_gate, w_up=w_up, w_down=w_down,
             ln2_g=ln2_g, ln2_b=ln2_b)
    m = dict(c_ctx=m_c_ctx, emb_ln_g=m_emb_ln_g, emb_ln_b=m_emb_ln_b, w_ada=m_w_ada, b_ada=m_b_ada, in_proj=m_in_proj,
             conv_w=m_conv_w, conv_b=m_conv_b, dt_bias=m_dt_bias, a_log=m_a_log, d_skip=m_d_skip,
             ssd_norm_g=m_ssd_norm_g, pool_w=m_pool_w, pool_scale=m_pool_scale, w_out=m_w_out, ln1_g=m_ln1_g,
             ln1_b=m_ln1_b, w_gate=m_w_gate, w_up=m_w_up, w_down=m_w_down, ln2_g=m_ln2_g, ln2_b=m_ln2_b)
    v = dict(c_ctx=v_c_ctx, emb_ln_g=v_emb_ln_g, emb_ln_b=v_emb_ln_b, w_ada=v_w_ada, b_ada=v_b_ada, in_proj=v_in_proj,
             conv_w=v_conv_w, conv_b=v_conv_b, dt_bias=v_dt_bias, a_log=v_a_log, d_skip=v_d_skip,
             ssd_norm_g=v_ssd_norm_g, pool_w=v_pool_w, pool_scale=v_pool_scale, w_out=v_w_out, ln1_g=v_ln1_g,
             ln1_b=v_ln1_b, w_gate=v_w_gate, w_up=v_w_up, w_down=v_w_down, ln2_g=v_ln2_g, ln2_b=v_ln2_b)

    xi, yi, ci = _mesh_pos()
    chip = 2 * xi + yi

    shards = [w[name][0].astype(BF16).reshape(shp) for name, shp, _ in _BIG]
    conv8 = jnp.concatenate([conv_w[0], jnp.zeros((8 - D_CONV, conv_w.shape[-1]), F32)], axis=0)
    *gathered, conv4 = _gather_weights(shards, conv8)
    wts = {name: _assemble(name, t) for (name, _, _), t in zip(_BIG, gathered)}
    w8 = conv4.transpose(1, 0, 2).reshape(8, D_XBC)
    small = {nme: (w[nme] if nme in ("c_ctx", "emb_ln_g", "emb_ln_b") else w[nme][0]) for nme in _SMALL if nme != "conv_w"}

    loss, grad_x, big, sml = _device_step(x[0], c, ctx[0], loss_target[0], wts, w8, small, 512)
    loss = lax.psum(loss, ("x", "y", "c"))

    g_big = _reduce_grads(big, ci)
    small_shapes = {nme: sml[nme].shape for nme in _SMALL}
    g_small = _unpack_small(_allreduce_small(_pack_small(sml, _SMALL), "reduce_small"), small_shapes, _SMALL)
    cw_cols = conv_w.shape[-1]
    g_small["conv_w"] = lax.dynamic_slice_in_dim(g_small["conv_w"], chip * cw_cols, cw_cols, axis=1)

    grads, delta, new_m, new_v = {}, {}, {}, {}
    for name, _, _ in _BIG:
        g2 = _as2d(g_big[name])
        d2, m2, v2 = _adamw(_as2d(w[name][0]), g2, _as2d(m[name][0]), _as2d(v[name][0]), "adamw_" + name)
        grads[name] = g2.reshape(w[name].shape)
        delta[name], new_m[name], new_v[name] = (t.reshape(w[name].shape) for t in (d2, m2, v2))
    shp = {nme: w[nme].shape for nme in _SMALL}
    gp = _pack_small(g_small, _SMALL)
    dp, mp, vp = _adamw(_pack_small(w, _SMALL), gp, _pack_small(m, _SMALL), _pack_small(v, _SMALL), "adamw_small")
    for dst, src in ((grads, gp), (delta, dp), (new_m, mp), (new_v, vp)):
        dst.update(_unpack_small(src, shp, _SMALL))

    return (loss, grad_x[None], *[grads[nme] for nme in _WEIGHT_ORDER], *[delta[nme] for nme in _WEIGHT_ORDER],
            *[new_m[nme] for nme in _WEIGHT_ORDER], *[new_v[nme] for nme in _WEIGHT_ORDER])
```

```python
import functools
import math

import jax
import jax.numpy as jnp
from jax import lax
from jax.experimental import pallas as pl
from jax.experimental.pallas import tpu as pltpu

F32 = jnp.float32
BF16 = jnp.bfloat16
MESH = pl.DeviceIdType.MESH

D_MODEL = 1024
SSD_HEADS = 16
HEAD_DIM = 64
D_STATE = 128
CHUNK = 128
D_CONV = 5
D_XBC = D_MODEL + 2 * D_STATE
D_XD = 1408
N_POOL = 4
POOL_DIM = 256
POOL_WINDOWS = (2, 4, 8, 16)
GRID_W = 64
D_FF = 2816
D_IN_PROJ = 3360
LN_EPS = 1e-5
ALPHA = 2.0 ** 0.25
POOL_TB = 512
MM_ROWS = 1024

ADAM_LR = 0.001
ADAM_B1 = 0.9
ADAM_B2 = 0.999
ADAM_EPS = 1e-08
ADAM_WD = 0.01
ADAM_STEP = 10

VMEM_LIMIT = 56 * 1024 * 1024


def _cp(sem=None):
    return pltpu.CompilerParams(dimension_semantics=sem, vmem_limit_bytes=VMEM_LIMIT)


def _sigmoid(x):
    return 1.0 / (1.0 + jnp.exp(-x))


def _silu(x):
    return x * _sigmoid(x)


def _dsilu(x):
    s = _sigmoid(x)
    return s * (1.0 + x * (1.0 - s))


def _softplus(x):
    t = jnp.exp(-jnp.abs(x))
    u = 1.0 + t
    log1p = jnp.where(u == 1.0, t, jnp.log(u) * t / (u - 1.0 + (u == 1.0)))
    return jnp.maximum(x, 0.0) + log1p


def _split(x, n):
    parts, r = [], x
    for _ in range(n):
        p = r.astype(BF16)
        parts.append(p)
        r = r - p.astype(F32)
    return parts


def _dot(a, b):
    return jnp.dot(a, b, preferred_element_type=F32)


def _dot_nt(a, b):
    return lax.dot_general(a, b, (((1,), (1,)), ((), ())), preferred_element_type=F32)


def _dot_tn(a, b):
    return lax.dot_general(a, b, (((0,), (0,)), ((), ())), preferred_element_type=F32)


def _dot_sel_l(sel_bf, x, n=3):
    out = None
    for p in _split(x, n):
        t = _dot(sel_bf, p)
        out = t if out is None else out + t
    return out


def _dot_sel_r(x, sel_bf, n=3):
    out = None
    for p in _split(x, n):
        t = _dot(p, sel_bf)
        out = t if out is None else out + t
    return out


def _row_block(n, cap=256, mult=8):
    best = None
    for t in range(mult, min(n, cap) + 1, mult):
        if n % t == 0:
            best = t
    return best if best is not None else n


def _vec(v):
    return v.reshape(1, -1).astype(F32)


def _mods_fwd(c8, wada_bf, b_ada):
    d = c8.shape[1]
    n = wada_bf.shape[1]

    def body(c_ref, w_ref, b_ref, o_ref):
        s = _silu(c_ref[...]).astype(BF16)
        o_ref[...] = _dot(s, w_ref[...]) + b_ref[...]

    return pl.pallas_call(
        body, name="mods_fwd", grid=(n // d,),
        in_specs=[pl.BlockSpec((8, d), lambda j: (0, 0)),
                  pl.BlockSpec((d, d), lambda j: (0, j)),
                  pl.BlockSpec((1, d), lambda j: (0, j))],
        out_specs=pl.BlockSpec((8, d), lambda j: (0, j)),
        out_shape=jax.ShapeDtypeStruct((8, n), F32),
        compiler_params=_cp(("arbitrary",)),
    )(c8, wada_bf, b_ada)


def _mods_bwd_w(ct8, dm8):
    d = ct8.shape[0]
    n = dm8.shape[1]
    tn = 512

    def body(ct_ref, dm_ref, dw_ref, db_ref):
        s = _silu(ct_ref[...])
        dm = dm_ref[...]
        dw_ref[...] = s[:, 0:1] * dm[0:1, :] + s[:, 1:2] * dm[1:2, :]
        db_ref[...] = jnp.broadcast_to(dm[0:1, :] + dm[1:2, :], (8, tn))

    return pl.pallas_call(
        body, name="mods_bwd_w", grid=(n // tn,),
        in_specs=[pl.BlockSpec((d, 8), lambda j: (0, 0)),
                  pl.BlockSpec((8, tn), lambda j: (0, j))],
        out_specs=[pl.BlockSpec((d, tn), lambda j: (0, j)),
                   pl.BlockSpec((8, tn), lambda j: (0, j))],
        out_shape=[jax.ShapeDtypeStruct((d, n), F32), jax.ShapeDtypeStruct((8, n), F32)],
        compiler_params=_cp(("arbitrary",)),
    )(ct8, dm8)


def _mods_bwd_c(dm8, wada_bf, c8):
    d = c8.shape[1]
    n = dm8.shape[1]
    nk = n // d

    def body(dm_ref, w_ref, c_ref, o_ref):
        k = pl.program_id(0)

        @pl.when(k == 0)
        def _():
            o_ref[...] = jnp.zeros_like(o_ref)

        o_ref[...] += _dot_nt(dm_ref[...].astype(BF16), w_ref[...])

        @pl.when(k == nk - 1)
        def _():
            o_ref[...] = o_ref[...] * _dsilu(c_ref[...])

    return pl.pallas_call(
        body, name="mods_bwd_c", grid=(nk,),
        in_specs=[pl.BlockSpec((8, d), lambda k: (0, k)),
                  pl.BlockSpec((d, d), lambda k: (0, k)),
                  pl.BlockSpec((8, d), lambda k: (0, 0))],
        out_specs=pl.BlockSpec((8, d), lambda k: (0, 0)),
        out_shape=jax.ShapeDtypeStruct((8, d), F32),
        compiler_params=_cp(("arbitrary",)),
    )(dm8, wada_bf, c8)


def _ln_stats(x):
    mu = jnp.mean(x, axis=-1, keepdims=True)
    xc = x - mu
    var = jnp.mean(xc * xc, axis=-1, keepdims=True)
    rstd = lax.rsqrt(var + LN_EPS)
    return xc * rstd, rstd


def _ln_bwd(dxhat, xhat, rstd):
    m1 = jnp.mean(dxhat, axis=-1, keepdims=True)
    m2 = jnp.mean(dxhat * xhat, axis=-1, keepdims=True)
    return rstd * (dxhat - m1 - xhat * m2)


def _row_spec(tb, d):
    return pl.BlockSpec((tb, d), lambda i: (i, 0))


def _par_spec(d):
    return pl.BlockSpec((1, d), lambda i: (0, 0))


def _acc_spec(d):
    return pl.BlockSpec((8, d), lambda i: (0, 0))


def _ln_mod(x, g, b, sh, sc, tb, name):
    n, d = x.shape

    def body(x_ref, g_ref, b_ref, sh_ref, sc_ref, x0_ref, h_ref):
        xhat, _ = _ln_stats(x_ref[...])
        x0 = xhat * g_ref[...] + b_ref[...]
        x0_ref[...] = x0
        h_ref[...] = (x0 * (1.0 + sc_ref[...]) + sh_ref[...]).astype(BF16)

    return pl.pallas_call(
        body, name=name, grid=(n // tb,),
        in_specs=[_row_spec(tb, d)] + [_par_spec(d)] * 4,
        out_specs=[_row_spec(tb, d), _row_spec(tb, d)],
        out_shape=[jax.ShapeDtypeStruct((n, d), F32), jax.ShapeDtypeStruct((n, d), BF16)],
        compiler_params=_cp(("parallel",)),
    )(x, g, b, sh, sc)


def _res_ln(xres, mix, gate, g, b, sh, sc, tb):
    n, d = xres.shape

    def body(xr_ref, mix_ref, gate_ref, g_ref, b_ref, sh_ref, sc_ref, x1_ref, h_ref):
        r = ALPHA * xr_ref[...] + gate_ref[...] * mix_ref[...]
        xhat, _ = _ln_stats(r)
        x1 = xhat * g_ref[...] + b_ref[...]
        x1_ref[...] = x1
        h_ref[...] = (x1 * (1.0 + sc_ref[...]) + sh_ref[...]).astype(BF16)

    return pl.pallas_call(
        body, name="res_ln1", grid=(n // tb,),
        in_specs=[_row_spec(tb, d)] * 2 + [_par_spec(d)] * 5,
        out_specs=[_row_spec(tb, d), _row_spec(tb, d)],
        out_shape=[jax.ShapeDtypeStruct((n, d), F32), jax.ShapeDtypeStruct((n, d), BF16)],
        compiler_params=_cp(("parallel",)),
    )(xres, mix, gate, g, b, sh, sc)


def _final_ln_loss(x1, ffn, gate, g, b, target, tb):
    n, d = x1.shape

    def body(x1_ref, ffn_ref, gate_ref, g_ref, b_ref, t_ref, dffn_ref, dr_ref, acc_ref):
        i = pl.program_id(0)

        @pl.when(i == 0)
        def _():
            acc_ref[...] = jnp.zeros_like(acc_ref)

        ffn = ffn_ref[...]
        r = ALPHA * x1_ref[...] + gate_ref[...] * ffn
        xhat, rstd = _ln_stats(r)
        err = xhat * g_ref[...] + b_ref[...] - t_ref[...]
        dx2 = err * (1.0 / d)
        dr = _ln_bwd(dx2 * g_ref[...], xhat, rstd)
        dr_ref[...] = dr
        dffn_ref[...] = (gate_ref[...] * dr).astype(BF16)
        acc_ref[0:1, :] += jnp.sum(dx2 * xhat, axis=0, keepdims=True)
        acc_ref[1:2, :] += jnp.sum(dx2, axis=0, keepdims=True)
        acc_ref[2:3, :] += jnp.sum(dr * ffn, axis=0, keepdims=True)
        acc_ref[3:4, :] += jnp.sum(err * err, axis=0, keepdims=True)

    return pl.pallas_call(
        body, name="final_ln_loss", grid=(n // tb,),
        in_specs=[_row_spec(tb, d)] * 2 + [_par_spec(d)] * 3 + [_row_spec(tb, d)],
        out_specs=[_row_spec(tb, d), _row_spec(tb, d), _acc_spec(d)],
        out_shape=[jax.ShapeDtypeStruct((n, d), BF16), jax.ShapeDtypeStruct((n, d), F32),
                   jax.ShapeDtypeStruct((8, d), F32)],
        compiler_params=_cp(("arbitrary",)),
    )(x1, ffn, gate, g, b, target)


def _bwd_ln1(dr2, dh2, x1, x0, mix, gate, sc2, g, tb):
    n, d = x1.shape

    def body(dr2_ref, dh2_ref, x1_ref, x0_ref, mix_ref, gate_ref, sc_ref, g_ref, dmix_ref, dr1_ref, acc_ref):
        i = pl.program_id(0)

        @pl.when(i == 0)
        def _():
            acc_ref[...] = jnp.zeros_like(acc_ref)

        dh2 = dh2_ref[...]
        mix = mix_ref[...]
        dx1 = ALPHA * dr2_ref[...] + dh2 * (1.0 + sc_ref[...])
        r = ALPHA * x0_ref[...] + gate_ref[...] * mix
        xhat, rstd = _ln_stats(r)
        dr1 = _ln_bwd(dx1 * g_ref[...], xhat, rstd)
        dr1_ref[...] = dr1
        dmix_ref[...] = (gate_ref[...] * dr1).astype(BF16)
        acc_ref[0:1, :] += jnp.sum(dh2 * x1_ref[...], axis=0, keepdims=True)
        acc_ref[1:2, :] += jnp.sum(dh2, axis=0, keepdims=True)
        acc_ref[2:3, :] += jnp.sum(dx1 * xhat, axis=0, keepdims=True)
        acc_ref[3:4, :] += jnp.sum(dx1, axis=0, keepdims=True)
        acc_ref[4:5, :] += jnp.sum(dr1 * mix, axis=0, keepdims=True)

    return pl.pallas_call(
        body, name="bwd_ln1", grid=(n // tb,),
        in_specs=[_row_spec(tb, d)] * 5 + [_par_spec(d)] * 3,
        out_specs=[_row_spec(tb, d), _row_spec(tb, d), _acc_spec(d)],
        out_shape=[jax.ShapeDtypeStruct((n, d), BF16), jax.ShapeDtypeStruct((n, d), F32),
                   jax.ShapeDtypeStruct((8, d), F32)],
        compiler_params=_cp(("arbitrary",)),
    )(dr2, dh2, x1, x0, mix, gate, sc2, g)


def _bwd_ln0(dres, dh, x, g, b, sc, tb, name):
    n, d = x.shape
    has_res = dres is not None

    def body(*refs):
        if has_res:
            dres_ref, dh_ref, x_ref, g_ref, b_ref, sc_ref, dx_ref, acc_ref = refs
        else:
            dh_ref, x_ref, g_ref, b_ref, sc_ref, dx_ref, acc_ref = refs
        i = pl.program_id(0)

        @pl.when(i == 0)
        def _():
            acc_ref[...] = jnp.zeros_like(acc_ref)

        dh = dh_ref[...]
        xhat, rstd = _ln_stats(x_ref[...])
        x0 = xhat * g_ref[...] + b_ref[...]
        dx0 = dh * (1.0 + sc_ref[...])
        if has_res:
            dx0 = dx0 + ALPHA * dres_ref[...]
        dx_ref[...] = _ln_bwd(dx0 * g_ref[...], xhat, rstd)
        acc_ref[0:1, :] += jnp.sum(dh * x0, axis=0, keepdims=True)
        acc_ref[1:2, :] += jnp.sum(dh, axis=0, keepdims=True)
        acc_ref[2:3, :] += jnp.sum(dx0 * xhat, axis=0, keepdims=True)
        acc_ref[3:4, :] += jnp.sum(dx0, axis=0, keepdims=True)

    ins = ([dres] if has_res else []) + [dh, x, g, b, sc]
    return pl.pallas_call(
        body, name=name, grid=(n // tb,),
        in_specs=[_row_spec(tb, d)] * (3 if has_res else 2) + [_par_spec(d)] * 3,
        out_specs=[_row_spec(tb, d), _acc_spec(d)],
        out_shape=[jax.ShapeDtypeStruct((n, d), F32), jax.ShapeDtypeStruct((8, d), F32)],
        compiler_params=_cp(("arbitrary",)),
    )(*ins)


def _matmul_nn(pairs, out_dtype, tm, tn, name):
    m = pairs[0][0].shape[0]
    n = pairs[0][1].shape[1]
    tm = min(tm, m)
    tn = min(tn, n)
    npair = len(pairs)

    def body(*refs):
        o_ref = refs[-1]
        acc = None
        for p in range(npair):
            t = _dot(refs[2 * p][...].astype(BF16), refs[2 * p + 1][...])
            acc = t if acc is None else acc + t
        o_ref[...] = acc.astype(out_dtype)

    in_specs, args = [], []
    for a, b in pairs:
        k = a.shape[1]
        in_specs += [pl.BlockSpec((tm, k), lambda i, j: (i, 0)), pl.BlockSpec((k, tn), lambda i, j: (0, j))]
        args += [a, b]
    return pl.pallas_call(
        body, name=name, grid=(m // tm, n // tn),
        in_specs=in_specs,
        out_specs=pl.BlockSpec((tm, tn), lambda i, j: (i, j)),
        out_shape=jax.ShapeDtypeStruct((m, n), out_dtype),
        compiler_params=_cp(("parallel", "arbitrary")),
    )(*args)


def _matmul_tn(a, g, tm, tn, name):
    m, k = a.shape
    n = g.shape[1]
    tm = min(tm, m)
    tn = min(tn, n)

    def body(a_ref, g_ref, o_ref):
        i = pl.program_id(1)

        @pl.when(i == 0)
        def _():
            o_ref[...] = jnp.zeros_like(o_ref)

        o_ref[...] += _dot_tn(a_ref[...].astype(BF16), g_ref[...].astype(BF16))

    return pl.pallas_call(
        body, name=name, grid=(n // tn, m // tm),
        in_specs=[pl.BlockSpec((tm, k), lambda j, i: (i, 0)), pl.BlockSpec((tm, tn), lambda j, i: (i, j))],
        out_specs=pl.BlockSpec((k, tn), lambda j, i: (0, j)),
        out_shape=jax.ShapeDtypeStruct((k, n), F32),
        compiler_params=_cp(("parallel", "arbitrary")),
    )(a, g)


def _swiglu_fwd(h, wg, wu, tm, tn):
    m, k = h.shape
    n = wg.shape[1]
    tm = min(tm, m)

    def body(h_ref, wg_ref, wu_ref, gate_ref, up_ref, hmid_ref):
        hv = h_ref[...]
        gate = _dot(hv, wg_ref[...])
        up = _dot(hv, wu_ref[...])
        gate_ref[...] = gate.astype(BF16)
        up_ref[...] = up.astype(BF16)
        hmid_ref[...] = (_silu(gate) * up).astype(BF16)

    blk = pl.BlockSpec((tm, tn), lambda i, j: (i, j))
    wspec = pl.BlockSpec((k, tn), lambda i, j: (0, j))
    return pl.pallas_call(
        body, name="swiglu_fwd", grid=(m // tm, n // tn),
        in_specs=[pl.BlockSpec((tm, k), lambda i, j: (i, 0)), wspec, wspec],
        out_specs=[blk, blk, blk],
        out_shape=[jax.ShapeDtypeStruct((m, n), BF16), jax.ShapeDtypeStruct((m, n), BF16),
                   jax.ShapeDtypeStruct((m, n), BF16)],
        compiler_params=_cp(("parallel", "arbitrary")),
    )(h, wg, wu)


def _swiglu_bwd(dffn, wdt, gate, up, tm, tn):
    m, k = dffn.shape
    n = wdt.shape[1]
    tm = min(tm, m)

    def body(d_ref, w_ref, gate_ref, up_ref, dg_ref, du_ref):
        dh = _dot(d_ref[...], w_ref[...])
        gate = gate_ref[...].astype(F32)
        dg_ref[...] = (dh * up_ref[...].astype(F32) * _dsilu(gate)).astype(BF16)
        du_ref[...] = (dh * _silu(gate)).astype(BF16)

    blk = pl.BlockSpec((tm, tn), lambda i, j: (i, j))
    return pl.pallas_call(
        body, name="swiglu_bwd", grid=(m // tm, n // tn),
        in_specs=[pl.BlockSpec((tm, k), lambda i, j: (i, 0)), pl.BlockSpec((k, tn), lambda i, j: (0, j)), blk, blk],
        out_specs=[blk, blk],
        out_shape=[jax.ShapeDtypeStruct((m, n), BF16), jax.ShapeDtypeStruct((m, n), BF16)],
        compiler_params=_cp(("parallel", "arbitrary")),
    )(dffn, wdt, gate, up)


def _halo_specs(tb, width, nrows):
    r8 = tb // 8
    last = nrows // 8 - 1
    prev = pl.BlockSpec((8, width), lambda i: (jnp.maximum(i * r8 - 1, 0), 0))
    nxt = pl.BlockSpec((8, width), lambda i: (jnp.minimum((i + 1) * r8, last), 0))
    return prev, nxt


def _fill_halo(buf, prev_ref, cur_ref, next_ref, tb, i, nb):
    buf[0:8, :] = prev_ref[...] * (i > 0).astype(F32)
    buf[8:8 + tb, :] = cur_ref[...]
    buf[8 + tb:16 + tb, :] = next_ref[...] * (i < nb - 1).astype(F32)


def _conv_fwd(xd, w8, b, tb, name):
    n = xd.shape[0]
    tb = min(tb, n)
    nb = n // tb
    prev, nxt = _halo_specs(tb, D_XBC, n)

    def body(p_ref, c_ref, n_ref, w_ref, b_ref, u_ref, buf):
        i = pl.program_id(0)
        _fill_halo(buf, p_ref, c_ref, n_ref, tb, i, nb)
        acc = jnp.broadcast_to(b_ref[...], (tb, D_XBC))
        for k in range(D_CONV):
            acc = acc + w_ref[k:k + 1, :] * buf[pl.ds(6 + k, tb), :]
        u_ref[...] = acc

    return pl.pallas_call(
        body, name=name, grid=(nb,),
        in_specs=[prev, pl.BlockSpec((tb, D_XBC), lambda i: (i, 0)), nxt,
                  pl.BlockSpec((8, D_XBC), lambda i: (0, 0)), _par_spec(D_XBC)],
        out_specs=_row_spec(tb, D_XBC),
        out_shape=jax.ShapeDtypeStruct((n, D_XBC), F32),
        scratch_shapes=[pltpu.VMEM((tb + 16, D_XBC), F32)],
        compiler_params=_cp(("parallel",)),
    )(xd, xd, xd, w8, b)


def _conv_bwd_a(dxs, dy, dskip_e, dbc, u, tb, name):
    n = u.shape[0]
    tb = min(tb, n)

    def body(dxs_ref, dy_ref, sk_ref, dbc_ref, u_ref, du_ref, acc_ref):
        i = pl.program_id(0)

        @pl.when(i == 0)
        def _():
            acc_ref[...] = jnp.zeros_like(acc_ref)

        uv = u_ref[...]
        ds = _dsilu(uv)
        gx = dxs_ref[0] + dxs_ref[1] + dy_ref[...] * sk_ref[...]
        gbc = dbc_ref[0] + dbc_ref[1]
        du = jnp.concatenate([gx, gbc], axis=1) * ds
        du_ref[...] = du
        acc_ref[0:1, :] += jnp.sum(du, axis=0, keepdims=True)

    return pl.pallas_call(
        body, name=name, grid=(n // tb,),
        in_specs=[pl.BlockSpec((2, tb, D_MODEL), lambda i: (0, i, 0)), _row_spec(tb, D_MODEL), _par_spec(D_MODEL),
                  pl.BlockSpec((2, tb, 2 * D_STATE), lambda i: (0, i, 0)), _row_spec(tb, D_XBC)],
        out_specs=[_row_spec(tb, D_XBC), _acc_spec(D_XBC)],
        out_shape=[jax.ShapeDtypeStruct((n, D_XBC), F32), jax.ShapeDtypeStruct((8, D_XBC), F32)],
        compiler_params=_cp(("arbitrary",)),
    )(dxs, dy, dskip_e, dbc, u)


def _conv_bwd_b(du, xd, ddt, w8, tb, name):
    n = du.shape[0]
    tb = min(tb, n)
    nb = n // tb
    prev, nxt = _halo_specs(tb, D_XBC, n)

    def body(dp_ref, dc_ref, dn_ref, xp_ref, xc_ref, xn_ref, ddt_ref, w_ref, dxd_ref, acc_ref, dbuf, xbuf):
        i = pl.program_id(0)

        @pl.when(i == 0)
        def _():
            acc_ref[...] = jnp.zeros_like(acc_ref)

        _fill_halo(dbuf, dp_ref, dc_ref, dn_ref, tb, i, nb)
        _fill_halo(xbuf, xp_ref, xc_ref, xn_ref, tb, i, nb)
        duc = dc_ref[...]
        acc = jnp.zeros((tb, D_XBC), F32)
        for k in range(D_CONV):
            acc = acc + w_ref[k:k + 1, :] * dbuf[pl.ds(10 - k, tb), :]
            acc_ref[k:k + 1, :] += jnp.sum(duc * xbuf[pl.ds(6 + k, tb), :], axis=0, keepdims=True)
        dxd_ref[:, 0:D_XBC] = acc.astype(BF16)
        ddt = ddt_ref[0] + pltpu.roll(ddt_ref[1], SSD_HEADS, 1)
        dxd_ref[:, D_XBC:D_XD] = ddt.astype(BF16)

    cur = pl.BlockSpec((tb, D_XBC), lambda i: (i, 0))
    return pl.pallas_call(
        body, name=name, grid=(nb,),
        in_specs=[prev, cur, nxt, prev, cur, nxt,
                  pl.BlockSpec((2, tb, 128), lambda i: (0, i, 0)), pl.BlockSpec((8, D_XBC), lambda i: (0, 0))],
        out_specs=[_row_spec(tb, D_XD), _acc_spec(D_XBC)],
        out_shape=[jax.ShapeDtypeStruct((n, D_XD), BF16), jax.ShapeDtypeStruct((8, D_XBC), F32)],
        scratch_shapes=[pltpu.VMEM((tb + 16, D_XBC), F32), pltpu.VMEM((tb + 16, D_XBC), F32)],
        compiler_params=_cp(("arbitrary",)),
    )(du, du, du, xd, xd, xd, ddt, w8)


def _ssd_chunk_index(nc, reverse):
    def idx(d, k):
        kk = (nc - 1 - k) if reverse else k
        return kk + d * (nc - 1 - 2 * kk)
    return idx


def _ssd_prologue(d, u_ref, xd_ref, bias_ref, a_ref, r_ref):
    q = CHUNK
    xbc = _silu(u_ref[...])
    xs = xbc[:, 0:D_MODEL]
    bm = xbc[:, D_MODEL:D_MODEL + D_STATE]
    cm = xbc[:, D_MODEL + D_STATE:D_XBC]
    row = lax.broadcasted_iota(jnp.int32, (q, q), 0)
    col = lax.broadcasted_iota(jnp.int32, (q, q), 1)
    sgn = 1 - 2 * d
    mask = ((row - col) * sgn) >= 0
    mask_t = ((row - col) * sgn) <= 0
    xdv = xd_ref[...]
    dtraw = jnp.where(d == 0, xdv, pltpu.roll(xdv, 128 - SSD_HEADS, 1)) + bias_ref[...]
    head_lane = col < SSD_HEADS
    dt = jnp.where(head_lane, _softplus(dtraw), 0.0)
    a = a_ref[...]
    tri = jnp.where(mask, 1.0, 0.0).astype(BF16)
    acum = _dot_sel_l(tri, dt * a)
    rexp = r_ref[...]
    alast = jnp.where(d == 0, acum[q - 1:q, :], acum[0:1, :])
    e16 = jnp.exp(acum)
    dend16 = jnp.exp(alast - acum)
    wend16 = dend16 * dt
    e = _dot_sel_r(e16, rexp, n=2)
    wend_e = _dot_sel_r(wend16, rexp, n=2)
    elast_e = jnp.where(d == 0, e[q - 1:q, :], e[0:1, :])
    g = _dot_nt(cm.astype(BF16), bm.astype(BF16))
    return dict(xs=xs, bm=bm, cm=cm, mask=mask, mask_t=mask_t, dtraw=dtraw, head_lane=head_lane, dt=dt, a=a,
                acum=acum, acum_t=acum.T, dt_t=dt.T, e16=e16, dend16=dend16, wend16=wend16, e=e, wend_e=wend_e,
                elast_e=elast_e, g=g, col=col, row=row)


def _ssd_head_mats(p, h):
    seg = p["acum"][:, h:h + 1] - p["acum_t"][h:h + 1, :]
    lm = jnp.exp(jnp.where(p["mask"], seg, -jnp.inf))
    gl = p["g"] * lm
    s = gl * p["dt_t"][h:h + 1, :]
    return lm, gl, s


def _ssd_fwd(u, xd, bias2, a2, rexp, h0, name, gather=()):
    n = u.shape[0]
    nc = n // CHUNK
    q = CHUNK
    cidx = _ssd_chunk_index(nc, reverse=False)
    ng = len(gather)

    def body(u_ref, xd_ref, bias_ref, a_ref, r_ref, h0_ref, *rest):
        g_ins, (y_ref, hp_ref, hf_ref), rest = rest[:ng], rest[ng:ng + 3], rest[ng + 3:]
        g_outs, st, sems = rest[:ng], rest[ng], rest[ng + 1:]
        d = pl.program_id(0)
        k = pl.program_id(1)
        if ng:
            g_start, g_finish = _gather_steps(g_ins, g_outs, *sems)
            pl.when((d == 0) & (k == 0))(g_start)

        @pl.when(k == 0)
        def _():
            st[...] = h0_ref[...]

        p = _ssd_prologue(d, u_ref, xd_ref, bias_ref, a_ref, r_ref)
        stv = st[...]
        st_bf = stv.astype(BF16)
        hp_ref[...] = st_bf
        xs = p["xs"]
        lane128 = p["col"]
        y_off = _dot(p["cm"].astype(BF16), st_bf) * p["e"]
        for pb in range(SSD_HEADS // 2):
            _, _, s0 = _ssd_head_mats(p, 2 * pb)
            _, _, s1 = _ssd_head_mats(p, 2 * pb + 1)
            xp = xs[:, pb * 128:(pb + 1) * 128]
            rhs = jnp.concatenate([jnp.where(lane128 < HEAD_DIM, xp, 0.0), jnp.where(lane128 >= HEAD_DIM, xp, 0.0)],
                                  axis=0).astype(BF16)
            lhs = jnp.concatenate([s0, s1], axis=1).astype(BF16)
            y_ref[:, pb * 128:(pb + 1) * 128] = _dot(lhs, rhs) + y_off[:, pb * 128:(pb + 1) * 128]
        xw = (xs * p["wend_e"]).astype(BF16)
        new = stv * p["elast_e"] + _dot(p["bm"].T.astype(BF16), xw)
        st[...] = new
        hf_ref[...] = new
        if ng:
            pl.when((d == 1) & (k == nc - 1))(g_finish)

    nsem = _GATHER_SEMS * ng
    return pl.pallas_call(
        body, name=name, grid=(2, nc),
        in_specs=[pl.BlockSpec((q, D_XBC), lambda d, k: (cidx(d, k), 0)),
                  pl.BlockSpec((q, 128), lambda d, k: (cidx(d, k), D_XBC // 128)),
                  pl.BlockSpec((None, 1, 128), lambda d, k: (d, 0, 0)),
                  pl.BlockSpec((None, 1, 128), lambda d, k: (d, 0, 0)),
                  pl.BlockSpec((128, D_MODEL), lambda d, k: (0, 0)),
                  pl.BlockSpec((None, D_STATE, D_MODEL), lambda d, k: (d, 0, 0))] + [_ANY] * ng,
        out_specs=[pl.BlockSpec((None, q, D_MODEL), lambda d, k: (d, cidx(d, k), 0)),
                   pl.BlockSpec((None, None, D_STATE, D_MODEL), lambda d, k: (d, cidx(d, k), 0, 0)),
                   pl.BlockSpec((None, D_STATE, D_MODEL), lambda d, k: (d, 0, 0))] + [_ANY] * ng,
        out_shape=[jax.ShapeDtypeStruct((2, n, D_MODEL), F32),
                   jax.ShapeDtypeStruct((2, nc, D_STATE, D_MODEL), BF16),
                   jax.ShapeDtypeStruct((2, D_STATE, D_MODEL), F32)] +
                  [jax.ShapeDtypeStruct((4,) + t.shape, t.dtype) for t in gather],
        scratch_shapes=[pltpu.VMEM((D_STATE, D_MODEL), F32)] +
                       ([pltpu.SemaphoreType.DMA((nsem,)), pltpu.SemaphoreType.DMA((nsem,))] if ng else []),
        compiler_params=_cp(("arbitrary", "arbitrary")),
    )(u, xd, bias2, a2, rexp, h0, *gather)


def _ssd_bwd(u, xd, bias2, a2, rexp, rexp_t, dy, hprev, lam0, name, exchange=()):
    n = u.shape[0]
    nc = n // CHUNK
    q = CHUNK
    cidx = _ssd_chunk_index(nc, reverse=True)
    ne = len(exchange)

    def body(u_ref, xd_ref, bias_ref, a_ref, r_ref, rt_ref, dy_ref, hp_ref, lam0_ref, *rest):
        e_ins, (dxs_ref, dbc_ref, ddt_ref, acc_ref, lamo_ref), rest = rest[:ne], rest[ne:ne + 5], rest[ne + 5:]
        e_outs, lam, sems = rest[:ne], rest[ne], rest[ne + 1:]
        d = pl.program_id(0)
        k = pl.program_id(1)
        if ne:
            e_start, e_finish = _exchange_steps(e_ins, e_outs, *sems)
            pl.when((d == 0) & (k == 0))(e_start)

        @pl.when(k == 0)
        def _():
            lam[...] = lam0_ref[...]
            acc_ref[...] = jnp.zeros_like(acc_ref)

        rexp_t = rt_ref[...]

        def hsum(t):
            return _dot_sel_r(t, rexp_t, n=2)

        p = _ssd_prologue(d, u_ref, xd_ref, bias_ref, a_ref, r_ref)
        xs, bm, cm = p["xs"], p["bm"], p["cm"]
        bm_bf, cm_bf = bm.astype(BF16), cm.astype(BF16)
        lamn = lam[...]
        lamn_bf = lamn.astype(BF16)
        stp = hp_ref[...]
        dyv = dy_ref[...]
        lane128 = p["col"]

        wend_e = p["wend_e"]
        cs = _dot(cm_bf, stp)
        dye_bf = (dyv * p["e"]).astype(BF16)
        dc_off = _dot_nt(dye_bf, stp)
        v = _dot(bm_bf, lamn_bf)
        xw_bf = (xs * wend_e).astype(BF16)
        db_off = _dot_nt(xw_bf, lamn_bf)
        elast_e = p["elast_e"]
        dlast_e = jnp.sum(stp.astype(F32) * lamn, axis=0, keepdims=True) * elast_e
        lam_new = lamn * elast_e + _dot(cm.T.astype(BF16), dye_bf)
        lam[...] = lam_new
        lamo_ref[...] = lam_new

        hs_vx = hsum(v * xs)
        om = p["wend16"] * hs_vx
        x1 = p["e16"] * hsum(dyv * cs) - om
        x2 = p["dend16"] * hs_vx
        x3 = jnp.sum(om, axis=0, keepdims=True) + hsum(jnp.broadcast_to(dlast_e, (8, D_MODEL)))[0:1, :]

        sub16 = lax.broadcasted_iota(jnp.int32, (SSD_HEADS, q), 0)
        rs = jnp.zeros((q, 128), F32)
        cs_m = jnp.zeros((SSD_HEADS, q), F32)
        dt_m = jnp.zeros((SSD_HEADS, q), F32)
        dg = jnp.zeros((q, q), F32)
        for pb in range(SSD_HEADS // 2):
            xp_bf = xs[:, pb * 128:(pb + 1) * 128].astype(BF16)
            dyp = dyv[:, pb * 128:(pb + 1) * 128]
            dxs_pair = None
            for half in range(2):
                h = 2 * pb + half
                sel = (lane128 < HEAD_DIM) if half == 0 else (lane128 >= HEAD_DIM)
                dyh_bf = jnp.where(sel, dyp, 0.0).astype(BF16)
                lm, gl, s = _ssd_head_mats(p, h)
                ds = _dot_nt(dyh_bf, xp_bf)
                t = _dot_tn(s.astype(BF16), dyh_bf)
                dxs_pair = t if dxs_pair is None else dxs_pair + t
                w = ds * s
                rs = rs + jnp.sum(w, axis=1, keepdims=True) * (lane128 == h).astype(F32)
                cs_m = jnp.where(sub16 == h, jnp.sum(w, axis=0, keepdims=True), cs_m)
                dt_m = jnp.where(sub16 == h, jnp.sum(ds * gl, axis=0, keepdims=True), dt_m)
                dg = dg + ds * lm * p["dt_t"][h:h + 1, :]
            sl = slice(pb * 128, (pb + 1) * 128)
            dxs_ref[:, sl] = dxs_pair + v[:, sl] * wend_e[:, sl]

        def to_lanes(m16):
            return jnp.concatenate([m16, jnp.zeros((128 - SSD_HEADS, q), F32)], axis=0).T

        last = jnp.where(d == 0, q - 1, 0)
        dacum = rs - to_lanes(cs_m) + x1 + jnp.where(p["row"] == last, x3[0:1, :], 0.0)
        tri_t = jnp.where(p["mask_t"], 1.0, 0.0).astype(BF16)
        ddta = _dot_sel_l(tri_t, dacum)
        dt = p["dt"]
        a = p["a"]
        ddt = to_lanes(dt_m) + x2 + a * ddta
        ddtraw = jnp.where(p["head_lane"], ddt * _sigmoid(p["dtraw"]), 0.0)
        ddt_ref[...] = ddtraw
        acc_ref[0:1, :] += jnp.sum(ddtraw, axis=0, keepdims=True)
        acc_ref[1:2, :] += jnp.sum(dt * ddta, axis=0, keepdims=True) * a

        dg_bf = dg.astype(BF16)
        dbc_ref[:, 0:D_STATE] = _dot_tn(dg_bf, cm_bf) + db_off
        dbc_ref[:, D_STATE:2 * D_STATE] = _dot(dg_bf, bm_bf) + dc_off
        if ne:
            pl.when((d == 1) & (k == nc - 1))(e_finish)

    cblk = lambda d, k: (cidx(d, k), 0)
    return pl.pallas_call(
        body, name=name, grid=(2, nc),
        in_specs=[pl.BlockSpec((q, D_XBC), cblk),
                  pl.BlockSpec((q, 128), lambda d, k: (cidx(d, k), D_XBC // 128)),
                  pl.BlockSpec((None, 1, 128), lambda d, k: (d, 0, 0)),
                  pl.BlockSpec((None, 1, 128), lambda d, k: (d, 0, 0)),
                  pl.BlockSpec((128, D_MODEL), lambda d, k: (0, 0)),
                  pl.BlockSpec((D_MODEL, 128), lambda d, k: (0, 0)),
                  pl.BlockSpec((q, D_MODEL), cblk),
                  pl.BlockSpec((None, None, D_STATE, D_MODEL), lambda d, k: (d, cidx(d, k), 0, 0)),
                  pl.BlockSpec((None, D_STATE, D_MODEL), lambda d, k: (d, 0, 0))] + [_ANY] * ne,
        out_specs=[pl.BlockSpec((None, q, D_MODEL), lambda d, k: (d, cidx(d, k), 0)),
                   pl.BlockSpec((None, q, 2 * D_STATE), lambda d, k: (d, cidx(d, k), 0)),
                   pl.BlockSpec((None, q, 128), lambda d, k: (d, cidx(d, k), 0)),
                   pl.BlockSpec((None, 8, 128), lambda d, k: (d, 0, 0)),
                   pl.BlockSpec((None, D_STATE, D_MODEL), lambda d, k: (d, 0, 0))] + [_ANY] * ne,
        out_shape=[jax.ShapeDtypeStruct((2, n, D_MODEL), F32),
                   jax.ShapeDtypeStruct((2, n, 2 * D_STATE), F32),
                   jax.ShapeDtypeStruct((2, n, 128), F32),
                   jax.ShapeDtypeStruct((2, 8, 128), F32),
                   jax.ShapeDtypeStruct((2, D_STATE, D_MODEL), F32)] +
                  [jax.ShapeDtypeStruct(t.shape, t.dtype) for t in exchange],
        scratch_shapes=[pltpu.VMEM((D_STATE, D_MODEL), F32)] +
                       ([pltpu.SemaphoreType.DMA((3 * ne,)), pltpu.SemaphoreType.DMA((3 * ne,)),
                         pltpu.SemaphoreType.DMA((ne,))] if ne else []),
        compiler_params=_cp(("arbitrary", "arbitrary")),
    )(u, xd, bias2, a2, rexp, rexp_t, dy, hprev, lam0, *exchange)


def _merge_fwd(y, u, z, dskip_e, gn, tb):
    n = z.shape[0]

    def body(y_ref, u_ref, z_ref, sk_ref, gn_ref, o_ref):
        xs = _silu(u_ref[...])
        ys = y_ref[0] + y_ref[1] + sk_ref[...] * xs
        gated = ys * _silu(z_ref[...])
        rstd = lax.rsqrt(jnp.mean(gated * gated, axis=-1, keepdims=True) + LN_EPS)
        o_ref[...] = (gated * rstd * gn_ref[...]).astype(BF16)

    return pl.pallas_call(
        body, name="merge_fwd", grid=(n // tb,),
        in_specs=[pl.BlockSpec((2, tb, D_MODEL), lambda i: (0, i, 0)), pl.BlockSpec((tb, D_MODEL), lambda i: (i, 0)),
                  _row_spec(tb, D_MODEL), _par_spec(D_MODEL), _par_spec(D_MODEL)],
        out_specs=_row_spec(tb, D_MODEL),
        out_shape=jax.ShapeDtypeStruct((n, D_MODEL), BF16),
        compiler_params=_cp(("parallel",)),
    )(y, u, z, dskip_e, gn)


def _merge_bwd(dyn, y, u, z, dskip_e, gn, tb):
    n = z.shape[0]

    def body(dyn_ref, y_ref, u_ref, z_ref, sk_ref, gn_ref, dy_ref, dz_ref, acc_ref):
        i = pl.program_id(0)

        @pl.when(i == 0)
        def _():
            acc_ref[...] = jnp.zeros_like(acc_ref)

        xs = _silu(u_ref[...])
        zv = z_ref[...]
        ys = y_ref[0] + y_ref[1] + sk_ref[...] * xs
        gated = ys * _silu(zv)
        rstd = lax.rsqrt(jnp.mean(gated * gated, axis=-1, keepdims=True) + LN_EPS)
        ghat = gated * rstd
        dyn_v = dyn_ref[...]
        t = dyn_v * gn_ref[...]
        dgated = rstd * (t - ghat * jnp.mean(t * ghat, axis=-1, keepdims=True))
        dys = dgated * _silu(zv)
        dy_ref[...] = dys
        dz_ref[...] = (dgated * ys * _dsilu(zv)).astype(BF16)
        acc_ref[0:1, :] += jnp.sum(dyn_v * ghat, axis=0, keepdims=True)
        acc_ref[1:2, :] += jnp.sum(dys * xs, axis=0, keepdims=True)

    return pl.pallas_call(
        body, name="merge_bwd", grid=(n // tb,),
        in_specs=[_row_spec(tb, D_MODEL), pl.BlockSpec((2, tb, D_MODEL), lambda i: (0, i, 0)),
                  pl.BlockSpec((tb, D_MODEL), lambda i: (i, 0)), _row_spec(tb, D_MODEL),
                  _par_spec(D_MODEL), _par_spec(D_MODEL)],
        out_specs=[_row_spec(tb, D_MODEL), _row_spec(tb, D_MODEL), _acc_spec(D_MODEL)],
        out_shape=[jax.ShapeDtypeStruct((n, D_MODEL), F32), jax.ShapeDtypeStruct((n, D_MODEL), BF16),
                   jax.ShapeDtypeStruct((8, D_MODEL), F32)],
        compiler_params=_cp(("arbitrary",)),
    )(dyn, y, u, z, dskip_e, gn)


def _pool_consts(transpose):
    tb = POOL_TB
    t = jnp.arange(tb)
    s = jnp.arange(3 * tb)
    rl, cl = t // GRID_W, t % GRID_W
    rs_, cs_ = s // GRID_W - tb // GRID_W, s % GRID_W
    s2 = jnp.arange(tb)
    rl2, cl2 = s2 // GRID_W, s2 % GRID_W
    brow, bcol = [], []
    for w in POOL_WINDOWS:
        lo, hi = -(w // 2), w - w // 2
        if transpose:
            lo, hi = -hi + 1, -lo + 1
        dr = rs_[None, :] - rl[:, None]
        brow.append(((cs_[None, :] == cl[:, None]) & (dr >= lo) & (dr < hi)).astype(BF16))
        dc = cl2[None, :] - cl[:, None]
        bcol.append(((rl2[None, :] == rl[:, None]) & (dc >= lo) & (dc < hi)).astype(BF16))
    return jnp.stack(brow), jnp.stack(bcol)


def _pool_inv(i, g, n):
    assert GRID_W == 64
    t = i * POOL_TB + lax.broadcasted_iota(jnp.int32, (POOL_TB, 1), 0)
    r = lax.shift_right_logical(t, 6)
    col = t & (GRID_W - 1)
    w = POOL_WINDOWS[g]
    lo, hi = -(w // 2), w - w // 2
    cnt_r = jnp.minimum(r + hi, n // GRID_W) - jnp.maximum(r + lo, 0)
    cnt_c = jnp.minimum(col + hi, GRID_W) - jnp.maximum(col + lo, 0)
    return 1.0 / (cnt_r * cnt_c).astype(F32)


def _pool_box(prev_ref, cur_ref, next_ref, brow_ref, bcol_ref, g, i, nb):
    sl = slice(g * POOL_DIM, (g + 1) * POOL_DIM)
    pv = prev_ref[:, sl] * (i > 0).astype(prev_ref.dtype)
    nx = next_ref[:, sl] * (i < nb - 1).astype(next_ref.dtype)
    stack = jnp.concatenate([pv.astype(BF16), cur_ref[:, sl].astype(BF16), nx.astype(BF16)], axis=0)
    r = _dot(brow_ref[g], stack)
    return _dot(bcol_ref[g], r.astype(BF16))


def _pool_halo_specs(n, d):
    tb = POOL_TB
    nb = n // tb
    prev = pl.BlockSpec((tb, d), lambda i: (jnp.maximum(i - 1, 0), 0))
    cur = pl.BlockSpec((tb, d), lambda i: (i, 0))
    nxt = pl.BlockSpec((tb, d), lambda i: (jnp.minimum(i + 1, nb - 1), 0))
    return prev, cur, nxt


def _pool_const_specs():
    tb = POOL_TB
    return [pl.BlockSpec((N_POOL, tb, 3 * tb), lambda i: (0, 0, 0)),
            pl.BlockSpec((N_POOL, tb, tb), lambda i: (0, 0, 0))]


def _pool_fwd(up, consts, pw_bf, pscale):
    n = up.shape[0]
    tb = POOL_TB
    nb = n // tb
    brow, bcol = consts
    prev, cur, nxt = _pool_halo_specs(n, D_MODEL)

    def body(p_ref, c_ref, n_ref, brow_ref, bcol_ref, pw_ref, sc_ref, o_ref, d_ref):
        i = pl.program_id(0)
        for g in range(N_POOL):
            sl = slice(g * POOL_DIM, (g + 1) * POOL_DIM)
            box = _pool_box(p_ref, c_ref, n_ref, brow_ref, bcol_ref, g, i, nb)
            dd = (box * _pool_inv(i, g, n) - c_ref[:, sl]).astype(BF16)
            d_ref[:, sl] = dd
            o_ref[:, sl] = (_dot(dd, pw_ref[g]) * sc_ref[:, sl]).astype(BF16)

    return pl.pallas_call(
        body, name="pool_fwd", grid=(nb,),
        in_specs=[prev, cur, nxt] + _pool_const_specs() +
                 [pl.BlockSpec((N_POOL, POOL_DIM, POOL_DIM), lambda i: (0, 0, 0)), _par_spec(D_MODEL)],
        out_specs=[_row_spec(tb, D_MODEL), _row_spec(tb, D_MODEL)],
        out_shape=[jax.ShapeDtypeStruct((n, D_MODEL), BF16), jax.ShapeDtypeStruct((n, D_MODEL), BF16)],
        compiler_params=_cp(("parallel",)),
    )(up, up, up, brow, bcol, pw_bf, pscale)


def _pool_bwd_a(dp, dsave, pw_bf, pwt_bf, pscale):
    n = dp.shape[0]
    tb = POOL_TB

    def body(dp_ref, d_ref, pw_ref, pwt_ref, sc_ref, dd_ref, dds_ref, gw_ref, gs_ref):
        i = pl.program_id(0)

        @pl.when(i == 0)
        def _():
            gw_ref[...] = jnp.zeros_like(gw_ref)
            gs_ref[...] = jnp.zeros_like(gs_ref)

        for g in range(N_POOL):
            sl = slice(g * POOL_DIM, (g + 1) * POOL_DIM)
            dpv = dp_ref[:, sl]
            dv = d_ref[:, sl]
            dpw_bf = (dpv * sc_ref[:, sl]).astype(BF16)
            dd = _dot(dpw_bf, pwt_ref[g])
            dd_ref[:, sl] = dd
            dds_ref[:, sl] = (dd * _pool_inv(i, g, n)).astype(BF16)
            gw_ref[g] += _dot_tn(dv, dpw_bf)
            gs_ref[0:1, sl] += jnp.sum(dpv * _dot(dv, pw_ref[g]), axis=0, keepdims=True)

    wspec = pl.BlockSpec((N_POOL, POOL_DIM, POOL_DIM), lambda i: (0, 0, 0))
    return pl.pallas_call(
        body, name="pool_bwd_a", grid=(n // tb,),
        in_specs=[_row_spec(tb, D_MODEL), _row_spec(tb, D_MODEL), wspec, wspec, _par_spec(D_MODEL)],
        out_specs=[_row_spec(tb, D_MODEL), _row_spec(tb, D_MODEL), wspec, _acc_spec(D_MODEL)],
        out_shape=[jax.ShapeDtypeStruct((n, D_MODEL), F32), jax.ShapeDtypeStruct((n, D_MODEL), BF16),
                   jax.ShapeDtypeStruct((N_POOL, POOL_DIM, POOL_DIM), F32), jax.ShapeDtypeStruct((8, D_MODEL), F32)],
        compiler_params=_cp(("arbitrary",)),
    )(dp, dsave, pw_bf, pwt_bf, pscale)


def _pool_bwd_b(dds, dd, consts_t):
    n = dd.shape[0]
    tb = POOL_TB
    nb = n // tb
    brow, bcol = consts_t
    prev, cur, nxt = _pool_halo_specs(n, D_MODEL)

    def body(p_ref, c_ref, n_ref, brow_ref, bcol_ref, dd_ref, o_ref):
        i = pl.program_id(0)
        for g in range(N_POOL):
            sl = slice(g * POOL_DIM, (g + 1) * POOL_DIM)
            box = _pool_box(p_ref, c_ref, n_ref, brow_ref, bcol_ref, g, i, nb)
            o_ref[:, sl] = (box - dd_ref[:, sl]).astype(BF16)

    return pl.pallas_call(
        body, name="pool_bwd_b", grid=(nb,),
        in_specs=[prev, cur, nxt] + _pool_const_specs() + [_row_spec(tb, D_MODEL)],
        out_specs=_row_spec(tb, D_MODEL),
        out_shape=jax.ShapeDtypeStruct((n, D_MODEL), BF16),
        compiler_params=_cp(("parallel",)),
    )(dds, dds, dds, brow, bcol, dd)


def _pair_add(slabs, recvs, core, name):
    na = len(slabs)
    hr = [t.shape[1] // 4 for t in slabs]

    def body(core_ref, *refs):
        for a in range(na):
            refs[2 * na + a][...] = (refs[a][...] + refs[na + a][...]).astype(BF16)

    own = [pl.BlockSpec((None, hr[a], slabs[a].shape[2]), lambda j, i, c_ref: (j, 2 * c_ref[0] + i, 0)) for a in range(na)]
    got = [pl.BlockSpec((None, hr[a], slabs[a].shape[2]), lambda j, i, c_ref: (j, i, 0)) for a in range(na)]
    return pl.pallas_call(
        body, name=name,
        grid_spec=pltpu.PrefetchScalarGridSpec(num_scalar_prefetch=1, grid=(4, 2), in_specs=own + got, out_specs=got),
        out_shape=[jax.ShapeDtypeStruct(r.shape, BF16) for r in recvs],
        compiler_params=_cp(("arbitrary", "arbitrary")),
    )(core, *slabs, *recvs)


def _sum4(parts, core):
    na = len(parts)
    hr = [t.shape[1] // 2 for t in parts]

    def body(core_ref, *refs):
        for a in range(na):
            p = refs[a]
            refs[na + a][...] = ((p[0].astype(F32) + p[1].astype(F32)) + p[2].astype(F32)) + p[3].astype(F32)

    return pl.pallas_call(
        body, name="reduce_g_sum",
        grid_spec=pltpu.PrefetchScalarGridSpec(
            num_scalar_prefetch=1, grid=(2,),
            in_specs=[pl.BlockSpec((4, hr[a], parts[a].shape[2]), lambda i, c_ref: (0, i, 0)) for a in range(na)],
            out_specs=[pl.BlockSpec((hr[a], parts[a].shape[2]), lambda i, c_ref: (2 * c_ref[0] + i, 0))
                       for a in range(na)]),
        out_shape=[jax.ShapeDtypeStruct((2 * t.shape[1], t.shape[2]), F32) for t in parts],
        compiler_params=_cp(("arbitrary",)),
    )(core, *parts)


def _adamw(w, g, m, v, name):
    r, cdim = w.shape
    tb = _row_block(r, 256)
    c1 = 1.0 - ADAM_B1 ** ADAM_STEP
    c2 = 1.0 - ADAM_B2 ** ADAM_STEP

    def body(w_ref, g_ref, m_ref, v_ref, d_ref, nm_ref, nv_ref):
        gv = g_ref[...]
        nm = ADAM_B1 * m_ref[...] + (1.0 - ADAM_B1) * gv
        nv = ADAM_B2 * v_ref[...] + (1.0 - ADAM_B2) * (gv * gv)
        m_hat = nm / c1
        v_hat = nv / c2
        d_ref[...] = -ADAM_LR * (m_hat / (jnp.sqrt(v_hat) + ADAM_EPS) + ADAM_WD * w_ref[...])
        nm_ref[...] = nm
        nv_ref[...] = nv

    spec = _row_spec(tb, cdim)
    shp = jax.ShapeDtypeStruct((r, cdim), F32)
    return pl.pallas_call(
        body, name=name, grid=(r // tb,),
        in_specs=[spec] * 4, out_specs=[spec] * 3, out_shape=[shp] * 3,
        compiler_params=_cp(("parallel",)),
    )(w, g, m, v)


def _mesh_pos():
    return lax.axis_index("x"), lax.axis_index("y"), lax.axis_index("c")


_ANY = pl.BlockSpec(memory_space=pl.ANY)


def _remote(src, dst, send_sem, recv_sem, device):
    return pltpu.make_async_remote_copy(src_ref=src, dst_ref=dst, send_sem=send_sem, recv_sem=recv_sem,
                                        device_id=device, device_id_type=MESH)


def _other_chips(x, y):
    return [(1 - x, y), (x, 1 - y), (1 - x, 1 - y)]


def _half(nrows, h):
    return pl.ds(h * (nrows // 2), nrows // 2)


_GATHER_SEMS = 7


def _gather_steps(ins, outs, send_sems, recv_sems):
    na = len(ins)
    nrow = [r.shape[0] for r in ins]

    def copies():
        x, y, c = _mesh_pos()
        me = 2 * x + y
        sib = (x, y, 1 - c)
        chips = _other_chips(x, y)

        def ici(k, a, slot):
            px, py = chips[k]
            rows = _half(nrow[a], c)
            return _remote(ins[a].at[rows, :], outs[a].at[slot, rows, :], send_sems.at[k * na + a],
                           recv_sems.at[k * na + a], (px, py, c))

        def fwd(k, a, h):
            px, py = chips[k]
            blk = outs[a].at[2 * px + py, _half(nrow[a], h), :]
            return _remote(blk, blk, send_sems.at[(3 + k) * na + a], recv_sems.at[(3 + k) * na + a], sib)

        def own(a):
            return _remote(ins[a], outs[a].at[me], send_sems.at[6 * na + a], recv_sems.at[6 * na + a], sib)

        slots = [2 * px + py for px, py in chips]
        return ici, fwd, own, me, c, slots

    def start():
        ici, _, own, me, _, _ = copies()
        for a in range(na):
            own(a).start()
        for k in range(3):
            for a in range(na):
                ici(k, a, me).start()

    def finish():
        ici, fwd, own, me, c, slots = copies()
        for k in range(3):
            for a in range(na):
                ici(k, a, slots[k]).wait_recv()
                fwd(k, a, c).start()
        for k in range(3):
            for a in range(na):
                fwd(k, a, 1 - c).wait_recv()
        for a in range(na):
            own(a).wait_recv()
        for a in range(na):
            own(a).wait_send()
        for k in range(3):
            for a in range(na):
                ici(k, a, me).wait_send()
                fwd(k, a, c).wait_send()

    return start, finish


def _exchange_steps(ins, outs, send_sems, recv_sems, local_sems):
    na = len(ins)

    def copies():
        x, y, c = _mesh_pos()
        me = 2 * x + y
        chips = _other_chips(x, y)

        def copy(k, a, slot):
            px, py = chips[k]
            return _remote(ins[a].at[2 * px + py], outs[a].at[slot], send_sems.at[k * na + a], recv_sems.at[k * na + a],
                           (px, py, c))

        def local(a):
            return pltpu.make_async_copy(ins[a].at[me], outs[a].at[me], local_sems.at[a])

        return copy, local, me, [2 * px + py for px, py in chips]

    def start():
        copy, local, me, _ = copies()
        for a in range(na):
            local(a).start()
        for k in range(3):
            for a in range(na):
                copy(k, a, me).start()

    def finish():
        copy, local, me, slots = copies()
        for k in range(3):
            for a in range(na):
                copy(k, a, slots[k]).wait_recv()
        for k in range(3):
            for a in range(na):
                copy(k, a, me).wait_send()
        for a in range(na):
            local(a).wait()

    return start, finish


def _gather_weights(shards, conv8):
    na = len(shards)

    def body(*refs):
        ins, conv_in = refs[:na], refs[na]
        outs, conv_out = refs[na + 1:2 * na + 1], refs[2 * na + 1]
        send_sems, recv_sems, local_sems = refs[2 * na + 2:]
        x, y, c = _mesh_pos()
        me = 2 * x + y
        chips = _other_chips(x, y)

        def conv(k, slot):
            px, py = chips[k]
            return _remote(conv_in, conv_out.at[slot], send_sems.at[7 * na + k], recv_sems.at[7 * na + k], (px, py, c))

        start, finish = _gather_steps(ins, outs, send_sems, recv_sems)
        local = pltpu.make_async_copy(conv_in, conv_out.at[me], local_sems.at[0])
        local.start()
        start()
        sends = [conv(k, me) for k in range(3)]
        for cp in sends:
            cp.start()
        finish()
        for k in range(3):
            px, py = chips[k]
            conv(k, 2 * px + py).wait_recv()
        for cp in sends:
            cp.wait_send()
        local.wait()

    nsem = _GATHER_SEMS * na + 3
    return pl.pallas_call(
        body, name="gather_w", in_specs=[_ANY] * (na + 1), out_specs=[_ANY] * (na + 1),
        out_shape=[jax.ShapeDtypeStruct((4,) + t.shape, t.dtype) for t in shards] +
                  [jax.ShapeDtypeStruct((4,) + conv8.shape, conv8.dtype)],
        scratch_shapes=[pltpu.SemaphoreType.DMA((nsem,)), pltpu.SemaphoreType.DMA((nsem,)),
                        pltpu.SemaphoreType.DMA((1,))],
    )(*shards, conv8)


def _pair_swap(slabs, name):
    na = len(slabs)

    def body(*refs):
        ins, outs = refs[:na], refs[na:2 * na]
        send_sems, recv_sems = refs[2 * na:]
        x, y, c = _mesh_pos()
        cps = [_remote(ins[a].at[:, _half(slabs[a].shape[1], 1 - c), :], outs[a], send_sems.at[a], recv_sems.at[a],
                       (x, y, 1 - c)) for a in range(na)]
        for cp in cps:
            cp.start()
        for cp in cps:
            cp.wait()

    return pl.pallas_call(
        body, name=name, in_specs=[_ANY] * na, out_specs=[_ANY] * na,
        out_shape=[jax.ShapeDtypeStruct((4, t.shape[1] // 2, t.shape[2]), t.dtype) for t in slabs],
        scratch_shapes=[pltpu.SemaphoreType.DMA((na,)), pltpu.SemaphoreType.DMA((na,))],
    )(*slabs)


def _chip_exchange(pairs):
    na = len(pairs)

    def body(*refs):
        start, finish = _exchange_steps(refs[:na], refs[na:2 * na], *refs[2 * na:])
        start()
        finish()

    return pl.pallas_call(
        body, name="reduce_g_ici", in_specs=[_ANY] * na, out_specs=[_ANY] * na,
        out_shape=[jax.ShapeDtypeStruct(t.shape, t.dtype) for t in pairs],
        scratch_shapes=[pltpu.SemaphoreType.DMA((3 * na,)), pltpu.SemaphoreType.DMA((3 * na,)),
                        pltpu.SemaphoreType.DMA((na,))],
    )(*pairs)


def _share_halves(totals):
    na = len(totals)

    def body(*refs):
        bufs = refs[na:2 * na]
        send_sems, recv_sems = refs[2 * na:]
        x, y, c = _mesh_pos()

        def copy(a, h):
            blk = bufs[a].at[_half(totals[a].shape[0], h), :]
            return _remote(blk, blk, send_sems.at[a], recv_sems.at[a], (x, y, 1 - c))

        sends = [copy(a, c) for a in range(na)]
        for cp in sends:
            cp.start()
        for a in range(na):
            copy(a, 1 - c).wait_recv()
        for cp in sends:
            cp.wait_send()

    return pl.pallas_call(
        body, name="reduce_g_share", in_specs=[_ANY] * na, out_specs=[_ANY] * na,
        out_shape=[jax.ShapeDtypeStruct(t.shape, t.dtype) for t in totals],
        input_output_aliases={a: a for a in range(na)},
        scratch_shapes=[pltpu.SemaphoreType.DMA((na,)), pltpu.SemaphoreType.DMA((na,))],
    )(*totals)


def _allreduce_small(v, name):
    r, cdim = v.shape

    def body(v_ref, out_ref, buf, send_sems, recv_sems):
        x, y, c = _mesh_pos()
        me = 4 * x + 2 * y + c
        buf[me] = v_ref[...]
        rel = [(bx, by, bc) for bx in (0, 1) for by in (0, 1) for bc in (0, 1)][1:]

        def peer(b):
            bx, by, bc = b
            return ((1 - x) if bx else x, (1 - y) if by else y, (1 - c) if bc else c)

        def copy(k, slot):
            return pltpu.make_async_remote_copy(
                src_ref=v_ref, dst_ref=buf.at[slot], send_sem=send_sems.at[k], recv_sem=recv_sems.at[k],
                device_id=peer(rel[k]), device_id_type=MESH)

        sends = [copy(k, me) for k in range(7)]
        for cp in sends:
            cp.start()
        for k in range(7):
            px, py, pc = peer(rel[k])
            copy(k, 4 * px + 2 * py + pc).wait_recv()
        for cp in sends:
            cp.wait_send()
        acc = buf[0]
        for j in range(1, 8):
            acc = acc + buf[j]
        out_ref[...] = acc

    vm = pl.BlockSpec(memory_space=pltpu.VMEM)
    return pl.pallas_call(
        body, name=name, in_specs=[vm], out_specs=vm,
        out_shape=jax.ShapeDtypeStruct((r, cdim), F32),
        scratch_shapes=[pltpu.VMEM((8, r, cdim), F32), pltpu.SemaphoreType.DMA((7,)), pltpu.SemaphoreType.DMA((7,))],
    )(v)


_BIG = (("in_proj", (D_MODEL, D_IN_PROJ // 4), 1), ("w_out", (2 * D_MODEL // 4, D_MODEL), 0),
        ("w_gate", (D_MODEL, D_FF // 4), 1), ("w_up", (D_MODEL, D_FF // 4), 1), ("w_down", (D_FF // 4, D_MODEL), 0),
        ("pool_w", (N_POOL * POOL_DIM // 4, POOL_DIM), None), ("w_ada", (D_MODEL, 6 * D_MODEL // 4), 1))


def _assemble(name, t):
    _, r, c = t.shape
    axis = {n: ax for n, _, ax in _BIG}[name]
    if axis == 0:
        return t.reshape(4 * r, c)
    if axis == 1:
        return t.transpose(1, 0, 2).reshape(r, 4 * c)
    return t.reshape(4, N_POOL, POOL_DIM // 4, POOL_DIM).transpose(1, 0, 2, 3).reshape(N_POOL, POOL_DIM, POOL_DIM)


def _to_slabs(name, g):
    (r, c), axis = {n: (sh, ax) for n, sh, ax in _BIG}[name]
    if axis == 0:
        return g.reshape(4, r, c)
    if axis == 1:
        return g.reshape(r, 4, c).transpose(1, 0, 2)
    return g.reshape(N_POOL, 4, POOL_DIM // 4, POOL_DIM).transpose(1, 0, 2, 3).reshape(4, r, c)


_EARLY = ("in_proj", "w_ada")
_LATE = tuple(n for n, _, _ in _BIG if n not in _EARLY)


def _reduce_grads(early_grads, late_parts, core):
    slabs = [_to_slabs(n, early_grads[n]) for n in _EARLY]
    pairs = _pair_add(slabs, _pair_swap(slabs, "reduce_g_d2d"), core, "reduce_g_pair")
    parts = dict(zip(_EARLY, _chip_exchange(pairs)), **dict(zip(_LATE, late_parts)))
    names = [n for n, _, _ in _BIG]
    totals = _sum4([parts[n] for n in names], core)
    return dict(zip(names, _share_halves(totals)))


def _pad_cols(w, n):
    return jnp.concatenate([w, jnp.zeros((w.shape[0], n - w.shape[1]), w.dtype)], axis=1)


def _device_step(x, c, ctx, target, wts, w8, small, tb, late_shards=None, core=None):
    n = x.shape[0]
    d = D_MODEL
    c_ctx = small["c_ctx"]

    win = wts["in_proj"]
    wz, wxd, wup = win[:, 0:d], _pad_cols(win[:, d:d + D_XBC + 2 * SSD_HEADS], D_XD), win[:, d + D_XBC + 2 * SSD_HEADS:]
    wada = wts["w_ada"]

    emb_g, emb_b = _vec(small["emb_ln_g"]), _vec(small["emb_ln_b"])
    ln1_g, ln1_b = _vec(small["ln1_g"]), _vec(small["ln1_b"])
    ln2_g, ln2_b = _vec(small["ln2_g"]), _vec(small["ln2_b"])
    gn = _vec(small["ssd_norm_g"])
    pscale = _vec(small["pool_scale"])
    conv_b = _vec(small["conv_b"])
    dskip_e = jnp.repeat(small["d_skip"].reshape(-1), HEAD_DIM).reshape(1, d)
    zpad = jnp.zeros((2, 1, 128 - SSD_HEADS), F32)
    bias2 = jnp.concatenate([small["dt_bias"].reshape(2, 1, SSD_HEADS), zpad], axis=2)
    a2 = jnp.concatenate([-jnp.exp(small["a_log"].reshape(2, 1, SSD_HEADS)), zpad], axis=2)
    rexp = (jnp.arange(128)[:, None] == (jnp.arange(d)[None, :] // HEAD_DIM)).astype(BF16)
    rexp_t = rexp.T

    c8 = jnp.concatenate([c.reshape(1, d), c_ctx.reshape(1, d), jnp.zeros((6, d), F32)], axis=0)
    mods = _mods_fwd(c8, wada, _vec(small["b_ada"]))
    sh1, sc1, g1, sh2, sc2, g2 = [mods[0:1, i * d:(i + 1) * d] for i in range(6)]
    sh1c, sc1c = mods[1:2, 0:d], mods[1:2, d:2 * d]

    tbc = min(tb, ctx.shape[0])
    xc0, hc = _ln_mod(ctx, emb_g, emb_b, sh1c, sc1c, tbc, "ln_mod_ctx")
    xdc = _matmul_nn([(hc, wxd)], F32, 512, D_XD, "in_proj_ctx")
    uc = _conv_fwd(xdc, w8, conv_b, tbc, "conv_fwd_ctx")
    hzero = jnp.zeros((2, D_STATE, d), F32)
    _, hprev_c, hfin_c = _ssd_fwd(uc, xdc, bias2, a2, rexp, hzero, "ssd_fwd_ctx")

    x0, h1 = _ln_mod(x, emb_g, emb_b, sh1, sc1, tb, "ln_mod")
    z = _matmul_nn([(h1, wz)], F32, MM_ROWS, 1024, "in_proj_z")
    xd = _matmul_nn([(h1, wxd)], F32, MM_ROWS, D_XD, "in_proj_xd")
    up = _matmul_nn([(h1, wup)], F32, MM_ROWS, 1024, "in_proj_up")
    u = _conv_fwd(xd, w8, conv_b, tb, "conv_fwd")
    y, hprev, _, *landed = _ssd_fwd(u, xd, bias2, a2, rexp, hfin_c, "ssd_fwd", gather=late_shards or ())
    if late_shards is not None:
        wts = dict(wts, **{nme: _assemble(nme, t) for nme, t in zip(_LATE, landed)})
    wout = wts["w_out"]
    wg, wu, wd = wts["w_gate"], wts["w_up"], wts["w_down"]
    pw = wts["pool_w"]
    yn = _merge_fwd(y, u, z, dskip_e, gn, tb)
    pconst = _pool_consts(False)
    pool, dsave = _pool_fwd(up, pconst, pw, pscale)
    mix = _matmul_nn([(yn, wout[0:d]), (pool, wout[d:2 * d])], F32, MM_ROWS, 1024, "out_proj")
    x1, h2 = _res_ln(x0, mix, g1, ln1_g, ln1_b, sh2, sc2, tb)

    gate, upp, hmid = _swiglu_fwd(h2, wg, wu, 512, D_FF // 2)
    ffn = _matmul_nn([(hmid, wd)], F32, MM_ROWS, 1024, "ffn_down")
    dffn, dr2, acc2 = _final_ln_loss(x1, ffn, g2, ln2_g, ln2_b, target, tb)
    loss = (0.5 / d) * jnp.sum(acc2[3])

    dgate, dupp = _swiglu_bwd(dffn, wd.T, gate, upp, 512, D_FF // 2)
    g_wdown = _matmul_tn(hmid, dffn, MM_ROWS, 1024, "g_w_down")
    g_wgate = _matmul_tn(h2, dgate, MM_ROWS, 1408, "g_w_gate")
    g_wup = _matmul_tn(h2, dupp, MM_ROWS, 1408, "g_w_up")
    dh2 = _matmul_nn([(dgate, wg.T), (dupp, wu.T)], F32, 512, 1024, "d_h2")
    dmix, dr1, acc1 = _bwd_ln1(dr2, dh2, x1, x0, mix, g1, sc2, ln1_g, tb)

    dyn = _matmul_nn([(dmix, wout[0:d].T)], F32, MM_ROWS, 1024, "d_yn")
    dpool = _matmul_nn([(dmix, wout[d:2 * d].T)], F32, MM_ROWS, 1024, "d_pool")
    g_wout = jnp.concatenate([_matmul_tn(yn, dmix, MM_ROWS, 1024, "g_w_out_a"),
                              _matmul_tn(pool, dmix, MM_ROWS, 1024, "g_w_out_b")], axis=0)
    dd, dds, g_pw, accp = _pool_bwd_a(dpool, dsave, pw, jnp.swapaxes(pw, 1, 2), pscale)
    dup = _pool_bwd_b(dds, dd, _pool_consts(True))
    dy, dz, accm = _merge_bwd(dyn, y, u, z, dskip_e, gn, tb)
    lam0 = jnp.zeros((2, D_STATE, d), F32)
    late_grads = dict(w_out=g_wout, w_gate=g_wgate, w_up=g_wup, w_down=g_wdown, pool_w=g_pw)
    pairs = ()
    if late_shards is not None:
        slabs = [_to_slabs(nme, late_grads[nme]) for nme in _LATE]
        pairs = _pair_add(slabs, _pair_swap(slabs, "reduce_g_d2d_late"), core, "reduce_g_pair_late")
    dxs, dbc, ddt, accs, lam_c, *arrived = _ssd_bwd(u, xd, bias2, a2, rexp, rexp_t, dy, hprev, lam0, "ssd_bwd",
                                                    exchange=pairs)
    du, accb = _conv_bwd_a(dxs, dy, dskip_e, dbc, u, tb, "conv_bwd_a")
    dxd, accw = _conv_bwd_b(du, xd, ddt, w8, tb, "conv_bwd_b")

    lc = ctx.shape[0]
    zeros_c = jnp.zeros((lc, d), F32)
    dxs_c, dbc_c, ddt_c, accs_c, _ = _ssd_bwd(uc, xdc, bias2, a2, rexp, rexp_t, zeros_c, hprev_c, lam_c, "ssd_bwd_ctx")
    du_c, accb_c = _conv_bwd_a(dxs_c, zeros_c, dskip_e, dbc_c, uc, tbc, "conv_bwd_a_ctx")
    dxd_c, accw_c = _conv_bwd_b(du_c, xdc, ddt_c, w8, tbc, "conv_bwd_b_ctx")
    dhc = _matmul_nn([(dxd_c, wxd.T)], F32, 512, 1024, "d_hc")
    _, acc0c = _bwd_ln0(None, dhc, ctx, emb_g, emb_b, sc1c, tbc, "bwd_ln0_ctx")

    dh1 = _matmul_nn([(dz, wz.T), (dxd, wxd.T), (dup, wup.T)], F32, MM_ROWS, 1024, "d_h1")
    g_wz = _matmul_tn(h1, dz, MM_ROWS, 1024, "g_in_proj_z")
    g_wxd = _matmul_tn(h1, dxd, MM_ROWS, D_XD, "g_in_proj_xd") + _matmul_tn(hc, dxd_c, 512, D_XD, "g_in_proj_xd_ctx")
    g_wpo = _matmul_tn(h1, dup, MM_ROWS, 1024, "g_in_proj_up")
    g_win = jnp.concatenate([g_wz, g_wxd[:, 0:D_XBC + 2 * SSD_HEADS], g_wpo], axis=1)
    grad_x, acc0 = _bwd_ln0(dr1, dh1, x, emb_g, emb_b, sc1, tb, "bwd_ln0")

    zero_d = jnp.zeros((1, d), F32)
    dmod = jnp.concatenate([acc0[1:2], acc0[0:1], acc1[4:5], acc1[1:2], acc1[0:1], acc2[2:3]], axis=1)
    dmodc = jnp.concatenate([acc0c[1:2], acc0c[0:1]] + [zero_d] * 4, axis=1)
    dm8 = jnp.concatenate([dmod, dmodc, jnp.zeros((6, 6 * d), F32)], axis=0)
    g_wada, g_bada8 = _mods_bwd_w(c8.T, dm8)
    g_cctx8 = _mods_bwd_c(dm8, wada, c8)

    big = dict(in_proj=g_win, w_ada=g_wada)
    if late_shards is None:
        big.update(late_grads)
    sml = dict(
        c_ctx=g_cctx8[1], emb_ln_g=acc0[2] + acc0c[2], emb_ln_b=acc0[3] + acc0c[3], b_ada=g_bada8[0],
        conv_w=accw[0:D_CONV] + accw_c[0:D_CONV], conv_b=accb[0] + accb_c[0],
        dt_bias=accs[:, 0, 0:SSD_HEADS] + accs_c[:, 0, 0:SSD_HEADS],
        a_log=accs[:, 1, 0:SSD_HEADS] + accs_c[:, 1, 0:SSD_HEADS],
        d_skip=jnp.sum(accm[1].reshape(SSD_HEADS, HEAD_DIM), axis=1),
        ssd_norm_g=accm[0], pool_scale=accp[0], ln1_g=acc1[2], ln1_b=acc1[3], ln2_g=acc2[0], ln2_b=acc2[1])
    return loss, grad_x, big, sml, (arrived if late_shards is not None else None)


_SMALL = ("c_ctx", "emb_ln_g", "emb_ln_b", "b_ada", "conv_w", "conv_b", "dt_bias", "a_log", "d_skip",
          "ssd_norm_g", "pool_scale", "ln1_g", "ln1_b", "ln2_g", "ln2_b")


_SMALL_ROWS = 8


def _pack_small(vals, names):
    rows = []
    for nme in names:
        flat = vals[nme].reshape(-1).astype(F32)
        assert flat.shape[0] <= _SMALL_ROWS * 1024
        rows.append(jnp.concatenate([flat, jnp.zeros((_SMALL_ROWS * 1024 - flat.shape[0],), F32)]).reshape(_SMALL_ROWS, 1024))
    return jnp.concatenate(rows, axis=0)


def _unpack_small(packed, shapes, names):
    out = {}
    for i, nme in enumerate(names):
        size = math.prod(shapes[nme])
        out[nme] = packed[i * _SMALL_ROWS:(i + 1) * _SMALL_ROWS].reshape(-1)[:size].reshape(shapes[nme])
    return out


_WEIGHT_ORDER = ("c_ctx", "emb_ln_g", "emb_ln_b", "w_ada", "b_ada", "in_proj", "conv_w", "conv_b", "dt_bias", "a_log",
                 "d_skip", "ssd_norm_g", "pool_w", "pool_scale", "w_out", "ln1_g", "ln1_b", "w_gate", "w_up", "w_down",
                 "ln2_g", "ln2_b")


def _as2d(a):
    return a.reshape(-1, a.shape[-1])


def kernel(x, c, ctx, c_ctx, emb_ln_g, emb_ln_b, w_ada, b_ada, in_proj, conv_w, conv_b, dt_bias, a_log, d_skip, ssd_norm_g, pool_w, pool_scale, w_out, ln1_g, ln1_b, w_gate, w_up, w_down, ln2_g, ln2_b, loss_target, m_c_ctx, m_emb_ln_g, m_emb_ln_b, m_w_ada, m_b_ada, m_in_proj, m_conv_w, m_conv_b, m_dt_bias, m_a_log, m_d_skip, m_ssd_norm_g, m_pool_w, m_pool_scale, m_w_out, m_ln1_g, m_ln1_b, m_w_gate, m_w_up, m_w_down, m_ln2_g, m_ln2_b, v_c_ctx, v_emb_ln_g, v_emb_ln_b, v_w_ada, v_b_ada, v_in_proj, v_conv_w, v_conv_b, v_dt_bias, v_a_log, v_d_skip, v_ssd_norm_g, v_pool_w, v_pool_scale, v_w_out, v_ln1_g, v_ln1_b, v_w_gate, v_w_up, v_w_down, v_ln2_g, v_ln2_b):
    w = dict(c_ctx=c_ctx, emb_ln_g=emb_ln_g, emb_ln_b=emb_ln_b, w_ada=w_ada, b_ada=b_ada, in_proj=in_proj, conv_w=conv_w,
             conv_b=conv_b, dt_bias=dt_bias, a_log=a_log, d_skip=d_skip, ssd_norm_g=ssd_norm_g, pool_w=pool_w,
             pool_scale=pool_scale, w_out=w_out, ln1_g=ln1_g, ln1_b=ln1_b, w_gate=w_gate, w_up=w_up, w_down=w_down,
             ln2_g=ln2_g, ln2_b=ln2_b)
    m = dict(c_ctx=m_c_ctx, emb_ln_g=m_emb_ln_g, emb_ln_b=m_emb_ln_b, w_ada=m_w_ada, b_ada=m_b_ada, in_proj=m_in_proj,
             conv_w=m_conv_w, conv_b=m_conv_b, dt_bias=m_dt_bias, a_log=m_a_log, d_skip=m_d_skip,
             ssd_norm_g=m_ssd_norm_g, pool_w=m_pool_w, pool_scale=m_pool_scale, w_out=m_w_out, ln1_g=m_ln1_g,
             ln1_b=m_ln1_b, w_gate=m_w_gate, w_up=m_w_up, w_down=m_w_down, ln2_g=m_ln2_g, ln2_b=m_ln2_b)
    v = dict(c_ctx=v_c_ctx, emb_ln_g=v_emb_ln_g, emb_ln_b=v_emb_ln_b, w_ada=v_w_ada, b_ada=v_b_ada, in_proj=v_in_proj,
             conv_w=v_conv_w, conv_b=v_conv_b, dt_bias=v_dt_bias, a_log=v_a_log, d_skip=v_d_skip,
             ssd_norm_g=v_ssd_norm_g, pool_w=v_pool_w, pool_scale=v_pool_scale, w_out=v_w_out, ln1_g=v_ln1_g,
             ln1_b=v_ln1_b, w_gate=v_w_gate, w_up=v_w_up, w_down=v_w_down, ln2_g=v_ln2_g, ln2_b=v_ln2_b)

    xi, yi, ci = _mesh_pos()
    chip = 2 * xi + yi

    core = ci.reshape(1).astype(jnp.int32)
    shard = {name: w[name][0].astype(BF16).reshape(shp) for name, shp, _ in _BIG}
    conv8 = jnp.concatenate([conv_w[0], jnp.zeros((8 - D_CONV, conv_w.shape[-1]), F32)], axis=0)
    *gathered, conv4 = _gather_weights([shard[nme] for nme in _EARLY], conv8)
    wts = {nme: _assemble(nme, t) for nme, t in zip(_EARLY, gathered)}
    w8 = conv4.transpose(1, 0, 2).reshape(8, D_XBC)
    small = {nme: (w[nme] if nme in ("c_ctx", "emb_ln_g", "emb_ln_b") else w[nme][0]) for nme in _SMALL if nme != "conv_w"}

    loss, grad_x, big, sml, late_parts = _device_step(x[0], c, ctx[0], loss_target[0], wts, w8, small, 512,
                                                      late_shards=[shard[nme] for nme in _LATE], core=core)
    loss = lax.psum(loss, ("x", "y", "c"))

    g_big = _reduce_grads(big, late_parts, core)
    small_shapes = {nme: sml[nme].shape for nme in _SMALL}
    g_small = _unpack_small(_allreduce_small(_pack_small(sml, _SMALL), "reduce_small"), small_shapes, _SMALL)
    cw_cols = conv_w.shape[-1]
    g_small["conv_w"] = lax.dynamic_slice_in_dim(g_small["conv_w"], chip * cw_cols, cw_cols, axis=1)

    grads, delta, new_m, new_v = {}, {}, {}, {}
    for name, _, _ in _BIG:
        g2 = _as2d(g_big[name])
        d2, m2, v2 = _adamw(_as2d(w[name][0]), g2, _as2d(m[name][0]), _as2d(v[name][0]), "adamw_" + name)
        grads[name] = g2.reshape(w[name].shape)
        delta[name], new_m[name], new_v[name] = (t.reshape(w[name].shape) for t in (d2, m2, v2))
    shp = {nme: w[nme].shape for nme in _SMALL}
    gp = _pack_small(g_small, _SMALL)
    dp, mp, vp = _adamw(_pack_small(w, _SMALL), gp, _pack_small(m, _SMALL), _pack_small(v, _SMALL), "adamw_small")
    for dst, src in ((grads, gp), (delta, dp), (new_m, mp), (new_v, vp)):
        dst.update(_unpack_small(src, shp, _SMALL))

    return (loss, grad_x[None], *[grads[nme] for nme in _WEIGHT_ORDER], *[delta[nme] for nme in _WEIGHT_ORDER],
            *[new_m[nme] for nme in _WEIGHT_ORDER], *[new_v[nme] for nme in _WEIGHT_ORDER])
```

```python
import functools
import math

import jax
import jax.numpy as jnp
from jax import lax
from jax.experimental import pallas as pl
from jax.experimental.pallas import tpu as pltpu

F32 = jnp.float32
BF16 = jnp.bfloat16
MESH = pl.DeviceIdType.MESH

D_MODEL = 1024
SSD_HEADS = 16
HEAD_DIM = 64
D_STATE = 128
CHUNK = 128
D_CONV = 5
D_XBC = D_MODEL + 2 * D_STATE
D_XD = 1408
N_POOL = 4
POOL_DIM = 256
POOL_WINDOWS = (2, 4, 8, 16)
GRID_W = 64
D_FF = 2816
D_IN_PROJ = 3360
LN_EPS = 1e-5
ALPHA = 2.0 ** 0.25
POOL_TB = 512
MM_ROWS = 1024

ADAM_LR = 0.001
ADAM_B1 = 0.9
ADAM_B2 = 0.999
ADAM_EPS = 1e-08
ADAM_WD = 0.01
ADAM_STEP = 10

VMEM_LIMIT = 56 * 1024 * 1024


def _cp(sem=None):
    return pltpu.CompilerParams(dimension_semantics=sem, vmem_limit_bytes=VMEM_LIMIT)


def _sigmoid(x):
    return 1.0 / (1.0 + jnp.exp(-x))


def _silu(x):
    return x * _sigmoid(x)


def _dsilu(x):
    s = _sigmoid(x)
    return s * (1.0 + x * (1.0 - s))


def _softplus(x):
    t = jnp.exp(-jnp.abs(x))
    u = 1.0 + t
    log1p = jnp.where(u == 1.0, t, jnp.log(u) * t / (u - 1.0 + (u == 1.0)))
    return jnp.maximum(x, 0.0) + log1p


def _split(x, n):
    parts, r = [], x
    for _ in range(n):
        p = r.astype(BF16)
        parts.append(p)
        r = r - p.astype(F32)
    return parts


def _dot(a, b):
    return jnp.dot(a, b, preferred_element_type=F32)


def _dot_nt(a, b):
    return lax.dot_general(a, b, (((1,), (1,)), ((), ())), preferred_element_type=F32)


def _dot_tn(a, b):
    return lax.dot_general(a, b, (((0,), (0,)), ((), ())), preferred_element_type=F32)


def _dot_sel_l(sel_bf, x, n=3):
    out = None
    for p in _split(x, n):
        t = _dot(sel_bf, p)
        out = t if out is None else out + t
    return out


def _dot_sel_r(x, sel_bf, n=3):
    out = None
    for p in _split(x, n):
        t = _dot(p, sel_bf)
        out = t if out is None else out + t
    return out


def _row_block(n, cap=256, mult=8):
    best = None
    for t in range(mult, min(n, cap) + 1, mult):
        if n % t == 0:
            best = t
    return best if best is not None else n


def _vec(v):
    return v.reshape(1, -1).astype(F32)


MOD_ROWS = 16
MOD_TN = 512


def _mods_fwd(c16, w_bf, b):
    r, d = c16.shape
    n = w_bf.shape[1]

    def body(c_ref, w_ref, b_ref, o_ref):
        s = _silu(c_ref[...]).astype(BF16)
        o_ref[...] = _dot(s, w_ref[...]) + b_ref[...]

    return pl.pallas_call(
        body, name="mods_fwd", grid=(n // MOD_TN,),
        in_specs=[pl.BlockSpec((r, d), lambda j: (0, 0)),
                  pl.BlockSpec((d, MOD_TN), lambda j: (0, j)),
                  pl.BlockSpec((1, MOD_TN), lambda j: (0, j))],
        out_specs=pl.BlockSpec((r, MOD_TN), lambda j: (0, j)),
        out_shape=jax.ShapeDtypeStruct((r, n), F32),
        compiler_params=_cp(("arbitrary",)),
    )(c16, w_bf, b)


def _mods_bwd_w(ct16, dm16):
    d = ct16.shape[0]
    n = dm16.shape[1]

    def body(ct_ref, dm_ref, dw_ref):
        s = _silu(ct_ref[...])
        dm = dm_ref[...]
        acc = s[:, 0:1] * dm[0:1, :]
        for r in range(1, 9):
            acc = acc + s[:, r:r + 1] * dm[r:r + 1, :]
        dw_ref[...] = acc

    return pl.pallas_call(
        body, name="mods_bwd_w", grid=(n // MOD_TN,),
        in_specs=[pl.BlockSpec((d, MOD_ROWS), lambda j: (0, 0)),
                  pl.BlockSpec((MOD_ROWS, MOD_TN), lambda j: (0, j))],
        out_specs=pl.BlockSpec((d, MOD_TN), lambda j: (0, j)),
        out_shape=jax.ShapeDtypeStruct((d, n), F32),
        compiler_params=_cp(("arbitrary",)),
    )(ct16, dm16)


def _mods_bwd_c(dm16, w_bf, c16):
    d = c16.shape[1]
    n = dm16.shape[1]
    nk = n // MOD_TN

    def body(dm_ref, w_ref, c_ref, o_ref):
        k = pl.program_id(0)

        @pl.when(k == 0)
        def _():
            o_ref[...] = jnp.zeros_like(o_ref)

        o_ref[...] += _dot_nt(dm_ref[...].astype(BF16), w_ref[...])

        @pl.when(k == nk - 1)
        def _():
            o_ref[...] = o_ref[...] * (0.5 * _dsilu(c_ref[...]))

    return pl.pallas_call(
        body, name="mods_bwd_c", grid=(nk,),
        in_specs=[pl.BlockSpec((MOD_ROWS, MOD_TN), lambda k: (0, k)),
                  pl.BlockSpec((d, MOD_TN), lambda k: (0, k)),
                  pl.BlockSpec((MOD_ROWS, d), lambda k: (0, 0))],
        out_specs=pl.BlockSpec((MOD_ROWS, d), lambda k: (0, 0)),
        out_shape=jax.ShapeDtypeStruct((MOD_ROWS, d), F32),
        compiler_params=_cp(("arbitrary",)),
    )(dm16, w_bf, c16)


def _mods_bwd_b(dm16):
    n = dm16.shape[1]

    def body(dm_ref, o_ref):
        dm = dm_ref[...]
        acc = dm[0:1, :]
        for r in range(1, 9):
            acc = acc + dm[r:r + 1, :]
        o_ref[...] = jnp.broadcast_to(acc, (8, MOD_TN))

    return pl.pallas_call(
        body, name="mods_bwd_b", grid=(n // MOD_TN,),
        in_specs=[pl.BlockSpec((MOD_ROWS, MOD_TN), lambda j: (0, j))],
        out_specs=pl.BlockSpec((8, MOD_TN), lambda j: (0, j)),
        out_shape=jax.ShapeDtypeStruct((8, n), F32),
        compiler_params=_cp(("arbitrary",)),
    )(dm16)


def _ln_stats(x):
    mu = jnp.mean(x, axis=-1, keepdims=True)
    xc = x - mu
    var = jnp.mean(xc * xc, axis=-1, keepdims=True)
    rstd = lax.rsqrt(var + LN_EPS)
    return xc * rstd, rstd


def _ln_bwd(dxhat, xhat, rstd):
    m1 = jnp.mean(dxhat, axis=-1, keepdims=True)
    m2 = jnp.mean(dxhat * xhat, axis=-1, keepdims=True)
    return rstd * (dxhat - m1 - xhat * m2)


def _row_spec(tb, d):
    return pl.BlockSpec((tb, d), lambda i: (i, 0))


def _par_spec(d):
    return pl.BlockSpec((1, d), lambda i: (0, 0))


def _acc_spec(d):
    return pl.BlockSpec((8, d), lambda i: (0, 0))


def _ln_mod(x, g, b, sh, sc, tb, name):
    n, d = x.shape

    def body(x_ref, g_ref, b_ref, sh_ref, sc_ref, x0_ref, h_ref):
        xhat, _ = _ln_stats(x_ref[...])
        x0 = xhat * g_ref[...] + b_ref[...]
        x0_ref[...] = x0
        h_ref[...] = (x0 * (1.0 + sc_ref[...]) + sh_ref[...]).astype(BF16)

    return pl.pallas_call(
        body, name=name, grid=(n // tb,),
        in_specs=[_row_spec(tb, d)] + [_par_spec(d)] * 4,
        out_specs=[_row_spec(tb, d), _row_spec(tb, d)],
        out_shape=[jax.ShapeDtypeStruct((n, d), F32), jax.ShapeDtypeStruct((n, d), BF16)],
        compiler_params=_cp(("parallel",)),
    )(x, g, b, sh, sc)


def _res_ln(xres, mix, gate, g, b, sh, sc, tb):
    n, d = xres.shape

    def body(xr_ref, mix_ref, gate_ref, g_ref, b_ref, sh_ref, sc_ref, x1_ref, h_ref):
        r = ALPHA * xr_ref[...] + gate_ref[...] * mix_ref[...]
        xhat, _ = _ln_stats(r)
        x1 = xhat * g_ref[...] + b_ref[...]
        x1_ref[...] = x1
        h_ref[...] = (x1 * (1.0 + sc_ref[...]) + sh_ref[...]).astype(BF16)

    return pl.pallas_call(
        body, name="res_ln1", grid=(n // tb,),
        in_specs=[_row_spec(tb, d)] * 2 + [_par_spec(d)] * 5,
        out_specs=[_row_spec(tb, d), _row_spec(tb, d)],
        out_shape=[jax.ShapeDtypeStruct((n, d), F32), jax.ShapeDtypeStruct((n, d), BF16)],
        compiler_params=_cp(("parallel",)),
    )(xres, mix, gate, g, b, sh, sc)


def _final_ln_loss(x1, ffn, gate, g, b, target, tb):
    n, d = x1.shape

    def body(x1_ref, ffn_ref, gate_ref, g_ref, b_ref, t_ref, dffn_ref, dr_ref, acc_ref):
        i = pl.program_id(0)

        @pl.when(i == 0)
        def _():
            acc_ref[...] = jnp.zeros_like(acc_ref)

        ffn = ffn_ref[...]
        r = ALPHA * x1_ref[...] + gate_ref[...] * ffn
        xhat, rstd = _ln_stats(r)
        err = xhat * g_ref[...] + b_ref[...] - t_ref[...]
        dx2 = err * (1.0 / d)
        dr = _ln_bwd(dx2 * g_ref[...], xhat, rstd)
        dr_ref[...] = dr
        dffn_ref[...] = (gate_ref[...] * dr).astype(BF16)
        acc_ref[0:1, :] += jnp.sum(dx2 * xhat, axis=0, keepdims=True)
        acc_ref[1:2, :] += jnp.sum(dx2, axis=0, keepdims=True)
        acc_ref[2:3, :] += jnp.sum(dr * ffn, axis=0, keepdims=True)
        acc_ref[3:4, :] += jnp.sum(err * err, axis=0, keepdims=True)

    return pl.pallas_call(
        body, name="final_ln_loss", grid=(n // tb,),
        in_specs=[_row_spec(tb, d)] * 2 + [_par_spec(d)] * 3 + [_row_spec(tb, d)],
        out_specs=[_row_spec(tb, d), _row_spec(tb, d), _acc_spec(d)],
        out_shape=[jax.ShapeDtypeStruct((n, d), BF16), jax.ShapeDtypeStruct((n, d), F32),
                   jax.ShapeDtypeStruct((8, d), F32)],
        compiler_params=_cp(("arbitrary",)),
    )(x1, ffn, gate, g, b, target)


def _bwd_ln1(dr2, dh2, x1, x0, mix, gate, sc2, g, tb):
    n, d = x1.shape

    def body(dr2_ref, dh2_ref, x1_ref, x0_ref, mix_ref, gate_ref, sc_ref, g_ref, dmix_ref, dr1_ref, acc_ref):
        i = pl.program_id(0)

        @pl.when(i == 0)
        def _():
            acc_ref[...] = jnp.zeros_like(acc_ref)

        dh2 = dh2_ref[...]
        mix = mix_ref[...]
        dx1 = ALPHA * dr2_ref[...] + dh2 * (1.0 + sc_ref[...])
        r = ALPHA * x0_ref[...] + gate_ref[...] * mix
        xhat, rstd = _ln_stats(r)
        dr1 = _ln_bwd(dx1 * g_ref[...], xhat, rstd)
        dr1_ref[...] = dr1
        dmix_ref[...] = (gate_ref[...] * dr1).astype(BF16)
        acc_ref[0:1, :] += jnp.sum(dh2 * x1_ref[...], axis=0, keepdims=True)
        acc_ref[1:2, :] += jnp.sum(dh2, axis=0, keepdims=True)
        acc_ref[2:3, :] += jnp.sum(dx1 * xhat, axis=0, keepdims=True)
        acc_ref[3:4, :] += jnp.sum(dx1, axis=0, keepdims=True)
        acc_ref[4:5, :] += jnp.sum(dr1 * mix, axis=0, keepdims=True)

    return pl.pallas_call(
        body, name="bwd_ln1", grid=(n // tb,),
        in_specs=[_row_spec(tb, d)] * 5 + [_par_spec(d)] * 3,
        out_specs=[_row_spec(tb, d), _row_spec(tb, d), _acc_spec(d)],
        out_shape=[jax.ShapeDtypeStruct((n, d), BF16), jax.ShapeDtypeStruct((n, d), F32),
                   jax.ShapeDtypeStruct((8, d), F32)],
        compiler_params=_cp(("arbitrary",)),
    )(dr2, dh2, x1, x0, mix, gate, sc2, g)


def _bwd_ln0(dres, dh, x, g, b, sc, tb, name):
    n, d = x.shape
    has_res = dres is not None

    def body(*refs):
        if has_res:
            dres_ref, dh_ref, x_ref, g_ref, b_ref, sc_ref, dx_ref, acc_ref = refs
        else:
            dh_ref, x_ref, g_ref, b_ref, sc_ref, dx_ref, acc_ref = refs
        i = pl.program_id(0)

        @pl.when(i == 0)
        def _():
            acc_ref[...] = jnp.zeros_like(acc_ref)

        dh = dh_ref[...]
        xhat, rstd = _ln_stats(x_ref[...])
        x0 = xhat * g_ref[...] + b_ref[...]
        dx0 = dh * (1.0 + sc_ref[...])
        if has_res:
            dx0 = dx0 + ALPHA * dres_ref[...]
        dx_ref[...] = _ln_bwd(dx0 * g_ref[...], xhat, rstd)
        acc_ref[0:1, :] += jnp.sum(dh * x0, axis=0, keepdims=True)
        acc_ref[1:2, :] += jnp.sum(dh, axis=0, keepdims=True)
        acc_ref[2:3, :] += jnp.sum(dx0 * xhat, axis=0, keepdims=True)
        acc_ref[3:4, :] += jnp.sum(dx0, axis=0, keepdims=True)

    ins = ([dres] if has_res else []) + [dh, x, g, b, sc]
    return pl.pallas_call(
        body, name=name, grid=(n // tb,),
        in_specs=[_row_spec(tb, d)] * (3 if has_res else 2) + [_par_spec(d)] * 3,
        out_specs=[_row_spec(tb, d), _acc_spec(d)],
        out_shape=[jax.ShapeDtypeStruct((n, d), F32), jax.ShapeDtypeStruct((8, d), F32)],
        compiler_params=_cp(("arbitrary",)),
    )(*ins)


def _matmul_nn(pairs, out_dtype, tm, tn, name):
    m = pairs[0][0].shape[0]
    n = pairs[0][1].shape[1]
    tm = min(tm, m)
    tn = min(tn, n)
    npair = len(pairs)

    def body(*refs):
        o_ref = refs[-1]
        acc = None
        for p in range(npair):
            t = _dot(refs[2 * p][...].astype(BF16), refs[2 * p + 1][...])
            acc = t if acc is None else acc + t
        o_ref[...] = acc.astype(out_dtype)

    in_specs, args = [], []
    for a, b in pairs:
        k = a.shape[1]
        in_specs += [pl.BlockSpec((tm, k), lambda i, j: (i, 0)), pl.BlockSpec((k, tn), lambda i, j: (0, j))]
        args += [a, b]
    return pl.pallas_call(
        body, name=name, grid=(m // tm, n // tn),
        in_specs=in_specs,
        out_specs=pl.BlockSpec((tm, tn), lambda i, j: (i, j)),
        out_shape=jax.ShapeDtypeStruct((m, n), out_dtype),
        compiler_params=_cp(("parallel", "arbitrary")),
    )(*args)


def _matmul_tn(a, g, tm, tn, name):
    m, k = a.shape
    n = g.shape[1]
    tm = min(tm, m)
    tn = min(tn, n)

    def body(a_ref, g_ref, o_ref):
        i = pl.program_id(1)

        @pl.when(i == 0)
        def _():
            o_ref[...] = jnp.zeros_like(o_ref)

        o_ref[...] += _dot_tn(a_ref[...].astype(BF16), g_ref[...].astype(BF16))

    return pl.pallas_call(
        body, name=name, grid=(n // tn, m // tm),
        in_specs=[pl.BlockSpec((tm, k), lambda j, i: (i, 0)), pl.BlockSpec((tm, tn), lambda j, i: (i, j))],
        out_specs=pl.BlockSpec((k, tn), lambda j, i: (0, j)),
        out_shape=jax.ShapeDtypeStruct((k, n), F32),
        compiler_params=_cp(("parallel", "arbitrary")),
    )(a, g)


def _swiglu_fwd(h, wg, wu, tm, tn):
    m, k = h.shape
    n = wg.shape[1]
    tm = min(tm, m)

    def body(h_ref, wg_ref, wu_ref, gate_ref, up_ref, hmid_ref):
        hv = h_ref[...]
        gate = _dot(hv, wg_ref[...])
        up = _dot(hv, wu_ref[...])
        gate_ref[...] = gate.astype(BF16)
        up_ref[...] = up.astype(BF16)
        hmid_ref[...] = (_silu(gate) * up).astype(BF16)

    blk = pl.BlockSpec((tm, tn), lambda i, j: (i, j))
    wspec = pl.BlockSpec((k, tn), lambda i, j: (0, j))
    return pl.pallas_call(
        body, name="swiglu_fwd", grid=(m // tm, n // tn),
        in_specs=[pl.BlockSpec((tm, k), lambda i, j: (i, 0)), wspec, wspec],
        out_specs=[blk, blk, blk],
        out_shape=[jax.ShapeDtypeStruct((m, n), BF16), jax.ShapeDtypeStruct((m, n), BF16),
                   jax.ShapeDtypeStruct((m, n), BF16)],
        compiler_params=_cp(("parallel", "arbitrary")),
    )(h, wg, wu)


def _swiglu_bwd(dffn, wdt, gate, up, tm, tn):
    m, k = dffn.shape
    n = wdt.shape[1]
    tm = min(tm, m)

    def body(d_ref, w_ref, gate_ref, up_ref, dg_ref, du_ref):
        dh = _dot(d_ref[...], w_ref[...])
        gate = gate_ref[...].astype(F32)
        dg_ref[...] = (dh * up_ref[...].astype(F32) * _dsilu(gate)).astype(BF16)
        du_ref[...] = (dh * _silu(gate)).astype(BF16)

    blk = pl.BlockSpec((tm, tn), lambda i, j: (i, j))
    return pl.pallas_call(
        body, name="swiglu_bwd", grid=(m // tm, n // tn),
        in_specs=[pl.BlockSpec((tm, k), lambda i, j: (i, 0)), pl.BlockSpec((k, tn), lambda i, j: (0, j)), blk, blk],
        out_specs=[blk, blk],
        out_shape=[jax.ShapeDtypeStruct((m, n), BF16), jax.ShapeDtypeStruct((m, n), BF16)],
        compiler_params=_cp(("parallel", "arbitrary")),
    )(dffn, wdt, gate, up)


def _halo_specs(tb, width, nrows):
    r8 = tb // 8
    last = nrows // 8 - 1
    prev = pl.BlockSpec((8, width), lambda i: (jnp.maximum(i * r8 - 1, 0), 0))
    nxt = pl.BlockSpec((8, width), lambda i: (jnp.minimum((i + 1) * r8, last), 0))
    return prev, nxt


def _fill_halo(buf, prev_ref, cur_ref, next_ref, tb, i, nb):
    buf[0:8, :] = prev_ref[...] * (i > 0).astype(F32)
    buf[8:8 + tb, :] = cur_ref[...]
    buf[8 + tb:16 + tb, :] = next_ref[...] * (i < nb - 1).astype(F32)


def _conv_fwd(xd, w8, b, tb, name):
    n = xd.shape[0]
    tb = min(tb, n)
    nb = n // tb
    prev, nxt = _halo_specs(tb, D_XBC, n)

    def body(p_ref, c_ref, n_ref, w_ref, b_ref, u_ref, buf):
        i = pl.program_id(0)
        _fill_halo(buf, p_ref, c_ref, n_ref, tb, i, nb)
        acc = jnp.broadcast_to(b_ref[...], (tb, D_XBC))
        for k in range(D_CONV):
            acc = acc + w_ref[k:k + 1, :] * buf[pl.ds(6 + k, tb), :]
        u_ref[...] = acc

    return pl.pallas_call(
        body, name=name, grid=(nb,),
        in_specs=[prev, pl.BlockSpec((tb, D_XBC), lambda i: (i, 0)), nxt,
                  pl.BlockSpec((8, D_XBC), lambda i: (0, 0)), _par_spec(D_XBC)],
        out_specs=_row_spec(tb, D_XBC),
        out_shape=jax.ShapeDtypeStruct((n, D_XBC), F32),
        scratch_shapes=[pltpu.VMEM((tb + 16, D_XBC), F32)],
        compiler_params=_cp(("parallel",)),
    )(xd, xd, xd, w8, b)


def _conv_bwd_a(dxs, dy, dskip_e, dbc, u, tb, name):
    n = u.shape[0]
    tb = min(tb, n)

    def body(dxs_ref, dy_ref, sk_ref, dbc_ref, u_ref, du_ref, acc_ref):
        i = pl.program_id(0)

        @pl.when(i == 0)
        def _():
            acc_ref[...] = jnp.zeros_like(acc_ref)

        uv = u_ref[...]
        ds = _dsilu(uv)
        gx = dxs_ref[0] + dxs_ref[1] + dy_ref[...] * sk_ref[...]
        gbc = dbc_ref[0] + dbc_ref[1]
        du = jnp.concatenate([gx, gbc], axis=1) * ds
        du_ref[...] = du
        acc_ref[0:1, :] += jnp.sum(du, axis=0, keepdims=True)

    return pl.pallas_call(
        body, name=name, grid=(n // tb,),
        in_specs=[pl.BlockSpec((2, tb, D_MODEL), lambda i: (0, i, 0)), _row_spec(tb, D_MODEL), _par_spec(D_MODEL),
                  pl.BlockSpec((2, tb, 2 * D_STATE), lambda i: (0, i, 0)), _row_spec(tb, D_XBC)],
        out_specs=[_row_spec(tb, D_XBC), _acc_spec(D_XBC)],
        out_shape=[jax.ShapeDtypeStruct((n, D_XBC), F32), jax.ShapeDtypeStruct((8, D_XBC), F32)],
        compiler_params=_cp(("arbitrary",)),
    )(dxs, dy, dskip_e, dbc, u)


def _conv_bwd_b(du, xd, ddt, w8, tb, name):
    n = du.shape[0]
    tb = min(tb, n)
    nb = n // tb
    prev, nxt = _halo_specs(tb, D_XBC, n)

    def body(dp_ref, dc_ref, dn_ref, xp_ref, xc_ref, xn_ref, ddt_ref, w_ref, dxd_ref, acc_ref, dbuf, xbuf):
        i = pl.program_id(0)

        @pl.when(i == 0)
        def _():
            acc_ref[...] = jnp.zeros_like(acc_ref)

        _fill_halo(dbuf, dp_ref, dc_ref, dn_ref, tb, i, nb)
        _fill_halo(xbuf, xp_ref, xc_ref, xn_ref, tb, i, nb)
        duc = dc_ref[...]
        acc = jnp.zeros((tb, D_XBC), F32)
        for k in range(D_CONV):
            acc = acc + w_ref[k:k + 1, :] * dbuf[pl.ds(10 - k, tb), :]
            acc_ref[k:k + 1, :] += jnp.sum(duc * xbuf[pl.ds(6 + k, tb), :], axis=0, keepdims=True)
        dxd_ref[:, 0:D_XBC] = acc.astype(BF16)
        ddt = ddt_ref[0] + pltpu.roll(ddt_ref[1], SSD_HEADS, 1)
        dxd_ref[:, D_XBC:D_XD] = ddt.astype(BF16)

    cur = pl.BlockSpec((tb, D_XBC), lambda i: (i, 0))
    return pl.pallas_call(
        body, name=name, grid=(nb,),
        in_specs=[prev, cur, nxt, prev, cur, nxt,
                  pl.BlockSpec((2, tb, 128), lambda i: (0, i, 0)), pl.BlockSpec((8, D_XBC), lambda i: (0, 0))],
        out_specs=[_row_spec(tb, D_XD), _acc_spec(D_XBC)],
        out_shape=[jax.ShapeDtypeStruct((n, D_XD), BF16), jax.ShapeDtypeStruct((8, D_XBC), F32)],
        scratch_shapes=[pltpu.VMEM((tb + 16, D_XBC), F32), pltpu.VMEM((tb + 16, D_XBC), F32)],
        compiler_params=_cp(("arbitrary",)),
    )(du, du, du, xd, xd, xd, ddt, w8)


def _ssd_chunk_index(nc, reverse):
    def idx(d, k):
        kk = (nc - 1 - k) if reverse else k
        return kk + d * (nc - 1 - 2 * kk)
    return idx


def _ssd_prologue(d, u_ref, xd_ref, bias_ref, a_ref, r_ref):
    q = CHUNK
    xbc = _silu(u_ref[...])
    xs = xbc[:, 0:D_MODEL]
    bm = xbc[:, D_MODEL:D_MODEL + D_STATE]
    cm = xbc[:, D_MODEL + D_STATE:D_XBC]
    row = lax.broadcasted_iota(jnp.int32, (q, q), 0)
    col = lax.broadcasted_iota(jnp.int32, (q, q), 1)
    sgn = 1 - 2 * d
    mask = ((row - col) * sgn) >= 0
    mask_t = ((row - col) * sgn) <= 0
    xdv = xd_ref[...]
    dtraw = jnp.where(d == 0, xdv, pltpu.roll(xdv, 128 - SSD_HEADS, 1)) + bias_ref[...]
    head_lane = col < SSD_HEADS
    dt = jnp.where(head_lane, _softplus(dtraw), 0.0)
    a = a_ref[...]
    tri = jnp.where(mask, 1.0, 0.0).astype(BF16)
    acum = _dot_sel_l(tri, dt * a)
    rexp = r_ref[...]
    alast = jnp.where(d == 0, acum[q - 1:q, :], acum[0:1, :])
    e16 = jnp.exp(acum)
    dend16 = jnp.exp(alast - acum)
    wend16 = dend16 * dt
    e = _dot_sel_r(e16, rexp, n=2)
    wend_e = _dot_sel_r(wend16, rexp, n=2)
    elast_e = jnp.where(d == 0, e[q - 1:q, :], e[0:1, :])
    g = _dot_nt(cm.astype(BF16), bm.astype(BF16))
    return dict(xs=xs, bm=bm, cm=cm, mask=mask, mask_t=mask_t, dtraw=dtraw, head_lane=head_lane, dt=dt, a=a,
                acum=acum, acum_t=acum.T, dt_t=dt.T, e16=e16, dend16=dend16, wend16=wend16, e=e, wend_e=wend_e,
                elast_e=elast_e, g=g, col=col, row=row)


def _ssd_head_mats(p, h):
    seg = p["acum"][:, h:h + 1] - p["acum_t"][h:h + 1, :]
    lm = jnp.exp(jnp.where(p["mask"], seg, -jnp.inf))
    gl = p["g"] * lm
    s = gl * p["dt_t"][h:h + 1, :]
    return lm, gl, s


def _ssd_fwd(u, xd, bias2, a2, rexp, h0, name, gather=()):
    n = u.shape[0]
    nc = n // CHUNK
    q = CHUNK
    cidx = _ssd_chunk_index(nc, reverse=False)
    ng = len(gather)

    def body(u_ref, xd_ref, bias_ref, a_ref, r_ref, h0_ref, *rest):
        g_ins, (y_ref, hp_ref, hf_ref), rest = rest[:ng], rest[ng:ng + 3], rest[ng + 3:]
        g_outs, st, sems = rest[:ng], rest[ng], rest[ng + 1:]
        d = pl.program_id(0)
        k = pl.program_id(1)
        if ng:
            g_start, g_finish = _gather_steps(g_ins, g_outs, *sems)
            pl.when((d == 0) & (k == 0))(g_start)

        @pl.when(k == 0)
        def _():
            st[...] = h0_ref[...]

        p = _ssd_prologue(d, u_ref, xd_ref, bias_ref, a_ref, r_ref)
        stv = st[...]
        st_bf = stv.astype(BF16)
        hp_ref[...] = st_bf
        xs = p["xs"]
        lane128 = p["col"]
        y_off = _dot(p["cm"].astype(BF16), st_bf) * p["e"]
        for pb in range(SSD_HEADS // 2):
            _, _, s0 = _ssd_head_mats(p, 2 * pb)
            _, _, s1 = _ssd_head_mats(p, 2 * pb + 1)
            xp = xs[:, pb * 128:(pb + 1) * 128]
            rhs = jnp.concatenate([jnp.where(lane128 < HEAD_DIM, xp, 0.0), jnp.where(lane128 >= HEAD_DIM, xp, 0.0)],
                                  axis=0).astype(BF16)
            lhs = jnp.concatenate([s0, s1], axis=1).astype(BF16)
            y_ref[:, pb * 128:(pb + 1) * 128] = _dot(lhs, rhs) + y_off[:, pb * 128:(pb + 1) * 128]
        xw = (xs * p["wend_e"]).astype(BF16)
        new = stv * p["elast_e"] + _dot(p["bm"].T.astype(BF16), xw)
        st[...] = new
        hf_ref[...] = new
        if ng:
            pl.when((d == 1) & (k == nc - 1))(g_finish)

    nsem = _GATHER_SEMS * ng
    return pl.pallas_call(
        body, name=name, grid=(2, nc),
        in_specs=[pl.BlockSpec((q, D_XBC), lambda d, k: (cidx(d, k), 0)),
                  pl.BlockSpec((q, 128), lambda d, k: (cidx(d, k), D_XBC // 128)),
                  pl.BlockSpec((None, 1, 128), lambda d, k: (d, 0, 0)),
                  pl.BlockSpec((None, 1, 128), lambda d, k: (d, 0, 0)),
                  pl.BlockSpec((128, D_MODEL), lambda d, k: (0, 0)),
                  pl.BlockSpec((None, D_STATE, D_MODEL), lambda d, k: (d, 0, 0))] + [_ANY] * ng,
        out_specs=[pl.BlockSpec((None, q, D_MODEL), lambda d, k: (d, cidx(d, k), 0)),
                   pl.BlockSpec((None, None, D_STATE, D_MODEL), lambda d, k: (d, cidx(d, k), 0, 0)),
                   pl.BlockSpec((None, D_STATE, D_MODEL), lambda d, k: (d, 0, 0))] + [_ANY] * ng,
        out_shape=[jax.ShapeDtypeStruct((2, n, D_MODEL), F32),
                   jax.ShapeDtypeStruct((2, nc, D_STATE, D_MODEL), BF16),
                   jax.ShapeDtypeStruct((2, D_STATE, D_MODEL), F32)] +
                  [jax.ShapeDtypeStruct((4,) + t.shape, t.dtype) for t in gather],
        scratch_shapes=[pltpu.VMEM((D_STATE, D_MODEL), F32)] +
                       ([pltpu.SemaphoreType.DMA((nsem,)), pltpu.SemaphoreType.DMA((nsem,))] if ng else []),
        compiler_params=_cp(("arbitrary", "arbitrary")),
    )(u, xd, bias2, a2, rexp, h0, *gather)


def _ssd_bwd(u, xd, bias2, a2, rexp, rexp_t, dy, hprev, lam0, name, exchange=()):
    n = u.shape[0]
    nc = n // CHUNK
    q = CHUNK
    cidx = _ssd_chunk_index(nc, reverse=True)
    ne = len(exchange)

    def body(u_ref, xd_ref, bias_ref, a_ref, r_ref, rt_ref, dy_ref, hp_ref, lam0_ref, *rest):
        e_ins, (dxs_ref, dbc_ref, ddt_ref, acc_ref, lamo_ref), rest = rest[:ne], rest[ne:ne + 5], rest[ne + 5:]
        e_outs, lam, sems = rest[:ne], rest[ne], rest[ne + 1:]
        d = pl.program_id(0)
        k = pl.program_id(1)
        if ne:
            e_start, e_finish = _exchange_steps(e_ins, e_outs, *sems)
            pl.when((d == 0) & (k == 0))(e_start)

        @pl.when(k == 0)
        def _():
            lam[...] = lam0_ref[...]
            acc_ref[...] = jnp.zeros_like(acc_ref)

        rexp_t = rt_ref[...]

        def hsum(t):
            return _dot_sel_r(t, rexp_t, n=2)

        p = _ssd_prologue(d, u_ref, xd_ref, bias_ref, a_ref, r_ref)
        xs, bm, cm = p["xs"], p["bm"], p["cm"]
        bm_bf, cm_bf = bm.astype(BF16), cm.astype(BF16)
        lamn = lam[...]
        lamn_bf = lamn.astype(BF16)
        stp = hp_ref[...]
        dyv = dy_ref[...]
        lane128 = p["col"]

        wend_e = p["wend_e"]
        cs = _dot(cm_bf, stp)
        dye_bf = (dyv * p["e"]).astype(BF16)
        dc_off = _dot_nt(dye_bf, stp)
        v = _dot(bm_bf, lamn_bf)
        xw_bf = (xs * wend_e).astype(BF16)
        db_off = _dot_nt(xw_bf, lamn_bf)
        elast_e = p["elast_e"]
        dlast_e = jnp.sum(stp.astype(F32) * lamn, axis=0, keepdims=True) * elast_e
        lam_new = lamn * elast_e + _dot(cm.T.astype(BF16), dye_bf)
        lam[...] = lam_new
        lamo_ref[...] = lam_new

        hs_vx = hsum(v * xs)
        om = p["wend16"] * hs_vx
        x1 = p["e16"] * hsum(dyv * cs) - om
        x2 = p["dend16"] * hs_vx
        x3 = jnp.sum(om, axis=0, keepdims=True) + hsum(jnp.broadcast_to(dlast_e, (8, D_MODEL)))[0:1, :]

        sub16 = lax.broadcasted_iota(jnp.int32, (SSD_HEADS, q), 0)
        rs = jnp.zeros((q, 128), F32)
        cs_m = jnp.zeros((SSD_HEADS, q), F32)
        dt_m = jnp.zeros((SSD_HEADS, q), F32)
        dg = jnp.zeros((q, q), F32)
        for pb in range(SSD_HEADS // 2):
            xp_bf = xs[:, pb * 128:(pb + 1) * 128].astype(BF16)
            dyp = dyv[:, pb * 128:(pb + 1) * 128]
            dxs_pair = None
            for half in range(2):
                h = 2 * pb + half
                sel = (lane128 < HEAD_DIM) if half == 0 else (lane128 >= HEAD_DIM)
                dyh_bf = jnp.where(sel, dyp, 0.0).astype(BF16)
                lm, gl, s = _ssd_head_mats(p, h)
                ds = _dot_nt(dyh_bf, xp_bf)
                t = _dot_tn(s.astype(BF16), dyh_bf)
                dxs_pair = t if dxs_pair is None else dxs_pair + t
                w = ds * s
                rs = rs + jnp.sum(w, axis=1, keepdims=True) * (lane128 == h).astype(F32)
                cs_m = jnp.where(sub16 == h, jnp.sum(w, axis=0, keepdims=True), cs_m)
                dt_m = jnp.where(sub16 == h, jnp.sum(ds * gl, axis=0, keepdims=True), dt_m)
                dg = dg + ds * lm * p["dt_t"][h:h + 1, :]
            sl = slice(pb * 128, (pb + 1) * 128)
            dxs_ref[:, sl] = dxs_pair + v[:, sl] * wend_e[:, sl]

        def to_lanes(m16):
            return jnp.concatenate([m16, jnp.zeros((128 - SSD_HEADS, q), F32)], axis=0).T

        last = jnp.where(d == 0, q - 1, 0)
        dacum = rs - to_lanes(cs_m) + x1 + jnp.where(p["row"] == last, x3[0:1, :], 0.0)
        tri_t = jnp.where(p["mask_t"], 1.0, 0.0).astype(BF16)
        ddta = _dot_sel_l(tri_t, dacum)
        dt = p["dt"]
        a = p["a"]
        ddt = to_lanes(dt_m) + x2 + a * ddta
        ddtraw = jnp.where(p["head_lane"], ddt * _sigmoid(p["dtraw"]), 0.0)
        ddt_ref[...] = ddtraw
        acc_ref[0:1, :] += jnp.sum(ddtraw, axis=0, keepdims=True)
        acc_ref[1:2, :] += jnp.sum(dt * ddta, axis=0, keepdims=True) * a

        dg_bf = dg.astype(BF16)
        dbc_ref[:, 0:D_STATE] = _dot_tn(dg_bf, cm_bf) + db_off
        dbc_ref[:, D_STATE:2 * D_STATE] = _dot(dg_bf, bm_bf) + dc_off
        if ne:
            pl.when((d == 1) & (k == nc - 1))(e_finish)

    cblk = lambda d, k: (cidx(d, k), 0)
    return pl.pallas_call(
        body, name=name, grid=(2, nc),
        in_specs=[pl.BlockSpec((q, D_XBC), cblk),
                  pl.BlockSpec((q, 128), lambda d, k: (cidx(d, k), D_XBC // 128)),
                  pl.BlockSpec((None, 1, 128), lambda d, k: (d, 0, 0)),
                  pl.BlockSpec((None, 1, 128), lambda d, k: (d, 0, 0)),
                  pl.BlockSpec((128, D_MODEL), lambda d, k: (0, 0)),
                  pl.BlockSpec((D_MODEL, 128), lambda d, k: (0, 0)),
                  pl.BlockSpec((q, D_MODEL), cblk),
                  pl.BlockSpec((None, None, D_STATE, D_MODEL), lambda d, k: (d, cidx(d, k), 0, 0)),
                  pl.BlockSpec((None, D_STATE, D_MODEL), lambda d, k: (d, 0, 0))] + [_ANY] * ne,
        out_specs=[pl.BlockSpec((None, q, D_MODEL), lambda d, k: (d, cidx(d, k), 0)),
                   pl.BlockSpec((None, q, 2 * D_STATE), lambda d, k: (d, cidx(d, k), 0)),
                   pl.BlockSpec((None, q, 128), lambda d, k: (d, cidx(d, k), 0)),
                   pl.BlockSpec((None, 8, 128), lambda d, k: (d, 0, 0)),
                   pl.BlockSpec((None, D_STATE, D_MODEL), lambda d, k: (d, 0, 0))] + [_ANY] * ne,
        out_shape=[jax.ShapeDtypeStruct((2, n, D_MODEL), F32),
                   jax.ShapeDtypeStruct((2, n, 2 * D_STATE), F32),
                   jax.ShapeDtypeStruct((2, n, 128), F32),
                   jax.ShapeDtypeStruct((2, 8, 128), F32),
                   jax.ShapeDtypeStruct((2, D_STATE, D_MODEL), F32)] +
                  [jax.ShapeDtypeStruct(t.shape, t.dtype) for t in exchange],
        scratch_shapes=[pltpu.VMEM((D_STATE, D_MODEL), F32)] +
                       ([pltpu.SemaphoreType.DMA((3 * ne,)), pltpu.SemaphoreType.DMA((3 * ne,)),
                         pltpu.SemaphoreType.DMA((ne,))] if ne else []),
        compiler_params=_cp(("arbitrary", "arbitrary")),
    )(u, xd, bias2, a2, rexp, rexp_t, dy, hprev, lam0, *exchange)


def _merge_fwd(y, u, z, dskip_e, gn, tb):
    n = z.shape[0]

    def body(y_ref, u_ref, z_ref, sk_ref, gn_ref, o_ref):
        xs = _silu(u_ref[...])
        ys = y_ref[0] + y_ref[1] + sk_ref[...] * xs
        gated = ys * _silu(z_ref[...])
        rstd = lax.rsqrt(jnp.mean(gated * gated, axis=-1, keepdims=True) + LN_EPS)
        o_ref[...] = (gated * rstd * gn_ref[...]).astype(BF16)

    return pl.pallas_call(
        body, name="merge_fwd", grid=(n // tb,),
        in_specs=[pl.BlockSpec((2, tb, D_MODEL), lambda i: (0, i, 0)), pl.BlockSpec((tb, D_MODEL), lambda i: (i, 0)),
                  _row_spec(tb, D_MODEL), _par_spec(D_MODEL), _par_spec(D_MODEL)],
        out_specs=_row_spec(tb, D_MODEL),
        out_shape=jax.ShapeDtypeStruct((n, D_MODEL), BF16),
        compiler_params=_cp(("parallel",)),
    )(y, u, z, dskip_e, gn)


def _merge_bwd(dyn, y, u, z, dskip_e, gn, tb):
    n = z.shape[0]

    def body(dyn_ref, y_ref, u_ref, z_ref, sk_ref, gn_ref, dy_ref, dz_ref, acc_ref):
        i = pl.program_id(0)

        @pl.when(i == 0)
        def _():
            acc_ref[...] = jnp.zeros_like(acc_ref)

        xs = _silu(u_ref[...])
        zv = z_ref[...]
        ys = y_ref[0] + y_ref[1] + sk_ref[...] * xs
        gated = ys * _silu(zv)
        rstd = lax.rsqrt(jnp.mean(gated * gated, axis=-1, keepdims=True) + LN_EPS)
        ghat = gated * rstd
        dyn_v = dyn_ref[...]
        t = dyn_v * gn_ref[...]
        dgated = rstd * (t - ghat * jnp.mean(t * ghat, axis=-1, keepdims=True))
        dys = dgated * _silu(zv)
        dy_ref[...] = dys
        dz_ref[...] = (dgated * ys * _dsilu(zv)).astype(BF16)
        acc_ref[0:1, :] += jnp.sum(dyn_v * ghat, axis=0, keepdims=True)
        acc_ref[1:2, :] += jnp.sum(dys * xs, axis=0, keepdims=True)

    return pl.pallas_call(
        body, name="merge_bwd", grid=(n // tb,),
        in_specs=[_row_spec(tb, D_MODEL), pl.BlockSpec((2, tb, D_MODEL), lambda i: (0, i, 0)),
                  pl.BlockSpec((tb, D_MODEL), lambda i: (i, 0)), _row_spec(tb, D_MODEL),
                  _par_spec(D_MODEL), _par_spec(D_MODEL)],
        out_specs=[_row_spec(tb, D_MODEL), _row_spec(tb, D_MODEL), _acc_spec(D_MODEL)],
        out_shape=[jax.ShapeDtypeStruct((n, D_MODEL), F32), jax.ShapeDtypeStruct((n, D_MODEL), BF16),
                   jax.ShapeDtypeStruct((8, D_MODEL), F32)],
        compiler_params=_cp(("arbitrary",)),
    )(dyn, y, u, z, dskip_e, gn)


def _pool_consts(transpose):
    tb = POOL_TB
    t = jnp.arange(tb)
    s = jnp.arange(3 * tb)
    rl, cl = t // GRID_W, t % GRID_W
    rs_, cs_ = s // GRID_W - tb // GRID_W, s % GRID_W
    s2 = jnp.arange(tb)
    rl2, cl2 = s2 // GRID_W, s2 % GRID_W
    brow, bcol = [], []
    for w in POOL_WINDOWS:
        lo, hi = -(w // 2), w - w // 2
        if transpose:
            lo, hi = -hi + 1, -lo + 1
        dr = rs_[None, :] - rl[:, None]
        brow.append(((cs_[None, :] == cl[:, None]) & (dr >= lo) & (dr < hi)).astype(BF16))
        dc = cl2[None, :] - cl[:, None]
        bcol.append(((rl2[None, :] == rl[:, None]) & (dc >= lo) & (dc < hi)).astype(BF16))
    return jnp.stack(brow), jnp.stack(bcol)


def _pool_inv(i, g, n):
    assert GRID_W == 64
    t = i * POOL_TB + lax.broadcasted_iota(jnp.int32, (POOL_TB, 1), 0)
    r = lax.shift_right_logical(t, 6)
    col = t & (GRID_W - 1)
    w = POOL_WINDOWS[g]
    lo, hi = -(w // 2), w - w // 2
    cnt_r = jnp.minimum(r + hi, n // GRID_W) - jnp.maximum(r + lo, 0)
    cnt_c = jnp.minimum(col + hi, GRID_W) - jnp.maximum(col + lo, 0)
    return 1.0 / (cnt_r * cnt_c).astype(F32)


def _pool_box(prev_ref, cur_ref, next_ref, brow_ref, bcol_ref, g, i, nb):
    sl = slice(g * POOL_DIM, (g + 1) * POOL_DIM)
    pv = prev_ref[:, sl] * (i > 0).astype(prev_ref.dtype)
    nx = next_ref[:, sl] * (i < nb - 1).astype(next_ref.dtype)
    stack = jnp.concatenate([pv.astype(BF16), cur_ref[:, sl].astype(BF16), nx.astype(BF16)], axis=0)
    r = _dot(brow_ref[g], stack)
    return _dot(bcol_ref[g], r.astype(BF16))


def _pool_halo_specs(n, d):
    tb = POOL_TB
    nb = n // tb
    prev = pl.BlockSpec((tb, d), lambda i: (jnp.maximum(i - 1, 0), 0))
    cur = pl.BlockSpec((tb, d), lambda i: (i, 0))
    nxt = pl.BlockSpec((tb, d), lambda i: (jnp.minimum(i + 1, nb - 1), 0))
    return prev, cur, nxt


def _pool_const_specs():
    tb = POOL_TB
    return [pl.BlockSpec((N_POOL, tb, 3 * tb), lambda i: (0, 0, 0)),
            pl.BlockSpec((N_POOL, tb, tb), lambda i: (0, 0, 0))]


def _pool_fwd(up, consts, pw_bf, pscale):
    n = up.shape[0]
    tb = POOL_TB
    nb = n // tb
    brow, bcol = consts
    prev, cur, nxt = _pool_halo_specs(n, D_MODEL)

    def body(p_ref, c_ref, n_ref, brow_ref, bcol_ref, pw_ref, sc_ref, o_ref, d_ref):
        i = pl.program_id(0)
        for g in range(N_POOL):
            sl = slice(g * POOL_DIM, (g + 1) * POOL_DIM)
            box = _pool_box(p_ref, c_ref, n_ref, brow_ref, bcol_ref, g, i, nb)
            dd = (box * _pool_inv(i, g, n) - c_ref[:, sl]).astype(BF16)
            d_ref[:, sl] = dd
            o_ref[:, sl] = (_dot(dd, pw_ref[g]) * sc_ref[:, sl]).astype(BF16)

    return pl.pallas_call(
        body, name="pool_fwd", grid=(nb,),
        in_specs=[prev, cur, nxt] + _pool_const_specs() +
                 [pl.BlockSpec((N_POOL, POOL_DIM, POOL_DIM), lambda i: (0, 0, 0)), _par_spec(D_MODEL)],
        out_specs=[_row_spec(tb, D_MODEL), _row_spec(tb, D_MODEL)],
        out_shape=[jax.ShapeDtypeStruct((n, D_MODEL), BF16), jax.ShapeDtypeStruct((n, D_MODEL), BF16)],
        compiler_params=_cp(("parallel",)),
    )(up, up, up, brow, bcol, pw_bf, pscale)


def _pool_bwd_a(dp, dsave, pw_bf, pwt_bf, pscale):
    n = dp.shape[0]
    tb = POOL_TB

    def body(dp_ref, d_ref, pw_ref, pwt_ref, sc_ref, dd_ref, dds_ref, gw_ref, gs_ref):
        i = pl.program_id(0)

        @pl.when(i == 0)
        def _():
            gw_ref[...] = jnp.zeros_like(gw_ref)
            gs_ref[...] = jnp.zeros_like(gs_ref)

        for g in range(N_POOL):
            sl = slice(g * POOL_DIM, (g + 1) * POOL_DIM)
            dpv = dp_ref[:, sl]
            dv = d_ref[:, sl]
            dpw_bf = (dpv * sc_ref[:, sl]).astype(BF16)
            dd = _dot(dpw_bf, pwt_ref[g])
            dd_ref[:, sl] = dd
            dds_ref[:, sl] = (dd * _pool_inv(i, g, n)).astype(BF16)
            gw_ref[g] += _dot_tn(dv, dpw_bf)
            gs_ref[0:1, sl] += jnp.sum(dpv * _dot(dv, pw_ref[g]), axis=0, keepdims=True)

    wspec = pl.BlockSpec((N_POOL, POOL_DIM, POOL_DIM), lambda i: (0, 0, 0))
    return pl.pallas_call(
        body, name="pool_bwd_a", grid=(n // tb,),
        in_specs=[_row_spec(tb, D_MODEL), _row_spec(tb, D_MODEL), wspec, wspec, _par_spec(D_MODEL)],
        out_specs=[_row_spec(tb, D_MODEL), _row_spec(tb, D_MODEL), wspec, _acc_spec(D_MODEL)],
        out_shape=[jax.ShapeDtypeStruct((n, D_MODEL), F32), jax.ShapeDtypeStruct((n, D_MODEL), BF16),
                   jax.ShapeDtypeStruct((N_POOL, POOL_DIM, POOL_DIM), F32), jax.ShapeDtypeStruct((8, D_MODEL), F32)],
        compiler_params=_cp(("arbitrary",)),
    )(dp, dsave, pw_bf, pwt_bf, pscale)


def _pool_bwd_b(dds, dd, consts_t):
    n = dd.shape[0]
    tb = POOL_TB
    nb = n // tb
    brow, bcol = consts_t
    prev, cur, nxt = _pool_halo_specs(n, D_MODEL)

    def body(p_ref, c_ref, n_ref, brow_ref, bcol_ref, dd_ref, o_ref):
        i = pl.program_id(0)
        for g in range(N_POOL):
            sl = slice(g * POOL_DIM, (g + 1) * POOL_DIM)
            box = _pool_box(p_ref, c_ref, n_ref, brow_ref, bcol_ref, g, i, nb)
            o_ref[:, sl] = (box - dd_ref[:, sl]).astype(BF16)

    return pl.pallas_call(
        body, name="pool_bwd_b", grid=(nb,),
        in_specs=[prev, cur, nxt] + _pool_const_specs() + [_row_spec(tb, D_MODEL)],
        out_specs=_row_spec(tb, D_MODEL),
        out_shape=jax.ShapeDtypeStruct((n, D_MODEL), BF16),
        compiler_params=_cp(("parallel",)),
    )(dds, dds, dds, brow, bcol, dd)


def _pair_add(slabs, recvs, core, name):
    na = len(slabs)
    hr = [t.shape[1] // 4 for t in slabs]

    def body(core_ref, *refs):
        for a in range(na):
            refs[2 * na + a][...] = (refs[a][...] + refs[na + a][...]).astype(BF16)

    own = [pl.BlockSpec((None, hr[a], slabs[a].shape[2]), lambda j, i, c_ref: (j, 2 * c_ref[0] + i, 0)) for a in range(na)]
    got = [pl.BlockSpec((None, hr[a], slabs[a].shape[2]), lambda j, i, c_ref: (j, i, 0)) for a in range(na)]
    return pl.pallas_call(
        body, name=name,
        grid_spec=pltpu.PrefetchScalarGridSpec(num_scalar_prefetch=1, grid=(4, 2), in_specs=own + got, out_specs=got),
        out_shape=[jax.ShapeDtypeStruct(r.shape, BF16) for r in recvs],
        compiler_params=_cp(("arbitrary", "arbitrary")),
    )(core, *slabs, *recvs)


def _sum4(parts, core):
    na = len(parts)
    hr = [t.shape[1] // 2 for t in parts]

    def body(core_ref, *refs):
        for a in range(na):
            p = refs[a]
            refs[na + a][...] = ((p[0].astype(F32) + p[1].astype(F32)) + p[2].astype(F32)) + p[3].astype(F32)

    return pl.pallas_call(
        body, name="reduce_g_sum",
        grid_spec=pltpu.PrefetchScalarGridSpec(
            num_scalar_prefetch=1, grid=(2,),
            in_specs=[pl.BlockSpec((4, hr[a], parts[a].shape[2]), lambda i, c_ref: (0, i, 0)) for a in range(na)],
            out_specs=[pl.BlockSpec((hr[a], parts[a].shape[2]), lambda i, c_ref: (2 * c_ref[0] + i, 0))
                       for a in range(na)]),
        out_shape=[jax.ShapeDtypeStruct((2 * t.shape[1], t.shape[2]), F32) for t in parts],
        compiler_params=_cp(("arbitrary",)),
    )(core, *parts)


def _adamw(w, g, m, v, name):
    r, cdim = w.shape
    tb = _row_block(r, 256)
    c1 = 1.0 - ADAM_B1 ** ADAM_STEP
    c2 = 1.0 - ADAM_B2 ** ADAM_STEP

    def body(w_ref, g_ref, m_ref, v_ref, d_ref, nm_ref, nv_ref):
        gv = g_ref[...]
        nm = ADAM_B1 * m_ref[...] + (1.0 - ADAM_B1) * gv
        nv = ADAM_B2 * v_ref[...] + (1.0 - ADAM_B2) * (gv * gv)
        m_hat = nm / c1
        v_hat = nv / c2
        d_ref[...] = -ADAM_LR * (m_hat / (jnp.sqrt(v_hat) + ADAM_EPS) + ADAM_WD * w_ref[...])
        nm_ref[...] = nm
        nv_ref[...] = nv

    spec = _row_spec(tb, cdim)
    shp = jax.ShapeDtypeStruct((r, cdim), F32)
    return pl.pallas_call(
        body, name=name, grid=(r // tb,),
        in_specs=[spec] * 4, out_specs=[spec] * 3, out_shape=[shp] * 3,
        compiler_params=_cp(("parallel",)),
    )(w, g, m, v)


def _mesh_pos():
    return lax.axis_index("x"), lax.axis_index("y"), lax.axis_index("c")


_ANY = pl.BlockSpec(memory_space=pl.ANY)


def _remote(src, dst, send_sem, recv_sem, device):
    return pltpu.make_async_remote_copy(src_ref=src, dst_ref=dst, send_sem=send_sem, recv_sem=recv_sem,
                                        device_id=device, device_id_type=MESH)


def _other_chips(x, y):
    return [(1 - x, y), (x, 1 - y), (1 - x, 1 - y)]


def _half(nrows, h):
    return pl.ds(h * (nrows // 2), nrows // 2)


_GATHER_SEMS = 7


def _gather_steps(ins, outs, send_sems, recv_sems):
    na = len(ins)
    nrow = [r.shape[0] for r in ins]

    def copies():
        x, y, c = _mesh_pos()
        me = 2 * x + y
        sib = (x, y, 1 - c)
        chips = _other_chips(x, y)

        def ici(k, a, slot):
            px, py = chips[k]
            rows = _half(nrow[a], c)
            return _remote(ins[a].at[rows, :], outs[a].at[slot, rows, :], send_sems.at[k * na + a],
                           recv_sems.at[k * na + a], (px, py, c))

        def fwd(k, a, h):
            px, py = chips[k]
            blk = outs[a].at[2 * px + py, _half(nrow[a], h), :]
            return _remote(blk, blk, send_sems.at[(3 + k) * na + a], recv_sems.at[(3 + k) * na + a], sib)

        def own(a):
            return _remote(ins[a], outs[a].at[me], send_sems.at[6 * na + a], recv_sems.at[6 * na + a], sib)

        slots = [2 * px + py for px, py in chips]
        return ici, fwd, own, me, c, slots

    def start():
        ici, _, own, me, _, _ = copies()
        for a in range(na):
            own(a).start()
        for k in range(3):
            for a in range(na):
                ici(k, a, me).start()

    def finish():
        ici, fwd, own, me, c, slots = copies()
        for k in range(3):
            for a in range(na):
                ici(k, a, slots[k]).wait_recv()
                fwd(k, a, c).start()
        for k in range(3):
            for a in range(na):
                fwd(k, a, 1 - c).wait_recv()
        for a in range(na):
            own(a).wait_recv()
        for a in range(na):
            own(a).wait_send()
        for k in range(3):
            for a in range(na):
                ici(k, a, me).wait_send()
                fwd(k, a, c).wait_send()

    return start, finish


def _exchange_steps(ins, outs, send_sems, recv_sems, local_sems):
    na = len(ins)

    def copies():
        x, y, c = _mesh_pos()
        me = 2 * x + y
        chips = _other_chips(x, y)

        def copy(k, a, slot):
            px, py = chips[k]
            return _remote(ins[a].at[2 * px + py], outs[a].at[slot], send_sems.at[k * na + a], recv_sems.at[k * na + a],
                           (px, py, c))

        def local(a):
            return pltpu.make_async_copy(ins[a].at[me], outs[a].at[me], local_sems.at[a])

        return copy, local, me, [2 * px + py for px, py in chips]

    def start():
        copy, local, me, _ = copies()
        for a in range(na):
            local(a).start()
        for k in range(3):
            for a in range(na):
                copy(k, a, me).start()

    def finish():
        copy, local, me, slots = copies()
        for k in range(3):
            for a in range(na):
                copy(k, a, slots[k]).wait_recv()
        for k in range(3):
            for a in range(na):
                copy(k, a, me).wait_send()
        for a in range(na):
            local(a).wait()

    return start, finish


def _gather_weights(shards, conv8):
    na = len(shards)

    def body(*refs):
        ins, conv_in = refs[:na], refs[na]
        outs, conv_out = refs[na + 1:2 * na + 1], refs[2 * na + 1]
        send_sems, recv_sems, local_sems = refs[2 * na + 2:]
        x, y, c = _mesh_pos()
        me = 2 * x + y
        chips = _other_chips(x, y)

        def conv(k, slot):
            px, py = chips[k]
            return _remote(conv_in, conv_out.at[slot], send_sems.at[7 * na + k], recv_sems.at[7 * na + k], (px, py, c))

        start, finish = _gather_steps(ins, outs, send_sems, recv_sems)
        local = pltpu.make_async_copy(conv_in, conv_out.at[me], local_sems.at[0])
        local.start()
        start()
        sends = [conv(k, me) for k in range(3)]
        for cp in sends:
            cp.start()
        finish()
        for k in range(3):
            px, py = chips[k]
            conv(k, 2 * px + py).wait_recv()
        for cp in sends:
            cp.wait_send()
        local.wait()

    nsem = _GATHER_SEMS * na + 3
    return pl.pallas_call(
        body, name="gather_w", in_specs=[_ANY] * (na + 1), out_specs=[_ANY] * (na + 1),
        out_shape=[jax.ShapeDtypeStruct((4,) + t.shape, t.dtype) for t in shards] +
                  [jax.ShapeDtypeStruct((4,) + conv8.shape, conv8.dtype)],
        scratch_shapes=[pltpu.SemaphoreType.DMA((nsem,)), pltpu.SemaphoreType.DMA((nsem,)),
                        pltpu.SemaphoreType.DMA((1,))],
    )(*shards, conv8)


def _pair_swap(slabs, name):
    na = len(slabs)

    def body(*refs):
        ins, outs = refs[:na], refs[na:2 * na]
        send_sems, recv_sems = refs[2 * na:]
        x, y, c = _mesh_pos()
        cps = [_remote(ins[a].at[:, _half(slabs[a].shape[1], 1 - c), :], outs[a], send_sems.at[a], recv_sems.at[a],
                       (x, y, 1 - c)) for a in range(na)]
        for cp in cps:
            cp.start()
        for cp in cps:
            cp.wait()

    return pl.pallas_call(
        body, name=name, in_specs=[_ANY] * na, out_specs=[_ANY] * na,
        out_shape=[jax.ShapeDtypeStruct((4, t.shape[1] // 2, t.shape[2]), t.dtype) for t in slabs],
        scratch_shapes=[pltpu.SemaphoreType.DMA((na,)), pltpu.SemaphoreType.DMA((na,))],
    )(*slabs)


def _chip_exchange(pairs):
    na = len(pairs)

    def body(*refs):
        start, finish = _exchange_steps(refs[:na], refs[na:2 * na], *refs[2 * na:])
        start()
        finish()

    return pl.pallas_call(
        body, name="reduce_g_ici", in_specs=[_ANY] * na, out_specs=[_ANY] * na,
        out_shape=[jax.ShapeDtypeStruct(t.shape, t.dtype) for t in pairs],
        scratch_shapes=[pltpu.SemaphoreType.DMA((3 * na,)), pltpu.SemaphoreType.DMA((3 * na,)),
                        pltpu.SemaphoreType.DMA((na,))],
    )(*pairs)


def _share_halves(totals):
    na = len(totals)

    def body(*refs):
        bufs = refs[na:2 * na]
        send_sems, recv_sems = refs[2 * na:]
        x, y, c = _mesh_pos()

        def copy(a, h):
            blk = bufs[a].at[_half(totals[a].shape[0], h), :]
            return _remote(blk, blk, send_sems.at[a], recv_sems.at[a], (x, y, 1 - c))

        sends = [copy(a, c) for a in range(na)]
        for cp in sends:
            cp.start()
        for a in range(na):
            copy(a, 1 - c).wait_recv()
        for cp in sends:
            cp.wait_send()

    return pl.pallas_call(
        body, name="reduce_g_share", in_specs=[_ANY] * na, out_specs=[_ANY] * na,
        out_shape=[jax.ShapeDtypeStruct(t.shape, t.dtype) for t in totals],
        input_output_aliases={a: a for a in range(na)},
        scratch_shapes=[pltpu.SemaphoreType.DMA((na,)), pltpu.SemaphoreType.DMA((na,))],
    )(*totals)


def _allreduce_small(v, name):
    r, cdim = v.shape

    def body(v_ref, out_ref, buf, send_sems, recv_sems):
        x, y, c = _mesh_pos()
        me = 4 * x + 2 * y + c
        buf[me] = v_ref[...]
        rel = [(bx, by, bc) for bx in (0, 1) for by in (0, 1) for bc in (0, 1)][1:]

        def peer(b):
            bx, by, bc = b
            return ((1 - x) if bx else x, (1 - y) if by else y, (1 - c) if bc else c)

        def copy(k, slot):
            return pltpu.make_async_remote_copy(
                src_ref=v_ref, dst_ref=buf.at[slot], send_sem=send_sems.at[k], recv_sem=recv_sems.at[k],
                device_id=peer(rel[k]), device_id_type=MESH)

        sends = [copy(k, me) for k in range(7)]
        for cp in sends:
            cp.start()
        for k in range(7):
            px, py, pc = peer(rel[k])
            copy(k, 4 * px + 2 * py + pc).wait_recv()
        for cp in sends:
            cp.wait_send()
        acc = buf[0]
        for j in range(1, 8):
            acc = acc + buf[j]
        out_ref[...] = acc

    vm = pl.BlockSpec(memory_space=pltpu.VMEM)
    return pl.pallas_call(
        body, name=name, in_specs=[vm], out_specs=[vm, vm],
        out_shape=[jax.ShapeDtypeStruct((r, cdim), F32), jax.ShapeDtypeStruct((8, r, cdim), F32)],
        scratch_shapes=[pltpu.SemaphoreType.DMA((7,)), pltpu.SemaphoreType.DMA((7,))],
    )(v)


def _chip_bcast(v, name):
    def body(v_ref, out_ref, send_sems, recv_sems):
        x, y, c = _mesh_pos()
        me = 2 * x + y
        chips = _other_chips(x, y)
        out_ref[me] = v_ref[...]

        def copy(k, slot):
            px, py = chips[k]
            return _remote(v_ref, out_ref.at[slot], send_sems.at[k], recv_sems.at[k], (px, py, c))

        sends = [copy(k, me) for k in range(3)]
        for cp in sends:
            cp.start()
        for k, (px, py) in enumerate(chips):
            copy(k, 2 * px + py).wait_recv()
        for cp in sends:
            cp.wait_send()

    vm = pl.BlockSpec(memory_space=pltpu.VMEM)
    return pl.pallas_call(
        body, name=name, in_specs=[vm], out_specs=vm,
        out_shape=jax.ShapeDtypeStruct((4,) + v.shape, F32),
        scratch_shapes=[pltpu.SemaphoreType.DMA((3,)), pltpu.SemaphoreType.DMA((3,))],
    )(v)


_BIG = (("in_proj", (D_MODEL, D_IN_PROJ // 4), 1), ("w_out", (2 * D_MODEL // 4, D_MODEL), 0),
        ("w_gate", (D_MODEL, D_FF // 4), 1), ("w_up", (D_MODEL, D_FF // 4), 1), ("w_down", (D_FF // 4, D_MODEL), 0),
        ("pool_w", (N_POOL * POOL_DIM // 4, POOL_DIM), None))


def _assemble(name, t):
    _, r, c = t.shape
    axis = {n: ax for n, _, ax in _BIG}[name]
    if axis == 0:
        return t.reshape(4 * r, c)
    if axis == 1:
        return t.transpose(1, 0, 2).reshape(r, 4 * c)
    return t.reshape(4, N_POOL, POOL_DIM // 4, POOL_DIM).transpose(1, 0, 2, 3).reshape(N_POOL, POOL_DIM, POOL_DIM)


def _to_slabs(name, g):
    (r, c), axis = {n: (sh, ax) for n, sh, ax in _BIG}[name]
    if axis == 0:
        return g.reshape(4, r, c)
    if axis == 1:
        return g.reshape(r, 4, c).transpose(1, 0, 2)
    return g.reshape(N_POOL, 4, POOL_DIM // 4, POOL_DIM).transpose(1, 0, 2, 3).reshape(4, r, c)


_EARLY = ("in_proj",)
_LATE = tuple(n for n, _, _ in _BIG if n not in _EARLY)


def _reduce_grads(early_grads, late_parts, core):
    slabs = [_to_slabs(n, early_grads[n]) for n in _EARLY]
    pairs = _pair_add(slabs, _pair_swap(slabs, "reduce_g_d2d"), core, "reduce_g_pair")
    parts = dict(zip(_EARLY, _chip_exchange(pairs)), **dict(zip(_LATE, late_parts)))
    names = [n for n, _, _ in _BIG]
    totals = _sum4([parts[n] for n in names], core)
    return dict(zip(names, _share_halves(totals)))


def _pad_cols(w, n):
    return jnp.concatenate([w, jnp.zeros((w.shape[0], n - w.shape[1]), w.dtype)], axis=1)


def _device_step(x, mod, mod_ctx, ctx, target, wts, w8, small, tb, late_shards=None, core=None):
    n = x.shape[0]
    d = D_MODEL

    win = wts["in_proj"]
    wz, wxd, wup = win[:, 0:d], _pad_cols(win[:, d:d + D_XBC + 2 * SSD_HEADS], D_XD), win[:, d + D_XBC + 2 * SSD_HEADS:]

    emb_g, emb_b = _vec(small["emb_ln_g"]), _vec(small["emb_ln_b"])
    ln1_g, ln1_b = _vec(small["ln1_g"]), _vec(small["ln1_b"])
    ln2_g, ln2_b = _vec(small["ln2_g"]), _vec(small["ln2_b"])
    gn = _vec(small["ssd_norm_g"])
    pscale = _vec(small["pool_scale"])
    conv_b = _vec(small["conv_b"])
    dskip_e = jnp.repeat(small["d_skip"].reshape(-1), HEAD_DIM).reshape(1, d)
    zpad = jnp.zeros((2, 1, 128 - SSD_HEADS), F32)
    bias2 = jnp.concatenate([small["dt_bias"].reshape(2, 1, SSD_HEADS), zpad], axis=2)
    a2 = jnp.concatenate([-jnp.exp(small["a_log"].reshape(2, 1, SSD_HEADS)), zpad], axis=2)
    rexp = (jnp.arange(128)[:, None] == (jnp.arange(d)[None, :] // HEAD_DIM)).astype(BF16)
    rexp_t = rexp.T

    sh1, sc1, g1, sh2, sc2, g2 = [mod[:, i * d:(i + 1) * d] for i in range(6)]
    sh1c, sc1c = mod_ctx[:, 0:d], mod_ctx[:, d:2 * d]

    tbc = min(tb, ctx.shape[0])
    xc0, hc = _ln_mod(ctx, emb_g, emb_b, sh1c, sc1c, tbc, "ln_mod_ctx")
    xdc = _matmul_nn([(hc, wxd)], F32, 512, D_XD, "in_proj_ctx")
    uc = _conv_fwd(xdc, w8, conv_b, tbc, "conv_fwd_ctx")
    hzero = jnp.zeros((2, D_STATE, d), F32)
    _, hprev_c, hfin_c = _ssd_fwd(uc, xdc, bias2, a2, rexp, hzero, "ssd_fwd_ctx")

    x0, h1 = _ln_mod(x, emb_g, emb_b, sh1, sc1, tb, "ln_mod")
    z = _matmul_nn([(h1, wz)], F32, MM_ROWS, 1024, "in_proj_z")
    xd = _matmul_nn([(h1, wxd)], F32, MM_ROWS, D_XD, "in_proj_xd")
    up = _matmul_nn([(h1, wup)], F32, MM_ROWS, 1024, "in_proj_up")
    u = _conv_fwd(xd, w8, conv_b, tb, "conv_fwd")
    y, hprev, _, *landed = _ssd_fwd(u, xd, bias2, a2, rexp, hfin_c, "ssd_fwd", gather=late_shards or ())
    if late_shards is not None:
        wts = dict(wts, **{nme: _assemble(nme, t) for nme, t in zip(_LATE, landed)})
    wout = wts["w_out"]
    wg, wu, wd = wts["w_gate"], wts["w_up"], wts["w_down"]
    pw = wts["pool_w"]
    yn = _merge_fwd(y, u, z, dskip_e, gn, tb)
    pconst = _pool_consts(False)
    pool, dsave = _pool_fwd(up, pconst, pw, pscale)
    mix = _matmul_nn([(yn, wout[0:d]), (pool, wout[d:2 * d])], F32, MM_ROWS, 1024, "out_proj")
    x1, h2 = _res_ln(x0, mix, g1, ln1_g, ln1_b, sh2, sc2, tb)

    gate, upp, hmid = _swiglu_fwd(h2, wg, wu, 512, D_FF // 2)
    ffn = _matmul_nn([(hmid, wd)], F32, MM_ROWS, 1024, "ffn_down")
    dffn, dr2, acc2 = _final_ln_loss(x1, ffn, g2, ln2_g, ln2_b, target, tb)
    loss = (0.5 / d) * jnp.sum(acc2[3])

    dgate, dupp = _swiglu_bwd(dffn, wd.T, gate, upp, 512, D_FF // 2)
    g_wdown = _matmul_tn(hmid, dffn, MM_ROWS, 1024, "g_w_down")
    g_wgate = _matmul_tn(h2, dgate, MM_ROWS, 1408, "g_w_gate")
    g_wup = _matmul_tn(h2, dupp, MM_ROWS, 1408, "g_w_up")
    dh2 = _matmul_nn([(dgate, wg.T), (dupp, wu.T)], F32, 512, 1024, "d_h2")
    dmix, dr1, acc1 = _bwd_ln1(dr2, dh2, x1, x0, mix, g1, sc2, ln1_g, tb)

    dyn = _matmul_nn([(dmix, wout[0:d].T)], F32, MM_ROWS, 1024, "d_yn")
    dpool = _matmul_nn([(dmix, wout[d:2 * d].T)], F32, MM_ROWS, 1024, "d_pool")
    g_wout = jnp.concatenate([_matmul_tn(yn, dmix, MM_ROWS, 1024, "g_w_out_a"),
                              _matmul_tn(pool, dmix, MM_ROWS, 1024, "g_w_out_b")], axis=0)
    dd, dds, g_pw, accp = _pool_bwd_a(dpool, dsave, pw, jnp.swapaxes(pw, 1, 2), pscale)
    dup = _pool_bwd_b(dds, dd, _pool_consts(True))
    dy, dz, accm = _merge_bwd(dyn, y, u, z, dskip_e, gn, tb)
    lam0 = jnp.zeros((2, D_STATE, d), F32)
    late_grads = dict(w_out=g_wout, w_gate=g_wgate, w_up=g_wup, w_down=g_wdown, pool_w=g_pw)
    pairs = ()
    if late_shards is not None:
        slabs = [_to_slabs(nme, late_grads[nme]) for nme in _LATE]
        pairs = _pair_add(slabs, _pair_swap(slabs, "reduce_g_d2d_late"), core, "reduce_g_pair_late")
    dxs, dbc, ddt, accs, lam_c, *arrived = _ssd_bwd(u, xd, bias2, a2, rexp, rexp_t, dy, hprev, lam0, "ssd_bwd",
                                                    exchange=pairs)
    du, accb = _conv_bwd_a(dxs, dy, dskip_e, dbc, u, tb, "conv_bwd_a")
    dxd, accw = _conv_bwd_b(du, xd, ddt, w8, tb, "conv_bwd_b")

    lc = ctx.shape[0]
    zeros_c = jnp.zeros((lc, d), F32)
    dxs_c, dbc_c, ddt_c, accs_c, _ = _ssd_bwd(uc, xdc, bias2, a2, rexp, rexp_t, zeros_c, hprev_c, lam_c, "ssd_bwd_ctx")
    du_c, accb_c = _conv_bwd_a(dxs_c, zeros_c, dskip_e, dbc_c, uc, tbc, "conv_bwd_a_ctx")
    dxd_c, accw_c = _conv_bwd_b(du_c, xdc, ddt_c, w8, tbc, "conv_bwd_b_ctx")
    dhc = _matmul_nn([(dxd_c, wxd.T)], F32, 512, 1024, "d_hc")
    _, acc0c = _bwd_ln0(None, dhc, ctx, emb_g, emb_b, sc1c, tbc, "bwd_ln0_ctx")

    dh1 = _matmul_nn([(dz, wz.T), (dxd, wxd.T), (dup, wup.T)], F32, MM_ROWS, 1024, "d_h1")
    g_wz = _matmul_tn(h1, dz, MM_ROWS, 1024, "g_in_proj_z")
    g_wxd = _matmul_tn(h1, dxd, MM_ROWS, D_XD, "g_in_proj_xd") + _matmul_tn(hc, dxd_c, 512, D_XD, "g_in_proj_xd_ctx")
    g_wpo = _matmul_tn(h1, dup, MM_ROWS, 1024, "g_in_proj_up")
    g_win = jnp.concatenate([g_wz, g_wxd[:, 0:D_XBC + 2 * SSD_HEADS], g_wpo], axis=1)
    grad_x, acc0 = _bwd_ln0(dr1, dh1, x, emb_g, emb_b, sc1, tb, "bwd_ln0")

    zero_d = jnp.zeros((1, d), F32)
    dmod = jnp.concatenate([acc0[1:2], acc0[0:1], acc1[4:5], acc1[1:2], acc1[0:1], acc2[2:3]], axis=1)
    dmodc = jnp.concatenate([acc0c[1:2], acc0c[0:1]] + [zero_d] * 4, axis=1)

    big = dict(in_proj=g_win)
    if late_shards is None:
        big.update(late_grads)
    sml = dict(
        dmod=dmod, dmod_ctx=dmodc, emb_ln_g=acc0[2] + acc0c[2], emb_ln_b=acc0[3] + acc0c[3],
        conv_w=accw[0:D_CONV] + accw_c[0:D_CONV], conv_b=accb[0] + accb_c[0],
        dt_bias=accs[:, 0, 0:SSD_HEADS] + accs_c[:, 0, 0:SSD_HEADS],
        a_log=accs[:, 1, 0:SSD_HEADS] + accs_c[:, 1, 0:SSD_HEADS],
        d_skip=jnp.sum(accm[1].reshape(SSD_HEADS, HEAD_DIM), axis=1),
        ssd_norm_g=accm[0], pool_scale=accp[0], ln1_g=acc1[2], ln1_b=acc1[3], ln2_g=acc2[0], ln2_b=acc2[1])
    return loss, grad_x, big, sml, (arrived if late_shards is not None else None)


_SMALL = ("c_ctx", "emb_ln_g", "emb_ln_b", "b_ada", "conv_w", "conv_b", "dt_bias", "a_log", "d_skip",
          "ssd_norm_g", "pool_scale", "ln1_g", "ln1_b", "ln2_g", "ln2_b")


def _small_rows(size):
    return -(-size // 1024)


def _pack_small(vals, names):
    pieces, rows = [], 0
    for nme in names:
        flat = vals[nme].reshape(-1).astype(F32)
        nr = _small_rows(flat.shape[0])
        pieces.append(flat)
        if nr * 1024 > flat.shape[0]:
            pieces.append(jnp.zeros((nr * 1024 - flat.shape[0],), F32))
        rows += nr
    if rows % 8:
        pieces.append(jnp.zeros(((8 - rows % 8) * 1024,), F32))
    return jnp.concatenate(pieces).reshape(-1, 1024)


def _small_offsets(shapes, names):
    out, off = {}, 0
    for nme in names:
        nr = _small_rows(math.prod(shapes[nme]))
        out[nme] = (off, nr)
        off += nr
    return out


def _unpack_small(packed, shapes, names):
    out = {}
    for nme, (off, nr) in _small_offsets(shapes, names).items():
        out[nme] = packed[off:off + nr].reshape(-1)[:math.prod(shapes[nme])].reshape(shapes[nme])
    return out


_WEIGHT_ORDER = ("c_ctx", "emb_ln_g", "emb_ln_b", "w_ada", "b_ada", "in_proj", "conv_w", "conv_b", "dt_bias", "a_log",
                 "d_skip", "ssd_norm_g", "pool_w", "pool_scale", "w_out", "ln1_g", "ln1_b", "w_gate", "w_up", "w_down",
                 "ln2_g", "ln2_b")


def _as2d(a):
    return a.reshape(-1, a.shape[-1])


def kernel(x, c, ctx, c_ctx, emb_ln_g, emb_ln_b, w_ada, b_ada, in_proj, conv_w, conv_b, dt_bias, a_log, d_skip, ssd_norm_g, pool_w, pool_scale, w_out, ln1_g, ln1_b, w_gate, w_up, w_down, ln2_g, ln2_b, loss_target, m_c_ctx, m_emb_ln_g, m_emb_ln_b, m_w_ada, m_b_ada, m_in_proj, m_conv_w, m_conv_b, m_dt_bias, m_a_log, m_d_skip, m_ssd_norm_g, m_pool_w, m_pool_scale, m_w_out, m_ln1_g, m_ln1_b, m_w_gate, m_w_up, m_w_down, m_ln2_g, m_ln2_b, v_c_ctx, v_emb_ln_g, v_emb_ln_b, v_w_ada, v_b_ada, v_in_proj, v_conv_w, v_conv_b, v_dt_bias, v_a_log, v_d_skip, v_ssd_norm_g, v_pool_w, v_pool_scale, v_w_out, v_ln1_g, v_ln1_b, v_w_gate, v_w_up, v_w_down, v_ln2_g, v_ln2_b):
    w = dict(c_ctx=c_ctx, emb_ln_g=emb_ln_g, emb_ln_b=emb_ln_b, w_ada=w_ada, b_ada=b_ada, in_proj=in_proj, conv_w=conv_w,
             conv_b=conv_b, dt_bias=dt_bias, a_log=a_log, d_skip=d_skip, ssd_norm_g=ssd_norm_g, pool_w=pool_w,
             pool_scale=pool_scale, w_out=w_out, ln1_g=ln1_g, ln1_b=ln1_b, w_gate=w_gate, w_up=w_up, w_down=w_down,
             ln2_g=ln2_g, ln2_b=ln2_b)
    m = dict(c_ctx=m_c_ctx, emb_ln_g=m_emb_ln_g, emb_ln_b=m_emb_ln_b, w_ada=m_w_ada, b_ada=m_b_ada, in_proj=m_in_proj,
             conv_w=m_conv_w, conv_b=m_conv_b, dt_bias=m_dt_bias, a_log=m_a_log, d_skip=m_d_skip,
             ssd_norm_g=m_ssd_norm_g, pool_w=m_pool_w, pool_scale=m_pool_scale, w_out=m_w_out, ln1_g=m_ln1_g,
             ln1_b=m_ln1_b, w_gate=m_w_gate, w_up=m_w_up, w_down=m_w_down, ln2_g=m_ln2_g, ln2_b=m_ln2_b)
    v = dict(c_ctx=v_c_ctx, emb_ln_g=v_emb_ln_g, emb_ln_b=v_emb_ln_b, w_ada=v_w_ada, b_ada=v_b_ada, in_proj=v_in_proj,
             conv_w=v_conv_w, conv_b=v_conv_b, dt_bias=v_dt_bias, a_log=v_a_log, d_skip=v_d_skip,
             ssd_norm_g=v_ssd_norm_g, pool_w=v_pool_w, pool_scale=v_pool_scale, w_out=v_w_out, ln1_g=v_ln1_g,
             ln1_b=v_ln1_b, w_gate=v_w_gate, w_up=v_w_up, w_down=v_w_down, ln2_g=v_ln2_g, ln2_b=v_ln2_b)

    xi, yi, ci = _mesh_pos()
    chip = 2 * xi + yi

    dev = 4 * xi + 2 * yi + ci
    d = D_MODEL
    core = ci.reshape(1).astype(jnp.int32)

    crow = jnp.concatenate([c.reshape(1, d), jnp.zeros((7, d), F32)], axis=0)
    _, c_all = _allreduce_small(crow, "gather_c")
    c16 = jnp.concatenate([c_all[:, 0, :], c_ctx.reshape(1, d), jnp.zeros((MOD_ROWS - 9, d), F32)], axis=0)
    ncol = w_ada.shape[-1]
    wada_bf = w_ada[0].astype(BF16)
    b_mine = lax.dynamic_slice_in_dim(b_ada, chip * ncol, ncol, axis=1)
    mods4 = _chip_bcast(_mods_fwd(c16, wada_bf, b_mine), "gather_mods")
    mods = mods4.transpose(1, 0, 2).reshape(MOD_ROWS, 4 * ncol)
    mod = lax.dynamic_slice_in_dim(mods, dev, 1, axis=0)
    mod_ctx = mods[8:9]

    shard = {name: w[name][0].astype(BF16).reshape(shp) for name, shp, _ in _BIG}
    conv8 = jnp.concatenate([conv_w[0], jnp.zeros((8 - D_CONV, conv_w.shape[-1]), F32)], axis=0)
    *gathered, conv4 = _gather_weights([shard[nme] for nme in _EARLY], conv8)
    wts = {nme: _assemble(nme, t) for nme, t in zip(_EARLY, gathered)}
    w8 = conv4.transpose(1, 0, 2).reshape(8, D_XBC)
    small = {nme: (w[nme] if nme in ("c_ctx", "emb_ln_g", "emb_ln_b") else w[nme][0]) for nme in _SMALL if nme != "conv_w"}

    loss, grad_x, big, sml, late_parts = _device_step(x[0], mod, mod_ctx, ctx[0], loss_target[0], wts, w8, small, 512,
                                                      late_shards=[shard[nme] for nme in _LATE], core=core)
    loss = lax.psum(loss, ("x", "y", "c"))

    g_big = _reduce_grads(big, late_parts, core)
    reduced = tuple(sml)
    small_shapes = {nme: sml[nme].shape for nme in reduced}
    total, each = _allreduce_small(_pack_small(sml, reduced), "reduce_small")
    g_small = _unpack_small(total, small_shapes, reduced)
    cw_cols = conv_w.shape[-1]
    g_small["conv_w"] = lax.dynamic_slice_in_dim(g_small["conv_w"], chip * cw_cols, cw_cols, axis=1)

    off, nr = _small_offsets(small_shapes, reduced)["dmod"]
    dm16 = jnp.concatenate([each[:, off:off + nr, :].reshape(8, nr * 1024)[:, :6 * d], g_small["dmod_ctx"],
                            jnp.zeros((MOD_ROWS - 9, 6 * d), F32)], axis=0)
    dm_mine = lax.dynamic_slice_in_dim(dm16, chip * ncol, ncol, axis=1)
    g_wada = _mods_bwd_w(c16.T, dm_mine)
    g_small["b_ada"] = _mods_bwd_b(dm16)[0:1]
    c_part = _mods_bwd_c(dm_mine, wada_bf, c16)[8:16]
    g_small["c_ctx"] = _allreduce_small(c_part, "reduce_c_ctx")[0][0]

    grads, delta, new_m, new_v = {}, {}, {}, {}
    grads["w_ada"] = g_wada[None]
    delta["w_ada"], new_m["w_ada"], new_v["w_ada"] = (
        t[None] for t in _adamw(w_ada[0], g_wada, m_w_ada[0], v_w_ada[0], "adamw_w_ada"))
    for name, _, _ in _BIG:
        g2 = _as2d(g_big[name])
        d2, m2, v2 = _adamw(_as2d(w[name][0]), g2, _as2d(m[name][0]), _as2d(v[name][0]), "adamw_" + name)
        grads[name] = g2.reshape(w[name].shape)
        delta[name], new_m[name], new_v[name] = (t.reshape(w[name].shape) for t in (d2, m2, v2))
    shp = {nme: w[nme].shape for nme in _SMALL}
    gp = _pack_small(g_small, _SMALL)
    dp, mp, vp = _adamw(_pack_small(w, _SMALL), gp, _pack_small(m, _SMALL), _pack_small(v, _SMALL), "adamw_small")
    for dst, src in ((grads, gp), (delta, dp), (new_m, mp), (new_v, vp)):
        dst.update(_unpack_small(src, shp, _SMALL))

    return (loss, grad_x[None], *[grads[nme] for nme in _WEIGHT_ORDER], *[delta[nme] for nme in _WEIGHT_ORDER],
            *[new_m[nme] for nme in _WEIGHT_ORDER], *[new_v[nme] for nme in _WEIGHT_ORDER])
```

```python
import functools
import math

import jax
import jax.numpy as jnp
from jax import lax
from jax.experimental import pallas as pl
from jax.experimental.pallas import tpu as pltpu

F32 = jnp.float32
BF16 = jnp.bfloat16
MESH = pl.DeviceIdType.MESH

D_MODEL = 1024
SSD_HEADS = 16
HEAD_DIM = 64
D_STATE = 128
CHUNK = 128
D_CONV = 5
D_XBC = D_MODEL + 2 * D_STATE
D_XD = 1408
N_POOL = 4
POOL_DIM = 256
POOL_WINDOWS = (2, 4, 8, 16)
GRID_W = 64
D_FF = 2816
D_IN_PROJ = 3360
LN_EPS = 1e-5
ALPHA = 2.0 ** 0.25
POOL_TB = 512
MM_ROWS = 1024

ADAM_LR = 0.001
ADAM_B1 = 0.9
ADAM_B2 = 0.999
ADAM_EPS = 1e-08
ADAM_WD = 0.01
ADAM_STEP = 10

VMEM_LIMIT = 56 * 1024 * 1024


def _cp(sem=None):
    return pltpu.CompilerParams(dimension_semantics=sem, vmem_limit_bytes=VMEM_LIMIT)


def _sigmoid(x):
    return 1.0 / (1.0 + jnp.exp(-x))


def _silu(x):
    return x * _sigmoid(x)


def _dsilu(x):
    s = _sigmoid(x)
    return s * (1.0 + x * (1.0 - s))


def _softplus(x):
    t = jnp.exp(-jnp.abs(x))
    u = 1.0 + t
    log1p = jnp.where(u == 1.0, t, jnp.log(u) * t / (u - 1.0 + (u == 1.0)))
    return jnp.maximum(x, 0.0) + log1p


def _split(x, n):
    parts, r = [], x
    for _ in range(n):
        p = r.astype(BF16)
        parts.append(p)
        r = r - p.astype(F32)
    return parts


def _dot(a, b):
    return jnp.dot(a, b, preferred_element_type=F32)


def _dot_nt(a, b):
    return lax.dot_general(a, b, (((1,), (1,)), ((), ())), preferred_element_type=F32)


def _dot_tn(a, b):
    return lax.dot_general(a, b, (((0,), (0,)), ((), ())), preferred_element_type=F32)


def _dot_sel_l(sel_bf, x, n=3):
    out = None
    for p in _split(x, n):
        t = _dot(sel_bf, p)
        out = t if out is None else out + t
    return out


def _dot_sel_r(x, sel_bf, n=3):
    out = None
    for p in _split(x, n):
        t = _dot(p, sel_bf)
        out = t if out is None else out + t
    return out


ROW_SUB = 16


def _row_tiles(tb):
    assert tb % ROW_SUB == 0
    return [slice(s * ROW_SUB, (s + 1) * ROW_SUB) for s in range(tb // ROW_SUB)]


def _fold8(v):
    out = v[0:8, :]
    for r in range(8, v.shape[0], 8):
        out = out + v[r:r + 8, :]
    return out


def _row_block(n, cap=256, mult=8):
    best = None
    for t in range(mult, min(n, cap) + 1, mult):
        if n % t == 0:
            best = t
    return best if best is not None else n


def _vec(v):
    return v.reshape(1, -1).astype(F32)


MOD_ROWS = 16
MOD_TN = 512


def _mods_fwd(c16, w_bf, b):
    r, d = c16.shape
    n = w_bf.shape[1]

    def body(c_ref, w_ref, b_ref, o_ref):
        s = _silu(c_ref[...]).astype(BF16)
        o_ref[...] = _dot(s, w_ref[...]) + b_ref[...]

    return pl.pallas_call(
        body, name="mods_fwd", grid=(n // MOD_TN,),
        in_specs=[pl.BlockSpec((r, d), lambda j: (0, 0)),
                  pl.BlockSpec((d, MOD_TN), lambda j: (0, j)),
                  pl.BlockSpec((1, MOD_TN), lambda j: (0, j))],
        out_specs=pl.BlockSpec((r, MOD_TN), lambda j: (0, j)),
        out_shape=jax.ShapeDtypeStruct((r, n), F32),
        compiler_params=_cp(("arbitrary",)),
    )(c16, w_bf, b)


def _mods_bwd_w(ct16, dm16):
    d = ct16.shape[0]
    n = dm16.shape[1]

    def body(ct_ref, dm_ref, dw_ref):
        s = _silu(ct_ref[...])
        dm = dm_ref[...]
        acc = s[:, 0:1] * dm[0:1, :]
        for r in range(1, 9):
            acc = acc + s[:, r:r + 1] * dm[r:r + 1, :]
        dw_ref[...] = acc

    return pl.pallas_call(
        body, name="mods_bwd_w", grid=(n // MOD_TN,),
        in_specs=[pl.BlockSpec((d, MOD_ROWS), lambda j: (0, 0)),
                  pl.BlockSpec((MOD_ROWS, MOD_TN), lambda j: (0, j))],
        out_specs=pl.BlockSpec((d, MOD_TN), lambda j: (0, j)),
        out_shape=jax.ShapeDtypeStruct((d, n), F32),
        compiler_params=_cp(("arbitrary",)),
    )(ct16, dm16)


def _mods_bwd_c(dm16, w_bf, c16):
    d = c16.shape[1]
    n = dm16.shape[1]
    nk = n // MOD_TN

    def body(dm_ref, w_ref, c_ref, o_ref):
        k = pl.program_id(0)

        @pl.when(k == 0)
        def _():
            o_ref[...] = jnp.zeros_like(o_ref)

        o_ref[...] += _dot_nt(dm_ref[...].astype(BF16), w_ref[...])

        @pl.when(k == nk - 1)
        def _():
            o_ref[...] = o_ref[...] * (0.5 * _dsilu(c_ref[...]))

    return pl.pallas_call(
        body, name="mods_bwd_c", grid=(nk,),
        in_specs=[pl.BlockSpec((MOD_ROWS, MOD_TN), lambda k: (0, k)),
                  pl.BlockSpec((d, MOD_TN), lambda k: (0, k)),
                  pl.BlockSpec((MOD_ROWS, d), lambda k: (0, 0))],
        out_specs=pl.BlockSpec((MOD_ROWS, d), lambda k: (0, 0)),
        out_shape=jax.ShapeDtypeStruct((MOD_ROWS, d), F32),
        compiler_params=_cp(("arbitrary",)),
    )(dm16, w_bf, c16)


def _mods_bwd_b(dm16):
    n = dm16.shape[1]

    def body(dm_ref, o_ref):
        dm = dm_ref[...]
        acc = dm[0:1, :]
        for r in range(1, 9):
            acc = acc + dm[r:r + 1, :]
        o_ref[...] = jnp.broadcast_to(acc, (8, MOD_TN))

    return pl.pallas_call(
        body, name="mods_bwd_b", grid=(n // MOD_TN,),
        in_specs=[pl.BlockSpec((MOD_ROWS, MOD_TN), lambda j: (0, j))],
        out_specs=pl.BlockSpec((8, MOD_TN), lambda j: (0, j)),
        out_shape=jax.ShapeDtypeStruct((8, n), F32),
        compiler_params=_cp(("arbitrary",)),
    )(dm16)


def _ln_stats(x):
    mu = jnp.mean(x, axis=-1, keepdims=True)
    xc = x - mu
    var = jnp.mean(xc * xc, axis=-1, keepdims=True)
    rstd = lax.rsqrt(var + LN_EPS)
    return xc * rstd, rstd


def _ln_bwd(dxhat, xhat, rstd):
    m1 = jnp.mean(dxhat, axis=-1, keepdims=True)
    m2 = jnp.mean(dxhat * xhat, axis=-1, keepdims=True)
    return rstd * (dxhat - m1 - xhat * m2)


def _row_spec(tb, d):
    return pl.BlockSpec((tb, d), lambda i: (i, 0))


def _par_spec(d):
    return pl.BlockSpec((1, d), lambda i: (0, 0))


def _acc_spec(d):
    return pl.BlockSpec((8, d), lambda i: (0, 0))


def _ln_mod(x, g, b, sh, sc, tb, name):
    n, d = x.shape

    def body(x_ref, g_ref, b_ref, sh_ref, sc_ref, x0_ref, h_ref):
        g, b, sh, sc1 = g_ref[...], b_ref[...], sh_ref[...], 1.0 + sc_ref[...]
        for r in _row_tiles(min(tb, n)):
            xhat, _ = _ln_stats(x_ref[r, :])
            x0 = xhat * g + b
            x0_ref[r, :] = x0
            h_ref[r, :] = (x0 * sc1 + sh).astype(BF16)

    return pl.pallas_call(
        body, name=name, grid=(n // tb,),
        in_specs=[_row_spec(tb, d)] + [_par_spec(d)] * 4,
        out_specs=[_row_spec(tb, d), _row_spec(tb, d)],
        out_shape=[jax.ShapeDtypeStruct((n, d), F32), jax.ShapeDtypeStruct((n, d), BF16)],
        compiler_params=_cp(("parallel",)),
    )(x, g, b, sh, sc)


def _res_ln(xres, mix, gate, g, b, sh, sc, tb):
    n, d = xres.shape

    def body(xr_ref, mix_ref, gate_ref, g_ref, b_ref, sh_ref, sc_ref, x1_ref, h_ref):
        gate_v, g, b, sh, sc1 = gate_ref[...], g_ref[...], b_ref[...], sh_ref[...], 1.0 + sc_ref[...]
        for r in _row_tiles(tb):
            xhat, _ = _ln_stats(ALPHA * xr_ref[r, :] + gate_v * mix_ref[r, :])
            x1 = xhat * g + b
            x1_ref[r, :] = x1
            h_ref[r, :] = (x1 * sc1 + sh).astype(BF16)

    return pl.pallas_call(
        body, name="res_ln1", grid=(n // tb,),
        in_specs=[_row_spec(tb, d)] * 2 + [_par_spec(d)] * 5,
        out_specs=[_row_spec(tb, d), _row_spec(tb, d)],
        out_shape=[jax.ShapeDtypeStruct((n, d), F32), jax.ShapeDtypeStruct((n, d), BF16)],
        compiler_params=_cp(("parallel",)),
    )(xres, mix, gate, g, b, sh, sc)


def _final_ln_loss(x1, ffn, gate, g, b, target, tb):
    n, d = x1.shape

    def body(x1_ref, ffn_ref, gate_ref, g_ref, b_ref, t_ref, dffn_ref, dr_ref, acc_ref):
        i = pl.program_id(0)

        @pl.when(i == 0)
        def _():
            acc_ref[...] = jnp.zeros_like(acc_ref)

        gate_v, g, b = gate_ref[...], g_ref[...], b_ref[...]
        parts = [jnp.zeros((8, d), F32)] * 4
        for r in _row_tiles(tb):
            ffn = ffn_ref[r, :]
            xhat, rstd = _ln_stats(ALPHA * x1_ref[r, :] + gate_v * ffn)
            err = xhat * g + b - t_ref[r, :]
            dx2 = err * (1.0 / d)
            dr = _ln_bwd(dx2 * g, xhat, rstd)
            dr_ref[r, :] = dr
            dffn_ref[r, :] = (gate_v * dr).astype(BF16)
            terms = (dx2 * xhat, dx2, dr * ffn, err * err)
            parts = [p + _fold8(t) for p, t in zip(parts, terms)]
        for j, p in enumerate(parts):
            acc_ref[j:j + 1, :] += jnp.sum(p, axis=0, keepdims=True)

    return pl.pallas_call(
        body, name="final_ln_loss", grid=(n // tb,),
        in_specs=[_row_spec(tb, d)] * 2 + [_par_spec(d)] * 3 + [_row_spec(tb, d)],
        out_specs=[_row_spec(tb, d), _row_spec(tb, d), _acc_spec(d)],
        out_shape=[jax.ShapeDtypeStruct((n, d), BF16), jax.ShapeDtypeStruct((n, d), F32),
                   jax.ShapeDtypeStruct((8, d), F32)],
        compiler_params=_cp(("arbitrary",)),
    )(x1, ffn, gate, g, b, target)


def _bwd_ln1(dr2, dh2, x1, x0, mix, gate, sc2, g, tb):
    n, d = x1.shape

    def body(dr2_ref, dh2_ref, x1_ref, x0_ref, mix_ref, gate_ref, sc_ref, g_ref, dmix_ref, dr1_ref, acc_ref):
        i = pl.program_id(0)

        @pl.when(i == 0)
        def _():
            acc_ref[...] = jnp.zeros_like(acc_ref)

        gate_v, g, sc1 = gate_ref[...], g_ref[...], 1.0 + sc_ref[...]
        parts = [jnp.zeros((8, d), F32)] * 5
        for r in _row_tiles(tb):
            dh2 = dh2_ref[r, :]
            mix = mix_ref[r, :]
            dx1 = ALPHA * dr2_ref[r, :] + dh2 * sc1
            xhat, rstd = _ln_stats(ALPHA * x0_ref[r, :] + gate_v * mix)
            dr1 = _ln_bwd(dx1 * g, xhat, rstd)
            dr1_ref[r, :] = dr1
            dmix_ref[r, :] = (gate_v * dr1).astype(BF16)
            terms = (dh2 * x1_ref[r, :], dh2, dx1 * xhat, dx1, dr1 * mix)
            parts = [p + _fold8(t) for p, t in zip(parts, terms)]
        for j, p in enumerate(parts):
            acc_ref[j:j + 1, :] += jnp.sum(p, axis=0, keepdims=True)

    return pl.pallas_call(
        body, name="bwd_ln1", grid=(n // tb,),
        in_specs=[_row_spec(tb, d)] * 5 + [_par_spec(d)] * 3,
        out_specs=[_row_spec(tb, d), _row_spec(tb, d), _acc_spec(d)],
        out_shape=[jax.ShapeDtypeStruct((n, d), BF16), jax.ShapeDtypeStruct((n, d), F32),
                   jax.ShapeDtypeStruct((8, d), F32)],
        compiler_params=_cp(("arbitrary",)),
    )(dr2, dh2, x1, x0, mix, gate, sc2, g)


def _bwd_ln0(dres, dh, x, g, b, sc, tb, name):
    n, d = x.shape
    has_res = dres is not None

    def body(*refs):
        if has_res:
            dres_ref, dh_ref, x_ref, g_ref, b_ref, sc_ref, dx_ref, acc_ref = refs
        else:
            dh_ref, x_ref, g_ref, b_ref, sc_ref, dx_ref, acc_ref = refs
        i = pl.program_id(0)

        @pl.when(i == 0)
        def _():
            acc_ref[...] = jnp.zeros_like(acc_ref)

        g, b, sc1 = g_ref[...], b_ref[...], 1.0 + sc_ref[...]
        parts = [jnp.zeros((8, d), F32)] * 4
        for r in _row_tiles(tb):
            dh = dh_ref[r, :]
            xhat, rstd = _ln_stats(x_ref[r, :])
            x0 = xhat * g + b
            dx0 = dh * sc1
            if has_res:
                dx0 = dx0 + ALPHA * dres_ref[r, :]
            dx_ref[r, :] = _ln_bwd(dx0 * g, xhat, rstd)
            terms = (dh * x0, dh, dx0 * xhat, dx0)
            parts = [p + _fold8(t) for p, t in zip(parts, terms)]
        for j, p in enumerate(parts):
            acc_ref[j:j + 1, :] += jnp.sum(p, axis=0, keepdims=True)

    ins = ([dres] if has_res else []) + [dh, x, g, b, sc]
    return pl.pallas_call(
        body, name=name, grid=(n // tb,),
        in_specs=[_row_spec(tb, d)] * (3 if has_res else 2) + [_par_spec(d)] * 3,
        out_specs=[_row_spec(tb, d), _acc_spec(d)],
        out_shape=[jax.ShapeDtypeStruct((n, d), F32), jax.ShapeDtypeStruct((8, d), F32)],
        compiler_params=_cp(("arbitrary",)),
    )(*ins)


def _matmul_nn(pairs, out_dtype, tm, tn, name):
    m = pairs[0][0].shape[0]
    n = pairs[0][1].shape[1]
    tm = min(tm, m)
    tn = min(tn, n)
    npair = len(pairs)

    def body(*refs):
        o_ref = refs[-1]
        acc = None
        for p in range(npair):
            t = _dot(refs[2 * p][...].astype(BF16), refs[2 * p + 1][...])
            acc = t if acc is None else acc + t
        o_ref[...] = acc.astype(out_dtype)

    in_specs, args = [], []
    for a, b in pairs:
        k = a.shape[1]
        in_specs += [pl.BlockSpec((tm, k), lambda i, j: (i, 0)), pl.BlockSpec((k, tn), lambda i, j: (0, j))]
        args += [a, b]
    return pl.pallas_call(
        body, name=name, grid=(m // tm, n // tn),
        in_specs=in_specs,
        out_specs=pl.BlockSpec((tm, tn), lambda i, j: (i, j)),
        out_shape=jax.ShapeDtypeStruct((m, n), out_dtype),
        compiler_params=_cp(("parallel", "arbitrary")),
    )(*args)


def _matmul_tn(a, g, tm, tn, name):
    m, k = a.shape
    n = g.shape[1]
    tm = min(tm, m)
    tn = min(tn, n)

    def body(a_ref, g_ref, o_ref):
        i = pl.program_id(1)

        @pl.when(i == 0)
        def _():
            o_ref[...] = jnp.zeros_like(o_ref)

        o_ref[...] += _dot_tn(a_ref[...].astype(BF16), g_ref[...].astype(BF16))

    return pl.pallas_call(
        body, name=name, grid=(n // tn, m // tm),
        in_specs=[pl.BlockSpec((tm, k), lambda j, i: (i, 0)), pl.BlockSpec((tm, tn), lambda j, i: (i, j))],
        out_specs=pl.BlockSpec((k, tn), lambda j, i: (0, j)),
        out_shape=jax.ShapeDtypeStruct((k, n), F32),
        compiler_params=_cp(("parallel", "arbitrary")),
    )(a, g)


def _swiglu_fwd(h, wg, wu, tm, tn):
    m, k = h.shape
    n = wg.shape[1]
    tm = min(tm, m)

    def body(h_ref, wg_ref, wu_ref, gate_ref, up_ref, hmid_ref):
        hv = h_ref[...]
        gate = _dot(hv, wg_ref[...])
        up = _dot(hv, wu_ref[...])
        gate_ref[...] = gate.astype(BF16)
        up_ref[...] = up.astype(BF16)
        hmid_ref[...] = (_silu(gate) * up).astype(BF16)

    blk = pl.BlockSpec((tm, tn), lambda i, j: (i, j))
    wspec = pl.BlockSpec((k, tn), lambda i, j: (0, j))
    return pl.pallas_call(
        body, name="swiglu_fwd", grid=(m // tm, n // tn),
        in_specs=[pl.BlockSpec((tm, k), lambda i, j: (i, 0)), wspec, wspec],
        out_specs=[blk, blk, blk],
        out_shape=[jax.ShapeDtypeStruct((m, n), BF16), jax.ShapeDtypeStruct((m, n), BF16),
                   jax.ShapeDtypeStruct((m, n), BF16)],
        compiler_params=_cp(("parallel", "arbitrary")),
    )(h, wg, wu)


def _swiglu_bwd(dffn, wdt, gate, up, tm, tn):
    m, k = dffn.shape
    n = wdt.shape[1]
    tm = min(tm, m)

    def body(d_ref, w_ref, gate_ref, up_ref, dg_ref, du_ref):
        dh = _dot(d_ref[...], w_ref[...])
        gate = gate_ref[...].astype(F32)
        dg_ref[...] = (dh * up_ref[...].astype(F32) * _dsilu(gate)).astype(BF16)
        du_ref[...] = (dh * _silu(gate)).astype(BF16)

    blk = pl.BlockSpec((tm, tn), lambda i, j: (i, j))
    return pl.pallas_call(
        body, name="swiglu_bwd", grid=(m // tm, n // tn),
        in_specs=[pl.BlockSpec((tm, k), lambda i, j: (i, 0)), pl.BlockSpec((k, tn), lambda i, j: (0, j)), blk, blk],
        out_specs=[blk, blk],
        out_shape=[jax.ShapeDtypeStruct((m, n), BF16), jax.ShapeDtypeStruct((m, n), BF16)],
        compiler_params=_cp(("parallel", "arbitrary")),
    )(dffn, wdt, gate, up)


def _halo_specs(tb, width, nrows):
    r8 = tb // 8
    last = nrows // 8 - 1
    prev = pl.BlockSpec((8, width), lambda i: (jnp.maximum(i * r8 - 1, 0), 0))
    nxt = pl.BlockSpec((8, width), lambda i: (jnp.minimum((i + 1) * r8, last), 0))
    return prev, nxt


CONV_SUB = 32


def _halo_scratch():
    return [pltpu.VMEM((CONV_SUB + 16, D_XBC), F32), pltpu.VMEM((CONV_SUB + 16, D_XBC), F32)]


def _shifted_rows(prev_ref, cur_ref, next_ref, top, bot, tb, i, nb):
    sub = CONV_SUB
    nsub = tb // sub
    assert nsub >= 2
    top[0:8, :] = prev_ref[...] * (i > 0).astype(F32)
    top[8:sub + 16, :] = cur_ref[0:sub + 8, :]
    bot[0:sub + 8, :] = cur_ref[tb - sub - 8:tb, :]
    bot[sub + 8:sub + 16, :] = next_ref[...] * (i < nb - 1).astype(F32)

    def rows(s, o):
        if s == 0:
            return top[8 + o:8 + o + sub, :]
        if s == nsub - 1:
            return bot[8 + o:8 + o + sub, :]
        return cur_ref[s * sub + o:(s + 1) * sub + o, :]

    return rows


def _conv_fwd(xd, w8, b, tb, name):
    n = xd.shape[0]
    tb = min(tb, n)
    nb = n // tb
    prev, nxt = _halo_specs(tb, D_XBC, n)

    def body(p_ref, c_ref, n_ref, w_ref, b_ref, u_ref, top, bot):
        i = pl.program_id(0)
        rows = _shifted_rows(p_ref, c_ref, n_ref, top, bot, tb, i, nb)
        w = [w_ref[k:k + 1, :] for k in range(D_CONV)]
        bias = jnp.broadcast_to(b_ref[...], (CONV_SUB, D_XBC))
        for s in range(tb // CONV_SUB):
            acc = bias
            for k in range(D_CONV):
                acc = acc + w[k] * rows(s, k - 2)
            u_ref[s * CONV_SUB:(s + 1) * CONV_SUB, :] = acc

    return pl.pallas_call(
        body, name=name, grid=(nb,),
        in_specs=[prev, pl.BlockSpec((tb, D_XBC), lambda i: (i, 0)), nxt,
                  pl.BlockSpec((8, D_XBC), lambda i: (0, 0)), _par_spec(D_XBC)],
        out_specs=_row_spec(tb, D_XBC),
        out_shape=jax.ShapeDtypeStruct((n, D_XBC), F32),
        scratch_shapes=_halo_scratch(),
        compiler_params=_cp(("parallel",)),
    )(xd, xd, xd, w8, b)


def _conv_bwd_a(dxs, dy, dskip_e, dbc, u, tb, name):
    n = u.shape[0]
    tb = min(tb, n)

    def body(dxs_ref, dy_ref, sk_ref, dbc_ref, u_ref, du_ref, acc_ref):
        i = pl.program_id(0)

        @pl.when(i == 0)
        def _():
            acc_ref[...] = jnp.zeros_like(acc_ref)

        sk = sk_ref[...]
        part = jnp.zeros((8, D_XBC), F32)
        for r in _row_tiles(tb):
            gx = dxs_ref[0, r, :] + dxs_ref[1, r, :] + dy_ref[r, :] * sk
            gbc = dbc_ref[0, r, :] + dbc_ref[1, r, :]
            du = jnp.concatenate([gx, gbc], axis=1) * _dsilu(u_ref[r, :])
            du_ref[r, :] = du
            part = part + _fold8(du)
        acc_ref[0:1, :] += jnp.sum(part, axis=0, keepdims=True)

    return pl.pallas_call(
        body, name=name, grid=(n // tb,),
        in_specs=[pl.BlockSpec((2, tb, D_MODEL), lambda i: (0, i, 0)), _row_spec(tb, D_MODEL), _par_spec(D_MODEL),
                  pl.BlockSpec((2, tb, 2 * D_STATE), lambda i: (0, i, 0)), _row_spec(tb, D_XBC)],
        out_specs=[_row_spec(tb, D_XBC), _acc_spec(D_XBC)],
        out_shape=[jax.ShapeDtypeStruct((n, D_XBC), F32), jax.ShapeDtypeStruct((8, D_XBC), F32)],
        compiler_params=_cp(("arbitrary",)),
    )(dxs, dy, dskip_e, dbc, u)


def _conv_bwd_b(du, xd, ddt, w8, tb, name):
    n = du.shape[0]
    tb = min(tb, n)
    nb = n // tb
    prev, nxt = _halo_specs(tb, D_XBC, n)

    def body(dp_ref, dc_ref, dn_ref, xp_ref, xc_ref, xn_ref, ddt_ref, w_ref, dxd_ref, acc_ref, dtop, dbot, xtop, xbot):
        i = pl.program_id(0)

        @pl.when(i == 0)
        def _():
            acc_ref[...] = jnp.zeros_like(acc_ref)

        sub = CONV_SUB
        nsub = tb // sub
        du_rows = _shifted_rows(dp_ref, dc_ref, dn_ref, dtop, dbot, tb, i, nb)
        x_rows = _shifted_rows(xp_ref, xc_ref, xn_ref, xtop, xbot, tb, i, nb)
        w = [w_ref[k:k + 1, :] for k in range(D_CONV)]
        for s in range(nsub):
            acc = w[0] * du_rows(s, 2)
            for k in range(1, D_CONV):
                acc = acc + w[k] * du_rows(s, 2 - k)
            dxd_ref[s * sub:(s + 1) * sub, 0:D_XBC] = acc.astype(BF16)
        for k in range(D_CONV):
            part = jnp.zeros((8, D_XBC), F32)
            for s in range(nsub):
                prod = dc_ref[s * sub:(s + 1) * sub, :] * x_rows(s, k - 2)
                for r in range(0, sub, 8):
                    part = part + prod[r:r + 8, :]
            acc_ref[k:k + 1, :] += jnp.sum(part, axis=0, keepdims=True)
        ddt = ddt_ref[0] + pltpu.roll(ddt_ref[1], SSD_HEADS, 1)
        dxd_ref[:, D_XBC:D_XD] = ddt.astype(BF16)

    cur = pl.BlockSpec((tb, D_XBC), lambda i: (i, 0))
    return pl.pallas_call(
        body, name=name, grid=(nb,),
        in_specs=[prev, cur, nxt, prev, cur, nxt,
                  pl.BlockSpec((2, tb, 128), lambda i: (0, i, 0)), pl.BlockSpec((8, D_XBC), lambda i: (0, 0))],
        out_specs=[_row_spec(tb, D_XD), _acc_spec(D_XBC)],
        out_shape=[jax.ShapeDtypeStruct((n, D_XD), BF16), jax.ShapeDtypeStruct((8, D_XBC), F32)],
        scratch_shapes=_halo_scratch() + _halo_scratch(),
        compiler_params=_cp(("arbitrary",)),
    )(du, du, du, xd, xd, xd, ddt, w8)


def _ssd_chunk_index(nc, reverse):
    def idx(d, k):
        kk = (nc - 1 - k) if reverse else k
        return kk + d * (nc - 1 - 2 * kk)
    return idx


def _ssd_prologue(d, u_ref, xd_ref, bias_ref, a_ref, r_ref):
    q = CHUNK
    xbc = _silu(u_ref[...])
    xs = xbc[:, 0:D_MODEL]
    bm = xbc[:, D_MODEL:D_MODEL + D_STATE]
    cm = xbc[:, D_MODEL + D_STATE:D_XBC]
    row = lax.broadcasted_iota(jnp.int32, (q, q), 0)
    col = lax.broadcasted_iota(jnp.int32, (q, q), 1)
    sgn = 1 - 2 * d
    mask = ((row - col) * sgn) >= 0
    mask_t = ((row - col) * sgn) <= 0
    xdv = xd_ref[...]
    dtraw = jnp.where(d == 0, xdv, pltpu.roll(xdv, 128 - SSD_HEADS, 1)) + bias_ref[...]
    head_lane = col < SSD_HEADS
    dt = jnp.where(head_lane, _softplus(dtraw), 0.0)
    a = a_ref[...]
    tri = jnp.where(mask, 1.0, 0.0).astype(BF16)
    acum = _dot_sel_l(tri, dt * a)
    rexp = r_ref[...]
    alast = jnp.where(d == 0, acum[q - 1:q, :], acum[0:1, :])
    e16 = jnp.exp(acum)
    dend16 = jnp.exp(alast - acum)
    wend16 = dend16 * dt
    e = _dot_sel_r(e16, rexp, n=2)
    wend_e = _dot_sel_r(wend16, rexp, n=2)
    elast_e = jnp.where(d == 0, e[q - 1:q, :], e[0:1, :])
    g = _dot_nt(cm.astype(BF16), bm.astype(BF16))
    return dict(xs=xs, bm=bm, cm=cm, mask=mask, mask_t=mask_t, dtraw=dtraw, head_lane=head_lane, dt=dt, a=a,
                acum=acum, acum_t=acum.T, dt_t=dt.T, e16=e16, dend16=dend16, wend16=wend16, e=e, wend_e=wend_e,
                elast_e=elast_e, g=g, col=col, row=row)


def _ssd_head_mats(p, h):
    seg = p["acum"][:, h:h + 1] - p["acum_t"][h:h + 1, :]
    lm = jnp.exp(jnp.where(p["mask"], seg, -jnp.inf))
    gl = p["g"] * lm
    s = gl * p["dt_t"][h:h + 1, :]
    return lm, gl, s


def _ssd_fwd(u, xd, bias2, a2, rexp, h0, name, gather=()):
    n = u.shape[0]
    nc = n // CHUNK
    q = CHUNK
    cidx = _ssd_chunk_index(nc, reverse=False)
    ng = len(gather)

    def body(u_ref, xd_ref, bias_ref, a_ref, r_ref, h0_ref, *rest):
        g_ins, (y_ref, hp_ref, hf_ref), rest = rest[:ng], rest[ng:ng + 3], rest[ng + 3:]
        g_outs, st, sems = rest[:ng], rest[ng], rest[ng + 1:]
        d = pl.program_id(0)
        k = pl.program_id(1)
        if ng:
            g_start, g_finish = _gather_steps(g_ins, g_outs, *sems)
            pl.when((d == 0) & (k == 0))(g_start)

        @pl.when(k == 0)
        def _():
            st[...] = h0_ref[...]

        p = _ssd_prologue(d, u_ref, xd_ref, bias_ref, a_ref, r_ref)
        stv = st[...]
        st_bf = stv.astype(BF16)
        hp_ref[...] = st_bf
        xs = p["xs"]
        lane128 = p["col"]
        y_off = _dot(p["cm"].astype(BF16), st_bf) * p["e"]
        for pb in range(SSD_HEADS // 2):
            _, _, s0 = _ssd_head_mats(p, 2 * pb)
            _, _, s1 = _ssd_head_mats(p, 2 * pb + 1)
            xp = xs[:, pb * 128:(pb + 1) * 128]
            rhs = jnp.concatenate([jnp.where(lane128 < HEAD_DIM, xp, 0.0), jnp.where(lane128 >= HEAD_DIM, xp, 0.0)],
                                  axis=0).astype(BF16)
            lhs = jnp.concatenate([s0, s1], axis=1).astype(BF16)
            y_ref[:, pb * 128:(pb + 1) * 128] = _dot(lhs, rhs) + y_off[:, pb * 128:(pb + 1) * 128]
        xw = (xs * p["wend_e"]).astype(BF16)
        new = stv * p["elast_e"] + _dot(p["bm"].T.astype(BF16), xw)
        st[...] = new
        hf_ref[...] = new
        if ng:
            pl.when((d == 1) & (k == nc - 1))(g_finish)

    nsem = _GATHER_SEMS * ng
    return pl.pallas_call(
        body, name=name, grid=(2, nc),
        in_specs=[pl.BlockSpec((q, D_XBC), lambda d, k: (cidx(d, k), 0)),
                  pl.BlockSpec((q, 128), lambda d, k: (cidx(d, k), D_XBC // 128)),
                  pl.BlockSpec((None, 1, 128), lambda d, k: (d, 0, 0)),
                  pl.BlockSpec((None, 1, 128), lambda d, k: (d, 0, 0)),
                  pl.BlockSpec((128, D_MODEL), lambda d, k: (0, 0)),
                  pl.BlockSpec((None, D_STATE, D_MODEL), lambda d, k: (d, 0, 0))] + [_ANY] * ng,
        out_specs=[pl.BlockSpec((None, q, D_MODEL), lambda d, k: (d, cidx(d, k), 0)),
                   pl.BlockSpec((None, None, D_STATE, D_MODEL), lambda d, k: (d, cidx(d, k), 0, 0)),
                   pl.BlockSpec((None, D_STATE, D_MODEL), lambda d, k: (d, 0, 0))] + [_ANY] * ng,
        out_shape=[jax.ShapeDtypeStruct((2, n, D_MODEL), F32),
                   jax.ShapeDtypeStruct((2, nc, D_STATE, D_MODEL), BF16),
                   jax.ShapeDtypeStruct((2, D_STATE, D_MODEL), F32)] +
                  [jax.ShapeDtypeStruct((4,) + t.shape, t.dtype) for t in gather],
        scratch_shapes=[pltpu.VMEM((D_STATE, D_MODEL), F32)] +
                       ([pltpu.SemaphoreType.DMA((nsem,)), pltpu.SemaphoreType.DMA((nsem,))] if ng else []),
        compiler_params=_cp(("arbitrary", "arbitrary")),
    )(u, xd, bias2, a2, rexp, h0, *gather)


def _ssd_bwd(u, xd, bias2, a2, rexp, rexp_t, dy, hprev, lam0, name, exchange=()):
    n = u.shape[0]
    nc = n // CHUNK
    q = CHUNK
    cidx = _ssd_chunk_index(nc, reverse=True)
    ne = len(exchange)

    def body(u_ref, xd_ref, bias_ref, a_ref, r_ref, rt_ref, dy_ref, hp_ref, lam0_ref, *rest):
        e_ins, (dxs_ref, dbc_ref, ddt_ref, acc_ref, lamo_ref), rest = rest[:ne], rest[ne:ne + 5], rest[ne + 5:]
        e_outs, lam, sems = rest[:ne], rest[ne], rest[ne + 1:]
        d = pl.program_id(0)
        k = pl.program_id(1)
        if ne:
            e_start, e_finish = _exchange_steps(e_ins, e_outs, *sems)
            pl.when((d == 0) & (k == 0))(e_start)

        @pl.when(k == 0)
        def _():
            lam[...] = lam0_ref[...]
            acc_ref[...] = jnp.zeros_like(acc_ref)

        rexp_t = rt_ref[...]

        def hsum(t):
            return _dot_sel_r(t, rexp_t, n=2)

        p = _ssd_prologue(d, u_ref, xd_ref, bias_ref, a_ref, r_ref)
        xs, bm, cm = p["xs"], p["bm"], p["cm"]
        bm_bf, cm_bf = bm.astype(BF16), cm.astype(BF16)
        lamn = lam[...]
        lamn_bf = lamn.astype(BF16)
        stp = hp_ref[...]
        dyv = dy_ref[...]
        lane128 = p["col"]

        wend_e = p["wend_e"]
        cs = _dot(cm_bf, stp)
        dye_bf = (dyv * p["e"]).astype(BF16)
        dc_off = _dot_nt(dye_bf, stp)
        v = _dot(bm_bf, lamn_bf)
        xw_bf = (xs * wend_e).astype(BF16)
        db_off = _dot_nt(xw_bf, lamn_bf)
        elast_e = p["elast_e"]
        dlast_e = jnp.sum(stp.astype(F32) * lamn, axis=0, keepdims=True) * elast_e
        lam_new = lamn * elast_e + _dot(cm.T.astype(BF16), dye_bf)
        lam[...] = lam_new
        lamo_ref[...] = lam_new

        hs_vx = hsum(v * xs)
        om = p["wend16"] * hs_vx
        x1 = p["e16"] * hsum(dyv * cs) - om
        x2 = p["dend16"] * hs_vx
        x3 = jnp.sum(om, axis=0, keepdims=True) + hsum(jnp.broadcast_to(dlast_e, (8, D_MODEL)))[0:1, :]

        sub16 = lax.broadcasted_iota(jnp.int32, (SSD_HEADS, q), 0)
        rs = jnp.zeros((q, 128), F32)
        cs_m = jnp.zeros((SSD_HEADS, q), F32)
        dt_m = jnp.zeros((SSD_HEADS, q), F32)
        dg = jnp.zeros((q, q), F32)
        for pb in range(SSD_HEADS // 2):
            xp_bf = xs[:, pb * 128:(pb + 1) * 128].astype(BF16)
            dyp = dyv[:, pb * 128:(pb + 1) * 128]
            dxs_pair = None
            for half in range(2):
                h = 2 * pb + half
                sel = (lane128 < HEAD_DIM) if half == 0 else (lane128 >= HEAD_DIM)
                dyh_bf = jnp.where(sel, dyp, 0.0).astype(BF16)
                lm, gl, s = _ssd_head_mats(p, h)
                ds = _dot_nt(dyh_bf, xp_bf)
                t = _dot_tn(s.astype(BF16), dyh_bf)
                dxs_pair = t if dxs_pair is None else dxs_pair + t
                w = ds * s
                rs = rs + jnp.sum(w, axis=1, keepdims=True) * (lane128 == h).astype(F32)
                cs_m = jnp.where(sub16 == h, jnp.sum(w, axis=0, keepdims=True), cs_m)
                dt_m = jnp.where(sub16 == h, jnp.sum(ds * gl, axis=0, keepdims=True), dt_m)
                dg = dg + ds * lm * p["dt_t"][h:h + 1, :]
            sl = slice(pb * 128, (pb + 1) * 128)
            dxs_ref[:, sl] = dxs_pair + v[:, sl] * wend_e[:, sl]

        def to_lanes(m16):
            return jnp.concatenate([m16, jnp.zeros((128 - SSD_HEADS, q), F32)], axis=0).T

        last = jnp.where(d == 0, q - 1, 0)
        dacum = rs - to_lanes(cs_m) + x1 + jnp.where(p["row"] == last, x3[0:1, :], 0.0)
        tri_t = jnp.where(p["mask_t"], 1.0, 0.0).astype(BF16)
        ddta = _dot_sel_l(tri_t, dacum)
        dt = p["dt"]
        a = p["a"]
        ddt = to_lanes(dt_m) + x2 + a * ddta
        ddtraw = jnp.where(p["head_lane"], ddt * _sigmoid(p["dtraw"]), 0.0)
        ddt_ref[...] = ddtraw
        acc_ref[0:1, :] += jnp.sum(ddtraw, axis=0, keepdims=True)
        acc_ref[1:2, :] += jnp.sum(dt * ddta, axis=0, keepdims=True) * a

        dg_bf = dg.astype(BF16)
        dbc_ref[:, 0:D_STATE] = _dot_tn(dg_bf, cm_bf) + db_off
        dbc_ref[:, D_STATE:2 * D_STATE] = _dot(dg_bf, bm_bf) + dc_off
        if ne:
            pl.when((d == 1) & (k == nc - 1))(e_finish)

    cblk = lambda d, k: (cidx(d, k), 0)
    return pl.pallas_call(
        body, name=name, grid=(2, nc),
        in_specs=[pl.BlockSpec((q, D_XBC), cblk),
                  pl.BlockSpec((q, 128), lambda d, k: (cidx(d, k), D_XBC // 128)),
                  pl.BlockSpec((None, 1, 128), lambda d, k: (d, 0, 0)),
                  pl.BlockSpec((None, 1, 128), lambda d, k: (d, 0, 0)),
                  pl.BlockSpec((128, D_MODEL), lambda d, k: (0, 0)),
                  pl.BlockSpec((D_MODEL, 128), lambda d, k: (0, 0)),
                  pl.BlockSpec((q, D_MODEL), cblk),
                  pl.BlockSpec((None, None, D_STATE, D_MODEL), lambda d, k: (d, cidx(d, k), 0, 0)),
                  pl.BlockSpec((None, D_STATE, D_MODEL), lambda d, k: (d, 0, 0))] + [_ANY] * ne,
        out_specs=[pl.BlockSpec((None, q, D_MODEL), lambda d, k: (d, cidx(d, k), 0)),
                   pl.BlockSpec((None, q, 2 * D_STATE), lambda d, k: (d, cidx(d, k), 0)),
                   pl.BlockSpec((None, q, 128), lambda d, k: (d, cidx(d, k), 0)),
                   pl.BlockSpec((None, 8, 128), lambda d, k: (d, 0, 0)),
                   pl.BlockSpec((None, D_STATE, D_MODEL), lambda d, k: (d, 0, 0))] + [_ANY] * ne,
        out_shape=[jax.ShapeDtypeStruct((2, n, D_MODEL), F32),
                   jax.ShapeDtypeStruct((2, n, 2 * D_STATE), F32),
                   jax.ShapeDtypeStruct((2, n, 128), F32),
                   jax.ShapeDtypeStruct((2, 8, 128), F32),
                   jax.ShapeDtypeStruct((2, D_STATE, D_MODEL), F32)] +
                  [jax.ShapeDtypeStruct(t.shape, t.dtype) for t in exchange],
        scratch_shapes=[pltpu.VMEM((D_STATE, D_MODEL), F32)] +
                       ([pltpu.SemaphoreType.DMA((3 * ne,)), pltpu.SemaphoreType.DMA((3 * ne,)),
                         pltpu.SemaphoreType.DMA((ne,))] if ne else []),
        compiler_params=_cp(("arbitrary", "arbitrary")),
    )(u, xd, bias2, a2, rexp, rexp_t, dy, hprev, lam0, *exchange)


def _merge_fwd(y, u, z, dskip_e, gn, tb):
    n = z.shape[0]

    def body(y_ref, u_ref, z_ref, sk_ref, gn_ref, o_ref):
        sk, gnv = sk_ref[...], gn_ref[...]
        for r in _row_tiles(tb):
            ys = y_ref[0, r, :] + y_ref[1, r, :] + sk * _silu(u_ref[r, :])
            gated = ys * _silu(z_ref[r, :])
            rstd = lax.rsqrt(jnp.mean(gated * gated, axis=-1, keepdims=True) + LN_EPS)
            o_ref[r, :] = (gated * rstd * gnv).astype(BF16)

    return pl.pallas_call(
        body, name="merge_fwd", grid=(n // tb,),
        in_specs=[pl.BlockSpec((2, tb, D_MODEL), lambda i: (0, i, 0)), pl.BlockSpec((tb, D_MODEL), lambda i: (i, 0)),
                  _row_spec(tb, D_MODEL), _par_spec(D_MODEL), _par_spec(D_MODEL)],
        out_specs=_row_spec(tb, D_MODEL),
        out_shape=jax.ShapeDtypeStruct((n, D_MODEL), BF16),
        compiler_params=_cp(("parallel",)),
    )(y, u, z, dskip_e, gn)


def _merge_bwd(dyn, y, u, z, dskip_e, gn, tb):
    n = z.shape[0]

    def body(dyn_ref, y_ref, u_ref, z_ref, sk_ref, gn_ref, dy_ref, dz_ref, acc_ref):
        i = pl.program_id(0)

        @pl.when(i == 0)
        def _():
            acc_ref[...] = jnp.zeros_like(acc_ref)

        sk, gnv = sk_ref[...], gn_ref[...]
        part0 = jnp.zeros((8, D_MODEL), F32)
        part1 = jnp.zeros((8, D_MODEL), F32)
        for s in range(tb // ROW_SUB):
            r = slice(s * ROW_SUB, (s + 1) * ROW_SUB)
            xs = _silu(u_ref[r, :])
            zv = z_ref[r, :]
            sz = _sigmoid(zv)
            ys = y_ref[0, r, :] + y_ref[1, r, :] + sk * xs
            gated = ys * (zv * sz)
            rstd = lax.rsqrt(jnp.mean(gated * gated, axis=-1, keepdims=True) + LN_EPS)
            ghat = gated * rstd
            dyn_v = dyn_ref[r, :]
            t = dyn_v * gnv
            dgated = rstd * (t - ghat * jnp.mean(t * ghat, axis=-1, keepdims=True))
            dys = dgated * (zv * sz)
            dy_ref[r, :] = dys
            dz_ref[r, :] = (dgated * ys * (sz * (1.0 + zv * (1.0 - sz)))).astype(BF16)
            part0 = part0 + _fold8(dyn_v * ghat)
            part1 = part1 + _fold8(dys * xs)
        acc_ref[0:1, :] += jnp.sum(part0, axis=0, keepdims=True)
        acc_ref[1:2, :] += jnp.sum(part1, axis=0, keepdims=True)

    return pl.pallas_call(
        body, name="merge_bwd", grid=(n // tb,),
        in_specs=[_row_spec(tb, D_MODEL), pl.BlockSpec((2, tb, D_MODEL), lambda i: (0, i, 0)),
                  pl.BlockSpec((tb, D_MODEL), lambda i: (i, 0)), _row_spec(tb, D_MODEL),
                  _par_spec(D_MODEL), _par_spec(D_MODEL)],
        out_specs=[_row_spec(tb, D_MODEL), _row_spec(tb, D_MODEL), _acc_spec(D_MODEL)],
        out_shape=[jax.ShapeDtypeStruct((n, D_MODEL), F32), jax.ShapeDtypeStruct((n, D_MODEL), BF16),
                   jax.ShapeDtypeStruct((8, D_MODEL), F32)],
        compiler_params=_cp(("arbitrary",)),
    )(dyn, y, u, z, dskip_e, gn)


def _pool_consts(transpose):
    tb = POOL_TB
    t = jnp.arange(tb)
    s = jnp.arange(3 * tb)
    rl, cl = t // GRID_W, t % GRID_W
    rs_, cs_ = s // GRID_W - tb // GRID_W, s % GRID_W
    s2 = jnp.arange(tb)
    rl2, cl2 = s2 // GRID_W, s2 % GRID_W
    brow, bcol = [], []
    for w in POOL_WINDOWS:
        lo, hi = -(w // 2), w - w // 2
        if transpose:
            lo, hi = -hi + 1, -lo + 1
        dr = rs_[None, :] - rl[:, None]
        brow.append(((cs_[None, :] == cl[:, None]) & (dr >= lo) & (dr < hi)).astype(BF16))
        dc = cl2[None, :] - cl[:, None]
        bcol.append(((rl2[None, :] == rl[:, None]) & (dc >= lo) & (dc < hi)).astype(BF16))
    return jnp.stack(brow), jnp.stack(bcol)


def _pool_inv(i, g, n):
    assert GRID_W == 64
    t = i * POOL_TB + lax.broadcasted_iota(jnp.int32, (POOL_TB, 1), 0)
    r = lax.shift_right_logical(t, 6)
    col = t & (GRID_W - 1)
    w = POOL_WINDOWS[g]
    lo, hi = -(w // 2), w - w // 2
    cnt_r = jnp.minimum(r + hi, n // GRID_W) - jnp.maximum(r + lo, 0)
    cnt_c = jnp.minimum(col + hi, GRID_W) - jnp.maximum(col + lo, 0)
    return 1.0 / (cnt_r * cnt_c).astype(F32)


def _pool_box(prev_ref, cur_ref, next_ref, brow_ref, bcol_ref, g, i, nb):
    sl = slice(g * POOL_DIM, (g + 1) * POOL_DIM)
    pv = prev_ref[:, sl] * (i > 0).astype(prev_ref.dtype)
    nx = next_ref[:, sl] * (i < nb - 1).astype(next_ref.dtype)
    stack = jnp.concatenate([pv.astype(BF16), cur_ref[:, sl].astype(BF16), nx.astype(BF16)], axis=0)
    r = _dot(brow_ref[g], stack)
    return _dot(bcol_ref[g], r.astype(BF16))


def _pool_halo_specs(n, d):
    tb = POOL_TB
    nb = n // tb
    prev = pl.BlockSpec((tb, d), lambda i: (jnp.maximum(i - 1, 0), 0))
    cur = pl.BlockSpec((tb, d), lambda i: (i, 0))
    nxt = pl.BlockSpec((tb, d), lambda i: (jnp.minimum(i + 1, nb - 1), 0))
    return prev, cur, nxt


def _pool_const_specs():
    tb = POOL_TB
    return [pl.BlockSpec((N_POOL, tb, 3 * tb), lambda i: (0, 0, 0)),
            pl.BlockSpec((N_POOL, tb, tb), lambda i: (0, 0, 0))]


def _pool_fwd(up, consts, pw_bf, pscale):
    n = up.shape[0]
    tb = POOL_TB
    nb = n // tb
    brow, bcol = consts
    prev, cur, nxt = _pool_halo_specs(n, D_MODEL)

    def body(p_ref, c_ref, n_ref, brow_ref, bcol_ref, pw_ref, sc_ref, o_ref, d_ref):
        i = pl.program_id(0)
        for g in range(N_POOL):
            sl = slice(g * POOL_DIM, (g + 1) * POOL_DIM)
            box = _pool_box(p_ref, c_ref, n_ref, brow_ref, bcol_ref, g, i, nb)
            dd = (box * _pool_inv(i, g, n) - c_ref[:, sl]).astype(BF16)
            d_ref[:, sl] = dd
            o_ref[:, sl] = (_dot(dd, pw_ref[g]) * sc_ref[:, sl]).astype(BF16)

    return pl.pallas_call(
        body, name="pool_fwd", grid=(nb,),
        in_specs=[prev, cur, nxt] + _pool_const_specs() +
                 [pl.BlockSpec((N_POOL, POOL_DIM, POOL_DIM), lambda i: (0, 0, 0)), _par_spec(D_MODEL)],
        out_specs=[_row_spec(tb, D_MODEL), _row_spec(tb, D_MODEL)],
        out_shape=[jax.ShapeDtypeStruct((n, D_MODEL), BF16), jax.ShapeDtypeStruct((n, D_MODEL), BF16)],
        compiler_params=_cp(("parallel",)),
    )(up, up, up, brow, bcol, pw_bf, pscale)


def _pool_bwd_a(dp, dsave, pw_bf, pwt_bf, pscale):
    n = dp.shape[0]
    tb = POOL_TB

    def body(dp_ref, d_ref, pw_ref, pwt_ref, sc_ref, dd_ref, dds_ref, gw_ref, gs_ref):
        i = pl.program_id(0)

        @pl.when(i == 0)
        def _():
            gw_ref[...] = jnp.zeros_like(gw_ref)
            gs_ref[...] = jnp.zeros_like(gs_ref)

        for g in range(N_POOL):
            sl = slice(g * POOL_DIM, (g + 1) * POOL_DIM)
            dpv = dp_ref[:, sl]
            dv = d_ref[:, sl]
            dpw_bf = (dpv * sc_ref[:, sl]).astype(BF16)
            dd = _dot(dpw_bf, pwt_ref[g])
            dd_ref[:, sl] = dd
            dds_ref[:, sl] = (dd * _pool_inv(i, g, n)).astype(BF16)
            gw_ref[g] += _dot_tn(dv, dpw_bf)
            gs_ref[0:1, sl] += jnp.sum(dpv * _dot(dv, pw_ref[g]), axis=0, keepdims=True)

    wspec = pl.BlockSpec((N_POOL, POOL_DIM, POOL_DIM), lambda i: (0, 0, 0))
    return pl.pallas_call(
        body, name="pool_bwd_a", grid=(n // tb,),
        in_specs=[_row_spec(tb, D_MODEL), _row_spec(tb, D_MODEL), wspec, wspec, _par_spec(D_MODEL)],
        out_specs=[_row_spec(tb, D_MODEL), _row_spec(tb, D_MODEL), wspec, _acc_spec(D_MODEL)],
        out_shape=[jax.ShapeDtypeStruct((n, D_MODEL), F32), jax.ShapeDtypeStruct((n, D_MODEL), BF16),
                   jax.ShapeDtypeStruct((N_POOL, POOL_DIM, POOL_DIM), F32), jax.ShapeDtypeStruct((8, D_MODEL), F32)],
        compiler_params=_cp(("arbitrary",)),
    )(dp, dsave, pw_bf, pwt_bf, pscale)


def _pool_bwd_b(dds, dd, consts_t):
    n = dd.shape[0]
    tb = POOL_TB
    nb = n // tb
    brow, bcol = consts_t
    prev, cur, nxt = _pool_halo_specs(n, D_MODEL)

    def body(p_ref, c_ref, n_ref, brow_ref, bcol_ref, dd_ref, o_ref):
        i = pl.program_id(0)
        for g in range(N_POOL):
            sl = slice(g * POOL_DIM, (g + 1) * POOL_DIM)
            box = _pool_box(p_ref, c_ref, n_ref, brow_ref, bcol_ref, g, i, nb)
            o_ref[:, sl] = (box - dd_ref[:, sl]).astype(BF16)

    return pl.pallas_call(
        body, name="pool_bwd_b", grid=(nb,),
        in_specs=[prev, cur, nxt] + _pool_const_specs() + [_row_spec(tb, D_MODEL)],
        out_specs=_row_spec(tb, D_MODEL),
        out_shape=jax.ShapeDtypeStruct((n, D_MODEL), BF16),
        compiler_params=_cp(("parallel",)),
    )(dds, dds, dds, brow, bcol, dd)


def _pair_add(slabs, recvs, core, name):
    na = len(slabs)
    hr = [t.shape[1] // 4 for t in slabs]

    def body(core_ref, *refs):
        for a in range(na):
            refs[2 * na + a][...] = (refs[a][...] + refs[na + a][...]).astype(BF16)

    own = [pl.BlockSpec((None, hr[a], slabs[a].shape[2]), lambda j, i, c_ref: (j, 2 * c_ref[0] + i, 0)) for a in range(na)]
    got = [pl.BlockSpec((None, hr[a], slabs[a].shape[2]), lambda j, i, c_ref: (j, i, 0)) for a in range(na)]
    return pl.pallas_call(
        body, name=name,
        grid_spec=pltpu.PrefetchScalarGridSpec(num_scalar_prefetch=1, grid=(4, 2), in_specs=own + got, out_specs=got),
        out_shape=[jax.ShapeDtypeStruct(r.shape, BF16) for r in recvs],
        compiler_params=_cp(("arbitrary", "arbitrary")),
    )(core, *slabs, *recvs)


def _sum4(parts, core):
    na = len(parts)
    hr = [t.shape[1] // 2 for t in parts]

    def body(core_ref, *refs):
        for a in range(na):
            p = refs[a]
            refs[na + a][...] = ((p[0].astype(F32) + p[1].astype(F32)) + p[2].astype(F32)) + p[3].astype(F32)

    return pl.pallas_call(
        body, name="reduce_g_sum",
        grid_spec=pltpu.PrefetchScalarGridSpec(
            num_scalar_prefetch=1, grid=(2,),
            in_specs=[pl.BlockSpec((4, hr[a], parts[a].shape[2]), lambda i, c_ref: (0, i, 0)) for a in range(na)],
            out_specs=[pl.BlockSpec((hr[a], parts[a].shape[2]), lambda i, c_ref: (2 * c_ref[0] + i, 0))
                       for a in range(na)]),
        out_shape=[jax.ShapeDtypeStruct((2 * t.shape[1], t.shape[2]), F32) for t in parts],
        compiler_params=_cp(("arbitrary",)),
    )(core, *parts)


def _adamw(w, g, m, v, name):
    r, cdim = w.shape
    tb = _row_block(r, 256)
    c1 = 1.0 - ADAM_B1 ** ADAM_STEP
    c2 = 1.0 - ADAM_B2 ** ADAM_STEP

    def body(w_ref, g_ref, m_ref, v_ref, d_ref, nm_ref, nv_ref):
        gv = g_ref[...]
        nm = ADAM_B1 * m_ref[...] + (1.0 - ADAM_B1) * gv
        nv = ADAM_B2 * v_ref[...] + (1.0 - ADAM_B2) * (gv * gv)
        m_hat = nm / c1
        v_hat = nv / c2
        d_ref[...] = -ADAM_LR * (m_hat / (jnp.sqrt(v_hat) + ADAM_EPS) + ADAM_WD * w_ref[...])
        nm_ref[...] = nm
        nv_ref[...] = nv

    spec = _row_spec(tb, cdim)
    shp = jax.ShapeDtypeStruct((r, cdim), F32)
    return pl.pallas_call(
        body, name=name, grid=(r // tb,),
        in_specs=[spec] * 4, out_specs=[spec] * 3, out_shape=[shp] * 3,
        compiler_params=_cp(("parallel",)),
    )(w, g, m, v)


def _mesh_pos():
    return lax.axis_index("x"), lax.axis_index("y"), lax.axis_index("c")


_ANY = pl.BlockSpec(memory_space=pl.ANY)


def _remote(src, dst, send_sem, recv_sem, device):
    return pltpu.make_async_remote_copy(src_ref=src, dst_ref=dst, send_sem=send_sem, recv_sem=recv_sem,
                                        device_id=device, device_id_type=MESH)


def _other_chips(x, y):
    return [(1 - x, y), (x, 1 - y), (1 - x, 1 - y)]


def _half(nrows, h):
    return pl.ds(h * (nrows // 2), nrows // 2)


_GATHER_SEMS = 7


def _gather_steps(ins, outs, send_sems, recv_sems):
    na = len(ins)
    nrow = [r.shape[0] for r in ins]

    def copies():
        x, y, c = _mesh_pos()
        me = 2 * x + y
        sib = (x, y, 1 - c)
        chips = _other_chips(x, y)

        def ici(k, a, slot):
            px, py = chips[k]
            rows = _half(nrow[a], c)
            return _remote(ins[a].at[rows, :], outs[a].at[slot, rows, :], send_sems.at[k * na + a],
                           recv_sems.at[k * na + a], (px, py, c))

        def fwd(k, a, h):
            px, py = chips[k]
            blk = outs[a].at[2 * px + py, _half(nrow[a], h), :]
            return _remote(blk, blk, send_sems.at[(3 + k) * na + a], recv_sems.at[(3 + k) * na + a], sib)

        def own(a):
            return _remote(ins[a], outs[a].at[me], send_sems.at[6 * na + a], recv_sems.at[6 * na + a], sib)

        slots = [2 * px + py for px, py in chips]
        return ici, fwd, own, me, c, slots

    def start():
        ici, _, own, me, _, _ = copies()
        for a in range(na):
            own(a).start()
        for k in range(3):
            for a in range(na):
                ici(k, a, me).start()

    def finish():
        ici, fwd, own, me, c, slots = copies()
        for k in range(3):
            for a in range(na):
                ici(k, a, slots[k]).wait_recv()
                fwd(k, a, c).start()
        for k in range(3):
            for a in range(na):
                fwd(k, a, 1 - c).wait_recv()
        for a in range(na):
            own(a).wait_recv()
        for a in range(na):
            own(a).wait_send()
        for k in range(3):
            for a in range(na):
                ici(k, a, me).wait_send()
                fwd(k, a, c).wait_send()

    return start, finish


def _exchange_steps(ins, outs, send_sems, recv_sems, local_sems):
    na = len(ins)

    def copies():
        x, y, c = _mesh_pos()
        me = 2 * x + y
        chips = _other_chips(x, y)

        def copy(k, a, slot):
            px, py = chips[k]
            return _remote(ins[a].at[2 * px + py], outs[a].at[slot], send_sems.at[k * na + a], recv_sems.at[k * na + a],
                           (px, py, c))

        def local(a):
            return pltpu.make_async_copy(ins[a].at[me], outs[a].at[me], local_sems.at[a])

        return copy, local, me, [2 * px + py for px, py in chips]

    def start():
        copy, local, me, _ = copies()
        for a in range(na):
            local(a).start()
        for k in range(3):
            for a in range(na):
                copy(k, a, me).start()

    def finish():
        copy, local, me, slots = copies()
        for k in range(3):
            for a in range(na):
                copy(k, a, slots[k]).wait_recv()
        for k in range(3):
            for a in range(na):
                copy(k, a, me).wait_send()
        for a in range(na):
            local(a).wait()

    return start, finish


def _gather_weights(shards, conv8):
    na = len(shards)

    def body(*refs):
        ins, conv_in = refs[:na], refs[na]
        outs, conv_out = refs[na + 1:2 * na + 1], refs[2 * na + 1]
        send_sems, recv_sems, local_sems = refs[2 * na + 2:]
        x, y, c = _mesh_pos()
        me = 2 * x + y
        chips = _other_chips(x, y)

        def conv(k, slot):
            px, py = chips[k]
            return _remote(conv_in, conv_out.at[slot], send_sems.at[7 * na + k], recv_sems.at[7 * na + k], (px, py, c))

        start, finish = _gather_steps(ins, outs, send_sems, recv_sems)
        local = pltpu.make_async_copy(conv_in, conv_out.at[me], local_sems.at[0])
        local.start()
        start()
        sends = [conv(k, me) for k in range(3)]
        for cp in sends:
            cp.start()
        finish()
        for k in range(3):
            px, py = chips[k]
            conv(k, 2 * px + py).wait_recv()
        for cp in sends:
            cp.wait_send()
        local.wait()

    nsem = _GATHER_SEMS * na + 3
    return pl.pallas_call(
        body, name="gather_w", in_specs=[_ANY] * (na + 1), out_specs=[_ANY] * (na + 1),
        out_shape=[jax.ShapeDtypeStruct((4,) + t.shape, t.dtype) for t in shards] +
                  [jax.ShapeDtypeStruct((4,) + conv8.shape, conv8.dtype)],
        scratch_shapes=[pltpu.SemaphoreType.DMA((nsem,)), pltpu.SemaphoreType.DMA((nsem,)),
                        pltpu.SemaphoreType.DMA((1,))],
    )(*shards, conv8)


def _pair_swap(slabs, name):
    na = len(slabs)

    def body(*refs):
        ins, outs = refs[:na], refs[na:2 * na]
        send_sems, recv_sems = refs[2 * na:]
        x, y, c = _mesh_pos()
        cps = [_remote(ins[a].at[:, _half(slabs[a].shape[1], 1 - c), :], outs[a], send_sems.at[a], recv_sems.at[a],
                       (x, y, 1 - c)) for a in range(na)]
        for cp in cps:
            cp.start()
        for cp in cps:
            cp.wait()

    return pl.pallas_call(
        body, name=name, in_specs=[_ANY] * na, out_specs=[_ANY] * na,
        out_shape=[jax.ShapeDtypeStruct((4, t.shape[1] // 2, t.shape[2]), t.dtype) for t in slabs],
        scratch_shapes=[pltpu.SemaphoreType.DMA((na,)), pltpu.SemaphoreType.DMA((na,))],
    )(*slabs)


def _chip_exchange(pairs):
    na = len(pairs)

    def body(*refs):
        start, finish = _exchange_steps(refs[:na], refs[na:2 * na], *refs[2 * na:])
        start()
        finish()

    return pl.pallas_call(
        body, name="reduce_g_ici", in_specs=[_ANY] * na, out_specs=[_ANY] * na,
        out_shape=[jax.ShapeDtypeStruct(t.shape, t.dtype) for t in pairs],
        scratch_shapes=[pltpu.SemaphoreType.DMA((3 * na,)), pltpu.SemaphoreType.DMA((3 * na,)),
                        pltpu.SemaphoreType.DMA((na,))],
    )(*pairs)


def _share_halves(totals):
    na = len(totals)

    def body(*refs):
        bufs = refs[na:2 * na]
        send_sems, recv_sems = refs[2 * na:]
        x, y, c = _mesh_pos()

        def copy(a, h):
            blk = bufs[a].at[_half(totals[a].shape[0], h), :]
            return _remote(blk, blk, send_sems.at[a], recv_sems.at[a], (x, y, 1 - c))

        sends = [copy(a, c) for a in range(na)]
        for cp in sends:
            cp.start()
        for a in range(na):
            copy(a, 1 - c).wait_recv()
        for cp in sends:
            cp.wait_send()

    return pl.pallas_call(
        body, name="reduce_g_share", in_specs=[_ANY] * na, out_specs=[_ANY] * na,
        out_shape=[jax.ShapeDtypeStruct(t.shape, t.dtype) for t in totals],
        input_output_aliases={a: a for a in range(na)},
        scratch_shapes=[pltpu.SemaphoreType.DMA((na,)), pltpu.SemaphoreType.DMA((na,))],
    )(*totals)


def _allreduce_small(v, name):
    r, cdim = v.shape

    def body(v_ref, out_ref, buf, send_sems, recv_sems):
        x, y, c = _mesh_pos()
        me = 4 * x + 2 * y + c
        buf[me] = v_ref[...]
        rel = [(bx, by, bc) for bx in (0, 1) for by in (0, 1) for bc in (0, 1)][1:]

        def peer(b):
            bx, by, bc = b
            return ((1 - x) if bx else x, (1 - y) if by else y, (1 - c) if bc else c)

        def copy(k, slot):
            return pltpu.make_async_remote_copy(
                src_ref=v_ref, dst_ref=buf.at[slot], send_sem=send_sems.at[k], recv_sem=recv_sems.at[k],
                device_id=peer(rel[k]), device_id_type=MESH)

        sends = [copy(k, me) for k in range(7)]
        for cp in sends:
            cp.start()
        for k in range(7):
            px, py, pc = peer(rel[k])
            copy(k, 4 * px + 2 * py + pc).wait_recv()
        for cp in sends:
            cp.wait_send()
        acc = buf[0]
        for j in range(1, 8):
            acc = acc + buf[j]
        out_ref[...] = acc

    vm = pl.BlockSpec(memory_space=pltpu.VMEM)
    return pl.pallas_call(
        body, name=name, in_specs=[vm], out_specs=[vm, vm],
        out_shape=[jax.ShapeDtypeStruct((r, cdim), F32), jax.ShapeDtypeStruct((8, r, cdim), F32)],
        scratch_shapes=[pltpu.SemaphoreType.DMA((7,)), pltpu.SemaphoreType.DMA((7,))],
    )(v)


def _chip_bcast(v, name):
    def body(v_ref, out_ref, send_sems, recv_sems):
        x, y, c = _mesh_pos()
        me = 2 * x + y
        chips = _other_chips(x, y)
        out_ref[me] = v_ref[...]

        def copy(k, slot):
            px, py = chips[k]
            return _remote(v_ref, out_ref.at[slot], send_sems.at[k], recv_sems.at[k], (px, py, c))

        sends = [copy(k, me) for k in range(3)]
        for cp in sends:
            cp.start()
        for k, (px, py) in enumerate(chips):
            copy(k, 2 * px + py).wait_recv()
        for cp in sends:
            cp.wait_send()

    vm = pl.BlockSpec(memory_space=pltpu.VMEM)
    return pl.pallas_call(
        body, name=name, in_specs=[vm], out_specs=vm,
        out_shape=jax.ShapeDtypeStruct((4,) + v.shape, F32),
        scratch_shapes=[pltpu.SemaphoreType.DMA((3,)), pltpu.SemaphoreType.DMA((3,))],
    )(v)


_BIG = (("in_proj", (D_MODEL, D_IN_PROJ // 4), 1), ("w_out", (2 * D_MODEL // 4, D_MODEL), 0),
        ("w_gate", (D_MODEL, D_FF // 4), 1), ("w_up", (D_MODEL, D_FF // 4), 1), ("w_down", (D_FF // 4, D_MODEL), 0),
        ("pool_w", (N_POOL * POOL_DIM // 4, POOL_DIM), None))


def _assemble(name, t):
    _, r, c = t.shape
    axis = {n: ax for n, _, ax in _BIG}[name]
    if axis == 0:
        return t.reshape(4 * r, c)
    if axis == 1:
        return t.transpose(1, 0, 2).reshape(r, 4 * c)
    return t.reshape(4, N_POOL, POOL_DIM // 4, POOL_DIM).transpose(1, 0, 2, 3).reshape(N_POOL, POOL_DIM, POOL_DIM)


def _to_slabs(name, g):
    (r, c), axis = {n: (sh, ax) for n, sh, ax in _BIG}[name]
    if axis == 0:
        return g.reshape(4, r, c)
    if axis == 1:
        return g.reshape(r, 4, c).transpose(1, 0, 2)
    return g.reshape(N_POOL, 4, POOL_DIM // 4, POOL_DIM).transpose(1, 0, 2, 3).reshape(4, r, c)


_EARLY = ("in_proj",)
_LATE = tuple(n for n, _, _ in _BIG if n not in _EARLY)


def _reduce_grads(early_grads, late_parts, core):
    slabs = [_to_slabs(n, early_grads[n]) for n in _EARLY]
    pairs = _pair_add(slabs, _pair_swap(slabs, "reduce_g_d2d"), core, "reduce_g_pair")
    parts = dict(zip(_EARLY, _chip_exchange(pairs)), **dict(zip(_LATE, late_parts)))
    names = [n for n, _, _ in _BIG]
    totals = _sum4([parts[n] for n in names], core)
    return dict(zip(names, _share_halves(totals)))


def _pad_cols(w, n):
    return jnp.concatenate([w, jnp.zeros((w.shape[0], n - w.shape[1]), w.dtype)], axis=1)


def _device_step(x, mod, mod_ctx, ctx, target, wts, w8, small, tb, late_shards=None, core=None):
    n = x.shape[0]
    d = D_MODEL

    win = wts["in_proj"]
    wz, wxd, wup = win[:, 0:d], _pad_cols(win[:, d:d + D_XBC + 2 * SSD_HEADS], D_XD), win[:, d + D_XBC + 2 * SSD_HEADS:]

    emb_g, emb_b = _vec(small["emb_ln_g"]), _vec(small["emb_ln_b"])
    ln1_g, ln1_b = _vec(small["ln1_g"]), _vec(small["ln1_b"])
    ln2_g, ln2_b = _vec(small["ln2_g"]), _vec(small["ln2_b"])
    gn = _vec(small["ssd_norm_g"])
    pscale = _vec(small["pool_scale"])
    conv_b = _vec(small["conv_b"])
    dskip_e = jnp.repeat(small["d_skip"].reshape(-1), HEAD_DIM).reshape(1, d)
    zpad = jnp.zeros((2, 1, 128 - SSD_HEADS), F32)
    bias2 = jnp.concatenate([small["dt_bias"].reshape(2, 1, SSD_HEADS), zpad], axis=2)
    a2 = jnp.concatenate([-jnp.exp(small["a_log"].reshape(2, 1, SSD_HEADS)), zpad], axis=2)
    rexp = (jnp.arange(128)[:, None] == (jnp.arange(d)[None, :] // HEAD_DIM)).astype(BF16)
    rexp_t = rexp.T

    sh1, sc1, g1, sh2, sc2, g2 = [mod[:, i * d:(i + 1) * d] for i in range(6)]
    sh1c, sc1c = mod_ctx[:, 0:d], mod_ctx[:, d:2 * d]

    tbc = min(tb, ctx.shape[0])
    xc0, hc = _ln_mod(ctx, emb_g, emb_b, sh1c, sc1c, tbc, "ln_mod_ctx")
    xdc = _matmul_nn([(hc, wxd)], F32, 512, D_XD, "in_proj_ctx")
    uc = _conv_fwd(xdc, w8, conv_b, tbc, "conv_fwd_ctx")
    hzero = jnp.zeros((2, D_STATE, d), F32)
    _, hprev_c, hfin_c = _ssd_fwd(uc, xdc, bias2, a2, rexp, hzero, "ssd_fwd_ctx")

    x0, h1 = _ln_mod(x, emb_g, emb_b, sh1, sc1, tb, "ln_mod")
    z = _matmul_nn([(h1, wz)], F32, MM_ROWS, 1024, "in_proj_z")
    xd = _matmul_nn([(h1, wxd)], F32, MM_ROWS, D_XD, "in_proj_xd")
    up = _matmul_nn([(h1, wup)], F32, MM_ROWS, 1024, "in_proj_up")
    u = _conv_fwd(xd, w8, conv_b, tb, "conv_fwd")
    y, hprev, _, *landed = _ssd_fwd(u, xd, bias2, a2, rexp, hfin_c, "ssd_fwd", gather=late_shards or ())
    if late_shards is not None:
        wts = dict(wts, **{nme: _assemble(nme, t) for nme, t in zip(_LATE, landed)})
    wout = wts["w_out"]
    wg, wu, wd = wts["w_gate"], wts["w_up"], wts["w_down"]
    pw = wts["pool_w"]
    yn = _merge_fwd(y, u, z, dskip_e, gn, tb)
    pconst = _pool_consts(False)
    pool, dsave = _pool_fwd(up, pconst, pw, pscale)
    mix = _matmul_nn([(yn, wout[0:d]), (pool, wout[d:2 * d])], F32, MM_ROWS, 1024, "out_proj")
    x1, h2 = _res_ln(x0, mix, g1, ln1_g, ln1_b, sh2, sc2, tb)

    gate, upp, hmid = _swiglu_fwd(h2, wg, wu, 512, D_FF // 2)
    ffn = _matmul_nn([(hmid, wd)], F32, MM_ROWS, 1024, "ffn_down")
    dffn, dr2, acc2 = _final_ln_loss(x1, ffn, g2, ln2_g, ln2_b, target, tb)
    loss = (0.5 / d) * jnp.sum(acc2[3])

    dgate, dupp = _swiglu_bwd(dffn, wd.T, gate, upp, 512, D_FF // 2)
    g_wdown = _matmul_tn(hmid, dffn, MM_ROWS, 1024, "g_w_down")
    g_wgate = _matmul_tn(h2, dgate, MM_ROWS, 1408, "g_w_gate")
    g_wup = _matmul_tn(h2, dupp, MM_ROWS, 1408, "g_w_up")
    dh2 = _matmul_nn([(dgate, wg.T), (dupp, wu.T)], F32, 512, 1024, "d_h2")
    dmix, dr1, acc1 = _bwd_ln1(dr2, dh2, x1, x0, mix, g1, sc2, ln1_g, tb)

    dyn = _matmul_nn([(dmix, wout[0:d].T)], F32, MM_ROWS, 1024, "d_yn")
    dpool = _matmul_nn([(dmix, wout[d:2 * d].T)], F32, MM_ROWS, 1024, "d_pool")
    g_wout = jnp.concatenate([_matmul_tn(yn, dmix, MM_ROWS, 1024, "g_w_out_a"),
                              _matmul_tn(pool, dmix, MM_ROWS, 1024, "g_w_out_b")], axis=0)
    dd, dds, g_pw, accp = _pool_bwd_a(dpool, dsave, pw, jnp.swapaxes(pw, 1, 2), pscale)
    dup = _pool_bwd_b(dds, dd, _pool_consts(True))
    dy, dz, accm = _merge_bwd(dyn, y, u, z, dskip_e, gn, tb)
    lam0 = jnp.zeros((2, D_STATE, d), F32)
    late_grads = dict(w_out=g_wout, w_gate=g_wgate, w_up=g_wup, w_down=g_wdown, pool_w=g_pw)
    pairs = ()
    if late_shards is not None:
        slabs = [_to_slabs(nme, late_grads[nme]) for nme in _LATE]
        pairs = _pair_add(slabs, _pair_swap(slabs, "reduce_g_d2d_late"), core, "reduce_g_pair_late")
    dxs, dbc, ddt, accs, lam_c, *arrived = _ssd_bwd(u, xd, bias2, a2, rexp, rexp_t, dy, hprev, lam0, "ssd_bwd",
                                                    exchange=pairs)
    du, accb = _conv_bwd_a(dxs, dy, dskip_e, dbc, u, tb, "conv_bwd_a")
    dxd, accw = _conv_bwd_b(du, xd, ddt, w8, tb, "conv_bwd_b")

    lc = ctx.shape[0]
    zeros_c = jnp.zeros((lc, d), F32)
    dxs_c, dbc_c, ddt_c, accs_c, _ = _ssd_bwd(uc, xdc, bias2, a2, rexp, rexp_t, zeros_c, hprev_c, lam_c, "ssd_bwd_ctx")
    du_c, accb_c = _conv_bwd_a(dxs_c, zeros_c, dskip_e, dbc_c, uc, tbc, "conv_bwd_a_ctx")
    dxd_c, accw_c = _conv_bwd_b(du_c, xdc, ddt_c, w8, tbc, "conv_bwd_b_ctx")
    dhc = _matmul_nn([(dxd_c, wxd.T)], F32, 512, 1024, "d_hc")
    _, acc0c = _bwd_ln0(None, dhc, ctx, emb_g, emb_b, sc1c, tbc, "bwd_ln0_ctx")

    dh1 = _matmul_nn([(dz, wz.T), (dxd, wxd.T), (dup, wup.T)], F32, MM_ROWS, 1024, "d_h1")
    g_wz = _matmul_tn(h1, dz, MM_ROWS, 1024, "g_in_proj_z")
    g_wxd = _matmul_tn(h1, dxd, MM_ROWS, D_XD, "g_in_proj_xd") + _matmul_tn(hc, dxd_c, 512, D_XD, "g_in_proj_xd_ctx")
    g_wpo = _matmul_tn(h1, dup, MM_ROWS, 1024, "g_in_proj_up")
    g_win = jnp.concatenate([g_wz, g_wxd[:, 0:D_XBC + 2 * SSD_HEADS], g_wpo], axis=1)
    grad_x, acc0 = _bwd_ln0(dr1, dh1, x, emb_g, emb_b, sc1, tb, "bwd_ln0")

    zero_d = jnp.zeros((1, d), F32)
    dmod = jnp.concatenate([acc0[1:2], acc0[0:1], acc1[4:5], acc1[1:2], acc1[0:1], acc2[2:3]], axis=1)
    dmodc = jnp.concatenate([acc0c[1:2], acc0c[0:1]] + [zero_d] * 4, axis=1)

    big = dict(in_proj=g_win)
    if late_shards is None:
        big.update(late_grads)
    sml = dict(
        dmod=dmod, dmod_ctx=dmodc, emb_ln_g=acc0[2] + acc0c[2], emb_ln_b=acc0[3] + acc0c[3],
        conv_w=accw[0:D_CONV] + accw_c[0:D_CONV], conv_b=accb[0] + accb_c[0],
        dt_bias=accs[:, 0, 0:SSD_HEADS] + accs_c[:, 0, 0:SSD_HEADS],
        a_log=accs[:, 1, 0:SSD_HEADS] + accs_c[:, 1, 0:SSD_HEADS],
        d_skip=jnp.sum(accm[1].reshape(SSD_HEADS, HEAD_DIM), axis=1),
        ssd_norm_g=accm[0], pool_scale=accp[0], ln1_g=acc1[2], ln1_b=acc1[3], ln2_g=acc2[0], ln2_b=acc2[1])
    return loss, grad_x, big, sml, (arrived if late_shards is not None else None)


_SMALL = ("c_ctx", "emb_ln_g", "emb_ln_b", "b_ada", "conv_w", "conv_b", "dt_bias", "a_log", "d_skip",
          "ssd_norm_g", "pool_scale", "ln1_g", "ln1_b", "ln2_g", "ln2_b")


def _small_rows(size):
    return -(-size // 1024)


def _pack_small(vals, names):
    pieces, rows = [], 0
    for nme in names:
        flat = vals[nme].reshape(-1).astype(F32)
        nr = _small_rows(flat.shape[0])
        pieces.append(flat)
        if nr * 1024 > flat.shape[0]:
            pieces.append(jnp.zeros((nr * 1024 - flat.shape[0],), F32))
        rows += nr
    if rows % 8:
        pieces.append(jnp.zeros(((8 - rows % 8) * 1024,), F32))
    return jnp.concatenate(pieces).reshape(-1, 1024)


def _small_offsets(shapes, names):
    out, off = {}, 0
    for nme in names:
        nr = _small_rows(math.prod(shapes[nme]))
        out[nme] = (off, nr)
        off += nr
    return out


def _unpack_small(packed, shapes, names):
    out = {}
    for nme, (off, nr) in _small_offsets(shapes, names).items():
        out[nme] = packed[off:off + nr].reshape(-1)[:math.prod(shapes[nme])].reshape(shapes[nme])
    return out


_WEIGHT_ORDER = ("c_ctx", "emb_ln_g", "emb_ln_b", "w_ada", "b_ada", "in_proj", "conv_w", "conv_b", "dt_bias", "a_log",
                 "d_skip", "ssd_norm_g", "pool_w", "pool_scale", "w_out", "ln1_g", "ln1_b", "w_gate", "w_up", "w_down",
                 "ln2_g", "ln2_b")


def _as2d(a):
    return a.reshape(-1, a.shape[-1])


def kernel(x, c, ctx, c_ctx, emb_ln_g, emb_ln_b, w_ada, b_ada, in_proj, conv_w, conv_b, dt_bias, a_log, d_skip, ssd_norm_g, pool_w, pool_scale, w_out, ln1_g, ln1_b, w_gate, w_up, w_down, ln2_g, ln2_b, loss_target, m_c_ctx, m_emb_ln_g, m_emb_ln_b, m_w_ada, m_b_ada, m_in_proj, m_conv_w, m_conv_b, m_dt_bias, m_a_log, m_d_skip, m_ssd_norm_g, m_pool_w, m_pool_scale, m_w_out, m_ln1_g, m_ln1_b, m_w_gate, m_w_up, m_w_down, m_ln2_g, m_ln2_b, v_c_ctx, v_emb_ln_g, v_emb_ln_b, v_w_ada, v_b_ada, v_in_proj, v_conv_w, v_conv_b, v_dt_bias, v_a_log, v_d_skip, v_ssd_norm_g, v_pool_w, v_pool_scale, v_w_out, v_ln1_g, v_ln1_b, v_w_gate, v_w_up, v_w_down, v_ln2_g, v_ln2_b):
    w = dict(c_ctx=c_ctx, emb_ln_g=emb_ln_g, emb_ln_b=emb_ln_b, w_ada=w_ada, b_ada=b_ada, in_proj=in_proj, conv_w=conv_w,
             conv_b=conv_b, dt_bias=dt_bias, a_log=a_log, d_skip=d_skip, ssd_norm_g=ssd_norm_g, pool_w=pool_w,
             pool_scale=pool_scale, w_out=w_out, ln1_g=ln1_g, ln1_b=ln1_b, w_gate=w_gate, w_up=w_up, w_down=w_down,
             ln2_g=ln2_g, ln2_b=ln2_b)
    m = dict(c_ctx=m_c_ctx, emb_ln_g=m_emb_ln_g, emb_ln_b=m_emb_ln_b, w_ada=m_w_ada, b_ada=m_b_ada, in_proj=m_in_proj,
             conv_w=m_conv_w, conv_b=m_conv_b, dt_bias=m_dt_bias, a_log=m_a_log, d_skip=m_d_skip,
             ssd_norm_g=m_ssd_norm_g, pool_w=m_pool_w, pool_scale=m_pool_scale, w_out=m_w_out, ln1_g=m_ln1_g,
             ln1_b=m_ln1_b, w_gate=m_w_gate, w_up=m_w_up, w_down=m_w_down, ln2_g=m_ln2_g, ln2_b=m_ln2_b)
    v = dict(c_ctx=v_c_ctx, emb_ln_g=v_emb_ln_g, emb_ln_b=v_emb_ln_b, w_ada=v_w_ada, b_ada=v_b_ada, in_proj=v_in_proj,
             conv_w=v_conv_w, conv_b=v_conv_b, dt_bias=v_dt_bias, a_log=v_a_log, d_skip=v_d_skip,
             ssd_norm_g=v_ssd_norm_g, pool_w=v_pool_w, pool_scale=v_pool_scale, w_out=v_w_out, ln1_g=v_ln1_g,
             ln1_b=v_ln1_b, w_gate=v_w_gate, w_up=v_w_up, w_down=v_w_down, ln2_g=v_ln2_g, ln2_b=v_ln2_b)

    xi, yi, ci = _mesh_pos()
    chip = 2 * xi + yi

    dev = 4 * xi + 2 * yi + ci
    d = D_MODEL
    core = ci.reshape(1).astype(jnp.int32)

    crow = jnp.concatenate([c.reshape(1, d), jnp.zeros((7, d), F32)], axis=0)
    _, c_all = _allreduce_small(crow, "gather_c")
    c16 = jnp.concatenate([c_all[:, 0, :], c_ctx.reshape(1, d), jnp.zeros((MOD_ROWS - 9, d), F32)], axis=0)
    ncol = w_ada.shape[-1]
    wada_bf = w_ada[0].astype(BF16)
    b_mine = lax.dynamic_slice_in_dim(b_ada, chip * ncol, ncol, axis=1)
    mods4 = _chip_bcast(_mods_fwd(c16, wada_bf, b_mine), "gather_mods")
    mods = mods4.transpose(1, 0, 2).reshape(MOD_ROWS, 4 * ncol)
    mod = lax.dynamic_slice_in_dim(mods, dev, 1, axis=0)
    mod_ctx = mods[8:9]

    shard = {name: w[name][0].astype(BF16).reshape(shp) for name, shp, _ in _BIG}
    conv8 = jnp.concatenate([conv_w[0], jnp.zeros((8 - D_CONV, conv_w.shape[-1]), F32)], axis=0)
    *gathered, conv4 = _gather_weights([shard[nme] for nme in _EARLY], conv8)
    wts = {nme: _assemble(nme, t) for nme, t in zip(_EARLY, gathered)}
    w8 = conv4.transpose(1, 0, 2).reshape(8, D_XBC)
    small = {nme: (w[nme] if nme in ("c_ctx", "emb_ln_g", "emb_ln_b") else w[nme][0]) for nme in _SMALL if nme != "conv_w"}

    loss, grad_x, big, sml, late_parts = _device_step(x[0], mod, mod_ctx, ctx[0], loss_target[0], wts, w8, small, 512,
                                                      late_shards=[shard[nme] for nme in _LATE], core=core)
    loss = lax.psum(loss, ("x", "y", "c"))

    g_big = _reduce_grads(big, late_parts, core)
    reduced = tuple(sml)
    small_shapes = {nme: sml[nme].shape for nme in reduced}
    total, each = _allreduce_small(_pack_small(sml, reduced), "reduce_small")
    g_small = _unpack_small(total, small_shapes, reduced)
    cw_cols = conv_w.shape[-1]
    g_small["conv_w"] = lax.dynamic_slice_in_dim(g_small["conv_w"], chip * cw_cols, cw_cols, axis=1)

    off, nr = _small_offsets(small_shapes, reduced)["dmod"]
    dm16 = jnp.concatenate([each[:, off:off + nr, :].reshape(8, nr * 1024)[:, :6 * d], g_small["dmod_ctx"],
                            jnp.zeros((MOD_ROWS - 9, 6 * d), F32)], axis=0)
    dm_mine = lax.dynamic_slice_in_dim(dm16, chip * ncol, ncol, axis=1)
    g_wada = _mods_bwd_w(c16.T, dm_mine)
    g_small["b_ada"] = _mods_bwd_b(dm16)[0:1]
    c_part = _mods_bwd_c(dm_mine, wada_bf, c16)[8:16]
    g_small["c_ctx"] = _allreduce_small(c_part, "reduce_c_ctx")[0][0]

    grads, delta, new_m, new_v = {}, {}, {}, {}
    grads["w_ada"] = g_wada[None]
    delta["w_ada"], new_m["w_ada"], new_v["w_ada"] = (
        t[None] for t in _adamw(w_ada[0], g_wada, m_w_ada[0], v_w_ada[0], "adamw_w_ada"))
    for name, _, _ in _BIG:
        g2 = _as2d(g_big[name])
        d2, m2, v2 = _adamw(_as2d(w[name][0]), g2, _as2d(m[name][0]), _as2d(v[name][0]), "adamw_" + name)
        grads[name] = g2.reshape(w[name].shape)
        delta[name], new_m[name], new_v[name] = (t.reshape(w[name].shape) for t in (d2, m2, v2))
    shp = {nme: w[nme].shape for nme in _SMALL}
    gp = _pack_small(g_small, _SMALL)
    dp, mp, vp = _adamw(_pack_small(w, _SMALL), gp, _pack_small(m, _SMALL), _pack_small(v, _SMALL), "adamw_small")
    for dst, src in ((grads, gp), (delta, dp), (new_m, mp), (new_v, vp)):
        dst.update(_unpack_small(src, shp, _SMALL))

    return (loss, grad_x[None], *[grads[nme] for nme in _WEIGHT_ORDER], *[delta[nme] for nme in _WEIGHT_ORDER],
            *[new_m[nme] for nme in _WEIGHT_ORDER], *[new_v[nme] for nme in _WEIGHT_ORDER])
```

```python
import functools
import math

import jax
import jax.numpy as jnp
from jax import lax
from jax.experimental import pallas as pl
from jax.experimental.pallas import tpu as pltpu

F32 = jnp.float32
BF16 = jnp.bfloat16
MESH = pl.DeviceIdType.MESH

D_MODEL = 1024
SSD_HEADS = 16
HEAD_DIM = 64
D_STATE = 128
CHUNK = 128
D_CONV = 5
D_XBC = D_MODEL + 2 * D_STATE
D_XD = 1408
N_POOL = 4
POOL_DIM = 256
POOL_WINDOWS = (2, 4, 8, 16)
GRID_W = 64
D_FF = 2816
D_IN_PROJ = 3360
LN_EPS = 1e-5
ALPHA = 2.0 ** 0.25
POOL_TB = 512
MM_ROWS = 1024

ADAM_LR = 0.001
ADAM_B1 = 0.9
ADAM_B2 = 0.999
ADAM_EPS = 1e-08
ADAM_WD = 0.01
ADAM_STEP = 10

VMEM_LIMIT = 56 * 1024 * 1024


def _cp(sem=None):
    return pltpu.CompilerParams(dimension_semantics=sem, vmem_limit_bytes=VMEM_LIMIT)


def _sigmoid(x):
    return 1.0 / (1.0 + jnp.exp(-x))


def _silu(x):
    return x * _sigmoid(x)


def _dsilu(x):
    s = _sigmoid(x)
    return s * (1.0 + x * (1.0 - s))


def _softplus(x):
    t = jnp.exp(-jnp.abs(x))
    u = 1.0 + t
    log1p = jnp.where(u == 1.0, t, jnp.log(u) * t / (u - 1.0 + (u == 1.0)))
    return jnp.maximum(x, 0.0) + log1p


def _split(x, n):
    parts, r = [], x
    for _ in range(n):
        p = r.astype(BF16)
        parts.append(p)
        r = r - p.astype(F32)
    return parts


def _dot(a, b):
    return jnp.dot(a, b, preferred_element_type=F32)


def _dot_nt(a, b):
    return lax.dot_general(a, b, (((1,), (1,)), ((), ())), preferred_element_type=F32)


def _dot_tn(a, b):
    return lax.dot_general(a, b, (((0,), (0,)), ((), ())), preferred_element_type=F32)


def _dot_sel_l(sel_bf, x, n=3):
    out = None
    for p in _split(x, n):
        t = _dot(sel_bf, p)
        out = t if out is None else out + t
    return out


def _dot_sel_r(x, sel_bf, n=3):
    out = None
    for p in _split(x, n):
        t = _dot(p, sel_bf)
        out = t if out is None else out + t
    return out


ROW_SUB = 16


def _row_tiles(tb):
    assert tb % ROW_SUB == 0
    return [slice(s * ROW_SUB, (s + 1) * ROW_SUB) for s in range(tb // ROW_SUB)]


def _fold8(v):
    out = v[0:8, :]
    for r in range(8, v.shape[0], 8):
        out = out + v[r:r + 8, :]
    return out


def _row_block(n, cap=256, mult=8):
    best = None
    for t in range(mult, min(n, cap) + 1, mult):
        if n % t == 0:
            best = t
    return best if best is not None else n


def _vec(v):
    return v.reshape(1, -1).astype(F32)


MOD_ROWS = 16
MOD_TN = 512


def _mods_fwd(c16, w_bf, b):
    r, d = c16.shape
    n = w_bf.shape[1]

    def body(c_ref, w_ref, b_ref, o_ref):
        s = _silu(c_ref[...]).astype(BF16)
        o_ref[...] = _dot(s, w_ref[...]) + b_ref[...]

    return pl.pallas_call(
        body, name="mods_fwd", grid=(n // MOD_TN,),
        in_specs=[pl.BlockSpec((r, d), lambda j: (0, 0)),
                  pl.BlockSpec((d, MOD_TN), lambda j: (0, j)),
                  pl.BlockSpec((1, MOD_TN), lambda j: (0, j))],
        out_specs=pl.BlockSpec((r, MOD_TN), lambda j: (0, j)),
        out_shape=jax.ShapeDtypeStruct((r, n), F32),
        compiler_params=_cp(("arbitrary",)),
    )(c16, w_bf, b)


def _mods_bwd_w(ct16, dm16):
    d = ct16.shape[0]
    n = dm16.shape[1]

    def body(ct_ref, dm_ref, dw_ref):
        s = _silu(ct_ref[...])
        dm = dm_ref[...]
        acc = s[:, 0:1] * dm[0:1, :]
        for r in range(1, 9):
            acc = acc + s[:, r:r + 1] * dm[r:r + 1, :]
        dw_ref[...] = acc

    return pl.pallas_call(
        body, name="mods_bwd_w", grid=(n // MOD_TN,),
        in_specs=[pl.BlockSpec((d, MOD_ROWS), lambda j: (0, 0)),
                  pl.BlockSpec((MOD_ROWS, MOD_TN), lambda j: (0, j))],
        out_specs=pl.BlockSpec((d, MOD_TN), lambda j: (0, j)),
        out_shape=jax.ShapeDtypeStruct((d, n), F32),
        compiler_params=_cp(("arbitrary",)),
    )(ct16, dm16)


def _mods_bwd_c(dm16, w_bf, c16):
    d = c16.shape[1]
    n = dm16.shape[1]
    nk = n // MOD_TN

    def body(dm_ref, w_ref, c_ref, o_ref):
        k = pl.program_id(0)

        @pl.when(k == 0)
        def _():
            o_ref[...] = jnp.zeros_like(o_ref)

        o_ref[...] += _dot_nt(dm_ref[...].astype(BF16), w_ref[...])

        @pl.when(k == nk - 1)
        def _():
            o_ref[...] = o_ref[...] * (0.5 * _dsilu(c_ref[...]))

    return pl.pallas_call(
        body, name="mods_bwd_c", grid=(nk,),
        in_specs=[pl.BlockSpec((MOD_ROWS, MOD_TN), lambda k: (0, k)),
                  pl.BlockSpec((d, MOD_TN), lambda k: (0, k)),
                  pl.BlockSpec((MOD_ROWS, d), lambda k: (0, 0))],
        out_specs=pl.BlockSpec((MOD_ROWS, d), lambda k: (0, 0)),
        out_shape=jax.ShapeDtypeStruct((MOD_ROWS, d), F32),
        compiler_params=_cp(("arbitrary",)),
    )(dm16, w_bf, c16)


def _mods_bwd_b(dm16):
    n = dm16.shape[1]

    def body(dm_ref, o_ref):
        dm = dm_ref[...]
        acc = dm[0:1, :]
        for r in range(1, 9):
            acc = acc + dm[r:r + 1, :]
        o_ref[...] = jnp.broadcast_to(acc, (8, MOD_TN))

    return pl.pallas_call(
        body, name="mods_bwd_b", grid=(n // MOD_TN,),
        in_specs=[pl.BlockSpec((MOD_ROWS, MOD_TN), lambda j: (0, j))],
        out_specs=pl.BlockSpec((8, MOD_TN), lambda j: (0, j)),
        out_shape=jax.ShapeDtypeStruct((8, n), F32),
        compiler_params=_cp(("arbitrary",)),
    )(dm16)


def _ln_stats(x):
    mu = jnp.mean(x, axis=-1, keepdims=True)
    xc = x - mu
    var = jnp.mean(xc * xc, axis=-1, keepdims=True)
    rstd = lax.rsqrt(var + LN_EPS)
    return xc * rstd, rstd


def _ln_bwd(dxhat, xhat, rstd):
    m1 = jnp.mean(dxhat, axis=-1, keepdims=True)
    m2 = jnp.mean(dxhat * xhat, axis=-1, keepdims=True)
    return rstd * (dxhat - m1 - xhat * m2)


def _row_spec(tb, d):
    return pl.BlockSpec((tb, d), lambda i: (i, 0))


def _par_spec(d):
    return pl.BlockSpec((1, d), lambda i: (0, 0))


def _acc_spec(d):
    return pl.BlockSpec((8, d), lambda i: (0, 0))


def _ln_mod(x, g, b, sh, sc, tb, name):
    n, d = x.shape

    def body(x_ref, g_ref, b_ref, sh_ref, sc_ref, x0_ref, h_ref):
        g, b, sh, sc1 = g_ref[...], b_ref[...], sh_ref[...], 1.0 + sc_ref[...]
        for r in _row_tiles(min(tb, n)):
            xhat, _ = _ln_stats(x_ref[r, :])
            x0 = xhat * g + b
            x0_ref[r, :] = x0
            h_ref[r, :] = (x0 * sc1 + sh).astype(BF16)

    return pl.pallas_call(
        body, name=name, grid=(n // tb,),
        in_specs=[_row_spec(tb, d)] + [_par_spec(d)] * 4,
        out_specs=[_row_spec(tb, d), _row_spec(tb, d)],
        out_shape=[jax.ShapeDtypeStruct((n, d), F32), jax.ShapeDtypeStruct((n, d), BF16)],
        compiler_params=_cp(("parallel",)),
    )(x, g, b, sh, sc)


def _res_ln(xres, mix, gate, g, b, sh, sc, tb):
    n, d = xres.shape

    def body(xr_ref, mix_ref, gate_ref, g_ref, b_ref, sh_ref, sc_ref, x1_ref, h_ref):
        gate_v, g, b, sh, sc1 = gate_ref[...], g_ref[...], b_ref[...], sh_ref[...], 1.0 + sc_ref[...]
        for r in _row_tiles(tb):
            xhat, _ = _ln_stats(ALPHA * xr_ref[r, :] + gate_v * mix_ref[r, :])
            x1 = xhat * g + b
            x1_ref[r, :] = x1
            h_ref[r, :] = (x1 * sc1 + sh).astype(BF16)

    return pl.pallas_call(
        body, name="res_ln1", grid=(n // tb,),
        in_specs=[_row_spec(tb, d)] * 2 + [_par_spec(d)] * 5,
        out_specs=[_row_spec(tb, d), _row_spec(tb, d)],
        out_shape=[jax.ShapeDtypeStruct((n, d), F32), jax.ShapeDtypeStruct((n, d), BF16)],
        compiler_params=_cp(("parallel",)),
    )(xres, mix, gate, g, b, sh, sc)


def _final_ln_loss(x1, ffn, gate, g, b, target, tb):
    n, d = x1.shape

    def body(x1_ref, ffn_ref, gate_ref, g_ref, b_ref, t_ref, dffn_ref, dr_ref, acc_ref):
        i = pl.program_id(0)

        @pl.when(i == 0)
        def _():
            acc_ref[...] = jnp.zeros_like(acc_ref)

        gate_v, g, b = gate_ref[...], g_ref[...], b_ref[...]
        parts = [jnp.zeros((8, d), F32)] * 4
        for r in _row_tiles(tb):
            ffn = ffn_ref[r, :]
            xhat, rstd = _ln_stats(ALPHA * x1_ref[r, :] + gate_v * ffn)
            err = xhat * g + b - t_ref[r, :]
            dx2 = err * (1.0 / d)
            dr = _ln_bwd(dx2 * g, xhat, rstd)
            dr_ref[r, :] = dr
            dffn_ref[r, :] = (gate_v * dr).astype(BF16)
            terms = (dx2 * xhat, dx2, dr * ffn, err * err)
            parts = [p + _fold8(t) for p, t in zip(parts, terms)]
        for j, p in enumerate(parts):
            acc_ref[j:j + 1, :] += jnp.sum(p, axis=0, keepdims=True)

    return pl.pallas_call(
        body, name="final_ln_loss", grid=(n // tb,),
        in_specs=[_row_spec(tb, d)] * 2 + [_par_spec(d)] * 3 + [_row_spec(tb, d)],
        out_specs=[_row_spec(tb, d), _row_spec(tb, d), _acc_spec(d)],
        out_shape=[jax.ShapeDtypeStruct((n, d), BF16), jax.ShapeDtypeStruct((n, d), F32),
                   jax.ShapeDtypeStruct((8, d), F32)],
        compiler_params=_cp(("arbitrary",)),
    )(x1, ffn, gate, g, b, target)


def _bwd_ln1(dr2, dh2, x1, x0, mix, gate, sc2, g, tb):
    n, d = x1.shape

    def body(dr2_ref, dh2_ref, x1_ref, x0_ref, mix_ref, gate_ref, sc_ref, g_ref, dmix_ref, dr1_ref, acc_ref):
        i = pl.program_id(0)

        @pl.when(i == 0)
        def _():
            acc_ref[...] = jnp.zeros_like(acc_ref)

        gate_v, g, sc1 = gate_ref[...], g_ref[...], 1.0 + sc_ref[...]
        parts = [jnp.zeros((8, d), F32)] * 5
        for r in _row_tiles(tb):
            dh2 = dh2_ref[r, :]
            mix = mix_ref[r, :]
            dx1 = ALPHA * dr2_ref[r, :] + dh2 * sc1
            xhat, rstd = _ln_stats(ALPHA * x0_ref[r, :] + gate_v * mix)
            dr1 = _ln_bwd(dx1 * g, xhat, rstd)
            dr1_ref[r, :] = dr1
            dmix_ref[r, :] = (gate_v * dr1).astype(BF16)
            terms = (dh2 * x1_ref[r, :], dh2, dx1 * xhat, dx1, dr1 * mix)
            parts = [p + _fold8(t) for p, t in zip(parts, terms)]
        for j, p in enumerate(parts):
            acc_ref[j:j + 1, :] += jnp.sum(p, axis=0, keepdims=True)

    return pl.pallas_call(
        body, name="bwd_ln1", grid=(n // tb,),
        in_specs=[_row_spec(tb, d)] * 5 + [_par_spec(d)] * 3,
        out_specs=[_row_spec(tb, d), _row_spec(tb, d), _acc_spec(d)],
        out_shape=[jax.ShapeDtypeStruct((n, d), BF16), jax.ShapeDtypeStruct((n, d), F32),
                   jax.ShapeDtypeStruct((8, d), F32)],
        compiler_params=_cp(("arbitrary",)),
    )(dr2, dh2, x1, x0, mix, gate, sc2, g)


def _bwd_ln0(dres, dh, x, g, b, sc, tb, name):
    n, d = x.shape
    has_res = dres is not None

    def body(*refs):
        if has_res:
            dres_ref, dh_ref, x_ref, g_ref, b_ref, sc_ref, dx_ref, acc_ref = refs
        else:
            dh_ref, x_ref, g_ref, b_ref, sc_ref, dx_ref, acc_ref = refs
        i = pl.program_id(0)

        @pl.when(i == 0)
        def _():
            acc_ref[...] = jnp.zeros_like(acc_ref)

        g, b, sc1 = g_ref[...], b_ref[...], 1.0 + sc_ref[...]
        parts = [jnp.zeros((8, d), F32)] * 4
        for r in _row_tiles(tb):
            dh = dh_ref[r, :]
            xhat, rstd = _ln_stats(x_ref[r, :])
            x0 = xhat * g + b
            dx0 = dh * sc1
            if has_res:
                dx0 = dx0 + ALPHA * dres_ref[r, :]
            dx_ref[r, :] = _ln_bwd(dx0 * g, xhat, rstd)
            terms = (dh * x0, dh, dx0 * xhat, dx0)
            parts = [p + _fold8(t) for p, t in zip(parts, terms)]
        for j, p in enumerate(parts):
            acc_ref[j:j + 1, :] += jnp.sum(p, axis=0, keepdims=True)

    ins = ([dres] if has_res else []) + [dh, x, g, b, sc]
    return pl.pallas_call(
        body, name=name, grid=(n // tb,),
        in_specs=[_row_spec(tb, d)] * (3 if has_res else 2) + [_par_spec(d)] * 3,
        out_specs=[_row_spec(tb, d), _acc_spec(d)],
        out_shape=[jax.ShapeDtypeStruct((n, d), F32), jax.ShapeDtypeStruct((8, d), F32)],
        compiler_params=_cp(("arbitrary",)),
    )(*ins)


def _matmul_nn(pairs, out_dtype, tm, tn, name):
    m = pairs[0][0].shape[0]
    n = pairs[0][1].shape[1]
    tm = min(tm, m)
    tn = min(tn, n)
    npair = len(pairs)

    def body(*refs):
        o_ref = refs[-1]
        acc = None
        for p in range(npair):
            t = _dot(refs[2 * p][...].astype(BF16), refs[2 * p + 1][...])
            acc = t if acc is None else acc + t
        o_ref[...] = acc.astype(out_dtype)

    in_specs, args = [], []
    for a, b in pairs:
        k = a.shape[1]
        in_specs += [pl.BlockSpec((tm, k), lambda i, j: (i, 0)), pl.BlockSpec((k, tn), lambda i, j: (0, j))]
        args += [a, b]
    return pl.pallas_call(
        body, name=name, grid=(m // tm, n // tn),
        in_specs=in_specs,
        out_specs=pl.BlockSpec((tm, tn), lambda i, j: (i, j)),
        out_shape=jax.ShapeDtypeStruct((m, n), out_dtype),
        compiler_params=_cp(("parallel", "arbitrary")),
    )(*args)


def _matmul_tn(a, g, tm, tn, name):
    m, k = a.shape
    n = g.shape[1]
    tm = min(tm, m)
    tn = min(tn, n)

    def body(a_ref, g_ref, o_ref):
        i = pl.program_id(1)

        @pl.when(i == 0)
        def _():
            o_ref[...] = jnp.zeros_like(o_ref)

        o_ref[...] += _dot_tn(a_ref[...].astype(BF16), g_ref[...].astype(BF16))

    return pl.pallas_call(
        body, name=name, grid=(n // tn, m // tm),
        in_specs=[pl.BlockSpec((tm, k), lambda j, i: (i, 0)), pl.BlockSpec((tm, tn), lambda j, i: (i, j))],
        out_specs=pl.BlockSpec((k, tn), lambda j, i: (0, j)),
        out_shape=jax.ShapeDtypeStruct((k, n), F32),
        compiler_params=_cp(("parallel", "arbitrary")),
    )(a, g)


def _swiglu_fwd(h, wg, wu, tm, tn):
    m, k = h.shape
    n = wg.shape[1]
    tm = min(tm, m)

    def body(h_ref, wg_ref, wu_ref, gate_ref, up_ref, hmid_ref):
        hv = h_ref[...]
        gate = _dot(hv, wg_ref[...])
        up = _dot(hv, wu_ref[...])
        gate_ref[...] = gate.astype(BF16)
        up_ref[...] = up.astype(BF16)
        hmid_ref[...] = (_silu(gate) * up).astype(BF16)

    blk = pl.BlockSpec((tm, tn), lambda i, j: (i, j))
    wspec = pl.BlockSpec((k, tn), lambda i, j: (0, j))
    return pl.pallas_call(
        body, name="swiglu_fwd", grid=(m // tm, n // tn),
        in_specs=[pl.BlockSpec((tm, k), lambda i, j: (i, 0)), wspec, wspec],
        out_specs=[blk, blk, blk],
        out_shape=[jax.ShapeDtypeStruct((m, n), BF16), jax.ShapeDtypeStruct((m, n), BF16),
                   jax.ShapeDtypeStruct((m, n), BF16)],
        compiler_params=_cp(("parallel", "arbitrary")),
    )(h, wg, wu)


def _swiglu_bwd(dffn, wdt, gate, up, tm, tn):
    m, k = dffn.shape
    n = wdt.shape[1]
    tm = min(tm, m)

    def body(d_ref, w_ref, gate_ref, up_ref, dg_ref, du_ref):
        dh = _dot(d_ref[...], w_ref[...])
        gate = gate_ref[...].astype(F32)
        dg_ref[...] = (dh * up_ref[...].astype(F32) * _dsilu(gate)).astype(BF16)
        du_ref[...] = (dh * _silu(gate)).astype(BF16)

    blk = pl.BlockSpec((tm, tn), lambda i, j: (i, j))
    return pl.pallas_call(
        body, name="swiglu_bwd", grid=(m // tm, n // tn),
        in_specs=[pl.BlockSpec((tm, k), lambda i, j: (i, 0)), pl.BlockSpec((k, tn), lambda i, j: (0, j)), blk, blk],
        out_specs=[blk, blk],
        out_shape=[jax.ShapeDtypeStruct((m, n), BF16), jax.ShapeDtypeStruct((m, n), BF16)],
        compiler_params=_cp(("parallel", "arbitrary")),
    )(dffn, wdt, gate, up)


def _halo_specs(tb, width, nrows):
    r8 = tb // 8
    last = nrows // 8 - 1
    prev = pl.BlockSpec((8, width), lambda i: (jnp.maximum(i * r8 - 1, 0), 0))
    nxt = pl.BlockSpec((8, width), lambda i: (jnp.minimum((i + 1) * r8, last), 0))
    return prev, nxt


CONV_SUB = 32


def _halo_scratch():
    return [pltpu.VMEM((CONV_SUB + 16, D_XBC), F32), pltpu.VMEM((CONV_SUB + 16, D_XBC), F32)]


def _shifted_rows(prev_ref, cur_ref, next_ref, top, bot, tb, i, nb):
    sub = CONV_SUB
    nsub = tb // sub
    assert nsub >= 2
    top[0:8, :] = prev_ref[...] * (i > 0).astype(F32)
    top[8:sub + 16, :] = cur_ref[0:sub + 8, :]
    bot[0:sub + 8, :] = cur_ref[tb - sub - 8:tb, :]
    bot[sub + 8:sub + 16, :] = next_ref[...] * (i < nb - 1).astype(F32)

    def rows(s, o):
        if s == 0:
            return top[8 + o:8 + o + sub, :]
        if s == nsub - 1:
            return bot[8 + o:8 + o + sub, :]
        return cur_ref[s * sub + o:(s + 1) * sub + o, :]

    return rows


def _conv_fwd(xd, w8, b, tb, name):
    n = xd.shape[0]
    tb = min(tb, n)
    nb = n // tb
    prev, nxt = _halo_specs(tb, D_XBC, n)

    def body(p_ref, c_ref, n_ref, w_ref, b_ref, u_ref, top, bot):
        i = pl.program_id(0)
        rows = _shifted_rows(p_ref, c_ref, n_ref, top, bot, tb, i, nb)
        w = [w_ref[k:k + 1, :] for k in range(D_CONV)]
        bias = jnp.broadcast_to(b_ref[...], (CONV_SUB, D_XBC))
        for s in range(tb // CONV_SUB):
            acc = bias
            for k in range(D_CONV):
                acc = acc + w[k] * rows(s, k - 2)
            u_ref[s * CONV_SUB:(s + 1) * CONV_SUB, :] = acc.astype(BF16)

    return pl.pallas_call(
        body, name=name, grid=(nb,),
        in_specs=[prev, pl.BlockSpec((tb, D_XBC), lambda i: (i, 0)), nxt,
                  pl.BlockSpec((8, D_XBC), lambda i: (0, 0)), _par_spec(D_XBC)],
        out_specs=_row_spec(tb, D_XBC),
        out_shape=jax.ShapeDtypeStruct((n, D_XBC), BF16),
        scratch_shapes=_halo_scratch(),
        compiler_params=_cp(("parallel",)),
    )(xd, xd, xd, w8, b)


def _conv_bwd_a(dxs, dy, dskip_e, dbc, u, tb, name):
    n = u.shape[0]
    tb = min(tb, n)

    def body(dxs_ref, dy_ref, sk_ref, dbc_ref, u_ref, du_ref, acc_ref):
        i = pl.program_id(0)

        @pl.when(i == 0)
        def _():
            acc_ref[...] = jnp.zeros_like(acc_ref)

        sk = sk_ref[...]
        part = jnp.zeros((8, D_XBC), F32)
        for r in _row_tiles(tb):
            gx = dxs_ref[0, r, :].astype(F32) + dxs_ref[1, r, :].astype(F32) + dy_ref[r, :].astype(F32) * sk
            gbc = dbc_ref[0, r, :] + dbc_ref[1, r, :]
            du = jnp.concatenate([gx, gbc], axis=1) * _dsilu(u_ref[r, :].astype(F32))
            du_ref[r, :] = du
            part = part + _fold8(du)
        acc_ref[0:1, :] += jnp.sum(part, axis=0, keepdims=True)

    return pl.pallas_call(
        body, name=name, grid=(n // tb,),
        in_specs=[pl.BlockSpec((2, tb, D_MODEL), lambda i: (0, i, 0)), _row_spec(tb, D_MODEL), _par_spec(D_MODEL),
                  pl.BlockSpec((2, tb, 2 * D_STATE), lambda i: (0, i, 0)), _row_spec(tb, D_XBC)],
        out_specs=[_row_spec(tb, D_XBC), _acc_spec(D_XBC)],
        out_shape=[jax.ShapeDtypeStruct((n, D_XBC), F32), jax.ShapeDtypeStruct((8, D_XBC), F32)],
        compiler_params=_cp(("arbitrary",)),
    )(dxs, dy, dskip_e, dbc, u)


def _conv_bwd_b(du, xd, ddt, w8, tb, name):
    n = du.shape[0]
    tb = min(tb, n)
    nb = n // tb
    prev, nxt = _halo_specs(tb, D_XBC, n)

    def body(dp_ref, dc_ref, dn_ref, xp_ref, xc_ref, xn_ref, ddt_ref, w_ref, dxd_ref, acc_ref, dtop, dbot, xtop, xbot):
        i = pl.program_id(0)

        @pl.when(i == 0)
        def _():
            acc_ref[...] = jnp.zeros_like(acc_ref)

        sub = CONV_SUB
        nsub = tb // sub
        du_rows = _shifted_rows(dp_ref, dc_ref, dn_ref, dtop, dbot, tb, i, nb)
        x_rows = _shifted_rows(xp_ref, xc_ref, xn_ref, xtop, xbot, tb, i, nb)
        w = [w_ref[k:k + 1, :] for k in range(D_CONV)]
        for s in range(nsub):
            acc = w[0] * du_rows(s, 2)
            for k in range(1, D_CONV):
                acc = acc + w[k] * du_rows(s, 2 - k)
            dxd_ref[s * sub:(s + 1) * sub, 0:D_XBC] = acc.astype(BF16)
        for k in range(D_CONV):
            part = jnp.zeros((8, D_XBC), F32)
            for s in range(nsub):
                prod = dc_ref[s * sub:(s + 1) * sub, :] * x_rows(s, k - 2)
                for r in range(0, sub, 8):
                    part = part + prod[r:r + 8, :]
            acc_ref[k:k + 1, :] += jnp.sum(part, axis=0, keepdims=True)
        ddt = ddt_ref[0] + pltpu.roll(ddt_ref[1], SSD_HEADS, 1)
        dxd_ref[:, D_XBC:D_XD] = ddt.astype(BF16)

    cur = pl.BlockSpec((tb, D_XBC), lambda i: (i, 0))
    return pl.pallas_call(
        body, name=name, grid=(nb,),
        in_specs=[prev, cur, nxt, prev, cur, nxt,
                  pl.BlockSpec((2, tb, 128), lambda i: (0, i, 0)), pl.BlockSpec((8, D_XBC), lambda i: (0, 0))],
        out_specs=[_row_spec(tb, D_XD), _acc_spec(D_XBC)],
        out_shape=[jax.ShapeDtypeStruct((n, D_XD), BF16), jax.ShapeDtypeStruct((8, D_XBC), F32)],
        scratch_shapes=_halo_scratch() + _halo_scratch(),
        compiler_params=_cp(("arbitrary",)),
    )(du, du, du, xd, xd, xd, ddt, w8)


def _ssd_chunk_index(nc, reverse):
    def idx(d, k):
        kk = (nc - 1 - k) if reverse else k
        return kk + d * (nc - 1 - 2 * kk)
    return idx


def _ssd_prologue(d, u_ref, xd_ref, bias_ref, a_ref, r_ref):
    q = CHUNK
    xbc = _silu(u_ref[...].astype(F32))
    xs = xbc[:, 0:D_MODEL]
    bm = xbc[:, D_MODEL:D_MODEL + D_STATE]
    cm = xbc[:, D_MODEL + D_STATE:D_XBC]
    row = lax.broadcasted_iota(jnp.int32, (q, q), 0)
    col = lax.broadcasted_iota(jnp.int32, (q, q), 1)
    sgn = 1 - 2 * d
    mask = ((row - col) * sgn) >= 0
    mask_t = ((row - col) * sgn) <= 0
    xdv = xd_ref[...]
    dtraw = jnp.where(d == 0, xdv, pltpu.roll(xdv, 128 - SSD_HEADS, 1)) + bias_ref[...]
    head_lane = col < SSD_HEADS
    dt = jnp.where(head_lane, _softplus(dtraw), 0.0)
    a = a_ref[...]
    tri = jnp.where(mask, 1.0, 0.0).astype(BF16)
    acum = _dot_sel_l(tri, dt * a)
    rexp = r_ref[...]
    alast = jnp.where(d == 0, acum[q - 1:q, :], acum[0:1, :])
    e16 = jnp.exp(acum)
    dend16 = jnp.exp(alast - acum)
    wend16 = dend16 * dt
    e = _dot_sel_r(e16, rexp, n=2)
    wend_e = _dot_sel_r(wend16, rexp, n=2)
    elast_e = jnp.where(d == 0, e[q - 1:q, :], e[0:1, :])
    g = _dot_nt(cm.astype(BF16), bm.astype(BF16))
    return dict(xs=xs, bm=bm, cm=cm, mask=mask, mask_t=mask_t, dtraw=dtraw, head_lane=head_lane, dt=dt, a=a,
                acum=acum, acum_t=acum.T, dt_t=dt.T, e16=e16, dend16=dend16, wend16=wend16, e=e, wend_e=wend_e,
                elast_e=elast_e, g=g, col=col, row=row)


def _ssd_head_mats(p, h):
    seg = p["acum"][:, h:h + 1] - p["acum_t"][h:h + 1, :]
    lm = jnp.exp(jnp.where(p["mask"], seg, -jnp.inf))
    gl = p["g"] * lm
    s = gl * p["dt_t"][h:h + 1, :]
    return lm, gl, s


def _ssd_fwd(u, xd, bias2, a2, rexp, h0, name, gather=()):
    n = u.shape[0]
    nc = n // CHUNK
    q = CHUNK
    cidx = _ssd_chunk_index(nc, reverse=False)
    ng = len(gather)

    def body(u_ref, xd_ref, bias_ref, a_ref, r_ref, h0_ref, *rest):
        g_ins, (y_ref, hp_ref, hf_ref), rest = rest[:ng], rest[ng:ng + 3], rest[ng + 3:]
        g_outs, st, sems = rest[:ng], rest[ng], rest[ng + 1:]
        d = pl.program_id(0)
        k = pl.program_id(1)
        if ng:
            g_start, g_finish = _gather_steps(g_ins, g_outs, *sems)
            pl.when((d == 0) & (k == 0))(g_start)

        @pl.when(k == 0)
        def _():
            st[...] = h0_ref[...]

        p = _ssd_prologue(d, u_ref, xd_ref, bias_ref, a_ref, r_ref)
        stv = st[...]
        st_bf = stv.astype(BF16)
        hp_ref[...] = st_bf
        xs = p["xs"]
        lane128 = p["col"]
        y_off = _dot(p["cm"].astype(BF16), st_bf) * p["e"]
        for pb in range(SSD_HEADS // 2):
            _, _, s0 = _ssd_head_mats(p, 2 * pb)
            _, _, s1 = _ssd_head_mats(p, 2 * pb + 1)
            xp = xs[:, pb * 128:(pb + 1) * 128]
            rhs = jnp.concatenate([jnp.where(lane128 < HEAD_DIM, xp, 0.0), jnp.where(lane128 >= HEAD_DIM, xp, 0.0)],
                                  axis=0).astype(BF16)
            lhs = jnp.concatenate([s0, s1], axis=1).astype(BF16)
            y_ref[:, pb * 128:(pb + 1) * 128] = (_dot(lhs, rhs) + y_off[:, pb * 128:(pb + 1) * 128]).astype(BF16)
        xw = (xs * p["wend_e"]).astype(BF16)
        new = stv * p["elast_e"] + _dot(p["bm"].T.astype(BF16), xw)
        st[...] = new
        hf_ref[...] = new
        if ng:
            pl.when((d == 1) & (k == nc - 1))(g_finish)

    nsem = _GATHER_SEMS * ng
    return pl.pallas_call(
        body, name=name, grid=(2, nc),
        in_specs=[pl.BlockSpec((q, D_XBC), lambda d, k: (cidx(d, k), 0)),
                  pl.BlockSpec((q, 128), lambda d, k: (cidx(d, k), D_XBC // 128)),
                  pl.BlockSpec((None, 1, 128), lambda d, k: (d, 0, 0)),
                  pl.BlockSpec((None, 1, 128), lambda d, k: (d, 0, 0)),
                  pl.BlockSpec((128, D_MODEL), lambda d, k: (0, 0)),
                  pl.BlockSpec((None, D_STATE, D_MODEL), lambda d, k: (d, 0, 0))] + [_ANY] * ng,
        out_specs=[pl.BlockSpec((None, q, D_MODEL), lambda d, k: (d, cidx(d, k), 0)),
                   pl.BlockSpec((None, None, D_STATE, D_MODEL), lambda d, k: (d, cidx(d, k), 0, 0)),
                   pl.BlockSpec((None, D_STATE, D_MODEL), lambda d, k: (d, 0, 0))] + [_ANY] * ng,
        out_shape=[jax.ShapeDtypeStruct((2, n, D_MODEL), BF16),
                   jax.ShapeDtypeStruct((2, nc, D_STATE, D_MODEL), BF16),
                   jax.ShapeDtypeStruct((2, D_STATE, D_MODEL), F32)] +
                  [jax.ShapeDtypeStruct((4,) + t.shape, t.dtype) for t in gather],
        scratch_shapes=[pltpu.VMEM((D_STATE, D_MODEL), F32)] +
                       ([pltpu.SemaphoreType.DMA((nsem,)), pltpu.SemaphoreType.DMA((nsem,))] if ng else []),
        compiler_params=_cp(("arbitrary", "arbitrary")),
    )(u, xd, bias2, a2, rexp, h0, *gather)


def _ssd_bwd(u, xd, bias2, a2, rexp, rexp_t, dy, hprev, lam0, name, exchange=()):
    n = u.shape[0]
    nc = n // CHUNK
    q = CHUNK
    cidx = _ssd_chunk_index(nc, reverse=True)
    ne = len(exchange)

    def body(u_ref, xd_ref, bias_ref, a_ref, r_ref, rt_ref, dy_ref, hp_ref, lam0_ref, *rest):
        e_ins, (dxs_ref, dbc_ref, ddt_ref, acc_ref, lamo_ref), rest = rest[:ne], rest[ne:ne + 5], rest[ne + 5:]
        e_outs, lam, sems = rest[:ne], rest[ne], rest[ne + 1:]
        d = pl.program_id(0)
        k = pl.program_id(1)
        if ne:
            e_start, e_finish = _exchange_steps(e_ins, e_outs, *sems)
            pl.when((d == 0) & (k == 0))(e_start)

        @pl.when(k == 0)
        def _():
            lam[...] = lam0_ref[...]
            acc_ref[...] = jnp.zeros_like(acc_ref)

        rexp_t = rt_ref[...]

        def hsum(t):
            return _dot_sel_r(t, rexp_t, n=2)

        p = _ssd_prologue(d, u_ref, xd_ref, bias_ref, a_ref, r_ref)
        xs, bm, cm = p["xs"], p["bm"], p["cm"]
        bm_bf, cm_bf = bm.astype(BF16), cm.astype(BF16)
        lamn = lam[...]
        lamn_bf = lamn.astype(BF16)
        stp = hp_ref[...]
        dyv = dy_ref[...].astype(F32)
        lane128 = p["col"]

        wend_e = p["wend_e"]
        cs = _dot(cm_bf, stp)
        dye_bf = (dyv * p["e"]).astype(BF16)
        dc_off = _dot_nt(dye_bf, stp)
        v = _dot(bm_bf, lamn_bf)
        xw_bf = (xs * wend_e).astype(BF16)
        db_off = _dot_nt(xw_bf, lamn_bf)
        elast_e = p["elast_e"]
        dlast_e = jnp.sum(stp.astype(F32) * lamn, axis=0, keepdims=True) * elast_e
        lam_new = lamn * elast_e + _dot(cm.T.astype(BF16), dye_bf)
        lam[...] = lam_new
        lamo_ref[...] = lam_new

        hs_vx = hsum(v * xs)
        om = p["wend16"] * hs_vx
        x1 = p["e16"] * hsum(dyv * cs) - om
        x2 = p["dend16"] * hs_vx
        x3 = jnp.sum(om, axis=0, keepdims=True) + hsum(jnp.broadcast_to(dlast_e, (8, D_MODEL)))[0:1, :]

        sub16 = lax.broadcasted_iota(jnp.int32, (SSD_HEADS, q), 0)
        rs = jnp.zeros((q, 128), F32)
        cs_m = jnp.zeros((SSD_HEADS, q), F32)
        dt_m = jnp.zeros((SSD_HEADS, q), F32)
        dg = jnp.zeros((q, q), F32)
        for pb in range(SSD_HEADS // 2):
            xp_bf = xs[:, pb * 128:(pb + 1) * 128].astype(BF16)
            dyp = dyv[:, pb * 128:(pb + 1) * 128]
            dxs_pair = None
            for half in range(2):
                h = 2 * pb + half
                sel = (lane128 < HEAD_DIM) if half == 0 else (lane128 >= HEAD_DIM)
                dyh_bf = jnp.where(sel, dyp, 0.0).astype(BF16)
                lm, gl, s = _ssd_head_mats(p, h)
                ds = _dot_nt(dyh_bf, xp_bf)
                t = _dot_tn(s.astype(BF16), dyh_bf)
                dxs_pair = t if dxs_pair is None else dxs_pair + t
                w = ds * s
                rs = rs + jnp.sum(w, axis=1, keepdims=True) * (lane128 == h).astype(F32)
                cs_m = jnp.where(sub16 == h, jnp.sum(w, axis=0, keepdims=True), cs_m)
                dt_m = jnp.where(sub16 == h, jnp.sum(ds * gl, axis=0, keepdims=True), dt_m)
                dg = dg + ds * lm * p["dt_t"][h:h + 1, :]
            sl = slice(pb * 128, (pb + 1) * 128)
            dxs_ref[:, sl] = (dxs_pair + v[:, sl] * wend_e[:, sl]).astype(BF16)

        def to_lanes(m16):
            return jnp.concatenate([m16, jnp.zeros((128 - SSD_HEADS, q), F32)], axis=0).T

        last = jnp.where(d == 0, q - 1, 0)
        dacum = rs - to_lanes(cs_m) + x1 + jnp.where(p["row"] == last, x3[0:1, :], 0.0)
        tri_t = jnp.where(p["mask_t"], 1.0, 0.0).astype(BF16)
        ddta = _dot_sel_l(tri_t, dacum)
        dt = p["dt"]
        a = p["a"]
        ddt = to_lanes(dt_m) + x2 + a * ddta
        ddtraw = jnp.where(p["head_lane"], ddt * _sigmoid(p["dtraw"]), 0.0)
        ddt_ref[...] = ddtraw
        acc_ref[0:1, :] += jnp.sum(ddtraw, axis=0, keepdims=True)
        acc_ref[1:2, :] += jnp.sum(dt * ddta, axis=0, keepdims=True) * a

        dg_bf = dg.astype(BF16)
        dbc_ref[:, 0:D_STATE] = _dot_tn(dg_bf, cm_bf) + db_off
        dbc_ref[:, D_STATE:2 * D_STATE] = _dot(dg_bf, bm_bf) + dc_off
        if ne:
            pl.when((d == 1) & (k == nc - 1))(e_finish)

    cblk = lambda d, k: (cidx(d, k), 0)
    return pl.pallas_call(
        body, name=name, grid=(2, nc),
        in_specs=[pl.BlockSpec((q, D_XBC), cblk),
                  pl.BlockSpec((q, 128), lambda d, k: (cidx(d, k), D_XBC // 128)),
                  pl.BlockSpec((None, 1, 128), lambda d, k: (d, 0, 0)),
                  pl.BlockSpec((None, 1, 128), lambda d, k: (d, 0, 0)),
                  pl.BlockSpec((128, D_MODEL), lambda d, k: (0, 0)),
                  pl.BlockSpec((D_MODEL, 128), lambda d, k: (0, 0)),
                  pl.BlockSpec((q, D_MODEL), cblk),
                  pl.BlockSpec((None, None, D_STATE, D_MODEL), lambda d, k: (d, cidx(d, k), 0, 0)),
                  pl.BlockSpec((None, D_STATE, D_MODEL), lambda d, k: (d, 0, 0))] + [_ANY] * ne,
        out_specs=[pl.BlockSpec((None, q, D_MODEL), lambda d, k: (d, cidx(d, k), 0)),
                   pl.BlockSpec((None, q, 2 * D_STATE), lambda d, k: (d, cidx(d, k), 0)),
                   pl.BlockSpec((None, q, 128), lambda d, k: (d, cidx(d, k), 0)),
                   pl.BlockSpec((None, 8, 128), lambda d, k: (d, 0, 0)),
                   pl.BlockSpec((None, D_STATE, D_MODEL), lambda d, k: (d, 0, 0))] + [_ANY] * ne,
        out_shape=[jax.ShapeDtypeStruct((2, n, D_MODEL), BF16),
                   jax.ShapeDtypeStruct((2, n, 2 * D_STATE), F32),
                   jax.ShapeDtypeStruct((2, n, 128), F32),
                   jax.ShapeDtypeStruct((2, 8, 128), F32),
                   jax.ShapeDtypeStruct((2, D_STATE, D_MODEL), F32)] +
                  [jax.ShapeDtypeStruct(t.shape, t.dtype) for t in exchange],
        scratch_shapes=[pltpu.VMEM((D_STATE, D_MODEL), F32)] +
                       ([pltpu.SemaphoreType.DMA((3 * ne,)), pltpu.SemaphoreType.DMA((3 * ne,)),
                         pltpu.SemaphoreType.DMA((ne,))] if ne else []),
        compiler_params=_cp(("arbitrary", "arbitrary")),
    )(u, xd, bias2, a2, rexp, rexp_t, dy, hprev, lam0, *exchange)


def _merge_fwd(y, u, z, dskip_e, gn, tb):
    n = z.shape[0]

    def body(y_ref, u_ref, z_ref, sk_ref, gn_ref, o_ref):
        sk, gnv = sk_ref[...], gn_ref[...]
        for r in _row_tiles(tb):
            ys = y_ref[0, r, :].astype(F32) + y_ref[1, r, :].astype(F32) + sk * _silu(u_ref[r, :].astype(F32))
            gated = ys * _silu(z_ref[r, :].astype(F32))
            rstd = lax.rsqrt(jnp.mean(gated * gated, axis=-1, keepdims=True) + LN_EPS)
            o_ref[r, :] = (gated * rstd * gnv).astype(BF16)

    return pl.pallas_call(
        body, name="merge_fwd", grid=(n // tb,),
        in_specs=[pl.BlockSpec((2, tb, D_MODEL), lambda i: (0, i, 0)), pl.BlockSpec((tb, D_MODEL), lambda i: (i, 0)),
                  _row_spec(tb, D_MODEL), _par_spec(D_MODEL), _par_spec(D_MODEL)],
        out_specs=_row_spec(tb, D_MODEL),
        out_shape=jax.ShapeDtypeStruct((n, D_MODEL), BF16),
        compiler_params=_cp(("parallel",)),
    )(y, u, z, dskip_e, gn)


def _merge_bwd(dyn, y, u, z, dskip_e, gn, tb):
    n = z.shape[0]

    def body(dyn_ref, y_ref, u_ref, z_ref, sk_ref, gn_ref, dy_ref, dz_ref, acc_ref):
        i = pl.program_id(0)

        @pl.when(i == 0)
        def _():
            acc_ref[...] = jnp.zeros_like(acc_ref)

        sk, gnv = sk_ref[...], gn_ref[...]
        part0 = jnp.zeros((8, D_MODEL), F32)
        part1 = jnp.zeros((8, D_MODEL), F32)
        for s in range(tb // ROW_SUB):
            r = slice(s * ROW_SUB, (s + 1) * ROW_SUB)
            xs = _silu(u_ref[r, :].astype(F32))
            zv = z_ref[r, :].astype(F32)
            sz = _sigmoid(zv)
            ys = y_ref[0, r, :].astype(F32) + y_ref[1, r, :].astype(F32) + sk * xs
            gated = ys * (zv * sz)
            rstd = lax.rsqrt(jnp.mean(gated * gated, axis=-1, keepdims=True) + LN_EPS)
            ghat = gated * rstd
            dyn_v = dyn_ref[r, :].astype(F32)
            t = dyn_v * gnv
            dgated = rstd * (t - ghat * jnp.mean(t * ghat, axis=-1, keepdims=True))
            dys = dgated * (zv * sz)
            dy_ref[r, :] = dys.astype(BF16)
            dz_ref[r, :] = (dgated * ys * (sz * (1.0 + zv * (1.0 - sz)))).astype(BF16)
            part0 = part0 + _fold8(dyn_v * ghat)
            part1 = part1 + _fold8(dys * xs)
        acc_ref[0:1, :] += jnp.sum(part0, axis=0, keepdims=True)
        acc_ref[1:2, :] += jnp.sum(part1, axis=0, keepdims=True)

    return pl.pallas_call(
        body, name="merge_bwd", grid=(n // tb,),
        in_specs=[_row_spec(tb, D_MODEL), pl.BlockSpec((2, tb, D_MODEL), lambda i: (0, i, 0)),
                  pl.BlockSpec((tb, D_MODEL), lambda i: (i, 0)), _row_spec(tb, D_MODEL),
                  _par_spec(D_MODEL), _par_spec(D_MODEL)],
        out_specs=[_row_spec(tb, D_MODEL), _row_spec(tb, D_MODEL), _acc_spec(D_MODEL)],
        out_shape=[jax.ShapeDtypeStruct((n, D_MODEL), BF16), jax.ShapeDtypeStruct((n, D_MODEL), BF16),
                   jax.ShapeDtypeStruct((8, D_MODEL), F32)],
        compiler_params=_cp(("arbitrary",)),
    )(dyn, y, u, z, dskip_e, gn)


def _pool_consts(transpose):
    tb = POOL_TB
    t = jnp.arange(tb)
    s = jnp.arange(3 * tb)
    rl, cl = t // GRID_W, t % GRID_W
    rs_, cs_ = s // GRID_W - tb // GRID_W, s % GRID_W
    s2 = jnp.arange(tb)
    rl2, cl2 = s2 // GRID_W, s2 % GRID_W
    brow, bcol = [], []
    for w in POOL_WINDOWS:
        lo, hi = -(w // 2), w - w // 2
        if transpose:
            lo, hi = -hi + 1, -lo + 1
        dr = rs_[None, :] - rl[:, None]
        brow.append(((cs_[None, :] == cl[:, None]) & (dr >= lo) & (dr < hi)).astype(BF16))
        dc = cl2[None, :] - cl[:, None]
        bcol.append(((rl2[None, :] == rl[:, None]) & (dc >= lo) & (dc < hi)).astype(BF16))
    return jnp.stack(brow), jnp.stack(bcol)


def _pool_inv(i, g, n):
    assert GRID_W == 64
    t = i * POOL_TB + lax.broadcasted_iota(jnp.int32, (POOL_TB, 1), 0)
    r = lax.shift_right_logical(t, 6)
    col = t & (GRID_W - 1)
    w = POOL_WINDOWS[g]
    lo, hi = -(w // 2), w - w // 2
    cnt_r = jnp.minimum(r + hi, n // GRID_W) - jnp.maximum(r + lo, 0)
    cnt_c = jnp.minimum(col + hi, GRID_W) - jnp.maximum(col + lo, 0)
    return 1.0 / (cnt_r * cnt_c).astype(F32)


def _pool_box(prev_ref, cur_ref, next_ref, brow_ref, bcol_ref, g, i, nb):
    sl = slice(g * POOL_DIM, (g + 1) * POOL_DIM)
    pv = prev_ref[:, sl] * (i > 0).astype(prev_ref.dtype)
    nx = next_ref[:, sl] * (i < nb - 1).astype(next_ref.dtype)
    stack = jnp.concatenate([pv.astype(BF16), cur_ref[:, sl].astype(BF16), nx.astype(BF16)], axis=0)
    r = _dot(brow_ref[g], stack)
    return _dot(bcol_ref[g], r.astype(BF16))


def _pool_halo_specs(n, d):
    tb = POOL_TB
    nb = n // tb
    prev = pl.BlockSpec((tb, d), lambda i: (jnp.maximum(i - 1, 0), 0))
    cur = pl.BlockSpec((tb, d), lambda i: (i, 0))
    nxt = pl.BlockSpec((tb, d), lambda i: (jnp.minimum(i + 1, nb - 1), 0))
    return prev, cur, nxt


def _pool_const_specs():
    tb = POOL_TB
    return [pl.BlockSpec((N_POOL, tb, 3 * tb), lambda i: (0, 0, 0)),
            pl.BlockSpec((N_POOL, tb, tb), lambda i: (0, 0, 0))]


def _pool_fwd(up, consts, pw_bf, pscale):
    n = up.shape[0]
    tb = POOL_TB
    nb = n // tb
    brow, bcol = consts
    prev, cur, nxt = _pool_halo_specs(n, D_MODEL)

    def body(p_ref, c_ref, n_ref, brow_ref, bcol_ref, pw_ref, sc_ref, o_ref, d_ref):
        i = pl.program_id(0)
        for g in range(N_POOL):
            sl = slice(g * POOL_DIM, (g + 1) * POOL_DIM)
            box = _pool_box(p_ref, c_ref, n_ref, brow_ref, bcol_ref, g, i, nb)
            dd = (box * _pool_inv(i, g, n) - c_ref[:, sl].astype(F32)).astype(BF16)
            d_ref[:, sl] = dd
            o_ref[:, sl] = (_dot(dd, pw_ref[g]) * sc_ref[:, sl]).astype(BF16)

    return pl.pallas_call(
        body, name="pool_fwd", grid=(nb,),
        in_specs=[prev, cur, nxt] + _pool_const_specs() +
                 [pl.BlockSpec((N_POOL, POOL_DIM, POOL_DIM), lambda i: (0, 0, 0)), _par_spec(D_MODEL)],
        out_specs=[_row_spec(tb, D_MODEL), _row_spec(tb, D_MODEL)],
        out_shape=[jax.ShapeDtypeStruct((n, D_MODEL), BF16), jax.ShapeDtypeStruct((n, D_MODEL), BF16)],
        compiler_params=_cp(("parallel",)),
    )(up, up, up, brow, bcol, pw_bf, pscale)


def _pool_bwd_a(dp, dsave, pw_bf, pwt_bf, pscale):
    n = dp.shape[0]
    tb = POOL_TB

    def body(dp_ref, d_ref, pw_ref, pwt_ref, sc_ref, dd_ref, dds_ref, gw_ref, gs_ref):
        i = pl.program_id(0)

        @pl.when(i == 0)
        def _():
            gw_ref[...] = jnp.zeros_like(gw_ref)
            gs_ref[...] = jnp.zeros_like(gs_ref)

        for g in range(N_POOL):
            sl = slice(g * POOL_DIM, (g + 1) * POOL_DIM)
            dpv = dp_ref[:, sl].astype(F32)
            dv = d_ref[:, sl]
            dpw_bf = (dpv * sc_ref[:, sl]).astype(BF16)
            dd = _dot(dpw_bf, pwt_ref[g])
            dd_ref[:, sl] = dd.astype(BF16)
            dds_ref[:, sl] = (dd * _pool_inv(i, g, n)).astype(BF16)
            gw_ref[g] += _dot_tn(dv, dpw_bf)
            gs_ref[0:1, sl] += jnp.sum(dpv * _dot(dv, pw_ref[g]), axis=0, keepdims=True)

    wspec = pl.BlockSpec((N_POOL, POOL_DIM, POOL_DIM), lambda i: (0, 0, 0))
    return pl.pallas_call(
        body, name="pool_bwd_a", grid=(n // tb,),
        in_specs=[_row_spec(tb, D_MODEL), _row_spec(tb, D_MODEL), wspec, wspec, _par_spec(D_MODEL)],
        out_specs=[_row_spec(tb, D_MODEL), _row_spec(tb, D_MODEL), wspec, _acc_spec(D_MODEL)],
        out_shape=[jax.ShapeDtypeStruct((n, D_MODEL), BF16), jax.ShapeDtypeStruct((n, D_MODEL), BF16),
                   jax.ShapeDtypeStruct((N_POOL, POOL_DIM, POOL_DIM), F32), jax.ShapeDtypeStruct((8, D_MODEL), F32)],
        compiler_params=_cp(("arbitrary",)),
    )(dp, dsave, pw_bf, pwt_bf, pscale)


def _pool_bwd_b(dds, dd, consts_t):
    n = dd.shape[0]
    tb = POOL_TB
    nb = n // tb
    brow, bcol = consts_t
    prev, cur, nxt = _pool_halo_specs(n, D_MODEL)

    def body(p_ref, c_ref, n_ref, brow_ref, bcol_ref, dd_ref, o_ref):
        i = pl.program_id(0)
        for g in range(N_POOL):
            sl = slice(g * POOL_DIM, (g + 1) * POOL_DIM)
            box = _pool_box(p_ref, c_ref, n_ref, brow_ref, bcol_ref, g, i, nb)
            o_ref[:, sl] = (box - dd_ref[:, sl].astype(F32)).astype(BF16)

    return pl.pallas_call(
        body, name="pool_bwd_b", grid=(nb,),
        in_specs=[prev, cur, nxt] + _pool_const_specs() + [_row_spec(tb, D_MODEL)],
        out_specs=_row_spec(tb, D_MODEL),
        out_shape=jax.ShapeDtypeStruct((n, D_MODEL), BF16),
        compiler_params=_cp(("parallel",)),
    )(dds, dds, dds, brow, bcol, dd)


def _pair_add(slabs, recvs, core, name):
    na = len(slabs)
    hr = [t.shape[1] // 4 for t in slabs]

    def body(core_ref, *refs):
        for a in range(na):
            refs[2 * na + a][...] = (refs[a][...] + refs[na + a][...]).astype(BF16)

    own = [pl.BlockSpec((None, hr[a], slabs[a].shape[2]), lambda j, i, c_ref: (j, 2 * c_ref[0] + i, 0)) for a in range(na)]
    got = [pl.BlockSpec((None, hr[a], slabs[a].shape[2]), lambda j, i, c_ref: (j, i, 0)) for a in range(na)]
    return pl.pallas_call(
        body, name=name,
        grid_spec=pltpu.PrefetchScalarGridSpec(num_scalar_prefetch=1, grid=(4, 2), in_specs=own + got, out_specs=got),
        out_shape=[jax.ShapeDtypeStruct(r.shape, BF16) for r in recvs],
        compiler_params=_cp(("arbitrary", "arbitrary")),
    )(core, *slabs, *recvs)


def _sum4(parts, core):
    na = len(parts)
    hr = [t.shape[1] // 2 for t in parts]

    def body(core_ref, *refs):
        for a in range(na):
            p = refs[a]
            refs[na + a][...] = ((p[0].astype(F32) + p[1].astype(F32)) + p[2].astype(F32)) + p[3].astype(F32)

    return pl.pallas_call(
        body, name="reduce_g_sum",
        grid_spec=pltpu.PrefetchScalarGridSpec(
            num_scalar_prefetch=1, grid=(2,),
            in_specs=[pl.BlockSpec((4, hr[a], parts[a].shape[2]), lambda i, c_ref: (0, i, 0)) for a in range(na)],
            out_specs=[pl.BlockSpec((hr[a], parts[a].shape[2]), lambda i, c_ref: (2 * c_ref[0] + i, 0))
                       for a in range(na)]),
        out_shape=[jax.ShapeDtypeStruct((2 * t.shape[1], t.shape[2]), F32) for t in parts],
        compiler_params=_cp(("arbitrary",)),
    )(core, *parts)


def _adamw(w, g, m, v, name):
    r, cdim = w.shape
    tb = _row_block(r, 256)
    c1 = 1.0 - ADAM_B1 ** ADAM_STEP
    c2 = 1.0 - ADAM_B2 ** ADAM_STEP

    def body(w_ref, g_ref, m_ref, v_ref, d_ref, nm_ref, nv_ref):
        gv = g_ref[...]
        nm = ADAM_B1 * m_ref[...] + (1.0 - ADAM_B1) * gv
        nv = ADAM_B2 * v_ref[...] + (1.0 - ADAM_B2) * (gv * gv)
        m_hat = nm / c1
        v_hat = nv / c2
        d_ref[...] = -ADAM_LR * (m_hat / (jnp.sqrt(v_hat) + ADAM_EPS) + ADAM_WD * w_ref[...])
        nm_ref[...] = nm
        nv_ref[...] = nv

    spec = _row_spec(tb, cdim)
    shp = jax.ShapeDtypeStruct((r, cdim), F32)
    return pl.pallas_call(
        body, name=name, grid=(r // tb,),
        in_specs=[spec] * 4, out_specs=[spec] * 3, out_shape=[shp] * 3,
        compiler_params=_cp(("parallel",)),
    )(w, g, m, v)


def _mesh_pos():
    return lax.axis_index("x"), lax.axis_index("y"), lax.axis_index("c")


_ANY = pl.BlockSpec(memory_space=pl.ANY)


def _remote(src, dst, send_sem, recv_sem, device):
    return pltpu.make_async_remote_copy(src_ref=src, dst_ref=dst, send_sem=send_sem, recv_sem=recv_sem,
                                        device_id=device, device_id_type=MESH)


def _other_chips(x, y):
    return [(1 - x, y), (x, 1 - y), (1 - x, 1 - y)]


def _half(nrows, h):
    return pl.ds(h * (nrows // 2), nrows // 2)


_GATHER_SEMS = 7


def _gather_steps(ins, outs, send_sems, recv_sems):
    na = len(ins)
    nrow = [r.shape[0] for r in ins]

    def copies():
        x, y, c = _mesh_pos()
        me = 2 * x + y
        sib = (x, y, 1 - c)
        chips = _other_chips(x, y)

        def ici(k, a, slot):
            px, py = chips[k]
            rows = _half(nrow[a], c)
            return _remote(ins[a].at[rows, :], outs[a].at[slot, rows, :], send_sems.at[k * na + a],
                           recv_sems.at[k * na + a], (px, py, c))

        def fwd(k, a, h):
            px, py = chips[k]
            blk = outs[a].at[2 * px + py, _half(nrow[a], h), :]
            return _remote(blk, blk, send_sems.at[(3 + k) * na + a], recv_sems.at[(3 + k) * na + a], sib)

        def own(a):
            return _remote(ins[a], outs[a].at[me], send_sems.at[6 * na + a], recv_sems.at[6 * na + a], sib)

        slots = [2 * px + py for px, py in chips]
        return ici, fwd, own, me, c, slots

    def start():
        ici, _, own, me, _, _ = copies()
        for a in range(na):
            own(a).start()
        for k in range(3):
            for a in range(na):
                ici(k, a, me).start()

    def finish():
        ici, fwd, own, me, c, slots = copies()
        for k in range(3):
            for a in range(na):
                ici(k, a, slots[k]).wait_recv()
                fwd(k, a, c).start()
        for k in range(3):
            for a in range(na):
                fwd(k, a, 1 - c).wait_recv()
        for a in range(na):
            own(a).wait_recv()
        for a in range(na):
            own(a).wait_send()
        for k in range(3):
            for a in range(na):
                ici(k, a, me).wait_send()
                fwd(k, a, c).wait_send()

    return start, finish


def _exchange_steps(ins, outs, send_sems, recv_sems, local_sems):
    na = len(ins)

    def copies():
        x, y, c = _mesh_pos()
        me = 2 * x + y
        chips = _other_chips(x, y)

        def copy(k, a, slot):
            px, py = chips[k]
            return _remote(ins[a].at[2 * px + py], outs[a].at[slot], send_sems.at[k * na + a], recv_sems.at[k * na + a],
                           (px, py, c))

        def local(a):
            return pltpu.make_async_copy(ins[a].at[me], outs[a].at[me], local_sems.at[a])

        return copy, local, me, [2 * px + py for px, py in chips]

    def start():
        copy, local, me, _ = copies()
        for a in range(na):
            local(a).start()
        for k in range(3):
            for a in range(na):
                copy(k, a, me).start()

    def finish():
        copy, local, me, slots = copies()
        for k in range(3):
            for a in range(na):
                copy(k, a, slots[k]).wait_recv()
        for k in range(3):
            for a in range(na):
                copy(k, a, me).wait_send()
        for a in range(na):
            local(a).wait()

    return start, finish


def _gather_weights(shards, conv8):
    na = len(shards)

    def body(*refs):
        ins, conv_in = refs[:na], refs[na]
        outs, conv_out = refs[na + 1:2 * na + 1], refs[2 * na + 1]
        send_sems, recv_sems, local_sems = refs[2 * na + 2:]
        x, y, c = _mesh_pos()
        me = 2 * x + y
        chips = _other_chips(x, y)

        def conv(k, slot):
            px, py = chips[k]
            return _remote(conv_in, conv_out.at[slot], send_sems.at[7 * na + k], recv_sems.at[7 * na + k], (px, py, c))

        start, finish = _gather_steps(ins, outs, send_sems, recv_sems)
        local = pltpu.make_async_copy(conv_in, conv_out.at[me], local_sems.at[0])
        local.start()
        start()
        sends = [conv(k, me) for k in range(3)]
        for cp in sends:
            cp.start()
        finish()
        for k in range(3):
            px, py = chips[k]
            conv(k, 2 * px + py).wait_recv()
        for cp in sends:
            cp.wait_send()
        local.wait()

    nsem = _GATHER_SEMS * na + 3
    return pl.pallas_call(
        body, name="gather_w", in_specs=[_ANY] * (na + 1), out_specs=[_ANY] * (na + 1),
        out_shape=[jax.ShapeDtypeStruct((4,) + t.shape, t.dtype) for t in shards] +
                  [jax.ShapeDtypeStruct((4,) + conv8.shape, conv8.dtype)],
        scratch_shapes=[pltpu.SemaphoreType.DMA((nsem,)), pltpu.SemaphoreType.DMA((nsem,)),
                        pltpu.SemaphoreType.DMA((1,))],
    )(*shards, conv8)


def _pair_swap(slabs, name):
    na = len(slabs)

    def body(*refs):
        ins, outs = refs[:na], refs[na:2 * na]
        send_sems, recv_sems = refs[2 * na:]
        x, y, c = _mesh_pos()
        cps = [_remote(ins[a].at[:, _half(slabs[a].shape[1], 1 - c), :], outs[a], send_sems.at[a], recv_sems.at[a],
                       (x, y, 1 - c)) for a in range(na)]
        for cp in cps:
            cp.start()
        for cp in cps:
            cp.wait()

    return pl.pallas_call(
        body, name=name, in_specs=[_ANY] * na, out_specs=[_ANY] * na,
        out_shape=[jax.ShapeDtypeStruct((4, t.shape[1] // 2, t.shape[2]), t.dtype) for t in slabs],
        scratch_shapes=[pltpu.SemaphoreType.DMA((na,)), pltpu.SemaphoreType.DMA((na,))],
    )(*slabs)


def _chip_exchange(pairs):
    na = len(pairs)

    def body(*refs):
        start, finish = _exchange_steps(refs[:na], refs[na:2 * na], *refs[2 * na:])
        start()
        finish()

    return pl.pallas_call(
        body, name="reduce_g_ici", in_specs=[_ANY] * na, out_specs=[_ANY] * na,
        out_shape=[jax.ShapeDtypeStruct(t.shape, t.dtype) for t in pairs],
        scratch_shapes=[pltpu.SemaphoreType.DMA((3 * na,)), pltpu.SemaphoreType.DMA((3 * na,)),
                        pltpu.SemaphoreType.DMA((na,))],
    )(*pairs)


def _share_halves(totals):
    na = len(totals)

    def body(*refs):
        bufs = refs[na:2 * na]
        send_sems, recv_sems = refs[2 * na:]
        x, y, c = _mesh_pos()

        def copy(a, h):
            blk = bufs[a].at[_half(totals[a].shape[0], h), :]
            return _remote(blk, blk, send_sems.at[a], recv_sems.at[a], (x, y, 1 - c))

        sends = [copy(a, c) for a in range(na)]
        for cp in sends:
            cp.start()
        for a in range(na):
            copy(a, 1 - c).wait_recv()
        for cp in sends:
            cp.wait_send()

    return pl.pallas_call(
        body, name="reduce_g_share", in_specs=[_ANY] * na, out_specs=[_ANY] * na,
        out_shape=[jax.ShapeDtypeStruct(t.shape, t.dtype) for t in totals],
        input_output_aliases={a: a for a in range(na)},
        scratch_shapes=[pltpu.SemaphoreType.DMA((na,)), pltpu.SemaphoreType.DMA((na,))],
    )(*totals)


def _allreduce_small(v, name):
    r, cdim = v.shape

    def body(v_ref, out_ref, buf, send_sems, recv_sems):
        x, y, c = _mesh_pos()
        me = 4 * x + 2 * y + c
        buf[me] = v_ref[...]
        rel = [(bx, by, bc) for bx in (0, 1) for by in (0, 1) for bc in (0, 1)][1:]

        def peer(b):
            bx, by, bc = b
            return ((1 - x) if bx else x, (1 - y) if by else y, (1 - c) if bc else c)

        def copy(k, slot):
            return pltpu.make_async_remote_copy(
                src_ref=v_ref, dst_ref=buf.at[slot], send_sem=send_sems.at[k], recv_sem=recv_sems.at[k],
                device_id=peer(rel[k]), device_id_type=MESH)

        sends = [copy(k, me) for k in range(7)]
        for cp in sends:
            cp.start()
        for k in range(7):
            px, py, pc = peer(rel[k])
            copy(k, 4 * px + 2 * py + pc).wait_recv()
        for cp in sends:
            cp.wait_send()
        acc = buf[0]
        for j in range(1, 8):
            acc = acc + buf[j]
        out_ref[...] = acc

    vm = pl.BlockSpec(memory_space=pltpu.VMEM)
    return pl.pallas_call(
        body, name=name, in_specs=[vm], out_specs=[vm, vm],
        out_shape=[jax.ShapeDtypeStruct((r, cdim), F32), jax.ShapeDtypeStruct((8, r, cdim), F32)],
        scratch_shapes=[pltpu.SemaphoreType.DMA((7,)), pltpu.SemaphoreType.DMA((7,))],
    )(v)


def _chip_bcast(v, name):
    def body(v_ref, out_ref, send_sems, recv_sems):
        x, y, c = _mesh_pos()
        me = 2 * x + y
        chips = _other_chips(x, y)
        out_ref[me] = v_ref[...]

        def copy(k, slot):
            px, py = chips[k]
            return _remote(v_ref, out_ref.at[slot], send_sems.at[k], recv_sems.at[k], (px, py, c))

        sends = [copy(k, me) for k in range(3)]
        for cp in sends:
            cp.start()
        for k, (px, py) in enumerate(chips):
            copy(k, 2 * px + py).wait_recv()
        for cp in sends:
            cp.wait_send()

    vm = pl.BlockSpec(memory_space=pltpu.VMEM)
    return pl.pallas_call(
        body, name=name, in_specs=[vm], out_specs=vm,
        out_shape=jax.ShapeDtypeStruct((4,) + v.shape, F32),
        scratch_shapes=[pltpu.SemaphoreType.DMA((3,)), pltpu.SemaphoreType.DMA((3,))],
    )(v)


_BIG = (("in_proj", (D_MODEL, D_IN_PROJ // 4), 1), ("w_out", (2 * D_MODEL // 4, D_MODEL), 0),
        ("w_gate", (D_MODEL, D_FF // 4), 1), ("w_up", (D_MODEL, D_FF // 4), 1), ("w_down", (D_FF // 4, D_MODEL), 0),
        ("pool_w", (N_POOL * POOL_DIM // 4, POOL_DIM), None))


def _assemble(name, t):
    _, r, c = t.shape
    axis = {n: ax for n, _, ax in _BIG}[name]
    if axis == 0:
        return t.reshape(4 * r, c)
    if axis == 1:
        return t.transpose(1, 0, 2).reshape(r, 4 * c)
    return t.reshape(4, N_POOL, POOL_DIM // 4, POOL_DIM).transpose(1, 0, 2, 3).reshape(N_POOL, POOL_DIM, POOL_DIM)


def _to_slabs(name, g):
    (r, c), axis = {n: (sh, ax) for n, sh, ax in _BIG}[name]
    if axis == 0:
        return g.reshape(4, r, c)
    if axis == 1:
        return g.reshape(r, 4, c).transpose(1, 0, 2)
    return g.reshape(N_POOL, 4, POOL_DIM // 4, POOL_DIM).transpose(1, 0, 2, 3).reshape(4, r, c)


_EARLY = ("in_proj",)
_LATE = tuple(n for n, _, _ in _BIG if n not in _EARLY)


def _reduce_grads(early_grads, late_parts, core):
    slabs = [_to_slabs(n, early_grads[n]) for n in _EARLY]
    pairs = _pair_add(slabs, _pair_swap(slabs, "reduce_g_d2d"), core, "reduce_g_pair")
    parts = dict(zip(_EARLY, _chip_exchange(pairs)), **dict(zip(_LATE, late_parts)))
    names = [n for n, _, _ in _BIG]
    totals = _sum4([parts[n] for n in names], core)
    return dict(zip(names, _share_halves(totals)))


def _pad_cols(w, n):
    return jnp.concatenate([w, jnp.zeros((w.shape[0], n - w.shape[1]), w.dtype)], axis=1)


def _device_step(x, mod, mod_ctx, ctx, target, wts, w8, small, tb, late_shards=None, core=None):
    n = x.shape[0]
    d = D_MODEL

    win = wts["in_proj"]
    wz, wxd, wup = win[:, 0:d], _pad_cols(win[:, d:d + D_XBC + 2 * SSD_HEADS], D_XD), win[:, d + D_XBC + 2 * SSD_HEADS:]

    emb_g, emb_b = _vec(small["emb_ln_g"]), _vec(small["emb_ln_b"])
    ln1_g, ln1_b = _vec(small["ln1_g"]), _vec(small["ln1_b"])
    ln2_g, ln2_b = _vec(small["ln2_g"]), _vec(small["ln2_b"])
    gn = _vec(small["ssd_norm_g"])
    pscale = _vec(small["pool_scale"])
    conv_b = _vec(small["conv_b"])
    dskip_e = jnp.repeat(small["d_skip"].reshape(-1), HEAD_DIM).reshape(1, d)
    zpad = jnp.zeros((2, 1, 128 - SSD_HEADS), F32)
    bias2 = jnp.concatenate([small["dt_bias"].reshape(2, 1, SSD_HEADS), zpad], axis=2)
    a2 = jnp.concatenate([-jnp.exp(small["a_log"].reshape(2, 1, SSD_HEADS)), zpad], axis=2)
    rexp = (jnp.arange(128)[:, None] == (jnp.arange(d)[None, :] // HEAD_DIM)).astype(BF16)
    rexp_t = rexp.T

    sh1, sc1, g1, sh2, sc2, g2 = [mod[:, i * d:(i + 1) * d] for i in range(6)]
    sh1c, sc1c = mod_ctx[:, 0:d], mod_ctx[:, d:2 * d]

    tbc = min(tb, ctx.shape[0])
    xc0, hc = _ln_mod(ctx, emb_g, emb_b, sh1c, sc1c, tbc, "ln_mod_ctx")
    xdc = _matmul_nn([(hc, wxd)], F32, 512, D_XD, "in_proj_ctx")
    uc = _conv_fwd(xdc, w8, conv_b, tbc, "conv_fwd_ctx")
    hzero = jnp.zeros((2, D_STATE, d), F32)
    _, hprev_c, hfin_c = _ssd_fwd(uc, xdc, bias2, a2, rexp, hzero, "ssd_fwd_ctx")

    x0, h1 = _ln_mod(x, emb_g, emb_b, sh1, sc1, tb, "ln_mod")
    z = _matmul_nn([(h1, wz)], BF16, MM_ROWS, 1024, "in_proj_z")
    xd = _matmul_nn([(h1, wxd)], F32, MM_ROWS, D_XD, "in_proj_xd")
    up = _matmul_nn([(h1, wup)], BF16, MM_ROWS, 1024, "in_proj_up")
    u = _conv_fwd(xd, w8, conv_b, tb, "conv_fwd")
    y, hprev, _, *landed = _ssd_fwd(u, xd, bias2, a2, rexp, hfin_c, "ssd_fwd", gather=late_shards or ())
    if late_shards is not None:
        wts = dict(wts, **{nme: _assemble(nme, t) for nme, t in zip(_LATE, landed)})
    wout = wts["w_out"]
    wg, wu, wd = wts["w_gate"], wts["w_up"], wts["w_down"]
    pw = wts["pool_w"]
    yn = _merge_fwd(y, u, z, dskip_e, gn, tb)
    pconst = _pool_consts(False)
    pool, dsave = _pool_fwd(up, pconst, pw, pscale)
    mix = _matmul_nn([(yn, wout[0:d]), (pool, wout[d:2 * d])], F32, MM_ROWS, 1024, "out_proj")
    x1, h2 = _res_ln(x0, mix, g1, ln1_g, ln1_b, sh2, sc2, tb)

    gate, upp, hmid = _swiglu_fwd(h2, wg, wu, 512, D_FF // 2)
    ffn = _matmul_nn([(hmid, wd)], F32, MM_ROWS, 1024, "ffn_down")
    dffn, dr2, acc2 = _final_ln_loss(x1, ffn, g2, ln2_g, ln2_b, target, tb)
    loss = (0.5 / d) * jnp.sum(acc2[3])

    dgate, dupp = _swiglu_bwd(dffn, wd.T, gate, upp, 512, D_FF // 2)
    g_wdown = _matmul_tn(hmid, dffn, MM_ROWS, 1024, "g_w_down")
    g_wgate = _matmul_tn(h2, dgate, MM_ROWS, 1408, "g_w_gate")
    g_wup = _matmul_tn(h2, dupp, MM_ROWS, 1408, "g_w_up")
    dh2 = _matmul_nn([(dgate, wg.T), (dupp, wu.T)], F32, 512, 1024, "d_h2")
    dmix, dr1, acc1 = _bwd_ln1(dr2, dh2, x1, x0, mix, g1, sc2, ln1_g, tb)

    dyn = _matmul_nn([(dmix, wout[0:d].T)], BF16, MM_ROWS, 1024, "d_yn")
    dpool = _matmul_nn([(dmix, wout[d:2 * d].T)], BF16, MM_ROWS, 1024, "d_pool")
    g_wout = jnp.concatenate([_matmul_tn(yn, dmix, MM_ROWS, 1024, "g_w_out_a"),
                              _matmul_tn(pool, dmix, MM_ROWS, 1024, "g_w_out_b")], axis=0)
    dd, dds, g_pw, accp = _pool_bwd_a(dpool, dsave, pw, jnp.swapaxes(pw, 1, 2), pscale)
    dup = _pool_bwd_b(dds, dd, _pool_consts(True))
    dy, dz, accm = _merge_bwd(dyn, y, u, z, dskip_e, gn, tb)
    lam0 = jnp.zeros((2, D_STATE, d), F32)
    late_grads = dict(w_out=g_wout, w_gate=g_wgate, w_up=g_wup, w_down=g_wdown, pool_w=g_pw)
    pairs = ()
    if late_shards is not None:
        slabs = [_to_slabs(nme, late_grads[nme]) for nme in _LATE]
        pairs = _pair_add(slabs, _pair_swap(slabs, "reduce_g_d2d_late"), core, "reduce_g_pair_late")
    dxs, dbc, ddt, accs, lam_c, *arrived = _ssd_bwd(u, xd, bias2, a2, rexp, rexp_t, dy, hprev, lam0, "ssd_bwd",
                                                    exchange=pairs)
    du, accb = _conv_bwd_a(dxs, dy, dskip_e, dbc, u, tb, "conv_bwd_a")
    dxd, accw = _conv_bwd_b(du, xd, ddt, w8, tb, "conv_bwd_b")

    lc = ctx.shape[0]
    zeros_c = jnp.zeros((lc, d), BF16)
    dxs_c, dbc_c, ddt_c, accs_c, _ = _ssd_bwd(uc, xdc, bias2, a2, rexp, rexp_t, zeros_c, hprev_c, lam_c, "ssd_bwd_ctx")
    du_c, accb_c = _conv_bwd_a(dxs_c, zeros_c, dskip_e, dbc_c, uc, tbc, "conv_bwd_a_ctx")
    dxd_c, accw_c = _conv_bwd_b(du_c, xdc, ddt_c, w8, tbc, "conv_bwd_b_ctx")
    dhc = _matmul_nn([(dxd_c, wxd.T)], F32, 512, 1024, "d_hc")
    _, acc0c = _bwd_ln0(None, dhc, ctx, emb_g, emb_b, sc1c, tbc, "bwd_ln0_ctx")

    dh1 = _matmul_nn([(dz, wz.T), (dxd, wxd.T), (dup, wup.T)], F32, MM_ROWS, 1024, "d_h1")
    g_wz = _matmul_tn(h1, dz, MM_ROWS, 1024, "g_in_proj_z")
    g_wxd = _matmul_tn(h1, dxd, MM_ROWS, D_XD, "g_in_proj_xd") + _matmul_tn(hc, dxd_c, 512, D_XD, "g_in_proj_xd_ctx")
    g_wpo = _matmul_tn(h1, dup, MM_ROWS, 1024, "g_in_proj_up")
    g_win = jnp.concatenate([g_wz, g_wxd[:, 0:D_XBC + 2 * SSD_HEADS], g_wpo], axis=1)
    grad_x, acc0 = _bwd_ln0(dr1, dh1, x, emb_g, emb_b, sc1, tb, "bwd_ln0")

    zero_d = jnp.zeros((1, d), F32)
    dmod = jnp.concatenate([acc0[1:2], acc0[0:1], acc1[4:5], acc1[1:2], acc1[0:1], acc2[2:3]], axis=1)
    dmodc = jnp.concatenate([acc0c[1:2], acc0c[0:1]] + [zero_d] * 4, axis=1)

    big = dict(in_proj=g_win)
    if late_shards is None:
        big.update(late_grads)
    sml = dict(
        dmod=dmod, dmod_ctx=dmodc, emb_ln_g=acc0[2] + acc0c[2], emb_ln_b=acc0[3] + acc0c[3],
        conv_w=accw[0:D_CONV] + accw_c[0:D_CONV], conv_b=accb[0] + accb_c[0],
        dt_bias=accs[:, 0, 0:SSD_HEADS] + accs_c[:, 0, 0:SSD_HEADS],
        a_log=accs[:, 1, 0:SSD_HEADS] + accs_c[:, 1, 0:SSD_HEADS],
        d_skip=jnp.sum(accm[1].reshape(SSD_HEADS, HEAD_DIM), axis=1),
        ssd_norm_g=accm[0], pool_scale=accp[0], ln1_g=acc1[2], ln1_b=acc1[3], ln2_g=acc2[0], ln2_b=acc2[1])
    return loss, grad_x, big, sml, (arrived if late_shards is not None else None)


_SMALL = ("c_ctx", "emb_ln_g", "emb_ln_b", "b_ada", "conv_w", "conv_b", "dt_bias", "a_log", "d_skip",
          "ssd_norm_g", "pool_scale", "ln1_g", "ln1_b", "ln2_g", "ln2_b")


def _small_rows(size):
    return -(-size // 1024)


def _pack_small(vals, names):
    pieces, rows = [], 0
    for nme in names:
        flat = vals[nme].reshape(-1).astype(F32)
        nr = _small_rows(flat.shape[0])
        pieces.append(flat)
        if nr * 1024 > flat.shape[0]:
            pieces.append(jnp.zeros((nr * 1024 - flat.shape[0],), F32))
        rows += nr
    if rows % 8:
        pieces.append(jnp.zeros(((8 - rows % 8) * 1024,), F32))
    return jnp.concatenate(pieces).reshape(-1, 1024)


def _small_offsets(shapes, names):
    out, off = {}, 0
    for nme in names:
        nr = _small_rows(math.prod(shapes[nme]))
        out[nme] = (off, nr)
        off += nr
    return out


def _unpack_small(packed, shapes, names):
    out = {}
    for nme, (off, nr) in _small_offsets(shapes, names).items():
        out[nme] = packed[off:off + nr].reshape(-1)[:math.prod(shapes[nme])].reshape(shapes[nme])
    return out


_WEIGHT_ORDER = ("c_ctx", "emb_ln_g", "emb_ln_b", "w_ada", "b_ada", "in_proj", "conv_w", "conv_b", "dt_bias", "a_log",
                 "d_skip", "ssd_norm_g", "pool_w", "pool_scale", "w_out", "ln1_g", "ln1_b", "w_gate", "w_up", "w_down",
                 "ln2_g", "ln2_b")


def _as2d(a):
    return a.reshape(-1, a.shape[-1])


def kernel(x, c, ctx, c_ctx, emb_ln_g, emb_ln_b, w_ada, b_ada, in_proj, conv_w, conv_b, dt_bias, a_log, d_skip, ssd_norm_g, pool_w, pool_scale, w_out, ln1_g, ln1_b, w_gate, w_up, w_down, ln2_g, ln2_b, loss_target, m_c_ctx, m_emb_ln_g, m_emb_ln_b, m_w_ada, m_b_ada, m_in_proj, m_conv_w, m_conv_b, m_dt_bias, m_a_log, m_d_skip, m_ssd_norm_g, m_pool_w, m_pool_scale, m_w_out, m_ln1_g, m_ln1_b, m_w_gate, m_w_up, m_w_down, m_ln2_g, m_ln2_b, v_c_ctx, v_emb_ln_g, v_emb_ln_b, v_w_ada, v_b_ada, v_in_proj, v_conv_w, v_conv_b, v_dt_bias, v_a_log, v_d_skip, v_ssd_norm_g, v_pool_w, v_pool_scale, v_w_out, v_ln1_g, v_ln1_b, v_w_gate, v_w_up, v_w_down, v_ln2_g, v_ln2_b):
    w = dict(c_ctx=c_ctx, emb_ln_g=emb_ln_g, emb_ln_b=emb_ln_b, w_ada=w_ada, b_ada=b_ada, in_proj=in_proj, conv_w=conv_w,
             conv_b=conv_b, dt_bias=dt_bias, a_log=a_log, d_skip=d_skip, ssd_norm_g=ssd_norm_g, pool_w=pool_w,
             pool_scale=pool_scale, w_out=w_out, ln1_g=ln1_g, ln1_b=ln1_b, w_gate=w_gate, w_up=w_up, w_down=w_down,
             ln2_g=ln2_g, ln2_b=ln2_b)
    m = dict(c_ctx=m_c_ctx, emb_ln_g=m_emb_ln_g, emb_ln_b=m_emb_ln_b, w_ada=m_w_ada, b_ada=m_b_ada, in_proj=m_in_proj,
             conv_w=m_conv_w, conv_b=m_conv_b, dt_bias=m_dt_bias, a_log=m_a_log, d_skip=m_d_skip,
             ssd_norm_g=m_ssd_norm_g, pool_w=m_pool_w, pool_scale=m_pool_scale, w_out=m_w_out, ln1_g=m_ln1_g,
             ln1_b=m_ln1_b, w_gate=m_w_gate, w_up=m_w_up, w_down=m_w_down, ln2_g=m_ln2_g, ln2_b=m_ln2_b)
    v = dict(c_ctx=v_c_ctx, emb_ln_g=v_emb_ln_g, emb_ln_b=v_emb_ln_b, w_ada=v_w_ada, b_ada=v_b_ada, in_proj=v_in_proj,
             conv_w=v_conv_w, conv_b=v_conv_b, dt_bias=v_dt_bias, a_log=v_a_log, d_skip=v_d_skip,
             ssd_norm_g=v_ssd_norm_g, pool_w=v_pool_w, pool_scale=v_pool_scale, w_out=v_w_out, ln1_g=v_ln1_g,
             ln1_b=v_ln1_b, w_gate=v_w_gate, w_up=v_w_up, w_down=v_w_down, ln2_g=v_ln2_g, ln2_b=v_ln2_b)

    xi, yi, ci = _mesh_pos()
    chip = 2 * xi + yi

    dev = 4 * xi + 2 * yi + ci
    d = D_MODEL
    core = ci.reshape(1).astype(jnp.int32)

    crow = jnp.concatenate([c.reshape(1, d), jnp.zeros((7, d), F32)], axis=0)
    _, c_all = _allreduce_small(crow, "gather_c")
    c16 = jnp.concatenate([c_all[:, 0, :], c_ctx.reshape(1, d), jnp.zeros((MOD_ROWS - 9, d), F32)], axis=0)
    ncol = w_ada.shape[-1]
    wada_bf = w_ada[0].astype(BF16)
    b_mine = lax.dynamic_slice_in_dim(b_ada, chip * ncol, ncol, axis=1)
    mods4 = _chip_bcast(_mods_fwd(c16, wada_bf, b_mine), "gather_mods")
    mods = mods4.transpose(1, 0, 2).reshape(MOD_ROWS, 4 * ncol)
    mod = lax.dynamic_slice_in_dim(mods, dev, 1, axis=0)
    mod_ctx = mods[8:9]

    shard = {name: w[name][0].astype(BF16).reshape(shp) for name, shp, _ in _BIG}
    conv8 = jnp.concatenate([conv_w[0], jnp.zeros((8 - D_CONV, conv_w.shape[-1]), F32)], axis=0)
    *gathered, conv4 = _gather_weights([shard[nme] for nme in _EARLY], conv8)
    wts = {nme: _assemble(nme, t) for nme, t in zip(_EARLY, gathered)}
    w8 = conv4.transpose(1, 0, 2).reshape(8, D_XBC)
    small = {nme: (w[nme] if nme in ("c_ctx", "emb_ln_g", "emb_ln_b") else w[nme][0]) for nme in _SMALL if nme != "conv_w"}

    loss, grad_x, big, sml, late_parts = _device_step(x[0], mod, mod_ctx, ctx[0], loss_target[0], wts, w8, small, 512,
                                                      late_shards=[shard[nme] for nme in _LATE], core=core)
    loss = lax.psum(loss, ("x", "y", "c"))

    g_big = _reduce_grads(big, late_parts, core)
    reduced = tuple(sml)
    small_shapes = {nme: sml[nme].shape for nme in reduced}
    total, each = _allreduce_small(_pack_small(sml, reduced), "reduce_small")
    g_small = _unpack_small(total, small_shapes, reduced)
    cw_cols = conv_w.shape[-1]
    g_small["conv_w"] = lax.dynamic_slice_in_dim(g_small["conv_w"], chip * cw_cols, cw_cols, axis=1)

    off, nr = _small_offsets(small_shapes, reduced)["dmod"]
    dm16 = jnp.concatenate([each[:, off:off + nr, :].reshape(8, nr * 1024)[:, :6 * d], g_small["dmod_ctx"],
                            jnp.zeros((MOD_ROWS - 9, 6 * d), F32)], axis=0)
    dm_mine = lax.dynamic_slice_in_dim(dm16, chip * ncol, ncol, axis=1)
    g_wada = _mods_bwd_w(c16.T, dm_mine)
    g_small["b_ada"] = _mods_bwd_b(dm16)[0:1]
    c_part = _mods_bwd_c(dm_mine, wada_bf, c16)[8:16]
    g_small["c_ctx"] = _allreduce_small(c_part, "reduce_c_ctx")[0][0]

    grads, delta, new_m, new_v = {}, {}, {}, {}
    grads["w_ada"] = g_wada[None]
    delta["w_ada"], new_m["w_ada"], new_v["w_ada"] = (
        t[None] for t in _adamw(w_ada[0], g_wada, m_w_ada[0], v_w_ada[0], "adamw_w_ada"))
    for name, _, _ in _BIG:
        g2 = _as2d(g_big[name])
        d2, m2, v2 = _adamw(_as2d(w[name][0]), g2, _as2d(m[name][0]), _as2d(v[name][0]), "adamw_" + name)
        grads[name] = g2.reshape(w[name].shape)
        delta[name], new_m[name], new_v[name] = (t.reshape(w[name].shape) for t in (d2, m2, v2))
    shp = {nme: w[nme].shape for nme in _SMALL}
    gp = _pack_small(g_small, _SMALL)
    dp, mp, vp = _adamw(_pack_small(w, _SMALL), gp, _pack_small(m, _SMALL), _pack_small(v, _SMALL), "adamw_small")
    for dst, src in ((grads, gp), (delta, dp), (new_m, mp), (new_v, vp)):
        dst.update(_unpack_small(src, shp, _SMALL))

    return (loss, grad_x[None], *[grads[nme] for nme in _WEIGHT_ORDER], *[delta[nme] for nme in _WEIGHT_ORDER],
            *[new_m[nme] for nme in _WEIGHT_ORDER], *[new_v[nme] for nme in _WEIGHT_ORDER])
```

```python
import functools
import math

import jax
import jax.numpy as jnp
from jax import lax
from jax.experimental import pallas as pl
from jax.experimental.pallas import tpu as pltpu

F32 = jnp.float32
BF16 = jnp.bfloat16
MESH = pl.DeviceIdType.MESH

D_MODEL = 1024
SSD_HEADS = 16
HEAD_DIM = 64
D_STATE = 128
CHUNK = 128
D_CONV = 5
D_XBC = D_MODEL + 2 * D_STATE
D_XD = 1408
N_POOL = 4
POOL_DIM = 256
POOL_WINDOWS = (2, 4, 8, 16)
GRID_W = 64
D_FF = 2816
D_IN_PROJ = 3360
LN_EPS = 1e-5
ALPHA = 2.0 ** 0.25
POOL_TB = 512
MM_ROWS = 1024

ADAM_LR = 0.001
ADAM_B1 = 0.9
ADAM_B2 = 0.999
ADAM_EPS = 1e-08
ADAM_WD = 0.01
ADAM_STEP = 10

VMEM_LIMIT = 56 * 1024 * 1024


def _cp(sem=None):
    return pltpu.CompilerParams(dimension_semantics=sem, vmem_limit_bytes=VMEM_LIMIT)


def _sigmoid(x):
    return 1.0 / (1.0 + jnp.exp(-x))


def _silu(x):
    return x * _sigmoid(x)


def _dsilu(x):
    s = _sigmoid(x)
    return s * (1.0 + x * (1.0 - s))


def _softplus(x):
    t = jnp.exp(-jnp.abs(x))
    u = 1.0 + t
    log1p = jnp.where(u == 1.0, t, jnp.log(u) * t / (u - 1.0 + (u == 1.0)))
    return jnp.maximum(x, 0.0) + log1p


def _split(x, n):
    parts, r = [], x
    for _ in range(n):
        p = r.astype(BF16)
        parts.append(p)
        r = r - p.astype(F32)
    return parts


def _dot(a, b):
    return jnp.dot(a, b, preferred_element_type=F32)


def _dot_nt(a, b):
    return lax.dot_general(a, b, (((1,), (1,)), ((), ())), preferred_element_type=F32)


def _dot_tn(a, b):
    return lax.dot_general(a, b, (((0,), (0,)), ((), ())), preferred_element_type=F32)


def _dot_sel_l(sel_bf, x, n=3):
    out = None
    for p in _split(x, n):
        t = _dot(sel_bf, p)
        out = t if out is None else out + t
    return out


def _dot_sel_r(x, sel_bf, n=3):
    out = None
    for p in _split(x, n):
        t = _dot(p, sel_bf)
        out = t if out is None else out + t
    return out


ROW_SUB = 16


def _row_tiles(tb):
    assert tb % ROW_SUB == 0
    return [slice(s * ROW_SUB, (s + 1) * ROW_SUB) for s in range(tb // ROW_SUB)]


def _fold8(v):
    out = v[0:8, :]
    for r in range(8, v.shape[0], 8):
        out = out + v[r:r + 8, :]
    return out


def _row_block(n, cap=256, mult=8):
    best = None
    for t in range(mult, min(n, cap) + 1, mult):
        if n % t == 0:
            best = t
    return best if best is not None else n


def _vec(v):
    return v.reshape(1, -1).astype(F32)


MOD_ROWS = 16
MOD_TN = 512


def _mods_fwd(c16, w_bf, b):
    r, d = c16.shape
    n = w_bf.shape[1]

    def body(c_ref, w_ref, b_ref, o_ref):
        s = _silu(c_ref[...]).astype(BF16)
        o_ref[...] = _dot(s, w_ref[...]) + b_ref[...]

    return pl.pallas_call(
        body, name="mods_fwd", grid=(n // MOD_TN,),
        in_specs=[pl.BlockSpec((r, d), lambda j: (0, 0)),
                  pl.BlockSpec((d, MOD_TN), lambda j: (0, j)),
                  pl.BlockSpec((1, MOD_TN), lambda j: (0, j))],
        out_specs=pl.BlockSpec((r, MOD_TN), lambda j: (0, j)),
        out_shape=jax.ShapeDtypeStruct((r, n), F32),
        compiler_params=_cp(("arbitrary",)),
    )(c16, w_bf, b)


def _mods_bwd_w(ct16, dm16):
    d = ct16.shape[0]
    n = dm16.shape[1]

    def body(ct_ref, dm_ref, dw_ref):
        s = _silu(ct_ref[...])
        dm = dm_ref[...]
        acc = s[:, 0:1] * dm[0:1, :]
        for r in range(1, 9):
            acc = acc + s[:, r:r + 1] * dm[r:r + 1, :]
        dw_ref[...] = acc

    return pl.pallas_call(
        body, name="mods_bwd_w", grid=(n // MOD_TN,),
        in_specs=[pl.BlockSpec((d, MOD_ROWS), lambda j: (0, 0)),
                  pl.BlockSpec((MOD_ROWS, MOD_TN), lambda j: (0, j))],
        out_specs=pl.BlockSpec((d, MOD_TN), lambda j: (0, j)),
        out_shape=jax.ShapeDtypeStruct((d, n), F32),
        compiler_params=_cp(("arbitrary",)),
    )(ct16, dm16)


def _mods_bwd_c(dm16, w_bf, c16):
    d = c16.shape[1]
    n = dm16.shape[1]
    nk = n // MOD_TN

    def body(dm_ref, w_ref, c_ref, o_ref):
        k = pl.program_id(0)

        @pl.when(k == 0)
        def _():
            o_ref[...] = jnp.zeros_like(o_ref)

        o_ref[...] += _dot_nt(dm_ref[...].astype(BF16), w_ref[...])

        @pl.when(k == nk - 1)
        def _():
            o_ref[...] = o_ref[...] * (0.5 * _dsilu(c_ref[...]))

    return pl.pallas_call(
        body, name="mods_bwd_c", grid=(nk,),
        in_specs=[pl.BlockSpec((MOD_ROWS, MOD_TN), lambda k: (0, k)),
                  pl.BlockSpec((d, MOD_TN), lambda k: (0, k)),
                  pl.BlockSpec((MOD_ROWS, d), lambda k: (0, 0))],
        out_specs=pl.BlockSpec((MOD_ROWS, d), lambda k: (0, 0)),
        out_shape=jax.ShapeDtypeStruct((MOD_ROWS, d), F32),
        compiler_params=_cp(("arbitrary",)),
    )(dm16, w_bf, c16)


def _mods_bwd_b(dm16):
    n = dm16.shape[1]

    def body(dm_ref, o_ref):
        dm = dm_ref[...]
        acc = dm[0:1, :]
        for r in range(1, 9):
            acc = acc + dm[r:r + 1, :]
        o_ref[...] = jnp.broadcast_to(acc, (8, MOD_TN))

    return pl.pallas_call(
        body, name="mods_bwd_b", grid=(n // MOD_TN,),
        in_specs=[pl.BlockSpec((MOD_ROWS, MOD_TN), lambda j: (0, j))],
        out_specs=pl.BlockSpec((8, MOD_TN), lambda j: (0, j)),
        out_shape=jax.ShapeDtypeStruct((8, n), F32),
        compiler_params=_cp(("arbitrary",)),
    )(dm16)


def _ln_stats(x):
    mu = jnp.mean(x, axis=-1, keepdims=True)
    xc = x - mu
    var = jnp.mean(xc * xc, axis=-1, keepdims=True)
    rstd = lax.rsqrt(var + LN_EPS)
    return xc * rstd, rstd


def _ln_bwd(dxhat, xhat, rstd):
    m1 = jnp.mean(dxhat, axis=-1, keepdims=True)
    m2 = jnp.mean(dxhat * xhat, axis=-1, keepdims=True)
    return rstd * (dxhat - m1 - xhat * m2)


def _row_spec(tb, d):
    return pl.BlockSpec((tb, d), lambda i: (i, 0))


def _par_spec(d):
    return pl.BlockSpec((1, d), lambda i: (0, 0))


def _acc_spec(d):
    return pl.BlockSpec((8, d), lambda i: (0, 0))


def _ln_mod(x, g, b, sh, sc, tb, name):
    n, d = x.shape

    def body(x_ref, g_ref, b_ref, sh_ref, sc_ref, x0_ref, h_ref):
        g, b, sh, sc1 = g_ref[...], b_ref[...], sh_ref[...], 1.0 + sc_ref[...]
        for r in _row_tiles(min(tb, n)):
            xhat, _ = _ln_stats(x_ref[r, :])
            x0 = xhat * g + b
            x0_ref[r, :] = x0
            h_ref[r, :] = (x0 * sc1 + sh).astype(BF16)

    return pl.pallas_call(
        body, name=name, grid=(n // tb,),
        in_specs=[_row_spec(tb, d)] + [_par_spec(d)] * 4,
        out_specs=[_row_spec(tb, d), _row_spec(tb, d)],
        out_shape=[jax.ShapeDtypeStruct((n, d), F32), jax.ShapeDtypeStruct((n, d), BF16)],
        compiler_params=_cp(("parallel",)),
    )(x, g, b, sh, sc)


def _res_ln(xres, mix, gate, g, b, sh, sc, tb):
    n, d = xres.shape

    def body(xr_ref, mix_ref, gate_ref, g_ref, b_ref, sh_ref, sc_ref, x1_ref, h_ref):
        gate_v, g, b, sh, sc1 = gate_ref[...], g_ref[...], b_ref[...], sh_ref[...], 1.0 + sc_ref[...]
        for r in _row_tiles(tb):
            xhat, _ = _ln_stats(ALPHA * xr_ref[r, :] + gate_v * mix_ref[r, :])
            x1 = xhat * g + b
            x1_ref[r, :] = x1
            h_ref[r, :] = (x1 * sc1 + sh).astype(BF16)

    return pl.pallas_call(
        body, name="res_ln1", grid=(n // tb,),
        in_specs=[_row_spec(tb, d)] * 2 + [_par_spec(d)] * 5,
        out_specs=[_row_spec(tb, d), _row_spec(tb, d)],
        out_shape=[jax.ShapeDtypeStruct((n, d), F32), jax.ShapeDtypeStruct((n, d), BF16)],
        compiler_params=_cp(("parallel",)),
    )(xres, mix, gate, g, b, sh, sc)


def _final_ln_loss(x1, ffn, gate, g, b, target, tb):
    n, d = x1.shape

    def body(x1_ref, ffn_ref, gate_ref, g_ref, b_ref, t_ref, dffn_ref, dr_ref, acc_ref):
        i = pl.program_id(0)

        @pl.when(i == 0)
        def _():
            acc_ref[...] = jnp.zeros_like(acc_ref)

        gate_v, g, b = gate_ref[...], g_ref[...], b_ref[...]
        parts = [jnp.zeros((8, d), F32)] * 4
        for r in _row_tiles(tb):
            ffn = ffn_ref[r, :]
            xhat, rstd = _ln_stats(ALPHA * x1_ref[r, :] + gate_v * ffn)
            err = xhat * g + b - t_ref[r, :]
            dx2 = err * (1.0 / d)
            dr = _ln_bwd(dx2 * g, xhat, rstd)
            dr_ref[r, :] = dr
            dffn_ref[r, :] = (gate_v * dr).astype(BF16)
            terms = (dx2 * xhat, dx2, dr * ffn, err * err)
            parts = [p + _fold8(t) for p, t in zip(parts, terms)]
        for j, p in enumerate(parts):
            acc_ref[j:j + 1, :] += jnp.sum(p, axis=0, keepdims=True)

    return pl.pallas_call(
        body, name="final_ln_loss", grid=(n // tb,),
        in_specs=[_row_spec(tb, d)] * 2 + [_par_spec(d)] * 3 + [_row_spec(tb, d)],
        out_specs=[_row_spec(tb, d), _row_spec(tb, d), _acc_spec(d)],
        out_shape=[jax.ShapeDtypeStruct((n, d), BF16), jax.ShapeDtypeStruct((n, d), F32),
                   jax.ShapeDtypeStruct((8, d), F32)],
        compiler_params=_cp(("arbitrary",)),
    )(x1, ffn, gate, g, b, target)


def _bwd_ln1(dr2, dh2, x1, x0, mix, gate, sc2, g, tb):
    n, d = x1.shape

    def body(dr2_ref, dh2_ref, x1_ref, x0_ref, mix_ref, gate_ref, sc_ref, g_ref, dmix_ref, dr1_ref, acc_ref):
        i = pl.program_id(0)

        @pl.when(i == 0)
        def _():
            acc_ref[...] = jnp.zeros_like(acc_ref)

        gate_v, g, sc1 = gate_ref[...], g_ref[...], 1.0 + sc_ref[...]
        parts = [jnp.zeros((8, d), F32)] * 5
        for r in _row_tiles(tb):
            dh2 = dh2_ref[r, :]
            mix = mix_ref[r, :]
            dx1 = ALPHA * dr2_ref[r, :] + dh2 * sc1
            xhat, rstd = _ln_stats(ALPHA * x0_ref[r, :] + gate_v * mix)
            dr1 = _ln_bwd(dx1 * g, xhat, rstd)
            dr1_ref[r, :] = dr1
            dmix_ref[r, :] = (gate_v * dr1).astype(BF16)
            terms = (dh2 * x1_ref[r, :], dh2, dx1 * xhat, dx1, dr1 * mix)
            parts = [p + _fold8(t) for p, t in zip(parts, terms)]
        for j, p in enumerate(parts):
            acc_ref[j:j + 1, :] += jnp.sum(p, axis=0, keepdims=True)

    return pl.pallas_call(
        body, name="bwd_ln1", grid=(n // tb,),
        in_specs=[_row_spec(tb, d)] * 5 + [_par_spec(d)] * 3,
        out_specs=[_row_spec(tb, d), _row_spec(tb, d), _acc_spec(d)],
        out_shape=[jax.ShapeDtypeStruct((n, d), BF16), jax.ShapeDtypeStruct((n, d), F32),
                   jax.ShapeDtypeStruct((8, d), F32)],
        compiler_params=_cp(("arbitrary",)),
    )(dr2, dh2, x1, x0, mix, gate, sc2, g)


def _bwd_ln0(dres, dh, x, g, b, sc, tb, name):
    n, d = x.shape
    has_res = dres is not None

    def body(*refs):
        if has_res:
            dres_ref, dh_ref, x_ref, g_ref, b_ref, sc_ref, dx_ref, acc_ref = refs
        else:
            dh_ref, x_ref, g_ref, b_ref, sc_ref, dx_ref, acc_ref = refs
        i = pl.program_id(0)

        @pl.when(i == 0)
        def _():
            acc_ref[...] = jnp.zeros_like(acc_ref)

        g, b, sc1 = g_ref[...], b_ref[...], 1.0 + sc_ref[...]
        parts = [jnp.zeros((8, d), F32)] * 4
        for r in _row_tiles(tb):
            dh = dh_ref[r, :]
            xhat, rstd = _ln_stats(x_ref[r, :])
            x0 = xhat * g + b
            dx0 = dh * sc1
            if has_res:
                dx0 = dx0 + ALPHA * dres_ref[r, :]
            dx_ref[r, :] = _ln_bwd(dx0 * g, xhat, rstd)
            terms = (dh * x0, dh, dx0 * xhat, dx0)
            parts = [p + _fold8(t) for p, t in zip(parts, terms)]
        for j, p in enumerate(parts):
            acc_ref[j:j + 1, :] += jnp.sum(p, axis=0, keepdims=True)

    ins = ([dres] if has_res else []) + [dh, x, g, b, sc]
    return pl.pallas_call(
        body, name=name, grid=(n // tb,),
        in_specs=[_row_spec(tb, d)] * (3 if has_res else 2) + [_par_spec(d)] * 3,
        out_specs=[_row_spec(tb, d), _acc_spec(d)],
        out_shape=[jax.ShapeDtypeStruct((n, d), F32), jax.ShapeDtypeStruct((8, d), F32)],
        compiler_params=_cp(("arbitrary",)),
    )(*ins)


def _matmul_nn(pairs, out_dtype, tm, tn, name):
    m = pairs[0][0].shape[0]
    n = pairs[0][1].shape[1]
    tm = min(tm, m)
    tn = min(tn, n)
    npair = len(pairs)

    def body(*refs):
        o_ref = refs[-1]
        acc = None
        for p in range(npair):
            t = _dot(refs[2 * p][...].astype(BF16), refs[2 * p + 1][...])
            acc = t if acc is None else acc + t
        o_ref[...] = acc.astype(out_dtype)

    in_specs, args = [], []
    for a, b in pairs:
        k = a.shape[1]
        in_specs += [pl.BlockSpec((tm, k), lambda i, j: (i, 0)), pl.BlockSpec((k, tn), lambda i, j: (0, j))]
        args += [a, b]
    return pl.pallas_call(
        body, name=name, grid=(m // tm, n // tn),
        in_specs=in_specs,
        out_specs=pl.BlockSpec((tm, tn), lambda i, j: (i, j)),
        out_shape=jax.ShapeDtypeStruct((m, n), out_dtype),
        compiler_params=_cp(("parallel", "arbitrary")),
    )(*args)


def _matmul_tn(a, g, tm, tn, name):
    m, k = a.shape
    n = g.shape[1]
    tm = min(tm, m)
    tn = min(tn, n)

    def body(a_ref, g_ref, o_ref):
        i = pl.program_id(1)

        @pl.when(i == 0)
        def _():
            o_ref[...] = jnp.zeros_like(o_ref)

        o_ref[...] += _dot_tn(a_ref[...].astype(BF16), g_ref[...].astype(BF16))

    return pl.pallas_call(
        body, name=name, grid=(n // tn, m // tm),
        in_specs=[pl.BlockSpec((tm, k), lambda j, i: (i, 0)), pl.BlockSpec((tm, tn), lambda j, i: (i, j))],
        out_specs=pl.BlockSpec((k, tn), lambda j, i: (0, j)),
        out_shape=jax.ShapeDtypeStruct((k, n), F32),
        compiler_params=_cp(("parallel", "arbitrary")),
    )(a, g)


def _matmul_nn_multi(a, bs, out_dtypes, tm, name):
    m, k = a.shape
    tm = min(tm, m)
    nb = len(bs)

    def body(a_ref, *refs):
        av = a_ref[...].astype(BF16)
        for j in range(nb):
            refs[nb + j][...] = _dot(av, refs[j][...]).astype(out_dtypes[j])

    return pl.pallas_call(
        body, name=name, grid=(m // tm,),
        in_specs=[pl.BlockSpec((tm, k), lambda i: (i, 0))] + [pl.BlockSpec(b.shape, lambda i: (0, 0)) for b in bs],
        out_specs=[pl.BlockSpec((tm, b.shape[1]), lambda i: (i, 0)) for b in bs],
        out_shape=[jax.ShapeDtypeStruct((m, b.shape[1]), dt) for b, dt in zip(bs, out_dtypes)],
        compiler_params=_cp(("parallel",)),
    )(a, *bs)


def _matmul_tn_multi(a, gs, tm, tns, name):
    m, k = a.shape
    tm = min(tm, m)
    ng = len(gs)
    nj = gs[0].shape[1] // tns[0]
    assert all(g.shape[1] // t == nj and g.shape[1] % t == 0 for g, t in zip(gs, tns))

    def body(a_ref, *refs):
        i = pl.program_id(1)

        @pl.when(i == 0)
        def _():
            for j in range(ng):
                refs[ng + j][...] = jnp.zeros_like(refs[ng + j])

        av = a_ref[...].astype(BF16)
        for j in range(ng):
            refs[ng + j][...] += _dot_tn(av, refs[j][...].astype(BF16))

    return pl.pallas_call(
        body, name=name, grid=(nj, m // tm),
        in_specs=[pl.BlockSpec((tm, k), lambda j, i: (i, 0))] +
                 [pl.BlockSpec((tm, t), lambda j, i: (i, j)) for t in tns],
        out_specs=[pl.BlockSpec((k, t), lambda j, i: (0, j)) for t in tns],
        out_shape=[jax.ShapeDtypeStruct((k, g.shape[1]), F32) for g in gs],
        compiler_params=_cp(("parallel", "arbitrary")),
    )(a, *gs)


def _swiglu_fwd(h, wg, wu, tm, tn):
    m, k = h.shape
    n = wg.shape[1]
    tm = min(tm, m)

    def body(h_ref, wg_ref, wu_ref, gate_ref, up_ref, hmid_ref):
        hv = h_ref[...]
        gate = _dot(hv, wg_ref[...])
        up = _dot(hv, wu_ref[...])
        gate_ref[...] = gate.astype(BF16)
        up_ref[...] = up.astype(BF16)
        hmid_ref[...] = (_silu(gate) * up).astype(BF16)

    blk = pl.BlockSpec((tm, tn), lambda i, j: (i, j))
    wspec = pl.BlockSpec((k, tn), lambda i, j: (0, j))
    return pl.pallas_call(
        body, name="swiglu_fwd", grid=(m // tm, n // tn),
        in_specs=[pl.BlockSpec((tm, k), lambda i, j: (i, 0)), wspec, wspec],
        out_specs=[blk, blk, blk],
        out_shape=[jax.ShapeDtypeStruct((m, n), BF16), jax.ShapeDtypeStruct((m, n), BF16),
                   jax.ShapeDtypeStruct((m, n), BF16)],
        compiler_params=_cp(("parallel", "arbitrary")),
    )(h, wg, wu)


def _swiglu_bwd(dffn, wdt, gate, up, tm, tn):
    m, k = dffn.shape
    n = wdt.shape[1]
    tm = min(tm, m)

    def body(d_ref, w_ref, gate_ref, up_ref, dg_ref, du_ref):
        dh = _dot(d_ref[...], w_ref[...])
        gate = gate_ref[...].astype(F32)
        dg_ref[...] = (dh * up_ref[...].astype(F32) * _dsilu(gate)).astype(BF16)
        du_ref[...] = (dh * _silu(gate)).astype(BF16)

    blk = pl.BlockSpec((tm, tn), lambda i, j: (i, j))
    return pl.pallas_call(
        body, name="swiglu_bwd", grid=(m // tm, n // tn),
        in_specs=[pl.BlockSpec((tm, k), lambda i, j: (i, 0)), pl.BlockSpec((k, tn), lambda i, j: (0, j)), blk, blk],
        out_specs=[blk, blk],
        out_shape=[jax.ShapeDtypeStruct((m, n), BF16), jax.ShapeDtypeStruct((m, n), BF16)],
        compiler_params=_cp(("parallel", "arbitrary")),
    )(dffn, wdt, gate, up)


def _halo_specs(tb, width, nrows):
    r8 = tb // 8
    last = nrows // 8 - 1
    prev = pl.BlockSpec((8, width), lambda i: (jnp.maximum(i * r8 - 1, 0), 0))
    nxt = pl.BlockSpec((8, width), lambda i: (jnp.minimum((i + 1) * r8, last), 0))
    return prev, nxt


CONV_SUB = 32


def _halo_scratch():
    return [pltpu.VMEM((CONV_SUB + 16, D_XBC), F32), pltpu.VMEM((CONV_SUB + 16, D_XBC), F32)]


def _shifted_rows(prev_ref, cur_ref, next_ref, top, bot, tb, i, nb):
    sub = CONV_SUB
    nsub = tb // sub
    assert nsub >= 2
    top[0:8, :] = prev_ref[...] * (i > 0).astype(F32)
    top[8:sub + 16, :] = cur_ref[0:sub + 8, :]
    bot[0:sub + 8, :] = cur_ref[tb - sub - 8:tb, :]
    bot[sub + 8:sub + 16, :] = next_ref[...] * (i < nb - 1).astype(F32)

    def rows(s, o):
        if s == 0:
            return top[8 + o:8 + o + sub, :]
        if s == nsub - 1:
            return bot[8 + o:8 + o + sub, :]
        return cur_ref[s * sub + o:(s + 1) * sub + o, :]

    return rows


def _conv_fwd(xd, w8, b, tb, name):
    n = xd.shape[0]
    tb = min(tb, n)
    nb = n // tb
    prev, nxt = _halo_specs(tb, D_XBC, n)

    def body(p_ref, c_ref, n_ref, w_ref, b_ref, u_ref, top, bot):
        i = pl.program_id(0)
        rows = _shifted_rows(p_ref, c_ref, n_ref, top, bot, tb, i, nb)
        w = [w_ref[k:k + 1, :] for k in range(D_CONV)]
        bias = jnp.broadcast_to(b_ref[...], (CONV_SUB, D_XBC))
        for s in range(tb // CONV_SUB):
            acc = bias
            for k in range(D_CONV):
                acc = acc + w[k] * rows(s, k - 2)
            u_ref[s * CONV_SUB:(s + 1) * CONV_SUB, :] = acc.astype(BF16)

    return pl.pallas_call(
        body, name=name, grid=(nb,),
        in_specs=[prev, pl.BlockSpec((tb, D_XBC), lambda i: (i, 0)), nxt,
                  pl.BlockSpec((8, D_XBC), lambda i: (0, 0)), _par_spec(D_XBC)],
        out_specs=_row_spec(tb, D_XBC),
        out_shape=jax.ShapeDtypeStruct((n, D_XBC), BF16),
        scratch_shapes=_halo_scratch(),
        compiler_params=_cp(("parallel",)),
    )(xd, xd, xd, w8, b)


def _conv_bwd_a(dxs, dy, dskip_e, dbc, u, tb, name):
    n = u.shape[0]
    tb = min(tb, n)

    def body(dxs_ref, dy_ref, sk_ref, dbc_ref, u_ref, du_ref, acc_ref):
        i = pl.program_id(0)

        @pl.when(i == 0)
        def _():
            acc_ref[...] = jnp.zeros_like(acc_ref)

        sk = sk_ref[...]
        part = jnp.zeros((8, D_XBC), F32)
        for r in _row_tiles(tb):
            gx = dxs_ref[0, r, :].astype(F32) + dxs_ref[1, r, :].astype(F32) + dy_ref[r, :].astype(F32) * sk
            gbc = dbc_ref[0, r, :] + dbc_ref[1, r, :]
            du = jnp.concatenate([gx, gbc], axis=1) * _dsilu(u_ref[r, :].astype(F32))
            du_ref[r, :] = du
            part = part + _fold8(du)
        acc_ref[0:1, :] += jnp.sum(part, axis=0, keepdims=True)

    return pl.pallas_call(
        body, name=name, grid=(n // tb,),
        in_specs=[pl.BlockSpec((2, tb, D_MODEL), lambda i: (0, i, 0)), _row_spec(tb, D_MODEL), _par_spec(D_MODEL),
                  pl.BlockSpec((2, tb, 2 * D_STATE), lambda i: (0, i, 0)), _row_spec(tb, D_XBC)],
        out_specs=[_row_spec(tb, D_XBC), _acc_spec(D_XBC)],
        out_shape=[jax.ShapeDtypeStruct((n, D_XBC), F32), jax.ShapeDtypeStruct((8, D_XBC), F32)],
        compiler_params=_cp(("arbitrary",)),
    )(dxs, dy, dskip_e, dbc, u)


def _conv_bwd_b(du, xd, ddt, w8, tb, name):
    n = du.shape[0]
    tb = min(tb, n)
    nb = n // tb
    prev, nxt = _halo_specs(tb, D_XBC, n)

    def body(dp_ref, dc_ref, dn_ref, xp_ref, xc_ref, xn_ref, ddt_ref, w_ref, dxd_ref, acc_ref, dtop, dbot, xtop, xbot):
        i = pl.program_id(0)

        @pl.when(i == 0)
        def _():
            acc_ref[...] = jnp.zeros_like(acc_ref)

        sub = CONV_SUB
        nsub = tb // sub
        du_rows = _shifted_rows(dp_ref, dc_ref, dn_ref, dtop, dbot, tb, i, nb)
        x_rows = _shifted_rows(xp_ref, xc_ref, xn_ref, xtop, xbot, tb, i, nb)
        w = [w_ref[k:k + 1, :] for k in range(D_CONV)]
        for s in range(nsub):
            acc = w[0] * du_rows(s, 2)
            for k in range(1, D_CONV):
                acc = acc + w[k] * du_rows(s, 2 - k)
            dxd_ref[s * sub:(s + 1) * sub, 0:D_XBC] = acc.astype(BF16)
        for k in range(D_CONV):
            part = jnp.zeros((8, D_XBC), F32)
            for s in range(nsub):
                prod = dc_ref[s * sub:(s + 1) * sub, :] * x_rows(s, k - 2)
                for r in range(0, sub, 8):
                    part = part + prod[r:r + 8, :]
            acc_ref[k:k + 1, :] += jnp.sum(part, axis=0, keepdims=True)
        ddt = ddt_ref[0] + pltpu.roll(ddt_ref[1], SSD_HEADS, 1)
        dxd_ref[:, D_XBC:D_XD] = ddt.astype(BF16)

    cur = pl.BlockSpec((tb, D_XBC), lambda i: (i, 0))
    return pl.pallas_call(
        body, name=name, grid=(nb,),
        in_specs=[prev, cur, nxt, prev, cur, nxt,
                  pl.BlockSpec((2, tb, 128), lambda i: (0, i, 0)), pl.BlockSpec((8, D_XBC), lambda i: (0, 0))],
        out_specs=[_row_spec(tb, D_XD), _acc_spec(D_XBC)],
        out_shape=[jax.ShapeDtypeStruct((n, D_XD), BF16), jax.ShapeDtypeStruct((8, D_XBC), F32)],
        scratch_shapes=_halo_scratch() + _halo_scratch(),
        compiler_params=_cp(("arbitrary",)),
    )(du, du, du, xd, xd, xd, ddt, w8)


def _ssd_chunk_index(nc, reverse):
    def idx(d, k):
        kk = (nc - 1 - k) if reverse else k
        return kk + d * (nc - 1 - 2 * kk)
    return idx


def _ssd_prologue(d, u_ref, xd_ref, bias_ref, a_ref, r_ref):
    q = CHUNK
    xbc = _silu(u_ref[...].astype(F32))
    xs = xbc[:, 0:D_MODEL]
    bm = xbc[:, D_MODEL:D_MODEL + D_STATE]
    cm = xbc[:, D_MODEL + D_STATE:D_XBC]
    row = lax.broadcasted_iota(jnp.int32, (q, q), 0)
    col = lax.broadcasted_iota(jnp.int32, (q, q), 1)
    sgn = 1 - 2 * d
    mask = ((row - col) * sgn) >= 0
    mask_t = ((row - col) * sgn) <= 0
    xdv = xd_ref[...]
    dtraw = jnp.where(d == 0, xdv, pltpu.roll(xdv, 128 - SSD_HEADS, 1)) + bias_ref[...]
    head_lane = col < SSD_HEADS
    dt = jnp.where(head_lane, _softplus(dtraw), 0.0)
    a = a_ref[...]
    tri = jnp.where(mask, 1.0, 0.0).astype(BF16)
    acum = _dot_sel_l(tri, dt * a)
    rexp = r_ref[...]
    alast = jnp.where(d == 0, acum[q - 1:q, :], acum[0:1, :])
    e16 = jnp.exp(acum)
    dend16 = jnp.exp(alast - acum)
    wend16 = dend16 * dt
    e = _dot_sel_r(e16, rexp, n=2)
    wend_e = _dot_sel_r(wend16, rexp, n=2)
    elast_e = jnp.where(d == 0, e[q - 1:q, :], e[0:1, :])
    g = _dot_nt(cm.astype(BF16), bm.astype(BF16))
    return dict(xs=xs, bm=bm, cm=cm, mask=mask, mask_t=mask_t, dtraw=dtraw, head_lane=head_lane, dt=dt, a=a,
                acum=acum, acum_t=acum.T, dt_t=dt.T, e16=e16, dend16=dend16, wend16=wend16, e=e, wend_e=wend_e,
                elast_e=elast_e, g=g, col=col, row=row)


def _ssd_head_mats(p, h):
    seg = p["acum"][:, h:h + 1] - p["acum_t"][h:h + 1, :]
    lm = jnp.exp(jnp.where(p["mask"], seg, -jnp.inf))
    gl = p["g"] * lm
    s = gl * p["dt_t"][h:h + 1, :]
    return lm, gl, s


def _ssd_fwd(u, xd, bias2, a2, rexp, h0, name, gather=()):
    n = u.shape[0]
    nc = n // CHUNK
    q = CHUNK
    cidx = _ssd_chunk_index(nc, reverse=False)
    ng = len(gather)

    def body(u_ref, xd_ref, bias_ref, a_ref, r_ref, h0_ref, *rest):
        g_ins, (y_ref, hp_ref, hf_ref), rest = rest[:ng], rest[ng:ng + 3], rest[ng + 3:]
        g_outs, st, sems = rest[:ng], rest[ng], rest[ng + 1:]
        d = pl.program_id(0)
        k = pl.program_id(1)
        if ng:
            g_start, g_finish = _gather_steps(g_ins, g_outs, *sems)
            pl.when((d == 0) & (k == 0))(g_start)

        @pl.when(k == 0)
        def _():
            st[...] = h0_ref[...]

        p = _ssd_prologue(d, u_ref, xd_ref, bias_ref, a_ref, r_ref)
        stv = st[...]
        st_bf = stv.astype(BF16)
        hp_ref[...] = st_bf
        xs = p["xs"]
        lane128 = p["col"]
        y_off = _dot(p["cm"].astype(BF16), st_bf) * p["e"]
        for pb in range(SSD_HEADS // 2):
            _, _, s0 = _ssd_head_mats(p, 2 * pb)
            _, _, s1 = _ssd_head_mats(p, 2 * pb + 1)
            xp = xs[:, pb * 128:(pb + 1) * 128]
            rhs = jnp.concatenate([jnp.where(lane128 < HEAD_DIM, xp, 0.0), jnp.where(lane128 >= HEAD_DIM, xp, 0.0)],
                                  axis=0).astype(BF16)
            lhs = jnp.concatenate([s0, s1], axis=1).astype(BF16)
            y_ref[:, pb * 128:(pb + 1) * 128] = (_dot(lhs, rhs) + y_off[:, pb * 128:(pb + 1) * 128]).astype(BF16)
        xw = (xs * p["wend_e"]).astype(BF16)
        new = stv * p["elast_e"] + _dot(p["bm"].T.astype(BF16), xw)
        st[...] = new
        hf_ref[...] = new
        if ng:
            pl.when((d == 1) & (k == nc - 1))(g_finish)

    nsem = _GATHER_SEMS * ng
    return pl.pallas_call(
        body, name=name, grid=(2, nc),
        in_specs=[pl.BlockSpec((q, D_XBC), lambda d, k: (cidx(d, k), 0)),
                  pl.BlockSpec((q, 128), lambda d, k: (cidx(d, k), D_XBC // 128)),
                  pl.BlockSpec((None, 1, 128), lambda d, k: (d, 0, 0)),
                  pl.BlockSpec((None, 1, 128), lambda d, k: (d, 0, 0)),
                  pl.BlockSpec((128, D_MODEL), lambda d, k: (0, 0)),
                  pl.BlockSpec((None, D_STATE, D_MODEL), lambda d, k: (d, 0, 0))] + [_ANY] * ng,
        out_specs=[pl.BlockSpec((None, q, D_MODEL), lambda d, k: (d, cidx(d, k), 0)),
                   pl.BlockSpec((None, None, D_STATE, D_MODEL), lambda d, k: (d, cidx(d, k), 0, 0)),
                   pl.BlockSpec((None, D_STATE, D_MODEL), lambda d, k: (d, 0, 0))] + [_ANY] * ng,
        out_shape=[jax.ShapeDtypeStruct((2, n, D_MODEL), BF16),
                   jax.ShapeDtypeStruct((2, nc, D_STATE, D_MODEL), BF16),
                   jax.ShapeDtypeStruct((2, D_STATE, D_MODEL), F32)] +
                  [jax.ShapeDtypeStruct((4,) + t.shape, t.dtype) for t in gather],
        scratch_shapes=[pltpu.VMEM((D_STATE, D_MODEL), F32)] +
                       ([pltpu.SemaphoreType.DMA((nsem,)), pltpu.SemaphoreType.DMA((nsem,))] if ng else []),
        compiler_params=_cp(("arbitrary", "arbitrary")),
    )(u, xd, bias2, a2, rexp, h0, *gather)


def _ssd_bwd(u, xd, bias2, a2, rexp, rexp_t, dy, hprev, lam0, name, exchange=()):
    n = u.shape[0]
    nc = n // CHUNK
    q = CHUNK
    cidx = _ssd_chunk_index(nc, reverse=True)
    ne = len(exchange)

    def body(u_ref, xd_ref, bias_ref, a_ref, r_ref, rt_ref, dy_ref, hp_ref, lam0_ref, *rest):
        e_ins, (dxs_ref, dbc_ref, ddt_ref, acc_ref, lamo_ref), rest = rest[:ne], rest[ne:ne + 5], rest[ne + 5:]
        e_outs, lam, sems = rest[:ne], rest[ne], rest[ne + 1:]
        d = pl.program_id(0)
        k = pl.program_id(1)
        if ne:
            e_start, e_finish = _exchange_steps(e_ins, e_outs, *sems)
            pl.when((d == 0) & (k == 0))(e_start)

        @pl.when(k == 0)
        def _():
            lam[...] = lam0_ref[...]
            acc_ref[...] = jnp.zeros_like(acc_ref)

        rexp_t = rt_ref[...]

        def hsum(t):
            return _dot_sel_r(t, rexp_t, n=2)

        p = _ssd_prologue(d, u_ref, xd_ref, bias_ref, a_ref, r_ref)
        xs, bm, cm = p["xs"], p["bm"], p["cm"]
        bm_bf, cm_bf = bm.astype(BF16), cm.astype(BF16)
        lamn = lam[...]
        lamn_bf = lamn.astype(BF16)
        stp = hp_ref[...]
        dyv = dy_ref[...].astype(F32)
        lane128 = p["col"]

        wend_e = p["wend_e"]
        cs = _dot(cm_bf, stp)
        dye_bf = (dyv * p["e"]).astype(BF16)
        dc_off = _dot_nt(dye_bf, stp)
        v = _dot(bm_bf, lamn_bf)
        xw_bf = (xs * wend_e).astype(BF16)
        db_off = _dot_nt(xw_bf, lamn_bf)
        elast_e = p["elast_e"]
        dlast_e = jnp.sum(stp.astype(F32) * lamn, axis=0, keepdims=True) * elast_e
        lam_new = lamn * elast_e + _dot(cm.T.astype(BF16), dye_bf)
        lam[...] = lam_new
        lamo_ref[...] = lam_new

        hs_vx = hsum(v * xs)
        om = p["wend16"] * hs_vx
        x1 = p["e16"] * hsum(dyv * cs) - om
        x2 = p["dend16"] * hs_vx
        x3 = jnp.sum(om, axis=0, keepdims=True) + hsum(jnp.broadcast_to(dlast_e, (8, D_MODEL)))[0:1, :]

        sub16 = lax.broadcasted_iota(jnp.int32, (SSD_HEADS, q), 0)
        rs = jnp.zeros((q, 128), F32)
        cs_m = jnp.zeros((SSD_HEADS, q), F32)
        dt_m = jnp.zeros((SSD_HEADS, q), F32)
        dg = jnp.zeros((q, q), F32)
        for pb in range(SSD_HEADS // 2):
            xp_bf = xs[:, pb * 128:(pb + 1) * 128].astype(BF16)
            dyp = dyv[:, pb * 128:(pb + 1) * 128]
            dxs_pair = None
            for half in range(2):
                h = 2 * pb + half
                sel = (lane128 < HEAD_DIM) if half == 0 else (lane128 >= HEAD_DIM)
                dyh_bf = jnp.where(sel, dyp, 0.0).astype(BF16)
                lm, gl, s = _ssd_head_mats(p, h)
                ds = _dot_nt(dyh_bf, xp_bf)
                t = _dot_tn(s.astype(BF16), dyh_bf)
                dxs_pair = t if dxs_pair is None else dxs_pair + t
                w = ds * s
                rs = rs + jnp.sum(w, axis=1, keepdims=True) * (lane128 == h).astype(F32)
                cs_m = jnp.where(sub16 == h, jnp.sum(w, axis=0, keepdims=True), cs_m)
                dt_m = jnp.where(sub16 == h, jnp.sum(ds * gl, axis=0, keepdims=True), dt_m)
                dg = dg + ds * lm * p["dt_t"][h:h + 1, :]
            sl = slice(pb * 128, (pb + 1) * 128)
            dxs_ref[:, sl] = (dxs_pair + v[:, sl] * wend_e[:, sl]).astype(BF16)

        def to_lanes(m16):
            return jnp.concatenate([m16, jnp.zeros((128 - SSD_HEADS, q), F32)], axis=0).T

        last = jnp.where(d == 0, q - 1, 0)
        dacum = rs - to_lanes(cs_m) + x1 + jnp.where(p["row"] == last, x3[0:1, :], 0.0)
        tri_t = jnp.where(p["mask_t"], 1.0, 0.0).astype(BF16)
        ddta = _dot_sel_l(tri_t, dacum)
        dt = p["dt"]
        a = p["a"]
        ddt = to_lanes(dt_m) + x2 + a * ddta
        ddtraw = jnp.where(p["head_lane"], ddt * _sigmoid(p["dtraw"]), 0.0)
        ddt_ref[...] = ddtraw
        acc_ref[0:1, :] += jnp.sum(ddtraw, axis=0, keepdims=True)
        acc_ref[1:2, :] += jnp.sum(dt * ddta, axis=0, keepdims=True) * a

        dg_bf = dg.astype(BF16)
        dbc_ref[:, 0:D_STATE] = _dot_tn(dg_bf, cm_bf) + db_off
        dbc_ref[:, D_STATE:2 * D_STATE] = _dot(dg_bf, bm_bf) + dc_off
        if ne:
            pl.when((d == 1) & (k == nc - 1))(e_finish)

    cblk = lambda d, k: (cidx(d, k), 0)
    return pl.pallas_call(
        body, name=name, grid=(2, nc),
        in_specs=[pl.BlockSpec((q, D_XBC), cblk),
                  pl.BlockSpec((q, 128), lambda d, k: (cidx(d, k), D_XBC // 128)),
                  pl.BlockSpec((None, 1, 128), lambda d, k: (d, 0, 0)),
                  pl.BlockSpec((None, 1, 128), lambda d, k: (d, 0, 0)),
                  pl.BlockSpec((128, D_MODEL), lambda d, k: (0, 0)),
                  pl.BlockSpec((D_MODEL, 128), lambda d, k: (0, 0)),
                  pl.BlockSpec((q, D_MODEL), cblk),
                  pl.BlockSpec((None, None, D_STATE, D_MODEL), lambda d, k: (d, cidx(d, k), 0, 0)),
                  pl.BlockSpec((None, D_STATE, D_MODEL), lambda d, k: (d, 0, 0))] + [_ANY] * ne,
        out_specs=[pl.BlockSpec((None, q, D_MODEL), lambda d, k: (d, cidx(d, k), 0)),
                   pl.BlockSpec((None, q, 2 * D_STATE), lambda d, k: (d, cidx(d, k), 0)),
                   pl.BlockSpec((None, q, 128), lambda d, k: (d, cidx(d, k), 0)),
                   pl.BlockSpec((None, 8, 128), lambda d, k: (d, 0, 0)),
                   pl.BlockSpec((None, D_STATE, D_MODEL), lambda d, k: (d, 0, 0))] + [_ANY] * ne,
        out_shape=[jax.ShapeDtypeStruct((2, n, D_MODEL), BF16),
                   jax.ShapeDtypeStruct((2, n, 2 * D_STATE), F32),
                   jax.ShapeDtypeStruct((2, n, 128), F32),
                   jax.ShapeDtypeStruct((2, 8, 128), F32),
                   jax.ShapeDtypeStruct((2, D_STATE, D_MODEL), F32)] +
                  [jax.ShapeDtypeStruct(t.shape, t.dtype) for t in exchange],
        scratch_shapes=[pltpu.VMEM((D_STATE, D_MODEL), F32)] +
                       ([pltpu.SemaphoreType.DMA((3 * ne,)), pltpu.SemaphoreType.DMA((3 * ne,)),
                         pltpu.SemaphoreType.DMA((ne,))] if ne else []),
        compiler_params=_cp(("arbitrary", "arbitrary")),
    )(u, xd, bias2, a2, rexp, rexp_t, dy, hprev, lam0, *exchange)


def _merge_fwd(y, u, z, dskip_e, gn, tb):
    n = z.shape[0]

    def body(y_ref, u_ref, z_ref, sk_ref, gn_ref, o_ref):
        sk, gnv = sk_ref[...], gn_ref[...]
        for r in _row_tiles(tb):
            ys = y_ref[0, r, :].astype(F32) + y_ref[1, r, :].astype(F32) + sk * _silu(u_ref[r, :].astype(F32))
            gated = ys * _silu(z_ref[r, :].astype(F32))
            rstd = lax.rsqrt(jnp.mean(gated * gated, axis=-1, keepdims=True) + LN_EPS)
            o_ref[r, :] = (gated * rstd * gnv).astype(BF16)

    return pl.pallas_call(
        body, name="merge_fwd", grid=(n // tb,),
        in_specs=[pl.BlockSpec((2, tb, D_MODEL), lambda i: (0, i, 0)), pl.BlockSpec((tb, D_MODEL), lambda i: (i, 0)),
                  _row_spec(tb, D_MODEL), _par_spec(D_MODEL), _par_spec(D_MODEL)],
        out_specs=_row_spec(tb, D_MODEL),
        out_shape=jax.ShapeDtypeStruct((n, D_MODEL), BF16),
        compiler_params=_cp(("parallel",)),
    )(y, u, z, dskip_e, gn)


def _merge_bwd(dyn, y, u, z, dskip_e, gn, tb):
    n = z.shape[0]

    def body(dyn_ref, y_ref, u_ref, z_ref, sk_ref, gn_ref, dy_ref, dz_ref, acc_ref):
        i = pl.program_id(0)

        @pl.when(i == 0)
        def _():
            acc_ref[...] = jnp.zeros_like(acc_ref)

        sk, gnv = sk_ref[...], gn_ref[...]
        part0 = jnp.zeros((8, D_MODEL), F32)
        part1 = jnp.zeros((8, D_MODEL), F32)
        for s in range(tb // ROW_SUB):
            r = slice(s * ROW_SUB, (s + 1) * ROW_SUB)
            xs = _silu(u_ref[r, :].astype(F32))
            zv = z_ref[r, :].astype(F32)
            sz = _sigmoid(zv)
            ys = y_ref[0, r, :].astype(F32) + y_ref[1, r, :].astype(F32) + sk * xs
            gated = ys * (zv * sz)
            rstd = lax.rsqrt(jnp.mean(gated * gated, axis=-1, keepdims=True) + LN_EPS)
            ghat = gated * rstd
            dyn_v = dyn_ref[r, :].astype(F32)
            t = dyn_v * gnv
            dgated = rstd * (t - ghat * jnp.mean(t * ghat, axis=-1, keepdims=True))
            dys = dgated * (zv * sz)
            dy_ref[r, :] = dys.astype(BF16)
            dz_ref[r, :] = (dgated * ys * (sz * (1.0 + zv * (1.0 - sz)))).astype(BF16)
            part0 = part0 + _fold8(dyn_v * ghat)
            part1 = part1 + _fold8(dys * xs)
        acc_ref[0:1, :] += jnp.sum(part0, axis=0, keepdims=True)
        acc_ref[1:2, :] += jnp.sum(part1, axis=0, keepdims=True)

    return pl.pallas_call(
        body, name="merge_bwd", grid=(n // tb,),
        in_specs=[_row_spec(tb, D_MODEL), pl.BlockSpec((2, tb, D_MODEL), lambda i: (0, i, 0)),
                  pl.BlockSpec((tb, D_MODEL), lambda i: (i, 0)), _row_spec(tb, D_MODEL),
                  _par_spec(D_MODEL), _par_spec(D_MODEL)],
        out_specs=[_row_spec(tb, D_MODEL), _row_spec(tb, D_MODEL), _acc_spec(D_MODEL)],
        out_shape=[jax.ShapeDtypeStruct((n, D_MODEL), BF16), jax.ShapeDtypeStruct((n, D_MODEL), BF16),
                   jax.ShapeDtypeStruct((8, D_MODEL), F32)],
        compiler_params=_cp(("arbitrary",)),
    )(dyn, y, u, z, dskip_e, gn)


def _pool_consts(transpose):
    tb = POOL_TB
    t = jnp.arange(tb)
    s = jnp.arange(3 * tb)
    rl, cl = t // GRID_W, t % GRID_W
    rs_, cs_ = s // GRID_W - tb // GRID_W, s % GRID_W
    s2 = jnp.arange(tb)
    rl2, cl2 = s2 // GRID_W, s2 % GRID_W
    brow, bcol = [], []
    for w in POOL_WINDOWS:
        lo, hi = -(w // 2), w - w // 2
        if transpose:
            lo, hi = -hi + 1, -lo + 1
        dr = rs_[None, :] - rl[:, None]
        before, after = _pool_halo(w, transpose)
        full = ((cs_[None, :] == cl[:, None]) & (dr >= lo) & (dr < hi)).astype(BF16)
        brow.append(full[:, tb - before:2 * tb + after])
        dc = cl2[None, :] - cl[:, None]
        bcol.append(((rl2[None, :] == rl[:, None]) & (dc >= lo) & (dc < hi)).astype(BF16))
    return brow, jnp.stack(bcol)


def _pool_halo(w, transpose):
    lo, hi = -(w // 2), w - w // 2
    if transpose:
        lo, hi = -hi + 1, -lo + 1
    return -lo * GRID_W, (hi - 1) * GRID_W


def _pool_inv(i, g, n):
    assert GRID_W == 64
    t = i * POOL_TB + lax.broadcasted_iota(jnp.int32, (POOL_TB, 1), 0)
    r = lax.shift_right_logical(t, 6)
    col = t & (GRID_W - 1)
    w = POOL_WINDOWS[g]
    lo, hi = -(w // 2), w - w // 2
    cnt_r = jnp.minimum(r + hi, n // GRID_W) - jnp.maximum(r + lo, 0)
    cnt_c = jnp.minimum(col + hi, GRID_W) - jnp.maximum(col + lo, 0)
    return 1.0 / (cnt_r * cnt_c).astype(F32)


def _pool_box(prev_ref, cur_ref, next_ref, brow_refs, bcol_ref, g, i, nb, transpose):
    tb = POOL_TB
    sl = slice(g * POOL_DIM, (g + 1) * POOL_DIM)
    before, after = _pool_halo(POOL_WINDOWS[g], transpose)
    pieces = []
    if before:
        pieces.append((prev_ref[tb - before:tb, sl] * (i > 0).astype(prev_ref.dtype)).astype(BF16))
    pieces.append(cur_ref[:, sl].astype(BF16))
    if after:
        pieces.append((next_ref[0:after, sl] * (i < nb - 1).astype(next_ref.dtype)).astype(BF16))
    r = _dot(brow_refs[g][...], jnp.concatenate(pieces, axis=0))
    return _dot(bcol_ref[g], r.astype(BF16))


def _pool_halo_specs(n, d):
    tb = POOL_TB
    nb = n // tb
    prev = pl.BlockSpec((tb, d), lambda i: (jnp.maximum(i - 1, 0), 0))
    cur = pl.BlockSpec((tb, d), lambda i: (i, 0))
    nxt = pl.BlockSpec((tb, d), lambda i: (jnp.minimum(i + 1, nb - 1), 0))
    return prev, cur, nxt


def _pool_const_specs(brow):
    tb = POOL_TB
    return [pl.BlockSpec(b.shape, lambda i: (0, 0)) for b in brow] + [pl.BlockSpec((N_POOL, tb, tb), lambda i: (0, 0, 0))]


def _pool_fwd(up, consts, pw_bf, pscale):
    n = up.shape[0]
    tb = POOL_TB
    nb = n // tb
    brow, bcol = consts
    prev, cur, nxt = _pool_halo_specs(n, D_MODEL)

    def body(p_ref, c_ref, n_ref, *rest):
        brow_refs, (bcol_ref, pw_ref, sc_ref, o_ref, d_ref) = rest[:N_POOL], rest[N_POOL:]
        i = pl.program_id(0)
        for g in range(N_POOL):
            sl = slice(g * POOL_DIM, (g + 1) * POOL_DIM)
            box = _pool_box(p_ref, c_ref, n_ref, brow_refs, bcol_ref, g, i, nb, False)
            dd = (box * _pool_inv(i, g, n) - c_ref[:, sl].astype(F32)).astype(BF16)
            d_ref[:, sl] = dd
            o_ref[:, sl] = (_dot(dd, pw_ref[g]) * sc_ref[:, sl]).astype(BF16)

    return pl.pallas_call(
        body, name="pool_fwd", grid=(nb,),
        in_specs=[prev, cur, nxt] + _pool_const_specs(brow) +
                 [pl.BlockSpec((N_POOL, POOL_DIM, POOL_DIM), lambda i: (0, 0, 0)), _par_spec(D_MODEL)],
        out_specs=[_row_spec(tb, D_MODEL), _row_spec(tb, D_MODEL)],
        out_shape=[jax.ShapeDtypeStruct((n, D_MODEL), BF16), jax.ShapeDtypeStruct((n, D_MODEL), BF16)],
        compiler_params=_cp(("parallel",)),
    )(up, up, up, *brow, bcol, pw_bf, pscale)


def _pool_bwd_a(dp, dsave, pw_bf, pwt_bf, pscale):
    n = dp.shape[0]
    tb = POOL_TB

    def body(dp_ref, d_ref, pw_ref, pwt_ref, sc_ref, dd_ref, dds_ref, gw_ref, gs_ref):
        i = pl.program_id(0)

        @pl.when(i == 0)
        def _():
            gw_ref[...] = jnp.zeros_like(gw_ref)
            gs_ref[...] = jnp.zeros_like(gs_ref)

        for g in range(N_POOL):
            sl = slice(g * POOL_DIM, (g + 1) * POOL_DIM)
            dpv = dp_ref[:, sl].astype(F32)
            dv = d_ref[:, sl]
            dpw_bf = (dpv * sc_ref[:, sl]).astype(BF16)
            dd = _dot(dpw_bf, pwt_ref[g])
            dd_ref[:, sl] = dd.astype(BF16)
            dds_ref[:, sl] = (dd * _pool_inv(i, g, n)).astype(BF16)
            gw_ref[g] += _dot_tn(dv, dpw_bf)
            gs_ref[0:1, sl] += jnp.sum(dpv * _dot(dv, pw_ref[g]), axis=0, keepdims=True)

    wspec = pl.BlockSpec((N_POOL, POOL_DIM, POOL_DIM), lambda i: (0, 0, 0))
    return pl.pallas_call(
        body, name="pool_bwd_a", grid=(n // tb,),
        in_specs=[_row_spec(tb, D_MODEL), _row_spec(tb, D_MODEL), wspec, wspec, _par_spec(D_MODEL)],
        out_specs=[_row_spec(tb, D_MODEL), _row_spec(tb, D_MODEL), wspec, _acc_spec(D_MODEL)],
        out_shape=[jax.ShapeDtypeStruct((n, D_MODEL), BF16), jax.ShapeDtypeStruct((n, D_MODEL), BF16),
                   jax.ShapeDtypeStruct((N_POOL, POOL_DIM, POOL_DIM), F32), jax.ShapeDtypeStruct((8, D_MODEL), F32)],
        compiler_params=_cp(("arbitrary",)),
    )(dp, dsave, pw_bf, pwt_bf, pscale)


def _pool_bwd_b(dds, dd, consts_t):
    n = dd.shape[0]
    tb = POOL_TB
    nb = n // tb
    brow, bcol = consts_t
    prev, cur, nxt = _pool_halo_specs(n, D_MODEL)

    def body(p_ref, c_ref, n_ref, *rest):
        brow_refs, (bcol_ref, dd_ref, o_ref) = rest[:N_POOL], rest[N_POOL:]
        i = pl.program_id(0)
        for g in range(N_POOL):
            sl = slice(g * POOL_DIM, (g + 1) * POOL_DIM)
            box = _pool_box(p_ref, c_ref, n_ref, brow_refs, bcol_ref, g, i, nb, True)
            o_ref[:, sl] = (box - dd_ref[:, sl].astype(F32)).astype(BF16)

    return pl.pallas_call(
        body, name="pool_bwd_b", grid=(nb,),
        in_specs=[prev, cur, nxt] + _pool_const_specs(brow) + [_row_spec(tb, D_MODEL)],
        out_specs=_row_spec(tb, D_MODEL),
        out_shape=jax.ShapeDtypeStruct((n, D_MODEL), BF16),
        compiler_params=_cp(("parallel",)),
    )(dds, dds, dds, *brow, bcol, dd)


def _pair_add(slabs, recvs, core, name):
    na = len(slabs)
    hr = [t.shape[1] // 4 for t in slabs]

    def body(core_ref, *refs):
        for a in range(na):
            refs[2 * na + a][...] = (refs[a][...] + refs[na + a][...]).astype(BF16)

    own = [pl.BlockSpec((None, hr[a], slabs[a].shape[2]), lambda j, i, c_ref: (j, 2 * c_ref[0] + i, 0)) for a in range(na)]
    got = [pl.BlockSpec((None, hr[a], slabs[a].shape[2]), lambda j, i, c_ref: (j, i, 0)) for a in range(na)]
    return pl.pallas_call(
        body, name=name,
        grid_spec=pltpu.PrefetchScalarGridSpec(num_scalar_prefetch=1, grid=(4, 2), in_specs=own + got, out_specs=got),
        out_shape=[jax.ShapeDtypeStruct(r.shape, BF16) for r in recvs],
        compiler_params=_cp(("arbitrary", "arbitrary")),
    )(core, *slabs, *recvs)


def _sum4(parts, core):
    na = len(parts)
    hr = [t.shape[1] // 2 for t in parts]

    def body(core_ref, *refs):
        for a in range(na):
            p = refs[a]
            refs[na + a][...] = ((p[0].astype(F32) + p[1].astype(F32)) + p[2].astype(F32)) + p[3].astype(F32)

    return pl.pallas_call(
        body, name="reduce_g_sum",
        grid_spec=pltpu.PrefetchScalarGridSpec(
            num_scalar_prefetch=1, grid=(2,),
            in_specs=[pl.BlockSpec((4, hr[a], parts[a].shape[2]), lambda i, c_ref: (0, i, 0)) for a in range(na)],
            out_specs=[pl.BlockSpec((hr[a], parts[a].shape[2]), lambda i, c_ref: (2 * c_ref[0] + i, 0))
                       for a in range(na)]),
        out_shape=[jax.ShapeDtypeStruct((2 * t.shape[1], t.shape[2]), F32) for t in parts],
        compiler_params=_cp(("arbitrary",)),
    )(core, *parts)


def _adamw(w, g, m, v, name):
    r, cdim = w.shape
    tb = _row_block(r, 256)
    c1 = 1.0 - ADAM_B1 ** ADAM_STEP
    c2 = 1.0 - ADAM_B2 ** ADAM_STEP

    def body(w_ref, g_ref, m_ref, v_ref, d_ref, nm_ref, nv_ref):
        gv = g_ref[...]
        nm = ADAM_B1 * m_ref[...] + (1.0 - ADAM_B1) * gv
        nv = ADAM_B2 * v_ref[...] + (1.0 - ADAM_B2) * (gv * gv)
        m_hat = nm / c1
        v_hat = nv / c2
        d_ref[...] = -ADAM_LR * (m_hat / (jnp.sqrt(v_hat) + ADAM_EPS) + ADAM_WD * w_ref[...])
        nm_ref[...] = nm
        nv_ref[...] = nv

    spec = _row_spec(tb, cdim)
    shp = jax.ShapeDtypeStruct((r, cdim), F32)
    return pl.pallas_call(
        body, name=name, grid=(r // tb,),
        in_specs=[spec] * 4, out_specs=[spec] * 3, out_shape=[shp] * 3,
        compiler_params=_cp(("parallel",)),
    )(w, g, m, v)


def _mesh_pos():
    return lax.axis_index("x"), lax.axis_index("y"), lax.axis_index("c")


_ANY = pl.BlockSpec(memory_space=pl.ANY)


def _remote(src, dst, send_sem, recv_sem, device):
    return pltpu.make_async_remote_copy(src_ref=src, dst_ref=dst, send_sem=send_sem, recv_sem=recv_sem,
                                        device_id=device, device_id_type=MESH)


def _other_chips(x, y):
    return [(1 - x, y), (x, 1 - y), (1 - x, 1 - y)]


def _half(nrows, h):
    return pl.ds(h * (nrows // 2), nrows // 2)


_GATHER_SEMS = 7


def _gather_steps(ins, outs, send_sems, recv_sems):
    na = len(ins)
    nrow = [r.shape[0] for r in ins]

    def copies():
        x, y, c = _mesh_pos()
        me = 2 * x + y
        sib = (x, y, 1 - c)
        chips = _other_chips(x, y)

        def ici(k, a, slot):
            px, py = chips[k]
            rows = _half(nrow[a], c)
            return _remote(ins[a].at[rows, :], outs[a].at[slot, rows, :], send_sems.at[k * na + a],
                           recv_sems.at[k * na + a], (px, py, c))

        def fwd(k, a, h):
            px, py = chips[k]
            blk = outs[a].at[2 * px + py, _half(nrow[a], h), :]
            return _remote(blk, blk, send_sems.at[(3 + k) * na + a], recv_sems.at[(3 + k) * na + a], sib)

        def own(a):
            return _remote(ins[a], outs[a].at[me], send_sems.at[6 * na + a], recv_sems.at[6 * na + a], sib)

        slots = [2 * px + py for px, py in chips]
        return ici, fwd, own, me, c, slots

    def start():
        ici, _, own, me, _, _ = copies()
        for a in range(na):
            own(a).start()
        for k in range(3):
            for a in range(na):
                ici(k, a, me).start()

    def finish():
        ici, fwd, own, me, c, slots = copies()
        for k in range(3):
            for a in range(na):
                ici(k, a, slots[k]).wait_recv()
                fwd(k, a, c).start()
        for k in range(3):
            for a in range(na):
                fwd(k, a, 1 - c).wait_recv()
        for a in range(na):
            own(a).wait_recv()
        for a in range(na):
            own(a).wait_send()
        for k in range(3):
            for a in range(na):
                ici(k, a, me).wait_send()
                fwd(k, a, c).wait_send()

    return start, finish


def _exchange_steps(ins, outs, send_sems, recv_sems, local_sems):
    na = len(ins)

    def copies():
        x, y, c = _mesh_pos()
        me = 2 * x + y
        chips = _other_chips(x, y)

        def copy(k, a, slot):
            px, py = chips[k]
            return _remote(ins[a].at[2 * px + py], outs[a].at[slot], send_sems.at[k * na + a], recv_sems.at[k * na + a],
                           (px, py, c))

        def local(a):
            return pltpu.make_async_copy(ins[a].at[me], outs[a].at[me], local_sems.at[a])

        return copy, local, me, [2 * px + py for px, py in chips]

    def start():
        copy, local, me, _ = copies()
        for a in range(na):
            local(a).start()
        for k in range(3):
            for a in range(na):
                copy(k, a, me).start()

    def finish():
        copy, local, me, slots = copies()
        for k in range(3):
            for a in range(na):
                copy(k, a, slots[k]).wait_recv()
        for k in range(3):
            for a in range(na):
                copy(k, a, me).wait_send()
        for a in range(na):
            local(a).wait()

    return start, finish


def _gather_weights(shards, conv8):
    na = len(shards)

    def body(*refs):
        ins, conv_in = refs[:na], refs[na]
        outs, conv_out = refs[na + 1:2 * na + 1], refs[2 * na + 1]
        send_sems, recv_sems, local_sems = refs[2 * na + 2:]
        x, y, c = _mesh_pos()
        me = 2 * x + y
        chips = _other_chips(x, y)

        def conv(k, slot):
            px, py = chips[k]
            return _remote(conv_in, conv_out.at[slot], send_sems.at[7 * na + k], recv_sems.at[7 * na + k], (px, py, c))

        start, finish = _gather_steps(ins, outs, send_sems, recv_sems)
        local = pltpu.make_async_copy(conv_in, conv_out.at[me], local_sems.at[0])
        local.start()
        start()
        sends = [conv(k, me) for k in range(3)]
        for cp in sends:
            cp.start()
        finish()
        for k in range(3):
            px, py = chips[k]
            conv(k, 2 * px + py).wait_recv()
        for cp in sends:
            cp.wait_send()
        local.wait()

    nsem = _GATHER_SEMS * na + 3
    return pl.pallas_call(
        body, name="gather_w", in_specs=[_ANY] * (na + 1), out_specs=[_ANY] * (na + 1),
        out_shape=[jax.ShapeDtypeStruct((4,) + t.shape, t.dtype) for t in shards] +
                  [jax.ShapeDtypeStruct((4,) + conv8.shape, conv8.dtype)],
        scratch_shapes=[pltpu.SemaphoreType.DMA((nsem,)), pltpu.SemaphoreType.DMA((nsem,)),
                        pltpu.SemaphoreType.DMA((1,))],
    )(*shards, conv8)


def _pair_swap(slabs, name):
    na = len(slabs)

    def body(*refs):
        ins, outs = refs[:na], refs[na:2 * na]
        send_sems, recv_sems = refs[2 * na:]
        x, y, c = _mesh_pos()
        cps = [_remote(ins[a].at[:, _half(slabs[a].shape[1], 1 - c), :], outs[a], send_sems.at[a], recv_sems.at[a],
                       (x, y, 1 - c)) for a in range(na)]
        for cp in cps:
            cp.start()
        for cp in cps:
            cp.wait()

    return pl.pallas_call(
        body, name=name, in_specs=[_ANY] * na, out_specs=[_ANY] * na,
        out_shape=[jax.ShapeDtypeStruct((4, t.shape[1] // 2, t.shape[2]), t.dtype) for t in slabs],
        scratch_shapes=[pltpu.SemaphoreType.DMA((na,)), pltpu.SemaphoreType.DMA((na,))],
    )(*slabs)


def _chip_exchange(pairs):
    na = len(pairs)

    def body(*refs):
        start, finish = _exchange_steps(refs[:na], refs[na:2 * na], *refs[2 * na:])
        start()
        finish()

    return pl.pallas_call(
        body, name="reduce_g_ici", in_specs=[_ANY] * na, out_specs=[_ANY] * na,
        out_shape=[jax.ShapeDtypeStruct(t.shape, t.dtype) for t in pairs],
        scratch_shapes=[pltpu.SemaphoreType.DMA((3 * na,)), pltpu.SemaphoreType.DMA((3 * na,)),
                        pltpu.SemaphoreType.DMA((na,))],
    )(*pairs)


def _share_halves(totals):
    na = len(totals)

    def body(*refs):
        bufs = refs[na:2 * na]
        send_sems, recv_sems = refs[2 * na:]
        x, y, c = _mesh_pos()

        def copy(a, h):
            blk = bufs[a].at[_half(totals[a].shape[0], h), :]
            return _remote(blk, blk, send_sems.at[a], recv_sems.at[a], (x, y, 1 - c))

        sends = [copy(a, c) for a in range(na)]
        for cp in sends:
            cp.start()
        for a in range(na):
            copy(a, 1 - c).wait_recv()
        for cp in sends:
            cp.wait_send()

    return pl.pallas_call(
        body, name="reduce_g_share", in_specs=[_ANY] * na, out_specs=[_ANY] * na,
        out_shape=[jax.ShapeDtypeStruct(t.shape, t.dtype) for t in totals],
        input_output_aliases={a: a for a in range(na)},
        scratch_shapes=[pltpu.SemaphoreType.DMA((na,)), pltpu.SemaphoreType.DMA((na,))],
    )(*totals)


def _allreduce_small(v, name):
    r, cdim = v.shape

    def body(v_ref, out_ref, buf, send_sems, recv_sems):
        x, y, c = _mesh_pos()
        me = 4 * x + 2 * y + c
        buf[me] = v_ref[...]
        rel = [(bx, by, bc) for bx in (0, 1) for by in (0, 1) for bc in (0, 1)][1:]

        def peer(b):
            bx, by, bc = b
            return ((1 - x) if bx else x, (1 - y) if by else y, (1 - c) if bc else c)

        def copy(k, slot):
            return pltpu.make_async_remote_copy(
                src_ref=v_ref, dst_ref=buf.at[slot], send_sem=send_sems.at[k], recv_sem=recv_sems.at[k],
                device_id=peer(rel[k]), device_id_type=MESH)

        sends = [copy(k, me) for k in range(7)]
        for cp in sends:
            cp.start()
        for k in range(7):
            px, py, pc = peer(rel[k])
            copy(k, 4 * px + 2 * py + pc).wait_recv()
        for cp in sends:
            cp.wait_send()
        acc = buf[0]
        for j in range(1, 8):
            acc = acc + buf[j]
        out_ref[...] = acc

    vm = pl.BlockSpec(memory_space=pltpu.VMEM)
    return pl.pallas_call(
        body, name=name, in_specs=[vm], out_specs=[vm, vm],
        out_shape=[jax.ShapeDtypeStruct((r, cdim), F32), jax.ShapeDtypeStruct((8, r, cdim), F32)],
        scratch_shapes=[pltpu.SemaphoreType.DMA((7,)), pltpu.SemaphoreType.DMA((7,))],
    )(v)


def _chip_bcast(v, name):
    def body(v_ref, out_ref, send_sems, recv_sems):
        x, y, c = _mesh_pos()
        me = 2 * x + y
        chips = _other_chips(x, y)
        out_ref[me] = v_ref[...]

        def copy(k, slot):
            px, py = chips[k]
            return _remote(v_ref, out_ref.at[slot], send_sems.at[k], recv_sems.at[k], (px, py, c))

        sends = [copy(k, me) for k in range(3)]
        for cp in sends:
            cp.start()
        for k, (px, py) in enumerate(chips):
            copy(k, 2 * px + py).wait_recv()
        for cp in sends:
            cp.wait_send()

    vm = pl.BlockSpec(memory_space=pltpu.VMEM)
    return pl.pallas_call(
        body, name=name, in_specs=[vm], out_specs=vm,
        out_shape=jax.ShapeDtypeStruct((4,) + v.shape, F32),
        scratch_shapes=[pltpu.SemaphoreType.DMA((3,)), pltpu.SemaphoreType.DMA((3,))],
    )(v)


_BIG = (("in_proj", (D_MODEL, D_IN_PROJ // 4), 1), ("w_out", (2 * D_MODEL // 4, D_MODEL), 0),
        ("w_gate", (D_MODEL, D_FF // 4), 1), ("w_up", (D_MODEL, D_FF // 4), 1), ("w_down", (D_FF // 4, D_MODEL), 0),
        ("pool_w", (N_POOL * POOL_DIM // 4, POOL_DIM), None))


def _assemble(name, t):
    _, r, c = t.shape
    axis = {n: ax for n, _, ax in _BIG}[name]
    if axis == 0:
        return t.reshape(4 * r, c)
    if axis == 1:
        return t.transpose(1, 0, 2).reshape(r, 4 * c)
    return t.reshape(4, N_POOL, POOL_DIM // 4, POOL_DIM).transpose(1, 0, 2, 3).reshape(N_POOL, POOL_DIM, POOL_DIM)


def _to_slabs(name, g):
    (r, c), axis = {n: (sh, ax) for n, sh, ax in _BIG}[name]
    if axis == 0:
        return g.reshape(4, r, c)
    if axis == 1:
        return g.reshape(r, 4, c).transpose(1, 0, 2)
    return g.reshape(N_POOL, 4, POOL_DIM // 4, POOL_DIM).transpose(1, 0, 2, 3).reshape(4, r, c)


_EARLY = ("in_proj",)
_LATE = tuple(n for n, _, _ in _BIG if n not in _EARLY)


def _reduce_grads(early_grads, late_parts, core):
    slabs = [_to_slabs(n, early_grads[n]) for n in _EARLY]
    pairs = _pair_add(slabs, _pair_swap(slabs, "reduce_g_d2d"), core, "reduce_g_pair")
    parts = dict(zip(_EARLY, _chip_exchange(pairs)), **dict(zip(_LATE, late_parts)))
    names = [n for n, _, _ in _BIG]
    totals = _sum4([parts[n] for n in names], core)
    return dict(zip(names, _share_halves(totals)))


def _pad_cols(w, n):
    return jnp.concatenate([w, jnp.zeros((w.shape[0], n - w.shape[1]), w.dtype)], axis=1)


def _device_step(x, mod, mod_ctx, ctx, target, wts, w8, small, tb, late_shards=None, core=None):
    n = x.shape[0]
    d = D_MODEL

    win = wts["in_proj"]
    wz, wxd, wup = win[:, 0:d], _pad_cols(win[:, d:d + D_XBC + 2 * SSD_HEADS], D_XD), win[:, d + D_XBC + 2 * SSD_HEADS:]

    emb_g, emb_b = _vec(small["emb_ln_g"]), _vec(small["emb_ln_b"])
    ln1_g, ln1_b = _vec(small["ln1_g"]), _vec(small["ln1_b"])
    ln2_g, ln2_b = _vec(small["ln2_g"]), _vec(small["ln2_b"])
    gn = _vec(small["ssd_norm_g"])
    pscale = _vec(small["pool_scale"])
    conv_b = _vec(small["conv_b"])
    dskip_e = jnp.repeat(small["d_skip"].reshape(-1), HEAD_DIM).reshape(1, d)
    zpad = jnp.zeros((2, 1, 128 - SSD_HEADS), F32)
    bias2 = jnp.concatenate([small["dt_bias"].reshape(2, 1, SSD_HEADS), zpad], axis=2)
    a2 = jnp.concatenate([-jnp.exp(small["a_log"].reshape(2, 1, SSD_HEADS)), zpad], axis=2)
    rexp = (jnp.arange(128)[:, None] == (jnp.arange(d)[None, :] // HEAD_DIM)).astype(BF16)
    rexp_t = rexp.T

    sh1, sc1, g1, sh2, sc2, g2 = [mod[:, i * d:(i + 1) * d] for i in range(6)]
    sh1c, sc1c = mod_ctx[:, 0:d], mod_ctx[:, d:2 * d]

    tbc = min(tb, ctx.shape[0])
    xc0, hc = _ln_mod(ctx, emb_g, emb_b, sh1c, sc1c, tbc, "ln_mod_ctx")
    xdc = _matmul_nn([(hc, wxd)], F32, 512, D_XD, "in_proj_ctx")
    uc = _conv_fwd(xdc, w8, conv_b, tbc, "conv_fwd_ctx")
    hzero = jnp.zeros((2, D_STATE, d), F32)
    _, hprev_c, hfin_c = _ssd_fwd(uc, xdc, bias2, a2, rexp, hzero, "ssd_fwd_ctx")

    x0, h1 = _ln_mod(x, emb_g, emb_b, sh1, sc1, tb, "ln_mod")
    z, xd, up = _matmul_nn_multi(h1, [wz, wxd, wup], [BF16, F32, BF16], 512, "in_proj")
    u = _conv_fwd(xd, w8, conv_b, tb, "conv_fwd")
    y, hprev, _, *landed = _ssd_fwd(u, xd, bias2, a2, rexp, hfin_c, "ssd_fwd", gather=late_shards or ())
    if late_shards is not None:
        wts = dict(wts, **{nme: _assemble(nme, t) for nme, t in zip(_LATE, landed)})
    wout = wts["w_out"]
    wg, wu, wd = wts["w_gate"], wts["w_up"], wts["w_down"]
    pw = wts["pool_w"]
    yn = _merge_fwd(y, u, z, dskip_e, gn, tb)
    pconst = _pool_consts(False)
    pool, dsave = _pool_fwd(up, pconst, pw, pscale)
    mix = _matmul_nn([(yn, wout[0:d]), (pool, wout[d:2 * d])], F32, MM_ROWS, 1024, "out_proj")
    x1, h2 = _res_ln(x0, mix, g1, ln1_g, ln1_b, sh2, sc2, tb)

    gate, upp, hmid = _swiglu_fwd(h2, wg, wu, 512, D_FF // 2)
    ffn = _matmul_nn([(hmid, wd)], F32, MM_ROWS, 1024, "ffn_down")
    dffn, dr2, acc2 = _final_ln_loss(x1, ffn, g2, ln2_g, ln2_b, target, tb)
    loss = (0.5 / d) * jnp.sum(acc2[3])

    dgate, dupp = _swiglu_bwd(dffn, wd.T, gate, upp, 512, D_FF // 2)
    g_wdown = _matmul_tn(hmid, dffn, MM_ROWS, 1024, "g_w_down")
    g_wgate, g_wup = _matmul_tn_multi(h2, [dgate, dupp], MM_ROWS, [D_FF // 2, D_FF // 2], "g_w_gate_up")
    dh2 = _matmul_nn([(dgate, wg.T), (dupp, wu.T)], F32, 512, 1024, "d_h2")
    dmix, dr1, acc1 = _bwd_ln1(dr2, dh2, x1, x0, mix, g1, sc2, ln1_g, tb)

    dyn, dpool = _matmul_nn_multi(dmix, [wout[0:d].T, wout[d:2 * d].T], [BF16, BF16], MM_ROWS, "d_yn_pool")
    g_wout = jnp.concatenate([_matmul_tn(yn, dmix, MM_ROWS, 1024, "g_w_out_a"),
                              _matmul_tn(pool, dmix, MM_ROWS, 1024, "g_w_out_b")], axis=0)
    dd, dds, g_pw, accp = _pool_bwd_a(dpool, dsave, pw, jnp.swapaxes(pw, 1, 2), pscale)
    dup = _pool_bwd_b(dds, dd, _pool_consts(True))
    dy, dz, accm = _merge_bwd(dyn, y, u, z, dskip_e, gn, tb)
    lam0 = jnp.zeros((2, D_STATE, d), F32)
    late_grads = dict(w_out=g_wout, w_gate=g_wgate, w_up=g_wup, w_down=g_wdown, pool_w=g_pw)
    pairs = ()
    if late_shards is not None:
        slabs = [_to_slabs(nme, late_grads[nme]) for nme in _LATE]
        pairs = _pair_add(slabs, _pair_swap(slabs, "reduce_g_d2d_late"), core, "reduce_g_pair_late")
    dxs, dbc, ddt, accs, lam_c, *arrived = _ssd_bwd(u, xd, bias2, a2, rexp, rexp_t, dy, hprev, lam0, "ssd_bwd",
                                                    exchange=pairs)
    du, accb = _conv_bwd_a(dxs, dy, dskip_e, dbc, u, tb, "conv_bwd_a")
    dxd, accw = _conv_bwd_b(du, xd, ddt, w8, tb, "conv_bwd_b")

    lc = ctx.shape[0]
    zeros_c = jnp.zeros((lc, d), BF16)
    dxs_c, dbc_c, ddt_c, accs_c, _ = _ssd_bwd(uc, xdc, bias2, a2, rexp, rexp_t, zeros_c, hprev_c, lam_c, "ssd_bwd_ctx")
    du_c, accb_c = _conv_bwd_a(dxs_c, zeros_c, dskip_e, dbc_c, uc, tbc, "conv_bwd_a_ctx")
    dxd_c, accw_c = _conv_bwd_b(du_c, xdc, ddt_c, w8, tbc, "conv_bwd_b_ctx")
    dhc = _matmul_nn([(dxd_c, wxd.T)], F32, 512, 1024, "d_hc")
    _, acc0c = _bwd_ln0(None, dhc, ctx, emb_g, emb_b, sc1c, tbc, "bwd_ln0_ctx")

    dh1 = _matmul_nn([(dz, wz.T), (dxd, wxd.T), (dup, wup.T)], F32, MM_ROWS, 1024, "d_h1")
    g_wz, g_wxd, g_wpo = _matmul_tn_multi(h1, [dz, dxd, dup], 512, [1024, D_XD, 1024], "g_in_proj")
    g_wxd = g_wxd + _matmul_tn(hc, dxd_c, 512, D_XD, "g_in_proj_xd_ctx")
    g_win = jnp.concatenate([g_wz, g_wxd[:, 0:D_XBC + 2 * SSD_HEADS], g_wpo], axis=1)
    grad_x, acc0 = _bwd_ln0(dr1, dh1, x, emb_g, emb_b, sc1, tb, "bwd_ln0")

    zero_d = jnp.zeros((1, d), F32)
    dmod = jnp.concatenate([acc0[1:2], acc0[0:1], acc1[4:5], acc1[1:2], acc1[0:1], acc2[2:3]], axis=1)
    dmodc = jnp.concatenate([acc0c[1:2], acc0c[0:1]] + [zero_d] * 4, axis=1)

    big = dict(in_proj=g_win)
    if late_shards is None:
        big.update(late_grads)
    sml = dict(
        dmod=dmod, dmod_ctx=dmodc, emb_ln_g=acc0[2] + acc0c[2], emb_ln_b=acc0[3] + acc0c[3],
        conv_w=accw[0:D_CONV] + accw_c[0:D_CONV], conv_b=accb[0] + accb_c[0],
        dt_bias=accs[:, 0, 0:SSD_HEADS] + accs_c[:, 0, 0:SSD_HEADS],
        a_log=accs[:, 1, 0:SSD_HEADS] + accs_c[:, 1, 0:SSD_HEADS],
        d_skip=jnp.sum(accm[1].reshape(SSD_HEADS, HEAD_DIM), axis=1),
        ssd_norm_g=accm[0], pool_scale=accp[0], ln1_g=acc1[2], ln1_b=acc1[3], ln2_g=acc2[0], ln2_b=acc2[1])
    return loss, grad_x, big, sml, (arrived if late_shards is not None else None)


_SMALL = ("c_ctx", "emb_ln_g", "emb_ln_b", "b_ada", "conv_w", "conv_b", "dt_bias", "a_log", "d_skip",
          "ssd_norm_g", "pool_scale", "ln1_g", "ln1_b", "ln2_g", "ln2_b")


def _small_rows(size):
    return -(-size // 1024)


def _pack_small(vals, names):
    pieces, rows = [], 0
    for nme in names:
        flat = vals[nme].reshape(-1).astype(F32)
        nr = _small_rows(flat.shape[0])
        pieces.append(flat)
        if nr * 1024 > flat.shape[0]:
            pieces.append(jnp.zeros((nr * 1024 - flat.shape[0],), F32))
        rows += nr
    if rows % 8:
        pieces.append(jnp.zeros(((8 - rows % 8) * 1024,), F32))
    return jnp.concatenate(pieces).reshape(-1, 1024)


def _small_offsets(shapes, names):
    out, off = {}, 0
    for nme in names:
        nr = _small_rows(math.prod(shapes[nme]))
        out[nme] = (off, nr)
        off += nr
    return out


def _unpack_small(packed, shapes, names):
    out = {}
    for nme, (off, nr) in _small_offsets(shapes, names).items():
        out[nme] = packed[off:off + nr].reshape(-1)[:math.prod(shapes[nme])].reshape(shapes[nme])
    return out


_WEIGHT_ORDER = ("c_ctx", "emb_ln_g", "emb_ln_b", "w_ada", "b_ada", "in_proj", "conv_w", "conv_b", "dt_bias", "a_log",
                 "d_skip", "ssd_norm_g", "pool_w", "pool_scale", "w_out", "ln1_g", "ln1_b", "w_gate", "w_up", "w_down",
                 "ln2_g", "ln2_b")


def _as2d(a):
    return a.reshape(-1, a.shape[-1])


def kernel(x, c, ctx, c_ctx, emb_ln_g, emb_ln_b, w_ada, b_ada, in_proj, conv_w, conv_b, dt_bias, a_log, d_skip, ssd_norm_g, pool_w, pool_scale, w_out, ln1_g, ln1_b, w_gate, w_up, w_down, ln2_g, ln2_b, loss_target, m_c_ctx, m_emb_ln_g, m_emb_ln_b, m_w_ada, m_b_ada, m_in_proj, m_conv_w, m_conv_b, m_dt_bias, m_a_log, m_d_skip, m_ssd_norm_g, m_pool_w, m_pool_scale, m_w_out, m_ln1_g, m_ln1_b, m_w_gate, m_w_up, m_w_down, m_ln2_g, m_ln2_b, v_c_ctx, v_emb_ln_g, v_emb_ln_b, v_w_ada, v_b_ada, v_in_proj, v_conv_w, v_conv_b, v_dt_bias, v_a_log, v_d_skip, v_ssd_norm_g, v_pool_w, v_pool_scale, v_w_out, v_ln1_g, v_ln1_b, v_w_gate, v_w_up, v_w_down, v_ln2_g, v_ln2_b):
    w = dict(c_ctx=c_ctx, emb_ln_g=emb_ln_g, emb_ln_b=emb_ln_b, w_ada=w_ada, b_ada=b_ada, in_proj=in_proj, conv_w=conv_w,
             conv_b=conv_b, dt_bias=dt_bias, a_log=a_log, d_skip=d_skip, ssd_norm_g=ssd_norm_g, pool_w=pool_w,
             pool_scale=pool_scale, w_out=w_out, ln1_g=ln1_g, ln1_b=ln1_b, w_gate=w_gate, w_up=w_up, w_down=w_down,
             ln2_g=ln2_g, ln2_b=ln2_b)
    m = dict(c_ctx=m_c_ctx, emb_ln_g=m_emb_ln_g, emb_ln_b=m_emb_ln_b, w_ada=m_w_ada, b_ada=m_b_ada, in_proj=m_in_proj,
             conv_w=m_conv_w, conv_b=m_conv_b, dt_bias=m_dt_bias, a_log=m_a_log, d_skip=m_d_skip,
             ssd_norm_g=m_ssd_norm_g, pool_w=m_pool_w, pool_scale=m_pool_scale, w_out=m_w_out, ln1_g=m_ln1_g,
             ln1_b=m_ln1_b, w_gate=m_w_gate, w_up=m_w_up, w_down=m_w_down, ln2_g=m_ln2_g, ln2_b=m_ln2_b)
    v = dict(c_ctx=v_c_ctx, emb_ln_g=v_emb_ln_g, emb_ln_b=v_emb_ln_b, w_ada=v_w_ada, b_ada=v_b_ada, in_proj=v_in_proj,
             conv_w=v_conv_w, conv_b=v_conv_b, dt_bias=v_dt_bias, a_log=v_a_log, d_skip=v_d_skip,
             ssd_norm_g=v_ssd_norm_g, pool_w=v_pool_w, pool_scale=v_pool_scale, w_out=v_w_out, ln1_g=v_ln1_g,
             ln1_b=v_ln1_b, w_gate=v_w_gate, w_up=v_w_up, w_down=v_w_down, ln2_g=v_ln2_g, ln2_b=v_ln2_b)

    xi, yi, ci = _mesh_pos()
    chip = 2 * xi + yi

    dev = 4 * xi + 2 * yi + ci
    d = D_MODEL
    core = ci.reshape(1).astype(jnp.int32)

    crow = jnp.concatenate([c.reshape(1, d), jnp.zeros((7, d), F32)], axis=0)
    _, c_all = _allreduce_small(crow, "gather_c")
    c16 = jnp.concatenate([c_all[:, 0, :], c_ctx.reshape(1, d), jnp.zeros((MOD_ROWS - 9, d), F32)], axis=0)
    ncol = w_ada.shape[-1]
    wada_bf = w_ada[0].astype(BF16)
    b_mine = lax.dynamic_slice_in_dim(b_ada, chip * ncol, ncol, axis=1)
    mods4 = _chip_bcast(_mods_fwd(c16, wada_bf, b_mine), "gather_mods")
    mods = mods4.transpose(1, 0, 2).reshape(MOD_ROWS, 4 * ncol)
    mod = lax.dynamic_slice_in_dim(mods, dev, 1, axis=0)
    mod_ctx = mods[8:9]

    shard = {name: w[name][0].astype(BF16).reshape(shp) for name, shp, _ in _BIG}
    conv8 = jnp.concatenate([conv_w[0], jnp.zeros((8 - D_CONV, conv_w.shape[-1]), F32)], axis=0)
    *gathered, conv4 = _gather_weights([shard[nme] for nme in _EARLY], conv8)
    wts = {nme: _assemble(nme, t) for nme, t in zip(_EARLY, gathered)}
    w8 = conv4.transpose(1, 0, 2).reshape(8, D_XBC)
    small = {nme: (w[nme] if nme in ("c_ctx", "emb_ln_g", "emb_ln_b") else w[nme][0]) for nme in _SMALL if nme != "conv_w"}

    loss, grad_x, big, sml, late_parts = _device_step(x[0], mod, mod_ctx, ctx[0], loss_target[0], wts, w8, small, 512,
                                                      late_shards=[shard[nme] for nme in _LATE], core=core)
    loss = lax.psum(loss, ("x", "y", "c"))

    g_big = _reduce_grads(big, late_parts, core)
    reduced = tuple(sml)
    small_shapes = {nme: sml[nme].shape for nme in reduced}
    total, each = _allreduce_small(_pack_small(sml, reduced), "reduce_small")
    g_small = _unpack_small(total, small_shapes, reduced)
    cw_cols = conv_w.shape[-1]
    g_small["conv_w"] = lax.dynamic_slice_in_dim(g_small["conv_w"], chip * cw_cols, cw_cols, axis=1)

    off, nr = _small_offsets(small_shapes, reduced)["dmod"]
    dm16 = jnp.concatenate([each[:, off:off + nr, :].reshape(8, nr * 1024)[:, :6 * d], g_small["dmod_ctx"],
                            jnp.zeros((MOD_ROWS - 9, 6 * d), F32)], axis=0)
    dm_mine = lax.dynamic_slice_in_dim(dm16, chip * ncol, ncol, axis=1)
    g_wada = _mods_bwd_w(c16.T, dm_mine)
    g_small["b_ada"] = _mods_bwd_b(dm16)[0:1]
    c_part = _mods_bwd_c(dm_mine, wada_bf, c16)[8:16]
    g_small["c_ctx"] = _allreduce_small(c_part, "reduce_c_ctx")[0][0]

    grads, delta, new_m, new_v = {}, {}, {}, {}
    grads["w_ada"] = g_wada[None]
    delta["w_ada"], new_m["w_ada"], new_v["w_ada"] = (
        t[None] for t in _adamw(w_ada[0], g_wada, m_w_ada[0], v_w_ada[0], "adamw_w_ada"))
    for name, _, _ in _BIG:
        g2 = _as2d(g_big[name])
        d2, m2, v2 = _adamw(_as2d(w[name][0]), g2, _as2d(m[name][0]), _as2d(v[name][0]), "adamw_" + name)
        grads[name] = g2.reshape(w[name].shape)
        delta[name], new_m[name], new_v[name] = (t.reshape(w[name].shape) for t in (d2, m2, v2))
    shp = {nme: w[nme].shape for nme in _SMALL}
    gp = _pack_small(g_small, _SMALL)
    dp, mp, vp = _adamw(_pack_small(w, _SMALL), gp, _pack_small(m, _SMALL), _pack_small(v, _SMALL), "adamw_small")
    for dst, src in ((grads, gp), (delta, dp), (new_m, mp), (new_v, vp)):
        dst.update(_unpack_small(src, shp, _SMALL))

    return (loss, grad_x[None], *[grads[nme] for nme in _WEIGHT_ORDER], *[delta[nme] for nme in _WEIGHT_ORDER],
            *[new_m[nme] for nme in _WEIGHT_ORDER], *[new_v[nme] for nme in _WEIGHT_ORDER])
```

```python
import functools
import math

import jax
import jax.numpy as jnp
from jax import lax
from jax.experimental import pallas as pl
from jax.experimental.pallas import tpu as pltpu

F32 = jnp.float32
BF16 = jnp.bfloat16
MESH = pl.DeviceIdType.MESH

D_MODEL = 1024
SSD_HEADS = 16
HEAD_DIM = 64
D_STATE = 128
CHUNK = 128
D_CONV = 5
D_XBC = D_MODEL + 2 * D_STATE
D_XD = 1408
N_POOL = 4
POOL_DIM = 256
POOL_WINDOWS = (2, 4, 8, 16)
GRID_W = 64
D_FF = 2816
D_IN_PROJ = 3360
LN_EPS = 1e-5
ALPHA = 2.0 ** 0.25
POOL_TB = 512
MM_ROWS = 1024

ADAM_LR = 0.001
ADAM_B1 = 0.9
ADAM_B2 = 0.999
ADAM_EPS = 1e-08
ADAM_WD = 0.01
ADAM_STEP = 10

VMEM_LIMIT = 56 * 1024 * 1024


def _cp(sem=None):
    return pltpu.CompilerParams(dimension_semantics=sem, vmem_limit_bytes=VMEM_LIMIT)


def _sigmoid(x):
    return 1.0 / (1.0 + jnp.exp(-x))


def _silu(x):
    return x * _sigmoid(x)


def _dsilu(x):
    s = _sigmoid(x)
    return s * (1.0 + x * (1.0 - s))


def _softplus(x):
    t = jnp.exp(-jnp.abs(x))
    u = 1.0 + t
    log1p = jnp.where(u == 1.0, t, jnp.log(u) * t / (u - 1.0 + (u == 1.0)))
    return jnp.maximum(x, 0.0) + log1p


def _split(x, n):
    parts, r = [], x
    for _ in range(n):
        p = r.astype(BF16)
        parts.append(p)
        r = r - p.astype(F32)
    return parts


def _dot(a, b):
    return jnp.dot(a, b, preferred_element_type=F32)


def _dot_nt(a, b):
    return lax.dot_general(a, b, (((1,), (1,)), ((), ())), preferred_element_type=F32)


def _dot_tn(a, b):
    return lax.dot_general(a, b, (((0,), (0,)), ((), ())), preferred_element_type=F32)


def _dot_sel_l(sel_bf, x, n=3):
    out = None
    for p in _split(x, n):
        t = _dot(sel_bf, p)
        out = t if out is None else out + t
    return out


def _dot_sel_r(x, sel_bf, n=3):
    out = None
    for p in _split(x, n):
        t = _dot(p, sel_bf)
        out = t if out is None else out + t
    return out


ROW_SUB = 16


def _row_tiles(tb):
    assert tb % ROW_SUB == 0
    return [slice(s * ROW_SUB, (s + 1) * ROW_SUB) for s in range(tb // ROW_SUB)]


def _fold8(v):
    out = v[0:8, :]
    for r in range(8, v.shape[0], 8):
        out = out + v[r:r + 8, :]
    return out


def _row_block(n, cap=256, mult=8):
    best = None
    for t in range(mult, min(n, cap) + 1, mult):
        if n % t == 0:
            best = t
    return best if best is not None else n


def _vec(v):
    return v.reshape(1, -1).astype(F32)


MOD_ROWS = 16
MOD_TN = 512


def _mods_fwd(c16, w_bf, b):
    r, d = c16.shape
    n = w_bf.shape[1]

    def body(c_ref, w_ref, b_ref, o_ref):
        s = _silu(c_ref[...]).astype(BF16)
        o_ref[...] = _dot(s, w_ref[...]) + b_ref[...]

    return pl.pallas_call(
        body, name="mods_fwd", grid=(n // MOD_TN,),
        in_specs=[pl.BlockSpec((r, d), lambda j: (0, 0)),
                  pl.BlockSpec((d, MOD_TN), lambda j: (0, j)),
                  pl.BlockSpec((1, MOD_TN), lambda j: (0, j))],
        out_specs=pl.BlockSpec((r, MOD_TN), lambda j: (0, j)),
        out_shape=jax.ShapeDtypeStruct((r, n), F32),
        compiler_params=_cp(("arbitrary",)),
    )(c16, w_bf, b)


def _mods_bwd_w(ct16, dm16):
    d = ct16.shape[0]
    n = dm16.shape[1]

    def body(ct_ref, dm_ref, dw_ref):
        s = _silu(ct_ref[...])
        dm = dm_ref[...]
        acc = s[:, 0:1] * dm[0:1, :]
        for r in range(1, 9):
            acc = acc + s[:, r:r + 1] * dm[r:r + 1, :]
        dw_ref[...] = acc

    return pl.pallas_call(
        body, name="mods_bwd_w", grid=(n // MOD_TN,),
        in_specs=[pl.BlockSpec((d, MOD_ROWS), lambda j: (0, 0)),
                  pl.BlockSpec((MOD_ROWS, MOD_TN), lambda j: (0, j))],
        out_specs=pl.BlockSpec((d, MOD_TN), lambda j: (0, j)),
        out_shape=jax.ShapeDtypeStruct((d, n), F32),
        compiler_params=_cp(("arbitrary",)),
    )(ct16, dm16)


def _mods_bwd_c(dm16, w_bf, c16):
    d = c16.shape[1]
    n = dm16.shape[1]
    nk = n // MOD_TN

    def body(dm_ref, w_ref, c_ref, o_ref):
        k = pl.program_id(0)

        @pl.when(k == 0)
        def _():
            o_ref[...] = jnp.zeros_like(o_ref)

        o_ref[...] += _dot_nt(dm_ref[...].astype(BF16), w_ref[...])

        @pl.when(k == nk - 1)
        def _():
            o_ref[...] = o_ref[...] * (0.5 * _dsilu(c_ref[...]))

    return pl.pallas_call(
        body, name="mods_bwd_c", grid=(nk,),
        in_specs=[pl.BlockSpec((MOD_ROWS, MOD_TN), lambda k: (0, k)),
                  pl.BlockSpec((d, MOD_TN), lambda k: (0, k)),
                  pl.BlockSpec((MOD_ROWS, d), lambda k: (0, 0))],
        out_specs=pl.BlockSpec((MOD_ROWS, d), lambda k: (0, 0)),
        out_shape=jax.ShapeDtypeStruct((MOD_ROWS, d), F32),
        compiler_params=_cp(("arbitrary",)),
    )(dm16, w_bf, c16)


def _mods_bwd_b(dm16):
    n = dm16.shape[1]

    def body(dm_ref, o_ref):
        dm = dm_ref[...]
        acc = dm[0:1, :]
        for r in range(1, 9):
            acc = acc + dm[r:r + 1, :]
        o_ref[...] = jnp.broadcast_to(acc, (8, MOD_TN))

    return pl.pallas_call(
        body, name="mods_bwd_b", grid=(n // MOD_TN,),
        in_specs=[pl.BlockSpec((MOD_ROWS, MOD_TN), lambda j: (0, j))],
        out_specs=pl.BlockSpec((8, MOD_TN), lambda j: (0, j)),
        out_shape=jax.ShapeDtypeStruct((8, n), F32),
        compiler_params=_cp(("arbitrary",)),
    )(dm16)


def _ln_stats(x):
    mu = jnp.mean(x, axis=-1, keepdims=True)
    xc = x - mu
    var = jnp.mean(xc * xc, axis=-1, keepdims=True)
    rstd = lax.rsqrt(var + LN_EPS)
    return xc * rstd, rstd


def _ln_bwd(dxhat, xhat, rstd):
    m1 = jnp.mean(dxhat, axis=-1, keepdims=True)
    m2 = jnp.mean(dxhat * xhat, axis=-1, keepdims=True)
    return rstd * (dxhat - m1 - xhat * m2)


def _row_spec(tb, d):
    return pl.BlockSpec((tb, d), lambda i: (i, 0))


def _par_spec(d):
    return pl.BlockSpec((1, d), lambda i: (0, 0))


def _acc_spec(d):
    return pl.BlockSpec((8, d), lambda i: (0, 0))


def _ln_mod(x, g, b, sh, sc, tb, name):
    n, d = x.shape

    def body(x_ref, g_ref, b_ref, sh_ref, sc_ref, x0_ref, h_ref):
        g, b, sh, sc1 = g_ref[...], b_ref[...], sh_ref[...], 1.0 + sc_ref[...]
        for r in _row_tiles(min(tb, n)):
            xhat, _ = _ln_stats(x_ref[r, :])
            x0 = xhat * g + b
            x0_ref[r, :] = x0
            h_ref[r, :] = (x0 * sc1 + sh).astype(BF16)

    return pl.pallas_call(
        body, name=name, grid=(n // tb,),
        in_specs=[_row_spec(tb, d)] + [_par_spec(d)] * 4,
        out_specs=[_row_spec(tb, d), _row_spec(tb, d)],
        out_shape=[jax.ShapeDtypeStruct((n, d), F32), jax.ShapeDtypeStruct((n, d), BF16)],
        compiler_params=_cp(("parallel",)),
    )(x, g, b, sh, sc)


def _res_ln(xres, mix, gate, g, b, sh, sc, tb):
    n, d = xres.shape

    def body(xr_ref, mix_ref, gate_ref, g_ref, b_ref, sh_ref, sc_ref, x1_ref, h_ref):
        gate_v, g, b, sh, sc1 = gate_ref[...], g_ref[...], b_ref[...], sh_ref[...], 1.0 + sc_ref[...]
        for r in _row_tiles(tb):
            xhat, _ = _ln_stats(ALPHA * xr_ref[r, :] + gate_v * mix_ref[r, :].astype(F32))
            x1 = xhat * g + b
            x1_ref[r, :] = x1
            h_ref[r, :] = (x1 * sc1 + sh).astype(BF16)

    return pl.pallas_call(
        body, name="res_ln1", grid=(n // tb,),
        in_specs=[_row_spec(tb, d)] * 2 + [_par_spec(d)] * 5,
        out_specs=[_row_spec(tb, d), _row_spec(tb, d)],
        out_shape=[jax.ShapeDtypeStruct((n, d), F32), jax.ShapeDtypeStruct((n, d), BF16)],
        compiler_params=_cp(("parallel",)),
    )(xres, mix, gate, g, b, sh, sc)


def _final_ln_loss(x1, ffn, gate, g, b, target, tb):
    n, d = x1.shape

    def body(x1_ref, ffn_ref, gate_ref, g_ref, b_ref, t_ref, dffn_ref, dr_ref, acc_ref):
        i = pl.program_id(0)

        @pl.when(i == 0)
        def _():
            acc_ref[...] = jnp.zeros_like(acc_ref)

        gate_v, g, b = gate_ref[...], g_ref[...], b_ref[...]
        parts = [jnp.zeros((8, d), F32)] * 4
        for r in _row_tiles(tb):
            ffn = ffn_ref[r, :].astype(F32)
            xhat, rstd = _ln_stats(ALPHA * x1_ref[r, :] + gate_v * ffn)
            err = xhat * g + b - t_ref[r, :]
            dx2 = err * (1.0 / d)
            dr = _ln_bwd(dx2 * g, xhat, rstd)
            dr_ref[r, :] = dr
            dffn_ref[r, :] = (gate_v * dr).astype(BF16)
            terms = (dx2 * xhat, dx2, dr * ffn, err * err)
            parts = [p + _fold8(t) for p, t in zip(parts, terms)]
        for j, p in enumerate(parts):
            acc_ref[j:j + 1, :] += jnp.sum(p, axis=0, keepdims=True)

    return pl.pallas_call(
        body, name="final_ln_loss", grid=(n // tb,),
        in_specs=[_row_spec(tb, d)] * 2 + [_par_spec(d)] * 3 + [_row_spec(tb, d)],
        out_specs=[_row_spec(tb, d), _row_spec(tb, d), _acc_spec(d)],
        out_shape=[jax.ShapeDtypeStruct((n, d), BF16), jax.ShapeDtypeStruct((n, d), F32),
                   jax.ShapeDtypeStruct((8, d), F32)],
        compiler_params=_cp(("arbitrary",)),
    )(x1, ffn, gate, g, b, target)


def _bwd_ln1(dr2, dh2, x1, x0, mix, gate, sc2, g, tb):
    n, d = x1.shape

    def body(dr2_ref, dh2_ref, x1_ref, x0_ref, mix_ref, gate_ref, sc_ref, g_ref, dmix_ref, dr1_ref, acc_ref):
        i = pl.program_id(0)

        @pl.when(i == 0)
        def _():
            acc_ref[...] = jnp.zeros_like(acc_ref)

        gate_v, g, sc1 = gate_ref[...], g_ref[...], 1.0 + sc_ref[...]
        parts = [jnp.zeros((8, d), F32)] * 5
        for r in _row_tiles(tb):
            dh2 = dh2_ref[r, :].astype(F32)
            mix = mix_ref[r, :].astype(F32)
            dx1 = ALPHA * dr2_ref[r, :] + dh2 * sc1
            xhat, rstd = _ln_stats(ALPHA * x0_ref[r, :] + gate_v * mix)
            dr1 = _ln_bwd(dx1 * g, xhat, rstd)
            dr1_ref[r, :] = dr1
            dmix_ref[r, :] = (gate_v * dr1).astype(BF16)
            terms = (dh2 * x1_ref[r, :], dh2, dx1 * xhat, dx1, dr1 * mix)
            parts = [p + _fold8(t) for p, t in zip(parts, terms)]
        for j, p in enumerate(parts):
            acc_ref[j:j + 1, :] += jnp.sum(p, axis=0, keepdims=True)

    return pl.pallas_call(
        body, name="bwd_ln1", grid=(n // tb,),
        in_specs=[_row_spec(tb, d)] * 5 + [_par_spec(d)] * 3,
        out_specs=[_row_spec(tb, d), _row_spec(tb, d), _acc_spec(d)],
        out_shape=[jax.ShapeDtypeStruct((n, d), BF16), jax.ShapeDtypeStruct((n, d), F32),
                   jax.ShapeDtypeStruct((8, d), F32)],
        compiler_params=_cp(("arbitrary",)),
    )(dr2, dh2, x1, x0, mix, gate, sc2, g)


def _bwd_ln0(dres, dh, x, g, b, sc, tb, name):
    n, d = x.shape
    has_res = dres is not None

    def body(*refs):
        if has_res:
            dres_ref, dh_ref, x_ref, g_ref, b_ref, sc_ref, dx_ref, acc_ref = refs
        else:
            dh_ref, x_ref, g_ref, b_ref, sc_ref, dx_ref, acc_ref = refs
        i = pl.program_id(0)

        @pl.when(i == 0)
        def _():
            acc_ref[...] = jnp.zeros_like(acc_ref)

        g, b, sc1 = g_ref[...], b_ref[...], 1.0 + sc_ref[...]
        parts = [jnp.zeros((8, d), F32)] * 4
        for r in _row_tiles(tb):
            dh = dh_ref[r, :].astype(F32)
            xhat, rstd = _ln_stats(x_ref[r, :])
            x0 = xhat * g + b
            dx0 = dh * sc1
            if has_res:
                dx0 = dx0 + ALPHA * dres_ref[r, :]
            dx_ref[r, :] = _ln_bwd(dx0 * g, xhat, rstd)
            terms = (dh * x0, dh, dx0 * xhat, dx0)
            parts = [p + _fold8(t) for p, t in zip(parts, terms)]
        for j, p in enumerate(parts):
            acc_ref[j:j + 1, :] += jnp.sum(p, axis=0, keepdims=True)

    ins = ([dres] if has_res else []) + [dh, x, g, b, sc]
    return pl.pallas_call(
        body, name=name, grid=(n // tb,),
        in_specs=[_row_spec(tb, d)] * (3 if has_res else 2) + [_par_spec(d)] * 3,
        out_specs=[_row_spec(tb, d), _acc_spec(d)],
        out_shape=[jax.ShapeDtypeStruct((n, d), F32), jax.ShapeDtypeStruct((8, d), F32)],
        compiler_params=_cp(("arbitrary",)),
    )(*ins)


def _matmul_nn(pairs, out_dtype, tm, tn, name):
    m = pairs[0][0].shape[0]
    n = pairs[0][1].shape[1]
    tm = min(tm, m)
    tn = min(tn, n)
    npair = len(pairs)

    def body(*refs):
        o_ref = refs[-1]
        acc = None
        for p in range(npair):
            t = _dot(refs[2 * p][...].astype(BF16), refs[2 * p + 1][...])
            acc = t if acc is None else acc + t
        o_ref[...] = acc.astype(out_dtype)

    in_specs, args = [], []
    for a, b in pairs:
        k = a.shape[1]
        in_specs += [pl.BlockSpec((tm, k), lambda i, j: (i, 0)), pl.BlockSpec((k, tn), lambda i, j: (0, j))]
        args += [a, b]
    return pl.pallas_call(
        body, name=name, grid=(m // tm, n // tn),
        in_specs=in_specs,
        out_specs=pl.BlockSpec((tm, tn), lambda i, j: (i, j)),
        out_shape=jax.ShapeDtypeStruct((m, n), out_dtype),
        compiler_params=_cp(("parallel", "arbitrary")),
    )(*args)


def _matmul_tn(a, g, tm, tn, name):
    m, k = a.shape
    n = g.shape[1]
    tm = min(tm, m)
    tn = min(tn, n)

    def body(a_ref, g_ref, o_ref):
        i = pl.program_id(1)

        @pl.when(i == 0)
        def _():
            o_ref[...] = jnp.zeros_like(o_ref)

        o_ref[...] += _dot_tn(a_ref[...].astype(BF16), g_ref[...].astype(BF16))

    return pl.pallas_call(
        body, name=name, grid=(n // tn, m // tm),
        in_specs=[pl.BlockSpec((tm, k), lambda j, i: (i, 0)), pl.BlockSpec((tm, tn), lambda j, i: (i, j))],
        out_specs=pl.BlockSpec((k, tn), lambda j, i: (0, j)),
        out_shape=jax.ShapeDtypeStruct((k, n), F32),
        compiler_params=_cp(("parallel", "arbitrary")),
    )(a, g)


def _matmul_nn_multi(a, bs, out_dtypes, tm, name):
    m, k = a.shape
    tm = min(tm, m)
    nb = len(bs)

    def body(a_ref, *refs):
        av = a_ref[...].astype(BF16)
        for j in range(nb):
            refs[nb + j][...] = _dot(av, refs[j][...]).astype(out_dtypes[j])

    return pl.pallas_call(
        body, name=name, grid=(m // tm,),
        in_specs=[pl.BlockSpec((tm, k), lambda i: (i, 0))] + [pl.BlockSpec(b.shape, lambda i: (0, 0)) for b in bs],
        out_specs=[pl.BlockSpec((tm, b.shape[1]), lambda i: (i, 0)) for b in bs],
        out_shape=[jax.ShapeDtypeStruct((m, b.shape[1]), dt) for b, dt in zip(bs, out_dtypes)],
        compiler_params=_cp(("parallel",)),
    )(a, *bs)


def _matmul_tn_multi(a, gs, tm, tns, name):
    m, k = a.shape
    tm = min(tm, m)
    ng = len(gs)
    nj = gs[0].shape[1] // tns[0]
    assert all(g.shape[1] // t == nj and g.shape[1] % t == 0 for g, t in zip(gs, tns))

    def body(a_ref, *refs):
        i = pl.program_id(1)

        @pl.when(i == 0)
        def _():
            for j in range(ng):
                refs[ng + j][...] = jnp.zeros_like(refs[ng + j])

        av = a_ref[...].astype(BF16)
        for j in range(ng):
            refs[ng + j][...] += _dot_tn(av, refs[j][...].astype(BF16))

    return pl.pallas_call(
        body, name=name, grid=(nj, m // tm),
        in_specs=[pl.BlockSpec((tm, k), lambda j, i: (i, 0))] +
                 [pl.BlockSpec((tm, t), lambda j, i: (i, j)) for t in tns],
        out_specs=[pl.BlockSpec((k, t), lambda j, i: (0, j)) for t in tns],
        out_shape=[jax.ShapeDtypeStruct((k, g.shape[1]), F32) for g in gs],
        compiler_params=_cp(("parallel", "arbitrary")),
    )(a, *gs)


def _swiglu_fwd(h, wg, wu, tm, tn):
    m, k = h.shape
    n = wg.shape[1]
    tm = min(tm, m)

    def body(h_ref, wg_ref, wu_ref, gate_ref, up_ref, hmid_ref):
        hv = h_ref[...]
        gate = _dot(hv, wg_ref[...])
        up = _dot(hv, wu_ref[...])
        gate_ref[...] = gate.astype(BF16)
        up_ref[...] = up.astype(BF16)
        hmid_ref[...] = (_silu(gate) * up).astype(BF16)

    blk = pl.BlockSpec((tm, tn), lambda i, j: (i, j))
    wspec = pl.BlockSpec((k, tn), lambda i, j: (0, j))
    return pl.pallas_call(
        body, name="swiglu_fwd", grid=(m // tm, n // tn),
        in_specs=[pl.BlockSpec((tm, k), lambda i, j: (i, 0)), wspec, wspec],
        out_specs=[blk, blk, blk],
        out_shape=[jax.ShapeDtypeStruct((m, n), BF16), jax.ShapeDtypeStruct((m, n), BF16),
                   jax.ShapeDtypeStruct((m, n), BF16)],
        compiler_params=_cp(("parallel", "arbitrary")),
    )(h, wg, wu)


def _swiglu_bwd(dffn, wdt, gate, up, tm, tn):
    m, k = dffn.shape
    n = wdt.shape[1]
    tm = min(tm, m)

    def body(d_ref, w_ref, gate_ref, up_ref, dg_ref, du_ref):
        dh = _dot(d_ref[...], w_ref[...])
        gate = gate_ref[...].astype(F32)
        dg_ref[...] = (dh * up_ref[...].astype(F32) * _dsilu(gate)).astype(BF16)
        du_ref[...] = (dh * _silu(gate)).astype(BF16)

    blk = pl.BlockSpec((tm, tn), lambda i, j: (i, j))
    return pl.pallas_call(
        body, name="swiglu_bwd", grid=(m // tm, n // tn),
        in_specs=[pl.BlockSpec((tm, k), lambda i, j: (i, 0)), pl.BlockSpec((k, tn), lambda i, j: (0, j)), blk, blk],
        out_specs=[blk, blk],
        out_shape=[jax.ShapeDtypeStruct((m, n), BF16), jax.ShapeDtypeStruct((m, n), BF16)],
        compiler_params=_cp(("parallel", "arbitrary")),
    )(dffn, wdt, gate, up)


def _halo_specs(tb, width, nrows):
    r8 = tb // 8
    last = nrows // 8 - 1
    prev = pl.BlockSpec((8, width), lambda i: (jnp.maximum(i * r8 - 1, 0), 0))
    nxt = pl.BlockSpec((8, width), lambda i: (jnp.minimum((i + 1) * r8, last), 0))
    return prev, nxt


CONV_SUB = 32


def _halo_scratch():
    return [pltpu.VMEM((CONV_SUB + 16, D_XBC), F32), pltpu.VMEM((CONV_SUB + 16, D_XBC), F32)]


def _shifted_rows(prev_ref, cur_ref, next_ref, top, bot, tb, i, nb):
    sub = CONV_SUB
    nsub = tb // sub
    assert nsub >= 2
    top[0:8, :] = prev_ref[...] * (i > 0).astype(F32)
    top[8:sub + 16, :] = cur_ref[0:sub + 8, :]
    bot[0:sub + 8, :] = cur_ref[tb - sub - 8:tb, :]
    bot[sub + 8:sub + 16, :] = next_ref[...] * (i < nb - 1).astype(F32)

    def rows(s, o):
        if s == 0:
            return top[8 + o:8 + o + sub, :]
        if s == nsub - 1:
            return bot[8 + o:8 + o + sub, :]
        return cur_ref[s * sub + o:(s + 1) * sub + o, :]

    return rows


def _conv_fwd(xd, w8, b, tb, name):
    n = xd.shape[0]
    tb = min(tb, n)
    nb = n // tb
    prev, nxt = _halo_specs(tb, D_XBC, n)

    def body(p_ref, c_ref, n_ref, w_ref, b_ref, u_ref, top, bot):
        i = pl.program_id(0)
        rows = _shifted_rows(p_ref, c_ref, n_ref, top, bot, tb, i, nb)
        w = [w_ref[k:k + 1, :] for k in range(D_CONV)]
        bias = jnp.broadcast_to(b_ref[...], (CONV_SUB, D_XBC))
        for s in range(tb // CONV_SUB):
            acc = bias
            for k in range(D_CONV):
                acc = acc + w[k] * rows(s, k - 2)
            u_ref[s * CONV_SUB:(s + 1) * CONV_SUB, :] = acc.astype(BF16)

    return pl.pallas_call(
        body, name=name, grid=(nb,),
        in_specs=[prev, pl.BlockSpec((tb, D_XBC), lambda i: (i, 0)), nxt,
                  pl.BlockSpec((8, D_XBC), lambda i: (0, 0)), _par_spec(D_XBC)],
        out_specs=_row_spec(tb, D_XBC),
        out_shape=jax.ShapeDtypeStruct((n, D_XBC), BF16),
        scratch_shapes=_halo_scratch(),
        compiler_params=_cp(("parallel",)),
    )(xd, xd, xd, w8, b)


def _conv_bwd_a(dxs, dy, dskip_e, dbc, u, tb, name):
    n = u.shape[0]
    tb = min(tb, n)

    def body(dxs_ref, dy_ref, sk_ref, dbc_ref, u_ref, du_ref, acc_ref):
        i = pl.program_id(0)

        @pl.when(i == 0)
        def _():
            acc_ref[...] = jnp.zeros_like(acc_ref)

        sk = sk_ref[...]
        part = jnp.zeros((8, D_XBC), F32)
        for r in _row_tiles(tb):
            gx = dxs_ref[0, r, :].astype(F32) + dxs_ref[1, r, :].astype(F32) + dy_ref[r, :].astype(F32) * sk
            gbc = dbc_ref[0, r, :] + dbc_ref[1, r, :]
            du = jnp.concatenate([gx, gbc], axis=1) * _dsilu(u_ref[r, :].astype(F32))
            du_ref[r, :] = du
            part = part + _fold8(du)
        acc_ref[0:1, :] += jnp.sum(part, axis=0, keepdims=True)

    return pl.pallas_call(
        body, name=name, grid=(n // tb,),
        in_specs=[pl.BlockSpec((2, tb, D_MODEL), lambda i: (0, i, 0)), _row_spec(tb, D_MODEL), _par_spec(D_MODEL),
                  pl.BlockSpec((2, tb, 2 * D_STATE), lambda i: (0, i, 0)), _row_spec(tb, D_XBC)],
        out_specs=[_row_spec(tb, D_XBC), _acc_spec(D_XBC)],
        out_shape=[jax.ShapeDtypeStruct((n, D_XBC), F32), jax.ShapeDtypeStruct((8, D_XBC), F32)],
        compiler_params=_cp(("arbitrary",)),
    )(dxs, dy, dskip_e, dbc, u)


def _conv_bwd_b(du, xd, ddt, w8, tb, name):
    n = du.shape[0]
    tb = min(tb, n)
    nb = n // tb
    prev, nxt = _halo_specs(tb, D_XBC, n)

    def body(dp_ref, dc_ref, dn_ref, xp_ref, xc_ref, xn_ref, ddt_ref, w_ref, dxd_ref, acc_ref, dtop, dbot, xtop, xbot):
        i = pl.program_id(0)

        @pl.when(i == 0)
        def _():
            acc_ref[...] = jnp.zeros_like(acc_ref)

        sub = CONV_SUB
        nsub = tb // sub
        du_rows = _shifted_rows(dp_ref, dc_ref, dn_ref, dtop, dbot, tb, i, nb)
        x_rows = _shifted_rows(xp_ref, xc_ref, xn_ref, xtop, xbot, tb, i, nb)
        w = [w_ref[k:k + 1, :] for k in range(D_CONV)]
        for s in range(nsub):
            acc = w[0] * du_rows(s, 2)
            for k in range(1, D_CONV):
                acc = acc + w[k] * du_rows(s, 2 - k)
            dxd_ref[s * sub:(s + 1) * sub, 0:D_XBC] = acc.astype(BF16)
        for k in range(D_CONV):
            part = jnp.zeros((8, D_XBC), F32)
            for s in range(nsub):
                prod = dc_ref[s * sub:(s + 1) * sub, :] * x_rows(s, k - 2)
                for r in range(0, sub, 8):
                    part = part + prod[r:r + 8, :]
            acc_ref[k:k + 1, :] += jnp.sum(part, axis=0, keepdims=True)
        ddt = ddt_ref[0] + pltpu.roll(ddt_ref[1], SSD_HEADS, 1)
        dxd_ref[:, D_XBC:D_XD] = ddt.astype(BF16)

    cur = pl.BlockSpec((tb, D_XBC), lambda i: (i, 0))
    return pl.pallas_call(
        body, name=name, grid=(nb,),
        in_specs=[prev, cur, nxt, prev, cur, nxt,
                  pl.BlockSpec((2, tb, 128), lambda i: (0, i, 0)), pl.BlockSpec((8, D_XBC), lambda i: (0, 0))],
        out_specs=[_row_spec(tb, D_XD), _acc_spec(D_XBC)],
        out_shape=[jax.ShapeDtypeStruct((n, D_XD), BF16), jax.ShapeDtypeStruct((8, D_XBC), F32)],
        scratch_shapes=_halo_scratch() + _halo_scratch(),
        compiler_params=_cp(("arbitrary",)),
    )(du, du, du, xd, xd, xd, ddt, w8)


def _ssd_chunk_index(nc, reverse):
    def idx(d, k):
        kk = (nc - 1 - k) if reverse else k
        return kk + d * (nc - 1 - 2 * kk)
    return idx


def _ssd_prologue(d, u_ref, xd_ref, bias_ref, a_ref, r_ref):
    q = CHUNK
    xbc = _silu(u_ref[...].astype(F32))
    xs = xbc[:, 0:D_MODEL]
    bm = xbc[:, D_MODEL:D_MODEL + D_STATE]
    cm = xbc[:, D_MODEL + D_STATE:D_XBC]
    row = lax.broadcasted_iota(jnp.int32, (q, q), 0)
    col = lax.broadcasted_iota(jnp.int32, (q, q), 1)
    sgn = 1 - 2 * d
    mask = ((row - col) * sgn) >= 0
    mask_t = ((row - col) * sgn) <= 0
    xdv = xd_ref[...]
    dtraw = jnp.where(d == 0, xdv, pltpu.roll(xdv, 128 - SSD_HEADS, 1)) + bias_ref[...]
    head_lane = col < SSD_HEADS
    dt = jnp.where(head_lane, _softplus(dtraw), 0.0)
    a = a_ref[...]
    tri = jnp.where(mask, 1.0, 0.0).astype(BF16)
    acum = _dot_sel_l(tri, dt * a)
    rexp = r_ref[...]
    alast = jnp.where(d == 0, acum[q - 1:q, :], acum[0:1, :])
    e16 = jnp.exp(acum)
    dend16 = jnp.exp(alast - acum)
    wend16 = dend16 * dt
    e = _dot_sel_r(e16, rexp, n=2)
    wend_e = _dot_sel_r(wend16, rexp, n=2)
    elast_e = jnp.where(d == 0, e[q - 1:q, :], e[0:1, :])
    g = _dot_nt(cm.astype(BF16), bm.astype(BF16))
    return dict(xs=xs, bm=bm, cm=cm, mask=mask, mask_t=mask_t, dtraw=dtraw, head_lane=head_lane, dt=dt, a=a,
                acum=acum, acum_t=acum.T, dt_t=dt.T, e16=e16, dend16=dend16, wend16=wend16, e=e, wend_e=wend_e,
                elast_e=elast_e, g=g, col=col, row=row)


def _ssd_head_mats(p, h):
    seg = p["acum"][:, h:h + 1] - p["acum_t"][h:h + 1, :]
    lm = jnp.exp(jnp.where(p["mask"], seg, -jnp.inf))
    gl = p["g"] * lm
    s = gl * p["dt_t"][h:h + 1, :]
    return lm, gl, s


def _ssd_fwd(u, xd, bias2, a2, rexp, h0, name, gather=()):
    n = u.shape[0]
    nc = n // CHUNK
    q = CHUNK
    cidx = _ssd_chunk_index(nc, reverse=False)
    ng = len(gather)

    def body(u_ref, xd_ref, bias_ref, a_ref, r_ref, h0_ref, *rest):
        g_ins, (y_ref, hp_ref, hf_ref), rest = rest[:ng], rest[ng:ng + 3], rest[ng + 3:]
        g_outs, st, sems = rest[:ng], rest[ng], rest[ng + 1:]
        d = pl.program_id(0)
        k = pl.program_id(1)
        if ng:
            g_start, g_finish = _gather_steps(g_ins, g_outs, *sems)
            pl.when((d == 0) & (k == 0))(g_start)

        @pl.when(k == 0)
        def _():
            st[...] = h0_ref[...]

        p = _ssd_prologue(d, u_ref, xd_ref, bias_ref, a_ref, r_ref)
        stv = st[...]
        st_bf = stv.astype(BF16)
        hp_ref[...] = st_bf
        xs = p["xs"]
        lane128 = p["col"]
        y_off = _dot(p["cm"].astype(BF16), st_bf) * p["e"]
        for pb in range(SSD_HEADS // 2):
            _, _, s0 = _ssd_head_mats(p, 2 * pb)
            _, _, s1 = _ssd_head_mats(p, 2 * pb + 1)
            xp = xs[:, pb * 128:(pb + 1) * 128]
            rhs = jnp.concatenate([jnp.where(lane128 < HEAD_DIM, xp, 0.0), jnp.where(lane128 >= HEAD_DIM, xp, 0.0)],
                                  axis=0).astype(BF16)
            lhs = jnp.concatenate([s0, s1], axis=1).astype(BF16)
            y_ref[:, pb * 128:(pb + 1) * 128] = (_dot(lhs, rhs) + y_off[:, pb * 128:(pb + 1) * 128]).astype(BF16)
        xw = (xs * p["wend_e"]).astype(BF16)
        new = stv * p["elast_e"] + _dot(p["bm"].T.astype(BF16), xw)
        st[...] = new
        hf_ref[...] = new
        if ng:
            pl.when((d == 1) & (k == nc - 1))(g_finish)

    nsem = _GATHER_SEMS * ng
    return pl.pallas_call(
        body, name=name, grid=(2, nc),
        in_specs=[pl.BlockSpec((q, D_XBC), lambda d, k: (cidx(d, k), 0)),
                  pl.BlockSpec((q, 128), lambda d, k: (cidx(d, k), D_XBC // 128)),
                  pl.BlockSpec((None, 1, 128), lambda d, k: (d, 0, 0)),
                  pl.BlockSpec((None, 1, 128), lambda d, k: (d, 0, 0)),
                  pl.BlockSpec((128, D_MODEL), lambda d, k: (0, 0)),
                  pl.BlockSpec((None, D_STATE, D_MODEL), lambda d, k: (d, 0, 0))] + [_ANY] * ng,
        out_specs=[pl.BlockSpec((None, q, D_MODEL), lambda d, k: (d, cidx(d, k), 0)),
                   pl.BlockSpec((None, None, D_STATE, D_MODEL), lambda d, k: (d, cidx(d, k), 0, 0)),
                   pl.BlockSpec((None, D_STATE, D_MODEL), lambda d, k: (d, 0, 0))] + [_ANY] * ng,
        out_shape=[jax.ShapeDtypeStruct((2, n, D_MODEL), BF16),
                   jax.ShapeDtypeStruct((2, nc, D_STATE, D_MODEL), BF16),
                   jax.ShapeDtypeStruct((2, D_STATE, D_MODEL), F32)] +
                  [jax.ShapeDtypeStruct((4,) + t.shape, t.dtype) for t in gather],
        scratch_shapes=[pltpu.VMEM((D_STATE, D_MODEL), F32)] +
                       ([pltpu.SemaphoreType.DMA((nsem,)), pltpu.SemaphoreType.DMA((nsem,))] if ng else []),
        compiler_params=_cp(("arbitrary", "arbitrary")),
    )(u, xd, bias2, a2, rexp, h0, *gather)


def _ssd_bwd(u, xd, bias2, a2, rexp, rexp_t, dy, hprev, lam0, name, exchange=()):
    n = u.shape[0]
    nc = n // CHUNK
    q = CHUNK
    cidx = _ssd_chunk_index(nc, reverse=True)
    ne = len(exchange)

    def body(u_ref, xd_ref, bias_ref, a_ref, r_ref, rt_ref, dy_ref, hp_ref, lam0_ref, *rest):
        e_ins, (dxs_ref, dbc_ref, ddt_ref, acc_ref, lamo_ref), rest = rest[:ne], rest[ne:ne + 5], rest[ne + 5:]
        e_outs, lam, sems = rest[:ne], rest[ne], rest[ne + 1:]
        d = pl.program_id(0)
        k = pl.program_id(1)
        if ne:
            e_start, e_finish = _exchange_steps(e_ins, e_outs, *sems)
            pl.when((d == 0) & (k == 0))(e_start)

        @pl.when(k == 0)
        def _():
            lam[...] = lam0_ref[...]
            acc_ref[...] = jnp.zeros_like(acc_ref)

        rexp_t = rt_ref[...]

        def hsum(t):
            return _dot_sel_r(t, rexp_t, n=2)

        p = _ssd_prologue(d, u_ref, xd_ref, bias_ref, a_ref, r_ref)
        xs, bm, cm = p["xs"], p["bm"], p["cm"]
        bm_bf, cm_bf = bm.astype(BF16), cm.astype(BF16)
        lamn = lam[...]
        lamn_bf = lamn.astype(BF16)
        stp = hp_ref[...]
        dyv = dy_ref[...].astype(F32)
        lane128 = p["col"]

        wend_e = p["wend_e"]
        cs = _dot(cm_bf, stp)
        dye_bf = (dyv * p["e"]).astype(BF16)
        dc_off = _dot_nt(dye_bf, stp)
        v = _dot(bm_bf, lamn_bf)
        xw_bf = (xs * wend_e).astype(BF16)
        db_off = _dot_nt(xw_bf, lamn_bf)
        elast_e = p["elast_e"]
        dlast_e = jnp.sum(stp.astype(F32) * lamn, axis=0, keepdims=True) * elast_e
        lam_new = lamn * elast_e + _dot(cm.T.astype(BF16), dye_bf)
        lam[...] = lam_new
        lamo_ref[...] = lam_new

        hs_vx = hsum(v * xs)
        om = p["wend16"] * hs_vx
        x1 = p["e16"] * hsum(dyv * cs) - om
        x2 = p["dend16"] * hs_vx
        x3 = jnp.sum(om, axis=0, keepdims=True) + hsum(jnp.broadcast_to(dlast_e, (8, D_MODEL)))[0:1, :]

        sub16 = lax.broadcasted_iota(jnp.int32, (SSD_HEADS, q), 0)
        rs = jnp.zeros((q, 128), F32)
        cs_m = jnp.zeros((SSD_HEADS, q), F32)
        dt_m = jnp.zeros((SSD_HEADS, q), F32)
        dg = jnp.zeros((q, q), F32)
        for pb in range(SSD_HEADS // 2):
            xp_bf = xs[:, pb * 128:(pb + 1) * 128].astype(BF16)
            dyp = dyv[:, pb * 128:(pb + 1) * 128]
            dxs_pair = None
            for half in range(2):
                h = 2 * pb + half
                sel = (lane128 < HEAD_DIM) if half == 0 else (lane128 >= HEAD_DIM)
                dyh_bf = jnp.where(sel, dyp, 0.0).astype(BF16)
                lm, gl, s = _ssd_head_mats(p, h)
                ds = _dot_nt(dyh_bf, xp_bf)
                t = _dot_tn(s.astype(BF16), dyh_bf)
                dxs_pair = t if dxs_pair is None else dxs_pair + t
                w = ds * s
                rs = rs + jnp.sum(w, axis=1, keepdims=True) * (lane128 == h).astype(F32)
                cs_m = jnp.where(sub16 == h, jnp.sum(w, axis=0, keepdims=True), cs_m)
                dt_m = jnp.where(sub16 == h, jnp.sum(ds * gl, axis=0, keepdims=True), dt_m)
                dg = dg + ds * lm * p["dt_t"][h:h + 1, :]
            sl = slice(pb * 128, (pb + 1) * 128)
            dxs_ref[:, sl] = (dxs_pair + v[:, sl] * wend_e[:, sl]).astype(BF16)

        def to_lanes(m16):
            return jnp.concatenate([m16, jnp.zeros((128 - SSD_HEADS, q), F32)], axis=0).T

        last = jnp.where(d == 0, q - 1, 0)
        dacum = rs - to_lanes(cs_m) + x1 + jnp.where(p["row"] == last, x3[0:1, :], 0.0)
        tri_t = jnp.where(p["mask_t"], 1.0, 0.0).astype(BF16)
        ddta = _dot_sel_l(tri_t, dacum)
        dt = p["dt"]
        a = p["a"]
        ddt = to_lanes(dt_m) + x2 + a * ddta
        ddtraw = jnp.where(p["head_lane"], ddt * _sigmoid(p["dtraw"]), 0.0)
        ddt_ref[...] = ddtraw
        acc_ref[0:1, :] += jnp.sum(ddtraw, axis=0, keepdims=True)
        acc_ref[1:2, :] += jnp.sum(dt * ddta, axis=0, keepdims=True) * a

        dg_bf = dg.astype(BF16)
        dbc_ref[:, 0:D_STATE] = _dot_tn(dg_bf, cm_bf) + db_off
        dbc_ref[:, D_STATE:2 * D_STATE] = _dot(dg_bf, bm_bf) + dc_off
        if ne:
            pl.when((d == 1) & (k == nc - 1))(e_finish)

    cblk = lambda d, k: (cidx(d, k), 0)
    return pl.pallas_call(
        body, name=name, grid=(2, nc),
        in_specs=[pl.BlockSpec((q, D_XBC), cblk),
                  pl.BlockSpec((q, 128), lambda d, k: (cidx(d, k), D_XBC // 128)),
                  pl.BlockSpec((None, 1, 128), lambda d, k: (d, 0, 0)),
                  pl.BlockSpec((None, 1, 128), lambda d, k: (d, 0, 0)),
                  pl.BlockSpec((128, D_MODEL), lambda d, k: (0, 0)),
                  pl.BlockSpec((D_MODEL, 128), lambda d, k: (0, 0)),
                  pl.BlockSpec((q, D_MODEL), cblk),
                  pl.BlockSpec((None, None, D_STATE, D_MODEL), lambda d, k: (d, cidx(d, k), 0, 0)),
                  pl.BlockSpec((None, D_STATE, D_MODEL), lambda d, k: (d, 0, 0))] + [_ANY] * ne,
        out_specs=[pl.BlockSpec((None, q, D_MODEL), lambda d, k: (d, cidx(d, k), 0)),
                   pl.BlockSpec((None, q, 2 * D_STATE), lambda d, k: (d, cidx(d, k), 0)),
                   pl.BlockSpec((None, q, 128), lambda d, k: (d, cidx(d, k), 0)),
                   pl.BlockSpec((None, 8, 128), lambda d, k: (d, 0, 0)),
                   pl.BlockSpec((None, D_STATE, D_MODEL), lambda d, k: (d, 0, 0))] + [_ANY] * ne,
        out_shape=[jax.ShapeDtypeStruct((2, n, D_MODEL), BF16),
                   jax.ShapeDtypeStruct((2, n, 2 * D_STATE), F32),
                   jax.ShapeDtypeStruct((2, n, 128), F32),
                   jax.ShapeDtypeStruct((2, 8, 128), F32),
                   jax.ShapeDtypeStruct((2, D_STATE, D_MODEL), F32)] +
                  [jax.ShapeDtypeStruct(t.shape, t.dtype) for t in exchange],
        scratch_shapes=[pltpu.VMEM((D_STATE, D_MODEL), F32)] +
                       ([pltpu.SemaphoreType.DMA((3 * ne,)), pltpu.SemaphoreType.DMA((3 * ne,)),
                         pltpu.SemaphoreType.DMA((ne,))] if ne else []),
        compiler_params=_cp(("arbitrary", "arbitrary")),
    )(u, xd, bias2, a2, rexp, rexp_t, dy, hprev, lam0, *exchange)


def _merge_fwd(y, u, z, dskip_e, gn, tb):
    n = z.shape[0]

    def body(y_ref, u_ref, z_ref, sk_ref, gn_ref, o_ref):
        sk, gnv = sk_ref[...], gn_ref[...]
        for r in _row_tiles(tb):
            ys = y_ref[0, r, :].astype(F32) + y_ref[1, r, :].astype(F32) + sk * _silu(u_ref[r, :].astype(F32))
            gated = ys * _silu(z_ref[r, :].astype(F32))
            rstd = lax.rsqrt(jnp.mean(gated * gated, axis=-1, keepdims=True) + LN_EPS)
            o_ref[r, :] = (gated * rstd * gnv).astype(BF16)

    return pl.pallas_call(
        body, name="merge_fwd", grid=(n // tb,),
        in_specs=[pl.BlockSpec((2, tb, D_MODEL), lambda i: (0, i, 0)), pl.BlockSpec((tb, D_MODEL), lambda i: (i, 0)),
                  _row_spec(tb, D_MODEL), _par_spec(D_MODEL), _par_spec(D_MODEL)],
        out_specs=_row_spec(tb, D_MODEL),
        out_shape=jax.ShapeDtypeStruct((n, D_MODEL), BF16),
        compiler_params=_cp(("parallel",)),
    )(y, u, z, dskip_e, gn)


def _merge_bwd(dyn, y, u, z, dskip_e, gn, tb):
    n = z.shape[0]

    def body(dyn_ref, y_ref, u_ref, z_ref, sk_ref, gn_ref, dy_ref, dz_ref, acc_ref):
        i = pl.program_id(0)

        @pl.when(i == 0)
        def _():
            acc_ref[...] = jnp.zeros_like(acc_ref)

        sk, gnv = sk_ref[...], gn_ref[...]
        part0 = jnp.zeros((8, D_MODEL), F32)
        part1 = jnp.zeros((8, D_MODEL), F32)
        for s in range(tb // ROW_SUB):
            r = slice(s * ROW_SUB, (s + 1) * ROW_SUB)
            xs = _silu(u_ref[r, :].astype(F32))
            zv = z_ref[r, :].astype(F32)
            sz = _sigmoid(zv)
            ys = y_ref[0, r, :].astype(F32) + y_ref[1, r, :].astype(F32) + sk * xs
            gated = ys * (zv * sz)
            rstd = lax.rsqrt(jnp.mean(gated * gated, axis=-1, keepdims=True) + LN_EPS)
            ghat = gated * rstd
            dyn_v = dyn_ref[r, :].astype(F32)
            t = dyn_v * gnv
            dgated = rstd * (t - ghat * jnp.mean(t * ghat, axis=-1, keepdims=True))
            dys = dgated * (zv * sz)
            dy_ref[r, :] = dys.astype(BF16)
            dz_ref[r, :] = (dgated * ys * (sz * (1.0 + zv * (1.0 - sz)))).astype(BF16)
            part0 = part0 + _fold8(dyn_v * ghat)
            part1 = part1 + _fold8(dys * xs)
        acc_ref[0:1, :] += jnp.sum(part0, axis=0, keepdims=True)
        acc_ref[1:2, :] += jnp.sum(part1, axis=0, keepdims=True)

    return pl.pallas_call(
        body, name="merge_bwd", grid=(n // tb,),
        in_specs=[_row_spec(tb, D_MODEL), pl.BlockSpec((2, tb, D_MODEL), lambda i: (0, i, 0)),
                  pl.BlockSpec((tb, D_MODEL), lambda i: (i, 0)), _row_spec(tb, D_MODEL),
                  _par_spec(D_MODEL), _par_spec(D_MODEL)],
        out_specs=[_row_spec(tb, D_MODEL), _row_spec(tb, D_MODEL), _acc_spec(D_MODEL)],
        out_shape=[jax.ShapeDtypeStruct((n, D_MODEL), BF16), jax.ShapeDtypeStruct((n, D_MODEL), BF16),
                   jax.ShapeDtypeStruct((8, D_MODEL), F32)],
        compiler_params=_cp(("arbitrary",)),
    )(dyn, y, u, z, dskip_e, gn)


def _pool_consts(transpose):
    tb = POOL_TB
    t = jnp.arange(tb)
    s = jnp.arange(3 * tb)
    rl, cl = t // GRID_W, t % GRID_W
    rs_, cs_ = s // GRID_W - tb // GRID_W, s % GRID_W
    s2 = jnp.arange(tb)
    rl2, cl2 = s2 // GRID_W, s2 % GRID_W
    brow, bcol = [], []
    for w in POOL_WINDOWS:
        lo, hi = -(w // 2), w - w // 2
        if transpose:
            lo, hi = -hi + 1, -lo + 1
        dr = rs_[None, :] - rl[:, None]
        before, after = _pool_halo(w, transpose)
        full = ((cs_[None, :] == cl[:, None]) & (dr >= lo) & (dr < hi)).astype(BF16)
        brow.append(full[:, tb - before:2 * tb + after])
        dc = cl2[None, :] - cl[:, None]
        bcol.append(((rl2[None, :] == rl[:, None]) & (dc >= lo) & (dc < hi)).astype(BF16))
    return brow, jnp.stack(bcol)


def _pool_halo(w, transpose):
    lo, hi = -(w // 2), w - w // 2
    if transpose:
        lo, hi = -hi + 1, -lo + 1
    return -lo * GRID_W, (hi - 1) * GRID_W


def _pool_inv(i, g, n):
    assert GRID_W == 64
    t = i * POOL_TB + lax.broadcasted_iota(jnp.int32, (POOL_TB, 1), 0)
    r = lax.shift_right_logical(t, 6)
    col = t & (GRID_W - 1)
    w = POOL_WINDOWS[g]
    lo, hi = -(w // 2), w - w // 2
    cnt_r = jnp.minimum(r + hi, n // GRID_W) - jnp.maximum(r + lo, 0)
    cnt_c = jnp.minimum(col + hi, GRID_W) - jnp.maximum(col + lo, 0)
    return 1.0 / (cnt_r * cnt_c).astype(F32)


def _pool_box(prev_ref, cur_ref, next_ref, brow_refs, bcol_ref, g, i, nb, transpose):
    tb = POOL_TB
    sl = slice(g * POOL_DIM, (g + 1) * POOL_DIM)
    before, after = _pool_halo(POOL_WINDOWS[g], transpose)
    pieces = []
    if before:
        pieces.append((prev_ref[tb - before:tb, sl] * (i > 0).astype(prev_ref.dtype)).astype(BF16))
    pieces.append(cur_ref[:, sl].astype(BF16))
    if after:
        pieces.append((next_ref[0:after, sl] * (i < nb - 1).astype(next_ref.dtype)).astype(BF16))
    r = _dot(brow_refs[g][...], jnp.concatenate(pieces, axis=0))
    return _dot(bcol_ref[g], r.astype(BF16))


def _pool_halo_specs(n, d):
    tb = POOL_TB
    nb = n // tb
    prev = pl.BlockSpec((tb, d), lambda i: (jnp.maximum(i - 1, 0), 0))
    cur = pl.BlockSpec((tb, d), lambda i: (i, 0))
    nxt = pl.BlockSpec((tb, d), lambda i: (jnp.minimum(i + 1, nb - 1), 0))
    return prev, cur, nxt


def _pool_const_specs(brow):
    tb = POOL_TB
    return [pl.BlockSpec(b.shape, lambda i: (0, 0)) for b in brow] + [pl.BlockSpec((N_POOL, tb, tb), lambda i: (0, 0, 0))]


def _pool_fwd(up, consts, pw_bf, pscale):
    n = up.shape[0]
    tb = POOL_TB
    nb = n // tb
    brow, bcol = consts
    prev, cur, nxt = _pool_halo_specs(n, D_MODEL)

    def body(p_ref, c_ref, n_ref, *rest):
        brow_refs, (bcol_ref, pw_ref, sc_ref, o_ref, d_ref) = rest[:N_POOL], rest[N_POOL:]
        i = pl.program_id(0)
        for g in range(N_POOL):
            sl = slice(g * POOL_DIM, (g + 1) * POOL_DIM)
            box = _pool_box(p_ref, c_ref, n_ref, brow_refs, bcol_ref, g, i, nb, False)
            dd = (box * _pool_inv(i, g, n) - c_ref[:, sl].astype(F32)).astype(BF16)
            d_ref[:, sl] = dd
            o_ref[:, sl] = (_dot(dd, pw_ref[g]) * sc_ref[:, sl]).astype(BF16)

    return pl.pallas_call(
        body, name="pool_fwd", grid=(nb,),
        in_specs=[prev, cur, nxt] + _pool_const_specs(brow) +
                 [pl.BlockSpec((N_POOL, POOL_DIM, POOL_DIM), lambda i: (0, 0, 0)), _par_spec(D_MODEL)],
        out_specs=[_row_spec(tb, D_MODEL), _row_spec(tb, D_MODEL)],
        out_shape=[jax.ShapeDtypeStruct((n, D_MODEL), BF16), jax.ShapeDtypeStruct((n, D_MODEL), BF16)],
        compiler_params=_cp(("parallel",)),
    )(up, up, up, *brow, bcol, pw_bf, pscale)


def _pool_bwd_a(dp, dsave, pw_bf, pwt_bf, pscale):
    n = dp.shape[0]
    tb = POOL_TB

    def body(dp_ref, d_ref, pw_ref, pwt_ref, sc_ref, dd_ref, dds_ref, gw_ref, gs_ref):
        i = pl.program_id(0)

        @pl.when(i == 0)
        def _():
            gw_ref[...] = jnp.zeros_like(gw_ref)
            gs_ref[...] = jnp.zeros_like(gs_ref)

        for g in range(N_POOL):
            sl = slice(g * POOL_DIM, (g + 1) * POOL_DIM)
            dpv = dp_ref[:, sl].astype(F32)
            dv = d_ref[:, sl]
            dpw_bf = (dpv * sc_ref[:, sl]).astype(BF16)
            dd = _dot(dpw_bf, pwt_ref[g])
            dd_ref[:, sl] = dd.astype(BF16)
            dds_ref[:, sl] = (dd * _pool_inv(i, g, n)).astype(BF16)
            gw_ref[g] += _dot_tn(dv, dpw_bf)
            gs_ref[0:1, sl] += jnp.sum(dpv * _dot(dv, pw_ref[g]), axis=0, keepdims=True)

    wspec = pl.BlockSpec((N_POOL, POOL_DIM, POOL_DIM), lambda i: (0, 0, 0))
    return pl.pallas_call(
        body, name="pool_bwd_a", grid=(n // tb,),
        in_specs=[_row_spec(tb, D_MODEL), _row_spec(tb, D_MODEL), wspec, wspec, _par_spec(D_MODEL)],
        out_specs=[_row_spec(tb, D_MODEL), _row_spec(tb, D_MODEL), wspec, _acc_spec(D_MODEL)],
        out_shape=[jax.ShapeDtypeStruct((n, D_MODEL), BF16), jax.ShapeDtypeStruct((n, D_MODEL), BF16),
                   jax.ShapeDtypeStruct((N_POOL, POOL_DIM, POOL_DIM), F32), jax.ShapeDtypeStruct((8, D_MODEL), F32)],
        compiler_params=_cp(("arbitrary",)),
    )(dp, dsave, pw_bf, pwt_bf, pscale)


def _pool_bwd_b(dds, dd, consts_t):
    n = dd.shape[0]
    tb = POOL_TB
    nb = n // tb
    brow, bcol = consts_t
    prev, cur, nxt = _pool_halo_specs(n, D_MODEL)

    def body(p_ref, c_ref, n_ref, *rest):
        brow_refs, (bcol_ref, dd_ref, o_ref) = rest[:N_POOL], rest[N_POOL:]
        i = pl.program_id(0)
        for g in range(N_POOL):
            sl = slice(g * POOL_DIM, (g + 1) * POOL_DIM)
            box = _pool_box(p_ref, c_ref, n_ref, brow_refs, bcol_ref, g, i, nb, True)
            o_ref[:, sl] = (box - dd_ref[:, sl].astype(F32)).astype(BF16)

    return pl.pallas_call(
        body, name="pool_bwd_b", grid=(nb,),
        in_specs=[prev, cur, nxt] + _pool_const_specs(brow) + [_row_spec(tb, D_MODEL)],
        out_specs=_row_spec(tb, D_MODEL),
        out_shape=jax.ShapeDtypeStruct((n, D_MODEL), BF16),
        compiler_params=_cp(("parallel",)),
    )(dds, dds, dds, *brow, bcol, dd)


def _pair_add(slabs, recvs, core, name):
    na = len(slabs)
    hr = [t.shape[1] // 4 for t in slabs]

    def body(core_ref, *refs):
        for a in range(na):
            refs[2 * na + a][...] = (refs[a][...] + refs[na + a][...]).astype(BF16)

    own = [pl.BlockSpec((None, hr[a], slabs[a].shape[2]), lambda j, i, c_ref: (j, 2 * c_ref[0] + i, 0)) for a in range(na)]
    got = [pl.BlockSpec((None, hr[a], slabs[a].shape[2]), lambda j, i, c_ref: (j, i, 0)) for a in range(na)]
    return pl.pallas_call(
        body, name=name,
        grid_spec=pltpu.PrefetchScalarGridSpec(num_scalar_prefetch=1, grid=(4, 2), in_specs=own + got, out_specs=got),
        out_shape=[jax.ShapeDtypeStruct(r.shape, BF16) for r in recvs],
        compiler_params=_cp(("arbitrary", "arbitrary")),
    )(core, *slabs, *recvs)


def _sum4(parts, core):
    na = len(parts)
    hr = [t.shape[1] // 2 for t in parts]

    def body(core_ref, *refs):
        for a in range(na):
            p = refs[a]
            refs[na + a][...] = ((p[0].astype(F32) + p[1].astype(F32)) + p[2].astype(F32)) + p[3].astype(F32)

    return pl.pallas_call(
        body, name="reduce_g_sum",
        grid_spec=pltpu.PrefetchScalarGridSpec(
            num_scalar_prefetch=1, grid=(2,),
            in_specs=[pl.BlockSpec((4, hr[a], parts[a].shape[2]), lambda i, c_ref: (0, i, 0)) for a in range(na)],
            out_specs=[pl.BlockSpec((hr[a], parts[a].shape[2]), lambda i, c_ref: (2 * c_ref[0] + i, 0))
                       for a in range(na)]),
        out_shape=[jax.ShapeDtypeStruct((2 * t.shape[1], t.shape[2]), F32) for t in parts],
        compiler_params=_cp(("arbitrary",)),
    )(core, *parts)


def _adamw(w, g, m, v, name):
    r, cdim = w.shape
    tb = _row_block(r, 256)
    c1 = 1.0 - ADAM_B1 ** ADAM_STEP
    c2 = 1.0 - ADAM_B2 ** ADAM_STEP

    def body(w_ref, g_ref, m_ref, v_ref, d_ref, nm_ref, nv_ref):
        gv = g_ref[...]
        nm = ADAM_B1 * m_ref[...] + (1.0 - ADAM_B1) * gv
        nv = ADAM_B2 * v_ref[...] + (1.0 - ADAM_B2) * (gv * gv)
        m_hat = nm / c1
        v_hat = nv / c2
        d_ref[...] = -ADAM_LR * (m_hat / (jnp.sqrt(v_hat) + ADAM_EPS) + ADAM_WD * w_ref[...])
        nm_ref[...] = nm
        nv_ref[...] = nv

    spec = _row_spec(tb, cdim)
    shp = jax.ShapeDtypeStruct((r, cdim), F32)
    return pl.pallas_call(
        body, name=name, grid=(r // tb,),
        in_specs=[spec] * 4, out_specs=[spec] * 3, out_shape=[shp] * 3,
        compiler_params=_cp(("parallel",)),
    )(w, g, m, v)


def _mesh_pos():
    return lax.axis_index("x"), lax.axis_index("y"), lax.axis_index("c")


_ANY = pl.BlockSpec(memory_space=pl.ANY)


def _remote(src, dst, send_sem, recv_sem, device):
    return pltpu.make_async_remote_copy(src_ref=src, dst_ref=dst, send_sem=send_sem, recv_sem=recv_sem,
                                        device_id=device, device_id_type=MESH)


def _other_chips(x, y):
    return [(1 - x, y), (x, 1 - y), (1 - x, 1 - y)]


def _half(nrows, h):
    return pl.ds(h * (nrows // 2), nrows // 2)


_GATHER_SEMS = 7


def _gather_steps(ins, outs, send_sems, recv_sems):
    na = len(ins)
    nrow = [r.shape[0] for r in ins]

    def copies():
        x, y, c = _mesh_pos()
        me = 2 * x + y
        sib = (x, y, 1 - c)
        chips = _other_chips(x, y)

        def ici(k, a, slot):
            px, py = chips[k]
            rows = _half(nrow[a], c)
            return _remote(ins[a].at[rows, :], outs[a].at[slot, rows, :], send_sems.at[k * na + a],
                           recv_sems.at[k * na + a], (px, py, c))

        def fwd(k, a, h):
            px, py = chips[k]
            blk = outs[a].at[2 * px + py, _half(nrow[a], h), :]
            return _remote(blk, blk, send_sems.at[(3 + k) * na + a], recv_sems.at[(3 + k) * na + a], sib)

        def own(a):
            return _remote(ins[a], outs[a].at[me], send_sems.at[6 * na + a], recv_sems.at[6 * na + a], sib)

        slots = [2 * px + py for px, py in chips]
        return ici, fwd, own, me, c, slots

    def start():
        ici, _, own, me, _, _ = copies()
        for a in range(na):
            own(a).start()
        for k in range(3):
            for a in range(na):
                ici(k, a, me).start()

    def finish():
        ici, fwd, own, me, c, slots = copies()
        for k in range(3):
            for a in range(na):
                ici(k, a, slots[k]).wait_recv()
                fwd(k, a, c).start()
        for k in range(3):
            for a in range(na):
                fwd(k, a, 1 - c).wait_recv()
        for a in range(na):
            own(a).wait_recv()
        for a in range(na):
            own(a).wait_send()
        for k in range(3):
            for a in range(na):
                ici(k, a, me).wait_send()
                fwd(k, a, c).wait_send()

    return start, finish


def _exchange_steps(ins, outs, send_sems, recv_sems, local_sems):
    na = len(ins)

    def copies():
        x, y, c = _mesh_pos()
        me = 2 * x + y
        chips = _other_chips(x, y)

        def copy(k, a, slot):
            px, py = chips[k]
            return _remote(ins[a].at[2 * px + py], outs[a].at[slot], send_sems.at[k * na + a], recv_sems.at[k * na + a],
                           (px, py, c))

        def local(a):
            return pltpu.make_async_copy(ins[a].at[me], outs[a].at[me], local_sems.at[a])

        return copy, local, me, [2 * px + py for px, py in chips]

    def start():
        copy, local, me, _ = copies()
        for a in range(na):
            local(a).start()
        for k in range(3):
            for a in range(na):
                copy(k, a, me).start()

    def finish():
        copy, local, me, slots = copies()
        for k in range(3):
            for a in range(na):
                copy(k, a, slots[k]).wait_recv()
        for k in range(3):
            for a in range(na):
                copy(k, a, me).wait_send()
        for a in range(na):
            local(a).wait()

    return start, finish


def _gather_weights(shards, conv8):
    na = len(shards)

    def body(*refs):
        ins, conv_in = refs[:na], refs[na]
        outs, conv_out = refs[na + 1:2 * na + 1], refs[2 * na + 1]
        send_sems, recv_sems, local_sems = refs[2 * na + 2:]
        x, y, c = _mesh_pos()
        me = 2 * x + y
        chips = _other_chips(x, y)

        def conv(k, slot):
            px, py = chips[k]
            return _remote(conv_in, conv_out.at[slot], send_sems.at[7 * na + k], recv_sems.at[7 * na + k], (px, py, c))

        start, finish = _gather_steps(ins, outs, send_sems, recv_sems)
        local = pltpu.make_async_copy(conv_in, conv_out.at[me], local_sems.at[0])
        local.start()
        start()
        sends = [conv(k, me) for k in range(3)]
        for cp in sends:
            cp.start()
        finish()
        for k in range(3):
            px, py = chips[k]
            conv(k, 2 * px + py).wait_recv()
        for cp in sends:
            cp.wait_send()
        local.wait()

    nsem = _GATHER_SEMS * na + 3
    return pl.pallas_call(
        body, name="gather_w", in_specs=[_ANY] * (na + 1), out_specs=[_ANY] * (na + 1),
        out_shape=[jax.ShapeDtypeStruct((4,) + t.shape, t.dtype) for t in shards] +
                  [jax.ShapeDtypeStruct((4,) + conv8.shape, conv8.dtype)],
        scratch_shapes=[pltpu.SemaphoreType.DMA((nsem,)), pltpu.SemaphoreType.DMA((nsem,)),
                        pltpu.SemaphoreType.DMA((1,))],
    )(*shards, conv8)


def _pair_swap(slabs, name):
    na = len(slabs)

    def body(*refs):
        ins, outs = refs[:na], refs[na:2 * na]
        send_sems, recv_sems = refs[2 * na:]
        x, y, c = _mesh_pos()
        cps = [_remote(ins[a].at[:, _half(slabs[a].shape[1], 1 - c), :], outs[a], send_sems.at[a], recv_sems.at[a],
                       (x, y, 1 - c)) for a in range(na)]
        for cp in cps:
            cp.start()
        for cp in cps:
            cp.wait()

    return pl.pallas_call(
        body, name=name, in_specs=[_ANY] * na, out_specs=[_ANY] * na,
        out_shape=[jax.ShapeDtypeStruct((4, t.shape[1] // 2, t.shape[2]), t.dtype) for t in slabs],
        scratch_shapes=[pltpu.SemaphoreType.DMA((na,)), pltpu.SemaphoreType.DMA((na,))],
    )(*slabs)


def _chip_exchange(pairs):
    na = len(pairs)

    def body(*refs):
        start, finish = _exchange_steps(refs[:na], refs[na:2 * na], *refs[2 * na:])
        start()
        finish()

    return pl.pallas_call(
        body, name="reduce_g_ici", in_specs=[_ANY] * na, out_specs=[_ANY] * na,
        out_shape=[jax.ShapeDtypeStruct(t.shape, t.dtype) for t in pairs],
        scratch_shapes=[pltpu.SemaphoreType.DMA((3 * na,)), pltpu.SemaphoreType.DMA((3 * na,)),
                        pltpu.SemaphoreType.DMA((na,))],
    )(*pairs)


def _share_halves(totals):
    na = len(totals)

    def body(*refs):
        bufs = refs[na:2 * na]
        send_sems, recv_sems = refs[2 * na:]
        x, y, c = _mesh_pos()

        def copy(a, h):
            blk = bufs[a].at[_half(totals[a].shape[0], h), :]
            return _remote(blk, blk, send_sems.at[a], recv_sems.at[a], (x, y, 1 - c))

        sends = [copy(a, c) for a in range(na)]
        for cp in sends:
            cp.start()
        for a in range(na):
            copy(a, 1 - c).wait_recv()
        for cp in sends:
            cp.wait_send()

    return pl.pallas_call(
        body, name="reduce_g_share", in_specs=[_ANY] * na, out_specs=[_ANY] * na,
        out_shape=[jax.ShapeDtypeStruct(t.shape, t.dtype) for t in totals],
        input_output_aliases={a: a for a in range(na)},
        scratch_shapes=[pltpu.SemaphoreType.DMA((na,)), pltpu.SemaphoreType.DMA((na,))],
    )(*totals)


def _allreduce_small(v, name):
    r, cdim = v.shape

    def body(v_ref, out_ref, buf, send_sems, recv_sems):
        x, y, c = _mesh_pos()
        me = 4 * x + 2 * y + c
        buf[me] = v_ref[...]
        rel = [(bx, by, bc) for bx in (0, 1) for by in (0, 1) for bc in (0, 1)][1:]

        def peer(b):
            bx, by, bc = b
            return ((1 - x) if bx else x, (1 - y) if by else y, (1 - c) if bc else c)

        def copy(k, slot):
            return pltpu.make_async_remote_copy(
                src_ref=v_ref, dst_ref=buf.at[slot], send_sem=send_sems.at[k], recv_sem=recv_sems.at[k],
                device_id=peer(rel[k]), device_id_type=MESH)

        sends = [copy(k, me) for k in range(7)]
        for cp in sends:
            cp.start()
        for k in range(7):
            px, py, pc = peer(rel[k])
            copy(k, 4 * px + 2 * py + pc).wait_recv()
        for cp in sends:
            cp.wait_send()
        acc = buf[0]
        for j in range(1, 8):
            acc = acc + buf[j]
        out_ref[...] = acc

    vm = pl.BlockSpec(memory_space=pltpu.VMEM)
    return pl.pallas_call(
        body, name=name, in_specs=[vm], out_specs=[vm, vm],
        out_shape=[jax.ShapeDtypeStruct((r, cdim), F32), jax.ShapeDtypeStruct((8, r, cdim), F32)],
        scratch_shapes=[pltpu.SemaphoreType.DMA((7,)), pltpu.SemaphoreType.DMA((7,))],
    )(v)


def _chip_bcast(v, name):
    def body(v_ref, out_ref, send_sems, recv_sems):
        x, y, c = _mesh_pos()
        me = 2 * x + y
        chips = _other_chips(x, y)
        out_ref[me] = v_ref[...]

        def copy(k, slot):
            px, py = chips[k]
            return _remote(v_ref, out_ref.at[slot], send_sems.at[k], recv_sems.at[k], (px, py, c))

        sends = [copy(k, me) for k in range(3)]
        for cp in sends:
            cp.start()
        for k, (px, py) in enumerate(chips):
            copy(k, 2 * px + py).wait_recv()
        for cp in sends:
            cp.wait_send()

    vm = pl.BlockSpec(memory_space=pltpu.VMEM)
    return pl.pallas_call(
        body, name=name, in_specs=[vm], out_specs=vm,
        out_shape=jax.ShapeDtypeStruct((4,) + v.shape, F32),
        scratch_shapes=[pltpu.SemaphoreType.DMA((3,)), pltpu.SemaphoreType.DMA((3,))],
    )(v)


_BIG = (("in_proj", (D_MODEL, D_IN_PROJ // 4), 1), ("w_out", (2 * D_MODEL // 4, D_MODEL), 0),
        ("w_gate", (D_MODEL, D_FF // 4), 1), ("w_up", (D_MODEL, D_FF // 4), 1), ("w_down", (D_FF // 4, D_MODEL), 0),
        ("pool_w", (N_POOL * POOL_DIM // 4, POOL_DIM), None))


def _assemble(name, t):
    _, r, c = t.shape
    axis = {n: ax for n, _, ax in _BIG}[name]
    if axis == 0:
        return t.reshape(4 * r, c)
    if axis == 1:
        return t.transpose(1, 0, 2).reshape(r, 4 * c)
    return t.reshape(4, N_POOL, POOL_DIM // 4, POOL_DIM).transpose(1, 0, 2, 3).reshape(N_POOL, POOL_DIM, POOL_DIM)


def _to_slabs(name, g):
    (r, c), axis = {n: (sh, ax) for n, sh, ax in _BIG}[name]
    if axis == 0:
        return g.reshape(4, r, c)
    if axis == 1:
        return g.reshape(r, 4, c).transpose(1, 0, 2)
    return g.reshape(N_POOL, 4, POOL_DIM // 4, POOL_DIM).transpose(1, 0, 2, 3).reshape(4, r, c)


_EARLY = ("in_proj",)
_LATE = tuple(n for n, _, _ in _BIG if n not in _EARLY)


def _reduce_grads(early_grads, late_parts, core):
    slabs = [_to_slabs(n, early_grads[n]) for n in _EARLY]
    pairs = _pair_add(slabs, _pair_swap(slabs, "reduce_g_d2d"), core, "reduce_g_pair")
    parts = dict(zip(_EARLY, _chip_exchange(pairs)), **dict(zip(_LATE, late_parts)))
    names = [n for n, _, _ in _BIG]
    totals = _sum4([parts[n] for n in names], core)
    return dict(zip(names, _share_halves(totals)))


def _pad_cols(w, n):
    return jnp.concatenate([w, jnp.zeros((w.shape[0], n - w.shape[1]), w.dtype)], axis=1)


def _device_step(x, mod, mod_ctx, ctx, target, wts, w8, small, tb, late_shards=None, core=None):
    n = x.shape[0]
    d = D_MODEL

    win = wts["in_proj"]
    wz, wxd, wup = win[:, 0:d], _pad_cols(win[:, d:d + D_XBC + 2 * SSD_HEADS], D_XD), win[:, d + D_XBC + 2 * SSD_HEADS:]

    emb_g, emb_b = _vec(small["emb_ln_g"]), _vec(small["emb_ln_b"])
    ln1_g, ln1_b = _vec(small["ln1_g"]), _vec(small["ln1_b"])
    ln2_g, ln2_b = _vec(small["ln2_g"]), _vec(small["ln2_b"])
    gn = _vec(small["ssd_norm_g"])
    pscale = _vec(small["pool_scale"])
    conv_b = _vec(small["conv_b"])
    dskip_e = jnp.repeat(small["d_skip"].reshape(-1), HEAD_DIM).reshape(1, d)
    zpad = jnp.zeros((2, 1, 128 - SSD_HEADS), F32)
    bias2 = jnp.concatenate([small["dt_bias"].reshape(2, 1, SSD_HEADS), zpad], axis=2)
    a2 = jnp.concatenate([-jnp.exp(small["a_log"].reshape(2, 1, SSD_HEADS)), zpad], axis=2)
    rexp = (jnp.arange(128)[:, None] == (jnp.arange(d)[None, :] // HEAD_DIM)).astype(BF16)
    rexp_t = rexp.T

    sh1, sc1, g1, sh2, sc2, g2 = [mod[:, i * d:(i + 1) * d] for i in range(6)]
    sh1c, sc1c = mod_ctx[:, 0:d], mod_ctx[:, d:2 * d]

    tbc = min(tb, ctx.shape[0])
    xc0, hc = _ln_mod(ctx, emb_g, emb_b, sh1c, sc1c, tbc, "ln_mod_ctx")
    xdc = _matmul_nn([(hc, wxd)], F32, 512, D_XD, "in_proj_ctx")
    uc = _conv_fwd(xdc, w8, conv_b, tbc, "conv_fwd_ctx")
    hzero = jnp.zeros((2, D_STATE, d), F32)
    _, hprev_c, hfin_c = _ssd_fwd(uc, xdc, bias2, a2, rexp, hzero, "ssd_fwd_ctx")

    x0, h1 = _ln_mod(x, emb_g, emb_b, sh1, sc1, tb, "ln_mod")
    z, xd, up = _matmul_nn_multi(h1, [wz, wxd, wup], [BF16, F32, BF16], 512, "in_proj")
    u = _conv_fwd(xd, w8, conv_b, tb, "conv_fwd")
    y, hprev, _, *landed = _ssd_fwd(u, xd, bias2, a2, rexp, hfin_c, "ssd_fwd", gather=late_shards or ())
    if late_shards is not None:
        wts = dict(wts, **{nme: _assemble(nme, t) for nme, t in zip(_LATE, landed)})
    wout = wts["w_out"]
    wg, wu, wd = wts["w_gate"], wts["w_up"], wts["w_down"]
    pw = wts["pool_w"]
    yn = _merge_fwd(y, u, z, dskip_e, gn, tb)
    pconst = _pool_consts(False)
    pool, dsave = _pool_fwd(up, pconst, pw, pscale)
    mix = _matmul_nn([(yn, wout[0:d]), (pool, wout[d:2 * d])], BF16, MM_ROWS, 1024, "out_proj")
    x1, h2 = _res_ln(x0, mix, g1, ln1_g, ln1_b, sh2, sc2, tb)

    gate, upp, hmid = _swiglu_fwd(h2, wg, wu, 512, D_FF // 2)
    ffn = _matmul_nn([(hmid, wd)], BF16, MM_ROWS, 1024, "ffn_down")
    dffn, dr2, acc2 = _final_ln_loss(x1, ffn, g2, ln2_g, ln2_b, target, tb)
    loss = (0.5 / d) * jnp.sum(acc2[3])

    dgate, dupp = _swiglu_bwd(dffn, wd.T, gate, upp, 512, D_FF // 2)
    g_wdown = _matmul_tn(hmid, dffn, MM_ROWS, 1024, "g_w_down")
    g_wgate, g_wup = _matmul_tn_multi(h2, [dgate, dupp], MM_ROWS, [D_FF // 2, D_FF // 2], "g_w_gate_up")
    dh2 = _matmul_nn([(dgate, wg.T), (dupp, wu.T)], BF16, 512, 1024, "d_h2")
    dmix, dr1, acc1 = _bwd_ln1(dr2, dh2, x1, x0, mix, g1, sc2, ln1_g, tb)

    dyn, dpool = _matmul_nn_multi(dmix, [wout[0:d].T, wout[d:2 * d].T], [BF16, BF16], MM_ROWS, "d_yn_pool")
    g_wout = jnp.concatenate([_matmul_tn(yn, dmix, MM_ROWS, 1024, "g_w_out_a"),
                              _matmul_tn(pool, dmix, MM_ROWS, 1024, "g_w_out_b")], axis=0)
    dd, dds, g_pw, accp = _pool_bwd_a(dpool, dsave, pw, jnp.swapaxes(pw, 1, 2), pscale)
    dup = _pool_bwd_b(dds, dd, _pool_consts(True))
    dy, dz, accm = _merge_bwd(dyn, y, u, z, dskip_e, gn, tb)
    lam0 = jnp.zeros((2, D_STATE, d), F32)
    late_grads = dict(w_out=g_wout, w_gate=g_wgate, w_up=g_wup, w_down=g_wdown, pool_w=g_pw)
    pairs = ()
    if late_shards is not None:
        slabs = [_to_slabs(nme, late_grads[nme]) for nme in _LATE]
        pairs = _pair_add(slabs, _pair_swap(slabs, "reduce_g_d2d_late"), core, "reduce_g_pair_late")
    dxs, dbc, ddt, accs, lam_c, *arrived = _ssd_bwd(u, xd, bias2, a2, rexp, rexp_t, dy, hprev, lam0, "ssd_bwd",
                                                    exchange=pairs)
    du, accb = _conv_bwd_a(dxs, dy, dskip_e, dbc, u, tb, "conv_bwd_a")
    dxd, accw = _conv_bwd_b(du, xd, ddt, w8, tb, "conv_bwd_b")

    lc = ctx.shape[0]
    zeros_c = jnp.zeros((lc, d), BF16)
    dxs_c, dbc_c, ddt_c, accs_c, _ = _ssd_bwd(uc, xdc, bias2, a2, rexp, rexp_t, zeros_c, hprev_c, lam_c, "ssd_bwd_ctx")
    du_c, accb_c = _conv_bwd_a(dxs_c, zeros_c, dskip_e, dbc_c, uc, tbc, "conv_bwd_a_ctx")
    dxd_c, accw_c = _conv_bwd_b(du_c, xdc, ddt_c, w8, tbc, "conv_bwd_b_ctx")
    dhc = _matmul_nn([(dxd_c, wxd.T)], F32, 512, 1024, "d_hc")
    _, acc0c = _bwd_ln0(None, dhc, ctx, emb_g, emb_b, sc1c, tbc, "bwd_ln0_ctx")

    dh1 = _matmul_nn([(dz, wz.T), (dxd, wxd.T), (dup, wup.T)], BF16, MM_ROWS, 1024, "d_h1")
    g_wz, g_wxd, g_wpo = _matmul_tn_multi(h1, [dz, dxd, dup], 512, [1024, D_XD, 1024], "g_in_proj")
    g_wxd = g_wxd + _matmul_tn(hc, dxd_c, 512, D_XD, "g_in_proj_xd_ctx")
    g_win = jnp.concatenate([g_wz, g_wxd[:, 0:D_XBC + 2 * SSD_HEADS], g_wpo], axis=1)
    grad_x, acc0 = _bwd_ln0(dr1, dh1, x, emb_g, emb_b, sc1, tb, "bwd_ln0")

    zero_d = jnp.zeros((1, d), F32)
    dmod = jnp.concatenate([acc0[1:2], acc0[0:1], acc1[4:5], acc1[1:2], acc1[0:1], acc2[2:3]], axis=1)
    dmodc = jnp.concatenate([acc0c[1:2], acc0c[0:1]] + [zero_d] * 4, axis=1)

    big = dict(in_proj=g_win)
    if late_shards is None:
        big.update(late_grads)
    sml = dict(
        dmod=dmod, dmod_ctx=dmodc, emb_ln_g=acc0[2] + acc0c[2], emb_ln_b=acc0[3] + acc0c[3],
        conv_w=accw[0:D_CONV] + accw_c[0:D_CONV], conv_b=accb[0] + accb_c[0],
        dt_bias=accs[:, 0, 0:SSD_HEADS] + accs_c[:, 0, 0:SSD_HEADS],
        a_log=accs[:, 1, 0:SSD_HEADS] + accs_c[:, 1, 0:SSD_HEADS],
        d_skip=jnp.sum(accm[1].reshape(SSD_HEADS, HEAD_DIM), axis=1),
        ssd_norm_g=accm[0], pool_scale=accp[0], ln1_g=acc1[2], ln1_b=acc1[3], ln2_g=acc2[0], ln2_b=acc2[1])
    return loss, grad_x, big, sml, (arrived if late_shards is not None else None)


_SMALL = ("c_ctx", "emb_ln_g", "emb_ln_b", "b_ada", "conv_w", "conv_b", "dt_bias", "a_log", "d_skip",
          "ssd_norm_g", "pool_scale", "ln1_g", "ln1_b", "ln2_g", "ln2_b")


def _small_rows(size):
    return -(-size // 1024)


def _pack_small(vals, names):
    pieces, rows = [], 0
    for nme in names:
        flat = vals[nme].reshape(-1).astype(F32)
        nr = _small_rows(flat.shape[0])
        pieces.append(flat)
        if nr * 1024 > flat.shape[0]:
            pieces.append(jnp.zeros((nr * 1024 - flat.shape[0],), F32))
        rows += nr
    if rows % 8:
        pieces.append(jnp.zeros(((8 - rows % 8) * 1024,), F32))
    return jnp.concatenate(pieces).reshape(-1, 1024)


def _small_offsets(shapes, names):
    out, off = {}, 0
    for nme in names:
        nr = _small_rows(math.prod(shapes[nme]))
        out[nme] = (off, nr)
        off += nr
    return out


def _unpack_small(packed, shapes, names):
    out = {}
    for nme, (off, nr) in _small_offsets(shapes, names).items():
        out[nme] = packed[off:off + nr].reshape(-1)[:math.prod(shapes[nme])].reshape(shapes[nme])
    return out


_WEIGHT_ORDER = ("c_ctx", "emb_ln_g", "emb_ln_b", "w_ada", "b_ada", "in_proj", "conv_w", "conv_b", "dt_bias", "a_log",
                 "d_skip", "ssd_norm_g", "pool_w", "pool_scale", "w_out", "ln1_g", "ln1_b", "w_gate", "w_up", "w_down",
                 "ln2_g", "ln2_b")


def _as2d(a):
    return a.reshape(-1, a.shape[-1])


def kernel(x, c, ctx, c_ctx, emb_ln_g, emb_ln_b, w_ada, b_ada, in_proj, conv_w, conv_b, dt_bias, a_log, d_skip, ssd_norm_g, pool_w, pool_scale, w_out, ln1_g, ln1_b, w_gate, w_up, w_down, ln2_g, ln2_b, loss_target, m_c_ctx, m_emb_ln_g, m_emb_ln_b, m_w_ada, m_b_ada, m_in_proj, m_conv_w, m_conv_b, m_dt_bias, m_a_log, m_d_skip, m_ssd_norm_g, m_pool_w, m_pool_scale, m_w_out, m_ln1_g, m_ln1_b, m_w_gate, m_w_up, m_w_down, m_ln2_g, m_ln2_b, v_c_ctx, v_emb_ln_g, v_emb_ln_b, v_w_ada, v_b_ada, v_in_proj, v_conv_w, v_conv_b, v_dt_bias, v_a_log, v_d_skip, v_ssd_norm_g, v_pool_w, v_pool_scale, v_w_out, v_ln1_g, v_ln1_b, v_w_gate, v_w_up, v_w_down, v_ln2_g, v_ln2_b):
    w = dict(c_ctx=c_ctx, emb_ln_g=emb_ln_g, emb_ln_b=emb_ln_b, w_ada=w_ada, b_ada=b_ada, in_proj=in_proj, conv_w=conv_w,
             conv_b=conv_b, dt_bias=dt_bias, a_log=a_log, d_skip=d_skip, ssd_norm_g=ssd_norm_g, pool_w=pool_w,
             pool_scale=pool_scale, w_out=w_out, ln1_g=ln1_g, ln1_b=ln1_b, w_gate=w_gate, w_up=w_up, w_down=w_down,
             ln2_g=ln2_g, ln2_b=ln2_b)
    m = dict(c_ctx=m_c_ctx, emb_ln_g=m_emb_ln_g, emb_ln_b=m_emb_ln_b, w_ada=m_w_ada, b_ada=m_b_ada, in_proj=m_in_proj,
             conv_w=m_conv_w, conv_b=m_conv_b, dt_bias=m_dt_bias, a_log=m_a_log, d_skip=m_d_skip,
             ssd_norm_g=m_ssd_norm_g, pool_w=m_pool_w, pool_scale=m_pool_scale, w_out=m_w_out, ln1_g=m_ln1_g,
             ln1_b=m_ln1_b, w_gate=m_w_gate, w_up=m_w_up, w_down=m_w_down, ln2_g=m_ln2_g, ln2_b=m_ln2_b)
    v = dict(c_ctx=v_c_ctx, emb_ln_g=v_emb_ln_g, emb_ln_b=v_emb_ln_b, w_ada=v_w_ada, b_ada=v_b_ada, in_proj=v_in_proj,
             conv_w=v_conv_w, conv_b=v_conv_b, dt_bias=v_dt_bias, a_log=v_a_log, d_skip=v_d_skip,
             ssd_norm_g=v_ssd_norm_g, pool_w=v_pool_w, pool_scale=v_pool_scale, w_out=v_w_out, ln1_g=v_ln1_g,
             ln1_b=v_ln1_b, w_gate=v_w_gate, w_up=v_w_up, w_down=v_w_down, ln2_g=v_ln2_g, ln2_b=v_ln2_b)

    xi, yi, ci = _mesh_pos()
    chip = 2 * xi + yi

    dev = 4 * xi + 2 * yi + ci
    d = D_MODEL
    core = ci.reshape(1).astype(jnp.int32)

    crow = jnp.concatenate([c.reshape(1, d), jnp.zeros((7, d), F32)], axis=0)
    _, c_all = _allreduce_small(crow, "gather_c")
    c16 = jnp.concatenate([c_all[:, 0, :], c_ctx.reshape(1, d), jnp.zeros((MOD_ROWS - 9, d), F32)], axis=0)
    ncol = w_ada.shape[-1]
    wada_bf = w_ada[0].astype(BF16)
    b_mine = lax.dynamic_slice_in_dim(b_ada, chip * ncol, ncol, axis=1)
    mods4 = _chip_bcast(_mods_fwd(c16, wada_bf, b_mine), "gather_mods")
    mods = mods4.transpose(1, 0, 2).reshape(MOD_ROWS, 4 * ncol)
    mod = lax.dynamic_slice_in_dim(mods, dev, 1, axis=0)
    mod_ctx = mods[8:9]

    shard = {name: w[name][0].astype(BF16).reshape(shp) for name, shp, _ in _BIG}
    conv8 = jnp.concatenate([conv_w[0], jnp.zeros((8 - D_CONV, conv_w.shape[-1]), F32)], axis=0)
    *gathered, conv4 = _gather_weights([shard[nme] for nme in _EARLY], conv8)
    wts = {nme: _assemble(nme, t) for nme, t in zip(_EARLY, gathered)}
    w8 = conv4.transpose(1, 0, 2).reshape(8, D_XBC)
    small = {nme: (w[nme] if nme in ("c_ctx", "emb_ln_g", "emb_ln_b") else w[nme][0]) for nme in _SMALL if nme != "conv_w"}

    loss, grad_x, big, sml, late_parts = _device_step(x[0], mod, mod_ctx, ctx[0], loss_target[0], wts, w8, small, 512,
                                                      late_shards=[shard[nme] for nme in _LATE], core=core)
    loss = lax.psum(loss, ("x", "y", "c"))

    g_big = _reduce_grads(big, late_parts, core)
    reduced = tuple(sml)
    small_shapes = {nme: sml[nme].shape for nme in reduced}
    total, each = _allreduce_small(_pack_small(sml, reduced), "reduce_small")
    g_small = _unpack_small(total, small_shapes, reduced)
    cw_cols = conv_w.shape[-1]
    g_small["conv_w"] = lax.dynamic_slice_in_dim(g_small["conv_w"], chip * cw_cols, cw_cols, axis=1)

    off, nr = _small_offsets(small_shapes, reduced)["dmod"]
    dm16 = jnp.concatenate([each[:, off:off + nr, :].reshape(8, nr * 1024)[:, :6 * d], g_small["dmod_ctx"],
                            jnp.zeros((MOD_ROWS - 9, 6 * d), F32)], axis=0)
    dm_mine = lax.dynamic_slice_in_dim(dm16, chip * ncol, ncol, axis=1)
    g_wada = _mods_bwd_w(c16.T, dm_mine)
    g_small["b_ada"] = _mods_bwd_b(dm16)[0:1]
    c_part = _mods_bwd_c(dm_mine, wada_bf, c16)[8:16]
    g_small["c_ctx"] = _allreduce_small(c_part, "reduce_c_ctx")[0][0]

    grads, delta, new_m, new_v = {}, {}, {}, {}
    grads["w_ada"] = g_wada[None]
    delta["w_ada"], new_m["w_ada"], new_v["w_ada"] = (
        t[None] for t in _adamw(w_ada[0], g_wada, m_w_ada[0], v_w_ada[0], "adamw_w_ada"))
    for name, _, _ in _BIG:
        g2 = _as2d(g_big[name])
        d2, m2, v2 = _adamw(_as2d(w[name][0]), g2, _as2d(m[name][0]), _as2d(v[name][0]), "adamw_" + name)
        grads[name] = g2.reshape(w[name].shape)
        delta[name], new_m[name], new_v[name] = (t.reshape(w[name].shape) for t in (d2, m2, v2))
    shp = {nme: w[nme].shape for nme in _SMALL}
    gp = _pack_small(g_small, _SMALL)
    dp, mp, vp = _adamw(_pack_small(w, _SMALL), gp, _pack_small(m, _SMALL), _pack_small(v, _SMALL), "adamw_small")
    for dst, src in ((grads, gp), (delta, dp), (new_m, mp), (new_v, vp)):
        dst.update(_unpack_small(src, shp, _SMALL))

    return (loss, grad_x[None], *[grads[nme] for nme in _WEIGHT_ORDER], *[delta[nme] for nme in _WEIGHT_ORDER],
            *[new_m[nme] for nme in _WEIGHT_ORDER], *[new_v[nme] for nme in _WEIGHT_ORDER])
```

```python
import functools
import math

import jax
import jax.numpy as jnp
from jax import lax
from jax.experimental import pallas as pl
from jax.experimental.pallas import tpu as pltpu

F32 = jnp.float32
BF16 = jnp.bfloat16
MESH = pl.DeviceIdType.MESH

D_MODEL = 1024
SSD_HEADS = 16
HEAD_DIM = 64
D_STATE = 128
CHUNK = 128
D_CONV = 5
D_XBC = D_MODEL + 2 * D_STATE
D_XD = 1408
N_POOL = 4
POOL_DIM = 256
POOL_WINDOWS = (2, 4, 8, 16)
GRID_W = 64
D_FF = 2816
D_IN_PROJ = 3360
LN_EPS = 1e-5
ALPHA = 2.0 ** 0.25
POOL_TB = 512
MM_ROWS = 1024

ADAM_LR = 0.001
ADAM_B1 = 0.9
ADAM_B2 = 0.999
ADAM_EPS = 1e-08
ADAM_WD = 0.01
ADAM_STEP = 10

VMEM_LIMIT = 56 * 1024 * 1024


def _cp(sem=None):
    return pltpu.CompilerParams(dimension_semantics=sem, vmem_limit_bytes=VMEM_LIMIT)


def _sigmoid(x):
    return 1.0 / (1.0 + jnp.exp(-x))


def _silu(x):
    return x * _sigmoid(x)


def _dsilu(x):
    s = _sigmoid(x)
    return s * (1.0 + x * (1.0 - s))


def _softplus(x):
    t = jnp.exp(-jnp.abs(x))
    u = 1.0 + t
    log1p = jnp.where(u == 1.0, t, jnp.log(u) * t / (u - 1.0 + (u == 1.0)))
    return jnp.maximum(x, 0.0) + log1p


def _split(x, n):
    parts, r = [], x
    for _ in range(n):
        p = r.astype(BF16)
        parts.append(p)
        r = r - p.astype(F32)
    return parts


def _dot(a, b):
    return jnp.dot(a, b, preferred_element_type=F32)


def _dot_nt(a, b):
    return lax.dot_general(a, b, (((1,), (1,)), ((), ())), preferred_element_type=F32)


def _dot_tn(a, b):
    return lax.dot_general(a, b, (((0,), (0,)), ((), ())), preferred_element_type=F32)


def _dot_sel_l(sel_bf, x, n=3):
    out = None
    for p in _split(x, n):
        t = _dot(sel_bf, p)
        out = t if out is None else out + t
    return out


def _dot_sel_r(x, sel_bf, n=3):
    out = None
    for p in _split(x, n):
        t = _dot(p, sel_bf)
        out = t if out is None else out + t
    return out


ROW_SUB = 16


def _row_tiles(tb):
    assert tb % ROW_SUB == 0
    return [slice(s * ROW_SUB, (s + 1) * ROW_SUB) for s in range(tb // ROW_SUB)]


def _fold8(v):
    out = v[0:8, :]
    for r in range(8, v.shape[0], 8):
        out = out + v[r:r + 8, :]
    return out


def _row_block(n, cap=256, mult=8):
    best = None
    for t in range(mult, min(n, cap) + 1, mult):
        if n % t == 0:
            best = t
    return best if best is not None else n


def _vec(v):
    return v.reshape(1, -1).astype(F32)


MOD_ROWS = 16
MOD_TN = 512


def _mods_fwd(c16, w_bf, b):
    r, d = c16.shape
    n = w_bf.shape[1]

    def body(c_ref, w_ref, b_ref, o_ref):
        s = _silu(c_ref[...]).astype(BF16)
        o_ref[...] = _dot(s, w_ref[...]) + b_ref[...]

    return pl.pallas_call(
        body, name="mods_fwd", grid=(n // MOD_TN,),
        in_specs=[pl.BlockSpec((r, d), lambda j: (0, 0)),
                  pl.BlockSpec((d, MOD_TN), lambda j: (0, j)),
                  pl.BlockSpec((1, MOD_TN), lambda j: (0, j))],
        out_specs=pl.BlockSpec((r, MOD_TN), lambda j: (0, j)),
        out_shape=jax.ShapeDtypeStruct((r, n), F32),
        compiler_params=_cp(("arbitrary",)),
    )(c16, w_bf, b)


def _mods_bwd_w(ct16, dm16):
    d = ct16.shape[0]
    n = dm16.shape[1]

    def body(ct_ref, dm_ref, dw_ref):
        s = _silu(ct_ref[...])
        dm = dm_ref[...]
        acc = s[:, 0:1] * dm[0:1, :]
        for r in range(1, 9):
            acc = acc + s[:, r:r + 1] * dm[r:r + 1, :]
        dw_ref[...] = acc

    return pl.pallas_call(
        body, name="mods_bwd_w", grid=(n // MOD_TN,),
        in_specs=[pl.BlockSpec((d, MOD_ROWS), lambda j: (0, 0)),
                  pl.BlockSpec((MOD_ROWS, MOD_TN), lambda j: (0, j))],
        out_specs=pl.BlockSpec((d, MOD_TN), lambda j: (0, j)),
        out_shape=jax.ShapeDtypeStruct((d, n), F32),
        compiler_params=_cp(("arbitrary",)),
    )(ct16, dm16)


def _mods_bwd_c(dm16, w_bf, c16):
    d = c16.shape[1]
    n = dm16.shape[1]
    nk = n // MOD_TN

    def body(dm_ref, w_ref, c_ref, o_ref):
        k = pl.program_id(0)

        @pl.when(k == 0)
        def _():
            o_ref[...] = jnp.zeros_like(o_ref)

        o_ref[...] += _dot_nt(dm_ref[...].astype(BF16), w_ref[...])

        @pl.when(k == nk - 1)
        def _():
            o_ref[...] = o_ref[...] * (0.5 * _dsilu(c_ref[...]))

    return pl.pallas_call(
        body, name="mods_bwd_c", grid=(nk,),
        in_specs=[pl.BlockSpec((MOD_ROWS, MOD_TN), lambda k: (0, k)),
                  pl.BlockSpec((d, MOD_TN), lambda k: (0, k)),
                  pl.BlockSpec((MOD_ROWS, d), lambda k: (0, 0))],
        out_specs=pl.BlockSpec((MOD_ROWS, d), lambda k: (0, 0)),
        out_shape=jax.ShapeDtypeStruct((MOD_ROWS, d), F32),
        compiler_params=_cp(("arbitrary",)),
    )(dm16, w_bf, c16)


def _mods_bwd_b(dm16):
    n = dm16.shape[1]

    def body(dm_ref, o_ref):
        dm = dm_ref[...]
        acc = dm[0:1, :]
        for r in range(1, 9):
            acc = acc + dm[r:r + 1, :]
        o_ref[...] = jnp.broadcast_to(acc, (8, MOD_TN))

    return pl.pallas_call(
        body, name="mods_bwd_b", grid=(n // MOD_TN,),
        in_specs=[pl.BlockSpec((MOD_ROWS, MOD_TN), lambda j: (0, j))],
        out_specs=pl.BlockSpec((8, MOD_TN), lambda j: (0, j)),
        out_shape=jax.ShapeDtypeStruct((8, n), F32),
        compiler_params=_cp(("arbitrary",)),
    )(dm16)


def _ln_stats(x):
    mu = jnp.mean(x, axis=-1, keepdims=True)
    xc = x - mu
    var = jnp.mean(xc * xc, axis=-1, keepdims=True)
    rstd = lax.rsqrt(var + LN_EPS)
    return xc * rstd, rstd


def _ln_bwd(dxhat, xhat, rstd):
    m1 = jnp.mean(dxhat, axis=-1, keepdims=True)
    m2 = jnp.mean(dxhat * xhat, axis=-1, keepdims=True)
    return rstd * (dxhat - m1 - xhat * m2)


def _row_spec(tb, d):
    return pl.BlockSpec((tb, d), lambda i: (i, 0))


def _par_spec(d):
    return pl.BlockSpec((1, d), lambda i: (0, 0))


def _acc_spec(d):
    return pl.BlockSpec((8, d), lambda i: (0, 0))


def _ln_mod(x, g, b, sh, sc, tb, name):
    n, d = x.shape

    def body(x_ref, g_ref, b_ref, sh_ref, sc_ref, x0_ref, h_ref):
        g, b, sh, sc1 = g_ref[...], b_ref[...], sh_ref[...], 1.0 + sc_ref[...]
        for r in _row_tiles(min(tb, n)):
            xhat, _ = _ln_stats(x_ref[r, :])
            x0 = xhat * g + b
            x0_ref[r, :] = x0
            h_ref[r, :] = (x0 * sc1 + sh).astype(BF16)

    return pl.pallas_call(
        body, name=name, grid=(n // tb,),
        in_specs=[_row_spec(tb, d)] + [_par_spec(d)] * 4,
        out_specs=[_row_spec(tb, d), _row_spec(tb, d)],
        out_shape=[jax.ShapeDtypeStruct((n, d), F32), jax.ShapeDtypeStruct((n, d), BF16)],
        compiler_params=_cp(("parallel",)),
    )(x, g, b, sh, sc)


def _res_ln(xres, mix, gate, g, b, sh, sc, tb):
    n, d = xres.shape

    def body(xr_ref, mix_ref, gate_ref, g_ref, b_ref, sh_ref, sc_ref, x1_ref, h_ref):
        gate_v, g, b, sh, sc1 = gate_ref[...], g_ref[...], b_ref[...], sh_ref[...], 1.0 + sc_ref[...]
        for r in _row_tiles(tb):
            xhat, _ = _ln_stats(ALPHA * xr_ref[r, :] + gate_v * mix_ref[r, :].astype(F32))
            x1 = xhat * g + b
            x1_ref[r, :] = x1
            h_ref[r, :] = (x1 * sc1 + sh).astype(BF16)

    return pl.pallas_call(
        body, name="res_ln1", grid=(n // tb,),
        in_specs=[_row_spec(tb, d)] * 2 + [_par_spec(d)] * 5,
        out_specs=[_row_spec(tb, d), _row_spec(tb, d)],
        out_shape=[jax.ShapeDtypeStruct((n, d), F32), jax.ShapeDtypeStruct((n, d), BF16)],
        compiler_params=_cp(("parallel",)),
    )(xres, mix, gate, g, b, sh, sc)


def _final_ln_loss(x1, ffn, gate, g, b, target, tb):
    n, d = x1.shape

    def body(x1_ref, ffn_ref, gate_ref, g_ref, b_ref, t_ref, dffn_ref, dr_ref, acc_ref):
        i = pl.program_id(0)

        @pl.when(i == 0)
        def _():
            acc_ref[...] = jnp.zeros_like(acc_ref)

        gate_v, g, b = gate_ref[...], g_ref[...], b_ref[...]
        parts = [jnp.zeros((8, d), F32)] * 4
        for r in _row_tiles(tb):
            ffn = ffn_ref[r, :].astype(F32)
            xhat, rstd = _ln_stats(ALPHA * x1_ref[r, :] + gate_v * ffn)
            err = xhat * g + b - t_ref[r, :]
            dx2 = err * (1.0 / d)
            dr = _ln_bwd(dx2 * g, xhat, rstd)
            dr_ref[r, :] = dr
            dffn_ref[r, :] = (gate_v * dr).astype(BF16)
            terms = (dx2 * xhat, dx2, dr * ffn, err * err)
            parts = [p + _fold8(t) for p, t in zip(parts, terms)]
        for j, p in enumerate(parts):
            acc_ref[j:j + 1, :] += jnp.sum(p, axis=0, keepdims=True)

    return pl.pallas_call(
        body, name="final_ln_loss", grid=(n // tb,),
        in_specs=[_row_spec(tb, d)] * 2 + [_par_spec(d)] * 3 + [_row_spec(tb, d)],
        out_specs=[_row_spec(tb, d), _row_spec(tb, d), _acc_spec(d)],
        out_shape=[jax.ShapeDtypeStruct((n, d), BF16), jax.ShapeDtypeStruct((n, d), F32),
                   jax.ShapeDtypeStruct((8, d), F32)],
        compiler_params=_cp(("arbitrary",)),
    )(x1, ffn, gate, g, b, target)


def _bwd_ln1(dr2, dh2, x1, x0, mix, gate, sc2, g, tb):
    n, d = x1.shape

    def body(dr2_ref, dh2_ref, x1_ref, x0_ref, mix_ref, gate_ref, sc_ref, g_ref, dmix_ref, dr1_ref, acc_ref):
        i = pl.program_id(0)

        @pl.when(i == 0)
        def _():
            acc_ref[...] = jnp.zeros_like(acc_ref)

        gate_v, g, sc1 = gate_ref[...], g_ref[...], 1.0 + sc_ref[...]
        parts = [jnp.zeros((8, d), F32)] * 5
        for r in _row_tiles(tb):
            dh2 = dh2_ref[r, :].astype(F32)
            mix = mix_ref[r, :].astype(F32)
            dx1 = ALPHA * dr2_ref[r, :] + dh2 * sc1
            xhat, rstd = _ln_stats(ALPHA * x0_ref[r, :] + gate_v * mix)
            dr1 = _ln_bwd(dx1 * g, xhat, rstd)
            dr1_ref[r, :] = dr1
            dmix_ref[r, :] = (gate_v * dr1).astype(BF16)
            terms = (dh2 * x1_ref[r, :], dh2, dx1 * xhat, dx1, dr1 * mix)
            parts = [p + _fold8(t) for p, t in zip(parts, terms)]
        for j, p in enumerate(parts):
            acc_ref[j:j + 1, :] += jnp.sum(p, axis=0, keepdims=True)

    return pl.pallas_call(
        body, name="bwd_ln1", grid=(n // tb,),
        in_specs=[_row_spec(tb, d)] * 5 + [_par_spec(d)] * 3,
        out_specs=[_row_spec(tb, d), _row_spec(tb, d), _acc_spec(d)],
        out_shape=[jax.ShapeDtypeStruct((n, d), BF16), jax.ShapeDtypeStruct((n, d), F32),
                   jax.ShapeDtypeStruct((8, d), F32)],
        compiler_params=_cp(("arbitrary",)),
    )(dr2, dh2, x1, x0, mix, gate, sc2, g)


def _bwd_ln0(dres, dh, x, g, b, sc, tb, name):
    n, d = x.shape
    has_res = dres is not None

    def body(*refs):
        if has_res:
            dres_ref, dh_ref, x_ref, g_ref, b_ref, sc_ref, dx_ref, acc_ref = refs
        else:
            dh_ref, x_ref, g_ref, b_ref, sc_ref, dx_ref, acc_ref = refs
        i = pl.program_id(0)

        @pl.when(i == 0)
        def _():
            acc_ref[...] = jnp.zeros_like(acc_ref)

        g, b, sc1 = g_ref[...], b_ref[...], 1.0 + sc_ref[...]
        parts = [jnp.zeros((8, d), F32)] * 4
        for r in _row_tiles(tb):
            dh = dh_ref[r, :].astype(F32)
            xhat, rstd = _ln_stats(x_ref[r, :])
            x0 = xhat * g + b
            dx0 = dh * sc1
            if has_res:
                dx0 = dx0 + ALPHA * dres_ref[r, :]
            dx_ref[r, :] = _ln_bwd(dx0 * g, xhat, rstd)
            terms = (dh * x0, dh, dx0 * xhat, dx0)
            parts = [p + _fold8(t) for p, t in zip(parts, terms)]
        for j, p in enumerate(parts):
            acc_ref[j:j + 1, :] += jnp.sum(p, axis=0, keepdims=True)

    ins = ([dres] if has_res else []) + [dh, x, g, b, sc]
    return pl.pallas_call(
        body, name=name, grid=(n // tb,),
        in_specs=[_row_spec(tb, d)] * (3 if has_res else 2) + [_par_spec(d)] * 3,
        out_specs=[_row_spec(tb, d), _acc_spec(d)],
        out_shape=[jax.ShapeDtypeStruct((n, d), F32), jax.ShapeDtypeStruct((8, d), F32)],
        compiler_params=_cp(("arbitrary",)),
    )(*ins)


def _matmul_nn(pairs, out_dtype, tm, tn, name):
    m = pairs[0][0].shape[0]
    n = pairs[0][1].shape[1]
    tm = min(tm, m)
    tn = min(tn, n)
    npair = len(pairs)

    def body(*refs):
        o_ref = refs[-1]
        acc = None
        for p in range(npair):
            t = _dot(refs[2 * p][...].astype(BF16), refs[2 * p + 1][...])
            acc = t if acc is None else acc + t
        o_ref[...] = acc.astype(out_dtype)

    in_specs, args = [], []
    for a, b in pairs:
        k = a.shape[1]
        in_specs += [pl.BlockSpec((tm, k), lambda i, j: (i, 0)), pl.BlockSpec((k, tn), lambda i, j: (0, j))]
        args += [a, b]
    return pl.pallas_call(
        body, name=name, grid=(m // tm, n // tn),
        in_specs=in_specs,
        out_specs=pl.BlockSpec((tm, tn), lambda i, j: (i, j)),
        out_shape=jax.ShapeDtypeStruct((m, n), out_dtype),
        compiler_params=_cp(("parallel", "arbitrary")),
    )(*args)


def _matmul_tn(a, g, tm, tn, name):
    m, k = a.shape
    n = g.shape[1]
    tm = min(tm, m)
    tn = min(tn, n)

    def body(a_ref, g_ref, o_ref):
        i = pl.program_id(1)

        @pl.when(i == 0)
        def _():
            o_ref[...] = jnp.zeros_like(o_ref)

        o_ref[...] += _dot_tn(a_ref[...].astype(BF16), g_ref[...].astype(BF16))

    return pl.pallas_call(
        body, name=name, grid=(n // tn, m // tm),
        in_specs=[pl.BlockSpec((tm, k), lambda j, i: (i, 0)), pl.BlockSpec((tm, tn), lambda j, i: (i, j))],
        out_specs=pl.BlockSpec((k, tn), lambda j, i: (0, j)),
        out_shape=jax.ShapeDtypeStruct((k, n), F32),
        compiler_params=_cp(("parallel", "arbitrary")),
    )(a, g)


def _matmul_nn_multi(a, bs, out_dtypes, tm, name):
    m, k = a.shape
    tm = min(tm, m)
    nb = len(bs)

    def body(a_ref, *refs):
        av = a_ref[...].astype(BF16)
        for j in range(nb):
            refs[nb + j][...] = _dot(av, refs[j][...]).astype(out_dtypes[j])

    return pl.pallas_call(
        body, name=name, grid=(m // tm,),
        in_specs=[pl.BlockSpec((tm, k), lambda i: (i, 0))] + [pl.BlockSpec(b.shape, lambda i: (0, 0)) for b in bs],
        out_specs=[pl.BlockSpec((tm, b.shape[1]), lambda i: (i, 0)) for b in bs],
        out_shape=[jax.ShapeDtypeStruct((m, b.shape[1]), dt) for b, dt in zip(bs, out_dtypes)],
        compiler_params=_cp(("parallel",)),
    )(a, *bs)


def _matmul_tn_multi(a, gs, tm, tns, name):
    m, k = a.shape
    tm = min(tm, m)
    ng = len(gs)
    nj = gs[0].shape[1] // tns[0]
    assert all(g.shape[1] // t == nj and g.shape[1] % t == 0 for g, t in zip(gs, tns))

    def body(a_ref, *refs):
        i = pl.program_id(1)

        @pl.when(i == 0)
        def _():
            for j in range(ng):
                refs[ng + j][...] = jnp.zeros_like(refs[ng + j])

        av = a_ref[...].astype(BF16)
        for j in range(ng):
            refs[ng + j][...] += _dot_tn(av, refs[j][...].astype(BF16))

    return pl.pallas_call(
        body, name=name, grid=(nj, m // tm),
        in_specs=[pl.BlockSpec((tm, k), lambda j, i: (i, 0))] +
                 [pl.BlockSpec((tm, t), lambda j, i: (i, j)) for t in tns],
        out_specs=[pl.BlockSpec((k, t), lambda j, i: (0, j)) for t in tns],
        out_shape=[jax.ShapeDtypeStruct((k, g.shape[1]), F32) for g in gs],
        compiler_params=_cp(("parallel", "arbitrary")),
    )(a, *gs)


def _swiglu_fwd(h, wg, wu, tm, tn):
    m, k = h.shape
    n = wg.shape[1]
    tm = min(tm, m)

    def body(h_ref, wg_ref, wu_ref, gate_ref, up_ref, hmid_ref):
        hv = h_ref[...]
        gate = _dot(hv, wg_ref[...])
        up = _dot(hv, wu_ref[...])
        gate_ref[...] = gate.astype(BF16)
        up_ref[...] = up.astype(BF16)
        hmid_ref[...] = (_silu(gate) * up).astype(BF16)

    blk = pl.BlockSpec((tm, tn), lambda i, j: (i, j))
    wspec = pl.BlockSpec((k, tn), lambda i, j: (0, j))
    return pl.pallas_call(
        body, name="swiglu_fwd", grid=(m // tm, n // tn),
        in_specs=[pl.BlockSpec((tm, k), lambda i, j: (i, 0)), wspec, wspec],
        out_specs=[blk, blk, blk],
        out_shape=[jax.ShapeDtypeStruct((m, n), BF16), jax.ShapeDtypeStruct((m, n), BF16),
                   jax.ShapeDtypeStruct((m, n), BF16)],
        compiler_params=_cp(("parallel", "arbitrary")),
    )(h, wg, wu)


def _swiglu_bwd(dffn, wdt, gate, up, tm, tn):
    m, k = dffn.shape
    n = wdt.shape[1]
    tm = min(tm, m)

    def body(d_ref, w_ref, gate_ref, up_ref, dg_ref, du_ref):
        dh = _dot(d_ref[...], w_ref[...])
        gate = gate_ref[...].astype(F32)
        dg_ref[...] = (dh * up_ref[...].astype(F32) * _dsilu(gate)).astype(BF16)
        du_ref[...] = (dh * _silu(gate)).astype(BF16)

    blk = pl.BlockSpec((tm, tn), lambda i, j: (i, j))
    return pl.pallas_call(
        body, name="swiglu_bwd", grid=(m // tm, n // tn),
        in_specs=[pl.BlockSpec((tm, k), lambda i, j: (i, 0)), pl.BlockSpec((k, tn), lambda i, j: (0, j)), blk, blk],
        out_specs=[blk, blk],
        out_shape=[jax.ShapeDtypeStruct((m, n), BF16), jax.ShapeDtypeStruct((m, n), BF16)],
        compiler_params=_cp(("parallel", "arbitrary")),
    )(dffn, wdt, gate, up)


def _halo_specs(tb, width, nrows):
    r8 = tb // 8
    last = nrows // 8 - 1
    prev = pl.BlockSpec((8, width), lambda i: (jnp.maximum(i * r8 - 1, 0), 0))
    nxt = pl.BlockSpec((8, width), lambda i: (jnp.minimum((i + 1) * r8, last), 0))
    return prev, nxt


CONV_SUB = 32


def _halo_scratch():
    return [pltpu.VMEM((CONV_SUB + 16, D_XBC), F32), pltpu.VMEM((CONV_SUB + 16, D_XBC), F32)]


def _shifted_rows(prev_ref, cur_ref, next_ref, top, bot, tb, i, nb):
    sub = CONV_SUB
    nsub = tb // sub
    assert nsub >= 2
    top[0:8, :] = prev_ref[...] * (i > 0).astype(F32)
    top[8:sub + 16, :] = cur_ref[0:sub + 8, :]
    bot[0:sub + 8, :] = cur_ref[tb - sub - 8:tb, :]
    bot[sub + 8:sub + 16, :] = next_ref[...] * (i < nb - 1).astype(F32)

    def rows(s, o):
        if s == 0:
            return top[8 + o:8 + o + sub, :]
        if s == nsub - 1:
            return bot[8 + o:8 + o + sub, :]
        return cur_ref[s * sub + o:(s + 1) * sub + o, :]

    return rows


def _conv_fwd(xd, w8, b, tb, name):
    n = xd.shape[0]
    tb = min(tb, n)
    nb = n // tb
    prev, nxt = _halo_specs(tb, D_XBC, n)

    def body(p_ref, c_ref, n_ref, w_ref, b_ref, u_ref, top, bot):
        i = pl.program_id(0)
        rows = _shifted_rows(p_ref, c_ref, n_ref, top, bot, tb, i, nb)
        w = [w_ref[k:k + 1, :] for k in range(D_CONV)]
        bias = jnp.broadcast_to(b_ref[...], (CONV_SUB, D_XBC))
        for s in range(tb // CONV_SUB):
            acc = bias
            for k in range(D_CONV):
                acc = acc + w[k] * rows(s, k - 2)
            u_ref[s * CONV_SUB:(s + 1) * CONV_SUB, :] = acc.astype(BF16)

    return pl.pallas_call(
        body, name=name, grid=(nb,),
        in_specs=[prev, pl.BlockSpec((tb, D_XBC), lambda i: (i, 0)), nxt,
                  pl.BlockSpec((8, D_XBC), lambda i: (0, 0)), _par_spec(D_XBC)],
        out_specs=_row_spec(tb, D_XBC),
        out_shape=jax.ShapeDtypeStruct((n, D_XBC), BF16),
        scratch_shapes=_halo_scratch(),
        compiler_params=_cp(("parallel",)),
    )(xd, xd, xd, w8, b)


def _conv_bwd_a(dxs, dy, dskip_e, dbc, u, tb, name):
    n = u.shape[0]
    tb = min(tb, n)

    def body(dxs_ref, dy_ref, sk_ref, dbc_ref, u_ref, du_ref, acc_ref):
        i = pl.program_id(0)

        @pl.when(i == 0)
        def _():
            acc_ref[...] = jnp.zeros_like(acc_ref)

        sk = sk_ref[...]
        part = jnp.zeros((8, D_XBC), F32)
        for r in _row_tiles(tb):
            gx = dxs_ref[0, r, :].astype(F32) + dxs_ref[1, r, :].astype(F32) + dy_ref[r, :].astype(F32) * sk
            gbc = dbc_ref[0, r, :] + dbc_ref[1, r, :]
            du = jnp.concatenate([gx, gbc], axis=1) * _dsilu(u_ref[r, :].astype(F32))
            du_ref[r, :] = du
            part = part + _fold8(du)
        acc_ref[0:1, :] += jnp.sum(part, axis=0, keepdims=True)

    return pl.pallas_call(
        body, name=name, grid=(n // tb,),
        in_specs=[pl.BlockSpec((2, tb, D_MODEL), lambda i: (0, i, 0)), _row_spec(tb, D_MODEL), _par_spec(D_MODEL),
                  pl.BlockSpec((2, tb, 2 * D_STATE), lambda i: (0, i, 0)), _row_spec(tb, D_XBC)],
        out_specs=[_row_spec(tb, D_XBC), _acc_spec(D_XBC)],
        out_shape=[jax.ShapeDtypeStruct((n, D_XBC), F32), jax.ShapeDtypeStruct((8, D_XBC), F32)],
        compiler_params=_cp(("arbitrary",)),
    )(dxs, dy, dskip_e, dbc, u)


def _conv_bwd_b(du, xd, ddt, w8, tb, name):
    n = du.shape[0]
    tb = min(tb, n)
    nb = n // tb
    prev, nxt = _halo_specs(tb, D_XBC, n)

    def body(dp_ref, dc_ref, dn_ref, xp_ref, xc_ref, xn_ref, ddt_ref, w_ref, dxd_ref, acc_ref, dtop, dbot, xtop, xbot):
        i = pl.program_id(0)

        @pl.when(i == 0)
        def _():
            acc_ref[...] = jnp.zeros_like(acc_ref)

        sub = CONV_SUB
        nsub = tb // sub
        du_rows = _shifted_rows(dp_ref, dc_ref, dn_ref, dtop, dbot, tb, i, nb)
        x_rows = _shifted_rows(xp_ref, xc_ref, xn_ref, xtop, xbot, tb, i, nb)
        w = [w_ref[k:k + 1, :] for k in range(D_CONV)]
        for s in range(nsub):
            acc = w[0] * du_rows(s, 2)
            for k in range(1, D_CONV):
                acc = acc + w[k] * du_rows(s, 2 - k)
            dxd_ref[s * sub:(s + 1) * sub, 0:D_XBC] = acc.astype(BF16)
        for k in range(D_CONV):
            part = jnp.zeros((8, D_XBC), F32)
            for s in range(nsub):
                prod = dc_ref[s * sub:(s + 1) * sub, :] * x_rows(s, k - 2)
                for r in range(0, sub, 8):
                    part = part + prod[r:r + 8, :]
            acc_ref[k:k + 1, :] += jnp.sum(part, axis=0, keepdims=True)
        ddt = ddt_ref[0] + pltpu.roll(ddt_ref[1], SSD_HEADS, 1)
        dxd_ref[:, D_XBC:D_XD] = ddt.astype(BF16)

    cur = pl.BlockSpec((tb, D_XBC), lambda i: (i, 0))
    return pl.pallas_call(
        body, name=name, grid=(nb,),
        in_specs=[prev, cur, nxt, prev, cur, nxt,
                  pl.BlockSpec((2, tb, 128), lambda i: (0, i, 0)), pl.BlockSpec((8, D_XBC), lambda i: (0, 0))],
        out_specs=[_row_spec(tb, D_XD), _acc_spec(D_XBC)],
        out_shape=[jax.ShapeDtypeStruct((n, D_XD), BF16), jax.ShapeDtypeStruct((8, D_XBC), F32)],
        scratch_shapes=_halo_scratch() + _halo_scratch(),
        compiler_params=_cp(("arbitrary",)),
    )(du, du, du, xd, xd, xd, ddt, w8)


def _ssd_chunk_index(nc, reverse):
    def idx(d, k):
        kk = (nc - 1 - k) if reverse else k
        return kk + d * (nc - 1 - 2 * kk)
    return idx


def _ssd_prologue(d, u_ref, xd_ref, bias_ref, a_ref, r_ref):
    q = CHUNK
    xbc = _silu(u_ref[...].astype(F32))
    xs = xbc[:, 0:D_MODEL]
    bm = xbc[:, D_MODEL:D_MODEL + D_STATE]
    cm = xbc[:, D_MODEL + D_STATE:D_XBC]
    row = lax.broadcasted_iota(jnp.int32, (q, q), 0)
    col = lax.broadcasted_iota(jnp.int32, (q, q), 1)
    sgn = 1 - 2 * d
    mask = ((row - col) * sgn) >= 0
    mask_t = ((row - col) * sgn) <= 0
    xdv = xd_ref[...]
    dtraw = jnp.where(d == 0, xdv, pltpu.roll(xdv, 128 - SSD_HEADS, 1)) + bias_ref[...]
    head_lane = col < SSD_HEADS
    dt = jnp.where(head_lane, _softplus(dtraw), 0.0)
    a = a_ref[...]
    tri = jnp.where(mask, 1.0, 0.0).astype(BF16)
    acum = _dot_sel_l(tri, dt * a)
    rexp = r_ref[...]
    alast = jnp.where(d == 0, acum[q - 1:q, :], acum[0:1, :])
    e16 = jnp.exp(acum)
    dend16 = jnp.exp(alast - acum)
    wend16 = dend16 * dt
    e = _dot_sel_r(e16, rexp, n=1)
    wend_e = _dot_sel_r(wend16, rexp, n=1)
    elast_e = _dot_sel_r(jnp.broadcast_to(jnp.exp(alast), (8, 128)), rexp, n=2)[0:1, :]
    g = _dot_nt(cm.astype(BF16), bm.astype(BF16))
    return dict(xs=xs, bm=bm, cm=cm, mask=mask, mask_t=mask_t, dtraw=dtraw, head_lane=head_lane, dt=dt, a=a,
                acum=acum, acum_t=acum.T, dt_t=dt.T, e16=e16, dend16=dend16, wend16=wend16, e=e, wend_e=wend_e,
                elast_e=elast_e, g=g, col=col, row=row)


def _ssd_head_mats(p, h):
    seg = p["acum"][:, h:h + 1] - p["acum_t"][h:h + 1, :]
    lm = jnp.exp(jnp.where(p["mask"], seg, -jnp.inf))
    gl = p["g"] * lm
    s = gl * p["dt_t"][h:h + 1, :]
    return lm, gl, s


def _ssd_fwd(u, xd, bias2, a2, rexp, h0, name, gather=()):
    n = u.shape[0]
    nc = n // CHUNK
    q = CHUNK
    cidx = _ssd_chunk_index(nc, reverse=False)
    ng = len(gather)

    def body(u_ref, xd_ref, bias_ref, a_ref, r_ref, h0_ref, *rest):
        g_ins, (y_ref, hp_ref, hf_ref), rest = rest[:ng], rest[ng:ng + 3], rest[ng + 3:]
        g_outs, st, sems = rest[:ng], rest[ng], rest[ng + 1:]
        d = pl.program_id(0)
        k = pl.program_id(1)
        if ng:
            g_start, g_finish = _gather_steps(g_ins, g_outs, *sems)
            pl.when((d == 0) & (k == 0))(g_start)

        @pl.when(k == 0)
        def _():
            st[...] = h0_ref[...]

        p = _ssd_prologue(d, u_ref, xd_ref, bias_ref, a_ref, r_ref)
        stv = st[...]
        st_bf = stv.astype(BF16)
        hp_ref[...] = st_bf
        xs = p["xs"]
        lane128 = p["col"]
        y_off = _dot(p["cm"].astype(BF16), st_bf) * p["e"]
        for pb in range(SSD_HEADS // 2):
            _, _, s0 = _ssd_head_mats(p, 2 * pb)
            _, _, s1 = _ssd_head_mats(p, 2 * pb + 1)
            xp = xs[:, pb * 128:(pb + 1) * 128]
            rhs = jnp.concatenate([jnp.where(lane128 < HEAD_DIM, xp, 0.0), jnp.where(lane128 >= HEAD_DIM, xp, 0.0)],
                                  axis=0).astype(BF16)
            lhs = jnp.concatenate([s0, s1], axis=1).astype(BF16)
            y_ref[:, pb * 128:(pb + 1) * 128] = (_dot(lhs, rhs) + y_off[:, pb * 128:(pb + 1) * 128]).astype(BF16)
        xw = (xs * p["wend_e"]).astype(BF16)
        new = stv * p["elast_e"] + _dot(p["bm"].T.astype(BF16), xw)
        st[...] = new
        hf_ref[...] = new
        if ng:
            pl.when((d == 1) & (k == nc - 1))(g_finish)

    nsem = _GATHER_SEMS * ng
    return pl.pallas_call(
        body, name=name, grid=(2, nc),
        in_specs=[pl.BlockSpec((q, D_XBC), lambda d, k: (cidx(d, k), 0)),
                  pl.BlockSpec((q, 128), lambda d, k: (cidx(d, k), D_XBC // 128)),
                  pl.BlockSpec((None, 1, 128), lambda d, k: (d, 0, 0)),
                  pl.BlockSpec((None, 1, 128), lambda d, k: (d, 0, 0)),
                  pl.BlockSpec((128, D_MODEL), lambda d, k: (0, 0)),
                  pl.BlockSpec((None, D_STATE, D_MODEL), lambda d, k: (d, 0, 0))] + [_ANY] * ng,
        out_specs=[pl.BlockSpec((None, q, D_MODEL), lambda d, k: (d, cidx(d, k), 0)),
                   pl.BlockSpec((None, None, D_STATE, D_MODEL), lambda d, k: (d, cidx(d, k), 0, 0)),
                   pl.BlockSpec((None, D_STATE, D_MODEL), lambda d, k: (d, 0, 0))] + [_ANY] * ng,
        out_shape=[jax.ShapeDtypeStruct((2, n, D_MODEL), BF16),
                   jax.ShapeDtypeStruct((2, nc, D_STATE, D_MODEL), BF16),
                   jax.ShapeDtypeStruct((2, D_STATE, D_MODEL), F32)] +
                  [jax.ShapeDtypeStruct((4,) + t.shape, t.dtype) for t in gather],
        scratch_shapes=[pltpu.VMEM((D_STATE, D_MODEL), F32)] +
                       ([pltpu.SemaphoreType.DMA((nsem,)), pltpu.SemaphoreType.DMA((nsem,))] if ng else []),
        compiler_params=_cp(("arbitrary", "arbitrary")),
    )(u, xd, bias2, a2, rexp, h0, *gather)


def _ssd_bwd(u, xd, bias2, a2, rexp, rexp_t, dy, hprev, lam0, name, exchange=()):
    n = u.shape[0]
    nc = n // CHUNK
    q = CHUNK
    cidx = _ssd_chunk_index(nc, reverse=True)
    ne = len(exchange)

    def body(u_ref, xd_ref, bias_ref, a_ref, r_ref, rt_ref, dy_ref, hp_ref, lam0_ref, *rest):
        e_ins, (dxs_ref, dbc_ref, ddt_ref, acc_ref, lamo_ref), rest = rest[:ne], rest[ne:ne + 5], rest[ne + 5:]
        e_outs, lam, sems = rest[:ne], rest[ne], rest[ne + 1:]
        d = pl.program_id(0)
        k = pl.program_id(1)
        if ne:
            e_start, e_finish = _exchange_steps(e_ins, e_outs, *sems)
            pl.when((d == 0) & (k == 0))(e_start)

        @pl.when(k == 0)
        def _():
            lam[...] = lam0_ref[...]
            acc_ref[...] = jnp.zeros_like(acc_ref)

        rexp_t = rt_ref[...]

        def hsum(t):
            return _dot_sel_r(t, rexp_t, n=1)

        p = _ssd_prologue(d, u_ref, xd_ref, bias_ref, a_ref, r_ref)
        xs, bm, cm = p["xs"], p["bm"], p["cm"]
        bm_bf, cm_bf = bm.astype(BF16), cm.astype(BF16)
        lamn = lam[...]
        lamn_bf = lamn.astype(BF16)
        stp = hp_ref[...]
        dyv = dy_ref[...].astype(F32)
        lane128 = p["col"]

        wend_e = p["wend_e"]
        cs = _dot(cm_bf, stp)
        dye_bf = (dyv * p["e"]).astype(BF16)
        dc_off = _dot_nt(dye_bf, stp)
        v = _dot(bm_bf, lamn_bf)
        xw_bf = (xs * wend_e).astype(BF16)
        db_off = _dot_nt(xw_bf, lamn_bf)
        elast_e = p["elast_e"]
        dlast_e = jnp.sum(stp.astype(F32) * lamn, axis=0, keepdims=True) * elast_e
        lam_new = lamn * elast_e + _dot(cm.T.astype(BF16), dye_bf)
        lam[...] = lam_new
        lamo_ref[...] = lam_new

        hs_vx = hsum(v * xs)
        om = p["wend16"] * hs_vx
        x1 = p["e16"] * hsum(dyv * cs) - om
        x2 = p["dend16"] * hs_vx
        x3 = jnp.sum(om, axis=0, keepdims=True) + _dot_sel_r(jnp.broadcast_to(dlast_e, (8, D_MODEL)), rexp_t, n=2)[0:1, :]

        sub16 = lax.broadcasted_iota(jnp.int32, (SSD_HEADS, q), 0)
        rs = jnp.zeros((q, 128), F32)
        cs_m = jnp.zeros((SSD_HEADS, q), F32)
        dt_m = jnp.zeros((SSD_HEADS, q), F32)
        dg = jnp.zeros((q, q), F32)
        for pb in range(SSD_HEADS // 2):
            xp_bf = xs[:, pb * 128:(pb + 1) * 128].astype(BF16)
            dyp = dyv[:, pb * 128:(pb + 1) * 128]
            dxs_pair = None
            for half in range(2):
                h = 2 * pb + half
                sel = (lane128 < HEAD_DIM) if half == 0 else (lane128 >= HEAD_DIM)
                dyh_bf = jnp.where(sel, dyp, 0.0).astype(BF16)
                lm, gl, s = _ssd_head_mats(p, h)
                ds = _dot_nt(dyh_bf, xp_bf)
                t = _dot_tn(s.astype(BF16), dyh_bf)
                dxs_pair = t if dxs_pair is None else dxs_pair + t
                w = ds * s
                rs = rs + jnp.sum(w, axis=1, keepdims=True) * (lane128 == h).astype(F32)
                cs_m = jnp.where(sub16 == h, jnp.sum(w, axis=0, keepdims=True), cs_m)
                dt_m = jnp.where(sub16 == h, jnp.sum(ds * gl, axis=0, keepdims=True), dt_m)
                dg = dg + ds * lm * p["dt_t"][h:h + 1, :]
            sl = slice(pb * 128, (pb + 1) * 128)
            dxs_ref[:, sl] = (dxs_pair + v[:, sl] * wend_e[:, sl]).astype(BF16)

        def to_lanes(m16):
            return jnp.concatenate([m16, jnp.zeros((128 - SSD_HEADS, q), F32)], axis=0).T

        last = jnp.where(d == 0, q - 1, 0)
        dacum = rs - to_lanes(cs_m) + x1 + jnp.where(p["row"] == last, x3[0:1, :], 0.0)
        tri_t = jnp.where(p["mask_t"], 1.0, 0.0).astype(BF16)
        ddta = _dot_sel_l(tri_t, dacum)
        dt = p["dt"]
        a = p["a"]
        ddt = to_lanes(dt_m) + x2 + a * ddta
        ddtraw = jnp.where(p["head_lane"], ddt * _sigmoid(p["dtraw"]), 0.0)
        ddt_ref[...] = ddtraw
        acc_ref[0:1, :] += jnp.sum(ddtraw, axis=0, keepdims=True)
        acc_ref[1:2, :] += jnp.sum(dt * ddta, axis=0, keepdims=True) * a

        dg_bf = dg.astype(BF16)
        dbc_ref[:, 0:D_STATE] = _dot_tn(dg_bf, cm_bf) + db_off
        dbc_ref[:, D_STATE:2 * D_STATE] = _dot(dg_bf, bm_bf) + dc_off
        if ne:
            pl.when((d == 1) & (k == nc - 1))(e_finish)

    cblk = lambda d, k: (cidx(d, k), 0)
    return pl.pallas_call(
        body, name=name, grid=(2, nc),
        in_specs=[pl.BlockSpec((q, D_XBC), cblk),
                  pl.BlockSpec((q, 128), lambda d, k: (cidx(d, k), D_XBC // 128)),
                  pl.BlockSpec((None, 1, 128), lambda d, k: (d, 0, 0)),
                  pl.BlockSpec((None, 1, 128), lambda d, k: (d, 0, 0)),
                  pl.BlockSpec((128, D_MODEL), lambda d, k: (0, 0)),
                  pl.BlockSpec((D_MODEL, 128), lambda d, k: (0, 0)),
                  pl.BlockSpec((q, D_MODEL), cblk),
                  pl.BlockSpec((None, None, D_STATE, D_MODEL), lambda d, k: (d, cidx(d, k), 0, 0)),
                  pl.BlockSpec((None, D_STATE, D_MODEL), lambda d, k: (d, 0, 0))] + [_ANY] * ne,
        out_specs=[pl.BlockSpec((None, q, D_MODEL), lambda d, k: (d, cidx(d, k), 0)),
                   pl.BlockSpec((None, q, 2 * D_STATE), lambda d, k: (d, cidx(d, k), 0)),
                   pl.BlockSpec((None, q, 128), lambda d, k: (d, cidx(d, k), 0)),
                   pl.BlockSpec((None, 8, 128), lambda d, k: (d, 0, 0)),
                   pl.BlockSpec((None, D_STATE, D_MODEL), lambda d, k: (d, 0, 0))] + [_ANY] * ne,
        out_shape=[jax.ShapeDtypeStruct((2, n, D_MODEL), BF16),
                   jax.ShapeDtypeStruct((2, n, 2 * D_STATE), F32),
                   jax.ShapeDtypeStruct((2, n, 128), F32),
                   jax.ShapeDtypeStruct((2, 8, 128), F32),
                   jax.ShapeDtypeStruct((2, D_STATE, D_MODEL), F32)] +
                  [jax.ShapeDtypeStruct(t.shape, t.dtype) for t in exchange],
        scratch_shapes=[pltpu.VMEM((D_STATE, D_MODEL), F32)] +
                       ([pltpu.SemaphoreType.DMA((3 * ne,)), pltpu.SemaphoreType.DMA((3 * ne,)),
                         pltpu.SemaphoreType.DMA((ne,))] if ne else []),
        compiler_params=_cp(("arbitrary", "arbitrary")),
    )(u, xd, bias2, a2, rexp, rexp_t, dy, hprev, lam0, *exchange)


def _merge_fwd(y, u, z, dskip_e, gn, tb):
    n = z.shape[0]

    def body(y_ref, u_ref, z_ref, sk_ref, gn_ref, o_ref):
        sk, gnv = sk_ref[...], gn_ref[...]
        for r in _row_tiles(tb):
            ys = y_ref[0, r, :].astype(F32) + y_ref[1, r, :].astype(F32) + sk * _silu(u_ref[r, :].astype(F32))
            gated = ys * _silu(z_ref[r, :].astype(F32))
            rstd = lax.rsqrt(jnp.mean(gated * gated, axis=-1, keepdims=True) + LN_EPS)
            o_ref[r, :] = (gated * rstd * gnv).astype(BF16)

    return pl.pallas_call(
        body, name="merge_fwd", grid=(n // tb,),
        in_specs=[pl.BlockSpec((2, tb, D_MODEL), lambda i: (0, i, 0)), pl.BlockSpec((tb, D_MODEL), lambda i: (i, 0)),
                  _row_spec(tb, D_MODEL), _par_spec(D_MODEL), _par_spec(D_MODEL)],
        out_specs=_row_spec(tb, D_MODEL),
        out_shape=jax.ShapeDtypeStruct((n, D_MODEL), BF16),
        compiler_params=_cp(("parallel",)),
    )(y, u, z, dskip_e, gn)


def _merge_bwd(dyn, y, u, z, dskip_e, gn, tb):
    n = z.shape[0]

    def body(dyn_ref, y_ref, u_ref, z_ref, sk_ref, gn_ref, dy_ref, dz_ref, acc_ref):
        i = pl.program_id(0)

        @pl.when(i == 0)
        def _():
            acc_ref[...] = jnp.zeros_like(acc_ref)

        sk, gnv = sk_ref[...], gn_ref[...]
        part0 = jnp.zeros((8, D_MODEL), F32)
        part1 = jnp.zeros((8, D_MODEL), F32)
        for s in range(tb // ROW_SUB):
            r = slice(s * ROW_SUB, (s + 1) * ROW_SUB)
            xs = _silu(u_ref[r, :].astype(F32))
            zv = z_ref[r, :].astype(F32)
            sz = _sigmoid(zv)
            ys = y_ref[0, r, :].astype(F32) + y_ref[1, r, :].astype(F32) + sk * xs
            gated = ys * (zv * sz)
            rstd = lax.rsqrt(jnp.mean(gated * gated, axis=-1, keepdims=True) + LN_EPS)
            ghat = gated * rstd
            dyn_v = dyn_ref[r, :].astype(F32)
            t = dyn_v * gnv
            dgated = rstd * (t - ghat * jnp.mean(t * ghat, axis=-1, keepdims=True))
            dys = dgated * (zv * sz)
            dy_ref[r, :] = dys.astype(BF16)
            dz_ref[r, :] = (dgated * ys * (sz * (1.0 + zv * (1.0 - sz)))).astype(BF16)
            part0 = part0 + _fold8(dyn_v * ghat)
            part1 = part1 + _fold8(dys * xs)
        acc_ref[0:1, :] += jnp.sum(part0, axis=0, keepdims=True)
        acc_ref[1:2, :] += jnp.sum(part1, axis=0, keepdims=True)

    return pl.pallas_call(
        body, name="merge_bwd", grid=(n // tb,),
        in_specs=[_row_spec(tb, D_MODEL), pl.BlockSpec((2, tb, D_MODEL), lambda i: (0, i, 0)),
                  pl.BlockSpec((tb, D_MODEL), lambda i: (i, 0)), _row_spec(tb, D_MODEL),
                  _par_spec(D_MODEL), _par_spec(D_MODEL)],
        out_specs=[_row_spec(tb, D_MODEL), _row_spec(tb, D_MODEL), _acc_spec(D_MODEL)],
        out_shape=[jax.ShapeDtypeStruct((n, D_MODEL), BF16), jax.ShapeDtypeStruct((n, D_MODEL), BF16),
                   jax.ShapeDtypeStruct((8, D_MODEL), F32)],
        compiler_params=_cp(("arbitrary",)),
    )(dyn, y, u, z, dskip_e, gn)


def _pool_consts(transpose):
    tb = POOL_TB
    t = jnp.arange(tb)
    s = jnp.arange(3 * tb)
    rl, cl = t // GRID_W, t % GRID_W
    rs_, cs_ = s // GRID_W - tb // GRID_W, s % GRID_W
    s2 = jnp.arange(tb)
    rl2, cl2 = s2 // GRID_W, s2 % GRID_W
    brow, bcol = [], []
    for w in POOL_WINDOWS:
        lo, hi = -(w // 2), w - w // 2
        if transpose:
            lo, hi = -hi + 1, -lo + 1
        dr = rs_[None, :] - rl[:, None]
        before, after = _pool_halo(w, transpose)
        full = ((cs_[None, :] == cl[:, None]) & (dr >= lo) & (dr < hi)).astype(BF16)
        brow.append(full[:, tb - before:2 * tb + after])
        dc = cl2[None, :] - cl[:, None]
        bcol.append(((rl2[None, :] == rl[:, None]) & (dc >= lo) & (dc < hi)).astype(BF16))
    return brow, jnp.stack(bcol)


def _pool_halo(w, transpose):
    lo, hi = -(w // 2), w - w // 2
    if transpose:
        lo, hi = -hi + 1, -lo + 1
    return -lo * GRID_W, (hi - 1) * GRID_W


def _pool_inv(i, g, n):
    assert GRID_W == 64
    t = i * POOL_TB + lax.broadcasted_iota(jnp.int32, (POOL_TB, 1), 0)
    r = lax.shift_right_logical(t, 6)
    col = t & (GRID_W - 1)
    w = POOL_WINDOWS[g]
    lo, hi = -(w // 2), w - w // 2
    cnt_r = jnp.minimum(r + hi, n // GRID_W) - jnp.maximum(r + lo, 0)
    cnt_c = jnp.minimum(col + hi, GRID_W) - jnp.maximum(col + lo, 0)
    return 1.0 / (cnt_r * cnt_c).astype(F32)


def _pool_box(prev_ref, cur_ref, next_ref, brow_refs, bcol_ref, g, i, nb, transpose):
    tb = POOL_TB
    sl = slice(g * POOL_DIM, (g + 1) * POOL_DIM)
    before, after = _pool_halo(POOL_WINDOWS[g], transpose)
    pieces = []
    if before:
        pieces.append((prev_ref[tb - before:tb, sl] * (i > 0).astype(prev_ref.dtype)).astype(BF16))
    pieces.append(cur_ref[:, sl].astype(BF16))
    if after:
        pieces.append((next_ref[0:after, sl] * (i < nb - 1).astype(next_ref.dtype)).astype(BF16))
    r = _dot(brow_refs[g][...], jnp.concatenate(pieces, axis=0))
    return _dot(bcol_ref[g], r.astype(BF16))


def _pool_halo_specs(n, d):
    tb = POOL_TB
    nb = n // tb
    prev = pl.BlockSpec((tb, d), lambda i: (jnp.maximum(i - 1, 0), 0))
    cur = pl.BlockSpec((tb, d), lambda i: (i, 0))
    nxt = pl.BlockSpec((tb, d), lambda i: (jnp.minimum(i + 1, nb - 1), 0))
    return prev, cur, nxt


def _pool_const_specs(brow):
    tb = POOL_TB
    return [pl.BlockSpec(b.shape, lambda i: (0, 0)) for b in brow] + [pl.BlockSpec((N_POOL, tb, tb), lambda i: (0, 0, 0))]


def _pool_fwd(up, consts, pw_bf, pscale):
    n = up.shape[0]
    tb = POOL_TB
    nb = n // tb
    brow, bcol = consts
    prev, cur, nxt = _pool_halo_specs(n, D_MODEL)

    def body(p_ref, c_ref, n_ref, *rest):
        brow_refs, (bcol_ref, pw_ref, sc_ref, o_ref, d_ref) = rest[:N_POOL], rest[N_POOL:]
        i = pl.program_id(0)
        for g in range(N_POOL):
            sl = slice(g * POOL_DIM, (g + 1) * POOL_DIM)
            box = _pool_box(p_ref, c_ref, n_ref, brow_refs, bcol_ref, g, i, nb, False)
            dd = (box * _pool_inv(i, g, n) - c_ref[:, sl].astype(F32)).astype(BF16)
            d_ref[:, sl] = dd
            o_ref[:, sl] = (_dot(dd, pw_ref[g]) * sc_ref[:, sl]).astype(BF16)

    return pl.pallas_call(
        body, name="pool_fwd", grid=(nb,),
        in_specs=[prev, cur, nxt] + _pool_const_specs(brow) +
                 [pl.BlockSpec((N_POOL, POOL_DIM, POOL_DIM), lambda i: (0, 0, 0)), _par_spec(D_MODEL)],
        out_specs=[_row_spec(tb, D_MODEL), _row_spec(tb, D_MODEL)],
        out_shape=[jax.ShapeDtypeStruct((n, D_MODEL), BF16), jax.ShapeDtypeStruct((n, D_MODEL), BF16)],
        compiler_params=_cp(("parallel",)),
    )(up, up, up, *brow, bcol, pw_bf, pscale)


def _pool_bwd_a(dp, dsave, pw_bf, pwt_bf, pscale):
    n = dp.shape[0]
    tb = POOL_TB

    def body(dp_ref, d_ref, pw_ref, pwt_ref, sc_ref, dd_ref, dds_ref, gw_ref, gs_ref):
        i = pl.program_id(0)

        @pl.when(i == 0)
        def _():
            gw_ref[...] = jnp.zeros_like(gw_ref)
            gs_ref[...] = jnp.zeros_like(gs_ref)

        for g in range(N_POOL):
            sl = slice(g * POOL_DIM, (g + 1) * POOL_DIM)
            dpv = dp_ref[:, sl].astype(F32)
            dv = d_ref[:, sl]
            dpw_bf = (dpv * sc_ref[:, sl]).astype(BF16)
            dd = _dot(dpw_bf, pwt_ref[g])
            dd_ref[:, sl] = dd.astype(BF16)
            dds_ref[:, sl] = (dd * _pool_inv(i, g, n)).astype(BF16)
            gw_ref[g] += _dot_tn(dv, dpw_bf)
            gs_ref[0:1, sl] += jnp.sum(dpv * _dot(dv, pw_ref[g]), axis=0, keepdims=True)

    wspec = pl.BlockSpec((N_POOL, POOL_DIM, POOL_DIM), lambda i: (0, 0, 0))
    return pl.pallas_call(
        body, name="pool_bwd_a", grid=(n // tb,),
        in_specs=[_row_spec(tb, D_MODEL), _row_spec(tb, D_MODEL), wspec, wspec, _par_spec(D_MODEL)],
        out_specs=[_row_spec(tb, D_MODEL), _row_spec(tb, D_MODEL), wspec, _acc_spec(D_MODEL)],
        out_shape=[jax.ShapeDtypeStruct((n, D_MODEL), BF16), jax.ShapeDtypeStruct((n, D_MODEL), BF16),
                   jax.ShapeDtypeStruct((N_POOL, POOL_DIM, POOL_DIM), F32), jax.ShapeDtypeStruct((8, D_MODEL), F32)],
        compiler_params=_cp(("arbitrary",)),
    )(dp, dsave, pw_bf, pwt_bf, pscale)


def _pool_bwd_b(dds, dd, consts_t):
    n = dd.shape[0]
    tb = POOL_TB
    nb = n // tb
    brow, bcol = consts_t
    prev, cur, nxt = _pool_halo_specs(n, D_MODEL)

    def body(p_ref, c_ref, n_ref, *rest):
        brow_refs, (bcol_ref, dd_ref, o_ref) = rest[:N_POOL], rest[N_POOL:]
        i = pl.program_id(0)
        for g in range(N_POOL):
            sl = slice(g * POOL_DIM, (g + 1) * POOL_DIM)
            box = _pool_box(p_ref, c_ref, n_ref, brow_refs, bcol_ref, g, i, nb, True)
            o_ref[:, sl] = (box - dd_ref[:, sl].astype(F32)).astype(BF16)

    return pl.pallas_call(
        body, name="pool_bwd_b", grid=(nb,),
        in_specs=[prev, cur, nxt] + _pool_const_specs(brow) + [_row_spec(tb, D_MODEL)],
        out_specs=_row_spec(tb, D_MODEL),
        out_shape=jax.ShapeDtypeStruct((n, D_MODEL), BF16),
        compiler_params=_cp(("parallel",)),
    )(dds, dds, dds, *brow, bcol, dd)


def _pair_add(slabs, recvs, core, name):
    na = len(slabs)
    hr = [t.shape[1] // 4 for t in slabs]

    def body(core_ref, *refs):
        for a in range(na):
            refs[2 * na + a][...] = (refs[a][...] + refs[na + a][...]).astype(BF16)

    own = [pl.BlockSpec((None, hr[a], slabs[a].shape[2]), lambda j, i, c_ref: (j, 2 * c_ref[0] + i, 0)) for a in range(na)]
    got = [pl.BlockSpec((None, hr[a], slabs[a].shape[2]), lambda j, i, c_ref: (j, i, 0)) for a in range(na)]
    return pl.pallas_call(
        body, name=name,
        grid_spec=pltpu.PrefetchScalarGridSpec(num_scalar_prefetch=1, grid=(4, 2), in_specs=own + got, out_specs=got),
        out_shape=[jax.ShapeDtypeStruct(r.shape, BF16) for r in recvs],
        compiler_params=_cp(("arbitrary", "arbitrary")),
    )(core, *slabs, *recvs)


def _sum4(parts, core):
    na = len(parts)
    hr = [t.shape[1] // 2 for t in parts]

    def body(core_ref, *refs):
        for a in range(na):
            p = refs[a]
            refs[na + a][...] = ((p[0].astype(F32) + p[1].astype(F32)) + p[2].astype(F32)) + p[3].astype(F32)

    return pl.pallas_call(
        body, name="reduce_g_sum",
        grid_spec=pltpu.PrefetchScalarGridSpec(
            num_scalar_prefetch=1, grid=(2,),
            in_specs=[pl.BlockSpec((4, hr[a], parts[a].shape[2]), lambda i, c_ref: (0, i, 0)) for a in range(na)],
            out_specs=[pl.BlockSpec((hr[a], parts[a].shape[2]), lambda i, c_ref: (2 * c_ref[0] + i, 0))
                       for a in range(na)]),
        out_shape=[jax.ShapeDtypeStruct((2 * t.shape[1], t.shape[2]), F32) for t in parts],
        compiler_params=_cp(("arbitrary",)),
    )(core, *parts)


def _adamw(w, g, m, v, name):
    r, cdim = w.shape
    tb = _row_block(r, 256)
    c1 = 1.0 - ADAM_B1 ** ADAM_STEP
    c2 = 1.0 - ADAM_B2 ** ADAM_STEP

    def body(w_ref, g_ref, m_ref, v_ref, d_ref, nm_ref, nv_ref):
        gv = g_ref[...]
        nm = ADAM_B1 * m_ref[...] + (1.0 - ADAM_B1) * gv
        nv = ADAM_B2 * v_ref[...] + (1.0 - ADAM_B2) * (gv * gv)
        m_hat = nm / c1
        v_hat = nv / c2
        d_ref[...] = -ADAM_LR * (m_hat / (jnp.sqrt(v_hat) + ADAM_EPS) + ADAM_WD * w_ref[...])
        nm_ref[...] = nm
        nv_ref[...] = nv

    spec = _row_spec(tb, cdim)
    shp = jax.ShapeDtypeStruct((r, cdim), F32)
    return pl.pallas_call(
        body, name=name, grid=(r // tb,),
        in_specs=[spec] * 4, out_specs=[spec] * 3, out_shape=[shp] * 3,
        compiler_params=_cp(("parallel",)),
    )(w, g, m, v)


def _mesh_pos():
    return lax.axis_index("x"), lax.axis_index("y"), lax.axis_index("c")


_ANY = pl.BlockSpec(memory_space=pl.ANY)


def _remote(src, dst, send_sem, recv_sem, device):
    return pltpu.make_async_remote_copy(src_ref=src, dst_ref=dst, send_sem=send_sem, recv_sem=recv_sem,
                                        device_id=device, device_id_type=MESH)


def _other_chips(x, y):
    return [(1 - x, y), (x, 1 - y), (1 - x, 1 - y)]


def _half(nrows, h):
    return pl.ds(h * (nrows // 2), nrows // 2)


_GATHER_SEMS = 7


def _gather_steps(ins, outs, send_sems, recv_sems):
    na = len(ins)
    nrow = [r.shape[0] for r in ins]

    def copies():
        x, y, c = _mesh_pos()
        me = 2 * x + y
        sib = (x, y, 1 - c)
        chips = _other_chips(x, y)

        def ici(k, a, slot):
            px, py = chips[k]
            rows = _half(nrow[a], c)
            return _remote(ins[a].at[rows, :], outs[a].at[slot, rows, :], send_sems.at[k * na + a],
                           recv_sems.at[k * na + a], (px, py, c))

        def fwd(k, a, h):
            px, py = chips[k]
            blk = outs[a].at[2 * px + py, _half(nrow[a], h), :]
            return _remote(blk, blk, send_sems.at[(3 + k) * na + a], recv_sems.at[(3 + k) * na + a], sib)

        def own(a):
            return _remote(ins[a], outs[a].at[me], send_sems.at[6 * na + a], recv_sems.at[6 * na + a], sib)

        slots = [2 * px + py for px, py in chips]
        return ici, fwd, own, me, c, slots

    def start():
        ici, _, own, me, _, _ = copies()
        for a in range(na):
            own(a).start()
        for k in range(3):
            for a in range(na):
                ici(k, a, me).start()

    def finish():
        ici, fwd, own, me, c, slots = copies()
        for k in range(3):
            for a in range(na):
                ici(k, a, slots[k]).wait_recv()
                fwd(k, a, c).start()
        for k in range(3):
            for a in range(na):
                fwd(k, a, 1 - c).wait_recv()
        for a in range(na):
            own(a).wait_recv()
        for a in range(na):
            own(a).wait_send()
        for k in range(3):
            for a in range(na):
                ici(k, a, me).wait_send()
                fwd(k, a, c).wait_send()

    return start, finish


def _exchange_steps(ins, outs, send_sems, recv_sems, local_sems):
    na = len(ins)

    def copies():
        x, y, c = _mesh_pos()
        me = 2 * x + y
        chips = _other_chips(x, y)

        def copy(k, a, slot):
            px, py = chips[k]
            return _remote(ins[a].at[2 * px + py], outs[a].at[slot], send_sems.at[k * na + a], recv_sems.at[k * na + a],
                           (px, py, c))

        def local(a):
            return pltpu.make_async_copy(ins[a].at[me], outs[a].at[me], local_sems.at[a])

        return copy, local, me, [2 * px + py for px, py in chips]

    def start():
        copy, local, me, _ = copies()
        for a in range(na):
            local(a).start()
        for k in range(3):
            for a in range(na):
                copy(k, a, me).start()

    def finish():
        copy, local, me, slots = copies()
        for k in range(3):
            for a in range(na):
                copy(k, a, slots[k]).wait_recv()
        for k in range(3):
            for a in range(na):
                copy(k, a, me).wait_send()
        for a in range(na):
            local(a).wait()

    return start, finish


def _gather_weights(shards, conv8):
    na = len(shards)

    def body(*refs):
        ins, conv_in = refs[:na], refs[na]
        outs, conv_out = refs[na + 1:2 * na + 1], refs[2 * na + 1]
        send_sems, recv_sems, local_sems = refs[2 * na + 2:]
        x, y, c = _mesh_pos()
        me = 2 * x + y
        chips = _other_chips(x, y)

        def conv(k, slot):
            px, py = chips[k]
            return _remote(conv_in, conv_out.at[slot], send_sems.at[7 * na + k], recv_sems.at[7 * na + k], (px, py, c))

        start, finish = _gather_steps(ins, outs, send_sems, recv_sems)
        local = pltpu.make_async_copy(conv_in, conv_out.at[me], local_sems.at[0])
        local.start()
        start()
        sends = [conv(k, me) for k in range(3)]
        for cp in sends:
            cp.start()
        finish()
        for k in range(3):
            px, py = chips[k]
            conv(k, 2 * px + py).wait_recv()
        for cp in sends:
            cp.wait_send()
        local.wait()

    nsem = _GATHER_SEMS * na + 3
    return pl.pallas_call(
        body, name="gather_w", in_specs=[_ANY] * (na + 1), out_specs=[_ANY] * (na + 1),
        out_shape=[jax.ShapeDtypeStruct((4,) + t.shape, t.dtype) for t in shards] +
                  [jax.ShapeDtypeStruct((4,) + conv8.shape, conv8.dtype)],
        scratch_shapes=[pltpu.SemaphoreType.DMA((nsem,)), pltpu.SemaphoreType.DMA((nsem,)),
                        pltpu.SemaphoreType.DMA((1,))],
    )(*shards, conv8)


def _pair_swap(slabs, name):
    na = len(slabs)

    def body(*refs):
        ins, outs = refs[:na], refs[na:2 * na]
        send_sems, recv_sems = refs[2 * na:]
        x, y, c = _mesh_pos()
        cps = [_remote(ins[a].at[:, _half(slabs[a].shape[1], 1 - c), :], outs[a], send_sems.at[a], recv_sems.at[a],
                       (x, y, 1 - c)) for a in range(na)]
        for cp in cps:
            cp.start()
        for cp in cps:
            cp.wait()

    return pl.pallas_call(
        body, name=name, in_specs=[_ANY] * na, out_specs=[_ANY] * na,
        out_shape=[jax.ShapeDtypeStruct((4, t.shape[1] // 2, t.shape[2]), t.dtype) for t in slabs],
        scratch_shapes=[pltpu.SemaphoreType.DMA((na,)), pltpu.SemaphoreType.DMA((na,))],
    )(*slabs)


def _chip_exchange(pairs):
    na = len(pairs)

    def body(*refs):
        start, finish = _exchange_steps(refs[:na], refs[na:2 * na], *refs[2 * na:])
        start()
        finish()

    return pl.pallas_call(
        body, name="reduce_g_ici", in_specs=[_ANY] * na, out_specs=[_ANY] * na,
        out_shape=[jax.ShapeDtypeStruct(t.shape, t.dtype) for t in pairs],
        scratch_shapes=[pltpu.SemaphoreType.DMA((3 * na,)), pltpu.SemaphoreType.DMA((3 * na,)),
                        pltpu.SemaphoreType.DMA((na,))],
    )(*pairs)


def _share_halves(totals):
    na = len(totals)

    def body(*refs):
        bufs = refs[na:2 * na]
        send_sems, recv_sems = refs[2 * na:]
        x, y, c = _mesh_pos()

        def copy(a, h):
            blk = bufs[a].at[_half(totals[a].shape[0], h), :]
            return _remote(blk, blk, send_sems.at[a], recv_sems.at[a], (x, y, 1 - c))

        sends = [copy(a, c) for a in range(na)]
        for cp in sends:
            cp.start()
        for a in range(na):
            copy(a, 1 - c).wait_recv()
        for cp in sends:
            cp.wait_send()

    return pl.pallas_call(
        body, name="reduce_g_share", in_specs=[_ANY] * na, out_specs=[_ANY] * na,
        out_shape=[jax.ShapeDtypeStruct(t.shape, t.dtype) for t in totals],
        input_output_aliases={a: a for a in range(na)},
        scratch_shapes=[pltpu.SemaphoreType.DMA((na,)), pltpu.SemaphoreType.DMA((na,))],
    )(*totals)


def _allreduce_small(v, name):
    r, cdim = v.shape

    def body(v_ref, out_ref, buf, send_sems, recv_sems):
        x, y, c = _mesh_pos()
        me = 4 * x + 2 * y + c
        buf[me] = v_ref[...]
        rel = [(bx, by, bc) for bx in (0, 1) for by in (0, 1) for bc in (0, 1)][1:]

        def peer(b):
            bx, by, bc = b
            return ((1 - x) if bx else x, (1 - y) if by else y, (1 - c) if bc else c)

        def copy(k, slot):
            return pltpu.make_async_remote_copy(
                src_ref=v_ref, dst_ref=buf.at[slot], send_sem=send_sems.at[k], recv_sem=recv_sems.at[k],
                device_id=peer(rel[k]), device_id_type=MESH)

        sends = [copy(k, me) for k in range(7)]
        for cp in sends:
            cp.start()
        for k in range(7):
            px, py, pc = peer(rel[k])
            copy(k, 4 * px + 2 * py + pc).wait_recv()
        for cp in sends:
            cp.wait_send()
        acc = buf[0]
        for j in range(1, 8):
            acc = acc + buf[j]
        out_ref[...] = acc

    vm = pl.BlockSpec(memory_space=pltpu.VMEM)
    return pl.pallas_call(
        body, name=name, in_specs=[vm], out_specs=[vm, vm],
        out_shape=[jax.ShapeDtypeStruct((r, cdim), F32), jax.ShapeDtypeStruct((8, r, cdim), F32)],
        scratch_shapes=[pltpu.SemaphoreType.DMA((7,)), pltpu.SemaphoreType.DMA((7,))],
    )(v)


def _chip_bcast(v, name):
    def body(v_ref, out_ref, send_sems, recv_sems):
        x, y, c = _mesh_pos()
        me = 2 * x + y
        chips = _other_chips(x, y)
        out_ref[me] = v_ref[...]

        def copy(k, slot):
            px, py = chips[k]
            return _remote(v_ref, out_ref.at[slot], send_sems.at[k], recv_sems.at[k], (px, py, c))

        sends = [copy(k, me) for k in range(3)]
        for cp in sends:
            cp.start()
        for k, (px, py) in enumerate(chips):
            copy(k, 2 * px + py).wait_recv()
        for cp in sends:
            cp.wait_send()

    vm = pl.BlockSpec(memory_space=pltpu.VMEM)
    return pl.pallas_call(
        body, name=name, in_specs=[vm], out_specs=vm,
        out_shape=jax.ShapeDtypeStruct((4,) + v.shape, F32),
        scratch_shapes=[pltpu.SemaphoreType.DMA((3,)), pltpu.SemaphoreType.DMA((3,))],
    )(v)


_BIG = (("in_proj", (D_MODEL, D_IN_PROJ // 4), 1), ("w_out", (2 * D_MODEL // 4, D_MODEL), 0),
        ("w_gate", (D_MODEL, D_FF // 4), 1), ("w_up", (D_MODEL, D_FF // 4), 1), ("w_down", (D_FF // 4, D_MODEL), 0),
        ("pool_w", (N_POOL * POOL_DIM // 4, POOL_DIM), None))


def _assemble(name, t):
    _, r, c = t.shape
    axis = {n: ax for n, _, ax in _BIG}[name]
    if axis == 0:
        return t.reshape(4 * r, c)
    if axis == 1:
        return t.transpose(1, 0, 2).reshape(r, 4 * c)
    return t.reshape(4, N_POOL, POOL_DIM // 4, POOL_DIM).transpose(1, 0, 2, 3).reshape(N_POOL, POOL_DIM, POOL_DIM)


def _to_slabs(name, g):
    (r, c), axis = {n: (sh, ax) for n, sh, ax in _BIG}[name]
    if axis == 0:
        return g.reshape(4, r, c)
    if axis == 1:
        return g.reshape(r, 4, c).transpose(1, 0, 2)
    return g.reshape(N_POOL, 4, POOL_DIM // 4, POOL_DIM).transpose(1, 0, 2, 3).reshape(4, r, c)


_EARLY = ("in_proj",)
_LATE = tuple(n for n, _, _ in _BIG if n not in _EARLY)


def _reduce_grads(early_grads, late_parts, core):
    slabs = [_to_slabs(n, early_grads[n]) for n in _EARLY]
    pairs = _pair_add(slabs, _pair_swap(slabs, "reduce_g_d2d"), core, "reduce_g_pair")
    parts = dict(zip(_EARLY, _chip_exchange(pairs)), **dict(zip(_LATE, late_parts)))
    names = [n for n, _, _ in _BIG]
    totals = _sum4([parts[n] for n in names], core)
    return dict(zip(names, _share_halves(totals)))


def _pad_cols(w, n):
    return jnp.concatenate([w, jnp.zeros((w.shape[0], n - w.shape[1]), w.dtype)], axis=1)


def _device_step(x, mod, mod_ctx, ctx, target, wts, w8, small, tb, late_shards=None, core=None):
    n = x.shape[0]
    d = D_MODEL

    win = wts["in_proj"]
    wz, wxd, wup = win[:, 0:d], _pad_cols(win[:, d:d + D_XBC + 2 * SSD_HEADS], D_XD), win[:, d + D_XBC + 2 * SSD_HEADS:]

    emb_g, emb_b = _vec(small["emb_ln_g"]), _vec(small["emb_ln_b"])
    ln1_g, ln1_b = _vec(small["ln1_g"]), _vec(small["ln1_b"])
    ln2_g, ln2_b = _vec(small["ln2_g"]), _vec(small["ln2_b"])
    gn = _vec(small["ssd_norm_g"])
    pscale = _vec(small["pool_scale"])
    conv_b = _vec(small["conv_b"])
    dskip_e = jnp.repeat(small["d_skip"].reshape(-1), HEAD_DIM).reshape(1, d)
    zpad = jnp.zeros((2, 1, 128 - SSD_HEADS), F32)
    bias2 = jnp.concatenate([small["dt_bias"].reshape(2, 1, SSD_HEADS), zpad], axis=2)
    a2 = jnp.concatenate([-jnp.exp(small["a_log"].reshape(2, 1, SSD_HEADS)), zpad], axis=2)
    rexp = (jnp.arange(128)[:, None] == (jnp.arange(d)[None, :] // HEAD_DIM)).astype(BF16)
    rexp_t = rexp.T

    sh1, sc1, g1, sh2, sc2, g2 = [mod[:, i * d:(i + 1) * d] for i in range(6)]
    sh1c, sc1c = mod_ctx[:, 0:d], mod_ctx[:, d:2 * d]

    tbc = min(tb, ctx.shape[0])
    xc0, hc = _ln_mod(ctx, emb_g, emb_b, sh1c, sc1c, tbc, "ln_mod_ctx")
    xdc = _matmul_nn([(hc, wxd)], F32, 512, D_XD, "in_proj_ctx")
    uc = _conv_fwd(xdc, w8, conv_b, tbc, "conv_fwd_ctx")
    hzero = jnp.zeros((2, D_STATE, d), F32)
    _, hprev_c, hfin_c = _ssd_fwd(uc, xdc, bias2, a2, rexp, hzero, "ssd_fwd_ctx")

    x0, h1 = _ln_mod(x, emb_g, emb_b, sh1, sc1, tb, "ln_mod")
    z, xd, up = _matmul_nn_multi(h1, [wz, wxd, wup], [BF16, F32, BF16], 512, "in_proj")
    u = _conv_fwd(xd, w8, conv_b, tb, "conv_fwd")
    y, hprev, _, *landed = _ssd_fwd(u, xd, bias2, a2, rexp, hfin_c, "ssd_fwd", gather=late_shards or ())
    if late_shards is not None:
        wts = dict(wts, **{nme: _assemble(nme, t) for nme, t in zip(_LATE, landed)})
    wout = wts["w_out"]
    wg, wu, wd = wts["w_gate"], wts["w_up"], wts["w_down"]
    pw = wts["pool_w"]
    yn = _merge_fwd(y, u, z, dskip_e, gn, tb)
    pconst = _pool_consts(False)
    pool, dsave = _pool_fwd(up, pconst, pw, pscale)
    mix = _matmul_nn([(yn, wout[0:d]), (pool, wout[d:2 * d])], BF16, MM_ROWS, 1024, "out_proj")
    x1, h2 = _res_ln(x0, mix, g1, ln1_g, ln1_b, sh2, sc2, tb)

    gate, upp, hmid = _swiglu_fwd(h2, wg, wu, 512, D_FF // 2)
    ffn = _matmul_nn([(hmid, wd)], BF16, MM_ROWS, 1024, "ffn_down")
    dffn, dr2, acc2 = _final_ln_loss(x1, ffn, g2, ln2_g, ln2_b, target, tb)
    loss = (0.5 / d) * jnp.sum(acc2[3])

    dgate, dupp = _swiglu_bwd(dffn, wd.T, gate, upp, 512, D_FF // 2)
    g_wdown = _matmul_tn(hmid, dffn, MM_ROWS, 1024, "g_w_down")
    g_wgate, g_wup = _matmul_tn_multi(h2, [dgate, dupp], MM_ROWS, [D_FF // 2, D_FF // 2], "g_w_gate_up")
    dh2 = _matmul_nn([(dgate, wg.T), (dupp, wu.T)], BF16, 512, 1024, "d_h2")
    dmix, dr1, acc1 = _bwd_ln1(dr2, dh2, x1, x0, mix, g1, sc2, ln1_g, tb)

    dyn, dpool = _matmul_nn_multi(dmix, [wout[0:d].T, wout[d:2 * d].T], [BF16, BF16], MM_ROWS, "d_yn_pool")
    g_wout = jnp.concatenate([_matmul_tn(yn, dmix, MM_ROWS, 1024, "g_w_out_a"),
                              _matmul_tn(pool, dmix, MM_ROWS, 1024, "g_w_out_b")], axis=0)
    dd, dds, g_pw, accp = _pool_bwd_a(dpool, dsave, pw, jnp.swapaxes(pw, 1, 2), pscale)
    dup = _pool_bwd_b(dds, dd, _pool_consts(True))
    dy, dz, accm = _merge_bwd(dyn, y, u, z, dskip_e, gn, tb)
    lam0 = jnp.zeros((2, D_STATE, d), F32)
    late_grads = dict(w_out=g_wout, w_gate=g_wgate, w_up=g_wup, w_down=g_wdown, pool_w=g_pw)
    pairs = ()
    if late_shards is not None:
        slabs = [_to_slabs(nme, late_grads[nme]) for nme in _LATE]
        pairs = _pair_add(slabs, _pair_swap(slabs, "reduce_g_d2d_late"), core, "reduce_g_pair_late")
    dxs, dbc, ddt, accs, lam_c, *arrived = _ssd_bwd(u, xd, bias2, a2, rexp, rexp_t, dy, hprev, lam0, "ssd_bwd",
                                                    exchange=pairs)
    du, accb = _conv_bwd_a(dxs, dy, dskip_e, dbc, u, tb, "conv_bwd_a")
    dxd, accw = _conv_bwd_b(du, xd, ddt, w8, tb, "conv_bwd_b")

    lc = ctx.shape[0]
    zeros_c = jnp.zeros((lc, d), BF16)
    dxs_c, dbc_c, ddt_c, accs_c, _ = _ssd_bwd(uc, xdc, bias2, a2, rexp, rexp_t, zeros_c, hprev_c, lam_c, "ssd_bwd_ctx")
    du_c, accb_c = _conv_bwd_a(dxs_c, zeros_c, dskip_e, dbc_c, uc, tbc, "conv_bwd_a_ctx")
    dxd_c, accw_c = _conv_bwd_b(du_c, xdc, ddt_c, w8, tbc, "conv_bwd_b_ctx")
    dhc = _matmul_nn([(dxd_c, wxd.T)], F32, 512, 1024, "d_hc")
    _, acc0c = _bwd_ln0(None, dhc, ctx, emb_g, emb_b, sc1c, tbc, "bwd_ln0_ctx")

    dh1 = _matmul_nn([(dz, wz.T), (dxd, wxd.T), (dup, wup.T)], BF16, MM_ROWS, 1024, "d_h1")
    g_wz, g_wxd, g_wpo = _matmul_tn_multi(h1, [dz, dxd, dup], 512, [1024, D_XD, 1024], "g_in_proj")
    g_wxd = g_wxd + _matmul_tn(hc, dxd_c, 512, D_XD, "g_in_proj_xd_ctx")
    g_win = jnp.concatenate([g_wz, g_wxd[:, 0:D_XBC + 2 * SSD_HEADS], g_wpo], axis=1)
    grad_x, acc0 = _bwd_ln0(dr1, dh1, x, emb_g, emb_b, sc1, tb, "bwd_ln0")

    zero_d = jnp.zeros((1, d), F32)
    dmod = jnp.concatenate([acc0[1:2], acc0[0:1], acc1[4:5], acc1[1:2], acc1[0:1], acc2[2:3]], axis=1)
    dmodc = jnp.concatenate([acc0c[1:2], acc0c[0:1]] + [zero_d] * 4, axis=1)

    big = dict(in_proj=g_win)
    if late_shards is None:
        big.update(late_grads)
    sml = dict(
        dmod=dmod, dmod_ctx=dmodc, emb_ln_g=acc0[2] + acc0c[2], emb_ln_b=acc0[3] + acc0c[3],
        conv_w=accw[0:D_CONV] + accw_c[0:D_CONV], conv_b=accb[0] + accb_c[0],
        dt_bias=accs[:, 0, 0:SSD_HEADS] + accs_c[:, 0, 0:SSD_HEADS],
        a_log=accs[:, 1, 0:SSD_HEADS] + accs_c[:, 1, 0:SSD_HEADS],
        d_skip=jnp.sum(accm[1].reshape(SSD_HEADS, HEAD_DIM), axis=1),
        ssd_norm_g=accm[0], pool_scale=accp[0], ln1_g=acc1[2], ln1_b=acc1[3], ln2_g=acc2[0], ln2_b=acc2[1])
    return loss, grad_x, big, sml, (arrived if late_shards is not None else None)


_SMALL = ("c_ctx", "emb_ln_g", "emb_ln_b", "b_ada", "conv_w", "conv_b", "dt_bias", "a_log", "d_skip",
          "ssd_norm_g", "pool_scale", "ln1_g", "ln1_b", "ln2_g", "ln2_b")


def _small_rows(size):
    return -(-size // 1024)


def _pack_small(vals, names):
    pieces, rows = [], 0
    for nme in names:
        flat = vals[nme].reshape(-1).astype(F32)
        nr = _small_rows(flat.shape[0])
        pieces.append(flat)
        if nr * 1024 > flat.shape[0]:
            pieces.append(jnp.zeros((nr * 1024 - flat.shape[0],), F32))
        rows += nr
    if rows % 8:
        pieces.append(jnp.zeros(((8 - rows % 8) * 1024,), F32))
    return jnp.concatenate(pieces).reshape(-1, 1024)


def _small_offsets(shapes, names):
    out, off = {}, 0
    for nme in names:
        nr = _small_rows(math.prod(shapes[nme]))
        out[nme] = (off, nr)
        off += nr
    return out


def _unpack_small(packed, shapes, names):
    out = {}
    for nme, (off, nr) in _small_offsets(shapes, names).items():
        out[nme] = packed[off:off + nr].reshape(-1)[:math.prod(shapes[nme])].reshape(shapes[nme])
    return out


_WEIGHT_ORDER = ("c_ctx", "emb_ln_g", "emb_ln_b", "w_ada", "b_ada", "in_proj", "conv_w", "conv_b", "dt_bias", "a_log",
                 "d_skip", "ssd_norm_g", "pool_w", "pool_scale", "w_out", "ln1_g", "ln1_b", "w_gate", "w_up", "w_down",
                 "ln2_g", "ln2_b")


def _as2d(a):
    return a.reshape(-1, a.shape[-1])


def kernel(x, c, ctx, c_ctx, emb_ln_g, emb_ln_b, w_ada, b_ada, in_proj, conv_w, conv_b, dt_bias, a_log, d_skip, ssd_norm_g, pool_w, pool_scale, w_out, ln1_g, ln1_b, w_gate, w_up, w_down, ln2_g, ln2_b, loss_target, m_c_ctx, m_emb_ln_g, m_emb_ln_b, m_w_ada, m_b_ada, m_in_proj, m_conv_w, m_conv_b, m_dt_bias, m_a_log, m_d_skip, m_ssd_norm_g, m_pool_w, m_pool_scale, m_w_out, m_ln1_g, m_ln1_b, m_w_gate, m_w_up, m_w_down, m_ln2_g, m_ln2_b, v_c_ctx, v_emb_ln_g, v_emb_ln_b, v_w_ada, v_b_ada, v_in_proj, v_conv_w, v_conv_b, v_dt_bias, v_a_log, v_d_skip, v_ssd_norm_g, v_pool_w, v_pool_scale, v_w_out, v_ln1_g, v_ln1_b, v_w_gate, v_w_up, v_w_down, v_ln2_g, v_ln2_b):
    w = dict(c_ctx=c_ctx, emb_ln_g=emb_ln_g, emb_ln_b=emb_ln_b, w_ada=w_ada, b_ada=b_ada, in_proj=in_proj, conv_w=conv_w,
             conv_b=conv_b, dt_bias=dt_bias, a_log=a_log, d_skip=d_skip, ssd_norm_g=ssd_norm_g, pool_w=pool_w,
             pool_scale=pool_scale, w_out=w_out, ln1_g=ln1_g, ln1_b=ln1_b, w_gate=w_gate, w_up=w_up, w_down=w_down,
             ln2_g=ln2_g, ln2_b=ln2_b)
    m = dict(c_ctx=m_c_ctx, emb_ln_g=m_emb_ln_g, emb_ln_b=m_emb_ln_b, w_ada=m_w_ada, b_ada=m_b_ada, in_proj=m_in_proj,
             conv_w=m_conv_w, conv_b=m_conv_b, dt_bias=m_dt_bias, a_log=m_a_log, d_skip=m_d_skip,
             ssd_norm_g=m_ssd_norm_g, pool_w=m_pool_w, pool_scale=m_pool_scale, w_out=m_w_out, ln1_g=m_ln1_g,
             ln1_b=m_ln1_b, w_gate=m_w_gate, w_up=m_w_up, w_down=m_w_down, ln2_g=m_ln2_g, ln2_b=m_ln2_b)
    v = dict(c_ctx=v_c_ctx, emb_ln_g=v_emb_ln_g, emb_ln_b=v_emb_ln_b, w_ada=v_w_ada, b_ada=v_b_ada, in_proj=v_in_proj,
             conv_w=v_conv_w, conv_b=v_conv_b, dt_bias=v_dt_bias, a_log=v_a_log, d_skip=v_d_skip,
             ssd_norm_g=v_ssd_norm_g, pool_w=v_pool_w, pool_scale=v_pool_scale, w_out=v_w_out, ln1_g=v_ln1_g,
             ln1_b=v_ln1_b, w_gate=v_w_gate, w_up=v_w_up, w_down=v_w_down, ln2_g=v_ln2_g, ln2_b=v_ln2_b)

    xi, yi, ci = _mesh_pos()
    chip = 2 * xi + yi

    dev = 4 * xi + 2 * yi + ci
    d = D_MODEL
    core = ci.reshape(1).astype(jnp.int32)

    crow = jnp.concatenate([c.reshape(1, d), jnp.zeros((7, d), F32)], axis=0)
    _, c_all = _allreduce_small(crow, "gather_c")
    c16 = jnp.concatenate([c_all[:, 0, :], c_ctx.reshape(1, d), jnp.zeros((MOD_ROWS - 9, d), F32)], axis=0)
    ncol = w_ada.shape[-1]
    wada_bf = w_ada[0].astype(BF16)
    b_mine = lax.dynamic_slice_in_dim(b_ada, chip * ncol, ncol, axis=1)
    mods4 = _chip_bcast(_mods_fwd(c16, wada_bf, b_mine), "gather_mods")
    mods = mods4.transpose(1, 0, 2).reshape(MOD_ROWS, 4 * ncol)
    mod = lax.dynamic_slice_in_dim(mods, dev, 1, axis=0)
    mod_ctx = mods[8:9]

    shard = {name: w[name][0].astype(BF16).reshape(shp) for name, shp, _ in _BIG}
    conv8 = jnp.concatenate([conv_w[0], jnp.zeros((8 - D_CONV, conv_w.shape[-1]), F32)], axis=0)
    *gathered, conv4 = _gather_weights([shard[nme] for nme in _EARLY], conv8)
    wts = {nme: _assemble(nme, t) for nme, t in zip(_EARLY, gathered)}
    w8 = conv4.transpose(1, 0, 2).reshape(8, D_XBC)
    small = {nme: (w[nme] if nme in ("c_ctx", "emb_ln_g", "emb_ln_b") else w[nme][0]) for nme in _SMALL if nme != "conv_w"}

    loss, grad_x, big, sml, late_parts = _device_step(x[0], mod, mod_ctx, ctx[0], loss_target[0], wts, w8, small, 512,
                                                      late_shards=[shard[nme] for nme in _LATE], core=core)
    loss = lax.psum(loss, ("x", "y", "c"))

    g_big = _reduce_grads(big, late_parts, core)
    reduced = tuple(sml)
    small_shapes = {nme: sml[nme].shape for nme in reduced}
    total, each = _allreduce_small(_pack_small(sml, reduced), "reduce_small")
    g_small = _unpack_small(total, small_shapes, reduced)
    cw_cols = conv_w.shape[-1]
    g_small["conv_w"] = lax.dynamic_slice_in_dim(g_small["conv_w"], chip * cw_cols, cw_cols, axis=1)

    off, nr = _small_offsets(small_shapes, reduced)["dmod"]
    dm16 = jnp.concatenate([each[:, off:off + nr, :].reshape(8, nr * 1024)[:, :6 * d], g_small["dmod_ctx"],
                            jnp.zeros((MOD_ROWS - 9, 6 * d), F32)], axis=0)
    dm_mine = lax.dynamic_slice_in_dim(dm16, chip * ncol, ncol, axis=1)
    g_wada = _mods_bwd_w(c16.T, dm_mine)
    g_small["b_ada"] = _mods_bwd_b(dm16)[0:1]
    c_part = _mods_bwd_c(dm_mine, wada_bf, c16)[8:16]
    g_small["c_ctx"] = _allreduce_small(c_part, "reduce_c_ctx")[0][0]

    grads, delta, new_m, new_v = {}, {}, {}, {}
    grads["w_ada"] = g_wada[None]
    delta["w_ada"], new_m["w_ada"], new_v["w_ada"] = (
        t[None] for t in _adamw(w_ada[0], g_wada, m_w_ada[0], v_w_ada[0], "adamw_w_ada"))
    for name, _, _ in _BIG:
        g2 = _as2d(g_big[name])
        d2, m2, v2 = _adamw(_as2d(w[name][0]), g2, _as2d(m[name][0]), _as2d(v[name][0]), "adamw_" + name)
        grads[name] = g2.reshape(w[name].shape)
        delta[name], new_m[name], new_v[name] = (t.reshape(w[name].shape) for t in (d2, m2, v2))
    shp = {nme: w[nme].shape for nme in _SMALL}
    gp = _pack_small(g_small, _SMALL)
    dp, mp, vp = _adamw(_pack_small(w, _SMALL), gp, _pack_small(m, _SMALL), _pack_small(v, _SMALL), "adamw_small")
    for dst, src in ((grads, gp), (delta, dp), (new_m, mp), (new_v, vp)):
        dst.update(_unpack_small(src, shp, _SMALL))

    return (loss, grad_x[None], *[grads[nme] for nme in _WEIGHT_ORDER], *[delta[nme] for nme in _WEIGHT_ORDER],
            *[new_m[nme] for nme in _WEIGHT_ORDER], *[new_v[nme] for nme in _WEIGHT_ORDER])
```

```python
import functools
import math

import jax
import jax.numpy as jnp
from jax import lax
from jax.experimental import pallas as pl
from jax.experimental.pallas import tpu as pltpu

F32 = jnp.float32
BF16 = jnp.bfloat16
MESH = pl.DeviceIdType.MESH

D_MODEL = 1024
SSD_HEADS = 16
HEAD_DIM = 64
D_STATE = 128
CHUNK = 128
D_CONV = 5
D_XBC = D_MODEL + 2 * D_STATE
D_XD = 1408
N_POOL = 4
POOL_DIM = 256
POOL_WINDOWS = (2, 4, 8, 16)
GRID_W = 64
D_FF = 2816
D_IN_PROJ = 3360
LN_EPS = 1e-5
ALPHA = 2.0 ** 0.25
POOL_TB = 512
MM_ROWS = 1024

ADAM_LR = 0.001
ADAM_B1 = 0.9
ADAM_B2 = 0.999
ADAM_EPS = 1e-08
ADAM_WD = 0.01
ADAM_STEP = 10

VMEM_LIMIT = 56 * 1024 * 1024


def _cp(sem=None):
    return pltpu.CompilerParams(dimension_semantics=sem, vmem_limit_bytes=VMEM_LIMIT)


def _sigmoid(x):
    return 1.0 / (1.0 + jnp.exp(-x))


def _silu(x):
    return x * _sigmoid(x)


def _dsilu(x):
    s = _sigmoid(x)
    return s * (1.0 + x * (1.0 - s))


def _softplus(x):
    t = jnp.exp(-jnp.abs(x))
    u = 1.0 + t
    log1p = jnp.where(u == 1.0, t, jnp.log(u) * t / (u - 1.0 + (u == 1.0)))
    return jnp.maximum(x, 0.0) + log1p


def _split(x, n):
    parts, r = [], x
    for _ in range(n):
        p = r.astype(BF16)
        parts.append(p)
        r = r - p.astype(F32)
    return parts


def _dot(a, b):
    return jnp.dot(a, b, preferred_element_type=F32)


def _dot_nt(a, b):
    return lax.dot_general(a, b, (((1,), (1,)), ((), ())), preferred_element_type=F32)


def _dot_tn(a, b):
    return lax.dot_general(a, b, (((0,), (0,)), ((), ())), preferred_element_type=F32)


def _dot_sel_l(sel_bf, x, n=3):
    out = None
    for p in _split(x, n):
        t = _dot(sel_bf, p)
        out = t if out is None else out + t
    return out


def _dot_sel_r(x, sel_bf, n=3):
    out = None
    for p in _split(x, n):
        t = _dot(p, sel_bf)
        out = t if out is None else out + t
    return out


ROW_SUB = 16


def _row_tiles(tb):
    assert tb % ROW_SUB == 0
    return [slice(s * ROW_SUB, (s + 1) * ROW_SUB) for s in range(tb // ROW_SUB)]


def _fold8(v):
    out = v[0:8, :]
    for r in range(8, v.shape[0], 8):
        out = out + v[r:r + 8, :]
    return out


def _row_block(n, cap=256, mult=8):
    best = None
    for t in range(mult, min(n, cap) + 1, mult):
        if n % t == 0:
            best = t
    return best if best is not None else n


def _vec(v):
    return v.reshape(1, -1).astype(F32)


MOD_ROWS = 16
MOD_TN = 512


def _mods_fwd(c16, w_bf, b):
    r, d = c16.shape
    n = w_bf.shape[1]

    def body(c_ref, w_ref, b_ref, o_ref):
        s = _silu(c_ref[...]).astype(BF16)
        o_ref[...] = _dot(s, w_ref[...]) + b_ref[...]

    return pl.pallas_call(
        body, name="mods_fwd", grid=(n // MOD_TN,),
        in_specs=[pl.BlockSpec((r, d), lambda j: (0, 0)),
                  pl.BlockSpec((d, MOD_TN), lambda j: (0, j)),
                  pl.BlockSpec((1, MOD_TN), lambda j: (0, j))],
        out_specs=pl.BlockSpec((r, MOD_TN), lambda j: (0, j)),
        out_shape=jax.ShapeDtypeStruct((r, n), F32),
        compiler_params=_cp(("arbitrary",)),
    )(c16, w_bf, b)


def _mods_bwd_w(ct16, dm16):
    d = ct16.shape[0]
    n = dm16.shape[1]

    def body(ct_ref, dm_ref, dw_ref):
        s = _silu(ct_ref[...])
        dm = dm_ref[...]
        acc = s[:, 0:1] * dm[0:1, :]
        for r in range(1, 9):
            acc = acc + s[:, r:r + 1] * dm[r:r + 1, :]
        dw_ref[...] = acc

    return pl.pallas_call(
        body, name="mods_bwd_w", grid=(n // MOD_TN,),
        in_specs=[pl.BlockSpec((d, MOD_ROWS), lambda j: (0, 0)),
                  pl.BlockSpec((MOD_ROWS, MOD_TN), lambda j: (0, j))],
        out_specs=pl.BlockSpec((d, MOD_TN), lambda j: (0, j)),
        out_shape=jax.ShapeDtypeStruct((d, n), F32),
        compiler_params=_cp(("arbitrary",)),
    )(ct16, dm16)


def _mods_bwd_c(dm16, w_bf, c16):
    d = c16.shape[1]
    n = dm16.shape[1]
    nk = n // MOD_TN

    def body(dm_ref, w_ref, c_ref, o_ref):
        k = pl.program_id(0)

        @pl.when(k == 0)
        def _():
            o_ref[...] = jnp.zeros_like(o_ref)

        o_ref[...] += _dot_nt(dm_ref[...].astype(BF16), w_ref[...])

        @pl.when(k == nk - 1)
        def _():
            o_ref[...] = o_ref[...] * (0.5 * _dsilu(c_ref[...]))

    return pl.pallas_call(
        body, name="mods_bwd_c", grid=(nk,),
        in_specs=[pl.BlockSpec((MOD_ROWS, MOD_TN), lambda k: (0, k)),
                  pl.BlockSpec((d, MOD_TN), lambda k: (0, k)),
                  pl.BlockSpec((MOD_ROWS, d), lambda k: (0, 0))],
        out_specs=pl.BlockSpec((MOD_ROWS, d), lambda k: (0, 0)),
        out_shape=jax.ShapeDtypeStruct((MOD_ROWS, d), F32),
        compiler_params=_cp(("arbitrary",)),
    )(dm16, w_bf, c16)


def _mods_bwd_b(dm16):
    n = dm16.shape[1]

    def body(dm_ref, o_ref):
        dm = dm_ref[...]
        acc = dm[0:1, :]
        for r in range(1, 9):
            acc = acc + dm[r:r + 1, :]
        o_ref[...] = jnp.broadcast_to(acc, (8, MOD_TN))

    return pl.pallas_call(
        body, name="mods_bwd_b", grid=(n // MOD_TN,),
        in_specs=[pl.BlockSpec((MOD_ROWS, MOD_TN), lambda j: (0, j))],
        out_specs=pl.BlockSpec((8, MOD_TN), lambda j: (0, j)),
        out_shape=jax.ShapeDtypeStruct((8, n), F32),
        compiler_params=_cp(("arbitrary",)),
    )(dm16)


def _ln_stats(x):
    mu = jnp.mean(x, axis=-1, keepdims=True)
    xc = x - mu
    var = jnp.mean(xc * xc, axis=-1, keepdims=True)
    rstd = lax.rsqrt(var + LN_EPS)
    return xc * rstd, rstd


def _ln_bwd(dxhat, xhat, rstd):
    m1 = jnp.mean(dxhat, axis=-1, keepdims=True)
    m2 = jnp.mean(dxhat * xhat, axis=-1, keepdims=True)
    return rstd * (dxhat - m1 - xhat * m2)


def _row_spec(tb, d):
    return pl.BlockSpec((tb, d), lambda i: (i, 0))


def _par_spec(d):
    return pl.BlockSpec((1, d), lambda i: (0, 0))


def _acc_spec(d):
    return pl.BlockSpec((8, d), lambda i: (0, 0))


def _ln_mod(x, g, b, sh, sc, tb, name):
    n, d = x.shape

    def body(x_ref, g_ref, b_ref, sh_ref, sc_ref, x0_ref, h_ref):
        g, b, sh, sc1 = g_ref[...], b_ref[...], sh_ref[...], 1.0 + sc_ref[...]
        for r in _row_tiles(min(tb, n)):
            xhat, _ = _ln_stats(x_ref[r, :])
            x0 = xhat * g + b
            x0_ref[r, :] = x0
            h_ref[r, :] = (x0 * sc1 + sh).astype(BF16)

    return pl.pallas_call(
        body, name=name, grid=(n // tb,),
        in_specs=[_row_spec(tb, d)] + [_par_spec(d)] * 4,
        out_specs=[_row_spec(tb, d), _row_spec(tb, d)],
        out_shape=[jax.ShapeDtypeStruct((n, d), F32), jax.ShapeDtypeStruct((n, d), BF16)],
        compiler_params=_cp(("parallel",)),
    )(x, g, b, sh, sc)


def _res_ln(xres, mix, gate, g, b, sh, sc, tb):
    n, d = xres.shape

    def body(xr_ref, mix_ref, gate_ref, g_ref, b_ref, sh_ref, sc_ref, x1_ref, h_ref):
        gate_v, g, b, sh, sc1 = gate_ref[...], g_ref[...], b_ref[...], sh_ref[...], 1.0 + sc_ref[...]
        for r in _row_tiles(tb):
            xhat, _ = _ln_stats(ALPHA * xr_ref[r, :] + gate_v * mix_ref[r, :].astype(F32))
            x1 = xhat * g + b
            x1_ref[r, :] = x1
            h_ref[r, :] = (x1 * sc1 + sh).astype(BF16)

    return pl.pallas_call(
        body, name="res_ln1", grid=(n // tb,),
        in_specs=[_row_spec(tb, d)] * 2 + [_par_spec(d)] * 5,
        out_specs=[_row_spec(tb, d), _row_spec(tb, d)],
        out_shape=[jax.ShapeDtypeStruct((n, d), F32), jax.ShapeDtypeStruct((n, d), BF16)],
        compiler_params=_cp(("parallel",)),
    )(xres, mix, gate, g, b, sh, sc)


def _final_ln_loss(x1, ffn, gate, g, b, target, tb):
    n, d = x1.shape

    def body(x1_ref, ffn_ref, gate_ref, g_ref, b_ref, t_ref, dffn_ref, dr_ref, acc_ref):
        i = pl.program_id(0)

        @pl.when(i == 0)
        def _():
            acc_ref[...] = jnp.zeros_like(acc_ref)

        gate_v, g, b = gate_ref[...], g_ref[...], b_ref[...]
        parts = [jnp.zeros((8, d), F32)] * 4
        for r in _row_tiles(tb):
            ffn = ffn_ref[r, :].astype(F32)
            xhat, rstd = _ln_stats(ALPHA * x1_ref[r, :] + gate_v * ffn)
            err = xhat * g + b - t_ref[r, :]
            dx2 = err * (1.0 / d)
            dr = _ln_bwd(dx2 * g, xhat, rstd)
            dr_ref[r, :] = dr
            dffn_ref[r, :] = (gate_v * dr).astype(BF16)
            terms = (dx2 * xhat, dx2, dr * ffn, err * err)
            parts = [p + _fold8(t) for p, t in zip(parts, terms)]
        for j, p in enumerate(parts):
            acc_ref[j:j + 1, :] += jnp.sum(p, axis=0, keepdims=True)

    return pl.pallas_call(
        body, name="final_ln_loss", grid=(n // tb,),
        in_specs=[_row_spec(tb, d)] * 2 + [_par_spec(d)] * 3 + [_row_spec(tb, d)],
        out_specs=[_row_spec(tb, d), _row_spec(tb, d), _acc_spec(d)],
        out_shape=[jax.ShapeDtypeStruct((n, d), BF16), jax.ShapeDtypeStruct((n, d), F32),
                   jax.ShapeDtypeStruct((8, d), F32)],
        compiler_params=_cp(("arbitrary",)),
    )(x1, ffn, gate, g, b, target)


def _bwd_ln1(dr2, dh2, x1, x0, mix, gate, sc2, g, tb):
    n, d = x1.shape

    def body(dr2_ref, dh2_ref, x1_ref, x0_ref, mix_ref, gate_ref, sc_ref, g_ref, dmix_ref, dr1_ref, acc_ref):
        i = pl.program_id(0)

        @pl.when(i == 0)
        def _():
            acc_ref[...] = jnp.zeros_like(acc_ref)

        gate_v, g, sc1 = gate_ref[...], g_ref[...], 1.0 + sc_ref[...]
        parts = [jnp.zeros((8, d), F32)] * 5
        for r in _row_tiles(tb):
            dh2 = dh2_ref[r, :].astype(F32)
            mix = mix_ref[r, :].astype(F32)
            dx1 = ALPHA * dr2_ref[r, :] + dh2 * sc1
            xhat, rstd = _ln_stats(ALPHA * x0_ref[r, :] + gate_v * mix)
            dr1 = _ln_bwd(dx1 * g, xhat, rstd)
            dr1_ref[r, :] = dr1
            dmix_ref[r, :] = (gate_v * dr1).astype(BF16)
            terms = (dh2 * x1_ref[r, :], dh2, dx1 * xhat, dx1, dr1 * mix)
            parts = [p + _fold8(t) for p, t in zip(parts, terms)]
        for j, p in enumerate(parts):
            acc_ref[j:j + 1, :] += jnp.sum(p, axis=0, keepdims=True)

    return pl.pallas_call(
        body, name="bwd_ln1", grid=(n // tb,),
        in_specs=[_row_spec(tb, d)] * 5 + [_par_spec(d)] * 3,
        out_specs=[_row_spec(tb, d), _row_spec(tb, d), _acc_spec(d)],
        out_shape=[jax.ShapeDtypeStruct((n, d), BF16), jax.ShapeDtypeStruct((n, d), F32),
                   jax.ShapeDtypeStruct((8, d), F32)],
        compiler_params=_cp(("arbitrary",)),
    )(dr2, dh2, x1, x0, mix, gate, sc2, g)


def _bwd_ln0(dres, dh, x, g, b, sc, tb, name):
    n, d = x.shape
    has_res = dres is not None

    def body(*refs):
        if has_res:
            dres_ref, dh_ref, x_ref, g_ref, b_ref, sc_ref, dx_ref, acc_ref = refs
        else:
            dh_ref, x_ref, g_ref, b_ref, sc_ref, dx_ref, acc_ref = refs
        i = pl.program_id(0)

        @pl.when(i == 0)
        def _():
            acc_ref[...] = jnp.zeros_like(acc_ref)

        g, b, sc1 = g_ref[...], b_ref[...], 1.0 + sc_ref[...]
        parts = [jnp.zeros((8, d), F32)] * 4
        for r in _row_tiles(tb):
            dh = dh_ref[r, :].astype(F32)
            xhat, rstd = _ln_stats(x_ref[r, :])
            x0 = xhat * g + b
            dx0 = dh * sc1
            if has_res:
                dx0 = dx0 + ALPHA * dres_ref[r, :]
            dx_ref[r, :] = _ln_bwd(dx0 * g, xhat, rstd)
            terms = (dh * x0, dh, dx0 * xhat, dx0)
            parts = [p + _fold8(t) for p, t in zip(parts, terms)]
        for j, p in enumerate(parts):
            acc_ref[j:j + 1, :] += jnp.sum(p, axis=0, keepdims=True)

    ins = ([dres] if has_res else []) + [dh, x, g, b, sc]
    return pl.pallas_call(
        body, name=name, grid=(n // tb,),
        in_specs=[_row_spec(tb, d)] * (3 if has_res else 2) + [_par_spec(d)] * 3,
        out_specs=[_row_spec(tb, d), _acc_spec(d)],
        out_shape=[jax.ShapeDtypeStruct((n, d), F32), jax.ShapeDtypeStruct((8, d), F32)],
        compiler_params=_cp(("arbitrary",)),
    )(*ins)


def _matmul_nn(pairs, out_dtype, tm, tn, name):
    m = pairs[0][0].shape[0]
    n = pairs[0][1].shape[1]
    tm = min(tm, m)
    tn = min(tn, n)
    npair = len(pairs)

    def body(*refs):
        o_ref = refs[-1]
        acc = None
        for p in range(npair):
            t = _dot(refs[2 * p][...].astype(BF16), refs[2 * p + 1][...])
            acc = t if acc is None else acc + t
        o_ref[...] = acc.astype(out_dtype)

    in_specs, args = [], []
    for a, b in pairs:
        k = a.shape[1]
        in_specs += [pl.BlockSpec((tm, k), lambda i, j: (i, 0)), pl.BlockSpec((k, tn), lambda i, j: (0, j))]
        args += [a, b]
    return pl.pallas_call(
        body, name=name, grid=(m // tm, n // tn),
        in_specs=in_specs,
        out_specs=pl.BlockSpec((tm, tn), lambda i, j: (i, j)),
        out_shape=jax.ShapeDtypeStruct((m, n), out_dtype),
        compiler_params=_cp(("parallel", "arbitrary")),
    )(*args)


def _matmul_tn(a, g, tm, tn, name):
    m, k = a.shape
    n = g.shape[1]
    tm = min(tm, m)
    tn = min(tn, n)

    def body(a_ref, g_ref, o_ref):
        i = pl.program_id(1)

        @pl.when(i == 0)
        def _():
            o_ref[...] = jnp.zeros_like(o_ref)

        o_ref[...] += _dot_tn(a_ref[...].astype(BF16), g_ref[...].astype(BF16))

    return pl.pallas_call(
        body, name=name, grid=(n // tn, m // tm),
        in_specs=[pl.BlockSpec((tm, k), lambda j, i: (i, 0)), pl.BlockSpec((tm, tn), lambda j, i: (i, j))],
        out_specs=pl.BlockSpec((k, tn), lambda j, i: (0, j)),
        out_shape=jax.ShapeDtypeStruct((k, n), F32),
        compiler_params=_cp(("parallel", "arbitrary")),
    )(a, g)


def _matmul_nn_multi(a, bs, out_dtypes, tm, name):
    m, k = a.shape
    tm = min(tm, m)
    nb = len(bs)

    def body(a_ref, *refs):
        av = a_ref[...].astype(BF16)
        for j in range(nb):
            refs[nb + j][...] = _dot(av, refs[j][...]).astype(out_dtypes[j])

    return pl.pallas_call(
        body, name=name, grid=(m // tm,),
        in_specs=[pl.BlockSpec((tm, k), lambda i: (i, 0))] + [pl.BlockSpec(b.shape, lambda i: (0, 0)) for b in bs],
        out_specs=[pl.BlockSpec((tm, b.shape[1]), lambda i: (i, 0)) for b in bs],
        out_shape=[jax.ShapeDtypeStruct((m, b.shape[1]), dt) for b, dt in zip(bs, out_dtypes)],
        compiler_params=_cp(("parallel",)),
    )(a, *bs)


def _matmul_tn_multi(a, gs, tm, tns, name):
    m, k = a.shape
    tm = min(tm, m)
    ng = len(gs)
    nj = gs[0].shape[1] // tns[0]
    assert all(g.shape[1] // t == nj and g.shape[1] % t == 0 for g, t in zip(gs, tns))

    def body(a_ref, *refs):
        i = pl.program_id(1)

        @pl.when(i == 0)
        def _():
            for j in range(ng):
                refs[ng + j][...] = jnp.zeros_like(refs[ng + j])

        av = a_ref[...].astype(BF16)
        for j in range(ng):
            refs[ng + j][...] += _dot_tn(av, refs[j][...].astype(BF16))

    return pl.pallas_call(
        body, name=name, grid=(nj, m // tm),
        in_specs=[pl.BlockSpec((tm, k), lambda j, i: (i, 0))] +
                 [pl.BlockSpec((tm, t), lambda j, i: (i, j)) for t in tns],
        out_specs=[pl.BlockSpec((k, t), lambda j, i: (0, j)) for t in tns],
        out_shape=[jax.ShapeDtypeStruct((k, g.shape[1]), F32) for g in gs],
        compiler_params=_cp(("parallel", "arbitrary")),
    )(a, *gs)


def _swiglu_fwd(h, wg, wu, tm, tn):
    m, k = h.shape
    n = wg.shape[1]
    tm = min(tm, m)

    def body(h_ref, wg_ref, wu_ref, gate_ref, up_ref, hmid_ref):
        hv = h_ref[...]
        gate = _dot(hv, wg_ref[...])
        up = _dot(hv, wu_ref[...])
        gate_ref[...] = gate.astype(BF16)
        up_ref[...] = up.astype(BF16)
        hmid_ref[...] = (_silu(gate) * up).astype(BF16)

    blk = pl.BlockSpec((tm, tn), lambda i, j: (i, j))
    wspec = pl.BlockSpec((k, tn), lambda i, j: (0, j))
    return pl.pallas_call(
        body, name="swiglu_fwd", grid=(m // tm, n // tn),
        in_specs=[pl.BlockSpec((tm, k), lambda i, j: (i, 0)), wspec, wspec],
        out_specs=[blk, blk, blk],
        out_shape=[jax.ShapeDtypeStruct((m, n), BF16), jax.ShapeDtypeStruct((m, n), BF16),
                   jax.ShapeDtypeStruct((m, n), BF16)],
        compiler_params=_cp(("parallel", "arbitrary")),
    )(h, wg, wu)


def _swiglu_bwd(dffn, wdt, gate, up, tm, tn):
    m, k = dffn.shape
    n = wdt.shape[1]
    tm = min(tm, m)

    def body(d_ref, w_ref, gate_ref, up_ref, dg_ref, du_ref):
        dh = _dot(d_ref[...], w_ref[...])
        gate = gate_ref[...].astype(F32)
        sg = _sigmoid(gate)
        dg_ref[...] = (dh * up_ref[...].astype(F32) * (sg * (1.0 + gate * (1.0 - sg)))).astype(BF16)
        du_ref[...] = (dh * (gate * sg)).astype(BF16)

    blk = pl.BlockSpec((tm, tn), lambda i, j: (i, j))
    return pl.pallas_call(
        body, name="swiglu_bwd", grid=(m // tm, n // tn),
        in_specs=[pl.BlockSpec((tm, k), lambda i, j: (i, 0)), pl.BlockSpec((k, tn), lambda i, j: (0, j)), blk, blk],
        out_specs=[blk, blk],
        out_shape=[jax.ShapeDtypeStruct((m, n), BF16), jax.ShapeDtypeStruct((m, n), BF16)],
        compiler_params=_cp(("parallel", "arbitrary")),
    )(dffn, wdt, gate, up)


def _halo_specs(tb, width, nrows):
    r8 = tb // 8
    last = nrows // 8 - 1
    prev = pl.BlockSpec((8, width), lambda i: (jnp.maximum(i * r8 - 1, 0), 0))
    nxt = pl.BlockSpec((8, width), lambda i: (jnp.minimum((i + 1) * r8, last), 0))
    return prev, nxt


CONV_SUB = 32


def _halo_scratch():
    return [pltpu.VMEM((CONV_SUB + 16, D_XBC), F32), pltpu.VMEM((CONV_SUB + 16, D_XBC), F32)]


def _shifted_rows(prev_ref, cur_ref, next_ref, top, bot, tb, i, nb):
    sub = CONV_SUB
    nsub = tb // sub
    assert nsub >= 2
    top[0:8, :] = prev_ref[...] * (i > 0).astype(F32)
    top[8:sub + 16, :] = cur_ref[0:sub + 8, :]
    bot[0:sub + 8, :] = cur_ref[tb - sub - 8:tb, :]
    bot[sub + 8:sub + 16, :] = next_ref[...] * (i < nb - 1).astype(F32)

    def rows(s, o):
        if s == 0:
            return top[8 + o:8 + o + sub, :]
        if s == nsub - 1:
            return bot[8 + o:8 + o + sub, :]
        return cur_ref[s * sub + o:(s + 1) * sub + o, :]

    return rows


def _conv_fwd(xd, w8, b, tb, name):
    n = xd.shape[0]
    tb = min(tb, n)
    nb = n // tb
    prev, nxt = _halo_specs(tb, D_XBC, n)

    def body(p_ref, c_ref, n_ref, w_ref, b_ref, u_ref, top, bot):
        i = pl.program_id(0)
        rows = _shifted_rows(p_ref, c_ref, n_ref, top, bot, tb, i, nb)
        w = [w_ref[k:k + 1, :] for k in range(D_CONV)]
        bias = jnp.broadcast_to(b_ref[...], (CONV_SUB, D_XBC))
        for s in range(tb // CONV_SUB):
            acc = bias
            for k in range(D_CONV):
                acc = acc + w[k] * rows(s, k - 2)
            u_ref[s * CONV_SUB:(s + 1) * CONV_SUB, :] = acc.astype(BF16)

    return pl.pallas_call(
        body, name=name, grid=(nb,),
        in_specs=[prev, pl.BlockSpec((tb, D_XBC), lambda i: (i, 0)), nxt,
                  pl.BlockSpec((8, D_XBC), lambda i: (0, 0)), _par_spec(D_XBC)],
        out_specs=_row_spec(tb, D_XBC),
        out_shape=jax.ShapeDtypeStruct((n, D_XBC), BF16),
        scratch_shapes=_halo_scratch(),
        compiler_params=_cp(("parallel",)),
    )(xd, xd, xd, w8, b)


def _conv_bwd_a(dxs, dy, dskip_e, dbc, u, tb, name):
    n = u.shape[0]
    tb = min(tb, n)

    def body(dxs_ref, dy_ref, sk_ref, dbc_ref, u_ref, du_ref, acc_ref):
        i = pl.program_id(0)

        @pl.when(i == 0)
        def _():
            acc_ref[...] = jnp.zeros_like(acc_ref)

        sk = sk_ref[...]
        part = jnp.zeros((8, D_XBC), F32)
        for r in _row_tiles(tb):
            gx = dxs_ref[0, r, :].astype(F32) + dxs_ref[1, r, :].astype(F32) + dy_ref[r, :].astype(F32) * sk
            gbc = dbc_ref[0, r, :] + dbc_ref[1, r, :]
            du = jnp.concatenate([gx, gbc], axis=1) * _dsilu(u_ref[r, :].astype(F32))
            du_ref[r, :] = du
            part = part + _fold8(du)
        acc_ref[0:1, :] += jnp.sum(part, axis=0, keepdims=True)

    return pl.pallas_call(
        body, name=name, grid=(n // tb,),
        in_specs=[pl.BlockSpec((2, tb, D_MODEL), lambda i: (0, i, 0)), _row_spec(tb, D_MODEL), _par_spec(D_MODEL),
                  pl.BlockSpec((2, tb, 2 * D_STATE), lambda i: (0, i, 0)), _row_spec(tb, D_XBC)],
        out_specs=[_row_spec(tb, D_XBC), _acc_spec(D_XBC)],
        out_shape=[jax.ShapeDtypeStruct((n, D_XBC), F32), jax.ShapeDtypeStruct((8, D_XBC), F32)],
        compiler_params=_cp(("arbitrary",)),
    )(dxs, dy, dskip_e, dbc, u)


def _conv_bwd_b(du, xd, ddt, w8, tb, name):
    n = du.shape[0]
    tb = min(tb, n)
    nb = n // tb
    prev, nxt = _halo_specs(tb, D_XBC, n)

    def body(dp_ref, dc_ref, dn_ref, xp_ref, xc_ref, xn_ref, ddt_ref, w_ref, dxd_ref, acc_ref, dtop, dbot, xtop, xbot):
        i = pl.program_id(0)

        @pl.when(i == 0)
        def _():
            acc_ref[...] = jnp.zeros_like(acc_ref)

        sub = CONV_SUB
        nsub = tb // sub
        du_rows = _shifted_rows(dp_ref, dc_ref, dn_ref, dtop, dbot, tb, i, nb)
        x_rows = _shifted_rows(xp_ref, xc_ref, xn_ref, xtop, xbot, tb, i, nb)
        w = [w_ref[k:k + 1, :] for k in range(D_CONV)]
        for s in range(nsub):
            acc = w[0] * du_rows(s, 2)
            for k in range(1, D_CONV):
                acc = acc + w[k] * du_rows(s, 2 - k)
            dxd_ref[s * sub:(s + 1) * sub, 0:D_XBC] = acc.astype(BF16)
        for k in range(D_CONV):
            part = jnp.zeros((8, D_XBC), F32)
            for s in range(nsub):
                prod = dc_ref[s * sub:(s + 1) * sub, :] * x_rows(s, k - 2)
                for r in range(0, sub, 8):
                    part = part + prod[r:r + 8, :]
            acc_ref[k:k + 1, :] += jnp.sum(part, axis=0, keepdims=True)
        ddt = ddt_ref[0] + pltpu.roll(ddt_ref[1], SSD_HEADS, 1)
        dxd_ref[:, D_XBC:D_XD] = ddt.astype(BF16)

    cur = pl.BlockSpec((tb, D_XBC), lambda i: (i, 0))
    return pl.pallas_call(
        body, name=name, grid=(nb,),
        in_specs=[prev, cur, nxt, prev, cur, nxt,
                  pl.BlockSpec((2, tb, 128), lambda i: (0, i, 0)), pl.BlockSpec((8, D_XBC), lambda i: (0, 0))],
        out_specs=[_row_spec(tb, D_XD), _acc_spec(D_XBC)],
        out_shape=[jax.ShapeDtypeStruct((n, D_XD), BF16), jax.ShapeDtypeStruct((8, D_XBC), F32)],
        scratch_shapes=_halo_scratch() + _halo_scratch(),
        compiler_params=_cp(("arbitrary",)),
    )(du, du, du, xd, xd, xd, ddt, w8)


def _ssd_chunk_index(nc, reverse):
    def idx(d, k):
        kk = (nc - 1 - k) if reverse else k
        return kk + d * (nc - 1 - 2 * kk)
    return idx


def _ssd_prologue(d, u_ref, xd_ref, bias_ref, a_ref, r_ref):
    q = CHUNK
    xbc = _silu(u_ref[...].astype(F32))
    xs = xbc[:, 0:D_MODEL]
    bm = xbc[:, D_MODEL:D_MODEL + D_STATE]
    cm = xbc[:, D_MODEL + D_STATE:D_XBC]
    row = lax.broadcasted_iota(jnp.int32, (q, q), 0)
    col = lax.broadcasted_iota(jnp.int32, (q, q), 1)
    sgn = 1 - 2 * d
    mask = ((row - col) * sgn) >= 0
    mask_t = ((row - col) * sgn) <= 0
    xdv = xd_ref[...]
    dtraw = jnp.where(d == 0, xdv, pltpu.roll(xdv, 128 - SSD_HEADS, 1)) + bias_ref[...]
    head_lane = col < SSD_HEADS
    dt = jnp.where(head_lane, _softplus(dtraw), 0.0)
    a = a_ref[...]
    tri = jnp.where(mask, 1.0, 0.0).astype(BF16)
    acum = _dot_sel_l(tri, dt * a)
    rexp = r_ref[...]
    alast = jnp.where(d == 0, acum[q - 1:q, :], acum[0:1, :])
    e16 = jnp.exp(acum)
    dend16 = jnp.exp(alast - acum)
    wend16 = dend16 * dt
    e = _dot_sel_r(e16, rexp, n=1)
    wend_e = _dot_sel_r(wend16, rexp, n=1)
    elast_e = _dot_sel_r(jnp.broadcast_to(jnp.exp(alast), (8, 128)), rexp, n=2)[0:1, :]
    g = _dot_nt(cm.astype(BF16), bm.astype(BF16))
    return dict(xs=xs, bm=bm, cm=cm, mask=mask, mask_t=mask_t, dtraw=dtraw, head_lane=head_lane, dt=dt, a=a,
                acum=acum, acum_t=acum.T, dt_t=dt.T, e16=e16, dend16=dend16, wend16=wend16, e=e, wend_e=wend_e,
                elast_e=elast_e, g=g, col=col, row=row)


def _ssd_head_mats(p, h):
    seg = p["acum"][:, h:h + 1] - p["acum_t"][h:h + 1, :]
    lm = jnp.exp(jnp.where(p["mask"], seg, -jnp.inf))
    gl = p["g"] * lm
    s = gl * p["dt_t"][h:h + 1, :]
    return lm, gl, s


def _ssd_fwd(u, xd, bias2, a2, rexp, h0, name, gather=()):
    n = u.shape[0]
    nc = n // CHUNK
    q = CHUNK
    cidx = _ssd_chunk_index(nc, reverse=False)
    ng = len(gather)

    def body(u_ref, xd_ref, bias_ref, a_ref, r_ref, h0_ref, *rest):
        g_ins, (y_ref, hp_ref, hf_ref), rest = rest[:ng], rest[ng:ng + 3], rest[ng + 3:]
        g_outs, st, sems = rest[:ng], rest[ng], rest[ng + 1:]
        d = pl.program_id(0)
        k = pl.program_id(1)
        if ng:
            g_start, g_finish = _gather_steps(g_ins, g_outs, *sems)
            pl.when((d == 0) & (k == 0))(g_start)

        @pl.when(k == 0)
        def _():
            st[...] = h0_ref[...]

        p = _ssd_prologue(d, u_ref, xd_ref, bias_ref, a_ref, r_ref)
        stv = st[...]
        st_bf = stv.astype(BF16)
        hp_ref[...] = st_bf
        xs = p["xs"]
        lane128 = p["col"]
        y_off = _dot(p["cm"].astype(BF16), st_bf) * p["e"]
        for pb in range(SSD_HEADS // 2):
            _, _, s0 = _ssd_head_mats(p, 2 * pb)
            _, _, s1 = _ssd_head_mats(p, 2 * pb + 1)
            xp = xs[:, pb * 128:(pb + 1) * 128]
            rhs = jnp.concatenate([jnp.where(lane128 < HEAD_DIM, xp, 0.0), jnp.where(lane128 >= HEAD_DIM, xp, 0.0)],
                                  axis=0).astype(BF16)
            lhs = jnp.concatenate([s0, s1], axis=1).astype(BF16)
            y_ref[:, pb * 128:(pb + 1) * 128] = (_dot(lhs, rhs) + y_off[:, pb * 128:(pb + 1) * 128]).astype(BF16)
        xw = (xs * p["wend_e"]).astype(BF16)
        new = stv * p["elast_e"] + _dot(p["bm"].T.astype(BF16), xw)
        st[...] = new
        hf_ref[...] = new
        if ng:
            pl.when((d == 1) & (k == nc - 1))(g_finish)

    nsem = _GATHER_SEMS * ng
    return pl.pallas_call(
        body, name=name, grid=(2, nc),
        in_specs=[pl.BlockSpec((q, D_XBC), lambda d, k: (cidx(d, k), 0)),
                  pl.BlockSpec((q, 128), lambda d, k: (cidx(d, k), D_XBC // 128)),
                  pl.BlockSpec((None, 1, 128), lambda d, k: (d, 0, 0)),
                  pl.BlockSpec((None, 1, 128), lambda d, k: (d, 0, 0)),
                  pl.BlockSpec((128, D_MODEL), lambda d, k: (0, 0)),
                  pl.BlockSpec((None, D_STATE, D_MODEL), lambda d, k: (d, 0, 0))] + [_ANY] * ng,
        out_specs=[pl.BlockSpec((None, q, D_MODEL), lambda d, k: (d, cidx(d, k), 0)),
                   pl.BlockSpec((None, None, D_STATE, D_MODEL), lambda d, k: (d, cidx(d, k), 0, 0)),
                   pl.BlockSpec((None, D_STATE, D_MODEL), lambda d, k: (d, 0, 0))] + [_ANY] * ng,
        out_shape=[jax.ShapeDtypeStruct((2, n, D_MODEL), BF16),
                   jax.ShapeDtypeStruct((2, nc, D_STATE, D_MODEL), BF16),
                   jax.ShapeDtypeStruct((2, D_STATE, D_MODEL), F32)] +
                  [jax.ShapeDtypeStruct((4,) + t.shape, t.dtype) for t in gather],
        scratch_shapes=[pltpu.VMEM((D_STATE, D_MODEL), F32)] +
                       ([pltpu.SemaphoreType.DMA((nsem,)), pltpu.SemaphoreType.DMA((nsem,))] if ng else []),
        compiler_params=_cp(("arbitrary", "arbitrary")),
    )(u, xd, bias2, a2, rexp, h0, *gather)


def _ssd_bwd(u, xd, bias2, a2, rexp, rexp_t, dy, hprev, lam0, name, exchange=()):
    n = u.shape[0]
    nc = n // CHUNK
    q = CHUNK
    cidx = _ssd_chunk_index(nc, reverse=True)
    ne = len(exchange)

    def body(u_ref, xd_ref, bias_ref, a_ref, r_ref, rt_ref, dy_ref, hp_ref, lam0_ref, *rest):
        e_ins, (dxs_ref, dbc_ref, ddt_ref, acc_ref, lamo_ref), rest = rest[:ne], rest[ne:ne + 5], rest[ne + 5:]
        e_outs, lam, sems = rest[:ne], rest[ne], rest[ne + 1:]
        d = pl.program_id(0)
        k = pl.program_id(1)
        if ne:
            e_start, e_finish = _exchange_steps(e_ins, e_outs, *sems)
            pl.when((d == 0) & (k == 0))(e_start)

        @pl.when(k == 0)
        def _():
            lam[...] = lam0_ref[...]
            acc_ref[...] = jnp.zeros_like(acc_ref)

        rexp_t = rt_ref[...]

        def hsum(t):
            return _dot_sel_r(t, rexp_t, n=1)

        p = _ssd_prologue(d, u_ref, xd_ref, bias_ref, a_ref, r_ref)
        xs, bm, cm = p["xs"], p["bm"], p["cm"]
        bm_bf, cm_bf = bm.astype(BF16), cm.astype(BF16)
        lamn = lam[...]
        lamn_bf = lamn.astype(BF16)
        stp = hp_ref[...]
        dyv = dy_ref[...].astype(F32)
        lane128 = p["col"]

        wend_e = p["wend_e"]
        cs = _dot(cm_bf, stp)
        dye_bf = (dyv * p["e"]).astype(BF16)
        dc_off = _dot_nt(dye_bf, stp)
        v = _dot(bm_bf, lamn_bf)
        xw_bf = (xs * wend_e).astype(BF16)
        db_off = _dot_nt(xw_bf, lamn_bf)
        elast_e = p["elast_e"]
        dlast_e = jnp.sum(stp.astype(F32) * lamn, axis=0, keepdims=True) * elast_e
        lam_new = lamn * elast_e + _dot(cm.T.astype(BF16), dye_bf)
        lam[...] = lam_new
        lamo_ref[...] = lam_new

        hs_vx = hsum(v * xs)
        om = p["wend16"] * hs_vx
        x1 = p["e16"] * hsum(dyv * cs) - om
        x2 = p["dend16"] * hs_vx
        x3 = jnp.sum(om, axis=0, keepdims=True) + _dot_sel_r(jnp.broadcast_to(dlast_e, (8, D_MODEL)), rexp_t, n=2)[0:1, :]

        sub16 = lax.broadcasted_iota(jnp.int32, (SSD_HEADS, q), 0)
        rs = jnp.zeros((q, 128), F32)
        cs_m = jnp.zeros((SSD_HEADS, q), F32)
        dt_m = jnp.zeros((SSD_HEADS, q), F32)
        dg = jnp.zeros((q, q), F32)
        for pb in range(SSD_HEADS // 2):
            xp_bf = xs[:, pb * 128:(pb + 1) * 128].astype(BF16)
            dyp = dyv[:, pb * 128:(pb + 1) * 128]
            dxs_pair = None
            for half in range(2):
                h = 2 * pb + half
                sel = (lane128 < HEAD_DIM) if half == 0 else (lane128 >= HEAD_DIM)
                dyh_bf = jnp.where(sel, dyp, 0.0).astype(BF16)
                lm, gl, s = _ssd_head_mats(p, h)
                ds = _dot_nt(dyh_bf, xp_bf)
                t = _dot_tn(s.astype(BF16), dyh_bf)
                dxs_pair = t if dxs_pair is None else dxs_pair + t
                w = ds * s
                rs = rs + jnp.sum(w, axis=1, keepdims=True) * (lane128 == h).astype(F32)
                cs_m = jnp.where(sub16 == h, jnp.sum(w, axis=0, keepdims=True), cs_m)
                dt_m = jnp.where(sub16 == h, jnp.sum(ds * gl, axis=0, keepdims=True), dt_m)
                dg = dg + ds * lm * p["dt_t"][h:h + 1, :]
            sl = slice(pb * 128, (pb + 1) * 128)
            dxs_ref[:, sl] = (dxs_pair + v[:, sl] * wend_e[:, sl]).astype(BF16)

        def to_lanes(m16):
            return jnp.concatenate([m16, jnp.zeros((128 - SSD_HEADS, q), F32)], axis=0).T

        last = jnp.where(d == 0, q - 1, 0)
        dacum = rs - to_lanes(cs_m) + x1 + jnp.where(p["row"] == last, x3[0:1, :], 0.0)
        tri_t = jnp.where(p["mask_t"], 1.0, 0.0).astype(BF16)
        ddta = _dot_sel_l(tri_t, dacum)
        dt = p["dt"]
        a = p["a"]
        ddt = to_lanes(dt_m) + x2 + a * ddta
        ddtraw = jnp.where(p["head_lane"], ddt * _sigmoid(p["dtraw"]), 0.0)
        ddt_ref[...] = ddtraw
        acc_ref[0:1, :] += jnp.sum(ddtraw, axis=0, keepdims=True)
        acc_ref[1:2, :] += jnp.sum(dt * ddta, axis=0, keepdims=True) * a

        dg_bf = dg.astype(BF16)
        dbc_ref[:, 0:D_STATE] = _dot_tn(dg_bf, cm_bf) + db_off
        dbc_ref[:, D_STATE:2 * D_STATE] = _dot(dg_bf, bm_bf) + dc_off
        if ne:
            pl.when((d == 1) & (k == nc - 1))(e_finish)

    cblk = lambda d, k: (cidx(d, k), 0)
    return pl.pallas_call(
        body, name=name, grid=(2, nc),
        in_specs=[pl.BlockSpec((q, D_XBC), cblk),
                  pl.BlockSpec((q, 128), lambda d, k: (cidx(d, k), D_XBC // 128)),
                  pl.BlockSpec((None, 1, 128), lambda d, k: (d, 0, 0)),
                  pl.BlockSpec((None, 1, 128), lambda d, k: (d, 0, 0)),
                  pl.BlockSpec((128, D_MODEL), lambda d, k: (0, 0)),
                  pl.BlockSpec((D_MODEL, 128), lambda d, k: (0, 0)),
                  pl.BlockSpec((q, D_MODEL), cblk),
                  pl.BlockSpec((None, None, D_STATE, D_MODEL), lambda d, k: (d, cidx(d, k), 0, 0)),
                  pl.BlockSpec((None, D_STATE, D_MODEL), lambda d, k: (d, 0, 0))] + [_ANY] * ne,
        out_specs=[pl.BlockSpec((None, q, D_MODEL), lambda d, k: (d, cidx(d, k), 0)),
                   pl.BlockSpec((None, q, 2 * D_STATE), lambda d, k: (d, cidx(d, k), 0)),
                   pl.BlockSpec((None, q, 128), lambda d, k: (d, cidx(d, k), 0)),
                   pl.BlockSpec((None, 8, 128), lambda d, k: (d, 0, 0)),
                   pl.BlockSpec((None, D_STATE, D_MODEL), lambda d, k: (d, 0, 0))] + [_ANY] * ne,
        out_shape=[jax.ShapeDtypeStruct((2, n, D_MODEL), BF16),
                   jax.ShapeDtypeStruct((2, n, 2 * D_STATE), F32),
                   jax.ShapeDtypeStruct((2, n, 128), F32),
                   jax.ShapeDtypeStruct((2, 8, 128), F32),
                   jax.ShapeDtypeStruct((2, D_STATE, D_MODEL), F32)] +
                  [jax.ShapeDtypeStruct(t.shape, t.dtype) for t in exchange],
        scratch_shapes=[pltpu.VMEM((D_STATE, D_MODEL), F32)] +
                       ([pltpu.SemaphoreType.DMA((3 * ne,)), pltpu.SemaphoreType.DMA((3 * ne,)),
                         pltpu.SemaphoreType.DMA((ne,))] if ne else []),
        compiler_params=_cp(("arbitrary", "arbitrary")),
    )(u, xd, bias2, a2, rexp, rexp_t, dy, hprev, lam0, *exchange)


def _merge_fwd(y, u, z, dskip_e, gn, tb):
    n = z.shape[0]

    def body(y_ref, u_ref, z_ref, sk_ref, gn_ref, o_ref):
        sk, gnv = sk_ref[...], gn_ref[...]
        for r in _row_tiles(tb):
            ys = y_ref[0, r, :].astype(F32) + y_ref[1, r, :].astype(F32) + sk * _silu(u_ref[r, :].astype(F32))
            gated = ys * _silu(z_ref[r, :].astype(F32))
            rstd = lax.rsqrt(jnp.mean(gated * gated, axis=-1, keepdims=True) + LN_EPS)
            o_ref[r, :] = (gated * rstd * gnv).astype(BF16)

    return pl.pallas_call(
        body, name="merge_fwd", grid=(n // tb,),
        in_specs=[pl.BlockSpec((2, tb, D_MODEL), lambda i: (0, i, 0)), pl.BlockSpec((tb, D_MODEL), lambda i: (i, 0)),
                  _row_spec(tb, D_MODEL), _par_spec(D_MODEL), _par_spec(D_MODEL)],
        out_specs=_row_spec(tb, D_MODEL),
        out_shape=jax.ShapeDtypeStruct((n, D_MODEL), BF16),
        compiler_params=_cp(("parallel",)),
    )(y, u, z, dskip_e, gn)


def _merge_bwd(dyn, y, u, z, dskip_e, gn, tb):
    n = z.shape[0]

    def body(dyn_ref, y_ref, u_ref, z_ref, sk_ref, gn_ref, dy_ref, dz_ref, acc_ref):
        i = pl.program_id(0)

        @pl.when(i == 0)
        def _():
            acc_ref[...] = jnp.zeros_like(acc_ref)

        sk, gnv = sk_ref[...], gn_ref[...]
        part0 = jnp.zeros((8, D_MODEL), F32)
        part1 = jnp.zeros((8, D_MODEL), F32)
        for s in range(tb // ROW_SUB):
            r = slice(s * ROW_SUB, (s + 1) * ROW_SUB)
            xs = _silu(u_ref[r, :].astype(F32))
            zv = z_ref[r, :].astype(F32)
            sz = _sigmoid(zv)
            ys = y_ref[0, r, :].astype(F32) + y_ref[1, r, :].astype(F32) + sk * xs
            gated = ys * (zv * sz)
            rstd = lax.rsqrt(jnp.mean(gated * gated, axis=-1, keepdims=True) + LN_EPS)
            ghat = gated * rstd
            dyn_v = dyn_ref[r, :].astype(F32)
            t = dyn_v * gnv
            dgated = rstd * (t - ghat * jnp.mean(t * ghat, axis=-1, keepdims=True))
            dys = dgated * (zv * sz)
            dy_ref[r, :] = dys.astype(BF16)
            dz_ref[r, :] = (dgated * ys * (sz * (1.0 + zv * (1.0 - sz)))).astype(BF16)
            part0 = part0 + _fold8(dyn_v * ghat)
            part1 = part1 + _fold8(dys * xs)
        acc_ref[0:1, :] += jnp.sum(part0, axis=0, keepdims=True)
        acc_ref[1:2, :] += jnp.sum(part1, axis=0, keepdims=True)

    return pl.pallas_call(
        body, name="merge_bwd", grid=(n // tb,),
        in_specs=[_row_spec(tb, D_MODEL), pl.BlockSpec((2, tb, D_MODEL), lambda i: (0, i, 0)),
                  pl.BlockSpec((tb, D_MODEL), lambda i: (i, 0)), _row_spec(tb, D_MODEL),
                  _par_spec(D_MODEL), _par_spec(D_MODEL)],
        out_specs=[_row_spec(tb, D_MODEL), _row_spec(tb, D_MODEL), _acc_spec(D_MODEL)],
        out_shape=[jax.ShapeDtypeStruct((n, D_MODEL), BF16), jax.ShapeDtypeStruct((n, D_MODEL), BF16),
                   jax.ShapeDtypeStruct((8, D_MODEL), F32)],
        compiler_params=_cp(("arbitrary",)),
    )(dyn, y, u, z, dskip_e, gn)


def _pool_consts(transpose):
    tb = POOL_TB
    t = jnp.arange(tb)
    s = jnp.arange(3 * tb)
    rl, cl = t // GRID_W, t % GRID_W
    rs_, cs_ = s // GRID_W - tb // GRID_W, s % GRID_W
    s2 = jnp.arange(tb)
    rl2, cl2 = s2 // GRID_W, s2 % GRID_W
    brow, bcol = [], []
    for w in POOL_WINDOWS:
        lo, hi = -(w // 2), w - w // 2
        if transpose:
            lo, hi = -hi + 1, -lo + 1
        dr = rs_[None, :] - rl[:, None]
        before, after = _pool_halo(w, transpose)
        full = ((cs_[None, :] == cl[:, None]) & (dr >= lo) & (dr < hi)).astype(BF16)
        brow.append(full[:, tb - before:2 * tb + after])
        dc = cl2[None, :] - cl[:, None]
        bcol.append(((rl2[None, :] == rl[:, None]) & (dc >= lo) & (dc < hi)).astype(BF16))
    return brow, jnp.stack(bcol)


def _pool_halo(w, transpose):
    lo, hi = -(w // 2), w - w // 2
    if transpose:
        lo, hi = -hi + 1, -lo + 1
    return -lo * GRID_W, (hi - 1) * GRID_W


def _pool_inv(i, g, n):
    assert GRID_W == 64
    t = i * POOL_TB + lax.broadcasted_iota(jnp.int32, (POOL_TB, 1), 0)
    r = lax.shift_right_logical(t, 6)
    col = t & (GRID_W - 1)
    w = POOL_WINDOWS[g]
    lo, hi = -(w // 2), w - w // 2
    cnt_r = jnp.minimum(r + hi, n // GRID_W) - jnp.maximum(r + lo, 0)
    cnt_c = jnp.minimum(col + hi, GRID_W) - jnp.maximum(col + lo, 0)
    return 1.0 / (cnt_r * cnt_c).astype(F32)


def _pool_box(prev_ref, cur_ref, next_ref, brow_refs, bcol_ref, g, i, nb, transpose):
    tb = POOL_TB
    sl = slice(g * POOL_DIM, (g + 1) * POOL_DIM)
    before, after = _pool_halo(POOL_WINDOWS[g], transpose)
    pieces = []
    if before:
        pieces.append((prev_ref[tb - before:tb, sl] * (i > 0).astype(prev_ref.dtype)).astype(BF16))
    pieces.append(cur_ref[:, sl].astype(BF16))
    if after:
        pieces.append((next_ref[0:after, sl] * (i < nb - 1).astype(next_ref.dtype)).astype(BF16))
    r = _dot(brow_refs[g][...], jnp.concatenate(pieces, axis=0))
    return _dot(bcol_ref[g], r.astype(BF16))


def _pool_halo_specs(n, d):
    tb = POOL_TB
    nb = n // tb
    prev = pl.BlockSpec((tb, d), lambda i: (jnp.maximum(i - 1, 0), 0))
    cur = pl.BlockSpec((tb, d), lambda i: (i, 0))
    nxt = pl.BlockSpec((tb, d), lambda i: (jnp.minimum(i + 1, nb - 1), 0))
    return prev, cur, nxt


def _pool_const_specs(brow):
    tb = POOL_TB
    return [pl.BlockSpec(b.shape, lambda i: (0, 0)) for b in brow] + [pl.BlockSpec((N_POOL, tb, tb), lambda i: (0, 0, 0))]


def _pool_fwd(up, consts, pw_bf, pscale):
    n = up.shape[0]
    tb = POOL_TB
    nb = n // tb
    brow, bcol = consts
    prev, cur, nxt = _pool_halo_specs(n, D_MODEL)

    def body(p_ref, c_ref, n_ref, *rest):
        brow_refs, (bcol_ref, pw_ref, sc_ref, o_ref, d_ref) = rest[:N_POOL], rest[N_POOL:]
        i = pl.program_id(0)
        for g in range(N_POOL):
            sl = slice(g * POOL_DIM, (g + 1) * POOL_DIM)
            box = _pool_box(p_ref, c_ref, n_ref, brow_refs, bcol_ref, g, i, nb, False)
            dd = (box * _pool_inv(i, g, n) - c_ref[:, sl].astype(F32)).astype(BF16)
            d_ref[:, sl] = dd
            o_ref[:, sl] = (_dot(dd, pw_ref[g]) * sc_ref[:, sl]).astype(BF16)

    return pl.pallas_call(
        body, name="pool_fwd", grid=(nb,),
        in_specs=[prev, cur, nxt] + _pool_const_specs(brow) +
                 [pl.BlockSpec((N_POOL, POOL_DIM, POOL_DIM), lambda i: (0, 0, 0)), _par_spec(D_MODEL)],
        out_specs=[_row_spec(tb, D_MODEL), _row_spec(tb, D_MODEL)],
        out_shape=[jax.ShapeDtypeStruct((n, D_MODEL), BF16), jax.ShapeDtypeStruct((n, D_MODEL), BF16)],
        compiler_params=_cp(("parallel",)),
    )(up, up, up, *brow, bcol, pw_bf, pscale)


def _pool_bwd_a(dp, dsave, pw_bf, pwt_bf, pscale):
    n = dp.shape[0]
    tb = POOL_TB

    def body(dp_ref, d_ref, pw_ref, pwt_ref, sc_ref, dd_ref, dds_ref, gw_ref, gs_ref):
        i = pl.program_id(0)

        @pl.when(i == 0)
        def _():
            gw_ref[...] = jnp.zeros_like(gw_ref)
            gs_ref[...] = jnp.zeros_like(gs_ref)

        for g in range(N_POOL):
            sl = slice(g * POOL_DIM, (g + 1) * POOL_DIM)
            dpv = dp_ref[:, sl].astype(F32)
            dv = d_ref[:, sl]
            dpw_bf = (dpv * sc_ref[:, sl]).astype(BF16)
            dd = _dot(dpw_bf, pwt_ref[g])
            dd_ref[:, sl] = dd.astype(BF16)
            dds_ref[:, sl] = (dd * _pool_inv(i, g, n)).astype(BF16)
            gw_ref[g] += _dot_tn(dv, dpw_bf)
            gs_ref[0:1, sl] += jnp.sum(dpv * _dot(dv, pw_ref[g]), axis=0, keepdims=True)

    wspec = pl.BlockSpec((N_POOL, POOL_DIM, POOL_DIM), lambda i: (0, 0, 0))
    return pl.pallas_call(
        body, name="pool_bwd_a", grid=(n // tb,),
        in_specs=[_row_spec(tb, D_MODEL), _row_spec(tb, D_MODEL), wspec, wspec, _par_spec(D_MODEL)],
        out_specs=[_row_spec(tb, D_MODEL), _row_spec(tb, D_MODEL), wspec, _acc_spec(D_MODEL)],
        out_shape=[jax.ShapeDtypeStruct((n, D_MODEL), BF16), jax.ShapeDtypeStruct((n, D_MODEL), BF16),
                   jax.ShapeDtypeStruct((N_POOL, POOL_DIM, POOL_DIM), F32), jax.ShapeDtypeStruct((8, D_MODEL), F32)],
        compiler_params=_cp(("arbitrary",)),
    )(dp, dsave, pw_bf, pwt_bf, pscale)


def _pool_bwd_b(dds, dd, consts_t):
    n = dd.shape[0]
    tb = POOL_TB
    nb = n // tb
    brow, bcol = consts_t
    prev, cur, nxt = _pool_halo_specs(n, D_MODEL)

    def body(p_ref, c_ref, n_ref, *rest):
        brow_refs, (bcol_ref, dd_ref, o_ref) = rest[:N_POOL], rest[N_POOL:]
        i = pl.program_id(0)
        for g in range(N_POOL):
            sl = slice(g * POOL_DIM, (g + 1) * POOL_DIM)
            box = _pool_box(p_ref, c_ref, n_ref, brow_refs, bcol_ref, g, i, nb, True)
            o_ref[:, sl] = (box - dd_ref[:, sl].astype(F32)).astype(BF16)

    return pl.pallas_call(
        body, name="pool_bwd_b", grid=(nb,),
        in_specs=[prev, cur, nxt] + _pool_const_specs(brow) + [_row_spec(tb, D_MODEL)],
        out_specs=_row_spec(tb, D_MODEL),
        out_shape=jax.ShapeDtypeStruct((n, D_MODEL), BF16),
        compiler_params=_cp(("parallel",)),
    )(dds, dds, dds, *brow, bcol, dd)


def _pair_add(slabs, recvs, core, name):
    na = len(slabs)
    hr = [t.shape[1] // 4 for t in slabs]

    def body(core_ref, *refs):
        for a in range(na):
            refs[2 * na + a][...] = (refs[a][...] + refs[na + a][...]).astype(BF16)

    own = [pl.BlockSpec((None, hr[a], slabs[a].shape[2]), lambda j, i, c_ref: (j, 2 * c_ref[0] + i, 0)) for a in range(na)]
    got = [pl.BlockSpec((None, hr[a], slabs[a].shape[2]), lambda j, i, c_ref: (j, i, 0)) for a in range(na)]
    return pl.pallas_call(
        body, name=name,
        grid_spec=pltpu.PrefetchScalarGridSpec(num_scalar_prefetch=1, grid=(4, 2), in_specs=own + got, out_specs=got),
        out_shape=[jax.ShapeDtypeStruct(r.shape, BF16) for r in recvs],
        compiler_params=_cp(("arbitrary", "arbitrary")),
    )(core, *slabs, *recvs)


def _sum4(parts, core):
    na = len(parts)
    hr = [t.shape[1] // 2 for t in parts]

    def body(core_ref, *refs):
        for a in range(na):
            p = refs[a]
            refs[na + a][...] = ((p[0].astype(F32) + p[1].astype(F32)) + p[2].astype(F32)) + p[3].astype(F32)

    return pl.pallas_call(
        body, name="reduce_g_sum",
        grid_spec=pltpu.PrefetchScalarGridSpec(
            num_scalar_prefetch=1, grid=(2,),
            in_specs=[pl.BlockSpec((4, hr[a], parts[a].shape[2]), lambda i, c_ref: (0, i, 0)) for a in range(na)],
            out_specs=[pl.BlockSpec((hr[a], parts[a].shape[2]), lambda i, c_ref: (2 * c_ref[0] + i, 0))
                       for a in range(na)]),
        out_shape=[jax.ShapeDtypeStruct((2 * t.shape[1], t.shape[2]), F32) for t in parts],
        compiler_params=_cp(("arbitrary",)),
    )(core, *parts)


def _adamw(w, g, m, v, name):
    r, cdim = w.shape
    tb = _row_block(r, 256)
    c1 = 1.0 - ADAM_B1 ** ADAM_STEP
    c2 = 1.0 - ADAM_B2 ** ADAM_STEP

    def body(w_ref, g_ref, m_ref, v_ref, d_ref, nm_ref, nv_ref):
        gv = g_ref[...]
        nm = ADAM_B1 * m_ref[...] + (1.0 - ADAM_B1) * gv
        nv = ADAM_B2 * v_ref[...] + (1.0 - ADAM_B2) * (gv * gv)
        m_hat = nm / c1
        v_hat = nv / c2
        d_ref[...] = -ADAM_LR * (m_hat / (jnp.sqrt(v_hat) + ADAM_EPS) + ADAM_WD * w_ref[...])
        nm_ref[...] = nm
        nv_ref[...] = nv

    spec = _row_spec(tb, cdim)
    shp = jax.ShapeDtypeStruct((r, cdim), F32)
    return pl.pallas_call(
        body, name=name, grid=(r // tb,),
        in_specs=[spec] * 4, out_specs=[spec] * 3, out_shape=[shp] * 3,
        compiler_params=_cp(("parallel",)),
    )(w, g, m, v)


def _mesh_pos():
    return lax.axis_index("x"), lax.axis_index("y"), lax.axis_index("c")


_ANY = pl.BlockSpec(memory_space=pl.ANY)


def _remote(src, dst, send_sem, recv_sem, device):
    return pltpu.make_async_remote_copy(src_ref=src, dst_ref=dst, send_sem=send_sem, recv_sem=recv_sem,
                                        device_id=device, device_id_type=MESH)


def _other_chips(x, y):
    return [(1 - x, y), (x, 1 - y), (1 - x, 1 - y)]


def _half(nrows, h):
    return pl.ds(h * (nrows // 2), nrows // 2)


_GATHER_SEMS = 7


def _gather_steps(ins, outs, send_sems, recv_sems):
    na = len(ins)
    nrow = [r.shape[0] for r in ins]

    def copies():
        x, y, c = _mesh_pos()
        me = 2 * x + y
        sib = (x, y, 1 - c)
        chips = _other_chips(x, y)

        def ici(k, a, slot):
            px, py = chips[k]
            rows = _half(nrow[a], c)
            return _remote(ins[a].at[rows, :], outs[a].at[slot, rows, :], send_sems.at[k * na + a],
                           recv_sems.at[k * na + a], (px, py, c))

        def fwd(k, a, h):
            px, py = chips[k]
            blk = outs[a].at[2 * px + py, _half(nrow[a], h), :]
            return _remote(blk, blk, send_sems.at[(3 + k) * na + a], recv_sems.at[(3 + k) * na + a], sib)

        def own(a):
            return _remote(ins[a], outs[a].at[me], send_sems.at[6 * na + a], recv_sems.at[6 * na + a], sib)

        slots = [2 * px + py for px, py in chips]
        return ici, fwd, own, me, c, slots

    def start():
        ici, _, own, me, _, _ = copies()
        for a in range(na):
            own(a).start()
        for k in range(3):
            for a in range(na):
                ici(k, a, me).start()

    def finish():
        ici, fwd, own, me, c, slots = copies()
        for k in range(3):
            for a in range(na):
                ici(k, a, slots[k]).wait_recv()
                fwd(k, a, c).start()
        for k in range(3):
            for a in range(na):
                fwd(k, a, 1 - c).wait_recv()
        for a in range(na):
            own(a).wait_recv()
        for a in range(na):
            own(a).wait_send()
        for k in range(3):
            for a in range(na):
                ici(k, a, me).wait_send()
                fwd(k, a, c).wait_send()

    return start, finish


def _exchange_steps(ins, outs, send_sems, recv_sems, local_sems):
    na = len(ins)

    def copies():
        x, y, c = _mesh_pos()
        me = 2 * x + y
        chips = _other_chips(x, y)

        def copy(k, a, slot):
            px, py = chips[k]
            return _remote(ins[a].at[2 * px + py], outs[a].at[slot], send_sems.at[k * na + a], recv_sems.at[k * na + a],
                           (px, py, c))

        def local(a):
            return pltpu.make_async_copy(ins[a].at[me], outs[a].at[me], local_sems.at[a])

        return copy, local, me, [2 * px + py for px, py in chips]

    def start():
        copy, local, me, _ = copies()
        for a in range(na):
            local(a).start()
        for k in range(3):
            for a in range(na):
                copy(k, a, me).start()

    def finish():
        copy, local, me, slots = copies()
        for k in range(3):
            for a in range(na):
                copy(k, a, slots[k]).wait_recv()
        for k in range(3):
            for a in range(na):
                copy(k, a, me).wait_send()
        for a in range(na):
            local(a).wait()

    return start, finish


def _gather_weights(shards, conv8):
    na = len(shards)

    def body(*refs):
        ins, conv_in = refs[:na], refs[na]
        outs, conv_out = refs[na + 1:2 * na + 1], refs[2 * na + 1]
        send_sems, recv_sems, local_sems = refs[2 * na + 2:]
        x, y, c = _mesh_pos()
        me = 2 * x + y
        chips = _other_chips(x, y)

        def conv(k, slot):
            px, py = chips[k]
            return _remote(conv_in, conv_out.at[slot], send_sems.at[7 * na + k], recv_sems.at[7 * na + k], (px, py, c))

        start, finish = _gather_steps(ins, outs, send_sems, recv_sems)
        local = pltpu.make_async_copy(conv_in, conv_out.at[me], local_sems.at[0])
        local.start()
        start()
        sends = [conv(k, me) for k in range(3)]
        for cp in sends:
            cp.start()
        finish()
        for k in range(3):
            px, py = chips[k]
            conv(k, 2 * px + py).wait_recv()
        for cp in sends:
            cp.wait_send()
        local.wait()

    nsem = _GATHER_SEMS * na + 3
    return pl.pallas_call(
        body, name="gather_w", in_specs=[_ANY] * (na + 1), out_specs=[_ANY] * (na + 1),
        out_shape=[jax.ShapeDtypeStruct((4,) + t.shape, t.dtype) for t in shards] +
                  [jax.ShapeDtypeStruct((4,) + conv8.shape, conv8.dtype)],
        scratch_shapes=[pltpu.SemaphoreType.DMA((nsem,)), pltpu.SemaphoreType.DMA((nsem,)),
                        pltpu.SemaphoreType.DMA((1,))],
    )(*shards, conv8)


def _pair_swap(slabs, name):
    na = len(slabs)

    def body(*refs):
        ins, outs = refs[:na], refs[na:2 * na]
        send_sems, recv_sems = refs[2 * na:]
        x, y, c = _mesh_pos()
        cps = [_remote(ins[a].at[:, _half(slabs[a].shape[1], 1 - c), :], outs[a], send_sems.at[a], recv_sems.at[a],
                       (x, y, 1 - c)) for a in range(na)]
        for cp in cps:
            cp.start()
        for cp in cps:
            cp.wait()

    return pl.pallas_call(
        body, name=name, in_specs=[_ANY] * na, out_specs=[_ANY] * na,
        out_shape=[jax.ShapeDtypeStruct((4, t.shape[1] // 2, t.shape[2]), t.dtype) for t in slabs],
        scratch_shapes=[pltpu.SemaphoreType.DMA((na,)), pltpu.SemaphoreType.DMA((na,))],
    )(*slabs)


def _chip_exchange(pairs):
    na = len(pairs)

    def body(*refs):
        start, finish = _exchange_steps(refs[:na], refs[na:2 * na], *refs[2 * na:])
        start()
        finish()

    return pl.pallas_call(
        body, name="reduce_g_ici", in_specs=[_ANY] * na, out_specs=[_ANY] * na,
        out_shape=[jax.ShapeDtypeStruct(t.shape, t.dtype) for t in pairs],
        scratch_shapes=[pltpu.SemaphoreType.DMA((3 * na,)), pltpu.SemaphoreType.DMA((3 * na,)),
                        pltpu.SemaphoreType.DMA((na,))],
    )(*pairs)


def _share_halves(totals):
    na = len(totals)

    def body(*refs):
        bufs = refs[na:2 * na]
        send_sems, recv_sems = refs[2 * na:]
        x, y, c = _mesh_pos()

        def copy(a, h):
            blk = bufs[a].at[_half(totals[a].shape[0], h), :]
            return _remote(blk, blk, send_sems.at[a], recv_sems.at[a], (x, y, 1 - c))

        sends = [copy(a, c) for a in range(na)]
        for cp in sends:
            cp.start()
        for a in range(na):
            copy(a, 1 - c).wait_recv()
        for cp in sends:
            cp.wait_send()

    return pl.pallas_call(
        body, name="reduce_g_share", in_specs=[_ANY] * na, out_specs=[_ANY] * na,
        out_shape=[jax.ShapeDtypeStruct(t.shape, t.dtype) for t in totals],
        input_output_aliases={a: a for a in range(na)},
        scratch_shapes=[pltpu.SemaphoreType.DMA((na,)), pltpu.SemaphoreType.DMA((na,))],
    )(*totals)


def _allreduce_small(v, name):
    r, cdim = v.shape

    def body(v_ref, out_ref, buf, send_sems, recv_sems):
        x, y, c = _mesh_pos()
        me = 4 * x + 2 * y + c
        buf[me] = v_ref[...]
        rel = [(bx, by, bc) for bx in (0, 1) for by in (0, 1) for bc in (0, 1)][1:]

        def peer(b):
            bx, by, bc = b
            return ((1 - x) if bx else x, (1 - y) if by else y, (1 - c) if bc else c)

        def copy(k, slot):
            return pltpu.make_async_remote_copy(
                src_ref=v_ref, dst_ref=buf.at[slot], send_sem=send_sems.at[k], recv_sem=recv_sems.at[k],
                device_id=peer(rel[k]), device_id_type=MESH)

        sends = [copy(k, me) for k in range(7)]
        for cp in sends:
            cp.start()
        for k in range(7):
            px, py, pc = peer(rel[k])
            copy(k, 4 * px + 2 * py + pc).wait_recv()
        for cp in sends:
            cp.wait_send()
        acc = buf[0]
        for j in range(1, 8):
            acc = acc + buf[j]
        out_ref[...] = acc

    vm = pl.BlockSpec(memory_space=pltpu.VMEM)
    return pl.pallas_call(
        body, name=name, in_specs=[vm], out_specs=[vm, vm],
        out_shape=[jax.ShapeDtypeStruct((r, cdim), F32), jax.ShapeDtypeStruct((8, r, cdim), F32)],
        scratch_shapes=[pltpu.SemaphoreType.DMA((7,)), pltpu.SemaphoreType.DMA((7,))],
    )(v)


def _chip_bcast(v, name):
    def body(v_ref, out_ref, send_sems, recv_sems):
        x, y, c = _mesh_pos()
        me = 2 * x + y
        chips = _other_chips(x, y)
        out_ref[me] = v_ref[...]

        def copy(k, slot):
            px, py = chips[k]
            return _remote(v_ref, out_ref.at[slot], send_sems.at[k], recv_sems.at[k], (px, py, c))

        sends = [copy(k, me) for k in range(3)]
        for cp in sends:
            cp.start()
        for k, (px, py) in enumerate(chips):
            copy(k, 2 * px + py).wait_recv()
        for cp in sends:
            cp.wait_send()

    vm = pl.BlockSpec(memory_space=pltpu.VMEM)
    return pl.pallas_call(
        body, name=name, in_specs=[vm], out_specs=vm,
        out_shape=jax.ShapeDtypeStruct((4,) + v.shape, F32),
        scratch_shapes=[pltpu.SemaphoreType.DMA((3,)), pltpu.SemaphoreType.DMA((3,))],
    )(v)


_BIG = (("in_proj", (D_MODEL, D_IN_PROJ // 4), 1), ("w_out", (2 * D_MODEL // 4, D_MODEL), 0),
        ("w_gate", (D_MODEL, D_FF // 4), 1), ("w_up", (D_MODEL, D_FF // 4), 1), ("w_down", (D_FF // 4, D_MODEL), 0),
        ("pool_w", (N_POOL * POOL_DIM // 4, POOL_DIM), None))


def _assemble(name, t):
    _, r, c = t.shape
    axis = {n: ax for n, _, ax in _BIG}[name]
    if axis == 0:
        return t.reshape(4 * r, c)
    if axis == 1:
        return t.transpose(1, 0, 2).reshape(r, 4 * c)
    return t.reshape(4, N_POOL, POOL_DIM // 4, POOL_DIM).transpose(1, 0, 2, 3).reshape(N_POOL, POOL_DIM, POOL_DIM)


def _to_slabs(name, g):
    (r, c), axis = {n: (sh, ax) for n, sh, ax in _BIG}[name]
    if axis == 0:
        return g.reshape(4, r, c)
    if axis == 1:
        return g.reshape(r, 4, c).transpose(1, 0, 2)
    return g.reshape(N_POOL, 4, POOL_DIM // 4, POOL_DIM).transpose(1, 0, 2, 3).reshape(4, r, c)


_EARLY = ("in_proj",)
_LATE = tuple(n for n, _, _ in _BIG if n not in _EARLY)


def _reduce_grads(early_grads, late_parts, core):
    slabs = [_to_slabs(n, early_grads[n]) for n in _EARLY]
    pairs = _pair_add(slabs, _pair_swap(slabs, "reduce_g_d2d"), core, "reduce_g_pair")
    parts = dict(zip(_EARLY, _chip_exchange(pairs)), **dict(zip(_LATE, late_parts)))
    names = [n for n, _, _ in _BIG]
    totals = _sum4([parts[n] for n in names], core)
    return dict(zip(names, _share_halves(totals)))


def _pad_cols(w, n):
    return jnp.concatenate([w, jnp.zeros((w.shape[0], n - w.shape[1]), w.dtype)], axis=1)


def _device_step(x, mod, mod_ctx, ctx, target, wts, w8, small, tb, late_shards=None, core=None):
    n = x.shape[0]
    d = D_MODEL

    win = wts["in_proj"]
    wz, wxd, wup = win[:, 0:d], _pad_cols(win[:, d:d + D_XBC + 2 * SSD_HEADS], D_XD), win[:, d + D_XBC + 2 * SSD_HEADS:]

    emb_g, emb_b = _vec(small["emb_ln_g"]), _vec(small["emb_ln_b"])
    ln1_g, ln1_b = _vec(small["ln1_g"]), _vec(small["ln1_b"])
    ln2_g, ln2_b = _vec(small["ln2_g"]), _vec(small["ln2_b"])
    gn = _vec(small["ssd_norm_g"])
    pscale = _vec(small["pool_scale"])
    conv_b = _vec(small["conv_b"])
    dskip_e = jnp.repeat(small["d_skip"].reshape(-1), HEAD_DIM).reshape(1, d)
    zpad = jnp.zeros((2, 1, 128 - SSD_HEADS), F32)
    bias2 = jnp.concatenate([small["dt_bias"].reshape(2, 1, SSD_HEADS), zpad], axis=2)
    a2 = jnp.concatenate([-jnp.exp(small["a_log"].reshape(2, 1, SSD_HEADS)), zpad], axis=2)
    rexp = (jnp.arange(128)[:, None] == (jnp.arange(d)[None, :] // HEAD_DIM)).astype(BF16)
    rexp_t = rexp.T

    sh1, sc1, g1, sh2, sc2, g2 = [mod[:, i * d:(i + 1) * d] for i in range(6)]
    sh1c, sc1c = mod_ctx[:, 0:d], mod_ctx[:, d:2 * d]

    tbc = min(tb, ctx.shape[0])
    xc0, hc = _ln_mod(ctx, emb_g, emb_b, sh1c, sc1c, tbc, "ln_mod_ctx")
    xdc = _matmul_nn([(hc, wxd)], F32, 512, D_XD, "in_proj_ctx")
    uc = _conv_fwd(xdc, w8, conv_b, tbc, "conv_fwd_ctx")
    hzero = jnp.zeros((2, D_STATE, d), F32)
    _, hprev_c, hfin_c = _ssd_fwd(uc, xdc, bias2, a2, rexp, hzero, "ssd_fwd_ctx")

    x0, h1 = _ln_mod(x, emb_g, emb_b, sh1, sc1, tb, "ln_mod")
    z, xd, up = _matmul_nn_multi(h1, [wz, wxd, wup], [BF16, F32, BF16], 512, "in_proj")
    u = _conv_fwd(xd, w8, conv_b, tb, "conv_fwd")
    y, hprev, _, *landed = _ssd_fwd(u, xd, bias2, a2, rexp, hfin_c, "ssd_fwd", gather=late_shards or ())
    if late_shards is not None:
        wts = dict(wts, **{nme: _assemble(nme, t) for nme, t in zip(_LATE, landed)})
    wout = wts["w_out"]
    wg, wu, wd = wts["w_gate"], wts["w_up"], wts["w_down"]
    pw = wts["pool_w"]
    yn = _merge_fwd(y, u, z, dskip_e, gn, tb)
    pconst = _pool_consts(False)
    pool, dsave = _pool_fwd(up, pconst, pw, pscale)
    mix = _matmul_nn([(yn, wout[0:d]), (pool, wout[d:2 * d])], BF16, MM_ROWS, 1024, "out_proj")
    x1, h2 = _res_ln(x0, mix, g1, ln1_g, ln1_b, sh2, sc2, tb)

    gate, upp, hmid = _swiglu_fwd(h2, wg, wu, 512, D_FF // 2)
    ffn = _matmul_nn([(hmid, wd)], BF16, MM_ROWS, 1024, "ffn_down")
    dffn, dr2, acc2 = _final_ln_loss(x1, ffn, g2, ln2_g, ln2_b, target, tb)
    loss = (0.5 / d) * jnp.sum(acc2[3])

    dgate, dupp = _swiglu_bwd(dffn, wd.T, gate, upp, 512, D_FF // 2)
    g_wdown = _matmul_tn(hmid, dffn, MM_ROWS, 1024, "g_w_down")
    g_wgate, g_wup = _matmul_tn_multi(h2, [dgate, dupp], MM_ROWS, [D_FF // 2, D_FF // 2], "g_w_gate_up")
    dh2 = _matmul_nn([(dgate, wg.T), (dupp, wu.T)], BF16, 512, 1024, "d_h2")
    dmix, dr1, acc1 = _bwd_ln1(dr2, dh2, x1, x0, mix, g1, sc2, ln1_g, tb)

    dyn, dpool = _matmul_nn_multi(dmix, [wout[0:d].T, wout[d:2 * d].T], [BF16, BF16], MM_ROWS, "d_yn_pool")
    g_wout = jnp.concatenate([_matmul_tn(yn, dmix, MM_ROWS, 1024, "g_w_out_a"),
                              _matmul_tn(pool, dmix, MM_ROWS, 1024, "g_w_out_b")], axis=0)
    dd, dds, g_pw, accp = _pool_bwd_a(dpool, dsave, pw, jnp.swapaxes(pw, 1, 2), pscale)
    dup = _pool_bwd_b(dds, dd, _pool_consts(True))
    dy, dz, accm = _merge_bwd(dyn, y, u, z, dskip_e, gn, tb)
    lam0 = jnp.zeros((2, D_STATE, d), F32)
    late_grads = dict(w_out=g_wout, w_gate=g_wgate, w_up=g_wup, w_down=g_wdown, pool_w=g_pw)
    pairs = ()
    if late_shards is not None:
        slabs = [_to_slabs(nme, late_grads[nme]) for nme in _LATE]
        pairs = _pair_add(slabs, _pair_swap(slabs, "reduce_g_d2d_late"), core, "reduce_g_pair_late")
    dxs, dbc, ddt, accs, lam_c, *arrived = _ssd_bwd(u, xd, bias2, a2, rexp, rexp_t, dy, hprev, lam0, "ssd_bwd",
                                                    exchange=pairs)
    du, accb = _conv_bwd_a(dxs, dy, dskip_e, dbc, u, tb, "conv_bwd_a")
    dxd, accw = _conv_bwd_b(du, xd, ddt, w8, tb, "conv_bwd_b")

    lc = ctx.shape[0]
    zeros_c = jnp.zeros((lc, d), BF16)
    dxs_c, dbc_c, ddt_c, accs_c, _ = _ssd_bwd(uc, xdc, bias2, a2, rexp, rexp_t, zeros_c, hprev_c, lam_c, "ssd_bwd_ctx")
    du_c, accb_c = _conv_bwd_a(dxs_c, zeros_c, dskip_e, dbc_c, uc, tbc, "conv_bwd_a_ctx")
    dxd_c, accw_c = _conv_bwd_b(du_c, xdc, ddt_c, w8, tbc, "conv_bwd_b_ctx")
    dhc = _matmul_nn([(dxd_c, wxd.T)], F32, 512, 1024, "d_hc")
    _, acc0c = _bwd_ln0(None, dhc, ctx, emb_g, emb_b, sc1c, tbc, "bwd_ln0_ctx")

    dh1 = _matmul_nn([(dz, wz.T), (dxd, wxd.T), (dup, wup.T)], BF16, MM_ROWS, 1024, "d_h1")
    g_wz, g_wxd, g_wpo = _matmul_tn_multi(h1, [dz, dxd, dup], 512, [1024, D_XD, 1024], "g_in_proj")
    g_wxd = g_wxd + _matmul_tn(hc, dxd_c, 512, D_XD, "g_in_proj_xd_ctx")
    g_win = jnp.concatenate([g_wz, g_wxd[:, 0:D_XBC + 2 * SSD_HEADS], g_wpo], axis=1)
    grad_x, acc0 = _bwd_ln0(dr1, dh1, x, emb_g, emb_b, sc1, tb, "bwd_ln0")

    zero_d = jnp.zeros((1, d), F32)
    dmod = jnp.concatenate([acc0[1:2], acc0[0:1], acc1[4:5], acc1[1:2], acc1[0:1], acc2[2:3]], axis=1)
    dmodc = jnp.concatenate([acc0c[1:2], acc0c[0:1]] + [zero_d] * 4, axis=1)

    big = dict(in_proj=g_win)
    if late_shards is None:
        big.update(late_grads)
    sml = dict(
        dmod=dmod, dmod_ctx=dmodc, emb_ln_g=acc0[2] + acc0c[2], emb_ln_b=acc0[3] + acc0c[3],
        conv_w=accw[0:D_CONV] + accw_c[0:D_CONV], conv_b=accb[0] + accb_c[0],
        dt_bias=accs[:, 0, 0:SSD_HEADS] + accs_c[:, 0, 0:SSD_HEADS],
        a_log=accs[:, 1, 0:SSD_HEADS] + accs_c[:, 1, 0:SSD_HEADS],
        d_skip=jnp.sum(accm[1].reshape(SSD_HEADS, HEAD_DIM), axis=1),
        ssd_norm_g=accm[0], pool_scale=accp[0], ln1_g=acc1[2], ln1_b=acc1[3], ln2_g=acc2[0], ln2_b=acc2[1])
    return loss, grad_x, big, sml, (arrived if late_shards is not None else None)


_SMALL = ("c_ctx", "emb_ln_g", "emb_ln_b", "b_ada", "conv_w", "conv_b", "dt_bias", "a_log", "d_skip",
          "ssd_norm_g", "pool_scale", "ln1_g", "ln1_b", "ln2_g", "ln2_b")


def _small_rows(size):
    return -(-size // 1024)


def _pack_small(vals, names):
    pieces, rows = [], 0
    for nme in names:
        flat = vals[nme].reshape(-1).astype(F32)
        nr = _small_rows(flat.shape[0])
        pieces.append(flat)
        if nr * 1024 > flat.shape[0]:
            pieces.append(jnp.zeros((nr * 1024 - flat.shape[0],), F32))
        rows += nr
    if rows % 8:
        pieces.append(jnp.zeros(((8 - rows % 8) * 1024,), F32))
    return jnp.concatenate(pieces).reshape(-1, 1024)


def _small_offsets(shapes, names):
    out, off = {}, 0
    for nme in names:
        nr = _small_rows(math.prod(shapes[nme]))
        out[nme] = (off, nr)
        off += nr
    return out


def _unpack_small(packed, shapes, names):
    out = {}
    for nme, (off, nr) in _small_offsets(shapes, names).items():
        out[nme] = packed[off:off + nr].reshape(-1)[:math.prod(shapes[nme])].reshape(shapes[nme])
    return out


_WEIGHT_ORDER = ("c_ctx", "emb_ln_g", "emb_ln_b", "w_ada", "b_ada", "in_proj", "conv_w", "conv_b", "dt_bias", "a_log",
                 "d_skip", "ssd_norm_g", "pool_w", "pool_scale", "w_out", "ln1_g", "ln1_b", "w_gate", "w_up", "w_down",
                 "ln2_g", "ln2_b")


def _as2d(a):
    return a.reshape(-1, a.shape[-1])


def kernel(x, c, ctx, c_ctx, emb_ln_g, emb_ln_b, w_ada, b_ada, in_proj, conv_w, conv_b, dt_bias, a_log, d_skip, ssd_norm_g, pool_w, pool_scale, w_out, ln1_g, ln1_b, w_gate, w_up, w_down, ln2_g, ln2_b, loss_target, m_c_ctx, m_emb_ln_g, m_emb_ln_b, m_w_ada, m_b_ada, m_in_proj, m_conv_w, m_conv_b, m_dt_bias, m_a_log, m_d_skip, m_ssd_norm_g, m_pool_w, m_pool_scale, m_w_out, m_ln1_g, m_ln1_b, m_w_gate, m_w_up, m_w_down, m_ln2_g, m_ln2_b, v_c_ctx, v_emb_ln_g, v_emb_ln_b, v_w_ada, v_b_ada, v_in_proj, v_conv_w, v_conv_b, v_dt_bias, v_a_log, v_d_skip, v_ssd_norm_g, v_pool_w, v_pool_scale, v_w_out, v_ln1_g, v_ln1_b, v_w_gate, v_w_up, v_w_down, v_ln2_g, v_ln2_b):
    w = dict(c_ctx=c_ctx, emb_ln_g=emb_ln_g, emb_ln_b=emb_ln_b, w_ada=w_ada, b_ada=b_ada, in_proj=in_proj, conv_w=conv_w,
             conv_b=conv_b, dt_bias=dt_bias, a_log=a_log, d_skip=d_skip, ssd_norm_g=ssd_norm_g, pool_w=pool_w,
             pool_scale=pool_scale, w_out=w_out, ln1_g=ln1_g, ln1_b=ln1_b, w_gate=w_gate, w_up=w_up, w_down=w_down,
             ln2_g=ln2_g, ln2_b=ln2_b)
    m = dict(c_ctx=m_c_ctx, emb_ln_g=m_emb_ln_g, emb_ln_b=m_emb_ln_b, w_ada=m_w_ada, b_ada=m_b_ada, in_proj=m_in_proj,
             conv_w=m_conv_w, conv_b=m_conv_b, dt_bias=m_dt_bias, a_log=m_a_log, d_skip=m_d_skip,
             ssd_norm_g=m_ssd_norm_g, pool_w=m_pool_w, pool_scale=m_pool_scale, w_out=m_w_out, ln1_g=m_ln1_g,
             ln1_b=m_ln1_b, w_gate=m_w_gate, w_up=m_w_up, w_down=m_w_down, ln2_g=m_ln2_g, ln2_b=m_ln2_b)
    v = dict(c_ctx=v_c_ctx, emb_ln_g=v_emb_ln_g, emb_ln_b=v_emb_ln_b, w_ada=v_w_ada, b_ada=v_b_ada, in_proj=v_in_proj,
             conv_w=v_conv_w, conv_b=v_conv_b, dt_bias=v_dt_bias, a_log=v_a_log, d_skip=v_d_skip,
             ssd_norm_g=v_ssd_norm_g, pool_w=v_pool_w, pool_scale=v_pool_scale, w_out=v_w_out, ln1_g=v_ln1_g,
             ln1_b=v_ln1_b, w_gate=v_w_gate, w_up=v_w_up, w_down=v_w_down, ln2_g=v_ln2_g, ln2_b=v_ln2_b)

    xi, yi, ci = _mesh_pos()
    chip = 2 * xi + yi

    dev = 4 * xi + 2 * yi + ci
    d = D_MODEL
    core = ci.reshape(1).astype(jnp.int32)

    crow = jnp.concatenate([c.reshape(1, d), jnp.zeros((7, d), F32)], axis=0)
    _, c_all = _allreduce_small(crow, "gather_c")
    c16 = jnp.concatenate([c_all[:, 0, :], c_ctx.reshape(1, d), jnp.zeros((MOD_ROWS - 9, d), F32)], axis=0)
    ncol = w_ada.shape[-1]
    wada_bf = w_ada[0].astype(BF16)
    b_mine = lax.dynamic_slice_in_dim(b_ada, chip * ncol, ncol, axis=1)
    mods4 = _chip_bcast(_mods_fwd(c16, wada_bf, b_mine), "gather_mods")
    mods = mods4.transpose(1, 0, 2).reshape(MOD_ROWS, 4 * ncol)
    mod = lax.dynamic_slice_in_dim(mods, dev, 1, axis=0)
    mod_ctx = mods[8:9]

    shard = {name: w[name][0].astype(BF16).reshape(shp) for name, shp, _ in _BIG}
    conv8 = jnp.concatenate([conv_w[0], jnp.zeros((8 - D_CONV, conv_w.shape[-1]), F32)], axis=0)
    *gathered, conv4 = _gather_weights([shard[nme] for nme in _EARLY], conv8)
    wts = {nme: _assemble(nme, t) for nme, t in zip(_EARLY, gathered)}
    w8 = conv4.transpose(1, 0, 2).reshape(8, D_XBC)
    small = {nme: (w[nme] if nme in ("c_ctx", "emb_ln_g", "emb_ln_b") else w[nme][0]) for nme in _SMALL if nme != "conv_w"}

    loss, grad_x, big, sml, late_parts = _device_step(x[0], mod, mod_ctx, ctx[0], loss_target[0], wts, w8, small, 512,
                                                      late_shards=[shard[nme] for nme in _LATE], core=core)
    loss = lax.psum(loss, ("x", "y", "c"))

    g_big = _reduce_grads(big, late_parts, core)
    reduced = tuple(sml)
    small_shapes = {nme: sml[nme].shape for nme in reduced}
    total, each = _allreduce_small(_pack_small(sml, reduced), "reduce_small")
    g_small = _unpack_small(total, small_shapes, reduced)
    cw_cols = conv_w.shape[-1]
    g_small["conv_w"] = lax.dynamic_slice_in_dim(g_small["conv_w"], chip * cw_cols, cw_cols, axis=1)

    off, nr = _small_offsets(small_shapes, reduced)["dmod"]
    dm16 = jnp.concatenate([each[:, off:off + nr, :].reshape(8, nr * 1024)[:, :6 * d], g_small["dmod_ctx"],
                            jnp.zeros((MOD_ROWS - 9, 6 * d), F32)], axis=0)
    dm_mine = lax.dynamic_slice_in_dim(dm16, chip * ncol, ncol, axis=1)
    g_wada = _mods_bwd_w(c16.T, dm_mine)
    g_small["b_ada"] = _mods_bwd_b(dm16)[0:1]
    c_part = _mods_bwd_c(dm_mine, wada_bf, c16)[8:16]
    g_small["c_ctx"] = _allreduce_small(c_part, "reduce_c_ctx")[0][0]

    grads, delta, new_m, new_v = {}, {}, {}, {}
    grads["w_ada"] = g_wada[None]
    delta["w_ada"], new_m["w_ada"], new_v["w_ada"] = (
        t[None] for t in _adamw(w_ada[0], g_wada, m_w_ada[0], v_w_ada[0], "adamw_w_ada"))
    for name, _, _ in _BIG:
        g2 = _as2d(g_big[name])
        d2, m2, v2 = _adamw(_as2d(w[name][0]), g2, _as2d(m[name][0]), _as2d(v[name][0]), "adamw_" + name)
        grads[name] = g2.reshape(w[name].shape)
        delta[name], new_m[name], new_v[name] = (t.reshape(w[name].shape) for t in (d2, m2, v2))
    shp = {nme: w[nme].shape for nme in _SMALL}
    gp = _pack_small(g_small, _SMALL)
    dp, mp, vp = _adamw(_pack_small(w, _SMALL), gp, _pack_small(m, _SMALL), _pack_small(v, _SMALL), "adamw_small")
    for dst, src in ((grads, gp), (delta, dp), (new_m, mp), (new_v, vp)):
        dst.update(_unpack_small(src, shp, _SMALL))

    return (loss, grad_x[None], *[grads[nme] for nme in _WEIGHT_ORDER], *[delta[nme] for nme in _WEIGHT_ORDER],
            *[new_m[nme] for nme in _WEIGHT_ORDER], *[new_v[nme] for nme in _WEIGHT_ORDER])
```

```python
import functools
import math

import jax
import jax.numpy as jnp
from jax import lax
from jax.experimental import pallas as pl
from jax.experimental.pallas import tpu as pltpu

F32 = jnp.float32
BF16 = jnp.bfloat16
MESH = pl.DeviceIdType.MESH

D_MODEL = 1024
SSD_HEADS = 16
HEAD_DIM = 64
D_STATE = 128
CHUNK = 128
D_CONV = 5
D_XBC = D_MODEL + 2 * D_STATE
D_XD = 1408
N_POOL = 4
POOL_DIM = 256
POOL_WINDOWS = (2, 4, 8, 16)
GRID_W = 64
D_FF = 2816
D_IN_PROJ = 3360
LN_EPS = 1e-5
ALPHA = 2.0 ** 0.25
POOL_TB = 512
MM_ROWS = 1024

ADAM_LR = 0.001
ADAM_B1 = 0.9
ADAM_B2 = 0.999
ADAM_EPS = 1e-08
ADAM_WD = 0.01
ADAM_STEP = 10

VMEM_LIMIT = 56 * 1024 * 1024


def _cp(sem=None):
    return pltpu.CompilerParams(dimension_semantics=sem, vmem_limit_bytes=VMEM_LIMIT)


def _sigmoid(x):
    return 1.0 / (1.0 + jnp.exp(-x))


def _silu(x):
    return x * _sigmoid(x)


def _dsilu(x):
    s = _sigmoid(x)
    return s * (1.0 + x * (1.0 - s))


def _softplus(x):
    t = jnp.exp(-jnp.abs(x))
    u = 1.0 + t
    log1p = jnp.where(u == 1.0, t, jnp.log(u) * t / (u - 1.0 + (u == 1.0)))
    return jnp.maximum(x, 0.0) + log1p


def _split(x, n):
    parts, r = [], x
    for _ in range(n):
        p = r.astype(BF16)
        parts.append(p)
        r = r - p.astype(F32)
    return parts


def _dot(a, b):
    return jnp.dot(a, b, preferred_element_type=F32)


def _dot_nt(a, b):
    return lax.dot_general(a, b, (((1,), (1,)), ((), ())), preferred_element_type=F32)


def _dot_tn(a, b):
    return lax.dot_general(a, b, (((0,), (0,)), ((), ())), preferred_element_type=F32)


def _dot_sel_l(sel_bf, x, n=3):
    out = None
    for p in _split(x, n):
        t = _dot(sel_bf, p)
        out = t if out is None else out + t
    return out


def _dot_sel_r(x, sel_bf, n=3):
    out = None
    for p in _split(x, n):
        t = _dot(p, sel_bf)
        out = t if out is None else out + t
    return out


ROW_SUB = 16


def _row_tiles(tb):
    assert tb % ROW_SUB == 0
    return [slice(s * ROW_SUB, (s + 1) * ROW_SUB) for s in range(tb // ROW_SUB)]


def _fold8(v):
    out = v[0:8, :]
    for r in range(8, v.shape[0], 8):
        out = out + v[r:r + 8, :]
    return out


def _row_block(n, cap=256, mult=8):
    best = None
    for t in range(mult, min(n, cap) + 1, mult):
        if n % t == 0:
            best = t
    return best if best is not None else n


def _vec(v):
    return v.reshape(1, -1).astype(F32)


MOD_ROWS = 16
MOD_TN = 512


def _mods_fwd(c16, w_bf, b):
    r, d = c16.shape
    n = w_bf.shape[1]

    def body(c_ref, w_ref, b_ref, o_ref):
        s = _silu(c_ref[...]).astype(BF16)
        o_ref[...] = _dot(s, w_ref[...]) + b_ref[...]

    return pl.pallas_call(
        body, name="mods_fwd", grid=(n // MOD_TN,),
        in_specs=[pl.BlockSpec((r, d), lambda j: (0, 0)),
                  pl.BlockSpec((d, MOD_TN), lambda j: (0, j)),
                  pl.BlockSpec((1, MOD_TN), lambda j: (0, j))],
        out_specs=pl.BlockSpec((r, MOD_TN), lambda j: (0, j)),
        out_shape=jax.ShapeDtypeStruct((r, n), F32),
        compiler_params=_cp(("arbitrary",)),
    )(c16, w_bf, b)


def _mods_bwd_w(ct16, dm16):
    d = ct16.shape[0]
    n = dm16.shape[1]

    def body(ct_ref, dm_ref, dw_ref):
        s = _silu(ct_ref[...])
        dm = dm_ref[...]
        acc = s[:, 0:1] * dm[0:1, :]
        for r in range(1, 9):
            acc = acc + s[:, r:r + 1] * dm[r:r + 1, :]
        dw_ref[...] = acc

    return pl.pallas_call(
        body, name="mods_bwd_w", grid=(n // MOD_TN,),
        in_specs=[pl.BlockSpec((d, MOD_ROWS), lambda j: (0, 0)),
                  pl.BlockSpec((MOD_ROWS, MOD_TN), lambda j: (0, j))],
        out_specs=pl.BlockSpec((d, MOD_TN), lambda j: (0, j)),
        out_shape=jax.ShapeDtypeStruct((d, n), F32),
        compiler_params=_cp(("arbitrary",)),
    )(ct16, dm16)


def _mods_bwd_c(dm16, w_bf, c16):
    d = c16.shape[1]
    n = dm16.shape[1]
    nk = n // MOD_TN

    def body(dm_ref, w_ref, c_ref, o_ref):
        k = pl.program_id(0)

        @pl.when(k == 0)
        def _():
            o_ref[...] = jnp.zeros_like(o_ref)

        o_ref[...] += _dot_nt(dm_ref[...].astype(BF16), w_ref[...])

        @pl.when(k == nk - 1)
        def _():
            o_ref[...] = o_ref[...] * (0.5 * _dsilu(c_ref[...]))

    return pl.pallas_call(
        body, name="mods_bwd_c", grid=(nk,),
        in_specs=[pl.BlockSpec((MOD_ROWS, MOD_TN), lambda k: (0, k)),
                  pl.BlockSpec((d, MOD_TN), lambda k: (0, k)),
                  pl.BlockSpec((MOD_ROWS, d), lambda k: (0, 0))],
        out_specs=pl.BlockSpec((MOD_ROWS, d), lambda k: (0, 0)),
        out_shape=jax.ShapeDtypeStruct((MOD_ROWS, d), F32),
        compiler_params=_cp(("arbitrary",)),
    )(dm16, w_bf, c16)


def _mods_bwd_b(dm16):
    n = dm16.shape[1]

    def body(dm_ref, o_ref):
        dm = dm_ref[...]
        acc = dm[0:1, :]
        for r in range(1, 9):
            acc = acc + dm[r:r + 1, :]
        o_ref[...] = jnp.broadcast_to(acc, (8, MOD_TN))

    return pl.pallas_call(
        body, name="mods_bwd_b", grid=(n // MOD_TN,),
        in_specs=[pl.BlockSpec((MOD_ROWS, MOD_TN), lambda j: (0, j))],
        out_specs=pl.BlockSpec((8, MOD_TN), lambda j: (0, j)),
        out_shape=jax.ShapeDtypeStruct((8, n), F32),
        compiler_params=_cp(("arbitrary",)),
    )(dm16)


def _ln_stats(x):
    mu = jnp.mean(x, axis=-1, keepdims=True)
    xc = x - mu
    var = jnp.mean(xc * xc, axis=-1, keepdims=True)
    rstd = lax.rsqrt(var + LN_EPS)
    return xc * rstd, rstd


def _ln_bwd(dxhat, xhat, rstd):
    m1 = jnp.mean(dxhat, axis=-1, keepdims=True)
    m2 = jnp.mean(dxhat * xhat, axis=-1, keepdims=True)
    return rstd * (dxhat - m1 - xhat * m2)


def _row_spec(tb, d):
    return pl.BlockSpec((tb, d), lambda i: (i, 0))


def _par_spec(d):
    return pl.BlockSpec((1, d), lambda i: (0, 0))


def _acc_spec(d):
    return pl.BlockSpec((8, d), lambda i: (0, 0))


def _ln_mod(x, g, b, sh, sc, tb, name):
    n, d = x.shape

    def body(x_ref, g_ref, b_ref, sh_ref, sc_ref, x0_ref, h_ref):
        g, b, sh, sc1 = g_ref[...], b_ref[...], sh_ref[...], 1.0 + sc_ref[...]
        for r in _row_tiles(min(tb, n)):
            xhat, _ = _ln_stats(x_ref[r, :])
            x0 = xhat * g + b
            x0_ref[r, :] = x0
            h_ref[r, :] = (x0 * sc1 + sh).astype(BF16)

    return pl.pallas_call(
        body, name=name, grid=(n // tb,),
        in_specs=[_row_spec(tb, d)] + [_par_spec(d)] * 4,
        out_specs=[_row_spec(tb, d), _row_spec(tb, d)],
        out_shape=[jax.ShapeDtypeStruct((n, d), F32), jax.ShapeDtypeStruct((n, d), BF16)],
        compiler_params=_cp(("parallel",)),
    )(x, g, b, sh, sc)


def _res_ln(xres, mix, gate, g, b, sh, sc, tb):
    n, d = xres.shape

    def body(xr_ref, mix_ref, gate_ref, g_ref, b_ref, sh_ref, sc_ref, x1_ref, h_ref):
        gate_v, g, b, sh, sc1 = gate_ref[...], g_ref[...], b_ref[...], sh_ref[...], 1.0 + sc_ref[...]
        for r in _row_tiles(tb):
            xhat, _ = _ln_stats(ALPHA * xr_ref[r, :] + gate_v * mix_ref[r, :].astype(F32))
            x1 = xhat * g + b
            x1_ref[r, :] = x1
            h_ref[r, :] = (x1 * sc1 + sh).astype(BF16)

    return pl.pallas_call(
        body, name="res_ln1", grid=(n // tb,),
        in_specs=[_row_spec(tb, d)] * 2 + [_par_spec(d)] * 5,
        out_specs=[_row_spec(tb, d), _row_spec(tb, d)],
        out_shape=[jax.ShapeDtypeStruct((n, d), F32), jax.ShapeDtypeStruct((n, d), BF16)],
        compiler_params=_cp(("parallel",)),
    )(xres, mix, gate, g, b, sh, sc)


def _final_ln_loss(x1, ffn, gate, g, b, target, tb):
    n, d = x1.shape

    def body(x1_ref, ffn_ref, gate_ref, g_ref, b_ref, t_ref, dffn_ref, dr_ref, acc_ref):
        i = pl.program_id(0)

        @pl.when(i == 0)
        def _():
            acc_ref[...] = jnp.zeros_like(acc_ref)

        gate_v, g, b = gate_ref[...], g_ref[...], b_ref[...]
        parts = [jnp.zeros((8, d), F32)] * 4
        for r in _row_tiles(tb):
            ffn = ffn_ref[r, :].astype(F32)
            xhat, rstd = _ln_stats(ALPHA * x1_ref[r, :] + gate_v * ffn)
            err = xhat * g + b - t_ref[r, :]
            dx2 = err * (1.0 / d)
            dr = _ln_bwd(dx2 * g, xhat, rstd)
            dr_ref[r, :] = dr
            dffn_ref[r, :] = (gate_v * dr).astype(BF16)
            terms = (dx2 * xhat, dx2, dr * ffn, err * err)
            parts = [p + _fold8(t) for p, t in zip(parts, terms)]
        for j, p in enumerate(parts):
            acc_ref[j:j + 1, :] += jnp.sum(p, axis=0, keepdims=True)

    return pl.pallas_call(
        body, name="final_ln_loss", grid=(n // tb,),
        in_specs=[_row_spec(tb, d)] * 2 + [_par_spec(d)] * 3 + [_row_spec(tb, d)],
        out_specs=[_row_spec(tb, d), _row_spec(tb, d), _acc_spec(d)],
        out_shape=[jax.ShapeDtypeStruct((n, d), BF16), jax.ShapeDtypeStruct((n, d), F32),
                   jax.ShapeDtypeStruct((8, d), F32)],
        compiler_params=_cp(("arbitrary",)),
    )(x1, ffn, gate, g, b, target)


def _bwd_ln1(dr2, dh2, x1, x0, mix, gate, sc2, g, tb):
    n, d = x1.shape

    def body(dr2_ref, dh2_ref, x1_ref, x0_ref, mix_ref, gate_ref, sc_ref, g_ref, dmix_ref, dr1_ref, acc_ref):
        i = pl.program_id(0)

        @pl.when(i == 0)
        def _():
            acc_ref[...] = jnp.zeros_like(acc_ref)

        gate_v, g, sc1 = gate_ref[...], g_ref[...], 1.0 + sc_ref[...]
        parts = [jnp.zeros((8, d), F32)] * 5
        for r in _row_tiles(tb):
            dh2 = dh2_ref[r, :].astype(F32)
            mix = mix_ref[r, :].astype(F32)
            dx1 = ALPHA * dr2_ref[r, :] + dh2 * sc1
            xhat, rstd = _ln_stats(ALPHA * x0_ref[r, :] + gate_v * mix)
            dr1 = _ln_bwd(dx1 * g, xhat, rstd)
            dr1_ref[r, :] = dr1
            dmix_ref[r, :] = (gate_v * dr1).astype(BF16)
            terms = (dh2 * x1_ref[r, :], dh2, dx1 * xhat, dx1, dr1 * mix)
            parts = [p + _fold8(t) for p, t in zip(parts, terms)]
        for j, p in enumerate(parts):
            acc_ref[j:j + 1, :] += jnp.sum(p, axis=0, keepdims=True)

    return pl.pallas_call(
        body, name="bwd_ln1", grid=(n // tb,),
        in_specs=[_row_spec(tb, d)] * 5 + [_par_spec(d)] * 3,
        out_specs=[_row_spec(tb, d), _row_spec(tb, d), _acc_spec(d)],
        out_shape=[jax.ShapeDtypeStruct((n, d), BF16), jax.ShapeDtypeStruct((n, d), F32),
                   jax.ShapeDtypeStruct((8, d), F32)],
        compiler_params=_cp(("arbitrary",)),
    )(dr2, dh2, x1, x0, mix, gate, sc2, g)


def _bwd_ln0(dres, dh, x, g, b, sc, tb, name):
    n, d = x.shape
    has_res = dres is not None

    def body(*refs):
        if has_res:
            dres_ref, dh_ref, x_ref, g_ref, b_ref, sc_ref, dx_ref, acc_ref = refs
        else:
            dh_ref, x_ref, g_ref, b_ref, sc_ref, dx_ref, acc_ref = refs
        i = pl.program_id(0)

        @pl.when(i == 0)
        def _():
            acc_ref[...] = jnp.zeros_like(acc_ref)

        g, b, sc1 = g_ref[...], b_ref[...], 1.0 + sc_ref[...]
        parts = [jnp.zeros((8, d), F32)] * 4
        for r in _row_tiles(tb):
            dh = dh_ref[r, :].astype(F32)
            xhat, rstd = _ln_stats(x_ref[r, :])
            x0 = xhat * g + b
            dx0 = dh * sc1
            if has_res:
                dx0 = dx0 + ALPHA * dres_ref[r, :]
            dx_ref[r, :] = _ln_bwd(dx0 * g, xhat, rstd)
            terms = (dh * x0, dh, dx0 * xhat, dx0)
            parts = [p + _fold8(t) for p, t in zip(parts, terms)]
        for j, p in enumerate(parts):
            acc_ref[j:j + 1, :] += jnp.sum(p, axis=0, keepdims=True)

    ins = ([dres] if has_res else []) + [dh, x, g, b, sc]
    return pl.pallas_call(
        body, name=name, grid=(n // tb,),
        in_specs=[_row_spec(tb, d)] * (3 if has_res else 2) + [_par_spec(d)] * 3,
        out_specs=[_row_spec(tb, d), _acc_spec(d)],
        out_shape=[jax.ShapeDtypeStruct((n, d), F32), jax.ShapeDtypeStruct((8, d), F32)],
        compiler_params=_cp(("arbitrary",)),
    )(*ins)


def _matmul_nn(pairs, out_dtype, tm, tn, name):
    m = pairs[0][0].shape[0]
    n = pairs[0][1].shape[1]
    tm = min(tm, m)
    tn = min(tn, n)
    npair = len(pairs)

    def body(*refs):
        o_ref = refs[-1]
        acc = None
        for p in range(npair):
            t = _dot(refs[2 * p][...].astype(BF16), refs[2 * p + 1][...])
            acc = t if acc is None else acc + t
        o_ref[...] = acc.astype(out_dtype)

    in_specs, args = [], []
    for a, b in pairs:
        k = a.shape[1]
        in_specs += [pl.BlockSpec((tm, k), lambda i, j: (i, 0)), pl.BlockSpec((k, tn), lambda i, j: (0, j))]
        args += [a, b]
    return pl.pallas_call(
        body, name=name, grid=(m // tm, n // tn),
        in_specs=in_specs,
        out_specs=pl.BlockSpec((tm, tn), lambda i, j: (i, j)),
        out_shape=jax.ShapeDtypeStruct((m, n), out_dtype),
        compiler_params=_cp(("parallel", "arbitrary")),
    )(*args)


def _matmul_tn(a, g, tm, tn, name):
    m, k = a.shape
    n = g.shape[1]
    tm = min(tm, m)
    tn = min(tn, n)

    def body(a_ref, g_ref, o_ref):
        i = pl.program_id(1)

        @pl.when(i == 0)
        def _():
            o_ref[...] = jnp.zeros_like(o_ref)

        o_ref[...] += _dot_tn(a_ref[...].astype(BF16), g_ref[...].astype(BF16))

    return pl.pallas_call(
        body, name=name, grid=(n // tn, m // tm),
        in_specs=[pl.BlockSpec((tm, k), lambda j, i: (i, 0)), pl.BlockSpec((tm, tn), lambda j, i: (i, j))],
        out_specs=pl.BlockSpec((k, tn), lambda j, i: (0, j)),
        out_shape=jax.ShapeDtypeStruct((k, n), F32),
        compiler_params=_cp(("parallel", "arbitrary")),
    )(a, g)


def _matmul_nn_multi(a, bs, out_dtypes, tm, name):
    m, k = a.shape
    tm = min(tm, m)
    nb = len(bs)

    def body(a_ref, *refs):
        av = a_ref[...].astype(BF16)
        for j in range(nb):
            refs[nb + j][...] = _dot(av, refs[j][...]).astype(out_dtypes[j])

    return pl.pallas_call(
        body, name=name, grid=(m // tm,),
        in_specs=[pl.BlockSpec((tm, k), lambda i: (i, 0))] + [pl.BlockSpec(b.shape, lambda i: (0, 0)) for b in bs],
        out_specs=[pl.BlockSpec((tm, b.shape[1]), lambda i: (i, 0)) for b in bs],
        out_shape=[jax.ShapeDtypeStruct((m, b.shape[1]), dt) for b, dt in zip(bs, out_dtypes)],
        compiler_params=_cp(("parallel",)),
    )(a, *bs)


def _matmul_tn_multi(a, gs, tm, tns, name):
    m, k = a.shape
    tm = min(tm, m)
    ng = len(gs)
    nj = gs[0].shape[1] // tns[0]
    assert all(g.shape[1] // t == nj and g.shape[1] % t == 0 for g, t in zip(gs, tns))

    def body(a_ref, *refs):
        i = pl.program_id(1)

        @pl.when(i == 0)
        def _():
            for j in range(ng):
                refs[ng + j][...] = jnp.zeros_like(refs[ng + j])

        av = a_ref[...].astype(BF16)
        for j in range(ng):
            refs[ng + j][...] += _dot_tn(av, refs[j][...].astype(BF16))

    return pl.pallas_call(
        body, name=name, grid=(nj, m // tm),
        in_specs=[pl.BlockSpec((tm, k), lambda j, i: (i, 0))] +
                 [pl.BlockSpec((tm, t), lambda j, i: (i, j)) for t in tns],
        out_specs=[pl.BlockSpec((k, t), lambda j, i: (0, j)) for t in tns],
        out_shape=[jax.ShapeDtypeStruct((k, g.shape[1]), F32) for g in gs],
        compiler_params=_cp(("parallel", "arbitrary")),
    )(a, *gs)


def _swiglu_fwd(h, wg, wu, tm, tn):
    m, k = h.shape
    n = wg.shape[1]
    tm = min(tm, m)

    def body(h_ref, wg_ref, wu_ref, gate_ref, up_ref, hmid_ref):
        hv = h_ref[...]
        gate = _dot(hv, wg_ref[...])
        up = _dot(hv, wu_ref[...])
        gate_ref[...] = gate.astype(BF16)
        up_ref[...] = up.astype(BF16)
        hmid_ref[...] = (_silu(gate) * up).astype(BF16)

    blk = pl.BlockSpec((tm, tn), lambda i, j: (i, j))
    wspec = pl.BlockSpec((k, tn), lambda i, j: (0, j))
    return pl.pallas_call(
        body, name="swiglu_fwd", grid=(m // tm, n // tn),
        in_specs=[pl.BlockSpec((tm, k), lambda i, j: (i, 0)), wspec, wspec],
        out_specs=[blk, blk, blk],
        out_shape=[jax.ShapeDtypeStruct((m, n), BF16), jax.ShapeDtypeStruct((m, n), BF16),
                   jax.ShapeDtypeStruct((m, n), BF16)],
        compiler_params=_cp(("parallel", "arbitrary")),
    )(h, wg, wu)


def _swiglu_bwd(dffn, wdt, gate, up, tm, tn):
    m, k = dffn.shape
    n = wdt.shape[1]
    tm = min(tm, m)

    def body(d_ref, w_ref, gate_ref, up_ref, dg_ref, du_ref):
        dh = _dot(d_ref[...], w_ref[...])
        gate = gate_ref[...].astype(F32)
        sg = _sigmoid(gate)
        dg_ref[...] = (dh * up_ref[...].astype(F32) * (sg * (1.0 + gate * (1.0 - sg)))).astype(BF16)
        du_ref[...] = (dh * (gate * sg)).astype(BF16)

    blk = pl.BlockSpec((tm, tn), lambda i, j: (i, j))
    return pl.pallas_call(
        body, name="swiglu_bwd", grid=(m // tm, n // tn),
        in_specs=[pl.BlockSpec((tm, k), lambda i, j: (i, 0)), pl.BlockSpec((k, tn), lambda i, j: (0, j)), blk, blk],
        out_specs=[blk, blk],
        out_shape=[jax.ShapeDtypeStruct((m, n), BF16), jax.ShapeDtypeStruct((m, n), BF16)],
        compiler_params=_cp(("parallel", "arbitrary")),
    )(dffn, wdt, gate, up)


def _halo_specs(tb, width, nrows):
    r8 = tb // 8
    last = nrows // 8 - 1
    prev = pl.BlockSpec((8, width), lambda i: (jnp.maximum(i * r8 - 1, 0), 0))
    nxt = pl.BlockSpec((8, width), lambda i: (jnp.minimum((i + 1) * r8, last), 0))
    return prev, nxt


CONV_SUB = 32


def _halo_scratch():
    return [pltpu.VMEM((CONV_SUB + 16, D_XBC), F32), pltpu.VMEM((CONV_SUB + 16, D_XBC), F32)]


def _shifted_rows(prev_ref, cur_ref, next_ref, top, bot, tb, i, nb):
    sub = CONV_SUB
    nsub = tb // sub
    assert nsub >= 2
    top[0:8, :] = prev_ref[...] * (i > 0).astype(F32)
    top[8:sub + 16, :] = cur_ref[0:sub + 8, :]
    bot[0:sub + 8, :] = cur_ref[tb - sub - 8:tb, :]
    bot[sub + 8:sub + 16, :] = next_ref[...] * (i < nb - 1).astype(F32)

    def rows(s, o):
        if s == 0:
            return top[8 + o:8 + o + sub, :]
        if s == nsub - 1:
            return bot[8 + o:8 + o + sub, :]
        return cur_ref[s * sub + o:(s + 1) * sub + o, :]

    return rows


def _conv_fwd(xd, w8, b, tb, name):
    n = xd.shape[0]
    tb = min(tb, n)
    nb = n // tb
    prev, nxt = _halo_specs(tb, D_XBC, n)

    def body(p_ref, c_ref, n_ref, w_ref, b_ref, u_ref, a_ref, top, bot):
        i = pl.program_id(0)
        rows = _shifted_rows(p_ref, c_ref, n_ref, top, bot, tb, i, nb)
        w = [w_ref[k:k + 1, :] for k in range(D_CONV)]
        bias = jnp.broadcast_to(b_ref[...], (CONV_SUB, D_XBC))
        for s in range(tb // CONV_SUB):
            acc = bias
            for k in range(D_CONV):
                acc = acc + w[k] * rows(s, k - 2)
            u_ref[s * CONV_SUB:(s + 1) * CONV_SUB, :] = acc.astype(BF16)
            a_ref[s * CONV_SUB:(s + 1) * CONV_SUB, :] = _silu(acc).astype(BF16)

    return pl.pallas_call(
        body, name=name, grid=(nb,),
        in_specs=[prev, pl.BlockSpec((tb, D_XBC), lambda i: (i, 0)), nxt,
                  pl.BlockSpec((8, D_XBC), lambda i: (0, 0)), _par_spec(D_XBC)],
        out_specs=[_row_spec(tb, D_XBC), _row_spec(tb, D_XBC)],
        out_shape=[jax.ShapeDtypeStruct((n, D_XBC), BF16), jax.ShapeDtypeStruct((n, D_XBC), BF16)],
        scratch_shapes=_halo_scratch(),
        compiler_params=_cp(("parallel",)),
    )(xd, xd, xd, w8, b)


def _conv_bwd_a(dxs, dy, dskip_e, dbc, u, tb, name):
    n = u.shape[0]
    tb = min(tb, n)

    def body(dxs_ref, dy_ref, sk_ref, dbc_ref, u_ref, du_ref, acc_ref):
        i = pl.program_id(0)

        @pl.when(i == 0)
        def _():
            acc_ref[...] = jnp.zeros_like(acc_ref)

        sk = sk_ref[...]
        part = jnp.zeros((8, D_XBC), F32)
        for r in _row_tiles(tb):
            gx = dxs_ref[0, r, :].astype(F32) + dxs_ref[1, r, :].astype(F32) + dy_ref[r, :].astype(F32) * sk
            gbc = dbc_ref[0, r, :] + dbc_ref[1, r, :]
            du = jnp.concatenate([gx, gbc], axis=1) * _dsilu(u_ref[r, :].astype(F32))
            du_ref[r, :] = du
            part = part + _fold8(du)
        acc_ref[0:1, :] += jnp.sum(part, axis=0, keepdims=True)

    return pl.pallas_call(
        body, name=name, grid=(n // tb,),
        in_specs=[pl.BlockSpec((2, tb, D_MODEL), lambda i: (0, i, 0)), _row_spec(tb, D_MODEL), _par_spec(D_MODEL),
                  pl.BlockSpec((2, tb, 2 * D_STATE), lambda i: (0, i, 0)), _row_spec(tb, D_XBC)],
        out_specs=[_row_spec(tb, D_XBC), _acc_spec(D_XBC)],
        out_shape=[jax.ShapeDtypeStruct((n, D_XBC), F32), jax.ShapeDtypeStruct((8, D_XBC), F32)],
        compiler_params=_cp(("arbitrary",)),
    )(dxs, dy, dskip_e, dbc, u)


def _conv_bwd_b(du, xd, ddt, w8, tb, name):
    n = du.shape[0]
    tb = min(tb, n)
    nb = n // tb
    prev, nxt = _halo_specs(tb, D_XBC, n)

    def body(dp_ref, dc_ref, dn_ref, xp_ref, xc_ref, xn_ref, ddt_ref, w_ref, dxd_ref, acc_ref, dtop, dbot, xtop, xbot):
        i = pl.program_id(0)

        @pl.when(i == 0)
        def _():
            acc_ref[...] = jnp.zeros_like(acc_ref)

        sub = CONV_SUB
        nsub = tb // sub
        du_rows = _shifted_rows(dp_ref, dc_ref, dn_ref, dtop, dbot, tb, i, nb)
        x_rows = _shifted_rows(xp_ref, xc_ref, xn_ref, xtop, xbot, tb, i, nb)
        w = [w_ref[k:k + 1, :] for k in range(D_CONV)]
        for s in range(nsub):
            acc = w[0] * du_rows(s, 2)
            for k in range(1, D_CONV):
                acc = acc + w[k] * du_rows(s, 2 - k)
            dxd_ref[s * sub:(s + 1) * sub, 0:D_XBC] = acc.astype(BF16)
        for k in range(D_CONV):
            part = jnp.zeros((8, D_XBC), F32)
            for s in range(nsub):
                prod = dc_ref[s * sub:(s + 1) * sub, :] * x_rows(s, k - 2)
                for r in range(0, sub, 8):
                    part = part + prod[r:r + 8, :]
            acc_ref[k:k + 1, :] += jnp.sum(part, axis=0, keepdims=True)
        ddt = ddt_ref[0] + pltpu.roll(ddt_ref[1], SSD_HEADS, 1)
        dxd_ref[:, D_XBC:D_XD] = ddt.astype(BF16)

    cur = pl.BlockSpec((tb, D_XBC), lambda i: (i, 0))
    return pl.pallas_call(
        body, name=name, grid=(nb,),
        in_specs=[prev, cur, nxt, prev, cur, nxt,
                  pl.BlockSpec((2, tb, 128), lambda i: (0, i, 0)), pl.BlockSpec((8, D_XBC), lambda i: (0, 0))],
        out_specs=[_row_spec(tb, D_XD), _acc_spec(D_XBC)],
        out_shape=[jax.ShapeDtypeStruct((n, D_XD), BF16), jax.ShapeDtypeStruct((8, D_XBC), F32)],
        scratch_shapes=_halo_scratch() + _halo_scratch(),
        compiler_params=_cp(("arbitrary",)),
    )(du, du, du, xd, xd, xd, ddt, w8)


def _ssd_chunk_index(nc, reverse):
    def idx(d, k):
        kk = (nc - 1 - k) if reverse else k
        return kk + d * (nc - 1 - 2 * kk)
    return idx


def _ssd_prologue(d, u_ref, xd_ref, bias_ref, a_ref, r_ref):
    q = CHUNK
    xbc = u_ref[...].astype(F32)
    xs = xbc[:, 0:D_MODEL]
    bm = xbc[:, D_MODEL:D_MODEL + D_STATE]
    cm = xbc[:, D_MODEL + D_STATE:D_XBC]
    row = lax.broadcasted_iota(jnp.int32, (q, q), 0)
    col = lax.broadcasted_iota(jnp.int32, (q, q), 1)
    sgn = 1 - 2 * d
    mask = ((row - col) * sgn) >= 0
    mask_t = ((row - col) * sgn) <= 0
    xdv = xd_ref[...]
    dtraw = jnp.where(d == 0, xdv, pltpu.roll(xdv, 128 - SSD_HEADS, 1)) + bias_ref[...]
    head_lane = col < SSD_HEADS
    dt = jnp.where(head_lane, _softplus(dtraw), 0.0)
    a = a_ref[...]
    tri = jnp.where(mask, 1.0, 0.0).astype(BF16)
    acum = _dot_sel_l(tri, dt * a)
    rexp = r_ref[...]
    alast = jnp.where(d == 0, acum[q - 1:q, :], acum[0:1, :])
    e16 = jnp.exp(acum)
    dend16 = jnp.exp(alast - acum)
    wend16 = dend16 * dt
    e = _dot_sel_r(e16, rexp, n=1)
    wend_e = _dot_sel_r(wend16, rexp, n=1)
    elast_e = _dot_sel_r(jnp.broadcast_to(jnp.exp(alast), (8, 128)), rexp, n=2)[0:1, :]
    g = _dot_nt(cm.astype(BF16), bm.astype(BF16))
    return dict(xs=xs, bm=bm, cm=cm, mask=mask, mask_t=mask_t, dtraw=dtraw, head_lane=head_lane, dt=dt, a=a,
                acum=acum, acum_t=acum.T, dt_t=dt.T, e16=e16, dend16=dend16, wend16=wend16, e=e, wend_e=wend_e,
                elast_e=elast_e, g=g, col=col, row=row)


def _ssd_head_mats(p, h):
    seg = p["acum"][:, h:h + 1] - p["acum_t"][h:h + 1, :]
    lm = jnp.exp(jnp.where(p["mask"], seg, -jnp.inf))
    gl = p["g"] * lm
    s = gl * p["dt_t"][h:h + 1, :]
    return lm, gl, s


def _ssd_fwd(u, xd, bias2, a2, rexp, h0, name, gather=()):
    n = u.shape[0]
    nc = n // CHUNK
    q = CHUNK
    cidx = _ssd_chunk_index(nc, reverse=False)
    ng = len(gather)

    def body(u_ref, xd_ref, bias_ref, a_ref, r_ref, h0_ref, *rest):
        g_ins, (y_ref, hp_ref, hf_ref), rest = rest[:ng], rest[ng:ng + 3], rest[ng + 3:]
        g_outs, st, sems = rest[:ng], rest[ng], rest[ng + 1:]
        d = pl.program_id(0)
        k = pl.program_id(1)
        if ng:
            g_start, g_finish = _gather_steps(g_ins, g_outs, *sems)
            pl.when((d == 0) & (k == 0))(g_start)

        @pl.when(k == 0)
        def _():
            st[...] = h0_ref[...]

        p = _ssd_prologue(d, u_ref, xd_ref, bias_ref, a_ref, r_ref)
        stv = st[...]
        st_bf = stv.astype(BF16)
        hp_ref[...] = st_bf
        xs = p["xs"]
        lane128 = p["col"]
        y_off = _dot(p["cm"].astype(BF16), st_bf) * p["e"]
        for pb in range(SSD_HEADS // 2):
            _, _, s0 = _ssd_head_mats(p, 2 * pb)
            _, _, s1 = _ssd_head_mats(p, 2 * pb + 1)
            xp = xs[:, pb * 128:(pb + 1) * 128]
            rhs = jnp.concatenate([jnp.where(lane128 < HEAD_DIM, xp, 0.0), jnp.where(lane128 >= HEAD_DIM, xp, 0.0)],
                                  axis=0).astype(BF16)
            lhs = jnp.concatenate([s0, s1], axis=1).astype(BF16)
            y_ref[:, pb * 128:(pb + 1) * 128] = (_dot(lhs, rhs) + y_off[:, pb * 128:(pb + 1) * 128]).astype(BF16)
        xw = (xs * p["wend_e"]).astype(BF16)
        new = stv * p["elast_e"] + _dot(p["bm"].T.astype(BF16), xw)
        st[...] = new
        hf_ref[...] = new
        if ng:
            pl.when((d == 1) & (k == nc - 1))(g_finish)

    nsem = _GATHER_SEMS * ng
    return pl.pallas_call(
        body, name=name, grid=(2, nc),
        in_specs=[pl.BlockSpec((q, D_XBC), lambda d, k: (cidx(d, k), 0)),
                  pl.BlockSpec((q, 128), lambda d, k: (cidx(d, k), D_XBC // 128)),
                  pl.BlockSpec((None, 1, 128), lambda d, k: (d, 0, 0)),
                  pl.BlockSpec((None, 1, 128), lambda d, k: (d, 0, 0)),
                  pl.BlockSpec((128, D_MODEL), lambda d, k: (0, 0)),
                  pl.BlockSpec((None, D_STATE, D_MODEL), lambda d, k: (d, 0, 0))] + [_ANY] * ng,
        out_specs=[pl.BlockSpec((None, q, D_MODEL), lambda d, k: (d, cidx(d, k), 0)),
                   pl.BlockSpec((None, None, D_STATE, D_MODEL), lambda d, k: (d, cidx(d, k), 0, 0)),
                   pl.BlockSpec((None, D_STATE, D_MODEL), lambda d, k: (d, 0, 0))] + [_ANY] * ng,
        out_shape=[jax.ShapeDtypeStruct((2, n, D_MODEL), BF16),
                   jax.ShapeDtypeStruct((2, nc, D_STATE, D_MODEL), BF16),
                   jax.ShapeDtypeStruct((2, D_STATE, D_MODEL), F32)] +
                  [jax.ShapeDtypeStruct((4,) + t.shape, t.dtype) for t in gather],
        scratch_shapes=[pltpu.VMEM((D_STATE, D_MODEL), F32)] +
                       ([pltpu.SemaphoreType.DMA((nsem,)), pltpu.SemaphoreType.DMA((nsem,))] if ng else []),
        compiler_params=_cp(("arbitrary", "arbitrary")),
    )(u, xd, bias2, a2, rexp, h0, *gather)


def _ssd_bwd(u, xd, bias2, a2, rexp, rexp_t, dy, hprev, lam0, name, exchange=()):
    n = u.shape[0]
    nc = n // CHUNK
    q = CHUNK
    cidx = _ssd_chunk_index(nc, reverse=True)
    ne = len(exchange)

    def body(u_ref, xd_ref, bias_ref, a_ref, r_ref, rt_ref, dy_ref, hp_ref, lam0_ref, *rest):
        e_ins, (dxs_ref, dbc_ref, ddt_ref, acc_ref, lamo_ref), rest = rest[:ne], rest[ne:ne + 5], rest[ne + 5:]
        e_outs, lam, sems = rest[:ne], rest[ne], rest[ne + 1:]
        d = pl.program_id(0)
        k = pl.program_id(1)
        if ne:
            e_start, e_finish = _exchange_steps(e_ins, e_outs, *sems)
            pl.when((d == 0) & (k == 0))(e_start)

        @pl.when(k == 0)
        def _():
            lam[...] = lam0_ref[...]
            acc_ref[...] = jnp.zeros_like(acc_ref)

        rexp_t = rt_ref[...]

        def hsum(t):
            return _dot_sel_r(t, rexp_t, n=1)

        p = _ssd_prologue(d, u_ref, xd_ref, bias_ref, a_ref, r_ref)
        xs, bm, cm = p["xs"], p["bm"], p["cm"]
        bm_bf, cm_bf = bm.astype(BF16), cm.astype(BF16)
        lamn = lam[...]
        lamn_bf = lamn.astype(BF16)
        stp = hp_ref[...]
        dyv = dy_ref[...].astype(F32)
        lane128 = p["col"]

        wend_e = p["wend_e"]
        cs = _dot(cm_bf, stp)
        dye_bf = (dyv * p["e"]).astype(BF16)
        dc_off = _dot_nt(dye_bf, stp)
        v = _dot(bm_bf, lamn_bf)
        xw_bf = (xs * wend_e).astype(BF16)
        db_off = _dot_nt(xw_bf, lamn_bf)
        elast_e = p["elast_e"]
        dlast_e = jnp.sum(stp.astype(F32) * lamn, axis=0, keepdims=True) * elast_e
        lam_new = lamn * elast_e + _dot(cm.T.astype(BF16), dye_bf)
        lam[...] = lam_new
        lamo_ref[...] = lam_new

        hs_vx = hsum(v * xs)
        om = p["wend16"] * hs_vx
        x1 = p["e16"] * hsum(dyv * cs) - om
        x2 = p["dend16"] * hs_vx
        x3 = jnp.sum(om, axis=0, keepdims=True) + _dot_sel_r(jnp.broadcast_to(dlast_e, (8, D_MODEL)), rexp_t, n=2)[0:1, :]

        sub16 = lax.broadcasted_iota(jnp.int32, (SSD_HEADS, q), 0)
        rs = jnp.zeros((q, 128), F32)
        cs_m = jnp.zeros((SSD_HEADS, q), F32)
        dt_m = jnp.zeros((SSD_HEADS, q), F32)
        dg = jnp.zeros((q, q), F32)
        for pb in range(SSD_HEADS // 2):
            xp_bf = xs[:, pb * 128:(pb + 1) * 128].astype(BF16)
            dyp = dyv[:, pb * 128:(pb + 1) * 128]
            dxs_pair = None
            for half in range(2):
                h = 2 * pb + half
                sel = (lane128 < HEAD_DIM) if half == 0 else (lane128 >= HEAD_DIM)
                dyh_bf = jnp.where(sel, dyp, 0.0).astype(BF16)
                lm, gl, s = _ssd_head_mats(p, h)
                ds = _dot_nt(dyh_bf, xp_bf)
                t = _dot_tn(s.astype(BF16), dyh_bf)
                dxs_pair = t if dxs_pair is None else dxs_pair + t
                w = ds * s
                rs = rs + jnp.sum(w, axis=1, keepdims=True) * (lane128 == h).astype(F32)
                cs_m = jnp.where(sub16 == h, jnp.sum(w, axis=0, keepdims=True), cs_m)
                dt_m = jnp.where(sub16 == h, jnp.sum(ds * gl, axis=0, keepdims=True), dt_m)
                dg = dg + ds * lm * p["dt_t"][h:h + 1, :]
            sl = slice(pb * 128, (pb + 1) * 128)
            dxs_ref[:, sl] = (dxs_pair + v[:, sl] * wend_e[:, sl]).astype(BF16)

        def to_lanes(m16):
            return jnp.concatenate([m16, jnp.zeros((128 - SSD_HEADS, q), F32)], axis=0).T

        last = jnp.where(d == 0, q - 1, 0)
        dacum = rs - to_lanes(cs_m) + x1 + jnp.where(p["row"] == last, x3[0:1, :], 0.0)
        tri_t = jnp.where(p["mask_t"], 1.0, 0.0).astype(BF16)
        ddta = _dot_sel_l(tri_t, dacum)
        dt = p["dt"]
        a = p["a"]
        ddt = to_lanes(dt_m) + x2 + a * ddta
        ddtraw = jnp.where(p["head_lane"], ddt * _sigmoid(p["dtraw"]), 0.0)
        ddt_ref[...] = ddtraw
        acc_ref[0:1, :] += jnp.sum(ddtraw, axis=0, keepdims=True)
        acc_ref[1:2, :] += jnp.sum(dt * ddta, axis=0, keepdims=True) * a

        dg_bf = dg.astype(BF16)
        dbc_ref[:, 0:D_STATE] = _dot_tn(dg_bf, cm_bf) + db_off
        dbc_ref[:, D_STATE:2 * D_STATE] = _dot(dg_bf, bm_bf) + dc_off
        if ne:
            pl.when((d == 1) & (k == nc - 1))(e_finish)

    cblk = lambda d, k: (cidx(d, k), 0)
    return pl.pallas_call(
        body, name=name, grid=(2, nc),
        in_specs=[pl.BlockSpec((q, D_XBC), cblk),
                  pl.BlockSpec((q, 128), lambda d, k: (cidx(d, k), D_XBC // 128)),
                  pl.BlockSpec((None, 1, 128), lambda d, k: (d, 0, 0)),
                  pl.BlockSpec((None, 1, 128), lambda d, k: (d, 0, 0)),
                  pl.BlockSpec((128, D_MODEL), lambda d, k: (0, 0)),
                  pl.BlockSpec((D_MODEL, 128), lambda d, k: (0, 0)),
                  pl.BlockSpec((q, D_MODEL), cblk),
                  pl.BlockSpec((None, None, D_STATE, D_MODEL), lambda d, k: (d, cidx(d, k), 0, 0)),
                  pl.BlockSpec((None, D_STATE, D_MODEL), lambda d, k: (d, 0, 0))] + [_ANY] * ne,
        out_specs=[pl.BlockSpec((None, q, D_MODEL), lambda d, k: (d, cidx(d, k), 0)),
                   pl.BlockSpec((None, q, 2 * D_STATE), lambda d, k: (d, cidx(d, k), 0)),
                   pl.BlockSpec((None, q, 128), lambda d, k: (d, cidx(d, k), 0)),
                   pl.BlockSpec((None, 8, 128), lambda d, k: (d, 0, 0)),
                   pl.BlockSpec((None, D_STATE, D_MODEL), lambda d, k: (d, 0, 0))] + [_ANY] * ne,
        out_shape=[jax.ShapeDtypeStruct((2, n, D_MODEL), BF16),
                   jax.ShapeDtypeStruct((2, n, 2 * D_STATE), F32),
                   jax.ShapeDtypeStruct((2, n, 128), F32),
                   jax.ShapeDtypeStruct((2, 8, 128), F32),
                   jax.ShapeDtypeStruct((2, D_STATE, D_MODEL), F32)] +
                  [jax.ShapeDtypeStruct(t.shape, t.dtype) for t in exchange],
        scratch_shapes=[pltpu.VMEM((D_STATE, D_MODEL), F32)] +
                       ([pltpu.SemaphoreType.DMA((3 * ne,)), pltpu.SemaphoreType.DMA((3 * ne,)),
                         pltpu.SemaphoreType.DMA((ne,))] if ne else []),
        compiler_params=_cp(("arbitrary", "arbitrary")),
    )(u, xd, bias2, a2, rexp, rexp_t, dy, hprev, lam0, *exchange)


def _merge_fwd(y, u, z, dskip_e, gn, tb):
    n = z.shape[0]

    def body(y_ref, u_ref, z_ref, sk_ref, gn_ref, o_ref):
        sk, gnv = sk_ref[...], gn_ref[...]
        for r in _row_tiles(tb):
            ys = y_ref[0, r, :].astype(F32) + y_ref[1, r, :].astype(F32) + sk * _silu(u_ref[r, :].astype(F32))
            gated = ys * _silu(z_ref[r, :].astype(F32))
            rstd = lax.rsqrt(jnp.mean(gated * gated, axis=-1, keepdims=True) + LN_EPS)
            o_ref[r, :] = (gated * rstd * gnv).astype(BF16)

    return pl.pallas_call(
        body, name="merge_fwd", grid=(n // tb,),
        in_specs=[pl.BlockSpec((2, tb, D_MODEL), lambda i: (0, i, 0)), pl.BlockSpec((tb, D_MODEL), lambda i: (i, 0)),
                  _row_spec(tb, D_MODEL), _par_spec(D_MODEL), _par_spec(D_MODEL)],
        out_specs=_row_spec(tb, D_MODEL),
        out_shape=jax.ShapeDtypeStruct((n, D_MODEL), BF16),
        compiler_params=_cp(("parallel",)),
    )(y, u, z, dskip_e, gn)


def _merge_bwd(dyn, y, u, z, dskip_e, gn, tb):
    n = z.shape[0]

    def body(dyn_ref, y_ref, u_ref, z_ref, sk_ref, gn_ref, dy_ref, dz_ref, acc_ref):
        i = pl.program_id(0)

        @pl.when(i == 0)
        def _():
            acc_ref[...] = jnp.zeros_like(acc_ref)

        sk, gnv = sk_ref[...], gn_ref[...]
        part0 = jnp.zeros((8, D_MODEL), F32)
        part1 = jnp.zeros((8, D_MODEL), F32)
        for s in range(tb // ROW_SUB):
            r = slice(s * ROW_SUB, (s + 1) * ROW_SUB)
            xs = _silu(u_ref[r, :].astype(F32))
            zv = z_ref[r, :].astype(F32)
            sz = _sigmoid(zv)
            ys = y_ref[0, r, :].astype(F32) + y_ref[1, r, :].astype(F32) + sk * xs
            gated = ys * (zv * sz)
            rstd = lax.rsqrt(jnp.mean(gated * gated, axis=-1, keepdims=True) + LN_EPS)
            ghat = gated * rstd
            dyn_v = dyn_ref[r, :].astype(F32)
            t = dyn_v * gnv
            dgated = rstd * (t - ghat * jnp.mean(t * ghat, axis=-1, keepdims=True))
            dys = dgated * (zv * sz)
            dy_ref[r, :] = dys.astype(BF16)
            dz_ref[r, :] = (dgated * ys * (sz * (1.0 + zv * (1.0 - sz)))).astype(BF16)
            part0 = part0 + _fold8(dyn_v * ghat)
            part1 = part1 + _fold8(dys * xs)
        acc_ref[0:1, :] += jnp.sum(part0, axis=0, keepdims=True)
        acc_ref[1:2, :] += jnp.sum(part1, axis=0, keepdims=True)

    return pl.pallas_call(
        body, name="merge_bwd", grid=(n // tb,),
        in_specs=[_row_spec(tb, D_MODEL), pl.BlockSpec((2, tb, D_MODEL), lambda i: (0, i, 0)),
                  pl.BlockSpec((tb, D_MODEL), lambda i: (i, 0)), _row_spec(tb, D_MODEL),
                  _par_spec(D_MODEL), _par_spec(D_MODEL)],
        out_specs=[_row_spec(tb, D_MODEL), _row_spec(tb, D_MODEL), _acc_spec(D_MODEL)],
        out_shape=[jax.ShapeDtypeStruct((n, D_MODEL), BF16), jax.ShapeDtypeStruct((n, D_MODEL), BF16),
                   jax.ShapeDtypeStruct((8, D_MODEL), F32)],
        compiler_params=_cp(("arbitrary",)),
    )(dyn, y, u, z, dskip_e, gn)


def _pool_consts(transpose):
    tb = POOL_TB
    t = jnp.arange(tb)
    s = jnp.arange(3 * tb)
    rl, cl = t // GRID_W, t % GRID_W
    rs_, cs_ = s // GRID_W - tb // GRID_W, s % GRID_W
    s2 = jnp.arange(tb)
    rl2, cl2 = s2 // GRID_W, s2 % GRID_W
    brow, bcol = [], []
    for w in POOL_WINDOWS:
        lo, hi = -(w // 2), w - w // 2
        if transpose:
            lo, hi = -hi + 1, -lo + 1
        dr = rs_[None, :] - rl[:, None]
        before, after = _pool_halo(w, transpose)
        full = ((cs_[None, :] == cl[:, None]) & (dr >= lo) & (dr < hi)).astype(BF16)
        brow.append(full[:, tb - before:2 * tb + after])
        dc = cl2[None, :] - cl[:, None]
        bcol.append(((rl2[None, :] == rl[:, None]) & (dc >= lo) & (dc < hi)).astype(BF16))
    return brow, jnp.stack(bcol)


def _pool_halo(w, transpose):
    lo, hi = -(w // 2), w - w // 2
    if transpose:
        lo, hi = -hi + 1, -lo + 1
    return -lo * GRID_W, (hi - 1) * GRID_W


def _pool_inv(i, g, n):
    assert GRID_W == 64
    t = i * POOL_TB + lax.broadcasted_iota(jnp.int32, (POOL_TB, 1), 0)
    r = lax.shift_right_logical(t, 6)
    col = t & (GRID_W - 1)
    w = POOL_WINDOWS[g]
    lo, hi = -(w // 2), w - w // 2
    cnt_r = jnp.minimum(r + hi, n // GRID_W) - jnp.maximum(r + lo, 0)
    cnt_c = jnp.minimum(col + hi, GRID_W) - jnp.maximum(col + lo, 0)
    return 1.0 / (cnt_r * cnt_c).astype(F32)


def _pool_box(prev_ref, cur_ref, next_ref, brow_refs, bcol_ref, g, i, nb, transpose):
    tb = POOL_TB
    sl = slice(g * POOL_DIM, (g + 1) * POOL_DIM)
    before, after = _pool_halo(POOL_WINDOWS[g], transpose)
    pieces = []
    if before:
        pieces.append((prev_ref[tb - before:tb, sl] * (i > 0).astype(prev_ref.dtype)).astype(BF16))
    pieces.append(cur_ref[:, sl].astype(BF16))
    if after:
        pieces.append((next_ref[0:after, sl] * (i < nb - 1).astype(next_ref.dtype)).astype(BF16))
    r = _dot(brow_refs[g][...], jnp.concatenate(pieces, axis=0))
    return _dot(bcol_ref[g], r.astype(BF16))


def _pool_halo_specs(n, d):
    tb = POOL_TB
    nb = n // tb
    prev = pl.BlockSpec((tb, d), lambda i: (jnp.maximum(i - 1, 0), 0))
    cur = pl.BlockSpec((tb, d), lambda i: (i, 0))
    nxt = pl.BlockSpec((tb, d), lambda i: (jnp.minimum(i + 1, nb - 1), 0))
    return prev, cur, nxt


def _pool_const_specs(brow):
    tb = POOL_TB
    return [pl.BlockSpec(b.shape, lambda i: (0, 0)) for b in brow] + [pl.BlockSpec((N_POOL, tb, tb), lambda i: (0, 0, 0))]


def _pool_fwd(up, consts, pw_bf, pscale):
    n = up.shape[0]
    tb = POOL_TB
    nb = n // tb
    brow, bcol = consts
    prev, cur, nxt = _pool_halo_specs(n, D_MODEL)

    def body(p_ref, c_ref, n_ref, *rest):
        brow_refs, (bcol_ref, pw_ref, sc_ref, o_ref, d_ref) = rest[:N_POOL], rest[N_POOL:]
        i = pl.program_id(0)
        for g in range(N_POOL):
            sl = slice(g * POOL_DIM, (g + 1) * POOL_DIM)
            box = _pool_box(p_ref, c_ref, n_ref, brow_refs, bcol_ref, g, i, nb, False)
            dd = (box * _pool_inv(i, g, n) - c_ref[:, sl].astype(F32)).astype(BF16)
            d_ref[:, sl] = dd
            o_ref[:, sl] = (_dot(dd, pw_ref[g]) * sc_ref[:, sl]).astype(BF16)

    return pl.pallas_call(
        body, name="pool_fwd", grid=(nb,),
        in_specs=[prev, cur, nxt] + _pool_const_specs(brow) +
                 [pl.BlockSpec((N_POOL, POOL_DIM, POOL_DIM), lambda i: (0, 0, 0)), _par_spec(D_MODEL)],
        out_specs=[_row_spec(tb, D_MODEL), _row_spec(tb, D_MODEL)],
        out_shape=[jax.ShapeDtypeStruct((n, D_MODEL), BF16), jax.ShapeDtypeStruct((n, D_MODEL), BF16)],
        compiler_params=_cp(("parallel",)),
    )(up, up, up, *brow, bcol, pw_bf, pscale)


def _pool_bwd_a(dp, dsave, pw_bf, pwt_bf, pscale):
    n = dp.shape[0]
    tb = POOL_TB

    def body(dp_ref, d_ref, pw_ref, pwt_ref, sc_ref, dd_ref, dds_ref, gw_ref, gs_ref):
        i = pl.program_id(0)

        @pl.when(i == 0)
        def _():
            gw_ref[...] = jnp.zeros_like(gw_ref)
            gs_ref[...] = jnp.zeros_like(gs_ref)

        for g in range(N_POOL):
            sl = slice(g * POOL_DIM, (g + 1) * POOL_DIM)
            dpv = dp_ref[:, sl].astype(F32)
            dv = d_ref[:, sl]
            dpw_bf = (dpv * sc_ref[:, sl]).astype(BF16)
            dd = _dot(dpw_bf, pwt_ref[g])
            dd_ref[:, sl] = dd.astype(BF16)
            dds_ref[:, sl] = (dd * _pool_inv(i, g, n)).astype(BF16)
            gw_ref[g] += _dot_tn(dv, dpw_bf)
            gs_ref[0:1, sl] += jnp.sum(dpv * _dot(dv, pw_ref[g]), axis=0, keepdims=True)

    wspec = pl.BlockSpec((N_POOL, POOL_DIM, POOL_DIM), lambda i: (0, 0, 0))
    return pl.pallas_call(
        body, name="pool_bwd_a", grid=(n // tb,),
        in_specs=[_row_spec(tb, D_MODEL), _row_spec(tb, D_MODEL), wspec, wspec, _par_spec(D_MODEL)],
        out_specs=[_row_spec(tb, D_MODEL), _row_spec(tb, D_MODEL), wspec, _acc_spec(D_MODEL)],
        out_shape=[jax.ShapeDtypeStruct((n, D_MODEL), BF16), jax.ShapeDtypeStruct((n, D_MODEL), BF16),
                   jax.ShapeDtypeStruct((N_POOL, POOL_DIM, POOL_DIM), F32), jax.ShapeDtypeStruct((8, D_MODEL), F32)],
        compiler_params=_cp(("arbitrary",)),
    )(dp, dsave, pw_bf, pwt_bf, pscale)


def _pool_bwd_b(dds, dd, consts_t):
    n = dd.shape[0]
    tb = POOL_TB
    nb = n // tb
    brow, bcol = consts_t
    prev, cur, nxt = _pool_halo_specs(n, D_MODEL)

    def body(p_ref, c_ref, n_ref, *rest):
        brow_refs, (bcol_ref, dd_ref, o_ref) = rest[:N_POOL], rest[N_POOL:]
        i = pl.program_id(0)
        for g in range(N_POOL):
            sl = slice(g * POOL_DIM, (g + 1) * POOL_DIM)
            box = _pool_box(p_ref, c_ref, n_ref, brow_refs, bcol_ref, g, i, nb, True)
            o_ref[:, sl] = (box - dd_ref[:, sl].astype(F32)).astype(BF16)

    return pl.pallas_call(
        body, name="pool_bwd_b", grid=(nb,),
        in_specs=[prev, cur, nxt] + _pool_const_specs(brow) + [_row_spec(tb, D_MODEL)],
        out_specs=_row_spec(tb, D_MODEL),
        out_shape=jax.ShapeDtypeStruct((n, D_MODEL), BF16),
        compiler_params=_cp(("parallel",)),
    )(dds, dds, dds, *brow, bcol, dd)


def _pair_add(slabs, recvs, core, name):
    na = len(slabs)
    hr = [t.shape[1] // 4 for t in slabs]

    def body(core_ref, *refs):
        for a in range(na):
            refs[2 * na + a][...] = (refs[a][...] + refs[na + a][...]).astype(BF16)

    own = [pl.BlockSpec((None, hr[a], slabs[a].shape[2]), lambda j, i, c_ref: (j, 2 * c_ref[0] + i, 0)) for a in range(na)]
    got = [pl.BlockSpec((None, hr[a], slabs[a].shape[2]), lambda j, i, c_ref: (j, i, 0)) for a in range(na)]
    return pl.pallas_call(
        body, name=name,
        grid_spec=pltpu.PrefetchScalarGridSpec(num_scalar_prefetch=1, grid=(4, 2), in_specs=own + got, out_specs=got),
        out_shape=[jax.ShapeDtypeStruct(r.shape, BF16) for r in recvs],
        compiler_params=_cp(("arbitrary", "arbitrary")),
    )(core, *slabs, *recvs)


def _sum4(parts, core):
    na = len(parts)
    hr = [t.shape[1] // 2 for t in parts]

    def body(core_ref, *refs):
        for a in range(na):
            p = refs[a]
            refs[na + a][...] = ((p[0].astype(F32) + p[1].astype(F32)) + p[2].astype(F32)) + p[3].astype(F32)

    return pl.pallas_call(
        body, name="reduce_g_sum",
        grid_spec=pltpu.PrefetchScalarGridSpec(
            num_scalar_prefetch=1, grid=(2,),
            in_specs=[pl.BlockSpec((4, hr[a], parts[a].shape[2]), lambda i, c_ref: (0, i, 0)) for a in range(na)],
            out_specs=[pl.BlockSpec((hr[a], parts[a].shape[2]), lambda i, c_ref: (2 * c_ref[0] + i, 0))
                       for a in range(na)]),
        out_shape=[jax.ShapeDtypeStruct((2 * t.shape[1], t.shape[2]), F32) for t in parts],
        compiler_params=_cp(("arbitrary",)),
    )(core, *parts)


def _adamw(w, g, m, v, name):
    r, cdim = w.shape
    tb = _row_block(r, 256)
    c1 = 1.0 - ADAM_B1 ** ADAM_STEP
    c2 = 1.0 - ADAM_B2 ** ADAM_STEP

    def body(w_ref, g_ref, m_ref, v_ref, d_ref, nm_ref, nv_ref):
        gv = g_ref[...]
        nm = ADAM_B1 * m_ref[...] + (1.0 - ADAM_B1) * gv
        nv = ADAM_B2 * v_ref[...] + (1.0 - ADAM_B2) * (gv * gv)
        m_hat = nm / c1
        v_hat = nv / c2
        d_ref[...] = -ADAM_LR * (m_hat / (jnp.sqrt(v_hat) + ADAM_EPS) + ADAM_WD * w_ref[...])
        nm_ref[...] = nm
        nv_ref[...] = nv

    spec = _row_spec(tb, cdim)
    shp = jax.ShapeDtypeStruct((r, cdim), F32)
    return pl.pallas_call(
        body, name=name, grid=(r // tb,),
        in_specs=[spec] * 4, out_specs=[spec] * 3, out_shape=[shp] * 3,
        compiler_params=_cp(("parallel",)),
    )(w, g, m, v)


def _mesh_pos():
    return lax.axis_index("x"), lax.axis_index("y"), lax.axis_index("c")


_ANY = pl.BlockSpec(memory_space=pl.ANY)


def _remote(src, dst, send_sem, recv_sem, device):
    return pltpu.make_async_remote_copy(src_ref=src, dst_ref=dst, send_sem=send_sem, recv_sem=recv_sem,
                                        device_id=device, device_id_type=MESH)


def _other_chips(x, y):
    return [(1 - x, y), (x, 1 - y), (1 - x, 1 - y)]


def _half(nrows, h):
    return pl.ds(h * (nrows // 2), nrows // 2)


_GATHER_SEMS = 7


def _gather_steps(ins, outs, send_sems, recv_sems):
    na = len(ins)
    nrow = [r.shape[0] for r in ins]

    def copies():
        x, y, c = _mesh_pos()
        me = 2 * x + y
        sib = (x, y, 1 - c)
        chips = _other_chips(x, y)

        def ici(k, a, slot):
            px, py = chips[k]
            rows = _half(nrow[a], c)
            return _remote(ins[a].at[rows, :], outs[a].at[slot, rows, :], send_sems.at[k * na + a],
                           recv_sems.at[k * na + a], (px, py, c))

        def fwd(k, a, h):
            px, py = chips[k]
            blk = outs[a].at[2 * px + py, _half(nrow[a], h), :]
            return _remote(blk, blk, send_sems.at[(3 + k) * na + a], recv_sems.at[(3 + k) * na + a], sib)

        def own(a):
            return _remote(ins[a], outs[a].at[me], send_sems.at[6 * na + a], recv_sems.at[6 * na + a], sib)

        slots = [2 * px + py for px, py in chips]
        return ici, fwd, own, me, c, slots

    def start():
        ici, _, own, me, _, _ = copies()
        for a in range(na):
            own(a).start()
        for k in range(3):
            for a in range(na):
                ici(k, a, me).start()

    def finish():
        ici, fwd, own, me, c, slots = copies()
        for k in range(3):
            for a in range(na):
                ici(k, a, slots[k]).wait_recv()
                fwd(k, a, c).start()
        for k in range(3):
            for a in range(na):
                fwd(k, a, 1 - c).wait_recv()
        for a in range(na):
            own(a).wait_recv()
        for a in range(na):
            own(a).wait_send()
        for k in range(3):
            for a in range(na):
                ici(k, a, me).wait_send()
                fwd(k, a, c).wait_send()

    return start, finish


def _exchange_steps(ins, outs, send_sems, recv_sems, local_sems):
    na = len(ins)

    def copies():
        x, y, c = _mesh_pos()
        me = 2 * x + y
        chips = _other_chips(x, y)

        def copy(k, a, slot):
            px, py = chips[k]
            return _remote(ins[a].at[2 * px + py], outs[a].at[slot], send_sems.at[k * na + a], recv_sems.at[k * na + a],
                           (px, py, c))

        def local(a):
            return pltpu.make_async_copy(ins[a].at[me], outs[a].at[me], local_sems.at[a])

        return copy, local, me, [2 * px + py for px, py in chips]

    def start():
        copy, local, me, _ = copies()
        for a in range(na):
            local(a).start()
        for k in range(3):
            for a in range(na):
                copy(k, a, me).start()

    def finish():
        copy, local, me, slots = copies()
        for k in range(3):
            for a in range(na):
                copy(k, a, slots[k]).wait_recv()
        for k in range(3):
            for a in range(na):
                copy(k, a, me).wait_send()
        for a in range(na):
            local(a).wait()

    return start, finish


def _gather_weights(shards, conv8):
    na = len(shards)

    def body(*refs):
        ins, conv_in = refs[:na], refs[na]
        outs, conv_out = refs[na + 1:2 * na + 1], refs[2 * na + 1]
        send_sems, recv_sems, local_sems = refs[2 * na + 2:]
        x, y, c = _mesh_pos()
        me = 2 * x + y
        chips = _other_chips(x, y)

        def conv(k, slot):
            px, py = chips[k]
            return _remote(conv_in, conv_out.at[slot], send_sems.at[7 * na + k], recv_sems.at[7 * na + k], (px, py, c))

        start, finish = _gather_steps(ins, outs, send_sems, recv_sems)
        local = pltpu.make_async_copy(conv_in, conv_out.at[me], local_sems.at[0])
        local.start()
        start()
        sends = [conv(k, me) for k in range(3)]
        for cp in sends:
            cp.start()
        finish()
        for k in range(3):
            px, py = chips[k]
            conv(k, 2 * px + py).wait_recv()
        for cp in sends:
            cp.wait_send()
        local.wait()

    nsem = _GATHER_SEMS * na + 3
    return pl.pallas_call(
        body, name="gather_w", in_specs=[_ANY] * (na + 1), out_specs=[_ANY] * (na + 1),
        out_shape=[jax.ShapeDtypeStruct((4,) + t.shape, t.dtype) for t in shards] +
                  [jax.ShapeDtypeStruct((4,) + conv8.shape, conv8.dtype)],
        scratch_shapes=[pltpu.SemaphoreType.DMA((nsem,)), pltpu.SemaphoreType.DMA((nsem,)),
                        pltpu.SemaphoreType.DMA((1,))],
    )(*shards, conv8)


def _pair_swap(slabs, name):
    na = len(slabs)

    def body(*refs):
        ins, outs = refs[:na], refs[na:2 * na]
        send_sems, recv_sems = refs[2 * na:]
        x, y, c = _mesh_pos()
        cps = [_remote(ins[a].at[:, _half(slabs[a].shape[1], 1 - c), :], outs[a], send_sems.at[a], recv_sems.at[a],
                       (x, y, 1 - c)) for a in range(na)]
        for cp in cps:
            cp.start()
        for cp in cps:
            cp.wait()

    return pl.pallas_call(
        body, name=name, in_specs=[_ANY] * na, out_specs=[_ANY] * na,
        out_shape=[jax.ShapeDtypeStruct((4, t.shape[1] // 2, t.shape[2]), t.dtype) for t in slabs],
        scratch_shapes=[pltpu.SemaphoreType.DMA((na,)), pltpu.SemaphoreType.DMA((na,))],
    )(*slabs)


def _chip_exchange(pairs):
    na = len(pairs)

    def body(*refs):
        start, finish = _exchange_steps(refs[:na], refs[na:2 * na], *refs[2 * na:])
        start()
        finish()

    return pl.pallas_call(
        body, name="reduce_g_ici", in_specs=[_ANY] * na, out_specs=[_ANY] * na,
        out_shape=[jax.ShapeDtypeStruct(t.shape, t.dtype) for t in pairs],
        scratch_shapes=[pltpu.SemaphoreType.DMA((3 * na,)), pltpu.SemaphoreType.DMA((3 * na,)),
                        pltpu.SemaphoreType.DMA((na,))],
    )(*pairs)


def _share_halves(totals):
    na = len(totals)

    def body(*refs):
        bufs = refs[na:2 * na]
        send_sems, recv_sems = refs[2 * na:]
        x, y, c = _mesh_pos()

        def copy(a, h):
            blk = bufs[a].at[_half(totals[a].shape[0], h), :]
            return _remote(blk, blk, send_sems.at[a], recv_sems.at[a], (x, y, 1 - c))

        sends = [copy(a, c) for a in range(na)]
        for cp in sends:
            cp.start()
        for a in range(na):
            copy(a, 1 - c).wait_recv()
        for cp in sends:
            cp.wait_send()

    return pl.pallas_call(
        body, name="reduce_g_share", in_specs=[_ANY] * na, out_specs=[_ANY] * na,
        out_shape=[jax.ShapeDtypeStruct(t.shape, t.dtype) for t in totals],
        input_output_aliases={a: a for a in range(na)},
        scratch_shapes=[pltpu.SemaphoreType.DMA((na,)), pltpu.SemaphoreType.DMA((na,))],
    )(*totals)


def _allreduce_small(v, name):
    r, cdim = v.shape

    def body(v_ref, out_ref, buf, send_sems, recv_sems):
        x, y, c = _mesh_pos()
        me = 4 * x + 2 * y + c
        buf[me] = v_ref[...]
        rel = [(bx, by, bc) for bx in (0, 1) for by in (0, 1) for bc in (0, 1)][1:]

        def peer(b):
            bx, by, bc = b
            return ((1 - x) if bx else x, (1 - y) if by else y, (1 - c) if bc else c)

        def copy(k, slot):
            return pltpu.make_async_remote_copy(
                src_ref=v_ref, dst_ref=buf.at[slot], send_sem=send_sems.at[k], recv_sem=recv_sems.at[k],
                device_id=peer(rel[k]), device_id_type=MESH)

        sends = [copy(k, me) for k in range(7)]
        for cp in sends:
            cp.start()
        for k in range(7):
            px, py, pc = peer(rel[k])
            copy(k, 4 * px + 2 * py + pc).wait_recv()
        for cp in sends:
            cp.wait_send()
        acc = buf[0]
        for j in range(1, 8):
            acc = acc + buf[j]
        out_ref[...] = acc

    vm = pl.BlockSpec(memory_space=pltpu.VMEM)
    return pl.pallas_call(
        body, name=name, in_specs=[vm], out_specs=[vm, vm],
        out_shape=[jax.ShapeDtypeStruct((r, cdim), F32), jax.ShapeDtypeStruct((8, r, cdim), F32)],
        scratch_shapes=[pltpu.SemaphoreType.DMA((7,)), pltpu.SemaphoreType.DMA((7,))],
    )(v)


def _chip_bcast(v, name):
    def body(v_ref, out_ref, send_sems, recv_sems):
        x, y, c = _mesh_pos()
        me = 2 * x + y
        chips = _other_chips(x, y)
        out_ref[me] = v_ref[...]

        def copy(k, slot):
            px, py = chips[k]
            return _remote(v_ref, out_ref.at[slot], send_sems.at[k], recv_sems.at[k], (px, py, c))

        sends = [copy(k, me) for k in range(3)]
        for cp in sends:
            cp.start()
        for k, (px, py) in enumerate(chips):
            copy(k, 2 * px + py).wait_recv()
        for cp in sends:
            cp.wait_send()

    vm = pl.BlockSpec(memory_space=pltpu.VMEM)
    return pl.pallas_call(
        body, name=name, in_specs=[vm], out_specs=vm,
        out_shape=jax.ShapeDtypeStruct((4,) + v.shape, F32),
        scratch_shapes=[pltpu.SemaphoreType.DMA((3,)), pltpu.SemaphoreType.DMA((3,))],
    )(v)


_BIG = (("in_proj", (D_MODEL, D_IN_PROJ // 4), 1), ("w_out", (2 * D_MODEL // 4, D_MODEL), 0),
        ("w_gate", (D_MODEL, D_FF // 4), 1), ("w_up", (D_MODEL, D_FF // 4), 1), ("w_down", (D_FF // 4, D_MODEL), 0),
        ("pool_w", (N_POOL * POOL_DIM // 4, POOL_DIM), None))


def _assemble(name, t):
    _, r, c = t.shape
    axis = {n: ax for n, _, ax in _BIG}[name]
    if axis == 0:
        return t.reshape(4 * r, c)
    if axis == 1:
        return t.transpose(1, 0, 2).reshape(r, 4 * c)
    return t.reshape(4, N_POOL, POOL_DIM // 4, POOL_DIM).transpose(1, 0, 2, 3).reshape(N_POOL, POOL_DIM, POOL_DIM)


def _to_slabs(name, g):
    (r, c), axis = {n: (sh, ax) for n, sh, ax in _BIG}[name]
    if axis == 0:
        return g.reshape(4, r, c)
    if axis == 1:
        return g.reshape(r, 4, c).transpose(1, 0, 2)
    return g.reshape(N_POOL, 4, POOL_DIM // 4, POOL_DIM).transpose(1, 0, 2, 3).reshape(4, r, c)


_EARLY = ("in_proj",)
_LATE = tuple(n for n, _, _ in _BIG if n not in _EARLY)


def _reduce_grads(early_grads, late_parts, core):
    slabs = [_to_slabs(n, early_grads[n]) for n in _EARLY]
    pairs = _pair_add(slabs, _pair_swap(slabs, "reduce_g_d2d"), core, "reduce_g_pair")
    parts = dict(zip(_EARLY, _chip_exchange(pairs)), **dict(zip(_LATE, late_parts)))
    names = [n for n, _, _ in _BIG]
    totals = _sum4([parts[n] for n in names], core)
    return dict(zip(names, _share_halves(totals)))


def _pad_cols(w, n):
    return jnp.concatenate([w, jnp.zeros((w.shape[0], n - w.shape[1]), w.dtype)], axis=1)


def _device_step(x, mod, mod_ctx, ctx, target, wts, w8, small, tb, late_shards=None, core=None):
    n = x.shape[0]
    d = D_MODEL

    win = wts["in_proj"]
    wz, wxd, wup = win[:, 0:d], _pad_cols(win[:, d:d + D_XBC + 2 * SSD_HEADS], D_XD), win[:, d + D_XBC + 2 * SSD_HEADS:]

    emb_g, emb_b = _vec(small["emb_ln_g"]), _vec(small["emb_ln_b"])
    ln1_g, ln1_b = _vec(small["ln1_g"]), _vec(small["ln1_b"])
    ln2_g, ln2_b = _vec(small["ln2_g"]), _vec(small["ln2_b"])
    gn = _vec(small["ssd_norm_g"])
    pscale = _vec(small["pool_scale"])
    conv_b = _vec(small["conv_b"])
    dskip_e = jnp.repeat(small["d_skip"].reshape(-1), HEAD_DIM).reshape(1, d)
    zpad = jnp.zeros((2, 1, 128 - SSD_HEADS), F32)
    bias2 = jnp.concatenate([small["dt_bias"].reshape(2, 1, SSD_HEADS), zpad], axis=2)
    a2 = jnp.concatenate([-jnp.exp(small["a_log"].reshape(2, 1, SSD_HEADS)), zpad], axis=2)
    rexp = (jnp.arange(128)[:, None] == (jnp.arange(d)[None, :] // HEAD_DIM)).astype(BF16)
    rexp_t = rexp.T

    sh1, sc1, g1, sh2, sc2, g2 = [mod[:, i * d:(i + 1) * d] for i in range(6)]
    sh1c, sc1c = mod_ctx[:, 0:d], mod_ctx[:, d:2 * d]

    tbc = min(tb, ctx.shape[0])
    xc0, hc = _ln_mod(ctx, emb_g, emb_b, sh1c, sc1c, tbc, "ln_mod_ctx")
    xdc = _matmul_nn([(hc, wxd)], F32, 512, D_XD, "in_proj_ctx")
    uc, ac = _conv_fwd(xdc, w8, conv_b, tbc, "conv_fwd_ctx")
    hzero = jnp.zeros((2, D_STATE, d), F32)
    _, hprev_c, hfin_c = _ssd_fwd(ac, xdc, bias2, a2, rexp, hzero, "ssd_fwd_ctx")

    x0, h1 = _ln_mod(x, emb_g, emb_b, sh1, sc1, tb, "ln_mod")
    z, xd, up = _matmul_nn_multi(h1, [wz, wxd, wup], [BF16, F32, BF16], 512, "in_proj")
    u, act = _conv_fwd(xd, w8, conv_b, tb, "conv_fwd")
    y, hprev, _, *landed = _ssd_fwd(act, xd, bias2, a2, rexp, hfin_c, "ssd_fwd", gather=late_shards or ())
    if late_shards is not None:
        wts = dict(wts, **{nme: _assemble(nme, t) for nme, t in zip(_LATE, landed)})
    wout = wts["w_out"]
    wg, wu, wd = wts["w_gate"], wts["w_up"], wts["w_down"]
    pw = wts["pool_w"]
    yn = _merge_fwd(y, u, z, dskip_e, gn, tb)
    pconst = _pool_consts(False)
    pool, dsave = _pool_fwd(up, pconst, pw, pscale)
    mix = _matmul_nn([(yn, wout[0:d]), (pool, wout[d:2 * d])], BF16, MM_ROWS, 1024, "out_proj")
    x1, h2 = _res_ln(x0, mix, g1, ln1_g, ln1_b, sh2, sc2, tb)

    gate, upp, hmid = _swiglu_fwd(h2, wg, wu, 512, D_FF // 2)
    ffn = _matmul_nn([(hmid, wd)], BF16, MM_ROWS, 1024, "ffn_down")
    dffn, dr2, acc2 = _final_ln_loss(x1, ffn, g2, ln2_g, ln2_b, target, tb)
    loss = (0.5 / d) * jnp.sum(acc2[3])

    dgate, dupp = _swiglu_bwd(dffn, wd.T, gate, upp, 512, D_FF // 2)
    g_wdown = _matmul_tn(hmid, dffn, MM_ROWS, 1024, "g_w_down")
    g_wgate, g_wup = _matmul_tn_multi(h2, [dgate, dupp], MM_ROWS, [D_FF // 2, D_FF // 2], "g_w_gate_up")
    dh2 = _matmul_nn([(dgate, wg.T), (dupp, wu.T)], BF16, 512, 1024, "d_h2")
    dmix, dr1, acc1 = _bwd_ln1(dr2, dh2, x1, x0, mix, g1, sc2, ln1_g, tb)

    dyn, dpool = _matmul_nn_multi(dmix, [wout[0:d].T, wout[d:2 * d].T], [BF16, BF16], MM_ROWS, "d_yn_pool")
    g_wout = jnp.concatenate([_matmul_tn(yn, dmix, MM_ROWS, 1024, "g_w_out_a"),
                              _matmul_tn(pool, dmix, MM_ROWS, 1024, "g_w_out_b")], axis=0)
    dd, dds, g_pw, accp = _pool_bwd_a(dpool, dsave, pw, jnp.swapaxes(pw, 1, 2), pscale)
    dup = _pool_bwd_b(dds, dd, _pool_consts(True))
    dy, dz, accm = _merge_bwd(dyn, y, u, z, dskip_e, gn, tb)
    lam0 = jnp.zeros((2, D_STATE, d), F32)
    late_grads = dict(w_out=g_wout, w_gate=g_wgate, w_up=g_wup, w_down=g_wdown, pool_w=g_pw)
    pairs = ()
    if late_shards is not None:
        slabs = [_to_slabs(nme, late_grads[nme]) for nme in _LATE]
        pairs = _pair_add(slabs, _pair_swap(slabs, "reduce_g_d2d_late"), core, "reduce_g_pair_late")
    dxs, dbc, ddt, accs, lam_c, *arrived = _ssd_bwd(act, xd, bias2, a2, rexp, rexp_t, dy, hprev, lam0, "ssd_bwd",
                                                    exchange=pairs)
    du, accb = _conv_bwd_a(dxs, dy, dskip_e, dbc, u, tb, "conv_bwd_a")
    dxd, accw = _conv_bwd_b(du, xd, ddt, w8, tb, "conv_bwd_b")

    lc = ctx.shape[0]
    zeros_c = jnp.zeros((lc, d), BF16)
    dxs_c, dbc_c, ddt_c, accs_c, _ = _ssd_bwd(ac, xdc, bias2, a2, rexp, rexp_t, zeros_c, hprev_c, lam_c, "ssd_bwd_ctx")
    du_c, accb_c = _conv_bwd_a(dxs_c, zeros_c, dskip_e, dbc_c, uc, tbc, "conv_bwd_a_ctx")
    dxd_c, accw_c = _conv_bwd_b(du_c, xdc, ddt_c, w8, tbc, "conv_bwd_b_ctx")
    dhc = _matmul_nn([(dxd_c, wxd.T)], F32, 512, 1024, "d_hc")
    _, acc0c = _bwd_ln0(None, dhc, ctx, emb_g, emb_b, sc1c, tbc, "bwd_ln0_ctx")

    dh1 = _matmul_nn([(dz, wz.T), (dxd, wxd.T), (dup, wup.T)], BF16, MM_ROWS, 1024, "d_h1")
    g_wz, g_wxd, g_wpo = _matmul_tn_multi(h1, [dz, dxd, dup], 512, [1024, D_XD, 1024], "g_in_proj")
    g_wxd = g_wxd + _matmul_tn(hc, dxd_c, 512, D_XD, "g_in_proj_xd_ctx")
    g_win = jnp.concatenate([g_wz, g_wxd[:, 0:D_XBC + 2 * SSD_HEADS], g_wpo], axis=1)
    grad_x, acc0 = _bwd_ln0(dr1, dh1, x, emb_g, emb_b, sc1, tb, "bwd_ln0")

    zero_d = jnp.zeros((1, d), F32)
    dmod = jnp.concatenate([acc0[1:2], acc0[0:1], acc1[4:5], acc1[1:2], acc1[0:1], acc2[2:3]], axis=1)
    dmodc = jnp.concatenate([acc0c[1:2], acc0c[0:1]] + [zero_d] * 4, axis=1)

    big = dict(in_proj=g_win)
    if late_shards is None:
        big.update(late_grads)
    sml = dict(
        dmod=dmod, dmod_ctx=dmodc, emb_ln_g=acc0[2] + acc0c[2], emb_ln_b=acc0[3] + acc0c[3],
        conv_w=accw[0:D_CONV] + accw_c[0:D_CONV], conv_b=accb[0] + accb_c[0],
        dt_bias=accs[:, 0, 0:SSD_HEADS] + accs_c[:, 0, 0:SSD_HEADS],
        a_log=accs[:, 1, 0:SSD_HEADS] + accs_c[:, 1, 0:SSD_HEADS],
        d_skip=jnp.sum(accm[1].reshape(SSD_HEADS, HEAD_DIM), axis=1),
        ssd_norm_g=accm[0], pool_scale=accp[0], ln1_g=acc1[2], ln1_b=acc1[3], ln2_g=acc2[0], ln2_b=acc2[1])
    return loss, grad_x, big, sml, (arrived if late_shards is not None else None)


_SMALL = ("c_ctx", "emb_ln_g", "emb_ln_b", "b_ada", "conv_w", "conv_b", "dt_bias", "a_log", "d_skip",
          "ssd_norm_g", "pool_scale", "ln1_g", "ln1_b", "ln2_g", "ln2_b")


def _small_rows(size):
    return -(-size // 1024)


def _pack_small(vals, names):
    pieces, rows = [], 0
    for nme in names:
        flat = vals[nme].reshape(-1).astype(F32)
        nr = _small_rows(flat.shape[0])
        pieces.append(flat)
        if nr * 1024 > flat.shape[0]:
            pieces.append(jnp.zeros((nr * 1024 - flat.shape[0],), F32))
        rows += nr
    if rows % 8:
        pieces.append(jnp.zeros(((8 - rows % 8) * 1024,), F32))
    return jnp.concatenate(pieces).reshape(-1, 1024)


def _small_offsets(shapes, names):
    out, off = {}, 0
    for nme in names:
        nr = _small_rows(math.prod(shapes[nme]))
        out[nme] = (off, nr)
        off += nr
    return out


def _unpack_small(packed, shapes, names):
    out = {}
    for nme, (off, nr) in _small_offsets(shapes, names).items():
        out[nme] = packed[off:off + nr].reshape(-1)[:math.prod(shapes[nme])].reshape(shapes[nme])
    return out


_WEIGHT_ORDER = ("c_ctx", "emb_ln_g", "emb_ln_b", "w_ada", "b_ada", "in_proj", "conv_w", "conv_b", "dt_bias", "a_log",
                 "d_skip", "ssd_norm_g", "pool_w", "pool_scale", "w_out", "ln1_g", "ln1_b", "w_gate", "w_up", "w_down",
                 "ln2_g", "ln2_b")


def _as2d(a):
    return a.reshape(-1, a.shape[-1])


def kernel(x, c, ctx, c_ctx, emb_ln_g, emb_ln_b, w_ada, b_ada, in_proj, conv_w, conv_b, dt_bias, a_log, d_skip, ssd_norm_g, pool_w, pool_scale, w_out, ln1_g, ln1_b, w_gate, w_up, w_down, ln2_g, ln2_b, loss_target, m_c_ctx, m_emb_ln_g, m_emb_ln_b, m_w_ada, m_b_ada, m_in_proj, m_conv_w, m_conv_b, m_dt_bias, m_a_log, m_d_skip, m_ssd_norm_g, m_pool_w, m_pool_scale, m_w_out, m_ln1_g, m_ln1_b, m_w_gate, m_w_up, m_w_down, m_ln2_g, m_ln2_b, v_c_ctx, v_emb_ln_g, v_emb_ln_b, v_w_ada, v_b_ada, v_in_proj, v_conv_w, v_conv_b, v_dt_bias, v_a_log, v_d_skip, v_ssd_norm_g, v_pool_w, v_pool_scale, v_w_out, v_ln1_g, v_ln1_b, v_w_gate, v_w_up, v_w_down, v_ln2_g, v_ln2_b):
    w = dict(c_ctx=c_ctx, emb_ln_g=emb_ln_g, emb_ln_b=emb_ln_b, w_ada=w_ada, b_ada=b_ada, in_proj=in_proj, conv_w=conv_w,
             conv_b=conv_b, dt_bias=dt_bias, a_log=a_log, d_skip=d_skip, ssd_norm_g=ssd_norm_g, pool_w=pool_w,
             pool_scale=pool_scale, w_out=w_out, ln1_g=ln1_g, ln1_b=ln1_b, w_gate=w_gate, w_up=w_up, w_down=w_down,
             ln2_g=ln2_g, ln2_b=ln2_b)
    m = dict(c_ctx=m_c_ctx, emb_ln_g=m_emb_ln_g, emb_ln_b=m_emb_ln_b, w_ada=m_w_ada, b_ada=m_b_ada, in_proj=m_in_proj,
             conv_w=m_conv_w, conv_b=m_conv_b, dt_bias=m_dt_bias, a_log=m_a_log, d_skip=m_d_skip,
             ssd_norm_g=m_ssd_norm_g, pool_w=m_pool_w, pool_scale=m_pool_scale, w_out=m_w_out, ln1_g=m_ln1_g,
             ln1_b=m_ln1_b, w_gate=m_w_gate, w_up=m_w_up, w_down=m_w_down, ln2_g=m_ln2_g, ln2_b=m_ln2_b)
    v = dict(c_ctx=v_c_ctx, emb_ln_g=v_emb_ln_g, emb_ln_b=v_emb_ln_b, w_ada=v_w_ada, b_ada=v_b_ada, in_proj=v_in_proj,
             conv_w=v_conv_w, conv_b=v_conv_b, dt_bias=v_dt_bias, a_log=v_a_log, d_skip=v_d_skip,
             ssd_norm_g=v_ssd_norm_g, pool_w=v_pool_w, pool_scale=v_pool_scale, w_out=v_w_out, ln1_g=v_ln1_g,
             ln1_b=v_ln1_b, w_gate=v_w_gate, w_up=v_w_up, w_down=v_w_down, ln2_g=v_ln2_g, ln2_b=v_ln2_b)

    xi, yi, ci = _mesh_pos()
    chip = 2 * xi + yi

    dev = 4 * xi + 2 * yi + ci
    d = D_MODEL
    core = ci.reshape(1).astype(jnp.int32)

    crow = jnp.concatenate([c.reshape(1, d), jnp.zeros((7, d), F32)], axis=0)
    _, c_all = _allreduce_small(crow, "gather_c")
    c16 = jnp.concatenate([c_all[:, 0, :], c_ctx.reshape(1, d), jnp.zeros((MOD_ROWS - 9, d), F32)], axis=0)
    ncol = w_ada.shape[-1]
    wada_bf = w_ada[0].astype(BF16)
    b_mine = lax.dynamic_slice_in_dim(b_ada, chip * ncol, ncol, axis=1)
    mods4 = _chip_bcast(_mods_fwd(c16, wada_bf, b_mine), "gather_mods")
    mods = mods4.transpose(1, 0, 2).reshape(MOD_ROWS, 4 * ncol)
    mod = lax.dynamic_slice_in_dim(mods, dev, 1, axis=0)
    mod_ctx = mods[8:9]

    shard = {name: w[name][0].astype(BF16).reshape(shp) for name, shp, _ in _BIG}
    conv8 = jnp.concatenate([conv_w[0], jnp.zeros((8 - D_CONV, conv_w.shape[-1]), F32)], axis=0)
    *gathered, conv4 = _gather_weights([shard[nme] for nme in _EARLY], conv8)
    wts = {nme: _assemble(nme, t) for nme, t in zip(_EARLY, gathered)}
    w8 = conv4.transpose(1, 0, 2).reshape(8, D_XBC)
    small = {nme: (w[nme] if nme in ("c_ctx", "emb_ln_g", "emb_ln_b") else w[nme][0]) for nme in _SMALL if nme != "conv_w"}

    loss, grad_x, big, sml, late_parts = _device_step(x[0], mod, mod_ctx, ctx[0], loss_target[0], wts, w8, small, 512,
                                                      late_shards=[shard[nme] for nme in _LATE], core=core)
    loss = lax.psum(loss, ("x", "y", "c"))

    g_big = _reduce_grads(big, late_parts, core)
    reduced = tuple(sml)
    small_shapes = {nme: sml[nme].shape for nme in reduced}
    total, each = _allreduce_small(_pack_small(sml, reduced), "reduce_small")
    g_small = _unpack_small(total, small_shapes, reduced)
    cw_cols = conv_w.shape[-1]
    g_small["conv_w"] = lax.dynamic_slice_in_dim(g_small["conv_w"], chip * cw_cols, cw_cols, axis=1)

    off, nr = _small_offsets(small_shapes, reduced)["dmod"]
    dm16 = jnp.concatenate([each[:, off:off + nr, :].reshape(8, nr * 1024)[:, :6 * d], g_small["dmod_ctx"],
                            jnp.zeros((MOD_ROWS - 9, 6 * d), F32)], axis=0)
    dm_mine = lax.dynamic_slice_in_dim(dm16, chip * ncol, ncol, axis=1)
    g_wada = _mods_bwd_w(c16.T, dm_mine)
    g_small["b_ada"] = _mods_bwd_b(dm16)[0:1]
    c_part = _mods_bwd_c(dm_mine, wada_bf, c16)[8:16]
    g_small["c_ctx"] = _allreduce_small(c_part, "reduce_c_ctx")[0][0]

    grads, delta, new_m, new_v = {}, {}, {}, {}
    grads["w_ada"] = g_wada[None]
    delta["w_ada"], new_m["w_ada"], new_v["w_ada"] = (
        t[None] for t in _adamw(w_ada[0], g_wada, m_w_ada[0], v_w_ada[0], "adamw_w_ada"))
    for name, _, _ in _BIG:
        g2 = _as2d(g_big[name])
        d2, m2, v2 = _adamw(_as2d(w[name][0]), g2, _as2d(m[name][0]), _as2d(v[name][0]), "adamw_" + name)
        grads[name] = g2.reshape(w[name].shape)
        delta[name], new_m[name], new_v[name] = (t.reshape(w[name].shape) for t in (d2, m2, v2))
    shp = {nme: w[nme].shape for nme in _SMALL}
    gp = _pack_small(g_small, _SMALL)
    dp, mp, vp = _adamw(_pack_small(w, _SMALL), gp, _pack_small(m, _SMALL), _pack_small(v, _SMALL), "adamw_small")
    for dst, src in ((grads, gp), (delta, dp), (new_m, mp), (new_v, vp)):
        dst.update(_unpack_small(src, shp, _SMALL))

    return (loss, grad_x[None], *[grads[nme] for nme in _WEIGHT_ORDER], *[delta[nme] for nme in _WEIGHT_ORDER],
            *[new_m[nme] for nme in _WEIGHT_ORDER], *[new_v[nme] for nme in _WEIGHT_ORDER])
```

```python
import math

import jax
import jax.numpy as jnp
from jax import lax
from jax.experimental import pallas as pl
from jax.experimental.pallas import tpu as pltpu

F32 = jnp.float32
BF16 = jnp.bfloat16
MESH = pl.DeviceIdType.MESH

D_MODEL = 1024
SSD_HEADS = 16
HEAD_DIM = 64
D_STATE = 128
CHUNK = 128
D_CONV = 5
D_XBC = D_MODEL + 2 * D_STATE
D_XD = 1408
N_POOL = 4
POOL_DIM = 256
POOL_WINDOWS = (2, 4, 8, 16)
GRID_W = 64
D_FF = 2816
D_IN_PROJ = 3360
LN_EPS = 1e-5
ALPHA = 2.0 ** 0.25
POOL_TB = 512
MM_ROWS = 1024

ADAM_LR = 0.001
ADAM_B1 = 0.9
ADAM_B2 = 0.999
ADAM_EPS = 1e-08
ADAM_WD = 0.01
ADAM_STEP = 10

VMEM_LIMIT = 56 * 1024 * 1024


def _cp(sem=None):
    return pltpu.CompilerParams(dimension_semantics=sem, vmem_limit_bytes=VMEM_LIMIT)


def _sigmoid(x):
    return 1.0 / (1.0 + jnp.exp(-x))


def _silu(x):
    return x * _sigmoid(x)


def _dsilu(x):
    s = _sigmoid(x)
    return s * (1.0 + x * (1.0 - s))


def _softplus(x):
    t = jnp.exp(-jnp.abs(x))
    u = 1.0 + t
    log1p = jnp.where(u == 1.0, t, jnp.log(u) * t / (u - 1.0 + (u == 1.0)))
    return jnp.maximum(x, 0.0) + log1p


def _split(x, n):
    parts, r = [], x
    for _ in range(n):
        p = r.astype(BF16)
        parts.append(p)
        r = r - p.astype(F32)
    return parts


def _dot(a, b):
    return jnp.dot(a, b, preferred_element_type=F32)


def _dot_nt(a, b):
    return lax.dot_general(a, b, (((1,), (1,)), ((), ())), preferred_element_type=F32)


def _dot_tn(a, b):
    return lax.dot_general(a, b, (((0,), (0,)), ((), ())), preferred_element_type=F32)


def _dot_sel_l(sel_bf, x, n=3):
    out = None
    for p in _split(x, n):
        t = _dot(sel_bf, p)
        out = t if out is None else out + t
    return out


def _dot_sel_r(x, sel_bf, n=3):
    out = None
    for p in _split(x, n):
        t = _dot(p, sel_bf)
        out = t if out is None else out + t
    return out


ROW_SUB = 16


def _row_tiles(tb):
    assert tb % ROW_SUB == 0
    return [slice(s * ROW_SUB, (s + 1) * ROW_SUB) for s in range(tb // ROW_SUB)]


def _fold8(v):
    out = v[0:8, :]
    for r in range(8, v.shape[0], 8):
        out = out + v[r:r + 8, :]
    return out


def _row_block(n, cap=256, mult=8):
    best = None
    for t in range(mult, min(n, cap) + 1, mult):
        if n % t == 0:
            best = t
    return best if best is not None else n


def _vec(v):
    return v.reshape(1, -1).astype(F32)


MOD_ROWS = 16
MOD_TN = 512


def _mods_fwd(c16, w_bf, b):
    r, d = c16.shape
    n = w_bf.shape[1]

    def body(c_ref, w_ref, b_ref, o_ref):
        s = _silu(c_ref[...]).astype(BF16)
        o_ref[...] = _dot(s, w_ref[...]) + b_ref[...]

    return pl.pallas_call(
        body, name="mods_fwd", grid=(n // MOD_TN,),
        in_specs=[pl.BlockSpec((r, d), lambda j: (0, 0)),
                  pl.BlockSpec((d, MOD_TN), lambda j: (0, j)),
                  pl.BlockSpec((1, MOD_TN), lambda j: (0, j))],
        out_specs=pl.BlockSpec((r, MOD_TN), lambda j: (0, j)),
        out_shape=jax.ShapeDtypeStruct((r, n), F32),
        compiler_params=_cp(("arbitrary",)),
    )(c16, w_bf, b)


def _mods_bwd_w(ct16, dm16):
    d = ct16.shape[0]
    n = dm16.shape[1]

    def body(ct_ref, dm_ref, dw_ref):
        s = _silu(ct_ref[...])
        dm = dm_ref[...]
        acc = s[:, 0:1] * dm[0:1, :]
        for r in range(1, 9):
            acc = acc + s[:, r:r + 1] * dm[r:r + 1, :]
        dw_ref[...] = acc

    return pl.pallas_call(
        body, name="mods_bwd_w", grid=(n // MOD_TN,),
        in_specs=[pl.BlockSpec((d, MOD_ROWS), lambda j: (0, 0)),
                  pl.BlockSpec((MOD_ROWS, MOD_TN), lambda j: (0, j))],
        out_specs=pl.BlockSpec((d, MOD_TN), lambda j: (0, j)),
        out_shape=jax.ShapeDtypeStruct((d, n), F32),
        compiler_params=_cp(("arbitrary",)),
    )(ct16, dm16)


def _mods_bwd_c(dm16, w_bf, c16):
    d = c16.shape[1]
    n = dm16.shape[1]
    nk = n // MOD_TN

    def body(dm_ref, w_ref, c_ref, o_ref):
        k = pl.program_id(0)

        @pl.when(k == 0)
        def _():
            o_ref[...] = jnp.zeros_like(o_ref)

        o_ref[...] += _dot_nt(dm_ref[...].astype(BF16), w_ref[...])

        @pl.when(k == nk - 1)
        def _():
            o_ref[...] = o_ref[...] * (0.5 * _dsilu(c_ref[...]))

    return pl.pallas_call(
        body, name="mods_bwd_c", grid=(nk,),
        in_specs=[pl.BlockSpec((MOD_ROWS, MOD_TN), lambda k: (0, k)),
                  pl.BlockSpec((d, MOD_TN), lambda k: (0, k)),
                  pl.BlockSpec((MOD_ROWS, d), lambda k: (0, 0))],
        out_specs=pl.BlockSpec((MOD_ROWS, d), lambda k: (0, 0)),
        out_shape=jax.ShapeDtypeStruct((MOD_ROWS, d), F32),
        compiler_params=_cp(("arbitrary",)),
    )(dm16, w_bf, c16)


def _mods_bwd_b(dm16):
    n = dm16.shape[1]

    def body(dm_ref, o_ref):
        dm = dm_ref[...]
        acc = dm[0:1, :]
        for r in range(1, 9):
            acc = acc + dm[r:r + 1, :]
        o_ref[...] = jnp.broadcast_to(acc, (8, MOD_TN))

    return pl.pallas_call(
        body, name="mods_bwd_b", grid=(n // MOD_TN,),
        in_specs=[pl.BlockSpec((MOD_ROWS, MOD_TN), lambda j: (0, j))],
        out_specs=pl.BlockSpec((8, MOD_TN), lambda j: (0, j)),
        out_shape=jax.ShapeDtypeStruct((8, n), F32),
        compiler_params=_cp(("arbitrary",)),
    )(dm16)


def _ln_stats(x):
    mu = jnp.mean(x, axis=-1, keepdims=True)
    xc = x - mu
    var = jnp.mean(xc * xc, axis=-1, keepdims=True)
    rstd = lax.rsqrt(var + LN_EPS)
    return xc * rstd, rstd


def _ln_bwd(dxhat, xhat, rstd):
    m1 = jnp.mean(dxhat, axis=-1, keepdims=True)
    m2 = jnp.mean(dxhat * xhat, axis=-1, keepdims=True)
    return rstd * (dxhat - m1 - xhat * m2)


def _row_spec(tb, d):
    return pl.BlockSpec((tb, d), lambda i: (i, 0))


def _par_spec(d):
    return pl.BlockSpec((1, d), lambda i: (0, 0))


def _acc_spec(d):
    return pl.BlockSpec((8, d), lambda i: (0, 0))


def _ln_mod(x, g, b, sh, sc, tb, name):
    n, d = x.shape

    def body(x_ref, g_ref, b_ref, sh_ref, sc_ref, x0_ref, h_ref):
        g, b, sh, sc1 = g_ref[...], b_ref[...], sh_ref[...], 1.0 + sc_ref[...]
        for r in _row_tiles(min(tb, n)):
            xhat, _ = _ln_stats(x_ref[r, :])
            x0 = xhat * g + b
            x0_ref[r, :] = x0
            h_ref[r, :] = (x0 * sc1 + sh).astype(BF16)

    return pl.pallas_call(
        body, name=name, grid=(n // tb,),
        in_specs=[_row_spec(tb, d)] + [_par_spec(d)] * 4,
        out_specs=[_row_spec(tb, d), _row_spec(tb, d)],
        out_shape=[jax.ShapeDtypeStruct((n, d), F32), jax.ShapeDtypeStruct((n, d), BF16)],
        compiler_params=_cp(("parallel",)),
    )(x, g, b, sh, sc)


def _res_ln(xres, mix, gate, g, b, sh, sc, tb):
    n, d = xres.shape

    def body(xr_ref, mix_ref, gate_ref, g_ref, b_ref, sh_ref, sc_ref, x1_ref, h_ref):
        gate_v, g, b, sh, sc1 = gate_ref[...], g_ref[...], b_ref[...], sh_ref[...], 1.0 + sc_ref[...]
        for r in _row_tiles(tb):
            xhat, _ = _ln_stats(ALPHA * xr_ref[r, :] + gate_v * mix_ref[r, :].astype(F32))
            x1 = xhat * g + b
            x1_ref[r, :] = x1
            h_ref[r, :] = (x1 * sc1 + sh).astype(BF16)

    return pl.pallas_call(
        body, name="res_ln1", grid=(n // tb,),
        in_specs=[_row_spec(tb, d)] * 2 + [_par_spec(d)] * 5,
        out_specs=[_row_spec(tb, d), _row_spec(tb, d)],
        out_shape=[jax.ShapeDtypeStruct((n, d), F32), jax.ShapeDtypeStruct((n, d), BF16)],
        compiler_params=_cp(("parallel",)),
    )(xres, mix, gate, g, b, sh, sc)


def _final_ln_loss(x1, ffn, gate, g, b, target, tb):
    n, d = x1.shape

    def body(x1_ref, ffn_ref, gate_ref, g_ref, b_ref, t_ref, dffn_ref, dr_ref, acc_ref):
        i = pl.program_id(0)

        @pl.when(i == 0)
        def _():
            acc_ref[...] = jnp.zeros_like(acc_ref)

        gate_v, g, b = gate_ref[...], g_ref[...], b_ref[...]
        parts = [jnp.zeros((8, d), F32)] * 4
        for r in _row_tiles(tb):
            ffn = ffn_ref[r, :].astype(F32)
            xhat, rstd = _ln_stats(ALPHA * x1_ref[r, :] + gate_v * ffn)
            err = xhat * g + b - t_ref[r, :]
            dx2 = err * (1.0 / d)
            dr = _ln_bwd(dx2 * g, xhat, rstd)
            dr_ref[r, :] = dr
            dffn_ref[r, :] = (gate_v * dr).astype(BF16)
            terms = (dx2 * xhat, dx2, dr * ffn, err * err)
            parts = [p + _fold8(t) for p, t in zip(parts, terms)]
        for j, p in enumerate(parts):
            acc_ref[j:j + 1, :] += jnp.sum(p, axis=0, keepdims=True)

    return pl.pallas_call(
        body, name="final_ln_loss", grid=(n // tb,),
        in_specs=[_row_spec(tb, d)] * 2 + [_par_spec(d)] * 3 + [_row_spec(tb, d)],
        out_specs=[_row_spec(tb, d), _row_spec(tb, d), _acc_spec(d)],
        out_shape=[jax.ShapeDtypeStruct((n, d), BF16), jax.ShapeDtypeStruct((n, d), F32),
                   jax.ShapeDtypeStruct((8, d), F32)],
        compiler_params=_cp(("arbitrary",)),
    )(x1, ffn, gate, g, b, target)


def _bwd_ln1(dr2, dh2, x1, x0, mix, gate, sc2, g, tb):
    n, d = x1.shape

    def body(dr2_ref, dh2_ref, x1_ref, x0_ref, mix_ref, gate_ref, sc_ref, g_ref, dmix_ref, dr1_ref, acc_ref):
        i = pl.program_id(0)

        @pl.when(i == 0)
        def _():
            acc_ref[...] = jnp.zeros_like(acc_ref)

        gate_v, g, sc1 = gate_ref[...], g_ref[...], 1.0 + sc_ref[...]
        parts = [jnp.zeros((8, d), F32)] * 5
        for r in _row_tiles(tb):
            dh2 = dh2_ref[r, :].astype(F32)
            mix = mix_ref[r, :].astype(F32)
            dx1 = ALPHA * dr2_ref[r, :] + dh2 * sc1
            xhat, rstd = _ln_stats(ALPHA * x0_ref[r, :] + gate_v * mix)
            dr1 = _ln_bwd(dx1 * g, xhat, rstd)
            dr1_ref[r, :] = dr1
            dmix_ref[r, :] = (gate_v * dr1).astype(BF16)
            terms = (dh2 * x1_ref[r, :], dh2, dx1 * xhat, dx1, dr1 * mix)
            parts = [p + _fold8(t) for p, t in zip(parts, terms)]
        for j, p in enumerate(parts):
            acc_ref[j:j + 1, :] += jnp.sum(p, axis=0, keepdims=True)

    return pl.pallas_call(
        body, name="bwd_ln1", grid=(n // tb,),
        in_specs=[_row_spec(tb, d)] * 5 + [_par_spec(d)] * 3,
        out_specs=[_row_spec(tb, d), _row_spec(tb, d), _acc_spec(d)],
        out_shape=[jax.ShapeDtypeStruct((n, d), BF16), jax.ShapeDtypeStruct((n, d), F32),
                   jax.ShapeDtypeStruct((8, d), F32)],
        compiler_params=_cp(("arbitrary",)),
    )(dr2, dh2, x1, x0, mix, gate, sc2, g)


def _bwd_ln0(dres, dh, x, g, b, sc, tb, name):
    n, d = x.shape
    has_res = dres is not None

    def body(*refs):
        if has_res:
            dres_ref, dh_ref, x_ref, g_ref, b_ref, sc_ref, dx_ref, acc_ref = refs
        else:
            dh_ref, x_ref, g_ref, b_ref, sc_ref, dx_ref, acc_ref = refs
        i = pl.program_id(0)

        @pl.when(i == 0)
        def _():
            acc_ref[...] = jnp.zeros_like(acc_ref)

        g, b, sc1 = g_ref[...], b_ref[...], 1.0 + sc_ref[...]
        parts = [jnp.zeros((8, d), F32)] * 4
        for r in _row_tiles(tb):
            dh = dh_ref[r, :].astype(F32)
            xhat, rstd = _ln_stats(x_ref[r, :])
            x0 = xhat * g + b
            dx0 = dh * sc1
            if has_res:
                dx0 = dx0 + ALPHA * dres_ref[r, :]
            dx_ref[r, :] = _ln_bwd(dx0 * g, xhat, rstd)
            terms = (dh * x0, dh, dx0 * xhat, dx0)
            parts = [p + _fold8(t) for p, t in zip(parts, terms)]
        for j, p in enumerate(parts):
            acc_ref[j:j + 1, :] += jnp.sum(p, axis=0, keepdims=True)

    ins = ([dres] if has_res else []) + [dh, x, g, b, sc]
    return pl.pallas_call(
        body, name=name, grid=(n // tb,),
        in_specs=[_row_spec(tb, d)] * (3 if has_res else 2) + [_par_spec(d)] * 3,
        out_specs=[_row_spec(tb, d), _acc_spec(d)],
        out_shape=[jax.ShapeDtypeStruct((n, d), F32), jax.ShapeDtypeStruct((8, d), F32)],
        compiler_params=_cp(("arbitrary",)),
    )(*ins)


def _matmul_nn(pairs, out_dtype, tm, tn, name):
    m = pairs[0][0].shape[0]
    n = pairs[0][1].shape[1]
    tm = min(tm, m)
    tn = min(tn, n)
    npair = len(pairs)

    def body(*refs):
        o_ref = refs[-1]
        acc = None
        for p in range(npair):
            t = _dot(refs[2 * p][...].astype(BF16), refs[2 * p + 1][...])
            acc = t if acc is None else acc + t
        o_ref[...] = acc.astype(out_dtype)

    in_specs, args = [], []
    for a, b in pairs:
        k = a.shape[1]
        in_specs += [pl.BlockSpec((tm, k), lambda i, j: (i, 0)), pl.BlockSpec((k, tn), lambda i, j: (0, j))]
        args += [a, b]
    return pl.pallas_call(
        body, name=name, grid=(m // tm, n // tn),
        in_specs=in_specs,
        out_specs=pl.BlockSpec((tm, tn), lambda i, j: (i, j)),
        out_shape=jax.ShapeDtypeStruct((m, n), out_dtype),
        compiler_params=_cp(("parallel", "arbitrary")),
    )(*args)


def _matmul_tn(a, g, tm, tn, name):
    m, k = a.shape
    n = g.shape[1]
    tm = min(tm, m)
    tn = min(tn, n)

    def body(a_ref, g_ref, o_ref):
        i = pl.program_id(1)

        @pl.when(i == 0)
        def _():
            o_ref[...] = jnp.zeros_like(o_ref)

        o_ref[...] += _dot_tn(a_ref[...].astype(BF16), g_ref[...].astype(BF16))

    return pl.pallas_call(
        body, name=name, grid=(n // tn, m // tm),
        in_specs=[pl.BlockSpec((tm, k), lambda j, i: (i, 0)), pl.BlockSpec((tm, tn), lambda j, i: (i, j))],
        out_specs=pl.BlockSpec((k, tn), lambda j, i: (0, j)),
        out_shape=jax.ShapeDtypeStruct((k, n), F32),
        compiler_params=_cp(("parallel", "arbitrary")),
    )(a, g)


def _matmul_nn_multi(a, bs, out_dtypes, tm, name):
    m, k = a.shape
    tm = min(tm, m)
    nb = len(bs)

    def body(a_ref, *refs):
        av = a_ref[...].astype(BF16)
        for j in range(nb):
            refs[nb + j][...] = _dot(av, refs[j][...]).astype(out_dtypes[j])

    return pl.pallas_call(
        body, name=name, grid=(m // tm,),
        in_specs=[pl.BlockSpec((tm, k), lambda i: (i, 0))] + [pl.BlockSpec(b.shape, lambda i: (0, 0)) for b in bs],
        out_specs=[pl.BlockSpec((tm, b.shape[1]), lambda i: (i, 0)) for b in bs],
        out_shape=[jax.ShapeDtypeStruct((m, b.shape[1]), dt) for b, dt in zip(bs, out_dtypes)],
        compiler_params=_cp(("parallel",)),
    )(a, *bs)


def _matmul_tn_multi(a, gs, tm, tns, name):
    m, k = a.shape
    tm = min(tm, m)
    ng = len(gs)
    nj = gs[0].shape[1] // tns[0]
    assert all(g.shape[1] // t == nj and g.shape[1] % t == 0 for g, t in zip(gs, tns))

    def body(a_ref, *refs):
        i = pl.program_id(1)

        @pl.when(i == 0)
        def _():
            for j in range(ng):
                refs[ng + j][...] = jnp.zeros_like(refs[ng + j])

        av = a_ref[...].astype(BF16)
        for j in range(ng):
            refs[ng + j][...] += _dot_tn(av, refs[j][...].astype(BF16))

    return pl.pallas_call(
        body, name=name, grid=(nj, m // tm),
        in_specs=[pl.BlockSpec((tm, k), lambda j, i: (i, 0))] +
                 [pl.BlockSpec((tm, t), lambda j, i: (i, j)) for t in tns],
        out_specs=[pl.BlockSpec((k, t), lambda j, i: (0, j)) for t in tns],
        out_shape=[jax.ShapeDtypeStruct((k, g.shape[1]), F32) for g in gs],
        compiler_params=_cp(("parallel", "arbitrary")),
    )(a, *gs)


def _swiglu_fwd(h, wg, wu, tm, tn):
    m, k = h.shape
    n = wg.shape[1]
    tm = min(tm, m)

    def body(h_ref, wg_ref, wu_ref, gate_ref, up_ref, hmid_ref):
        hv = h_ref[...]
        gate = _dot(hv, wg_ref[...])
        up = _dot(hv, wu_ref[...])
        gate_ref[...] = gate.astype(BF16)
        up_ref[...] = up.astype(BF16)
        hmid_ref[...] = (_silu(gate) * up).astype(BF16)

    blk = pl.BlockSpec((tm, tn), lambda i, j: (i, j))
    wspec = pl.BlockSpec((k, tn), lambda i, j: (0, j))
    return pl.pallas_call(
        body, name="swiglu_fwd", grid=(m // tm, n // tn),
        in_specs=[pl.BlockSpec((tm, k), lambda i, j: (i, 0)), wspec, wspec],
        out_specs=[blk, blk, blk],
        out_shape=[jax.ShapeDtypeStruct((m, n), BF16), jax.ShapeDtypeStruct((m, n), BF16),
                   jax.ShapeDtypeStruct((m, n), BF16)],
        compiler_params=_cp(("parallel", "arbitrary")),
    )(h, wg, wu)


def _swiglu_bwd(dffn, wdt, gate, up, tm, tn):
    m, k = dffn.shape
    n = wdt.shape[1]
    tm = min(tm, m)

    def body(d_ref, w_ref, gate_ref, up_ref, dg_ref, du_ref):
        dh = _dot(d_ref[...], w_ref[...])
        gate = gate_ref[...].astype(F32)
        sg = _sigmoid(gate)
        dg_ref[...] = (dh * up_ref[...].astype(F32) * (sg * (1.0 + gate * (1.0 - sg)))).astype(BF16)
        du_ref[...] = (dh * (gate * sg)).astype(BF16)

    blk = pl.BlockSpec((tm, tn), lambda i, j: (i, j))
    return pl.pallas_call(
        body, name="swiglu_bwd", grid=(m // tm, n // tn),
        in_specs=[pl.BlockSpec((tm, k), lambda i, j: (i, 0)), pl.BlockSpec((k, tn), lambda i, j: (0, j)), blk, blk],
        out_specs=[blk, blk],
        out_shape=[jax.ShapeDtypeStruct((m, n), BF16), jax.ShapeDtypeStruct((m, n), BF16)],
        compiler_params=_cp(("parallel", "arbitrary")),
    )(dffn, wdt, gate, up)


def _halo_specs(tb, width, nrows):
    r8 = tb // 8
    last = nrows // 8 - 1
    prev = pl.BlockSpec((8, width), lambda i: (jnp.maximum(i * r8 - 1, 0), 0))
    nxt = pl.BlockSpec((8, width), lambda i: (jnp.minimum((i + 1) * r8, last), 0))
    return prev, nxt


CONV_SUB = 32


def _halo_scratch():
    return [pltpu.VMEM((CONV_SUB + 16, D_XBC), F32), pltpu.VMEM((CONV_SUB + 16, D_XBC), F32)]


def _shifted_rows(prev_ref, cur_ref, next_ref, top, bot, tb, i, nb):
    sub = CONV_SUB
    nsub = tb // sub
    assert nsub >= 2
    top[0:8, :] = prev_ref[...] * (i > 0).astype(F32)
    top[8:sub + 16, :] = cur_ref[0:sub + 8, :]
    bot[0:sub + 8, :] = cur_ref[tb - sub - 8:tb, :]
    bot[sub + 8:sub + 16, :] = next_ref[...] * (i < nb - 1).astype(F32)

    def rows(s, o):
        if s == 0:
            return top[8 + o:8 + o + sub, :]
        if s == nsub - 1:
            return bot[8 + o:8 + o + sub, :]
        return cur_ref[s * sub + o:(s + 1) * sub + o, :]

    return rows


def _conv_fwd(xd, w8, b, tb, name):
    n = xd.shape[0]
    tb = min(tb, n)
    nb = n // tb
    prev, nxt = _halo_specs(tb, D_XBC, n)

    def body(p_ref, c_ref, n_ref, w_ref, b_ref, u_ref, a_ref, top, bot):
        i = pl.program_id(0)
        rows = _shifted_rows(p_ref, c_ref, n_ref, top, bot, tb, i, nb)
        w = [w_ref[k:k + 1, :] for k in range(D_CONV)]
        bias = jnp.broadcast_to(b_ref[...], (CONV_SUB, D_XBC))
        for s in range(tb // CONV_SUB):
            acc = bias
            for k in range(D_CONV):
                acc = acc + w[k] * rows(s, k - 2)
            u_ref[s * CONV_SUB:(s + 1) * CONV_SUB, :] = acc.astype(BF16)
            a_ref[s * CONV_SUB:(s + 1) * CONV_SUB, :] = _silu(acc).astype(BF16)

    return pl.pallas_call(
        body, name=name, grid=(nb,),
        in_specs=[prev, pl.BlockSpec((tb, D_XBC), lambda i: (i, 0)), nxt,
                  pl.BlockSpec((8, D_XBC), lambda i: (0, 0)), _par_spec(D_XBC)],
        out_specs=[_row_spec(tb, D_XBC), _row_spec(tb, D_XBC)],
        out_shape=[jax.ShapeDtypeStruct((n, D_XBC), BF16), jax.ShapeDtypeStruct((n, D_XBC), BF16)],
        scratch_shapes=_halo_scratch(),
        compiler_params=_cp(("parallel",)),
    )(xd, xd, xd, w8, b)


def _conv_bwd_a(dxs, dy, dskip_e, dbc, u, tb, name):
    n = u.shape[0]
    tb = min(tb, n)

    def body(dxs_ref, dy_ref, sk_ref, dbc_ref, u_ref, du_ref, acc_ref):
        i = pl.program_id(0)

        @pl.when(i == 0)
        def _():
            acc_ref[...] = jnp.zeros_like(acc_ref)

        sk = sk_ref[...]
        part = jnp.zeros((8, D_XBC), F32)
        for r in _row_tiles(tb):
            gx = dxs_ref[0, r, :].astype(F32) + dxs_ref[1, r, :].astype(F32) + dy_ref[r, :].astype(F32) * sk
            gbc = dbc_ref[0, r, :] + dbc_ref[1, r, :]
            du = jnp.concatenate([gx, gbc], axis=1) * _dsilu(u_ref[r, :].astype(F32))
            du_ref[r, :] = du
            part = part + _fold8(du)
        acc_ref[0:1, :] += jnp.sum(part, axis=0, keepdims=True)

    return pl.pallas_call(
        body, name=name, grid=(n // tb,),
        in_specs=[pl.BlockSpec((2, tb, D_MODEL), lambda i: (0, i, 0)), _row_spec(tb, D_MODEL), _par_spec(D_MODEL),
                  pl.BlockSpec((2, tb, 2 * D_STATE), lambda i: (0, i, 0)), _row_spec(tb, D_XBC)],
        out_specs=[_row_spec(tb, D_XBC), _acc_spec(D_XBC)],
        out_shape=[jax.ShapeDtypeStruct((n, D_XBC), F32), jax.ShapeDtypeStruct((8, D_XBC), F32)],
        compiler_params=_cp(("arbitrary",)),
    )(dxs, dy, dskip_e, dbc, u)


def _conv_bwd_b(du, xd, ddt, w8, tb, name):
    n = du.shape[0]
    tb = min(tb, n)
    nb = n // tb
    prev, nxt = _halo_specs(tb, D_XBC, n)

    def body(dp_ref, dc_ref, dn_ref, xp_ref, xc_ref, xn_ref, ddt_ref, w_ref, dxd_ref, acc_ref, dtop, dbot, xtop, xbot):
        i = pl.program_id(0)

        @pl.when(i == 0)
        def _():
            acc_ref[...] = jnp.zeros_like(acc_ref)

        sub = CONV_SUB
        nsub = tb // sub
        du_rows = _shifted_rows(dp_ref, dc_ref, dn_ref, dtop, dbot, tb, i, nb)
        x_rows = _shifted_rows(xp_ref, xc_ref, xn_ref, xtop, xbot, tb, i, nb)
        w = [w_ref[k:k + 1, :] for k in range(D_CONV)]
        for s in range(nsub):
            acc = w[0] * du_rows(s, 2)
            for k in range(1, D_CONV):
                acc = acc + w[k] * du_rows(s, 2 - k)
            dxd_ref[s * sub:(s + 1) * sub, 0:D_XBC] = acc.astype(BF16)
        for k in range(D_CONV):
            part = jnp.zeros((8, D_XBC), F32)
            for s in range(nsub):
                prod = dc_ref[s * sub:(s + 1) * sub, :] * x_rows(s, k - 2)
                for r in range(0, sub, 8):
                    part = part + prod[r:r + 8, :]
            acc_ref[k:k + 1, :] += jnp.sum(part, axis=0, keepdims=True)
        ddt = ddt_ref[0] + pltpu.roll(ddt_ref[1], SSD_HEADS, 1)
        dxd_ref[:, D_XBC:D_XD] = ddt.astype(BF16)

    cur = pl.BlockSpec((tb, D_XBC), lambda i: (i, 0))
    return pl.pallas_call(
        body, name=name, grid=(nb,),
        in_specs=[prev, cur, nxt, prev, cur, nxt,
                  pl.BlockSpec((2, tb, 128), lambda i: (0, i, 0)), pl.BlockSpec((8, D_XBC), lambda i: (0, 0))],
        out_specs=[_row_spec(tb, D_XD), _acc_spec(D_XBC)],
        out_shape=[jax.ShapeDtypeStruct((n, D_XD), BF16), jax.ShapeDtypeStruct((8, D_XBC), F32)],
        scratch_shapes=_halo_scratch() + _halo_scratch(),
        compiler_params=_cp(("arbitrary",)),
    )(du, du, du, xd, xd, xd, ddt, w8)


def _ssd_chunk_index(nc, reverse):
    def idx(d, k):
        kk = (nc - 1 - k) if reverse else k
        return kk + d * (nc - 1 - 2 * kk)
    return idx


def _ssd_prologue(d, u_ref, xd_ref, bias_ref, a_ref, r_ref):
    q = CHUNK
    xbc = u_ref[...].astype(F32)
    xs = xbc[:, 0:D_MODEL]
    bm = xbc[:, D_MODEL:D_MODEL + D_STATE]
    cm = xbc[:, D_MODEL + D_STATE:D_XBC]
    row = lax.broadcasted_iota(jnp.int32, (q, q), 0)
    col = lax.broadcasted_iota(jnp.int32, (q, q), 1)
    sgn = 1 - 2 * d
    mask = ((row - col) * sgn) >= 0
    mask_t = ((row - col) * sgn) <= 0
    xdv = xd_ref[...]
    dtraw = jnp.where(d == 0, xdv, pltpu.roll(xdv, 128 - SSD_HEADS, 1)) + bias_ref[...]
    head_lane = col < SSD_HEADS
    dt = jnp.where(head_lane, _softplus(dtraw), 0.0)
    a = a_ref[...]
    tri = jnp.where(mask, 1.0, 0.0).astype(BF16)
    acum = _dot_sel_l(tri, dt * a)
    rexp = r_ref[...]
    alast = jnp.where(d == 0, acum[q - 1:q, :], acum[0:1, :])
    e16 = jnp.exp(acum)
    dend16 = jnp.exp(alast - acum)
    wend16 = dend16 * dt
    e = _dot_sel_r(e16, rexp, n=1)
    wend_e = _dot_sel_r(wend16, rexp, n=1)
    elast_e = _dot_sel_r(jnp.broadcast_to(jnp.exp(alast), (8, 128)), rexp, n=2)[0:1, :]
    g = _dot_nt(cm.astype(BF16), bm.astype(BF16))
    return dict(xs=xs, bm=bm, cm=cm, mask=mask, mask_t=mask_t, dtraw=dtraw, head_lane=head_lane, dt=dt, a=a,
                acum=acum, acum_t=acum.T, dt_t=dt.T, e16=e16, dend16=dend16, wend16=wend16, e=e, wend_e=wend_e,
                elast_e=elast_e, g=g, col=col, row=row)


def _ssd_head_mats(p, h):
    seg = p["acum"][:, h:h + 1] - p["acum_t"][h:h + 1, :]
    lm = jnp.exp(jnp.where(p["mask"], seg, -jnp.inf))
    gl = p["g"] * lm
    s = gl * p["dt_t"][h:h + 1, :]
    return lm, gl, s


def _ssd_fwd(u, xd, bias2, a2, rexp, h0, name, gather=()):
    n = u.shape[0]
    nc = n // CHUNK
    q = CHUNK
    cidx = _ssd_chunk_index(nc, reverse=False)
    ng = len(gather)

    def body(u_ref, xd_ref, bias_ref, a_ref, r_ref, h0_ref, *rest):
        g_ins, (y_ref, hp_ref, hf_ref), rest = rest[:ng], rest[ng:ng + 3], rest[ng + 3:]
        g_outs, st, sems = rest[:ng], rest[ng], rest[ng + 1:]
        d = pl.program_id(0)
        k = pl.program_id(1)
        if ng:
            g_start, g_finish = _gather_steps(g_ins, g_outs, *sems)
            pl.when((d == 0) & (k == 0))(g_start)

        @pl.when(k == 0)
        def _():
            st[...] = h0_ref[...]

        p = _ssd_prologue(d, u_ref, xd_ref, bias_ref, a_ref, r_ref)
        stv = st[...]
        st_bf = stv.astype(BF16)
        hp_ref[...] = st_bf
        xs = p["xs"]
        lane128 = p["col"]
        y_off = _dot(p["cm"].astype(BF16), st_bf) * p["e"]
        for pb in range(SSD_HEADS // 2):
            _, _, s0 = _ssd_head_mats(p, 2 * pb)
            _, _, s1 = _ssd_head_mats(p, 2 * pb + 1)
            xp = xs[:, pb * 128:(pb + 1) * 128]
            rhs = jnp.concatenate([jnp.where(lane128 < HEAD_DIM, xp, 0.0), jnp.where(lane128 >= HEAD_DIM, xp, 0.0)],
                                  axis=0).astype(BF16)
            lhs = jnp.concatenate([s0, s1], axis=1).astype(BF16)
            y_ref[:, pb * 128:(pb + 1) * 128] = (_dot(lhs, rhs) + y_off[:, pb * 128:(pb + 1) * 128]).astype(BF16)
        xw = (xs * p["wend_e"]).astype(BF16)
        new = stv * p["elast_e"] + _dot(p["bm"].T.astype(BF16), xw)
        st[...] = new
        hf_ref[...] = new
        if ng:
            pl.when((d == 1) & (k == nc - 1))(g_finish)

    nsem = _GATHER_SEMS * ng
    return pl.pallas_call(
        body, name=name, grid=(2, nc),
        in_specs=[pl.BlockSpec((q, D_XBC), lambda d, k: (cidx(d, k), 0)),
                  pl.BlockSpec((q, 128), lambda d, k: (cidx(d, k), D_XBC // 128)),
                  pl.BlockSpec((None, 1, 128), lambda d, k: (d, 0, 0)),
                  pl.BlockSpec((None, 1, 128), lambda d, k: (d, 0, 0)),
                  pl.BlockSpec((128, D_MODEL), lambda d, k: (0, 0)),
                  pl.BlockSpec((None, D_STATE, D_MODEL), lambda d, k: (d, 0, 0))] + [_ANY] * ng,
        out_specs=[pl.BlockSpec((None, q, D_MODEL), lambda d, k: (d, cidx(d, k), 0)),
                   pl.BlockSpec((None, None, D_STATE, D_MODEL), lambda d, k: (d, cidx(d, k), 0, 0)),
                   pl.BlockSpec((None, D_STATE, D_MODEL), lambda d, k: (d, 0, 0))] + [_ANY] * ng,
        out_shape=[jax.ShapeDtypeStruct((2, n, D_MODEL), BF16),
                   jax.ShapeDtypeStruct((2, nc, D_STATE, D_MODEL), BF16),
                   jax.ShapeDtypeStruct((2, D_STATE, D_MODEL), F32)] +
                  [jax.ShapeDtypeStruct((4,) + t.shape, t.dtype) for t in gather],
        scratch_shapes=[pltpu.VMEM((D_STATE, D_MODEL), F32)] +
                       ([pltpu.SemaphoreType.DMA((nsem,)), pltpu.SemaphoreType.DMA((nsem,))] if ng else []),
        compiler_params=_cp(("arbitrary", "arbitrary")),
    )(u, xd, bias2, a2, rexp, h0, *gather)


def _ssd_bwd(u, xd, bias2, a2, rexp, rexp_t, dy, hprev, lam0, name, exchange=()):
    n = u.shape[0]
    nc = n // CHUNK
    q = CHUNK
    cidx = _ssd_chunk_index(nc, reverse=True)
    ne = len(exchange)

    def body(u_ref, xd_ref, bias_ref, a_ref, r_ref, rt_ref, dy_ref, hp_ref, lam0_ref, *rest):
        e_ins, (dxs_ref, dbc_ref, ddt_ref, acc_ref, lamo_ref), rest = rest[:ne], rest[ne:ne + 5], rest[ne + 5:]
        e_outs, lam, sems = rest[:ne], rest[ne], rest[ne + 1:]
        d = pl.program_id(0)
        k = pl.program_id(1)
        if ne:
            e_start, e_finish = _exchange_steps(e_ins, e_outs, *sems)
            pl.when((d == 0) & (k == 0))(e_start)

        @pl.when(k == 0)
        def _():
            lam[...] = lam0_ref[...]
            acc_ref[...] = jnp.zeros_like(acc_ref)

        rexp_t = rt_ref[...]

        def hsum(t):
            return _dot_sel_r(t, rexp_t, n=1)

        p = _ssd_prologue(d, u_ref, xd_ref, bias_ref, a_ref, r_ref)
        xs, bm, cm = p["xs"], p["bm"], p["cm"]
        bm_bf, cm_bf = bm.astype(BF16), cm.astype(BF16)
        lamn = lam[...]
        lamn_bf = lamn.astype(BF16)
        stp = hp_ref[...]
        dyv = dy_ref[...].astype(F32)
        lane128 = p["col"]

        wend_e = p["wend_e"]
        cs = _dot(cm_bf, stp)
        dye_bf = (dyv * p["e"]).astype(BF16)
        dc_off = _dot_nt(dye_bf, stp)
        v = _dot(bm_bf, lamn_bf)
        xw_bf = (xs * wend_e).astype(BF16)
        db_off = _dot_nt(xw_bf, lamn_bf)
        elast_e = p["elast_e"]
        dlast_e = jnp.sum(stp.astype(F32) * lamn, axis=0, keepdims=True) * elast_e
        lam_new = lamn * elast_e + _dot(cm.T.astype(BF16), dye_bf)
        lam[...] = lam_new
        lamo_ref[...] = lam_new

        hs_vx = hsum(v * xs)
        om = p["wend16"] * hs_vx
        x1 = p["e16"] * hsum(dyv * cs) - om
        x2 = p["dend16"] * hs_vx
        x3 = jnp.sum(om, axis=0, keepdims=True) + _dot_sel_r(jnp.broadcast_to(dlast_e, (8, D_MODEL)), rexp_t, n=2)[0:1, :]

        sub16 = lax.broadcasted_iota(jnp.int32, (SSD_HEADS, q), 0)
        rs = jnp.zeros((q, 128), F32)
        cs_m = jnp.zeros((SSD_HEADS, q), F32)
        dt_m = jnp.zeros((SSD_HEADS, q), F32)
        dg = jnp.zeros((q, q), F32)
        for pb in range(SSD_HEADS // 2):
            xp_bf = xs[:, pb * 128:(pb + 1) * 128].astype(BF16)
            dyp = dyv[:, pb * 128:(pb + 1) * 128]
            dxs_pair = None
            for half in range(2):
                h = 2 * pb + half
                sel = (lane128 < HEAD_DIM) if half == 0 else (lane128 >= HEAD_DIM)
                dyh_bf = jnp.where(sel, dyp, 0.0).astype(BF16)
                lm, gl, s = _ssd_head_mats(p, h)
                ds = _dot_nt(dyh_bf, xp_bf)
                t = _dot_tn(s.astype(BF16), dyh_bf)
                dxs_pair = t if dxs_pair is None else dxs_pair + t
                w = ds * s
                rs = rs + jnp.sum(w, axis=1, keepdims=True) * (lane128 == h).astype(F32)
                cs_m = jnp.where(sub16 == h, jnp.sum(w, axis=0, keepdims=True), cs_m)
                dt_m = jnp.where(sub16 == h, jnp.sum(ds * gl, axis=0, keepdims=True), dt_m)
                dg = dg + ds * lm * p["dt_t"][h:h + 1, :]
            sl = slice(pb * 128, (pb + 1) * 128)
            dxs_ref[:, sl] = (dxs_pair + v[:, sl] * wend_e[:, sl]).astype(BF16)

        def to_lanes(m16):
            return jnp.concatenate([m16, jnp.zeros((128 - SSD_HEADS, q), F32)], axis=0).T

        last = jnp.where(d == 0, q - 1, 0)
        dacum = rs - to_lanes(cs_m) + x1 + jnp.where(p["row"] == last, x3[0:1, :], 0.0)
        tri_t = jnp.where(p["mask_t"], 1.0, 0.0).astype(BF16)
        ddta = _dot_sel_l(tri_t, dacum)
        dt = p["dt"]
        a = p["a"]
        ddt = to_lanes(dt_m) + x2 + a * ddta
        ddtraw = jnp.where(p["head_lane"], ddt * _sigmoid(p["dtraw"]), 0.0)
        ddt_ref[...] = ddtraw
        acc_ref[0:1, :] += jnp.sum(ddtraw, axis=0, keepdims=True)
        acc_ref[1:2, :] += jnp.sum(dt * ddta, axis=0, keepdims=True) * a

        dg_bf = dg.astype(BF16)
        dbc_ref[:, 0:D_STATE] = _dot_tn(dg_bf, cm_bf) + db_off
        dbc_ref[:, D_STATE:2 * D_STATE] = _dot(dg_bf, bm_bf) + dc_off
        if ne:
            pl.when((d == 1) & (k == nc - 1))(e_finish)

    cblk = lambda d, k: (cidx(d, k), 0)
    return pl.pallas_call(
        body, name=name, grid=(2, nc),
        in_specs=[pl.BlockSpec((q, D_XBC), cblk),
                  pl.BlockSpec((q, 128), lambda d, k: (cidx(d, k), D_XBC // 128)),
                  pl.BlockSpec((None, 1, 128), lambda d, k: (d, 0, 0)),
                  pl.BlockSpec((None, 1, 128), lambda d, k: (d, 0, 0)),
                  pl.BlockSpec((128, D_MODEL), lambda d, k: (0, 0)),
                  pl.BlockSpec((D_MODEL, 128), lambda d, k: (0, 0)),
                  pl.BlockSpec((q, D_MODEL), cblk),
                  pl.BlockSpec((None, None, D_STATE, D_MODEL), lambda d, k: (d, cidx(d, k), 0, 0)),
                  pl.BlockSpec((None, D_STATE, D_MODEL), lambda d, k: (d, 0, 0))] + [_ANY] * ne,
        out_specs=[pl.BlockSpec((None, q, D_MODEL), lambda d, k: (d, cidx(d, k), 0)),
                   pl.BlockSpec((None, q, 2 * D_STATE), lambda d, k: (d, cidx(d, k), 0)),
                   pl.BlockSpec((None, q, 128), lambda d, k: (d, cidx(d, k), 0)),
                   pl.BlockSpec((None, 8, 128), lambda d, k: (d, 0, 0)),
                   pl.BlockSpec((None, D_STATE, D_MODEL), lambda d, k: (d, 0, 0))] + [_ANY] * ne,
        out_shape=[jax.ShapeDtypeStruct((2, n, D_MODEL), BF16),
                   jax.ShapeDtypeStruct((2, n, 2 * D_STATE), F32),
                   jax.ShapeDtypeStruct((2, n, 128), F32),
                   jax.ShapeDtypeStruct((2, 8, 128), F32),
                   jax.ShapeDtypeStruct((2, D_STATE, D_MODEL), F32)] +
                  [jax.ShapeDtypeStruct(t.shape, t.dtype) for t in exchange],
        scratch_shapes=[pltpu.VMEM((D_STATE, D_MODEL), F32)] +
                       ([pltpu.SemaphoreType.DMA((3 * ne,)), pltpu.SemaphoreType.DMA((3 * ne,)),
                         pltpu.SemaphoreType.DMA((ne,))] if ne else []),
        compiler_params=_cp(("arbitrary", "arbitrary")),
    )(u, xd, bias2, a2, rexp, rexp_t, dy, hprev, lam0, *exchange)


def _merge_fwd(y, u, z, dskip_e, gn, tb):
    n = z.shape[0]

    def body(y_ref, u_ref, z_ref, sk_ref, gn_ref, o_ref):
        sk, gnv = sk_ref[...], gn_ref[...]
        for r in _row_tiles(tb):
            ys = y_ref[0, r, :].astype(F32) + y_ref[1, r, :].astype(F32) + sk * _silu(u_ref[r, :].astype(F32))
            gated = ys * _silu(z_ref[r, :].astype(F32))
            rstd = lax.rsqrt(jnp.mean(gated * gated, axis=-1, keepdims=True) + LN_EPS)
            o_ref[r, :] = (gated * rstd * gnv).astype(BF16)

    return pl.pallas_call(
        body, name="merge_fwd", grid=(n // tb,),
        in_specs=[pl.BlockSpec((2, tb, D_MODEL), lambda i: (0, i, 0)), pl.BlockSpec((tb, D_MODEL), lambda i: (i, 0)),
                  _row_spec(tb, D_MODEL), _par_spec(D_MODEL), _par_spec(D_MODEL)],
        out_specs=_row_spec(tb, D_MODEL),
        out_shape=jax.ShapeDtypeStruct((n, D_MODEL), BF16),
        compiler_params=_cp(("parallel",)),
    )(y, u, z, dskip_e, gn)


def _merge_bwd(dyn, y, u, z, dskip_e, gn, tb):
    n = z.shape[0]

    def body(dyn_ref, y_ref, u_ref, z_ref, sk_ref, gn_ref, dy_ref, dz_ref, acc_ref):
        i = pl.program_id(0)

        @pl.when(i == 0)
        def _():
            acc_ref[...] = jnp.zeros_like(acc_ref)

        sk, gnv = sk_ref[...], gn_ref[...]
        part0 = jnp.zeros((8, D_MODEL), F32)
        part1 = jnp.zeros((8, D_MODEL), F32)
        for s in range(tb // ROW_SUB):
            r = slice(s * ROW_SUB, (s + 1) * ROW_SUB)
            xs = _silu(u_ref[r, :].astype(F32))
            zv = z_ref[r, :].astype(F32)
            sz = _sigmoid(zv)
            ys = y_ref[0, r, :].astype(F32) + y_ref[1, r, :].astype(F32) + sk * xs
            gated = ys * (zv * sz)
            rstd = lax.rsqrt(jnp.mean(gated * gated, axis=-1, keepdims=True) + LN_EPS)
            ghat = gated * rstd
            dyn_v = dyn_ref[r, :].astype(F32)
            t = dyn_v * gnv
            dgated = rstd * (t - ghat * jnp.mean(t * ghat, axis=-1, keepdims=True))
            dys = dgated * (zv * sz)
            dy_ref[r, :] = dys.astype(BF16)
            dz_ref[r, :] = (dgated * ys * (sz * (1.0 + zv * (1.0 - sz)))).astype(BF16)
            part0 = part0 + _fold8(dyn_v * ghat)
            part1 = part1 + _fold8(dys * xs)
        acc_ref[0:1, :] += jnp.sum(part0, axis=0, keepdims=True)
        acc_ref[1:2, :] += jnp.sum(part1, axis=0, keepdims=True)

    return pl.pallas_call(
        body, name="merge_bwd", grid=(n // tb,),
        in_specs=[_row_spec(tb, D_MODEL), pl.BlockSpec((2, tb, D_MODEL), lambda i: (0, i, 0)),
                  pl.BlockSpec((tb, D_MODEL), lambda i: (i, 0)), _row_spec(tb, D_MODEL),
                  _par_spec(D_MODEL), _par_spec(D_MODEL)],
        out_specs=[_row_spec(tb, D_MODEL), _row_spec(tb, D_MODEL), _acc_spec(D_MODEL)],
        out_shape=[jax.ShapeDtypeStruct((n, D_MODEL), BF16), jax.ShapeDtypeStruct((n, D_MODEL), BF16),
                   jax.ShapeDtypeStruct((8, D_MODEL), F32)],
        compiler_params=_cp(("arbitrary",)),
    )(dyn, y, u, z, dskip_e, gn)


def _pool_consts(transpose):
    tb = POOL_TB
    t = jnp.arange(tb)
    s = jnp.arange(3 * tb)
    rl, cl = t // GRID_W, t % GRID_W
    rs_, cs_ = s // GRID_W - tb // GRID_W, s % GRID_W
    s2 = jnp.arange(tb)
    rl2, cl2 = s2 // GRID_W, s2 % GRID_W
    brow, bcol = [], []
    for w in POOL_WINDOWS:
        lo, hi = -(w // 2), w - w // 2
        if transpose:
            lo, hi = -hi + 1, -lo + 1
        dr = rs_[None, :] - rl[:, None]
        before, after = _pool_halo(w, transpose)
        full = ((cs_[None, :] == cl[:, None]) & (dr >= lo) & (dr < hi)).astype(BF16)
        brow.append(full[:, tb - before:2 * tb + after])
        dc = cl2[None, :] - cl[:, None]
        bcol.append(((rl2[None, :] == rl[:, None]) & (dc >= lo) & (dc < hi)).astype(BF16))
    return brow, jnp.stack(bcol)


def _pool_halo(w, transpose):
    lo, hi = -(w // 2), w - w // 2
    if transpose:
        lo, hi = -hi + 1, -lo + 1
    return -lo * GRID_W, (hi - 1) * GRID_W


def _pool_inv(i, g, n):
    assert GRID_W == 64
    t = i * POOL_TB + lax.broadcasted_iota(jnp.int32, (POOL_TB, 1), 0)
    r = lax.shift_right_logical(t, 6)
    col = t & (GRID_W - 1)
    w = POOL_WINDOWS[g]
    lo, hi = -(w // 2), w - w // 2
    cnt_r = jnp.minimum(r + hi, n // GRID_W) - jnp.maximum(r + lo, 0)
    cnt_c = jnp.minimum(col + hi, GRID_W) - jnp.maximum(col + lo, 0)
    return 1.0 / (cnt_r * cnt_c).astype(F32)


def _pool_box(prev_ref, cur_ref, next_ref, brow_refs, bcol_ref, g, i, nb, transpose):
    tb = POOL_TB
    sl = slice(g * POOL_DIM, (g + 1) * POOL_DIM)
    before, after = _pool_halo(POOL_WINDOWS[g], transpose)
    pieces = []
    if before:
        pieces.append((prev_ref[tb - before:tb, sl] * (i > 0).astype(prev_ref.dtype)).astype(BF16))
    pieces.append(cur_ref[:, sl].astype(BF16))
    if after:
        pieces.append((next_ref[0:after, sl] * (i < nb - 1).astype(next_ref.dtype)).astype(BF16))
    r = _dot(brow_refs[g][...], jnp.concatenate(pieces, axis=0))
    return _dot(bcol_ref[g], r.astype(BF16))


def _pool_halo_specs(n, d):
    tb = POOL_TB
    nb = n // tb
    prev = pl.BlockSpec((tb, d), lambda i: (jnp.maximum(i - 1, 0), 0))
    cur = pl.BlockSpec((tb, d), lambda i: (i, 0))
    nxt = pl.BlockSpec((tb, d), lambda i: (jnp.minimum(i + 1, nb - 1), 0))
    return prev, cur, nxt


def _pool_const_specs(brow):
    tb = POOL_TB
    return [pl.BlockSpec(b.shape, lambda i: (0, 0)) for b in brow] + [pl.BlockSpec((N_POOL, tb, tb), lambda i: (0, 0, 0))]


def _pool_fwd(up, consts, pw_bf, pscale):
    n = up.shape[0]
    tb = POOL_TB
    nb = n // tb
    brow, bcol = consts
    prev, cur, nxt = _pool_halo_specs(n, D_MODEL)

    def body(p_ref, c_ref, n_ref, *rest):
        brow_refs, (bcol_ref, pw_ref, sc_ref, o_ref, d_ref) = rest[:N_POOL], rest[N_POOL:]
        i = pl.program_id(0)
        for g in range(N_POOL):
            sl = slice(g * POOL_DIM, (g + 1) * POOL_DIM)
            box = _pool_box(p_ref, c_ref, n_ref, brow_refs, bcol_ref, g, i, nb, False)
            dd = (box * _pool_inv(i, g, n) - c_ref[:, sl].astype(F32)).astype(BF16)
            d_ref[:, sl] = dd
            o_ref[:, sl] = (_dot(dd, pw_ref[g]) * sc_ref[:, sl]).astype(BF16)

    return pl.pallas_call(
        body, name="pool_fwd", grid=(nb,),
        in_specs=[prev, cur, nxt] + _pool_const_specs(brow) +
                 [pl.BlockSpec((N_POOL, POOL_DIM, POOL_DIM), lambda i: (0, 0, 0)), _par_spec(D_MODEL)],
        out_specs=[_row_spec(tb, D_MODEL), _row_spec(tb, D_MODEL)],
        out_shape=[jax.ShapeDtypeStruct((n, D_MODEL), BF16), jax.ShapeDtypeStruct((n, D_MODEL), BF16)],
        compiler_params=_cp(("parallel",)),
    )(up, up, up, *brow, bcol, pw_bf, pscale)


def _pool_bwd_a(dp, dsave, pw_bf, pwt_bf, pscale):
    n = dp.shape[0]
    tb = POOL_TB

    def body(dp_ref, d_ref, pw_ref, pwt_ref, sc_ref, dd_ref, dds_ref, gw_ref, gs_ref):
        i = pl.program_id(0)

        @pl.when(i == 0)
        def _():
            gw_ref[...] = jnp.zeros_like(gw_ref)
            gs_ref[...] = jnp.zeros_like(gs_ref)

        for g in range(N_POOL):
            sl = slice(g * POOL_DIM, (g + 1) * POOL_DIM)
            dpv = dp_ref[:, sl].astype(F32)
            dv = d_ref[:, sl]
            dpw_bf = (dpv * sc_ref[:, sl]).astype(BF16)
            dd = _dot(dpw_bf, pwt_ref[g])
            dd_ref[:, sl] = dd.astype(BF16)
            dds_ref[:, sl] = (dd * _pool_inv(i, g, n)).astype(BF16)
            gw_ref[g] += _dot_tn(dv, dpw_bf)
            gs_ref[0:1, sl] += jnp.sum(dpv * _dot(dv, pw_ref[g]), axis=0, keepdims=True)

    wspec = pl.BlockSpec((N_POOL, POOL_DIM, POOL_DIM), lambda i: (0, 0, 0))
    return pl.pallas_call(
        body, name="pool_bwd_a", grid=(n // tb,),
        in_specs=[_row_spec(tb, D_MODEL), _row_spec(tb, D_MODEL), wspec, wspec, _par_spec(D_MODEL)],
        out_specs=[_row_spec(tb, D_MODEL), _row_spec(tb, D_MODEL), wspec, _acc_spec(D_MODEL)],
        out_shape=[jax.ShapeDtypeStruct((n, D_MODEL), BF16), jax.ShapeDtypeStruct((n, D_MODEL), BF16),
                   jax.ShapeDtypeStruct((N_POOL, POOL_DIM, POOL_DIM), F32), jax.ShapeDtypeStruct((8, D_MODEL), F32)],
        compiler_params=_cp(("arbitrary",)),
    )(dp, dsave, pw_bf, pwt_bf, pscale)


def _pool_bwd_b(dds, dd, consts_t):
    n = dd.shape[0]
    tb = POOL_TB
    nb = n // tb
    brow, bcol = consts_t
    prev, cur, nxt = _pool_halo_specs(n, D_MODEL)

    def body(p_ref, c_ref, n_ref, *rest):
        brow_refs, (bcol_ref, dd_ref, o_ref) = rest[:N_POOL], rest[N_POOL:]
        i = pl.program_id(0)
        for g in range(N_POOL):
            sl = slice(g * POOL_DIM, (g + 1) * POOL_DIM)
            box = _pool_box(p_ref, c_ref, n_ref, brow_refs, bcol_ref, g, i, nb, True)
            o_ref[:, sl] = (box - dd_ref[:, sl].astype(F32)).astype(BF16)

    return pl.pallas_call(
        body, name="pool_bwd_b", grid=(nb,),
        in_specs=[prev, cur, nxt] + _pool_const_specs(brow) + [_row_spec(tb, D_MODEL)],
        out_specs=_row_spec(tb, D_MODEL),
        out_shape=jax.ShapeDtypeStruct((n, D_MODEL), BF16),
        compiler_params=_cp(("parallel",)),
    )(dds, dds, dds, *brow, bcol, dd)


def _pair_add(slabs, recvs, core, name):
    na = len(slabs)
    hr = [t.shape[1] // 4 for t in slabs]

    def body(core_ref, *refs):
        for a in range(na):
            refs[2 * na + a][...] = (refs[a][...] + refs[na + a][...]).astype(BF16)

    own = [pl.BlockSpec((None, hr[a], slabs[a].shape[2]), lambda j, i, c_ref: (j, 2 * c_ref[0] + i, 0)) for a in range(na)]
    got = [pl.BlockSpec((None, hr[a], slabs[a].shape[2]), lambda j, i, c_ref: (j, i, 0)) for a in range(na)]
    return pl.pallas_call(
        body, name=name,
        grid_spec=pltpu.PrefetchScalarGridSpec(num_scalar_prefetch=1, grid=(4, 2), in_specs=own + got, out_specs=got),
        out_shape=[jax.ShapeDtypeStruct(r.shape, BF16) for r in recvs],
        compiler_params=_cp(("arbitrary", "arbitrary")),
    )(core, *slabs, *recvs)


def _sum4(parts, core):
    na = len(parts)
    hr = [t.shape[1] // 2 for t in parts]

    def body(core_ref, *refs):
        for a in range(na):
            p = refs[a]
            refs[na + a][...] = ((p[0].astype(F32) + p[1].astype(F32)) + p[2].astype(F32)) + p[3].astype(F32)

    return pl.pallas_call(
        body, name="reduce_g_sum",
        grid_spec=pltpu.PrefetchScalarGridSpec(
            num_scalar_prefetch=1, grid=(2,),
            in_specs=[pl.BlockSpec((4, hr[a], parts[a].shape[2]), lambda i, c_ref: (0, i, 0)) for a in range(na)],
            out_specs=[pl.BlockSpec((hr[a], parts[a].shape[2]), lambda i, c_ref: (2 * c_ref[0] + i, 0))
                       for a in range(na)]),
        out_shape=[jax.ShapeDtypeStruct((2 * t.shape[1], t.shape[2]), F32) for t in parts],
        compiler_params=_cp(("arbitrary",)),
    )(core, *parts)


def _adamw(w, g, m, v, name):
    r, cdim = w.shape
    tb = _row_block(r, 256)
    c1 = 1.0 - ADAM_B1 ** ADAM_STEP
    c2 = 1.0 - ADAM_B2 ** ADAM_STEP

    def body(w_ref, g_ref, m_ref, v_ref, d_ref, nm_ref, nv_ref):
        gv = g_ref[...]
        nm = ADAM_B1 * m_ref[...] + (1.0 - ADAM_B1) * gv
        nv = ADAM_B2 * v_ref[...] + (1.0 - ADAM_B2) * (gv * gv)
        m_hat = nm / c1
        v_hat = nv / c2
        d_ref[...] = -ADAM_LR * (m_hat / (jnp.sqrt(v_hat) + ADAM_EPS) + ADAM_WD * w_ref[...])
        nm_ref[...] = nm
        nv_ref[...] = nv

    spec = _row_spec(tb, cdim)
    shp = jax.ShapeDtypeStruct((r, cdim), F32)
    return pl.pallas_call(
        body, name=name, grid=(r // tb,),
        in_specs=[spec] * 4, out_specs=[spec] * 3, out_shape=[shp] * 3,
        compiler_params=_cp(("parallel",)),
    )(w, g, m, v)


def _mesh_pos():
    return lax.axis_index("x"), lax.axis_index("y"), lax.axis_index("c")


_ANY = pl.BlockSpec(memory_space=pl.ANY)


def _remote(src, dst, send_sem, recv_sem, device):
    return pltpu.make_async_remote_copy(src_ref=src, dst_ref=dst, send_sem=send_sem, recv_sem=recv_sem,
                                        device_id=device, device_id_type=MESH)


def _other_chips(x, y):
    return [(1 - x, y), (x, 1 - y), (1 - x, 1 - y)]


def _half(nrows, h):
    return pl.ds(h * (nrows // 2), nrows // 2)


_GATHER_SEMS = 7


def _gather_steps(ins, outs, send_sems, recv_sems):
    na = len(ins)
    nrow = [r.shape[0] for r in ins]

    def copies():
        x, y, c = _mesh_pos()
        me = 2 * x + y
        sib = (x, y, 1 - c)
        chips = _other_chips(x, y)

        def ici(k, a, slot):
            px, py = chips[k]
            rows = _half(nrow[a], c)
            return _remote(ins[a].at[rows, :], outs[a].at[slot, rows, :], send_sems.at[k * na + a],
                           recv_sems.at[k * na + a], (px, py, c))

        def fwd(k, a, h):
            px, py = chips[k]
            blk = outs[a].at[2 * px + py, _half(nrow[a], h), :]
            return _remote(blk, blk, send_sems.at[(3 + k) * na + a], recv_sems.at[(3 + k) * na + a], sib)

        def own(a):
            return _remote(ins[a], outs[a].at[me], send_sems.at[6 * na + a], recv_sems.at[6 * na + a], sib)

        slots = [2 * px + py for px, py in chips]
        return ici, fwd, own, me, c, slots

    def start():
        ici, _, own, me, _, _ = copies()
        for a in range(na):
            own(a).start()
        for k in range(3):
            for a in range(na):
                ici(k, a, me).start()

    def finish():
        ici, fwd, own, me, c, slots = copies()
        for k in range(3):
            for a in range(na):
                ici(k, a, slots[k]).wait_recv()
                fwd(k, a, c).start()
        for k in range(3):
            for a in range(na):
                fwd(k, a, 1 - c).wait_recv()
        for a in range(na):
            own(a).wait_recv()
        for a in range(na):
            own(a).wait_send()
        for k in range(3):
            for a in range(na):
                ici(k, a, me).wait_send()
                fwd(k, a, c).wait_send()

    return start, finish


def _exchange_steps(ins, outs, send_sems, recv_sems, local_sems):
    na = len(ins)

    def copies():
        x, y, c = _mesh_pos()
        me = 2 * x + y
        chips = _other_chips(x, y)

        def copy(k, a, slot):
            px, py = chips[k]
            return _remote(ins[a].at[2 * px + py], outs[a].at[slot], send_sems.at[k * na + a], recv_sems.at[k * na + a],
                           (px, py, c))

        def local(a):
            return pltpu.make_async_copy(ins[a].at[me], outs[a].at[me], local_sems.at[a])

        return copy, local, me, [2 * px + py for px, py in chips]

    def start():
        copy, local, me, _ = copies()
        for a in range(na):
            local(a).start()
        for k in range(3):
            for a in range(na):
                copy(k, a, me).start()

    def finish():
        copy, local, me, slots = copies()
        for k in range(3):
            for a in range(na):
                copy(k, a, slots[k]).wait_recv()
        for k in range(3):
            for a in range(na):
                copy(k, a, me).wait_send()
        for a in range(na):
            local(a).wait()

    return start, finish


def _gather_weights(shards, conv8):
    na = len(shards)

    def body(*refs):
        ins, conv_in = refs[:na], refs[na]
        outs, conv_out = refs[na + 1:2 * na + 1], refs[2 * na + 1]
        send_sems, recv_sems, local_sems = refs[2 * na + 2:]
        x, y, c = _mesh_pos()
        me = 2 * x + y
        chips = _other_chips(x, y)

        def conv(k, slot):
            px, py = chips[k]
            return _remote(conv_in, conv_out.at[slot], send_sems.at[7 * na + k], recv_sems.at[7 * na + k], (px, py, c))

        start, finish = _gather_steps(ins, outs, send_sems, recv_sems)
        local = pltpu.make_async_copy(conv_in, conv_out.at[me], local_sems.at[0])
        local.start()
        start()
        sends = [conv(k, me) for k in range(3)]
        for cp in sends:
            cp.start()
        finish()
        for k in range(3):
            px, py = chips[k]
            conv(k, 2 * px + py).wait_recv()
        for cp in sends:
            cp.wait_send()
        local.wait()

    nsem = _GATHER_SEMS * na + 3
    return pl.pallas_call(
        body, name="gather_w", in_specs=[_ANY] * (na + 1), out_specs=[_ANY] * (na + 1),
        out_shape=[jax.ShapeDtypeStruct((4,) + t.shape, t.dtype) for t in shards] +
                  [jax.ShapeDtypeStruct((4,) + conv8.shape, conv8.dtype)],
        scratch_shapes=[pltpu.SemaphoreType.DMA((nsem,)), pltpu.SemaphoreType.DMA((nsem,)),
                        pltpu.SemaphoreType.DMA((1,))],
    )(*shards, conv8)


def _pair_swap(slabs, name):
    na = len(slabs)

    def body(*refs):
        ins, outs = refs[:na], refs[na:2 * na]
        send_sems, recv_sems = refs[2 * na:]
        x, y, c = _mesh_pos()
        cps = [_remote(ins[a].at[:, _half(slabs[a].shape[1], 1 - c), :], outs[a], send_sems.at[a], recv_sems.at[a],
                       (x, y, 1 - c)) for a in range(na)]
        for cp in cps:
            cp.start()
        for cp in cps:
            cp.wait()

    return pl.pallas_call(
        body, name=name, in_specs=[_ANY] * na, out_specs=[_ANY] * na,
        out_shape=[jax.ShapeDtypeStruct((4, t.shape[1] // 2, t.shape[2]), t.dtype) for t in slabs],
        scratch_shapes=[pltpu.SemaphoreType.DMA((na,)), pltpu.SemaphoreType.DMA((na,))],
    )(*slabs)


def _chip_exchange(pairs):
    na = len(pairs)

    def body(*refs):
        start, finish = _exchange_steps(refs[:na], refs[na:2 * na], *refs[2 * na:])
        start()
        finish()

    return pl.pallas_call(
        body, name="reduce_g_ici", in_specs=[_ANY] * na, out_specs=[_ANY] * na,
        out_shape=[jax.ShapeDtypeStruct(t.shape, t.dtype) for t in pairs],
        scratch_shapes=[pltpu.SemaphoreType.DMA((3 * na,)), pltpu.SemaphoreType.DMA((3 * na,)),
                        pltpu.SemaphoreType.DMA((na,))],
    )(*pairs)


def _share_halves(totals):
    na = len(totals)

    def body(*refs):
        bufs = refs[na:2 * na]
        send_sems, recv_sems = refs[2 * na:]
        x, y, c = _mesh_pos()

        def copy(a, h):
            blk = bufs[a].at[_half(totals[a].shape[0], h), :]
            return _remote(blk, blk, send_sems.at[a], recv_sems.at[a], (x, y, 1 - c))

        sends = [copy(a, c) for a in range(na)]
        for cp in sends:
            cp.start()
        for a in range(na):
            copy(a, 1 - c).wait_recv()
        for cp in sends:
            cp.wait_send()

    return pl.pallas_call(
        body, name="reduce_g_share", in_specs=[_ANY] * na, out_specs=[_ANY] * na,
        out_shape=[jax.ShapeDtypeStruct(t.shape, t.dtype) for t in totals],
        input_output_aliases={a: a for a in range(na)},
        scratch_shapes=[pltpu.SemaphoreType.DMA((na,)), pltpu.SemaphoreType.DMA((na,))],
    )(*totals)


def _allreduce_small(v, name):
    r, cdim = v.shape

    def body(v_ref, out_ref, buf, send_sems, recv_sems):
        x, y, c = _mesh_pos()
        me = 4 * x + 2 * y + c
        buf[me] = v_ref[...]
        rel = [(bx, by, bc) for bx in (0, 1) for by in (0, 1) for bc in (0, 1)][1:]

        def peer(b):
            bx, by, bc = b
            return ((1 - x) if bx else x, (1 - y) if by else y, (1 - c) if bc else c)

        def copy(k, slot):
            return pltpu.make_async_remote_copy(
                src_ref=v_ref, dst_ref=buf.at[slot], send_sem=send_sems.at[k], recv_sem=recv_sems.at[k],
                device_id=peer(rel[k]), device_id_type=MESH)

        sends = [copy(k, me) for k in range(7)]
        for cp in sends:
            cp.start()
        for k in range(7):
            px, py, pc = peer(rel[k])
            copy(k, 4 * px + 2 * py + pc).wait_recv()
        for cp in sends:
            cp.wait_send()
        acc = buf[0]
        for j in range(1, 8):
            acc = acc + buf[j]
        out_ref[...] = acc

    vm = pl.BlockSpec(memory_space=pltpu.VMEM)
    return pl.pallas_call(
        body, name=name, in_specs=[vm], out_specs=[vm, vm],
        out_shape=[jax.ShapeDtypeStruct((r, cdim), F32), jax.ShapeDtypeStruct((8, r, cdim), F32)],
        scratch_shapes=[pltpu.SemaphoreType.DMA((7,)), pltpu.SemaphoreType.DMA((7,))],
    )(v)


def _chip_bcast(v, name):
    def body(v_ref, out_ref, send_sems, recv_sems):
        x, y, c = _mesh_pos()
        me = 2 * x + y
        chips = _other_chips(x, y)
        out_ref[me] = v_ref[...]

        def copy(k, slot):
            px, py = chips[k]
            return _remote(v_ref, out_ref.at[slot], send_sems.at[k], recv_sems.at[k], (px, py, c))

        sends = [copy(k, me) for k in range(3)]
        for cp in sends:
            cp.start()
        for k, (px, py) in enumerate(chips):
            copy(k, 2 * px + py).wait_recv()
        for cp in sends:
            cp.wait_send()

    vm = pl.BlockSpec(memory_space=pltpu.VMEM)
    return pl.pallas_call(
        body, name=name, in_specs=[vm], out_specs=vm,
        out_shape=jax.ShapeDtypeStruct((4,) + v.shape, F32),
        scratch_shapes=[pltpu.SemaphoreType.DMA((3,)), pltpu.SemaphoreType.DMA((3,))],
    )(v)


_BIG = (("in_proj", (D_MODEL, D_IN_PROJ // 4), 1), ("w_out", (2 * D_MODEL // 4, D_MODEL), 0),
        ("w_gate", (D_MODEL, D_FF // 4), 1), ("w_up", (D_MODEL, D_FF // 4), 1), ("w_down", (D_FF // 4, D_MODEL), 0),
        ("pool_w", (N_POOL * POOL_DIM // 4, POOL_DIM), None))


def _assemble(name, t):
    _, r, c = t.shape
    axis = {n: ax for n, _, ax in _BIG}[name]
    if axis == 0:
        return t.reshape(4 * r, c)
    if axis == 1:
        return t.transpose(1, 0, 2).reshape(r, 4 * c)
    return t.reshape(4, N_POOL, POOL_DIM // 4, POOL_DIM).transpose(1, 0, 2, 3).reshape(N_POOL, POOL_DIM, POOL_DIM)


def _to_slabs(name, g):
    (r, c), axis = {n: (sh, ax) for n, sh, ax in _BIG}[name]
    if axis == 0:
        return g.reshape(4, r, c)
    if axis == 1:
        return g.reshape(r, 4, c).transpose(1, 0, 2)
    return g.reshape(N_POOL, 4, POOL_DIM // 4, POOL_DIM).transpose(1, 0, 2, 3).reshape(4, r, c)


_EARLY = ("in_proj",)
_LATE = tuple(n for n, _, _ in _BIG if n not in _EARLY)


def _reduce_grads(early_grads, late_parts, core):
    slabs = [_to_slabs(n, early_grads[n]) for n in _EARLY]
    pairs = _pair_add(slabs, _pair_swap(slabs, "reduce_g_d2d"), core, "reduce_g_pair")
    parts = dict(zip(_EARLY, _chip_exchange(pairs)), **dict(zip(_LATE, late_parts)))
    names = [n for n, _, _ in _BIG]
    totals = _sum4([parts[n] for n in names], core)
    return dict(zip(names, _share_halves(totals)))


def _pad_cols(w, n):
    return jnp.concatenate([w, jnp.zeros((w.shape[0], n - w.shape[1]), w.dtype)], axis=1)


def _device_step(x, mod, mod_ctx, ctx, target, wts, w8, small, tb, late_shards=None, core=None):
    n = x.shape[0]
    d = D_MODEL

    win = wts["in_proj"]
    wz, wxd, wup = win[:, 0:d], _pad_cols(win[:, d:d + D_XBC + 2 * SSD_HEADS], D_XD), win[:, d + D_XBC + 2 * SSD_HEADS:]

    emb_g, emb_b = _vec(small["emb_ln_g"]), _vec(small["emb_ln_b"])
    ln1_g, ln1_b = _vec(small["ln1_g"]), _vec(small["ln1_b"])
    ln2_g, ln2_b = _vec(small["ln2_g"]), _vec(small["ln2_b"])
    gn = _vec(small["ssd_norm_g"])
    pscale = _vec(small["pool_scale"])
    conv_b = _vec(small["conv_b"])
    dskip_e = jnp.repeat(small["d_skip"].reshape(-1), HEAD_DIM).reshape(1, d)
    zpad = jnp.zeros((2, 1, 128 - SSD_HEADS), F32)
    bias2 = jnp.concatenate([small["dt_bias"].reshape(2, 1, SSD_HEADS), zpad], axis=2)
    a2 = jnp.concatenate([-jnp.exp(small["a_log"].reshape(2, 1, SSD_HEADS)), zpad], axis=2)
    rexp = (jnp.arange(128)[:, None] == (jnp.arange(d)[None, :] // HEAD_DIM)).astype(BF16)
    rexp_t = rexp.T

    sh1, sc1, g1, sh2, sc2, g2 = [mod[:, i * d:(i + 1) * d] for i in range(6)]
    sh1c, sc1c = mod_ctx[:, 0:d], mod_ctx[:, d:2 * d]

    tbc = min(tb, ctx.shape[0])
    xc0, hc = _ln_mod(ctx, emb_g, emb_b, sh1c, sc1c, tbc, "ln_mod_ctx")
    xdc = _matmul_nn([(hc, wxd)], F32, 512, D_XD, "in_proj_ctx")
    uc, ac = _conv_fwd(xdc, w8, conv_b, tbc, "conv_fwd_ctx")
    hzero = jnp.zeros((2, D_STATE, d), F32)
    _, hprev_c, hfin_c = _ssd_fwd(ac, xdc, bias2, a2, rexp, hzero, "ssd_fwd_ctx")

    x0, h1 = _ln_mod(x, emb_g, emb_b, sh1, sc1, tb, "ln_mod")
    z, xd, up = _matmul_nn_multi(h1, [wz, wxd, wup], [BF16, F32, BF16], 512, "in_proj")
    u, act = _conv_fwd(xd, w8, conv_b, tb, "conv_fwd")
    y, hprev, _, *landed = _ssd_fwd(act, xd, bias2, a2, rexp, hfin_c, "ssd_fwd", gather=late_shards or ())
    if late_shards is not None:
        wts = dict(wts, **{nme: _assemble(nme, t) for nme, t in zip(_LATE, landed)})
    wout = wts["w_out"]
    wg, wu, wd = wts["w_gate"], wts["w_up"], wts["w_down"]
    pw = wts["pool_w"]
    yn = _merge_fwd(y, u, z, dskip_e, gn, tb)
    pconst = _pool_consts(False)
    pool, dsave = _pool_fwd(up, pconst, pw, pscale)
    mix = _matmul_nn([(yn, wout[0:d]), (pool, wout[d:2 * d])], BF16, MM_ROWS, 1024, "out_proj")
    x1, h2 = _res_ln(x0, mix, g1, ln1_g, ln1_b, sh2, sc2, tb)

    gate, upp, hmid = _swiglu_fwd(h2, wg, wu, MM_ROWS, D_FF // 2)
    ffn = _matmul_nn([(hmid, wd)], BF16, MM_ROWS, 1024, "ffn_down")
    dffn, dr2, acc2 = _final_ln_loss(x1, ffn, g2, ln2_g, ln2_b, target, tb)
    loss = (0.5 / d) * jnp.sum(acc2[3])

    dgate, dupp = _swiglu_bwd(dffn, wd.T, gate, upp, MM_ROWS, D_FF // 2)
    g_wdown = _matmul_tn(hmid, dffn, MM_ROWS, 1024, "g_w_down")
    g_wgate, g_wup = _matmul_tn_multi(h2, [dgate, dupp], MM_ROWS, [D_FF // 2, D_FF // 2], "g_w_gate_up")
    dh2 = _matmul_nn([(dgate, wg.T), (dupp, wu.T)], BF16, 512, 1024, "d_h2")
    dmix, dr1, acc1 = _bwd_ln1(dr2, dh2, x1, x0, mix, g1, sc2, ln1_g, tb)

    dyn, dpool = _matmul_nn_multi(dmix, [wout[0:d].T, wout[d:2 * d].T], [BF16, BF16], MM_ROWS, "d_yn_pool")
    g_wout = jnp.concatenate([_matmul_tn(yn, dmix, MM_ROWS, 1024, "g_w_out_a"),
                              _matmul_tn(pool, dmix, MM_ROWS, 1024, "g_w_out_b")], axis=0)
    dd, dds, g_pw, accp = _pool_bwd_a(dpool, dsave, pw, jnp.swapaxes(pw, 1, 2), pscale)
    dup = _pool_bwd_b(dds, dd, _pool_consts(True))
    dy, dz, accm = _merge_bwd(dyn, y, u, z, dskip_e, gn, tb)
    lam0 = jnp.zeros((2, D_STATE, d), F32)
    late_grads = dict(w_out=g_wout, w_gate=g_wgate, w_up=g_wup, w_down=g_wdown, pool_w=g_pw)
    pairs = ()
    if late_shards is not None:
        slabs = [_to_slabs(nme, late_grads[nme]) for nme in _LATE]
        pairs = _pair_add(slabs, _pair_swap(slabs, "reduce_g_d2d_late"), core, "reduce_g_pair_late")
    dxs, dbc, ddt, accs, lam_c, *arrived = _ssd_bwd(act, xd, bias2, a2, rexp, rexp_t, dy, hprev, lam0, "ssd_bwd",
                                                    exchange=pairs)
    du, accb = _conv_bwd_a(dxs, dy, dskip_e, dbc, u, tb, "conv_bwd_a")
    dxd, accw = _conv_bwd_b(du, xd, ddt, w8, tb, "conv_bwd_b")

    lc = ctx.shape[0]
    zeros_c = jnp.zeros((lc, d), BF16)
    dxs_c, dbc_c, ddt_c, accs_c, _ = _ssd_bwd(ac, xdc, bias2, a2, rexp, rexp_t, zeros_c, hprev_c, lam_c, "ssd_bwd_ctx")
    du_c, accb_c = _conv_bwd_a(dxs_c, zeros_c, dskip_e, dbc_c, uc, tbc, "conv_bwd_a_ctx")
    dxd_c, accw_c = _conv_bwd_b(du_c, xdc, ddt_c, w8, tbc, "conv_bwd_b_ctx")
    dhc = _matmul_nn([(dxd_c, wxd.T)], F32, 512, 1024, "d_hc")
    _, acc0c = _bwd_ln0(None, dhc, ctx, emb_g, emb_b, sc1c, tbc, "bwd_ln0_ctx")

    dh1 = _matmul_nn([(dz, wz.T), (dxd, wxd.T), (dup, wup.T)], BF16, MM_ROWS, 1024, "d_h1")
    g_wz, g_wxd, g_wpo = _matmul_tn_multi(h1, [dz, dxd, dup], 512, [1024, D_XD, 1024], "g_in_proj")
    g_wxd = g_wxd + _matmul_tn(hc, dxd_c, 512, D_XD, "g_in_proj_xd_ctx")
    g_win = jnp.concatenate([g_wz, g_wxd[:, 0:D_XBC + 2 * SSD_HEADS], g_wpo], axis=1)
    grad_x, acc0 = _bwd_ln0(dr1, dh1, x, emb_g, emb_b, sc1, tb, "bwd_ln0")

    zero_d = jnp.zeros((1, d), F32)
    dmod = jnp.concatenate([acc0[1:2], acc0[0:1], acc1[4:5], acc1[1:2], acc1[0:1], acc2[2:3]], axis=1)
    dmodc = jnp.concatenate([acc0c[1:2], acc0c[0:1]] + [zero_d] * 4, axis=1)

    big = dict(in_proj=g_win)
    if late_shards is None:
        big.update(late_grads)
    sml = dict(
        dmod=dmod, dmod_ctx=dmodc, emb_ln_g=acc0[2] + acc0c[2], emb_ln_b=acc0[3] + acc0c[3],
        conv_w=accw[0:D_CONV] + accw_c[0:D_CONV], conv_b=accb[0] + accb_c[0],
        dt_bias=accs[:, 0, 0:SSD_HEADS] + accs_c[:, 0, 0:SSD_HEADS],
        a_log=accs[:, 1, 0:SSD_HEADS] + accs_c[:, 1, 0:SSD_HEADS],
        d_skip=jnp.sum(accm[1].reshape(SSD_HEADS, HEAD_DIM), axis=1),
        ssd_norm_g=accm[0], pool_scale=accp[0], ln1_g=acc1[2], ln1_b=acc1[3], ln2_g=acc2[0], ln2_b=acc2[1])
    return loss, grad_x, big, sml, (arrived if late_shards is not None else None)


_SMALL = ("c_ctx", "emb_ln_g", "emb_ln_b", "b_ada", "conv_w", "conv_b", "dt_bias", "a_log", "d_skip",
          "ssd_norm_g", "pool_scale", "ln1_g", "ln1_b", "ln2_g", "ln2_b")


def _small_rows(size):
    return -(-size // 1024)


def _pack_small(vals, names):
    pieces, rows = [], 0
    for nme in names:
        flat = vals[nme].reshape(-1).astype(F32)
        nr = _small_rows(flat.shape[0])
        pieces.append(flat)
        if nr * 1024 > flat.shape[0]:
            pieces.append(jnp.zeros((nr * 1024 - flat.shape[0],), F32))
        rows += nr
    if rows % 8:
        pieces.append(jnp.zeros(((8 - rows % 8) * 1024,), F32))
    return jnp.concatenate(pieces).reshape(-1, 1024)


def _small_offsets(shapes, names):
    out, off = {}, 0
    for nme in names:
        nr = _small_rows(math.prod(shapes[nme]))
        out[nme] = (off, nr)
        off += nr
    return out


def _unpack_small(packed, shapes, names):
    out = {}
    for nme, (off, nr) in _small_offsets(shapes, names).items():
        out[nme] = packed[off:off + nr].reshape(-1)[:math.prod(shapes[nme])].reshape(shapes[nme])
    return out


_WEIGHT_ORDER = ("c_ctx", "emb_ln_g", "emb_ln_b", "w_ada", "b_ada", "in_proj", "conv_w", "conv_b", "dt_bias", "a_log",
                 "d_skip", "ssd_norm_g", "pool_w", "pool_scale", "w_out", "ln1_g", "ln1_b", "w_gate", "w_up", "w_down",
                 "ln2_g", "ln2_b")


def _as2d(a):
    return a.reshape(-1, a.shape[-1])


def kernel(x, c, ctx, c_ctx, emb_ln_g, emb_ln_b, w_ada, b_ada, in_proj, conv_w, conv_b, dt_bias, a_log, d_skip, ssd_norm_g, pool_w, pool_scale, w_out, ln1_g, ln1_b, w_gate, w_up, w_down, ln2_g, ln2_b, loss_target, m_c_ctx, m_emb_ln_g, m_emb_ln_b, m_w_ada, m_b_ada, m_in_proj, m_conv_w, m_conv_b, m_dt_bias, m_a_log, m_d_skip, m_ssd_norm_g, m_pool_w, m_pool_scale, m_w_out, m_ln1_g, m_ln1_b, m_w_gate, m_w_up, m_w_down, m_ln2_g, m_ln2_b, v_c_ctx, v_emb_ln_g, v_emb_ln_b, v_w_ada, v_b_ada, v_in_proj, v_conv_w, v_conv_b, v_dt_bias, v_a_log, v_d_skip, v_ssd_norm_g, v_pool_w, v_pool_scale, v_w_out, v_ln1_g, v_ln1_b, v_w_gate, v_w_up, v_w_down, v_ln2_g, v_ln2_b):
    w = dict(c_ctx=c_ctx, emb_ln_g=emb_ln_g, emb_ln_b=emb_ln_b, w_ada=w_ada, b_ada=b_ada, in_proj=in_proj, conv_w=conv_w,
             conv_b=conv_b, dt_bias=dt_bias, a_log=a_log, d_skip=d_skip, ssd_norm_g=ssd_norm_g, pool_w=pool_w,
             pool_scale=pool_scale, w_out=w_out, ln1_g=ln1_g, ln1_b=ln1_b, w_gate=w_gate, w_up=w_up, w_down=w_down,
             ln2_g=ln2_g, ln2_b=ln2_b)
    m = dict(c_ctx=m_c_ctx, emb_ln_g=m_emb_ln_g, emb_ln_b=m_emb_ln_b, w_ada=m_w_ada, b_ada=m_b_ada, in_proj=m_in_proj,
             conv_w=m_conv_w, conv_b=m_conv_b, dt_bias=m_dt_bias, a_log=m_a_log, d_skip=m_d_skip,
             ssd_norm_g=m_ssd_norm_g, pool_w=m_pool_w, pool_scale=m_pool_scale, w_out=m_w_out, ln1_g=m_ln1_g,
             ln1_b=m_ln1_b, w_gate=m_w_gate, w_up=m_w_up, w_down=m_w_down, ln2_g=m_ln2_g, ln2_b=m_ln2_b)
    v = dict(c_ctx=v_c_ctx, emb_ln_g=v_emb_ln_g, emb_ln_b=v_emb_ln_b, w_ada=v_w_ada, b_ada=v_b_ada, in_proj=v_in_proj,
             conv_w=v_conv_w, conv_b=v_conv_b, dt_bias=v_dt_bias, a_log=v_a_log, d_skip=v_d_skip,
             ssd_norm_g=v_ssd_norm_g, pool_w=v_pool_w, pool_scale=v_pool_scale, w_out=v_w_out, ln1_g=v_ln1_g,
             ln1_b=v_ln1_b, w_gate=v_w_gate, w_up=v_w_up, w_down=v_w_down, ln2_g=v_ln2_g, ln2_b=v_ln2_b)

    xi, yi, ci = _mesh_pos()
    chip = 2 * xi + yi

    dev = 4 * xi + 2 * yi + ci
    d = D_MODEL
    core = ci.reshape(1).astype(jnp.int32)

    crow = jnp.concatenate([c.reshape(1, d), jnp.zeros((7, d), F32)], axis=0)
    _, c_all = _allreduce_small(crow, "gather_c")
    c16 = jnp.concatenate([c_all[:, 0, :], c_ctx.reshape(1, d), jnp.zeros((MOD_ROWS - 9, d), F32)], axis=0)
    ncol = w_ada.shape[-1]
    wada_bf = w_ada[0].astype(BF16)
    b_mine = lax.dynamic_slice_in_dim(b_ada, chip * ncol, ncol, axis=1)
    mods4 = _chip_bcast(_mods_fwd(c16, wada_bf, b_mine), "gather_mods")
    mods = mods4.transpose(1, 0, 2).reshape(MOD_ROWS, 4 * ncol)
    mod = lax.dynamic_slice_in_dim(mods, dev, 1, axis=0)
    mod_ctx = mods[8:9]

    shard = {name: w[name][0].astype(BF16).reshape(shp) for name, shp, _ in _BIG}
    conv8 = jnp.concatenate([conv_w[0], jnp.zeros((8 - D_CONV, conv_w.shape[-1]), F32)], axis=0)
    *gathered, conv4 = _gather_weights([shard[nme] for nme in _EARLY], conv8)
    wts = {nme: _assemble(nme, t) for nme, t in zip(_EARLY, gathered)}
    w8 = conv4.transpose(1, 0, 2).reshape(8, D_XBC)
    small = {nme: (w[nme] if nme in ("c_ctx", "emb_ln_g", "emb_ln_b") else w[nme][0]) for nme in _SMALL if nme != "conv_w"}

    loss, grad_x, big, sml, late_parts = _device_step(x[0], mod, mod_ctx, ctx[0], loss_target[0], wts, w8, small, 512,
                                                      late_shards=[shard[nme] for nme in _LATE], core=core)
    loss = lax.psum(loss, ("x", "y", "c"))

    g_big = _reduce_grads(big, late_parts, core)
    reduced = tuple(sml)
    small_shapes = {nme: sml[nme].shape for nme in reduced}
    total, each = _allreduce_small(_pack_small(sml, reduced), "reduce_small")
    g_small = _unpack_small(total, small_shapes, reduced)
    cw_cols = conv_w.shape[-1]
    g_small["conv_w"] = lax.dynamic_slice_in_dim(g_small["conv_w"], chip * cw_cols, cw_cols, axis=1)

    off, nr = _small_offsets(small_shapes, reduced)["dmod"]
    dm16 = jnp.concatenate([each[:, off:off + nr, :].reshape(8, nr * 1024)[:, :6 * d], g_small["dmod_ctx"],
                            jnp.zeros((MOD_ROWS - 9, 6 * d), F32)], axis=0)
    dm_mine = lax.dynamic_slice_in_dim(dm16, chip * ncol, ncol, axis=1)
    g_wada = _mods_bwd_w(c16.T, dm_mine)
    g_small["b_ada"] = _mods_bwd_b(dm16)[0:1]
    c_part = _mods_bwd_c(dm_mine, wada_bf, c16)[8:16]
    g_small["c_ctx"] = _allreduce_small(c_part, "reduce_c_ctx")[0][0]

    grads, delta, new_m, new_v = {}, {}, {}, {}
    grads["w_ada"] = g_wada[None]
    delta["w_ada"], new_m["w_ada"], new_v["w_ada"] = (
        t[None] for t in _adamw(w_ada[0], g_wada, m_w_ada[0], v_w_ada[0], "adamw_w_ada"))
    for name, _, _ in _BIG:
        g2 = _as2d(g_big[name])
        d2, m2, v2 = _adamw(_as2d(w[name][0]), g2, _as2d(m[name][0]), _as2d(v[name][0]), "adamw_" + name)
        grads[name] = g2.reshape(w[name].shape)
        delta[name], new_m[name], new_v[name] = (t.reshape(w[name].shape) for t in (d2, m2, v2))
    shp = {nme: w[nme].shape for nme in _SMALL}
    gp = _pack_small(g_small, _SMALL)
    dp, mp, vp = _adamw(_pack_small(w, _SMALL), gp, _pack_small(m, _SMALL), _pack_small(v, _SMALL), "adamw_small")
    for dst, src in ((grads, gp), (delta, dp), (new_m, mp), (new_v, vp)):
        dst.update(_unpack_small(src, shp, _SMALL))

    return (loss, grad_x[None], *[grads[nme] for nme in _WEIGHT_ORDER], *[delta[nme] for nme in _WEIGHT_ORDER],
            *[new_m[nme] for nme in _WEIGHT_ORDER], *[new_v[nme] for nme in _WEIGHT_ORDER])
```

```python
import math

import jax
import jax.numpy as jnp
from jax import lax
from jax.experimental import pallas as pl
from jax.experimental.pallas import tpu as pltpu

F32 = jnp.float32
BF16 = jnp.bfloat16
MESH = pl.DeviceIdType.MESH

D_MODEL = 1024
SSD_HEADS = 16
HEAD_DIM = 64
D_STATE = 128
CHUNK = 128
D_CONV = 5
D_XBC = D_MODEL + 2 * D_STATE
D_XD = 1408
N_POOL = 4
POOL_DIM = 256
POOL_WINDOWS = (2, 4, 8, 16)
GRID_W = 64
D_FF = 2816
D_IN_PROJ = 3360
LN_EPS = 1e-5
ALPHA = 2.0 ** 0.25
POOL_TB = 512
MM_ROWS = 1024

ADAM_LR = 0.001
ADAM_B1 = 0.9
ADAM_B2 = 0.999
ADAM_EPS = 1e-08
ADAM_WD = 0.01
ADAM_STEP = 10

VMEM_LIMIT = 56 * 1024 * 1024


def _cp(sem=None):
    return pltpu.CompilerParams(dimension_semantics=sem, vmem_limit_bytes=VMEM_LIMIT)


def _sigmoid(x):
    return 1.0 / (1.0 + jnp.exp(-x))


def _silu(x):
    return x * _sigmoid(x)


def _dsilu(x):
    s = _sigmoid(x)
    return s * (1.0 + x * (1.0 - s))


def _softplus(x):
    t = jnp.exp(-jnp.abs(x))
    u = 1.0 + t
    log1p = jnp.where(u == 1.0, t, jnp.log(u) * t / (u - 1.0 + (u == 1.0)))
    return jnp.maximum(x, 0.0) + log1p


def _split(x, n):
    parts, r = [], x
    for _ in range(n):
        p = r.astype(BF16)
        parts.append(p)
        r = r - p.astype(F32)
    return parts


def _dot(a, b):
    return jnp.dot(a, b, preferred_element_type=F32)


def _dot_nt(a, b):
    return lax.dot_general(a, b, (((1,), (1,)), ((), ())), preferred_element_type=F32)


def _dot_tn(a, b):
    return lax.dot_general(a, b, (((0,), (0,)), ((), ())), preferred_element_type=F32)


def _dot_sel_l(sel_bf, x, n=3):
    out = None
    for p in _split(x, n):
        t = _dot(sel_bf, p)
        out = t if out is None else out + t
    return out


def _dot_sel_r(x, sel_bf, n=3):
    out = None
    for p in _split(x, n):
        t = _dot(p, sel_bf)
        out = t if out is None else out + t
    return out


ROW_SUB = 16


def _row_tiles(tb):
    assert tb % ROW_SUB == 0
    return [slice(s * ROW_SUB, (s + 1) * ROW_SUB) for s in range(tb // ROW_SUB)]


def _fold8(v):
    out = v[0:8, :]
    for r in range(8, v.shape[0], 8):
        out = out + v[r:r + 8, :]
    return out


def _row_block(n, cap=256, mult=8):
    best = None
    for t in range(mult, min(n, cap) + 1, mult):
        if n % t == 0:
            best = t
    return best if best is not None else n


def _vec(v):
    return v.reshape(1, -1).astype(F32)


MOD_ROWS = 16
MOD_TN = 512


def _mods_fwd(c16, w_bf, b):
    r, d = c16.shape
    n = w_bf.shape[1]

    def body(c_ref, w_ref, b_ref, o_ref):
        s = _silu(c_ref[...]).astype(BF16)
        o_ref[...] = _dot(s, w_ref[...]) + b_ref[...]

    return pl.pallas_call(
        body, name="mods_fwd", grid=(n // MOD_TN,),
        in_specs=[pl.BlockSpec((r, d), lambda j: (0, 0)),
                  pl.BlockSpec((d, MOD_TN), lambda j: (0, j)),
                  pl.BlockSpec((1, MOD_TN), lambda j: (0, j))],
        out_specs=pl.BlockSpec((r, MOD_TN), lambda j: (0, j)),
        out_shape=jax.ShapeDtypeStruct((r, n), F32),
        compiler_params=_cp(("arbitrary",)),
    )(c16, w_bf, b)


def _mods_bwd_w(ct16, dm16):
    d = ct16.shape[0]
    n = dm16.shape[1]

    def body(ct_ref, dm_ref, dw_ref):
        s = _silu(ct_ref[...])
        dm = dm_ref[...]
        acc = s[:, 0:1] * dm[0:1, :]
        for r in range(1, 9):
            acc = acc + s[:, r:r + 1] * dm[r:r + 1, :]
        dw_ref[...] = acc

    return pl.pallas_call(
        body, name="mods_bwd_w", grid=(n // MOD_TN,),
        in_specs=[pl.BlockSpec((d, MOD_ROWS), lambda j: (0, 0)),
                  pl.BlockSpec((MOD_ROWS, MOD_TN), lambda j: (0, j))],
        out_specs=pl.BlockSpec((d, MOD_TN), lambda j: (0, j)),
        out_shape=jax.ShapeDtypeStruct((d, n), F32),
        compiler_params=_cp(("arbitrary",)),
    )(ct16, dm16)


def _mods_bwd_c(dm16, w_bf, c16):
    d = c16.shape[1]
    n = dm16.shape[1]
    nk = n // MOD_TN

    def body(dm_ref, w_ref, c_ref, o_ref):
        k = pl.program_id(0)

        @pl.when(k == 0)
        def _():
            o_ref[...] = jnp.zeros_like(o_ref)

        o_ref[...] += _dot_nt(dm_ref[...].astype(BF16), w_ref[...])

        @pl.when(k == nk - 1)
        def _():
            o_ref[...] = o_ref[...] * (0.5 * _dsilu(c_ref[...]))

    return pl.pallas_call(
        body, name="mods_bwd_c", grid=(nk,),
        in_specs=[pl.BlockSpec((MOD_ROWS, MOD_TN), lambda k: (0, k)),
                  pl.BlockSpec((d, MOD_TN), lambda k: (0, k)),
                  pl.BlockSpec((MOD_ROWS, d), lambda k: (0, 0))],
        out_specs=pl.BlockSpec((MOD_ROWS, d), lambda k: (0, 0)),
        out_shape=jax.ShapeDtypeStruct((MOD_ROWS, d), F32),
        compiler_params=_cp(("arbitrary",)),
    )(dm16, w_bf, c16)


def _mods_bwd_b(dm16):
    n = dm16.shape[1]

    def body(dm_ref, o_ref):
        dm = dm_ref[...]
        acc = dm[0:1, :]
        for r in range(1, 9):
            acc = acc + dm[r:r + 1, :]
        o_ref[...] = jnp.broadcast_to(acc, (8, MOD_TN))

    return pl.pallas_call(
        body, name="mods_bwd_b", grid=(n // MOD_TN,),
        in_specs=[pl.BlockSpec((MOD_ROWS, MOD_TN), lambda j: (0, j))],
        out_specs=pl.BlockSpec((8, MOD_TN), lambda j: (0, j)),
        out_shape=jax.ShapeDtypeStruct((8, n), F32),
        compiler_params=_cp(("arbitrary",)),
    )(dm16)


def _ln_stats(x):
    mu = jnp.mean(x, axis=-1, keepdims=True)
    xc = x - mu
    var = jnp.mean(xc * xc, axis=-1, keepdims=True)
    rstd = lax.rsqrt(var + LN_EPS)
    return xc * rstd, rstd


def _ln_bwd(dxhat, xhat, rstd):
    m1 = jnp.mean(dxhat, axis=-1, keepdims=True)
    m2 = jnp.mean(dxhat * xhat, axis=-1, keepdims=True)
    return rstd * (dxhat - m1 - xhat * m2)


def _row_spec(tb, d):
    return pl.BlockSpec((tb, d), lambda i: (i, 0))


def _par_spec(d):
    return pl.BlockSpec((1, d), lambda i: (0, 0))


def _acc_spec(d):
    return pl.BlockSpec((8, d), lambda i: (0, 0))


def _ln_mod(x, g, b, sh, sc, tb, name):
    n, d = x.shape

    def body(x_ref, g_ref, b_ref, sh_ref, sc_ref, x0_ref, h_ref):
        g, b, sh, sc1 = g_ref[...], b_ref[...], sh_ref[...], 1.0 + sc_ref[...]
        for r in _row_tiles(min(tb, n)):
            xhat, _ = _ln_stats(x_ref[r, :])
            x0 = xhat * g + b
            x0_ref[r, :] = x0
            h_ref[r, :] = (x0 * sc1 + sh).astype(BF16)

    return pl.pallas_call(
        body, name=name, grid=(n // tb,),
        in_specs=[_row_spec(tb, d)] + [_par_spec(d)] * 4,
        out_specs=[_row_spec(tb, d), _row_spec(tb, d)],
        out_shape=[jax.ShapeDtypeStruct((n, d), F32), jax.ShapeDtypeStruct((n, d), BF16)],
        compiler_params=_cp(("parallel",)),
    )(x, g, b, sh, sc)


def _res_ln(xres, mix, gate, g, b, sh, sc, tb):
    n, d = xres.shape

    def body(xr_ref, mix_ref, gate_ref, g_ref, b_ref, sh_ref, sc_ref, x1_ref, h_ref):
        gate_v, g, b, sh, sc1 = gate_ref[...], g_ref[...], b_ref[...], sh_ref[...], 1.0 + sc_ref[...]
        for r in _row_tiles(tb):
            xhat, _ = _ln_stats(ALPHA * xr_ref[r, :] + gate_v * mix_ref[r, :].astype(F32))
            x1 = xhat * g + b
            x1_ref[r, :] = x1
            h_ref[r, :] = (x1 * sc1 + sh).astype(BF16)

    return pl.pallas_call(
        body, name="res_ln1", grid=(n // tb,),
        in_specs=[_row_spec(tb, d)] * 2 + [_par_spec(d)] * 5,
        out_specs=[_row_spec(tb, d), _row_spec(tb, d)],
        out_shape=[jax.ShapeDtypeStruct((n, d), F32), jax.ShapeDtypeStruct((n, d), BF16)],
        compiler_params=_cp(("parallel",)),
    )(xres, mix, gate, g, b, sh, sc)


def _final_ln_loss(x1, ffn, gate, g, b, target, tb):
    n, d = x1.shape

    def body(x1_ref, ffn_ref, gate_ref, g_ref, b_ref, t_ref, dffn_ref, dr_ref, acc_ref):
        i = pl.program_id(0)

        @pl.when(i == 0)
        def _():
            acc_ref[...] = jnp.zeros_like(acc_ref)

        gate_v, g, b = gate_ref[...], g_ref[...], b_ref[...]
        parts = [jnp.zeros((8, d), F32)] * 4
        for r in _row_tiles(tb):
            ffn = ffn_ref[r, :].astype(F32)
            xhat, rstd = _ln_stats(ALPHA * x1_ref[r, :] + gate_v * ffn)
            err = xhat * g + b - t_ref[r, :]
            dx2 = err * (1.0 / d)
            dr = _ln_bwd(dx2 * g, xhat, rstd)
            dr_ref[r, :] = dr
            dffn_ref[r, :] = (gate_v * dr).astype(BF16)
            terms = (dx2 * xhat, dx2, dr * ffn, err * err)
            parts = [p + _fold8(t) for p, t in zip(parts, terms)]
        for j, p in enumerate(parts):
            acc_ref[j:j + 1, :] += jnp.sum(p, axis=0, keepdims=True)

    return pl.pallas_call(
        body, name="final_ln_loss", grid=(n // tb,),
        in_specs=[_row_spec(tb, d)] * 2 + [_par_spec(d)] * 3 + [_row_spec(tb, d)],
        out_specs=[_row_spec(tb, d), _row_spec(tb, d), _acc_spec(d)],
        out_shape=[jax.ShapeDtypeStruct((n, d), BF16), jax.ShapeDtypeStruct((n, d), F32),
                   jax.ShapeDtypeStruct((8, d), F32)],
        compiler_params=_cp(("arbitrary",)),
    )(x1, ffn, gate, g, b, target)


def _bwd_ln1(dr2, dh2, x1, x0, mix, gate, sc2, g, tb):
    n, d = x1.shape

    def body(dr2_ref, dh2_ref, x1_ref, x0_ref, mix_ref, gate_ref, sc_ref, g_ref, dmix_ref, dr1_ref, acc_ref):
        i = pl.program_id(0)

        @pl.when(i == 0)
        def _():
            acc_ref[...] = jnp.zeros_like(acc_ref)

        gate_v, g, sc1 = gate_ref[...], g_ref[...], 1.0 + sc_ref[...]
        parts = [jnp.zeros((8, d), F32)] * 5
        for r in _row_tiles(tb):
            dh2 = dh2_ref[r, :].astype(F32)
            mix = mix_ref[r, :].astype(F32)
            dx1 = ALPHA * dr2_ref[r, :] + dh2 * sc1
            xhat, rstd = _ln_stats(ALPHA * x0_ref[r, :] + gate_v * mix)
            dr1 = _ln_bwd(dx1 * g, xhat, rstd)
            dr1_ref[r, :] = dr1
            dmix_ref[r, :] = (gate_v * dr1).astype(BF16)
            terms = (dh2 * x1_ref[r, :], dh2, dx1 * xhat, dx1, dr1 * mix)
            parts = [p + _fold8(t) for p, t in zip(parts, terms)]
        for j, p in enumerate(parts):
            acc_ref[j:j + 1, :] += jnp.sum(p, axis=0, keepdims=True)

    return pl.pallas_call(
        body, name="bwd_ln1", grid=(n // tb,),
        in_specs=[_row_spec(tb, d)] * 5 + [_par_spec(d)] * 3,
        out_specs=[_row_spec(tb, d), _row_spec(tb, d), _acc_spec(d)],
        out_shape=[jax.ShapeDtypeStruct((n, d), BF16), jax.ShapeDtypeStruct((n, d), F32),
                   jax.ShapeDtypeStruct((8, d), F32)],
        compiler_params=_cp(("arbitrary",)),
    )(dr2, dh2, x1, x0, mix, gate, sc2, g)


def _bwd_ln0(dres, dh, x, g, b, sc, tb, name):
    n, d = x.shape
    has_res = dres is not None

    def body(*refs):
        if has_res:
            dres_ref, dh_ref, x_ref, g_ref, b_ref, sc_ref, dx_ref, acc_ref = refs
        else:
            dh_ref, x_ref, g_ref, b_ref, sc_ref, dx_ref, acc_ref = refs
        i = pl.program_id(0)

        @pl.when(i == 0)
        def _():
            acc_ref[...] = jnp.zeros_like(acc_ref)

        g, b, sc1 = g_ref[...], b_ref[...], 1.0 + sc_ref[...]
        parts = [jnp.zeros((8, d), F32)] * 4
        for r in _row_tiles(tb):
            dh = dh_ref[r, :].astype(F32)
            xhat, rstd = _ln_stats(x_ref[r, :])
            x0 = xhat * g + b
            dx0 = dh * sc1
            if has_res:
                dx0 = dx0 + ALPHA * dres_ref[r, :]
            dx_ref[r, :] = _ln_bwd(dx0 * g, xhat, rstd)
            terms = (dh * x0, dh, dx0 * xhat, dx0)
            parts = [p + _fold8(t) for p, t in zip(parts, terms)]
        for j, p in enumerate(parts):
            acc_ref[j:j + 1, :] += jnp.sum(p, axis=0, keepdims=True)

    ins = ([dres] if has_res else []) + [dh, x, g, b, sc]
    return pl.pallas_call(
        body, name=name, grid=(n // tb,),
        in_specs=[_row_spec(tb, d)] * (3 if has_res else 2) + [_par_spec(d)] * 3,
        out_specs=[_row_spec(tb, d), _acc_spec(d)],
        out_shape=[jax.ShapeDtypeStruct((n, d), F32), jax.ShapeDtypeStruct((8, d), F32)],
        compiler_params=_cp(("arbitrary",)),
    )(*ins)


def _matmul_nn(pairs, out_dtype, tm, tn, name):
    m = pairs[0][0].shape[0]
    n = pairs[0][1].shape[1]
    tm = min(tm, m)
    tn = min(tn, n)
    npair = len(pairs)

    def body(*refs):
        o_ref = refs[-1]
        acc = None
        for p in range(npair):
            t = _dot(refs[2 * p][...].astype(BF16), refs[2 * p + 1][...])
            acc = t if acc is None else acc + t
        o_ref[...] = acc.astype(out_dtype)

    in_specs, args = [], []
    for a, b in pairs:
        k = a.shape[1]
        in_specs += [pl.BlockSpec((tm, k), lambda i, j: (i, 0)), pl.BlockSpec((k, tn), lambda i, j: (0, j))]
        args += [a, b]
    return pl.pallas_call(
        body, name=name, grid=(m // tm, n // tn),
        in_specs=in_specs,
        out_specs=pl.BlockSpec((tm, tn), lambda i, j: (i, j)),
        out_shape=jax.ShapeDtypeStruct((m, n), out_dtype),
        compiler_params=_cp(("parallel", "arbitrary")),
    )(*args)


def _matmul_tn(a, g, tm, tn, name):
    m, k = a.shape
    n = g.shape[1]
    tm = min(tm, m)
    tn = min(tn, n)

    def body(a_ref, g_ref, o_ref):
        i = pl.program_id(1)

        @pl.when(i == 0)
        def _():
            o_ref[...] = jnp.zeros_like(o_ref)

        o_ref[...] += _dot_tn(a_ref[...].astype(BF16), g_ref[...].astype(BF16))

    return pl.pallas_call(
        body, name=name, grid=(n // tn, m // tm),
        in_specs=[pl.BlockSpec((tm, k), lambda j, i: (i, 0)), pl.BlockSpec((tm, tn), lambda j, i: (i, j))],
        out_specs=pl.BlockSpec((k, tn), lambda j, i: (0, j)),
        out_shape=jax.ShapeDtypeStruct((k, n), F32),
        compiler_params=_cp(("parallel", "arbitrary")),
    )(a, g)


def _matmul_nn_multi(a, bs, out_dtypes, tm, name):
    m, k = a.shape
    tm = min(tm, m)
    nb = len(bs)

    def body(a_ref, *refs):
        av = a_ref[...].astype(BF16)
        for j in range(nb):
            refs[nb + j][...] = _dot(av, refs[j][...]).astype(out_dtypes[j])

    return pl.pallas_call(
        body, name=name, grid=(m // tm,),
        in_specs=[pl.BlockSpec((tm, k), lambda i: (i, 0))] + [pl.BlockSpec(b.shape, lambda i: (0, 0)) for b in bs],
        out_specs=[pl.BlockSpec((tm, b.shape[1]), lambda i: (i, 0)) for b in bs],
        out_shape=[jax.ShapeDtypeStruct((m, b.shape[1]), dt) for b, dt in zip(bs, out_dtypes)],
        compiler_params=_cp(("parallel",)),
    )(a, *bs)


def _matmul_tn_multi(a, gs, tm, tns, name):
    m, k = a.shape
    tm = min(tm, m)
    ng = len(gs)
    nj = gs[0].shape[1] // tns[0]
    assert all(g.shape[1] // t == nj and g.shape[1] % t == 0 for g, t in zip(gs, tns))

    def body(a_ref, *refs):
        i = pl.program_id(1)

        @pl.when(i == 0)
        def _():
            for j in range(ng):
                refs[ng + j][...] = jnp.zeros_like(refs[ng + j])

        av = a_ref[...].astype(BF16)
        for j in range(ng):
            refs[ng + j][...] += _dot_tn(av, refs[j][...].astype(BF16))

    return pl.pallas_call(
        body, name=name, grid=(nj, m // tm),
        in_specs=[pl.BlockSpec((tm, k), lambda j, i: (i, 0))] +
                 [pl.BlockSpec((tm, t), lambda j, i: (i, j)) for t in tns],
        out_specs=[pl.BlockSpec((k, t), lambda j, i: (0, j)) for t in tns],
        out_shape=[jax.ShapeDtypeStruct((k, g.shape[1]), F32) for g in gs],
        compiler_params=_cp(("parallel", "arbitrary")),
    )(a, *gs)


def _swiglu_fwd(h, wg, wu, tm, tn):
    m, k = h.shape
    n = wg.shape[1]
    tm = min(tm, m)

    def body(h_ref, wg_ref, wu_ref, gate_ref, up_ref, hmid_ref):
        hv = h_ref[...]
        gate = _dot(hv, wg_ref[...])
        up = _dot(hv, wu_ref[...])
        gate_ref[...] = gate.astype(BF16)
        up_ref[...] = up.astype(BF16)
        hmid_ref[...] = (_silu(gate) * up).astype(BF16)

    blk = pl.BlockSpec((tm, tn), lambda i, j: (i, j))
    wspec = pl.BlockSpec((k, tn), lambda i, j: (0, j))
    return pl.pallas_call(
        body, name="swiglu_fwd", grid=(m // tm, n // tn),
        in_specs=[pl.BlockSpec((tm, k), lambda i, j: (i, 0)), wspec, wspec],
        out_specs=[blk, blk, blk],
        out_shape=[jax.ShapeDtypeStruct((m, n), BF16), jax.ShapeDtypeStruct((m, n), BF16),
                   jax.ShapeDtypeStruct((m, n), BF16)],
        compiler_params=_cp(("parallel", "arbitrary")),
    )(h, wg, wu)


def _swiglu_bwd(dffn, wdt, gate, up, tm, tn):
    m, k = dffn.shape
    n = wdt.shape[1]
    tm = min(tm, m)

    def body(d_ref, w_ref, gate_ref, up_ref, dg_ref, du_ref):
        dh = _dot(d_ref[...], w_ref[...])
        gate = gate_ref[...].astype(F32)
        sg = _sigmoid(gate)
        dg_ref[...] = (dh * up_ref[...].astype(F32) * (sg * (1.0 + gate * (1.0 - sg)))).astype(BF16)
        du_ref[...] = (dh * (gate * sg)).astype(BF16)

    blk = pl.BlockSpec((tm, tn), lambda i, j: (i, j))
    return pl.pallas_call(
        body, name="swiglu_bwd", grid=(m // tm, n // tn),
        in_specs=[pl.BlockSpec((tm, k), lambda i, j: (i, 0)), pl.BlockSpec((k, tn), lambda i, j: (0, j)), blk, blk],
        out_specs=[blk, blk],
        out_shape=[jax.ShapeDtypeStruct((m, n), BF16), jax.ShapeDtypeStruct((m, n), BF16)],
        compiler_params=_cp(("parallel", "arbitrary")),
    )(dffn, wdt, gate, up)


def _halo_specs(tb, width, nrows):
    r8 = tb // 8
    last = nrows // 8 - 1
    prev = pl.BlockSpec((8, width), lambda i: (jnp.maximum(i * r8 - 1, 0), 0))
    nxt = pl.BlockSpec((8, width), lambda i: (jnp.minimum((i + 1) * r8, last), 0))
    return prev, nxt


CONV_SUB = 32


def _halo_scratch():
    return [pltpu.VMEM((CONV_SUB + 16, D_XBC), F32), pltpu.VMEM((CONV_SUB + 16, D_XBC), F32)]


def _shifted_rows(prev_ref, cur_ref, next_ref, top, bot, tb, i, nb):
    sub = CONV_SUB
    nsub = tb // sub
    assert nsub >= 2
    top[0:8, :] = prev_ref[...] * (i > 0).astype(F32)
    top[8:sub + 16, :] = cur_ref[0:sub + 8, :]
    bot[0:sub + 8, :] = cur_ref[tb - sub - 8:tb, :]
    bot[sub + 8:sub + 16, :] = next_ref[...] * (i < nb - 1).astype(F32)

    def rows(s, o):
        if s == 0:
            return top[8 + o:8 + o + sub, :]
        if s == nsub - 1:
            return bot[8 + o:8 + o + sub, :]
        return cur_ref[s * sub + o:(s + 1) * sub + o, :]

    return rows


def _conv_fwd(xd, w8, b, tb, name):
    n = xd.shape[0]
    tb = min(tb, n)
    nb = n // tb
    prev, nxt = _halo_specs(tb, D_XBC, n)

    def body(p_ref, c_ref, n_ref, w_ref, b_ref, u_ref, a_ref, top, bot):
        i = pl.program_id(0)
        rows = _shifted_rows(p_ref, c_ref, n_ref, top, bot, tb, i, nb)
        w = [w_ref[k:k + 1, :] for k in range(D_CONV)]
        bias = jnp.broadcast_to(b_ref[...], (CONV_SUB, D_XBC))
        for s in range(tb // CONV_SUB):
            acc = bias
            for k in range(D_CONV):
                acc = acc + w[k] * rows(s, k - 2)
            u_ref[s * CONV_SUB:(s + 1) * CONV_SUB, :] = acc.astype(BF16)
            a_ref[s * CONV_SUB:(s + 1) * CONV_SUB, :] = _silu(acc).astype(BF16)

    return pl.pallas_call(
        body, name=name, grid=(nb,),
        in_specs=[prev, pl.BlockSpec((tb, D_XBC), lambda i: (i, 0)), nxt,
                  pl.BlockSpec((8, D_XBC), lambda i: (0, 0)), _par_spec(D_XBC)],
        out_specs=[_row_spec(tb, D_XBC), _row_spec(tb, D_XBC)],
        out_shape=[jax.ShapeDtypeStruct((n, D_XBC), BF16), jax.ShapeDtypeStruct((n, D_XBC), BF16)],
        scratch_shapes=_halo_scratch(),
        compiler_params=_cp(("parallel",)),
    )(xd, xd, xd, w8, b)


def _conv_bwd_a(dxs, dy, dskip_e, dbc, u, tb, name):
    n = u.shape[0]
    tb = min(tb, n)

    def body(dxs_ref, dy_ref, sk_ref, dbc_ref, u_ref, du_ref, acc_ref):
        i = pl.program_id(0)

        @pl.when(i == 0)
        def _():
            acc_ref[...] = jnp.zeros_like(acc_ref)

        sk = sk_ref[...]
        part = jnp.zeros((8, D_XBC), F32)
        for r in _row_tiles(tb):
            gx = dxs_ref[0, r, :].astype(F32) + dxs_ref[1, r, :].astype(F32) + dy_ref[r, :].astype(F32) * sk
            gbc = dbc_ref[0, r, :] + dbc_ref[1, r, :]
            du = jnp.concatenate([gx, gbc], axis=1) * _dsilu(u_ref[r, :].astype(F32))
            du_ref[r, :] = du
            part = part + _fold8(du)
        acc_ref[0:1, :] += jnp.sum(part, axis=0, keepdims=True)

    return pl.pallas_call(
        body, name=name, grid=(n // tb,),
        in_specs=[pl.BlockSpec((2, tb, D_MODEL), lambda i: (0, i, 0)), _row_spec(tb, D_MODEL), _par_spec(D_MODEL),
                  pl.BlockSpec((2, tb, 2 * D_STATE), lambda i: (0, i, 0)), _row_spec(tb, D_XBC)],
        out_specs=[_row_spec(tb, D_XBC), _acc_spec(D_XBC)],
        out_shape=[jax.ShapeDtypeStruct((n, D_XBC), F32), jax.ShapeDtypeStruct((8, D_XBC), F32)],
        compiler_params=_cp(("arbitrary",)),
    )(dxs, dy, dskip_e, dbc, u)


def _conv_bwd_b(du, xd, ddt, w8, tb, name):
    n = du.shape[0]
    tb = min(tb, n)
    nb = n // tb
    prev, nxt = _halo_specs(tb, D_XBC, n)

    def body(dp_ref, dc_ref, dn_ref, xp_ref, xc_ref, xn_ref, ddt_ref, w_ref, dxd_ref, acc_ref, dtop, dbot, xtop, xbot):
        i = pl.program_id(0)

        @pl.when(i == 0)
        def _():
            acc_ref[...] = jnp.zeros_like(acc_ref)

        sub = CONV_SUB
        nsub = tb // sub
        du_rows = _shifted_rows(dp_ref, dc_ref, dn_ref, dtop, dbot, tb, i, nb)
        x_rows = _shifted_rows(xp_ref, xc_ref, xn_ref, xtop, xbot, tb, i, nb)
        w = [w_ref[k:k + 1, :] for k in range(D_CONV)]
        for s in range(nsub):
            acc = w[0] * du_rows(s, 2)
            for k in range(1, D_CONV):
                acc = acc + w[k] * du_rows(s, 2 - k)
            dxd_ref[s * sub:(s + 1) * sub, 0:D_XBC] = acc.astype(BF16)
        for k in range(D_CONV):
            part = jnp.zeros((8, D_XBC), F32)
            for s in range(nsub):
                prod = dc_ref[s * sub:(s + 1) * sub, :] * x_rows(s, k - 2)
                for r in range(0, sub, 8):
                    part = part + prod[r:r + 8, :]
            acc_ref[k:k + 1, :] += jnp.sum(part, axis=0, keepdims=True)
        ddt = ddt_ref[0] + pltpu.roll(ddt_ref[1], SSD_HEADS, 1)
        dxd_ref[:, D_XBC:D_XD] = ddt.astype(BF16)

    cur = pl.BlockSpec((tb, D_XBC), lambda i: (i, 0))
    return pl.pallas_call(
        body, name=name, grid=(nb,),
        in_specs=[prev, cur, nxt, prev, cur, nxt,
                  pl.BlockSpec((2, tb, 128), lambda i: (0, i, 0)), pl.BlockSpec((8, D_XBC), lambda i: (0, 0))],
        out_specs=[_row_spec(tb, D_XD), _acc_spec(D_XBC)],
        out_shape=[jax.ShapeDtypeStruct((n, D_XD), BF16), jax.ShapeDtypeStruct((8, D_XBC), F32)],
        scratch_shapes=_halo_scratch() + _halo_scratch(),
        compiler_params=_cp(("arbitrary",)),
    )(du, du, du, xd, xd, xd, ddt, w8)


def _ssd_chunk_index(nc, reverse):
    def idx(d, k):
        kk = (nc - 1 - k) if reverse else k
        return kk + d * (nc - 1 - 2 * kk)
    return idx


def _ssd_prologue(d, u_ref, xd_ref, bias_ref, a_ref, r_ref):
    q = CHUNK
    xbc = u_ref[...].astype(F32)
    xs = xbc[:, 0:D_MODEL]
    bm = xbc[:, D_MODEL:D_MODEL + D_STATE]
    cm = xbc[:, D_MODEL + D_STATE:D_XBC]
    row = lax.broadcasted_iota(jnp.int32, (q, q), 0)
    col = lax.broadcasted_iota(jnp.int32, (q, q), 1)
    sgn = 1 - 2 * d
    mask = ((row - col) * sgn) >= 0
    mask_t = ((row - col) * sgn) <= 0
    xdv = xd_ref[...]
    dtraw = jnp.where(d == 0, xdv, pltpu.roll(xdv, 128 - SSD_HEADS, 1)) + bias_ref[...]
    head_lane = col < SSD_HEADS
    dt = jnp.where(head_lane, _softplus(dtraw), 0.0)
    a = a_ref[...]
    tri = jnp.where(mask, 1.0, 0.0).astype(BF16)
    acum = _dot_sel_l(tri, dt * a)
    rexp = r_ref[...]
    alast = jnp.where(d == 0, acum[q - 1:q, :], acum[0:1, :])
    e16 = jnp.exp(acum)
    dend16 = jnp.exp(alast - acum)
    wend16 = dend16 * dt
    e = _dot_sel_r(e16, rexp, n=1)
    wend_e = _dot_sel_r(wend16, rexp, n=1)
    elast_e = _dot_sel_r(jnp.broadcast_to(jnp.exp(alast), (8, 128)), rexp, n=2)[0:1, :]
    g = _dot_nt(cm.astype(BF16), bm.astype(BF16))
    return dict(xs=xs, bm=bm, cm=cm, mask=mask, mask_t=mask_t, dtraw=dtraw, head_lane=head_lane, dt=dt, a=a,
                acum=acum, acum_t=acum.T, dt_t=dt.T, e16=e16, dend16=dend16, wend16=wend16, e=e, wend_e=wend_e,
                elast_e=elast_e, g=g, col=col, row=row)


def _ssd_head_mats(p, h):
    seg = p["acum"][:, h:h + 1] - p["acum_t"][h:h + 1, :]
    lm = jnp.exp(jnp.where(p["mask"], seg, -jnp.inf))
    gl = p["g"] * lm
    s = gl * p["dt_t"][h:h + 1, :]
    return lm, gl, s


def _ssd_fwd(u, xd, bias2, a2, rexp, h0, name, gather=()):
    n = u.shape[0]
    nc = n // CHUNK
    q = CHUNK
    cidx = _ssd_chunk_index(nc, reverse=False)
    ng = len(gather)

    def body(u_ref, xd_ref, bias_ref, a_ref, r_ref, h0_ref, *rest):
        g_ins, (y_ref, hp_ref, hf_ref), rest = rest[:ng], rest[ng:ng + 3], rest[ng + 3:]
        g_outs, st, sems = rest[:ng], rest[ng], rest[ng + 1:]
        d = pl.program_id(0)
        k = pl.program_id(1)
        if ng:
            g_start, g_finish = _gather_steps(g_ins, g_outs, *sems)
            pl.when((d == 0) & (k == 0))(g_start)

        @pl.when(k == 0)
        def _():
            st[...] = h0_ref[...]

        p = _ssd_prologue(d, u_ref, xd_ref, bias_ref, a_ref, r_ref)
        stv = st[...]
        st_bf = stv.astype(BF16)
        hp_ref[...] = st_bf
        xs = p["xs"]
        lane128 = p["col"]
        y_off = _dot(p["cm"].astype(BF16), st_bf) * p["e"]
        for pb in range(SSD_HEADS // 2):
            _, _, s0 = _ssd_head_mats(p, 2 * pb)
            _, _, s1 = _ssd_head_mats(p, 2 * pb + 1)
            xp = xs[:, pb * 128:(pb + 1) * 128]
            rhs = jnp.concatenate([jnp.where(lane128 < HEAD_DIM, xp, 0.0), jnp.where(lane128 >= HEAD_DIM, xp, 0.0)],
                                  axis=0).astype(BF16)
            lhs = jnp.concatenate([s0, s1], axis=1).astype(BF16)
            y_ref[:, pb * 128:(pb + 1) * 128] = (_dot(lhs, rhs) + y_off[:, pb * 128:(pb + 1) * 128]).astype(BF16)
        xw = (xs * p["wend_e"]).astype(BF16)
        new = stv * p["elast_e"] + _dot(p["bm"].T.astype(BF16), xw)
        st[...] = new
        hf_ref[...] = new
        if ng:
            pl.when((d == 1) & (k == nc - 1))(g_finish)

    nsem = _GATHER_SEMS * ng
    return pl.pallas_call(
        body, name=name, grid=(2, nc),
        in_specs=[pl.BlockSpec((q, D_XBC), lambda d, k: (cidx(d, k), 0)),
                  pl.BlockSpec((q, 128), lambda d, k: (cidx(d, k), D_XBC // 128)),
                  pl.BlockSpec((None, 1, 128), lambda d, k: (d, 0, 0)),
                  pl.BlockSpec((None, 1, 128), lambda d, k: (d, 0, 0)),
                  pl.BlockSpec((128, D_MODEL), lambda d, k: (0, 0)),
                  pl.BlockSpec((None, D_STATE, D_MODEL), lambda d, k: (d, 0, 0))] + [_ANY] * ng,
        out_specs=[pl.BlockSpec((None, q, D_MODEL), lambda d, k: (d, cidx(d, k), 0)),
                   pl.BlockSpec((None, None, D_STATE, D_MODEL), lambda d, k: (d, cidx(d, k), 0, 0)),
                   pl.BlockSpec((None, D_STATE, D_MODEL), lambda d, k: (d, 0, 0))] + [_ANY] * ng,
        out_shape=[jax.ShapeDtypeStruct((2, n, D_MODEL), BF16),
                   jax.ShapeDtypeStruct((2, nc, D_STATE, D_MODEL), BF16),
                   jax.ShapeDtypeStruct((2, D_STATE, D_MODEL), F32)] +
                  [jax.ShapeDtypeStruct((4,) + t.shape, t.dtype) for t in gather],
        scratch_shapes=[pltpu.VMEM((D_STATE, D_MODEL), F32)] +
                       ([pltpu.SemaphoreType.DMA((nsem,)), pltpu.SemaphoreType.DMA((nsem,))] if ng else []),
        compiler_params=_cp(("arbitrary", "arbitrary")),
    )(u, xd, bias2, a2, rexp, h0, *gather)


def _ssd_bwd(u, xd, bias2, a2, rexp, rexp_t, dy, hprev, lam0, name, exchange=()):
    n = u.shape[0]
    nc = n // CHUNK
    q = CHUNK
    cidx = _ssd_chunk_index(nc, reverse=True)
    ne = len(exchange)

    def body(u_ref, xd_ref, bias_ref, a_ref, r_ref, rt_ref, dy_ref, hp_ref, lam0_ref, *rest):
        e_ins, (dxs_ref, dbc_ref, ddt_ref, acc_ref, lamo_ref), rest = rest[:ne], rest[ne:ne + 5], rest[ne + 5:]
        e_outs, lam, sems = rest[:ne], rest[ne], rest[ne + 1:]
        d = pl.program_id(0)
        k = pl.program_id(1)
        if ne:
            e_start, e_finish = _exchange_steps(e_ins, e_outs, *sems)
            pl.when((d == 0) & (k == 0))(e_start)

        @pl.when(k == 0)
        def _():
            lam[...] = lam0_ref[...]
            acc_ref[...] = jnp.zeros_like(acc_ref)

        rexp_t = rt_ref[...]

        def hsum(t):
            return _dot_sel_r(t, rexp_t, n=1)

        p = _ssd_prologue(d, u_ref, xd_ref, bias_ref, a_ref, r_ref)
        xs, bm, cm = p["xs"], p["bm"], p["cm"]
        bm_bf, cm_bf = bm.astype(BF16), cm.astype(BF16)
        lamn = lam[...]
        lamn_bf = lamn.astype(BF16)
        stp = hp_ref[...]
        dyv = dy_ref[...].astype(F32)
        lane128 = p["col"]

        wend_e = p["wend_e"]
        cs = _dot(cm_bf, stp)
        dye_bf = (dyv * p["e"]).astype(BF16)
        dc_off = _dot_nt(dye_bf, stp)
        v = _dot(bm_bf, lamn_bf)
        xw_bf = (xs * wend_e).astype(BF16)
        db_off = _dot_nt(xw_bf, lamn_bf)
        elast_e = p["elast_e"]
        dlast_e = jnp.sum(stp.astype(F32) * lamn, axis=0, keepdims=True) * elast_e
        lam_new = lamn * elast_e + _dot(cm.T.astype(BF16), dye_bf)
        lam[...] = lam_new
        lamo_ref[...] = lam_new

        hs_vx = hsum(v * xs)
        om = p["wend16"] * hs_vx
        x1 = p["e16"] * hsum(dyv * cs) - om
        x2 = p["dend16"] * hs_vx
        x3 = jnp.sum(om, axis=0, keepdims=True) + _dot_sel_r(jnp.broadcast_to(dlast_e, (8, D_MODEL)), rexp_t, n=2)[0:1, :]

        sub16 = lax.broadcasted_iota(jnp.int32, (SSD_HEADS, q), 0)
        rs = jnp.zeros((q, 128), F32)
        cs_m = jnp.zeros((SSD_HEADS, q), F32)
        dt_m = jnp.zeros((SSD_HEADS, q), F32)
        dg = jnp.zeros((q, q), F32)
        for pb in range(SSD_HEADS // 2):
            xp_bf = xs[:, pb * 128:(pb + 1) * 128].astype(BF16)
            dyp = dyv[:, pb * 128:(pb + 1) * 128]
            dxs_pair = None
            for half in range(2):
                h = 2 * pb + half
                sel = (lane128 < HEAD_DIM) if half == 0 else (lane128 >= HEAD_DIM)
                dyh_bf = jnp.where(sel, dyp, 0.0).astype(BF16)
                lm, gl, s = _ssd_head_mats(p, h)
                ds = _dot_nt(dyh_bf, xp_bf)
                t = _dot_tn(s.astype(BF16), dyh_bf)
                dxs_pair = t if dxs_pair is None else dxs_pair + t
                w = ds * s
                rs = rs + jnp.sum(w, axis=1, keepdims=True) * (lane128 == h).astype(F32)
                cs_m = jnp.where(sub16 == h, jnp.sum(w, axis=0, keepdims=True), cs_m)
                dt_m = jnp.where(sub16 == h, jnp.sum(ds * gl, axis=0, keepdims=True), dt_m)
                dg = dg + ds * lm * p["dt_t"][h:h + 1, :]
            sl = slice(pb * 128, (pb + 1) * 128)
            dxs_ref[:, sl] = (dxs_pair + v[:, sl] * wend_e[:, sl]).astype(BF16)

        def to_lanes(m16):
            return jnp.concatenate([m16, jnp.zeros((128 - SSD_HEADS, q), F32)], axis=0).T

        last = jnp.where(d == 0, q - 1, 0)
        dacum = rs - to_lanes(cs_m) + x1 + jnp.where(p["row"] == last, x3[0:1, :], 0.0)
        tri_t = jnp.where(p["mask_t"], 1.0, 0.0).astype(BF16)
        ddta = _dot_sel_l(tri_t, dacum)
        dt = p["dt"]
        a = p["a"]
        ddt = to_lanes(dt_m) + x2 + a * ddta
        ddtraw = jnp.where(p["head_lane"], ddt * _sigmoid(p["dtraw"]), 0.0)
        ddt_ref[...] = ddtraw
        acc_ref[0:1, :] += jnp.sum(ddtraw, axis=0, keepdims=True)
        acc_ref[1:2, :] += jnp.sum(dt * ddta, axis=0, keepdims=True) * a

        dg_bf = dg.astype(BF16)
        dbc_ref[:, 0:D_STATE] = _dot_tn(dg_bf, cm_bf) + db_off
        dbc_ref[:, D_STATE:2 * D_STATE] = _dot(dg_bf, bm_bf) + dc_off
        if ne:
            pl.when((d == 1) & (k == nc - 1))(e_finish)

    cblk = lambda d, k: (cidx(d, k), 0)
    return pl.pallas_call(
        body, name=name, grid=(2, nc),
        in_specs=[pl.BlockSpec((q, D_XBC), cblk),
                  pl.BlockSpec((q, 128), lambda d, k: (cidx(d, k), D_XBC // 128)),
                  pl.BlockSpec((None, 1, 128), lambda d, k: (d, 0, 0)),
                  pl.BlockSpec((None, 1, 128), lambda d, k: (d, 0, 0)),
                  pl.BlockSpec((128, D_MODEL), lambda d, k: (0, 0)),
                  pl.BlockSpec((D_MODEL, 128), lambda d, k: (0, 0)),
                  pl.BlockSpec((q, D_MODEL), cblk),
                  pl.BlockSpec((None, None, D_STATE, D_MODEL), lambda d, k: (d, cidx(d, k), 0, 0)),
                  pl.BlockSpec((None, D_STATE, D_MODEL), lambda d, k: (d, 0, 0))] + [_ANY] * ne,
        out_specs=[pl.BlockSpec((None, q, D_MODEL), lambda d, k: (d, cidx(d, k), 0)),
                   pl.BlockSpec((None, q, 2 * D_STATE), lambda d, k: (d, cidx(d, k), 0)),
                   pl.BlockSpec((None, q, 128), lambda d, k: (d, cidx(d, k), 0)),
                   pl.BlockSpec((None, 8, 128), lambda d, k: (d, 0, 0)),
                   pl.BlockSpec((None, D_STATE, D_MODEL), lambda d, k: (d, 0, 0))] + [_ANY] * ne,
        out_shape=[jax.ShapeDtypeStruct((2, n, D_MODEL), BF16),
                   jax.ShapeDtypeStruct((2, n, 2 * D_STATE), F32),
                   jax.ShapeDtypeStruct((2, n, 128), F32),
                   jax.ShapeDtypeStruct((2, 8, 128), F32),
                   jax.ShapeDtypeStruct((2, D_STATE, D_MODEL), F32)] +
                  [jax.ShapeDtypeStruct(t.shape, t.dtype) for t in exchange],
        scratch_shapes=[pltpu.VMEM((D_STATE, D_MODEL), F32)] +
                       ([pltpu.SemaphoreType.DMA((3 * ne,)), pltpu.SemaphoreType.DMA((3 * ne,)),
                         pltpu.SemaphoreType.DMA((ne,))] if ne else []),
        compiler_params=_cp(("arbitrary", "arbitrary")),
    )(u, xd, bias2, a2, rexp, rexp_t, dy, hprev, lam0, *exchange)


def _merge_fwd(y, u, z, dskip_e, gn, tb):
    n = z.shape[0]

    def body(y_ref, u_ref, z_ref, sk_ref, gn_ref, o_ref):
        sk, gnv = sk_ref[...], gn_ref[...]
        for r in _row_tiles(tb):
            ys = y_ref[0, r, :].astype(F32) + y_ref[1, r, :].astype(F32) + sk * _silu(u_ref[r, :].astype(F32))
            gated = ys * _silu(z_ref[r, :].astype(F32))
            rstd = lax.rsqrt(jnp.mean(gated * gated, axis=-1, keepdims=True) + LN_EPS)
            o_ref[r, :] = (gated * rstd * gnv).astype(BF16)

    return pl.pallas_call(
        body, name="merge_fwd", grid=(n // tb,),
        in_specs=[pl.BlockSpec((2, tb, D_MODEL), lambda i: (0, i, 0)), pl.BlockSpec((tb, D_MODEL), lambda i: (i, 0)),
                  _row_spec(tb, D_MODEL), _par_spec(D_MODEL), _par_spec(D_MODEL)],
        out_specs=_row_spec(tb, D_MODEL),
        out_shape=jax.ShapeDtypeStruct((n, D_MODEL), BF16),
        compiler_params=_cp(("parallel",)),
    )(y, u, z, dskip_e, gn)


def _merge_bwd(dyn, y, u, z, dskip_e, gn, tb):
    n = z.shape[0]

    def body(dyn_ref, y_ref, u_ref, z_ref, sk_ref, gn_ref, dy_ref, dz_ref, acc_ref):
        i = pl.program_id(0)

        @pl.when(i == 0)
        def _():
            acc_ref[...] = jnp.zeros_like(acc_ref)

        sk, gnv = sk_ref[...], gn_ref[...]
        part0 = jnp.zeros((8, D_MODEL), F32)
        part1 = jnp.zeros((8, D_MODEL), F32)
        for s in range(tb // ROW_SUB):
            r = slice(s * ROW_SUB, (s + 1) * ROW_SUB)
            xs = _silu(u_ref[r, :].astype(F32))
            zv = z_ref[r, :].astype(F32)
            sz = _sigmoid(zv)
            ys = y_ref[0, r, :].astype(F32) + y_ref[1, r, :].astype(F32) + sk * xs
            gated = ys * (zv * sz)
            rstd = lax.rsqrt(jnp.mean(gated * gated, axis=-1, keepdims=True) + LN_EPS)
            ghat = gated * rstd
            dyn_v = dyn_ref[r, :].astype(F32)
            t = dyn_v * gnv
            dgated = rstd * (t - ghat * jnp.mean(t * ghat, axis=-1, keepdims=True))
            dys = dgated * (zv * sz)
            dy_ref[r, :] = dys.astype(BF16)
            dz_ref[r, :] = (dgated * ys * (sz * (1.0 + zv * (1.0 - sz)))).astype(BF16)
            part0 = part0 + _fold8(dyn_v * ghat)
            part1 = part1 + _fold8(dys * xs)
        acc_ref[0:1, :] += jnp.sum(part0, axis=0, keepdims=True)
        acc_ref[1:2, :] += jnp.sum(part1, axis=0, keepdims=True)

    return pl.pallas_call(
        body, name="merge_bwd", grid=(n // tb,),
        in_specs=[_row_spec(tb, D_MODEL), pl.BlockSpec((2, tb, D_MODEL), lambda i: (0, i, 0)),
                  pl.BlockSpec((tb, D_MODEL), lambda i: (i, 0)), _row_spec(tb, D_MODEL),
                  _par_spec(D_MODEL), _par_spec(D_MODEL)],
        out_specs=[_row_spec(tb, D_MODEL), _row_spec(tb, D_MODEL), _acc_spec(D_MODEL)],
        out_shape=[jax.ShapeDtypeStruct((n, D_MODEL), BF16), jax.ShapeDtypeStruct((n, D_MODEL), BF16),
                   jax.ShapeDtypeStruct((8, D_MODEL), F32)],
        compiler_params=_cp(("arbitrary",)),
    )(dyn, y, u, z, dskip_e, gn)


def _pool_consts(transpose):
    tb = POOL_TB
    t = jnp.arange(tb)
    s = jnp.arange(3 * tb)
    rl, cl = t // GRID_W, t % GRID_W
    rs_, cs_ = s // GRID_W - tb // GRID_W, s % GRID_W
    s2 = jnp.arange(tb)
    rl2, cl2 = s2 // GRID_W, s2 % GRID_W
    brow, bcol = [], []
    for w in POOL_WINDOWS:
        lo, hi = -(w // 2), w - w // 2
        if transpose:
            lo, hi = -hi + 1, -lo + 1
        dr = rs_[None, :] - rl[:, None]
        before, after = _pool_halo(w, transpose)
        full = ((cs_[None, :] == cl[:, None]) & (dr >= lo) & (dr < hi)).astype(BF16)
        brow.append(full[:, tb - before:2 * tb + after])
        dc = cl2[None, :] - cl[:, None]
        bcol.append(((rl2[None, :] == rl[:, None]) & (dc >= lo) & (dc < hi)).astype(BF16))
    return brow, jnp.stack(bcol)


def _pool_halo(w, transpose):
    lo, hi = -(w // 2), w - w // 2
    if transpose:
        lo, hi = -hi + 1, -lo + 1
    return -lo * GRID_W, (hi - 1) * GRID_W


def _pool_inv(i, g, n):
    assert GRID_W == 64
    t = i * POOL_TB + lax.broadcasted_iota(jnp.int32, (POOL_TB, 1), 0)
    r = lax.shift_right_logical(t, 6)
    col = t & (GRID_W - 1)
    w = POOL_WINDOWS[g]
    lo, hi = -(w // 2), w - w // 2
    cnt_r = jnp.minimum(r + hi, n // GRID_W) - jnp.maximum(r + lo, 0)
    cnt_c = jnp.minimum(col + hi, GRID_W) - jnp.maximum(col + lo, 0)
    return 1.0 / (cnt_r * cnt_c).astype(F32)


def _pool_box(prev_ref, cur_ref, next_ref, brow_refs, bcol_ref, g, i, nb, transpose):
    tb = POOL_TB
    sl = slice(g * POOL_DIM, (g + 1) * POOL_DIM)
    before, after = _pool_halo(POOL_WINDOWS[g], transpose)
    pieces = []
    if before:
        pieces.append((prev_ref[tb - before:tb, sl] * (i > 0).astype(prev_ref.dtype)).astype(BF16))
    pieces.append(cur_ref[:, sl].astype(BF16))
    if after:
        pieces.append((next_ref[0:after, sl] * (i < nb - 1).astype(next_ref.dtype)).astype(BF16))
    r = _dot(brow_refs[g][...], jnp.concatenate(pieces, axis=0))
    return _dot(bcol_ref[g], r.astype(BF16))


def _pool_halo_specs(n, d):
    tb = POOL_TB
    nb = n // tb
    prev = pl.BlockSpec((tb, d), lambda i: (jnp.maximum(i - 1, 0), 0))
    cur = pl.BlockSpec((tb, d), lambda i: (i, 0))
    nxt = pl.BlockSpec((tb, d), lambda i: (jnp.minimum(i + 1, nb - 1), 0))
    return prev, cur, nxt


def _pool_const_specs(brow):
    tb = POOL_TB
    return [pl.BlockSpec(b.shape, lambda i: (0, 0)) for b in brow] + [pl.BlockSpec((N_POOL, tb, tb), lambda i: (0, 0, 0))]


def _pool_fwd(up, consts, pw_bf, pscale):
    n = up.shape[0]
    tb = POOL_TB
    nb = n // tb
    brow, bcol = consts
    prev, cur, nxt = _pool_halo_specs(n, D_MODEL)

    def body(p_ref, c_ref, n_ref, *rest):
        brow_refs, (bcol_ref, pw_ref, sc_ref, o_ref, d_ref) = rest[:N_POOL], rest[N_POOL:]
        i = pl.program_id(0)
        for g in range(N_POOL):
            sl = slice(g * POOL_DIM, (g + 1) * POOL_DIM)
            box = _pool_box(p_ref, c_ref, n_ref, brow_refs, bcol_ref, g, i, nb, False)
            dd = (box * _pool_inv(i, g, n) - c_ref[:, sl].astype(F32)).astype(BF16)
            d_ref[:, sl] = dd
            o_ref[:, sl] = (_dot(dd, pw_ref[g]) * sc_ref[:, sl]).astype(BF16)

    return pl.pallas_call(
        body, name="pool_fwd", grid=(nb,),
        in_specs=[prev, cur, nxt] + _pool_const_specs(brow) +
                 [pl.BlockSpec((N_POOL, POOL_DIM, POOL_DIM), lambda i: (0, 0, 0)), _par_spec(D_MODEL)],
        out_specs=[_row_spec(tb, D_MODEL), _row_spec(tb, D_MODEL)],
        out_shape=[jax.ShapeDtypeStruct((n, D_MODEL), BF16), jax.ShapeDtypeStruct((n, D_MODEL), BF16)],
        compiler_params=_cp(("parallel",)),
    )(up, up, up, *brow, bcol, pw_bf, pscale)


def _pool_bwd_a(dp, dsave, pw_bf, pwt_bf, pscale):
    n = dp.shape[0]
    tb = POOL_TB

    def body(dp_ref, d_ref, pw_ref, pwt_ref, sc_ref, dd_ref, dds_ref, gw_ref, gs_ref):
        i = pl.program_id(0)

        @pl.when(i == 0)
        def _():
            gw_ref[...] = jnp.zeros_like(gw_ref)
            gs_ref[...] = jnp.zeros_like(gs_ref)

        for g in range(N_POOL):
            sl = slice(g * POOL_DIM, (g + 1) * POOL_DIM)
            dpv = dp_ref[:, sl].astype(F32)
            dv = d_ref[:, sl]
            dpw_bf = (dpv * sc_ref[:, sl]).astype(BF16)
            dd = _dot(dpw_bf, pwt_ref[g])
            dd_ref[:, sl] = dd.astype(BF16)
            dds_ref[:, sl] = (dd * _pool_inv(i, g, n)).astype(BF16)
            gw_ref[g] += _dot_tn(dv, dpw_bf)
            gs_ref[0:1, sl] += jnp.sum(dpv * _dot(dv, pw_ref[g]), axis=0, keepdims=True)

    wspec = pl.BlockSpec((N_POOL, POOL_DIM, POOL_DIM), lambda i: (0, 0, 0))
    return pl.pallas_call(
        body, name="pool_bwd_a", grid=(n // tb,),
        in_specs=[_row_spec(tb, D_MODEL), _row_spec(tb, D_MODEL), wspec, wspec, _par_spec(D_MODEL)],
        out_specs=[_row_spec(tb, D_MODEL), _row_spec(tb, D_MODEL), wspec, _acc_spec(D_MODEL)],
        out_shape=[jax.ShapeDtypeStruct((n, D_MODEL), BF16), jax.ShapeDtypeStruct((n, D_MODEL), BF16),
                   jax.ShapeDtypeStruct((N_POOL, POOL_DIM, POOL_DIM), F32), jax.ShapeDtypeStruct((8, D_MODEL), F32)],
        compiler_params=_cp(("arbitrary",)),
    )(dp, dsave, pw_bf, pwt_bf, pscale)


def _pool_bwd_b(dds, dd, consts_t):
    n = dd.shape[0]
    tb = POOL_TB
    nb = n // tb
    brow, bcol = consts_t
    prev, cur, nxt = _pool_halo_specs(n, D_MODEL)

    def body(p_ref, c_ref, n_ref, *rest):
        brow_refs, (bcol_ref, dd_ref, o_ref) = rest[:N_POOL], rest[N_POOL:]
        i = pl.program_id(0)
        for g in range(N_POOL):
            sl = slice(g * POOL_DIM, (g + 1) * POOL_DIM)
            box = _pool_box(p_ref, c_ref, n_ref, brow_refs, bcol_ref, g, i, nb, True)
            o_ref[:, sl] = (box - dd_ref[:, sl].astype(F32)).astype(BF16)

    return pl.pallas_call(
        body, name="pool_bwd_b", grid=(nb,),
        in_specs=[prev, cur, nxt] + _pool_const_specs(brow) + [_row_spec(tb, D_MODEL)],
        out_specs=_row_spec(tb, D_MODEL),
        out_shape=jax.ShapeDtypeStruct((n, D_MODEL), BF16),
        compiler_params=_cp(("parallel",)),
    )(dds, dds, dds, *brow, bcol, dd)


def _pair_add(slabs, recvs, core, name):
    na = len(slabs)
    hr = [t.shape[1] // 4 for t in slabs]

    def body(core_ref, *refs):
        for a in range(na):
            refs[2 * na + a][...] = (refs[a][...] + refs[na + a][...]).astype(BF16)

    own = [pl.BlockSpec((None, hr[a], slabs[a].shape[2]), lambda j, i, c_ref: (j, 2 * c_ref[0] + i, 0)) for a in range(na)]
    got = [pl.BlockSpec((None, hr[a], slabs[a].shape[2]), lambda j, i, c_ref: (j, i, 0)) for a in range(na)]
    return pl.pallas_call(
        body, name=name,
        grid_spec=pltpu.PrefetchScalarGridSpec(num_scalar_prefetch=1, grid=(4, 2), in_specs=own + got, out_specs=got),
        out_shape=[jax.ShapeDtypeStruct(r.shape, BF16) for r in recvs],
        compiler_params=_cp(("arbitrary", "arbitrary")),
    )(core, *slabs, *recvs)


def _sum4(parts, core):
    na = len(parts)
    hr = [t.shape[1] // 2 for t in parts]

    def body(core_ref, *refs):
        for a in range(na):
            p = refs[a]
            refs[na + a][...] = ((p[0].astype(F32) + p[1].astype(F32)) + p[2].astype(F32)) + p[3].astype(F32)

    return pl.pallas_call(
        body, name="reduce_g_sum",
        grid_spec=pltpu.PrefetchScalarGridSpec(
            num_scalar_prefetch=1, grid=(2,),
            in_specs=[pl.BlockSpec((4, hr[a], parts[a].shape[2]), lambda i, c_ref: (0, i, 0)) for a in range(na)],
            out_specs=[pl.BlockSpec((hr[a], parts[a].shape[2]), lambda i, c_ref: (2 * c_ref[0] + i, 0))
                       for a in range(na)]),
        out_shape=[jax.ShapeDtypeStruct((2 * t.shape[1], t.shape[2]), F32) for t in parts],
        compiler_params=_cp(("arbitrary",)),
    )(core, *parts)


def _adamw(w, g, m, v, name):
    r, cdim = w.shape
    tb = _row_block(r, 256)
    c1 = 1.0 - ADAM_B1 ** ADAM_STEP
    c2 = 1.0 - ADAM_B2 ** ADAM_STEP

    def body(w_ref, g_ref, m_ref, v_ref, d_ref, nm_ref, nv_ref):
        gv = g_ref[...]
        nm = ADAM_B1 * m_ref[...] + (1.0 - ADAM_B1) * gv
        nv = ADAM_B2 * v_ref[...] + (1.0 - ADAM_B2) * (gv * gv)
        m_hat = nm / c1
        v_hat = nv / c2
        d_ref[...] = -ADAM_LR * (m_hat / (jnp.sqrt(v_hat) + ADAM_EPS) + ADAM_WD * w_ref[...])
        nm_ref[...] = nm
        nv_ref[...] = nv

    spec = _row_spec(tb, cdim)
    shp = jax.ShapeDtypeStruct((r, cdim), F32)
    return pl.pallas_call(
        body, name=name, grid=(r // tb,),
        in_specs=[spec] * 4, out_specs=[spec] * 3, out_shape=[shp] * 3,
        compiler_params=_cp(("parallel",)),
    )(w, g, m, v)


def _mesh_pos():
    return lax.axis_index("x"), lax.axis_index("y"), lax.axis_index("c")


_ANY = pl.BlockSpec(memory_space=pl.ANY)


def _remote(src, dst, send_sem, recv_sem, device):
    return pltpu.make_async_remote_copy(src_ref=src, dst_ref=dst, send_sem=send_sem, recv_sem=recv_sem,
                                        device_id=device, device_id_type=MESH)


def _other_chips(x, y):
    return [(1 - x, y), (x, 1 - y), (1 - x, 1 - y)]


def _half(nrows, h):
    return pl.ds(h * (nrows // 2), nrows // 2)


_GATHER_SEMS = 7


def _gather_steps(ins, outs, send_sems, recv_sems):
    na = len(ins)
    nrow = [r.shape[0] for r in ins]

    def copies():
        x, y, c = _mesh_pos()
        me = 2 * x + y
        sib = (x, y, 1 - c)
        chips = _other_chips(x, y)

        def ici(k, a, slot):
            px, py = chips[k]
            rows = _half(nrow[a], c)
            return _remote(ins[a].at[rows, :], outs[a].at[slot, rows, :], send_sems.at[k * na + a],
                           recv_sems.at[k * na + a], (px, py, c))

        def fwd(k, a, h):
            px, py = chips[k]
            blk = outs[a].at[2 * px + py, _half(nrow[a], h), :]
            return _remote(blk, blk, send_sems.at[(3 + k) * na + a], recv_sems.at[(3 + k) * na + a], sib)

        def own(a):
            return _remote(ins[a], outs[a].at[me], send_sems.at[6 * na + a], recv_sems.at[6 * na + a], sib)

        slots = [2 * px + py for px, py in chips]
        return ici, fwd, own, me, c, slots

    def start():
        ici, _, own, me, _, _ = copies()
        for a in range(na):
            own(a).start()
        for k in range(3):
            for a in range(na):
                ici(k, a, me).start()

    def finish():
        ici, fwd, own, me, c, slots = copies()
        for k in range(3):
            for a in range(na):
                ici(k, a, slots[k]).wait_recv()
                fwd(k, a, c).start()
        for k in range(3):
            for a in range(na):
                fwd(k, a, 1 - c).wait_recv()
        for a in range(na):
            own(a).wait_recv()
        for a in range(na):
            own(a).wait_send()
        for k in range(3):
            for a in range(na):
                ici(k, a, me).wait_send()
                fwd(k, a, c).wait_send()

    return start, finish


def _exchange_steps(ins, outs, send_sems, recv_sems, local_sems):
    na = len(ins)

    def copies():
        x, y, c = _mesh_pos()
        me = 2 * x + y
        chips = _other_chips(x, y)

        def copy(k, a, slot):
            px, py = chips[k]
            return _remote(ins[a].at[2 * px + py], outs[a].at[slot], send_sems.at[k * na + a], recv_sems.at[k * na + a],
                           (px, py, c))

        def local(a):
            return pltpu.make_async_copy(ins[a].at[me], outs[a].at[me], local_sems.at[a])

        return copy, local, me, [2 * px + py for px, py in chips]

    def start():
        copy, local, me, _ = copies()
        for a in range(na):
            local(a).start()
        for k in range(3):
            for a in range(na):
                copy(k, a, me).start()

    def finish():
        copy, local, me, slots = copies()
        for k in range(3):
            for a in range(na):
                copy(k, a, slots[k]).wait_recv()
        for k in range(3):
            for a in range(na):
                copy(k, a, me).wait_send()
        for a in range(na):
            local(a).wait()

    return start, finish


def _gather_weights(shards, conv8):
    na = len(shards)

    def body(*refs):
        ins, conv_in = refs[:na], refs[na]
        outs, conv_out = refs[na + 1:2 * na + 1], refs[2 * na + 1]
        send_sems, recv_sems, local_sems = refs[2 * na + 2:]
        x, y, c = _mesh_pos()
        me = 2 * x + y
        chips = _other_chips(x, y)

        def conv(k, slot):
            px, py = chips[k]
            return _remote(conv_in, conv_out.at[slot], send_sems.at[7 * na + k], recv_sems.at[7 * na + k], (px, py, c))

        start, finish = _gather_steps(ins, outs, send_sems, recv_sems)
        local = pltpu.make_async_copy(conv_in, conv_out.at[me], local_sems.at[0])
        local.start()
        start()
        sends = [conv(k, me) for k in range(3)]
        for cp in sends:
            cp.start()
        finish()
        for k in range(3):
            px, py = chips[k]
            conv(k, 2 * px + py).wait_recv()
        for cp in sends:
            cp.wait_send()
        local.wait()

    nsem = _GATHER_SEMS * na + 3
    return pl.pallas_call(
        body, name="gather_w", in_specs=[_ANY] * (na + 1), out_specs=[_ANY] * (na + 1),
        out_shape=[jax.ShapeDtypeStruct((4,) + t.shape, t.dtype) for t in shards] +
                  [jax.ShapeDtypeStruct((4,) + conv8.shape, conv8.dtype)],
        scratch_shapes=[pltpu.SemaphoreType.DMA((nsem,)), pltpu.SemaphoreType.DMA((nsem,)),
                        pltpu.SemaphoreType.DMA((1,))],
    )(*shards, conv8)


def _pair_swap(slabs, name):
    na = len(slabs)

    def body(*refs):
        ins, outs = refs[:na], refs[na:2 * na]
        send_sems, recv_sems = refs[2 * na:]
        x, y, c = _mesh_pos()
        cps = [_remote(ins[a].at[:, _half(slabs[a].shape[1], 1 - c), :], outs[a], send_sems.at[a], recv_sems.at[a],
                       (x, y, 1 - c)) for a in range(na)]
        for cp in cps:
            cp.start()
        for cp in cps:
            cp.wait()

    return pl.pallas_call(
        body, name=name, in_specs=[_ANY] * na, out_specs=[_ANY] * na,
        out_shape=[jax.ShapeDtypeStruct((4, t.shape[1] // 2, t.shape[2]), t.dtype) for t in slabs],
        scratch_shapes=[pltpu.SemaphoreType.DMA((na,)), pltpu.SemaphoreType.DMA((na,))],
    )(*slabs)


def _chip_exchange(pairs):
    na = len(pairs)

    def body(*refs):
        start, finish = _exchange_steps(refs[:na], refs[na:2 * na], *refs[2 * na:])
        start()
        finish()

    return pl.pallas_call(
        body, name="reduce_g_ici", in_specs=[_ANY] * na, out_specs=[_ANY] * na,
        out_shape=[jax.ShapeDtypeStruct(t.shape, t.dtype) for t in pairs],
        scratch_shapes=[pltpu.SemaphoreType.DMA((3 * na,)), pltpu.SemaphoreType.DMA((3 * na,)),
                        pltpu.SemaphoreType.DMA((na,))],
    )(*pairs)


def _share_halves(totals):
    na = len(totals)

    def body(*refs):
        bufs = refs[na:2 * na]
        send_sems, recv_sems = refs[2 * na:]
        x, y, c = _mesh_pos()

        def copy(a, h):
            blk = bufs[a].at[_half(totals[a].shape[0], h), :]
            return _remote(blk, blk, send_sems.at[a], recv_sems.at[a], (x, y, 1 - c))

        sends = [copy(a, c) for a in range(na)]
        for cp in sends:
            cp.start()
        for a in range(na):
            copy(a, 1 - c).wait_recv()
        for cp in sends:
            cp.wait_send()

    return pl.pallas_call(
        body, name="reduce_g_share", in_specs=[_ANY] * na, out_specs=[_ANY] * na,
        out_shape=[jax.ShapeDtypeStruct(t.shape, t.dtype) for t in totals],
        input_output_aliases={a: a for a in range(na)},
        scratch_shapes=[pltpu.SemaphoreType.DMA((na,)), pltpu.SemaphoreType.DMA((na,))],
    )(*totals)


def _allreduce_small(v, name):
    r, cdim = v.shape

    def body(v_ref, out_ref, buf, send_sems, recv_sems):
        x, y, c = _mesh_pos()
        me = 4 * x + 2 * y + c
        buf[me] = v_ref[...]
        rel = [(bx, by, bc) for bx in (0, 1) for by in (0, 1) for bc in (0, 1)][1:]

        def peer(b):
            bx, by, bc = b
            return ((1 - x) if bx else x, (1 - y) if by else y, (1 - c) if bc else c)

        def copy(k, slot):
            return pltpu.make_async_remote_copy(
                src_ref=v_ref, dst_ref=buf.at[slot], send_sem=send_sems.at[k], recv_sem=recv_sems.at[k],
                device_id=peer(rel[k]), device_id_type=MESH)

        sends = [copy(k, me) for k in range(7)]
        for cp in sends:
            cp.start()
        for k in range(7):
            px, py, pc = peer(rel[k])
            copy(k, 4 * px + 2 * py + pc).wait_recv()
        for cp in sends:
            cp.wait_send()
        acc = buf[0]
        for j in range(1, 8):
            acc = acc + buf[j]
        out_ref[...] = acc

    vm = pl.BlockSpec(memory_space=pltpu.VMEM)
    return pl.pallas_call(
        body, name=name, in_specs=[vm], out_specs=[vm, vm],
        out_shape=[jax.ShapeDtypeStruct((r, cdim), F32), jax.ShapeDtypeStruct((8, r, cdim), F32)],
        scratch_shapes=[pltpu.SemaphoreType.DMA((7,)), pltpu.SemaphoreType.DMA((7,))],
    )(v)


def _chip_bcast(v, name):
    def body(v_ref, out_ref, send_sems, recv_sems):
        x, y, c = _mesh_pos()
        me = 2 * x + y
        chips = _other_chips(x, y)
        out_ref[me] = v_ref[...]

        def copy(k, slot):
            px, py = chips[k]
            return _remote(v_ref, out_ref.at[slot], send_sems.at[k], recv_sems.at[k], (px, py, c))

        sends = [copy(k, me) for k in range(3)]
        for cp in sends:
            cp.start()
        for k, (px, py) in enumerate(chips):
            copy(k, 2 * px + py).wait_recv()
        for cp in sends:
            cp.wait_send()

    vm = pl.BlockSpec(memory_space=pltpu.VMEM)
    return pl.pallas_call(
        body, name=name, in_specs=[vm], out_specs=vm,
        out_shape=jax.ShapeDtypeStruct((4,) + v.shape, F32),
        scratch_shapes=[pltpu.SemaphoreType.DMA((3,)), pltpu.SemaphoreType.DMA((3,))],
    )(v)


_BIG = (("in_proj", (D_MODEL, D_IN_PROJ // 4), 1), ("w_out", (2 * D_MODEL // 4, D_MODEL), 0),
        ("w_gate", (D_MODEL, D_FF // 4), 1), ("w_up", (D_MODEL, D_FF // 4), 1), ("w_down", (D_FF // 4, D_MODEL), 0),
        ("pool_w", (N_POOL * POOL_DIM // 4, POOL_DIM), None))


def _assemble(name, t):
    _, r, c = t.shape
    axis = {n: ax for n, _, ax in _BIG}[name]
    if axis == 0:
        return t.reshape(4 * r, c)
    if axis == 1:
        return t.transpose(1, 0, 2).reshape(r, 4 * c)
    return t.reshape(4, N_POOL, POOL_DIM // 4, POOL_DIM).transpose(1, 0, 2, 3).reshape(N_POOL, POOL_DIM, POOL_DIM)


def _to_slabs(name, g):
    (r, c), axis = {n: (sh, ax) for n, sh, ax in _BIG}[name]
    if axis == 0:
        return g.reshape(4, r, c)
    if axis == 1:
        return g.reshape(r, 4, c).transpose(1, 0, 2)
    return g.reshape(N_POOL, 4, POOL_DIM // 4, POOL_DIM).transpose(1, 0, 2, 3).reshape(4, r, c)


_EARLY = ("in_proj",)
_LATE = tuple(n for n, _, _ in _BIG if n not in _EARLY)


def _reduce_grads(early_grads, late_parts, core):
    slabs = [_to_slabs(n, early_grads[n]) for n in _EARLY]
    pairs = _pair_add(slabs, _pair_swap(slabs, "reduce_g_d2d"), core, "reduce_g_pair")
    parts = dict(zip(_EARLY, _chip_exchange(pairs)), **dict(zip(_LATE, late_parts)))
    names = [n for n, _, _ in _BIG]
    totals = _sum4([parts[n] for n in names], core)
    return dict(zip(names, _share_halves(totals)))


def _pad_cols(w, n):
    return jnp.concatenate([w, jnp.zeros((w.shape[0], n - w.shape[1]), w.dtype)], axis=1)


def _device_step(x, mod, mod_ctx, ctx, target, wts, w8, small, tb, late_shards=None, core=None):
    n = x.shape[0]
    d = D_MODEL

    win = wts["in_proj"]
    wz, wxd, wup = win[:, 0:d], _pad_cols(win[:, d:d + D_XBC + 2 * SSD_HEADS], D_XD), win[:, d + D_XBC + 2 * SSD_HEADS:]

    emb_g, emb_b = _vec(small["emb_ln_g"]), _vec(small["emb_ln_b"])
    ln1_g, ln1_b = _vec(small["ln1_g"]), _vec(small["ln1_b"])
    ln2_g, ln2_b = _vec(small["ln2_g"]), _vec(small["ln2_b"])
    gn = _vec(small["ssd_norm_g"])
    pscale = _vec(small["pool_scale"])
    conv_b = _vec(small["conv_b"])
    dskip_e = jnp.repeat(small["d_skip"].reshape(-1), HEAD_DIM).reshape(1, d)
    zpad = jnp.zeros((2, 1, 128 - SSD_HEADS), F32)
    bias2 = jnp.concatenate([small["dt_bias"].reshape(2, 1, SSD_HEADS), zpad], axis=2)
    a2 = jnp.concatenate([-jnp.exp(small["a_log"].reshape(2, 1, SSD_HEADS)), zpad], axis=2)
    rexp = (jnp.arange(128)[:, None] == (jnp.arange(d)[None, :] // HEAD_DIM)).astype(BF16)
    rexp_t = rexp.T

    sh1, sc1, g1, sh2, sc2, g2 = [mod[:, i * d:(i + 1) * d] for i in range(6)]
    sh1c, sc1c = mod_ctx[:, 0:d], mod_ctx[:, d:2 * d]

    tbc = min(tb, ctx.shape[0])
    xc0, hc = _ln_mod(ctx, emb_g, emb_b, sh1c, sc1c, tbc, "ln_mod_ctx")
    xdc = _matmul_nn([(hc, wxd)], F32, 512, D_XD, "in_proj_ctx")
    uc, ac = _conv_fwd(xdc, w8, conv_b, tbc, "conv_fwd_ctx")
    hzero = jnp.zeros((2, D_STATE, d), F32)
    _, hprev_c, hfin_c = _ssd_fwd(ac, xdc, bias2, a2, rexp, hzero, "ssd_fwd_ctx")

    x0, h1 = _ln_mod(x, emb_g, emb_b, sh1, sc1, tb, "ln_mod")
    z, xd, up = _matmul_nn_multi(h1, [wz, wxd, wup], [BF16, F32, BF16], MM_ROWS, "in_proj")
    u, act = _conv_fwd(xd, w8, conv_b, tb, "conv_fwd")
    y, hprev, _, *landed = _ssd_fwd(act, xd, bias2, a2, rexp, hfin_c, "ssd_fwd", gather=late_shards or ())
    if late_shards is not None:
        wts = dict(wts, **{nme: _assemble(nme, t) for nme, t in zip(_LATE, landed)})
    wout = wts["w_out"]
    wg, wu, wd = wts["w_gate"], wts["w_up"], wts["w_down"]
    pw = wts["pool_w"]
    yn = _merge_fwd(y, u, z, dskip_e, gn, tb)
    pconst = _pool_consts(False)
    pool, dsave = _pool_fwd(up, pconst, pw, pscale)
    mix = _matmul_nn([(yn, wout[0:d]), (pool, wout[d:2 * d])], BF16, MM_ROWS, 1024, "out_proj")
    x1, h2 = _res_ln(x0, mix, g1, ln1_g, ln1_b, sh2, sc2, tb)

    gate, upp, hmid = _swiglu_fwd(h2, wg, wu, MM_ROWS, D_FF // 2)
    ffn = _matmul_nn([(hmid, wd)], BF16, MM_ROWS, 1024, "ffn_down")
    dffn, dr2, acc2 = _final_ln_loss(x1, ffn, g2, ln2_g, ln2_b, target, tb)
    loss = (0.5 / d) * jnp.sum(acc2[3])

    dgate, dupp = _swiglu_bwd(dffn, wd.T, gate, upp, MM_ROWS, D_FF // 2)
    g_wdown = _matmul_tn(hmid, dffn, MM_ROWS, 1024, "g_w_down")
    g_wgate, g_wup = _matmul_tn_multi(h2, [dgate, dupp], MM_ROWS, [D_FF // 2, D_FF // 2], "g_w_gate_up")
    dh2 = _matmul_nn([(dgate, wg.T), (dupp, wu.T)], BF16, 512, 1024, "d_h2")
    dmix, dr1, acc1 = _bwd_ln1(dr2, dh2, x1, x0, mix, g1, sc2, ln1_g, tb)

    dyn, dpool = _matmul_nn_multi(dmix, [wout[0:d].T, wout[d:2 * d].T], [BF16, BF16], MM_ROWS, "d_yn_pool")
    g_wout = jnp.concatenate([_matmul_tn(yn, dmix, MM_ROWS, 1024, "g_w_out_a"),
                              _matmul_tn(pool, dmix, MM_ROWS, 1024, "g_w_out_b")], axis=0)
    dd, dds, g_pw, accp = _pool_bwd_a(dpool, dsave, pw, jnp.swapaxes(pw, 1, 2), pscale)
    dup = _pool_bwd_b(dds, dd, _pool_consts(True))
    dy, dz, accm = _merge_bwd(dyn, y, u, z, dskip_e, gn, tb)
    lam0 = jnp.zeros((2, D_STATE, d), F32)
    late_grads = dict(w_out=g_wout, w_gate=g_wgate, w_up=g_wup, w_down=g_wdown, pool_w=g_pw)
    pairs = ()
    if late_shards is not None:
        slabs = [_to_slabs(nme, late_grads[nme]) for nme in _LATE]
        pairs = _pair_add(slabs, _pair_swap(slabs, "reduce_g_d2d_late"), core, "reduce_g_pair_late")
    dxs, dbc, ddt, accs, lam_c, *arrived = _ssd_bwd(act, xd, bias2, a2, rexp, rexp_t, dy, hprev, lam0, "ssd_bwd",
                                                    exchange=pairs)
    du, accb = _conv_bwd_a(dxs, dy, dskip_e, dbc, u, tb, "conv_bwd_a")
    dxd, accw = _conv_bwd_b(du, xd, ddt, w8, tb, "conv_bwd_b")

    lc = ctx.shape[0]
    zeros_c = jnp.zeros((lc, d), BF16)
    dxs_c, dbc_c, ddt_c, accs_c, _ = _ssd_bwd(ac, xdc, bias2, a2, rexp, rexp_t, zeros_c, hprev_c, lam_c, "ssd_bwd_ctx")
    du_c, accb_c = _conv_bwd_a(dxs_c, zeros_c, dskip_e, dbc_c, uc, tbc, "conv_bwd_a_ctx")
    dxd_c, accw_c = _conv_bwd_b(du_c, xdc, ddt_c, w8, tbc, "conv_bwd_b_ctx")
    dhc = _matmul_nn([(dxd_c, wxd.T)], F32, 512, 1024, "d_hc")
    _, acc0c = _bwd_ln0(None, dhc, ctx, emb_g, emb_b, sc1c, tbc, "bwd_ln0_ctx")

    dh1 = _matmul_nn([(dz, wz.T), (dxd, wxd.T), (dup, wup.T)], BF16, MM_ROWS, 1024, "d_h1")
    g_wz, g_wxd, g_wpo = _matmul_tn_multi(h1, [dz, dxd, dup], MM_ROWS, [1024, D_XD, 1024], "g_in_proj")
    g_wxd = g_wxd + _matmul_tn(hc, dxd_c, 512, D_XD, "g_in_proj_xd_ctx")
    g_win = jnp.concatenate([g_wz, g_wxd[:, 0:D_XBC + 2 * SSD_HEADS], g_wpo], axis=1)
    grad_x, acc0 = _bwd_ln0(dr1, dh1, x, emb_g, emb_b, sc1, tb, "bwd_ln0")

    zero_d = jnp.zeros((1, d), F32)
    dmod = jnp.concatenate([acc0[1:2], acc0[0:1], acc1[4:5], acc1[1:2], acc1[0:1], acc2[2:3]], axis=1)
    dmodc = jnp.concatenate([acc0c[1:2], acc0c[0:1]] + [zero_d] * 4, axis=1)

    big = dict(in_proj=g_win)
    if late_shards is None:
        big.update(late_grads)
    sml = dict(
        dmod=dmod, dmod_ctx=dmodc, emb_ln_g=acc0[2] + acc0c[2], emb_ln_b=acc0[3] + acc0c[3],
        conv_w=accw[0:D_CONV] + accw_c[0:D_CONV], conv_b=accb[0] + accb_c[0],
        dt_bias=accs[:, 0, 0:SSD_HEADS] + accs_c[:, 0, 0:SSD_HEADS],
        a_log=accs[:, 1, 0:SSD_HEADS] + accs_c[:, 1, 0:SSD_HEADS],
        d_skip=jnp.sum(accm[1].reshape(SSD_HEADS, HEAD_DIM), axis=1),
        ssd_norm_g=accm[0], pool_scale=accp[0], ln1_g=acc1[2], ln1_b=acc1[3], ln2_g=acc2[0], ln2_b=acc2[1])
    return loss, grad_x, big, sml, (arrived if late_shards is not None else None)


_SMALL = ("c_ctx", "emb_ln_g", "emb_ln_b", "b_ada", "conv_w", "conv_b", "dt_bias", "a_log", "d_skip",
          "ssd_norm_g", "pool_scale", "ln1_g", "ln1_b", "ln2_g", "ln2_b")


def _small_rows(size):
    return -(-size // 1024)


def _pack_small(vals, names):
    pieces, rows = [], 0
    for nme in names:
        flat = vals[nme].reshape(-1).astype(F32)
        nr = _small_rows(flat.shape[0])
        pieces.append(flat)
        if nr * 1024 > flat.shape[0]:
            pieces.append(jnp.zeros((nr * 1024 - flat.shape[0],), F32))
        rows += nr
    if rows % 8:
        pieces.append(jnp.zeros(((8 - rows % 8) * 1024,), F32))
    return jnp.concatenate(pieces).reshape(-1, 1024)


def _small_offsets(shapes, names):
    out, off = {}, 0
    for nme in names:
        nr = _small_rows(math.prod(shapes[nme]))
        out[nme] = (off, nr)
        off += nr
    return out


def _unpack_small(packed, shapes, names):
    out = {}
    for nme, (off, nr) in _small_offsets(shapes, names).items():
        out[nme] = packed[off:off + nr].reshape(-1)[:math.prod(shapes[nme])].reshape(shapes[nme])
    return out


_WEIGHT_ORDER = ("c_ctx", "emb_ln_g", "emb_ln_b", "w_ada", "b_ada", "in_proj", "conv_w", "conv_b", "dt_bias", "a_log",
                 "d_skip", "ssd_norm_g", "pool_w", "pool_scale", "w_out", "ln1_g", "ln1_b", "w_gate", "w_up", "w_down",
                 "ln2_g", "ln2_b")


def _as2d(a):
    return a.reshape(-1, a.shape[-1])


def kernel(x, c, ctx, c_ctx, emb_ln_g, emb_ln_b, w_ada, b_ada, in_proj, conv_w, conv_b, dt_bias, a_log, d_skip, ssd_norm_g, pool_w, pool_scale, w_out, ln1_g, ln1_b, w_gate, w_up, w_down, ln2_g, ln2_b, loss_target, m_c_ctx, m_emb_ln_g, m_emb_ln_b, m_w_ada, m_b_ada, m_in_proj, m_conv_w, m_conv_b, m_dt_bias, m_a_log, m_d_skip, m_ssd_norm_g, m_pool_w, m_pool_scale, m_w_out, m_ln1_g, m_ln1_b, m_w_gate, m_w_up, m_w_down, m_ln2_g, m_ln2_b, v_c_ctx, v_emb_ln_g, v_emb_ln_b, v_w_ada, v_b_ada, v_in_proj, v_conv_w, v_conv_b, v_dt_bias, v_a_log, v_d_skip, v_ssd_norm_g, v_pool_w, v_pool_scale, v_w_out, v_ln1_g, v_ln1_b, v_w_gate, v_w_up, v_w_down, v_ln2_g, v_ln2_b):
    w = dict(c_ctx=c_ctx, emb_ln_g=emb_ln_g, emb_ln_b=emb_ln_b, w_ada=w_ada, b_ada=b_ada, in_proj=in_proj, conv_w=conv_w,
             conv_b=conv_b, dt_bias=dt_bias, a_log=a_log, d_skip=d_skip, ssd_norm_g=ssd_norm_g, pool_w=pool_w,
             pool_scale=pool_scale, w_out=w_out, ln1_g=ln1_g, ln1_b=ln1_b, w_gate=w_gate, w_up=w_up, w_down=w_down,
             ln2_g=ln2_g, ln2_b=ln2_b)
    m = dict(c_ctx=m_c_ctx, emb_ln_g=m_emb_ln_g, emb_ln_b=m_emb_ln_b, w_ada=m_w_ada, b_ada=m_b_ada, in_proj=m_in_proj,
             conv_w=m_conv_w, conv_b=m_conv_b, dt_bias=m_dt_bias, a_log=m_a_log, d_skip=m_d_skip,
             ssd_norm_g=m_ssd_norm_g, pool_w=m_pool_w, pool_scale=m_pool_scale, w_out=m_w_out, ln1_g=m_ln1_g,
             ln1_b=m_ln1_b, w_gate=m_w_gate, w_up=m_w_up, w_down=m_w_down, ln2_g=m_ln2_g, ln2_b=m_ln2_b)
    v = dict(c_ctx=v_c_ctx, emb_ln_g=v_emb_ln_g, emb_ln_b=v_emb_ln_b, w_ada=v_w_ada, b_ada=v_b_ada, in_proj=v_in_proj,
             conv_w=v_conv_w, conv_b=v_conv_b, dt_bias=v_dt_bias, a_log=v_a_log, d_skip=v_d_skip,
             ssd_norm_g=v_ssd_norm_g, pool_w=v_pool_w, pool_scale=v_pool_scale, w_out=v_w_out, ln1_g=v_ln1_g,
             ln1_b=v_ln1_b, w_gate=v_w_gate, w_up=v_w_up, w_down=v_w_down, ln2_g=v_ln2_g, ln2_b=v_ln2_b)

    xi, yi, ci = _mesh_pos()
    chip = 2 * xi + yi

    dev = 4 * xi + 2 * yi + ci
    d = D_MODEL
    core = ci.reshape(1).astype(jnp.int32)

    crow = jnp.concatenate([c.reshape(1, d), jnp.zeros((7, d), F32)], axis=0)
    _, c_all = _allreduce_small(crow, "gather_c")
    c16 = jnp.concatenate([c_all[:, 0, :], c_ctx.reshape(1, d), jnp.zeros((MOD_ROWS - 9, d), F32)], axis=0)
    ncol = w_ada.shape[-1]
    wada_bf = w_ada[0].astype(BF16)
    b_mine = lax.dynamic_slice_in_dim(b_ada, chip * ncol, ncol, axis=1)
    mods4 = _chip_bcast(_mods_fwd(c16, wada_bf, b_mine), "gather_mods")
    mods = mods4.transpose(1, 0, 2).reshape(MOD_ROWS, 4 * ncol)
    mod = lax.dynamic_slice_in_dim(mods, dev, 1, axis=0)
    mod_ctx = mods[8:9]

    shard = {name: w[name][0].astype(BF16).reshape(shp) for name, shp, _ in _BIG}
    conv8 = jnp.concatenate([conv_w[0], jnp.zeros((8 - D_CONV, conv_w.shape[-1]), F32)], axis=0)
    *gathered, conv4 = _gather_weights([shard[nme] for nme in _EARLY], conv8)
    wts = {nme: _assemble(nme, t) for nme, t in zip(_EARLY, gathered)}
    w8 = conv4.transpose(1, 0, 2).reshape(8, D_XBC)
    small = {nme: (w[nme] if nme in ("c_ctx", "emb_ln_g", "emb_ln_b") else w[nme][0]) for nme in _SMALL if nme != "conv_w"}

    loss, grad_x, big, sml, late_parts = _device_step(x[0], mod, mod_ctx, ctx[0], loss_target[0], wts, w8, small, 512,
                                                      late_shards=[shard[nme] for nme in _LATE], core=core)
    loss = lax.psum(loss, ("x", "y", "c"))

    g_big = _reduce_grads(big, late_parts, core)
    reduced = tuple(sml)
    small_shapes = {nme: sml[nme].shape for nme in reduced}
    total, each = _allreduce_small(_pack_small(sml, reduced), "reduce_small")
    g_small = _unpack_small(total, small_shapes, reduced)
    cw_cols = conv_w.shape[-1]
    g_small["conv_w"] = lax.dynamic_slice_in_dim(g_small["conv_w"], chip * cw_cols, cw_cols, axis=1)

    off, nr = _small_offsets(small_shapes, reduced)["dmod"]
    dm16 = jnp.concatenate([each[:, off:off + nr, :].reshape(8, nr * 1024)[:, :6 * d], g_small["dmod_ctx"],
                            jnp.zeros((MOD_ROWS - 9, 6 * d), F32)], axis=0)
    dm_mine = lax.dynamic_slice_in_dim(dm16, chip * ncol, ncol, axis=1)
    g_wada = _mods_bwd_w(c16.T, dm_mine)
    g_small["b_ada"] = _mods_bwd_b(dm16)[0:1]
    c_part = _mods_bwd_c(dm_mine, wada_bf, c16)[8:16]
    g_small["c_ctx"] = _allreduce_small(c_part, "reduce_c_ctx")[0][0]

    grads, delta, new_m, new_v = {}, {}, {}, {}
    grads["w_ada"] = g_wada[None]
    delta["w_ada"], new_m["w_ada"], new_v["w_ada"] = (
        t[None] for t in _adamw(w_ada[0], g_wada, m_w_ada[0], v_w_ada[0], "adamw_w_ada"))
    for name, _, _ in _BIG:
        g2 = _as2d(g_big[name])
        d2, m2, v2 = _adamw(_as2d(w[name][0]), g2, _as2d(m[name][0]), _as2d(v[name][0]), "adamw_" + name)
        grads[name] = g2.reshape(w[name].shape)
        delta[name], new_m[name], new_v[name] = (t.reshape(w[name].shape) for t in (d2, m2, v2))
    shp = {nme: w[nme].shape for nme in _SMALL}
    gp = _pack_small(g_small, _SMALL)
    dp, mp, vp = _adamw(_pack_small(w, _SMALL), gp, _pack_small(m, _SMALL), _pack_small(v, _SMALL), "adamw_small")
    for dst, src in ((grads, gp), (delta, dp), (new_m, mp), (new_v, vp)):
        dst.update(_unpack_small(src, shp, _SMALL))

    return (loss, grad_x[None], *[grads[nme] for nme in _WEIGHT_ORDER], *[delta[nme] for nme in _WEIGHT_ORDER],
            *[new_m[nme] for nme in _WEIGHT_ORDER], *[new_v[nme] for nme in _WEIGHT_ORDER])
```
